```python
import jax, jax.numpy as jnp
from jax import lax
import numpy as np

D_MODEL = 1024
BATCH = 8
SEQ = 2048
DEPTH = 1
DEC_BATCH = 128
DEC_SEQ = 4
PAST_LEN = 16384
PAGE_SIZE = 128

D_MIX = D_MODEL
D_POOL = D_MIX // 2
D_CONV = D_MIX - D_POOL
POOL_WINDOWS = (2, 4, 8, 16)
N_POOL_GROUPS = len(POOL_WINDOWS)
POOL_GROUP = D_POOL // N_POOL_GROUPS
POOL_STATE = max(POOL_WINDOWS) - 1
N_CONV_HEADS = 8
CONV_W = 3
CONV_STATE = CONV_W - 1
D_IN = D_POOL + 3 * D_CONV
N_EXPERTS = 32
TOP_K = 4
D_FF = D_MODEL
SWIGLU_LIMIT = 7.0
SWIGLU_ALPHA = 1.702
MOE_BLOCK = 128
EPS = 1e-5

kernel_name = "hybrid_pool_shortconv_moe_adaln_step"


def rmsnorm(x, g):
    xf = x.astype(jnp.float32)
    y = xf * lax.rsqrt(jnp.mean(xf * xf, axis=-1, keepdims=True) + EPS)
    return (y * g.astype(jnp.float32)).astype(x.dtype)


def pool_mixer(u, prefix, pos0, w_pool, pool_scale):
    b, t, _ = u.shape
    ext_raw = jnp.concatenate([prefix, u], axis=1)
    ext = ext_raw.astype(jnp.float32)
    cs = jnp.concatenate([jnp.zeros((b, 1, D_POOL), jnp.float32),
                          jnp.cumsum(ext, axis=1)], axis=1)
    pos = pos0 + jnp.arange(t)
    end = cs[:, POOL_STATE + 1:]
    means = []
    for g, w in enumerate(POOL_WINDOWS):
        sl = slice(g * POOL_GROUP, (g + 1) * POOL_GROUP)
        start = cs[:, POOL_STATE + 1 - w: POOL_STATE + 1 - w + t, sl]
        cnt = jnp.minimum(pos + 1, w).astype(jnp.float32)[None, :, None]
        means.append((end[..., sl] - start) / cnt)
    pooled = jnp.concatenate(means, axis=-1) - ext[:, POOL_STATE:]
    pooled = pooled.astype(u.dtype).reshape(b, t, N_POOL_GROUPS, POOL_GROUP)
    mixed = jnp.einsum("btgc,gcd->btgd", pooled, w_pool).reshape(b, t, D_POOL)
    return mixed * pool_scale, ext_raw[:, -POOL_STATE:]


def conv_mixer(gate_b, gate_c, val, prefix, w_conv):
    t = val.shape[1]
    v = gate_c * val
    ext = jnp.concatenate([prefix, v], axis=1)
    y = w_conv[0] * ext[:, 0:t] + w_conv[1] * ext[:, 1:t + 1] + w_conv[2] * ext[:, 2:t + 2]
    return gate_b * y, ext[:, -CONV_STATE:]


def moe(h, w_router, b_router, w_gate, b_gate, w_up, b_up, w_down, b_down):
    n = h.shape[0]
    logits = h.astype(jnp.float32) @ w_router.astype(jnp.float32) + b_router.astype(jnp.float32)
    top_v, top_i = lax.top_k(logits, TOP_K)
    gates = jax.nn.softmax(top_v, axis=-1).astype(h.dtype)
    nk = n * TOP_K
    e_flat = top_i.reshape(-1)
    tok_flat = jnp.repeat(jnp.arange(n), TOP_K)
    g_flat = gates.reshape(-1)
    order = jnp.argsort(e_flat, stable=True)
    e_s, tok_s, g_s = e_flat[order], tok_flat[order], g_flat[order]
    counts = jnp.zeros((N_EXPERTS,), jnp.int32).at[e_flat].add(1)
    padded = (counts + MOE_BLOCK - 1) // MOE_BLOCK * MOE_BLOCK
    start = jnp.cumsum(counts) - counts
    pad_end = jnp.cumsum(padded)
    pad_start = pad_end - padded
    dest = pad_start[e_s] + (jnp.arange(nk) - start[e_s])
    n_blocks = -(-nk // MOE_BLOCK) + N_EXPERTS
    rows = jnp.zeros((n_blocks * MOE_BLOCK, h.shape[1]), h.dtype).at[dest].set(h[tok_s])
    block_start = jnp.arange(n_blocks) * MOE_BLOCK
    block_expert = jnp.minimum(jnp.sum(block_start[:, None] >= pad_end[None, :], axis=1),
                               N_EXPERTS - 1).astype(jnp.int32)

    def expert_block(args):
        xb, e = args
        g = xb @ w_gate[e] + b_gate[e]
        u = xb @ w_up[e] + b_up[e]
        g = jnp.minimum(g, SWIGLU_LIMIT)
        u = jnp.clip(u, -SWIGLU_LIMIT, SWIGLU_LIMIT)
        glu = g * jax.nn.sigmoid(SWIGLU_ALPHA * g)
        return ((u + 1.0) * glu) @ w_down[e] + b_down[e]

    y_rows = lax.map(expert_block, (rows.reshape(n_blocks, MOE_BLOCK, -1), block_expert))
    y_rows = y_rows.reshape(n_blocks * MOE_BLOCK, -1)
    contrib = y_rows[dest] * g_s[:, None]
    return jnp.zeros_like(h).at[tok_s].add(contrib)


def layer(x, c, pool_prefix, conv_prefix, pos0, norm1, norm2, w_ada, b_ada, w_in, w_pool,
          pool_scale, w_conv, w_out, w_router, b_router, w_gate, b_gate, w_up, b_up,
          w_down, b_down):
    b, t, d = x.shape
    mod = jax.nn.silu(c) @ w_ada + b_ada
    sh1, sc1, g1, sh2, sc2, g2 = jnp.split(mod[:, None, :], 6, axis=-1)
    h = rmsnorm(x, norm1) * (1.0 + sc1) + sh1
    z = h @ w_in
    u, gate_b, gate_c, val = jnp.split(z, [D_POOL, D_POOL + D_CONV, D_POOL + 2 * D_CONV], axis=-1)
    pool_out, pool_state = pool_mixer(u, pool_prefix, pos0, w_pool, pool_scale)
    conv_out, conv_state = conv_mixer(gate_b, gate_c, val, conv_prefix, w_conv)
    mix = jnp.concatenate([pool_out, conv_out], axis=-1) @ w_out
    x = x + g1 * mix
    h2 = rmsnorm(x, norm2) * (1.0 + sc2) + sh2
    ffn = moe(h2.reshape(b * t, d), w_router, b_router, w_gate, b_gate, w_up, b_up,
              w_down, b_down).reshape(b, t, d)
    x = x + g2 * ffn
    return x, pool_state, conv_state


def setup_inputs(seed: int = 0) -> dict:
    key = jax.random.key(seed)
    ks = jax.random.split(key, 26)
    f32 = jnp.float32
    nrm = lambda k, shape, s: (jax.random.normal(k, shape, f32) * s)
    L = DEPTH
    return {
        "x_prompt": nrm(ks[0], (BATCH, SEQ, D_MODEL), 1.0),
        "x_sample": nrm(ks[1], (DEC_BATCH, DEC_SEQ, D_MODEL), 1.0),
        "state_pool": nrm(ks[2], (L, DEC_BATCH, POOL_STATE, D_POOL), 1.0),
        "state_conv": nrm(ks[3], (L, DEC_BATCH, CONV_STATE, D_CONV), 1.0),
        "c_prompt": nrm(ks[4], (BATCH, D_MODEL), 1.0),
        "c_sample": nrm(ks[5], (DEC_BATCH, D_MODEL), 1.0),
        "norm1": 1.0 + nrm(ks[6], (L, D_MODEL), 0.02),
        "norm2": 1.0 + nrm(ks[7], (L, D_MODEL), 0.02),
        "w_ada": nrm(ks[8], (L, D_MODEL, 6 * D_MODEL), 0.5 * D_MODEL ** -0.5),
        "b_ada": nrm(ks[9], (L, 6 * D_MODEL), 0.02),
        "w_in": nrm(ks[10], (L, D_MODEL, D_IN), D_MODEL ** -0.5),
        "w_pool": nrm(ks[11], (L, N_POOL_GROUPS, POOL_GROUP, POOL_GROUP), POOL_GROUP ** -0.5),
        "pool_scale": 1.0 + nrm(ks[12], (L, D_POOL), 0.1),
        "w_conv": nrm(ks[13], (L, CONV_W, D_CONV), CONV_W ** -0.5),
        "w_out": nrm(ks[14], (L, D_MIX, D_MODEL), D_MIX ** -0.5),
        "w_router": nrm(ks[15], (L, D_MODEL, N_EXPERTS), D_MODEL ** -0.5),
        "b_router": nrm(ks[16], (L, N_EXPERTS), 0.01),
        "w_gate": nrm(ks[17], (L, N_EXPERTS, D_MODEL, D_FF), D_MODEL ** -0.5),
        "b_gate": nrm(ks[18], (L, N_EXPERTS, D_FF), 0.02),
        "w_up": nrm(ks[19], (L, N_EXPERTS, D_MODEL, D_FF), D_MODEL ** -0.5),
        "b_up": nrm(ks[20], (L, N_EXPERTS, D_FF), 0.02),
        "w_down": nrm(ks[21], (L, N_EXPERTS, D_FF, D_MODEL), D_FF ** -0.5),
        "b_down": nrm(ks[22], (L, N_EXPERTS, D_MODEL), 0.02),
        "final_norm": 1.0 + nrm(ks[23], (D_MODEL,), 0.02),
    }


def reference(x_prompt, x_sample, state_pool, state_conv, c_prompt, c_sample, norm1, norm2,
              w_ada, b_ada, w_in, w_pool, pool_scale, w_conv, w_out, w_router, b_router,
              w_gate, b_gate, w_up, b_up, w_down, b_down, final_norm):
    yp, ys = x_prompt, x_sample
    pool_p, conv_p, pool_s, conv_s = [], [], [], []
    zero_pool = jnp.zeros((x_prompt.shape[0], POOL_STATE, D_POOL), x_prompt.dtype)
    zero_conv = jnp.zeros((x_prompt.shape[0], CONV_STATE, D_CONV), x_prompt.dtype)
    for l in range(DEPTH):
        params = (norm1[l], norm2[l], w_ada[l], b_ada[l], w_in[l], w_pool[l], pool_scale[l],
                  w_conv[l], w_out[l], w_router[l], b_router[l], w_gate[l], b_gate[l],
                  w_up[l], b_up[l], w_down[l], b_down[l])
        yp, sp, scp = layer(yp, c_prompt, zero_pool, zero_conv, 0, *params)
        ys, ss, scs = layer(ys, c_sample, state_pool[l], state_conv[l], PAST_LEN, *params)
        pool_p.append(sp)
        conv_p.append(scp)
        pool_s.append(ss)
        conv_s.append(scs)
    y_prompt = rmsnorm(yp, final_norm)
    y_sample = rmsnorm(ys, final_norm)
    return (y_prompt, y_sample, jnp.stack(pool_p), jnp.stack(conv_p), jnp.stack(pool_s), jnp.stack(conv_s))
```

```python
import functools

import jax
import jax.numpy as jnp
from jax import lax
from jax.experimental import pallas as pl
from jax.experimental.pallas import tpu as pltpu

F32 = jnp.float32
BF16 = jnp.bfloat16
I32 = jnp.int32

POOL_WINDOWS = (2, 4, 8, 16)
POOL_HALO = 16
CONV_TAPS = 3
CONV_HALO = 8
N_EXPERTS = 32
TOP_K = 4
SWIGLU_LIMIT = 7.0
SWIGLU_ALPHA = 1.702
EPS = 1e-5
PAST_LEN = 16384

LANES = 128
NEG_BIG = -1e30

TOKEN_TILE = 512
ROUTE_TILE = 512
MOVE_TILE = 256
EXPERT_TILE = 512
VMEM_LIMIT = 56 * 1024 * 1024


def _rmsnorm(x, g):
    ms = jnp.mean(x * x, axis=-1, keepdims=True)
    return x * lax.rsqrt(ms + EPS) * g


def _dot(a, b):
    return jnp.dot(a, b, preferred_element_type=F32)


def _adaln_kernel(c_ref, w_ref, b_ref, o_ref):
    c = c_ref[...]
    s = c * jax.nn.sigmoid(c)
    o_ref[...] = _dot(s.astype(BF16), w_ref[...].astype(BF16)) + b_ref[...]


def _adaln(c, w_ada, b_ada):
    rows, d = c.shape
    n = w_ada.shape[1]
    tn = 1024
    return pl.pallas_call(
        _adaln_kernel,
        out_shape=jax.ShapeDtypeStruct((rows, n), F32),
        grid=(n // tn,),
        in_specs=[
            pl.BlockSpec((rows, d), lambda j: (0, 0)),
            pl.BlockSpec((d, tn), lambda j: (0, j)),
            pl.BlockSpec((1, tn), lambda j: (0, j)),
        ],
        out_specs=pl.BlockSpec((rows, tn), lambda j: (0, j)),
        compiler_params=pltpu.CompilerParams(
            dimension_semantics=("arbitrary",), vmem_limit_bytes=VMEM_LIMIT),
        name="adaln",
    )(c, w_ada, b_ada.reshape(1, n))


def _mix_tail(x, pool_in, conv_out, g1, sc2, sh2, n2, wpool_ref, pscale, wout, wr, br):
    gw = pool_in.shape[1] // len(POOL_WINDOWS)
    mixed = [_dot(pool_in[:, g * gw:(g + 1) * gw].astype(BF16), wpool_ref[g])
             for g in range(len(POOL_WINDOWS))]
    pool_out = jnp.concatenate(mixed, axis=-1) * pscale
    mix_in = jnp.concatenate([pool_out, conv_out], axis=-1).astype(BF16)
    x1 = x + g1 * _dot(mix_in, wout)
    h2 = _rmsnorm(x1, n2) * (1.0 + sc2) + sh2
    logits = _dot(h2.astype(BF16), wr) + br
    return x1, h2, logits


def _mixer_prompt_kernel(x_ref, mod_ref, n1_ref, n2_ref, win_ref, wpool_ref, pscale_ref, wconv_ref,
                         wout_ref, wr_ref, br_ref,
                         x1_ref, h2_ref, lg_ref, upool_ref, vconv_ref, ubuf, vbuf):
    tt = x_ref.shape[0]
    dp = ubuf.shape[1]
    gw = dp // len(POOL_WINDOWS)
    t = pl.program_id(1)

    @pl.when(t == 0)
    def _():
        ubuf[0:POOL_HALO, :] = jnp.zeros((POOL_HALO, dp), F32)
        vbuf[0:CONV_HALO, :] = jnp.zeros((CONV_HALO, dp), F32)

    x = x_ref[...]
    mod = mod_ref[...]
    sh1, sc1, g1, sh2, sc2, _ = [mod[i:i + 1, :] for i in range(6)]
    h = _rmsnorm(x, n1_ref[...]) * (1.0 + sc1) + sh1
    z = _dot(h.astype(BF16), win_ref[...])
    u, gate_b, gate_c, val = [z[:, i * dp:(i + 1) * dp] for i in range(4)]

    ubuf[POOL_HALO:POOL_HALO + tt, :] = u
    pos = lax.broadcasted_iota(I32, (tt, gw), 0) + t * tt
    pooled = []
    for g, w in enumerate(POOL_WINDOWS):
        cols = slice(g * gw, (g + 1) * gw)
        acc = u[:, cols]
        for j in range(1, w):
            acc = acc + ubuf[POOL_HALO - j:POOL_HALO - j + tt, cols]
        cnt = jnp.minimum(pos + 1, w).astype(F32)
        pooled.append(acc / cnt - u[:, cols])
    pool_in = jnp.concatenate(pooled, axis=-1)

    v = gate_c * val
    vbuf[CONV_HALO:CONV_HALO + tt, :] = v
    wc = wconv_ref[...]
    y = (wc[0:1, :] * vbuf[CONV_HALO - 2:CONV_HALO - 2 + tt, :]
         + wc[1:2, :] * vbuf[CONV_HALO - 1:CONV_HALO - 1 + tt, :]
         + wc[2:3, :] * v)
    conv_out = gate_b * y

    x1, h2, logits = _mix_tail(x, pool_in, conv_out, g1, sc2, sh2, n2_ref[...], wpool_ref,
                               pscale_ref[...], wout_ref[...], wr_ref[...], br_ref[...])
    x1_ref[...] = x1
    h2_ref[...] = h2
    lg_ref[...] = logits

    ubuf[0:POOL_HALO, :] = ubuf[tt:tt + POOL_HALO, :]
    vbuf[0:CONV_HALO, :] = vbuf[tt:tt + CONV_HALO, :]

    @pl.when(t == pl.num_programs(1) - 1)
    def _():
        upool_ref[...] = ubuf[0:POOL_HALO, :]
        vconv_ref[...] = vbuf[0:CONV_HALO, :]


def _mixer_prompt(x, mod_p, n1, n2, w_in, w_pool, pscale, w_conv, w_out, w_r, b_r):
    b, t, d = x.shape
    dp = w_pool.shape[0] * w_pool.shape[1]
    tt = min(TOKEN_TILE, t)
    const2 = lambda i, j: (0, 0)
    const3 = lambda i, j: (0, 0, 0)
    return pl.pallas_call(
        _mixer_prompt_kernel,
        out_shape=(
            jax.ShapeDtypeStruct((b * t, d), F32),
            jax.ShapeDtypeStruct((b * t, d), F32),
            jax.ShapeDtypeStruct((b * t, LANES), F32),
            jax.ShapeDtypeStruct((b, POOL_HALO, dp), F32),
            jax.ShapeDtypeStruct((b, CONV_HALO, dp), F32),
        ),
        grid=(b, t // tt),
        in_specs=[
            pl.BlockSpec((None, tt, d), lambda i, j: (i, j, 0)),
            pl.BlockSpec((None, 6, d), lambda i, j: (i, 0, 0)),
            pl.BlockSpec((1, d), const2),
            pl.BlockSpec((1, d), const2),
            pl.BlockSpec(w_in.shape, const2),
            pl.BlockSpec(w_pool.shape, const3),
            pl.BlockSpec((1, dp), const2),
            pl.BlockSpec(w_conv.shape, const2),
            pl.BlockSpec(w_out.shape, const2),
            pl.BlockSpec(w_r.shape, const2),
            pl.BlockSpec((1, LANES), const2),
        ],
        out_specs=(
            pl.BlockSpec((tt, d), lambda i, j: (i * (t // tt) + j, 0)),
            pl.BlockSpec((tt, d), lambda i, j: (i * (t // tt) + j, 0)),
            pl.BlockSpec((tt, LANES), lambda i, j: (i * (t // tt) + j, 0)),
            pl.BlockSpec((None, POOL_HALO, dp), lambda i, j: (i, 0, 0)),
            pl.BlockSpec((None, CONV_HALO, dp), lambda i, j: (i, 0, 0)),
        ),
        scratch_shapes=[
            pltpu.VMEM((POOL_HALO + tt, dp), F32),
            pltpu.VMEM((CONV_HALO + tt, dp), F32),
        ],
        compiler_params=pltpu.CompilerParams(
            dimension_semantics=("arbitrary", "arbitrary"), vmem_limit_bytes=VMEM_LIMIT),
        name="mixer_prompt",
    )(x, mod_p, n1, n2, w_in, w_pool, pscale, w_conv, w_out, w_r, b_r)


def _mixer_sample_kernel(x_ref, mod_ref, pstate_ref, cstate_ref, n1_ref, n2_ref, win_ref, wpool_ref,
                         pscale_ref, wconv_ref, wout_ref, wr_ref, br_ref,
                         x1_ref, h2_ref, lg_ref, newp_ref, newc_ref, *, steps):
    nb = mod_ref.shape[0]
    d = x_ref.shape[1]
    dp = pstate_ref.shape[2]
    gw = dp // len(POOL_WINDOWS)
    n_hist = pstate_ref.shape[0]
    n_chist = cstate_ref.shape[0]

    x = x_ref[...]
    mod = mod_ref[...]
    rep = lambda a: jnp.concatenate([a] * steps, axis=0)
    sh1, sc1, g1, sh2, sc2, _ = [rep(mod[:, i * d:(i + 1) * d]) for i in range(6)]
    h = _rmsnorm(x, n1_ref[...]) * (1.0 + sc1) + sh1
    z = _dot(h.astype(BF16), win_ref[...])
    u, gate_b, gate_c, val = [z[:, i * dp:(i + 1) * dp] for i in range(4)]

    ext = [pstate_ref[i] for i in range(n_hist)] + [u[s * nb:(s + 1) * nb, :] for s in range(steps)]
    pooled_steps = []
    for s in range(steps):
        groups = []
        for g, w in enumerate(POOL_WINDOWS):
            cols = slice(g * gw, (g + 1) * gw)
            acc = ext[n_hist + s][:, cols]
            for j in range(1, w):
                acc = acc + ext[n_hist + s - j][:, cols]
            cnt = float(min(PAST_LEN + s + 1, w))
            groups.append(acc / cnt - ext[n_hist + s][:, cols])
        pooled_steps.append(jnp.concatenate(groups, axis=-1))
    pool_in = jnp.concatenate(pooled_steps, axis=0)

    v = gate_c * val
    vext = [cstate_ref[i] for i in range(n_chist)] + [v[s * nb:(s + 1) * nb, :] for s in range(steps)]
    wc = wconv_ref[...]
    y = jnp.concatenate(
        [wc[0:1, :] * vext[s] + wc[1:2, :] * vext[s + 1] + wc[2:3, :] * vext[s + 2] for s in range(steps)],
        axis=0)
    conv_out = gate_b * y

    x1, h2, logits = _mix_tail(x, pool_in, conv_out, g1, sc2, sh2, n2_ref[...], wpool_ref,
                               pscale_ref[...], wout_ref[...], wr_ref[...], br_ref[...])
    x1_ref[...] = x1
    h2_ref[...] = h2
    lg_ref[...] = logits
    for i in range(n_hist):
        newp_ref[i] = ext[steps + i]
    for i in range(n_chist):
        newc_ref[i] = vext[steps + i]


def _mixer_sample(x_tm, mod_s, pstate_tm, cstate_tm, n1, n2, w_in, w_pool, pscale, w_conv, w_out, w_r, b_r,
                  steps):
    rows, d = x_tm.shape
    return pl.pallas_call(
        functools.partial(_mixer_sample_kernel, steps=steps),
        out_shape=(
            jax.ShapeDtypeStruct((rows, d), F32),
            jax.ShapeDtypeStruct((rows, d), F32),
            jax.ShapeDtypeStruct((rows, LANES), F32),
            jax.ShapeDtypeStruct(pstate_tm.shape, F32),
            jax.ShapeDtypeStruct(cstate_tm.shape, F32),
        ),
        compiler_params=pltpu.CompilerParams(vmem_limit_bytes=VMEM_LIMIT),
        name="mixer_sample",
    )(x_tm, mod_s, pstate_tm, cstate_tm, n1, n2, w_in, w_pool, pscale, w_conv, w_out, w_r, b_r)


def _route_kernel(lg_ref, dest_ref, gate_ref, cnt_ref, counts, start):
    phase = pl.program_id(0)
    i = pl.program_id(1)
    tr = lg_ref.shape[0]

    @pl.when((phase == 0) & (i == 0))
    def _():
        counts[...] = jnp.zeros_like(counts)

    @pl.when((phase == 1) & (i == 0))
    def _():
        c = counts[...]
        hi = jnp.floor(c * (1.0 / 256.0))
        lo = c - hi * 256.0
        r = lax.broadcasted_iota(I32, (LANES, LANES), 0)
        col = lax.broadcasted_iota(I32, (LANES, LANES), 1)
        upper = jnp.where(r < col, 1.0, 0.0).astype(BF16)
        start[...] = 256.0 * _dot(hi.astype(BF16), upper) + _dot(lo.astype(BF16), upper)
        cnt_ref[...] = c
        counts[...] = jnp.zeros_like(counts)

    work = lg_ref[...]
    lane = lax.broadcasted_iota(I32, (tr, LANES), 1)
    top_v, onehots = [], []
    for _ in range(TOP_K):
        m = jnp.max(work, axis=1, keepdims=True)
        idx = jnp.min(jnp.where(work == m, lane, LANES), axis=1, keepdims=True)
        sel = lane == idx
        top_v.append(m)
        onehots.append(sel)
        work = jnp.where(sel, -jnp.inf, work)
    mask = jnp.where(onehots[0] | onehots[1] | onehots[2] | onehots[3], 1.0, 0.0)

    @pl.when(phase == 0)
    def _():
        counts[0:1, :] = counts[0:1, :] + jnp.sum(mask, axis=0, keepdims=True)

    @pl.when(phase == 1)
    def _():
        r = lax.broadcasted_iota(I32, (tr, tr), 0)
        col = lax.broadcasted_iota(I32, (tr, tr), 1)
        lower = jnp.where(col < r, 1.0, 0.0).astype(BF16)
        rank = _dot(lower, mask.astype(BF16)) + counts[0:1, :] + start[0:1, :]
        counts[0:1, :] = counts[0:1, :] + jnp.sum(mask, axis=0, keepdims=True)
        es = [jnp.exp(v - top_v[0]) for v in top_v]
        denom = es[0] + es[1] + es[2] + es[3]
        for k in range(TOP_K):
            d = jnp.sum(jnp.where(onehots[k], rank, 0.0), axis=1, keepdims=True)
            dest_ref[:, k:k + 1] = d.astype(I32)
            gate_ref[:, k:k + 1] = es[k] / denom


def _route(logits):
    n = logits.shape[0]
    tr = ROUTE_TILE
    return pl.pallas_call(
        _route_kernel,
        out_shape=(
            jax.ShapeDtypeStruct((n, TOP_K), I32),
            jax.ShapeDtypeStruct((n, TOP_K), F32),
            jax.ShapeDtypeStruct((8, LANES), F32),
        ),
        grid=(2, n // tr),
        in_specs=[pl.BlockSpec((tr, LANES), lambda p, i: (i, 0))],
        out_specs=(
            pl.BlockSpec((tr, TOP_K), lambda p, i: (i * p, 0)),
            pl.BlockSpec((tr, TOP_K), lambda p, i: (i * p, 0)),
            pl.BlockSpec((8, LANES), lambda p, i: (0, 0)),
        ),
        scratch_shapes=[pltpu.VMEM((8, LANES), F32), pltpu.VMEM((8, LANES), F32)],
        compiler_params=pltpu.CompilerParams(
            dimension_semantics=("arbitrary", "arbitrary"), vmem_limit_bytes=VMEM_LIMIT),
        name="route",
    )(logits)


def _group_metadata(counts, n_rows, tile):
    n_tiles = n_rows // tile
    n_steps = n_tiles + N_EXPERTS - 1
    ends = jnp.cumsum(counts)
    offs = jnp.concatenate([jnp.zeros((1,), I32), ends]).astype(I32)
    first_tile = offs[:-1] // tile
    last_tile = (ends - 1) // tile
    tiles_e = jnp.where(counts > 0, last_tile - first_tile + 1, 0)
    step_end = jnp.cumsum(tiles_e)
    step_start = step_end - tiles_e
    n_active = step_end[-1]
    s = jnp.minimum(jnp.arange(n_steps, dtype=I32), n_active - 1)
    gid = jnp.minimum(jnp.searchsorted(step_end, s, side="right"), N_EXPERTS - 1).astype(I32)
    tid = (first_tile[gid] + s - step_start[gid]).astype(I32)
    return gid, tid, offs, n_active.reshape(1).astype(I32), n_steps


def _dispatch_kernel(dest_ref, hp_ref, hs_ref, rows_ref, sem, *, n_prompt_tiles):
    i = pl.program_id(0)
    tm = hp_ref.shape[0]

    def scatter(src_ref):
        def issue(t, carry):
            for k in range(TOP_K):
                d = dest_ref[t * TOP_K + k]
                pltpu.make_async_copy(src_ref.at[pl.ds(t, 1)], rows_ref.at[pl.ds(d, 1)], sem).start()
            return carry
        lax.fori_loop(0, tm, issue, 0)
        for _ in range(TOP_K):
            pltpu.make_async_copy(src_ref, rows_ref.at[pl.ds(0, tm)], sem).wait()

    @pl.when(i < n_prompt_tiles)
    def _():
        scatter(hp_ref)

    @pl.when(i >= n_prompt_tiles)
    def _():
        scatter(hs_ref)


def _dispatch(dest_flat, h2_p, h2_s):
    n_p, d = h2_p.shape
    n_s = h2_s.shape[0]
    tm = MOVE_TILE
    npt, nst = n_p // tm, n_s // tm
    return pl.pallas_call(
        functools.partial(_dispatch_kernel, n_prompt_tiles=npt),
        out_shape=jax.ShapeDtypeStruct(((n_p + n_s) * TOP_K, d), h2_p.dtype),
        grid=(npt + nst,),
        in_specs=[
            pl.BlockSpec((tm * TOP_K,), lambda i: (i,), memory_space=pltpu.SMEM),
            pl.BlockSpec((tm, d), lambda i: (jnp.minimum(i, npt - 1), 0)),
            pl.BlockSpec((tm, d), lambda i: (jnp.maximum(i - npt, 0), 0)),
        ],
        out_specs=pl.BlockSpec(memory_space=pl.ANY),
        scratch_shapes=[pltpu.SemaphoreType.DMA],
        compiler_params=pltpu.CompilerParams(
            dimension_semantics=("arbitrary",), vmem_limit_bytes=VMEM_LIMIT),
        name="dispatch",
    )(dest_flat, h2_p, h2_s)


def _experts_kernel(gid_ref, tid_ref, offs_ref, nact_ref,
                    rows_ref, wg_ref, bg_ref, wu_ref, bu_ref, wd_ref, bd_ref, out_ref):
    s = pl.program_id(0)
    tm = rows_ref.shape[0]

    @pl.when(s < nact_ref[0])
    def _():
        e = gid_ref[s]
        m = tid_ref[s]
        x = rows_ref[...].astype(BF16)
        g = _dot(x, wg_ref[...]) + bg_ref[...]
        u = _dot(x, wu_ref[...]) + bu_ref[...]
        g = jnp.minimum(g, SWIGLU_LIMIT)
        u = jnp.clip(u, -SWIGLU_LIMIT, SWIGLU_LIMIT)
        glu = g * jax.nn.sigmoid(SWIGLU_ALPHA * g)
        y = _dot(((u + 1.0) * glu).astype(BF16), wd_ref[...]) + bd_ref[...]

        row = m * tm + lax.broadcasted_iota(I32, (tm, 1), 0)
        mine = (row >= offs_ref[e]) & (row < offs_ref[e + 1])
        first_visit = (s == 0) | (tid_ref[jnp.maximum(s - 1, 0)] != m)

        @pl.when(first_visit)
        def _():
            out_ref[...] = y

        @pl.when(jnp.logical_not(first_visit))
        def _():
            out_ref[...] = jnp.where(mine, y, out_ref[...])


def _experts(gid, tid, offs, nact, n_steps, rows, wg, bg, wu, bu, wd, bd):
    r, d = rows.shape
    ne, _, f = wg.shape
    tm = EXPERT_TILE
    grid_spec = pltpu.PrefetchScalarGridSpec(
        num_scalar_prefetch=4,
        grid=(n_steps,),
        in_specs=[
            pl.BlockSpec((tm, d), lambda s, gid, tid, offs, nact: (tid[s], 0)),
            pl.BlockSpec((None, d, f), lambda s, gid, tid, offs, nact: (gid[s], 0, 0)),
            pl.BlockSpec((None, 1, f), lambda s, gid, tid, offs, nact: (gid[s], 0, 0)),
            pl.BlockSpec((None, d, f), lambda s, gid, tid, offs, nact: (gid[s], 0, 0)),
            pl.BlockSpec((None, 1, f), lambda s, gid, tid, offs, nact: (gid[s], 0, 0)),
            pl.BlockSpec((None, f, d), lambda s, gid, tid, offs, nact: (gid[s], 0, 0)),
            pl.BlockSpec((None, 1, d), lambda s, gid, tid, offs, nact: (gid[s], 0, 0)),
        ],
        out_specs=pl.BlockSpec((tm, d), lambda s, gid, tid, offs, nact: (tid[s], 0)),
    )
    return pl.pallas_call(
        _experts_kernel,
        out_shape=jax.ShapeDtypeStruct((r, d), F32),
        grid_spec=grid_spec,
        compiler_params=pltpu.CompilerParams(
            dimension_semantics=("arbitrary",), vmem_limit_bytes=VMEM_LIMIT),
        name="experts",
    )(gid, tid, offs, nact, rows, wg, bg.reshape(ne, 1, f), wu, bu.reshape(ne, 1, f), wd,
      bd.reshape(ne, 1, d))


def _combine_kernel(dest_ref, gate_ref, x1p_ref, x1s_ref, modp_ref, g2s_ref, fn_ref, y_hbm,
                    outp_ref, outs_ref, gbuf, sem, *, n_prompt_tiles):
    i = pl.program_id(0)
    tm = x1p_ref.shape[0]

    def issue(t, carry):
        for k in range(TOP_K):
            d = dest_ref[t * TOP_K + k]
            pltpu.make_async_copy(y_hbm.at[pl.ds(d, 1)], gbuf.at[k, pl.ds(t, 1)], sem).start()
        return carry
    lax.fori_loop(0, tm, issue, 0)
    for k in range(TOP_K):
        pltpu.make_async_copy(y_hbm.at[pl.ds(0, tm)], gbuf.at[k], sem).wait()

    gates = gate_ref[...]
    ffn = gates[:, 0:1] * gbuf[0]
    for k in range(1, TOP_K):
        ffn = ffn + gates[:, k:k + 1] * gbuf[k]

    @pl.when(i < n_prompt_tiles)
    def _():
        g2 = modp_ref[...][5:6, :]
        outp_ref[...] = _rmsnorm(x1p_ref[...] + g2 * ffn, fn_ref[...])

    @pl.when(i >= n_prompt_tiles)
    def _():
        outs_ref[...] = _rmsnorm(x1s_ref[...] + g2s_ref[...] * ffn, fn_ref[...])


def _combine(dest_flat, gates, x1_p, x1_s, mod_p, g2_s, final_norm, y_rows, tokens_per_seq):
    n_p, d = x1_p.shape
    n_s = x1_s.shape[0]
    tm = MOVE_TILE
    npt, nst = n_p // tm, n_s // tm
    tiles_per_seq = tokens_per_seq // tm
    pmap = lambda i: (jnp.minimum(i, npt - 1), 0)
    smap = lambda i: (jnp.maximum(i - npt, 0), 0)
    return pl.pallas_call(
        functools.partial(_combine_kernel, n_prompt_tiles=npt),
        out_shape=(jax.ShapeDtypeStruct((n_p, d), F32), jax.ShapeDtypeStruct((n_s, d), F32)),
        grid=(npt + nst,),
        in_specs=[
            pl.BlockSpec((tm * TOP_K,), lambda i: (i,), memory_space=pltpu.SMEM),
            pl.BlockSpec((tm, TOP_K), lambda i: (i, 0)),
            pl.BlockSpec((tm, d), pmap),
            pl.BlockSpec((tm, d), smap),
            pl.BlockSpec((None, 6, d), lambda i: (jnp.minimum(i, npt - 1) // tiles_per_seq, 0, 0)),
            pl.BlockSpec((tm, d), smap),
            pl.BlockSpec((1, d), lambda i: (0, 0)),
            pl.BlockSpec(memory_space=pl.ANY),
        ],
        out_specs=(pl.BlockSpec((tm, d), pmap), pl.BlockSpec((tm, d), smap)),
        scratch_shapes=[pltpu.VMEM((TOP_K, tm, d), F32), pltpu.SemaphoreType.DMA],
        compiler_params=pltpu.CompilerParams(
            dimension_semantics=("arbitrary",), vmem_limit_bytes=VMEM_LIMIT),
        name="combine",
    )(dest_flat, gates, x1_p, x1_s, mod_p, g2_s, final_norm, y_rows)


def kernel(x_prompt, x_sample, state_pool, state_conv, c_prompt, c_sample, norm1, norm2, w_ada, b_ada,
           w_in, w_pool, pool_scale, w_conv, w_out, w_router, b_router, w_gate, b_gate, w_up, b_up,
           w_down, b_down, final_norm):
    depth = norm1.shape[0]
    assert depth == 1, "single-layer step"
    bp, tp, d = x_prompt.shape
    bs, ts, _ = x_sample.shape
    dp = state_pool.shape[-1]
    n_hist = state_pool.shape[2]
    n_chist = state_conv.shape[2]
    n_p, n_s = bp * tp, bs * ts
    assert tp % TOKEN_TILE == 0 and n_p % MOVE_TILE == 0 and n_s % MOVE_TILE == 0
    assert (n_p + n_s) % ROUTE_TILE == 0 and ((n_p + n_s) * TOP_K) % EXPERT_TILE == 0
    assert tp % MOVE_TILE == 0

    l = 0
    n1 = norm1[l].reshape(1, d)
    n2 = norm2[l].reshape(1, d)
    w_in_b = w_in[l].astype(BF16)
    w_pool_b = w_pool[l].astype(BF16)
    w_out_b = w_out[l].astype(BF16)
    pscale = pool_scale[l].reshape(1, dp)
    w_r = jnp.pad(w_router[l], ((0, 0), (0, LANES - N_EXPERTS))).astype(BF16)
    b_r = jnp.pad(b_router[l], (0, LANES - N_EXPERTS), constant_values=NEG_BIG).reshape(1, LANES)

    mod = _adaln(jnp.concatenate([c_prompt, c_sample], axis=0), w_ada[l], b_ada[l])
    mod_p = mod[:bp].reshape(bp, 6, d)
    mod_s = mod[bp:]

    x1_p, h2_p, lg_p, u_tail, v_tail = _mixer_prompt(
        x_prompt, mod_p, n1, n2, w_in_b, w_pool_b, pscale, w_conv[l], w_out_b, w_r, b_r)

    xs_tm = jnp.transpose(x_sample, (1, 0, 2)).reshape(n_s, d)
    ps_tm = jnp.transpose(state_pool[l], (1, 0, 2))
    cs_tm = jnp.transpose(state_conv[l], (1, 0, 2))
    x1_s, h2_s, lg_s, newp_tm, newc_tm = _mixer_sample(
        xs_tm, mod_s, ps_tm, cs_tm, n1, n2, w_in_b, w_pool_b, pscale, w_conv[l], w_out_b, w_r, b_r, ts)

    dest, gates, counts_f = _route(jnp.concatenate([lg_p, lg_s], axis=0))
    counts = counts_f[0, :N_EXPERTS].astype(I32)
    dest_flat = dest.reshape(-1)
    n_rows = (n_p + n_s) * TOP_K
    gid, tid, offs, nact, n_steps = _group_metadata(counts, n_rows, EXPERT_TILE)

    rows = _dispatch(dest_flat, h2_p, h2_s)
    y_rows = _experts(gid, tid, offs, nact, n_steps, rows,
                      w_gate[l].astype(BF16), b_gate[l], w_up[l].astype(BF16), b_up[l],
                      w_down[l].astype(BF16), b_down[l])

    g2_s = jnp.tile(mod_s[:, 5 * d:], (ts, 1))
    y_p, y_s = _combine(dest_flat, gates, x1_p, x1_s, mod_p, g2_s, final_norm.reshape(1, d), y_rows, tp)

    y_prompt = y_p.reshape(bp, tp, d)
    y_sample = jnp.transpose(y_s.reshape(ts, bs, d), (1, 0, 2))
    new_pool_prompt = u_tail[:, POOL_HALO - n_hist:, :][None]
    new_conv_prompt = v_tail[:, CONV_HALO - n_chist:, :][None]
    new_pool_sample = jnp.transpose(newp_tm, (1, 0, 2))[None]
    new_conv_sample = jnp.transpose(newc_tm, (1, 0, 2))[None]
    return (y_prompt, y_sample, new_pool_prompt, new_conv_prompt, new_pool_sample, new_conv_sample)
```

```python
import functools

import jax
import jax.numpy as jnp
from jax import lax
from jax.experimental import pallas as pl
from jax.experimental.pallas import tpu as pltpu

F32 = jnp.float32
BF16 = jnp.bfloat16
I32 = jnp.int32

POOL_WINDOWS = (2, 4, 8, 16)
POOL_HALO = 16
CONV_TAPS = 3
CONV_HALO = 8
N_EXPERTS = 32
TOP_K = 4
SWIGLU_LIMIT = 7.0
SWIGLU_ALPHA = 1.702
EPS = 1e-5
PAST_LEN = 16384

LANES = 128
NEG_BIG = -1e30

TOKEN_TILE = 512
ROUTE_TILE = 512
MOVE_TILE = 256
EXPERT_TILE = 512
VMEM_LIMIT = 56 * 1024 * 1024


def _rmsnorm(x, g):
    ms = jnp.mean(x * x, axis=-1, keepdims=True)
    return x * lax.rsqrt(ms + EPS) * g


def _dot(a, b):
    return jnp.dot(a, b, preferred_element_type=F32)


def _adaln_kernel(c_ref, w_ref, b_ref, o_ref):
    c = c_ref[...]
    s = c * jax.nn.sigmoid(c)
    o_ref[...] = _dot(s.astype(BF16), w_ref[...].astype(BF16)) + b_ref[...]


def _adaln(c, w_ada, b_ada):
    rows, d = c.shape
    n = w_ada.shape[1]
    tn = 1024
    return pl.pallas_call(
        _adaln_kernel,
        out_shape=jax.ShapeDtypeStruct((rows, n), F32),
        grid=(n // tn,),
        in_specs=[
            pl.BlockSpec((rows, d), lambda j: (0, 0)),
            pl.BlockSpec((d, tn), lambda j: (0, j)),
            pl.BlockSpec((1, tn), lambda j: (0, j)),
        ],
        out_specs=pl.BlockSpec((rows, tn), lambda j: (0, j)),
        compiler_params=pltpu.CompilerParams(
            dimension_semantics=("arbitrary",), vmem_limit_bytes=VMEM_LIMIT),
        name="adaln",
    )(c, w_ada, b_ada.reshape(1, n))


def _mix_tail(x, pool_in, conv_out, g1, sc2, sh2, n2, wpool_ref, pscale, wout, wr, br):
    gw = pool_in.shape[1] // len(POOL_WINDOWS)
    mixed = [_dot(pool_in[:, g * gw:(g + 1) * gw].astype(BF16), wpool_ref[g])
             for g in range(len(POOL_WINDOWS))]
    pool_out = jnp.concatenate(mixed, axis=-1) * pscale
    mix_in = jnp.concatenate([pool_out, conv_out], axis=-1).astype(BF16)
    x1 = x + g1 * _dot(mix_in, wout)
    h2 = _rmsnorm(x1, n2) * (1.0 + sc2) + sh2
    logits = _dot(h2.astype(BF16), wr) + br
    return x1, h2, logits


def _mixer_prompt_kernel(x_ref, mod_ref, n1_ref, n2_ref, win_ref, wpool_ref, pscale_ref, wconv_ref,
                         wout_ref, wr_ref, br_ref,
                         x1_ref, h2_ref, lg_ref, upool_ref, vconv_ref, ubuf, vbuf):
    tt = x_ref.shape[0]
    dp = ubuf.shape[1]
    gw = dp // len(POOL_WINDOWS)
    t = pl.program_id(1)

    @pl.when(t == 0)
    def _():
        ubuf[0:POOL_HALO, :] = jnp.zeros((POOL_HALO, dp), F32)
        vbuf[0:CONV_HALO, :] = jnp.zeros((CONV_HALO, dp), F32)

    x = x_ref[...]
    mod = mod_ref[...]
    sh1, sc1, g1, sh2, sc2, _ = [mod[i:i + 1, :] for i in range(6)]
    h = _rmsnorm(x, n1_ref[...]) * (1.0 + sc1) + sh1
    z = _dot(h.astype(BF16), win_ref[...])
    u, gate_b, gate_c, val = [z[:, i * dp:(i + 1) * dp] for i in range(4)]

    ubuf[POOL_HALO:POOL_HALO + tt, :] = u
    pos = lax.broadcasted_iota(I32, (tt, gw), 0) + t * tt
    pooled = []
    for g, w in enumerate(POOL_WINDOWS):
        cols = slice(g * gw, (g + 1) * gw)
        acc = u[:, cols]
        for j in range(1, w):
            acc = acc + ubuf[POOL_HALO - j:POOL_HALO - j + tt, cols]
        cnt = jnp.minimum(pos + 1, w).astype(F32)
        pooled.append(acc / cnt - u[:, cols])
    pool_in = jnp.concatenate(pooled, axis=-1)

    v = gate_c * val
    vbuf[CONV_HALO:CONV_HALO + tt, :] = v
    wc = wconv_ref[...]
    y = (wc[0:1, :] * vbuf[CONV_HALO - 2:CONV_HALO - 2 + tt, :]
         + wc[1:2, :] * vbuf[CONV_HALO - 1:CONV_HALO - 1 + tt, :]
         + wc[2:3, :] * v)
    conv_out = gate_b * y

    x1, h2, logits = _mix_tail(x, pool_in, conv_out, g1, sc2, sh2, n2_ref[...], wpool_ref,
                               pscale_ref[...], wout_ref[...], wr_ref[...], br_ref[...])
    x1_ref[...] = x1
    h2_ref[...] = h2
    lg_ref[...] = logits

    ubuf[0:POOL_HALO, :] = ubuf[tt:tt + POOL_HALO, :]
    vbuf[0:CONV_HALO, :] = vbuf[tt:tt + CONV_HALO, :]

    @pl.when(t == pl.num_programs(1) - 1)
    def _():
        upool_ref[...] = ubuf[0:POOL_HALO, :]
        vconv_ref[...] = vbuf[0:CONV_HALO, :]


def _mixer_prompt(x, mod_p, n1, n2, w_in, w_pool, pscale, w_conv, w_out, w_r, b_r):
    b, t, d = x.shape
    dp = w_pool.shape[0] * w_pool.shape[1]
    tt = min(TOKEN_TILE, t)
    const2 = lambda i, j: (0, 0)
    const3 = lambda i, j: (0, 0, 0)
    return pl.pallas_call(
        _mixer_prompt_kernel,
        out_shape=(
            jax.ShapeDtypeStruct((b * t, d), F32),
            jax.ShapeDtypeStruct((b * t, d), F32),
            jax.ShapeDtypeStruct((b * t, LANES), F32),
            jax.ShapeDtypeStruct((b, POOL_HALO, dp), F32),
            jax.ShapeDtypeStruct((b, CONV_HALO, dp), F32),
        ),
        grid=(b, t // tt),
        in_specs=[
            pl.BlockSpec((None, tt, d), lambda i, j: (i, j, 0)),
            pl.BlockSpec((None, 6, d), lambda i, j: (i, 0, 0)),
            pl.BlockSpec((1, d), const2),
            pl.BlockSpec((1, d), const2),
            pl.BlockSpec(w_in.shape, const2),
            pl.BlockSpec(w_pool.shape, const3),
            pl.BlockSpec((1, dp), const2),
            pl.BlockSpec(w_conv.shape, const2),
            pl.BlockSpec(w_out.shape, const2),
            pl.BlockSpec(w_r.shape, const2),
            pl.BlockSpec((1, LANES), const2),
        ],
        out_specs=(
            pl.BlockSpec((tt, d), lambda i, j: (i * (t // tt) + j, 0)),
            pl.BlockSpec((tt, d), lambda i, j: (i * (t // tt) + j, 0)),
            pl.BlockSpec((tt, LANES), lambda i, j: (i * (t // tt) + j, 0)),
            pl.BlockSpec((None, POOL_HALO, dp), lambda i, j: (i, 0, 0)),
            pl.BlockSpec((None, CONV_HALO, dp), lambda i, j: (i, 0, 0)),
        ),
        scratch_shapes=[
            pltpu.VMEM((POOL_HALO + tt, dp), F32),
            pltpu.VMEM((CONV_HALO + tt, dp), F32),
        ],
        compiler_params=pltpu.CompilerParams(
            dimension_semantics=("arbitrary", "arbitrary"), vmem_limit_bytes=VMEM_LIMIT),
        name="mixer_prompt",
    )(x, mod_p, n1, n2, w_in, w_pool, pscale, w_conv, w_out, w_r, b_r)


def _mixer_sample_kernel(x_ref, mod_ref, pstate_ref, cstate_ref, n1_ref, n2_ref, win_ref, wpool_ref,
                         pscale_ref, wconv_ref, wout_ref, wr_ref, br_ref,
                         x1_ref, h2_ref, lg_ref, newp_ref, newc_ref, *, steps):
    nb = mod_ref.shape[0]
    d = x_ref.shape[1]
    dp = pstate_ref.shape[2]
    gw = dp // len(POOL_WINDOWS)
    n_hist = pstate_ref.shape[0]
    n_chist = cstate_ref.shape[0]

    x = x_ref[...]
    mod = mod_ref[...]
    rep = lambda a: jnp.concatenate([a] * steps, axis=0)
    sh1, sc1, g1, sh2, sc2, _ = [rep(mod[:, i * d:(i + 1) * d]) for i in range(6)]
    h = _rmsnorm(x, n1_ref[...]) * (1.0 + sc1) + sh1
    z = _dot(h.astype(BF16), win_ref[...])
    u, gate_b, gate_c, val = [z[:, i * dp:(i + 1) * dp] for i in range(4)]

    ext = [pstate_ref[i] for i in range(n_hist)] + [u[s * nb:(s + 1) * nb, :] for s in range(steps)]
    pooled_steps = []
    for s in range(steps):
        groups = []
        for g, w in enumerate(POOL_WINDOWS):
            cols = slice(g * gw, (g + 1) * gw)
            acc = ext[n_hist + s][:, cols]
            for j in range(1, w):
                acc = acc + ext[n_hist + s - j][:, cols]
            cnt = float(min(PAST_LEN + s + 1, w))
            groups.append(acc / cnt - ext[n_hist + s][:, cols])
        pooled_steps.append(jnp.concatenate(groups, axis=-1))
    pool_in = jnp.concatenate(pooled_steps, axis=0)

    v = gate_c * val
    vext = [cstate_ref[i] for i in range(n_chist)] + [v[s * nb:(s + 1) * nb, :] for s in range(steps)]
    wc = wconv_ref[...]
    y = jnp.concatenate(
        [wc[0:1, :] * vext[s] + wc[1:2, :] * vext[s + 1] + wc[2:3, :] * vext[s + 2] for s in range(steps)],
        axis=0)
    conv_out = gate_b * y

    x1, h2, logits = _mix_tail(x, pool_in, conv_out, g1, sc2, sh2, n2_ref[...], wpool_ref,
                               pscale_ref[...], wout_ref[...], wr_ref[...], br_ref[...])
    x1_ref[...] = x1
    h2_ref[...] = h2
    lg_ref[...] = logits
    for i in range(n_hist):
        newp_ref[i] = ext[steps + i]
    for i in range(n_chist):
        newc_ref[i] = vext[steps + i]


def _mixer_sample(x_tm, mod_s, pstate_tm, cstate_tm, n1, n2, w_in, w_pool, pscale, w_conv, w_out, w_r, b_r,
                  steps):
    rows, d = x_tm.shape
    return pl.pallas_call(
        functools.partial(_mixer_sample_kernel, steps=steps),
        out_shape=(
            jax.ShapeDtypeStruct((rows, d), F32),
            jax.ShapeDtypeStruct((rows, d), F32),
            jax.ShapeDtypeStruct((rows, LANES), F32),
            jax.ShapeDtypeStruct(pstate_tm.shape, F32),
            jax.ShapeDtypeStruct(cstate_tm.shape, F32),
        ),
        compiler_params=pltpu.CompilerParams(vmem_limit_bytes=VMEM_LIMIT),
        name="mixer_sample",
    )(x_tm, mod_s, pstate_tm, cstate_tm, n1, n2, w_in, w_pool, pscale, w_conv, w_out, w_r, b_r)


def _route_kernel(lg_ref, dest_ref, gate_ref, cnt_ref, counts, start):
    phase = pl.program_id(0)
    i = pl.program_id(1)
    tr = lg_ref.shape[0]

    @pl.when((phase == 0) & (i == 0))
    def _():
        counts[...] = jnp.zeros_like(counts)

    @pl.when((phase == 1) & (i == 0))
    def _():
        c = counts[...]
        hi = jnp.floor(c * (1.0 / 256.0))
        lo = c - hi * 256.0
        r = lax.broadcasted_iota(I32, (LANES, LANES), 0)
        col = lax.broadcasted_iota(I32, (LANES, LANES), 1)
        upper = jnp.where(r < col, 1.0, 0.0).astype(BF16)
        start[...] = 256.0 * _dot(hi.astype(BF16), upper) + _dot(lo.astype(BF16), upper)
        cnt_ref[...] = c
        counts[...] = jnp.zeros_like(counts)

    work = lg_ref[...]
    lane = lax.broadcasted_iota(I32, (tr, LANES), 1)
    top_v, onehots = [], []
    for _ in range(TOP_K):
        m = jnp.max(work, axis=1, keepdims=True)
        idx = jnp.min(jnp.where(work == m, lane, LANES), axis=1, keepdims=True)
        sel = lane == idx
        top_v.append(m)
        onehots.append(sel)
        work = jnp.where(sel, -jnp.inf, work)
    mask = jnp.where(onehots[0] | onehots[1] | onehots[2] | onehots[3], 1.0, 0.0)

    @pl.when(phase == 0)
    def _():
        counts[0:1, :] = counts[0:1, :] + jnp.sum(mask, axis=0, keepdims=True)

    @pl.when(phase == 1)
    def _():
        r = lax.broadcasted_iota(I32, (tr, tr), 0)
        col = lax.broadcasted_iota(I32, (tr, tr), 1)
        lower = jnp.where(col < r, 1.0, 0.0).astype(BF16)
        rank = _dot(lower, mask.astype(BF16)) + counts[0:1, :] + start[0:1, :]
        counts[0:1, :] = counts[0:1, :] + jnp.sum(mask, axis=0, keepdims=True)
        es = [jnp.exp(v - top_v[0]) for v in top_v]
        denom = es[0] + es[1] + es[2] + es[3]
        for k in range(TOP_K):
            d = jnp.sum(jnp.where(onehots[k], rank, 0.0), axis=1, keepdims=True)
            dest_ref[:, k:k + 1] = d.astype(I32)
            gate_ref[:, k:k + 1] = es[k] / denom


def _route(logits):
    n = logits.shape[0]
    tr = ROUTE_TILE
    return pl.pallas_call(
        _route_kernel,
        out_shape=(
            jax.ShapeDtypeStruct((n, TOP_K), I32),
            jax.ShapeDtypeStruct((n, TOP_K), F32),
            jax.ShapeDtypeStruct((8, LANES), F32),
        ),
        grid=(2, n // tr),
        in_specs=[pl.BlockSpec((tr, LANES), lambda p, i: (i, 0))],
        out_specs=(
            pl.BlockSpec((tr, TOP_K), lambda p, i: (i * p, 0)),
            pl.BlockSpec((tr, TOP_K), lambda p, i: (i * p, 0)),
            pl.BlockSpec((8, LANES), lambda p, i: (0, 0)),
        ),
        scratch_shapes=[pltpu.VMEM((8, LANES), F32), pltpu.VMEM((8, LANES), F32)],
        compiler_params=pltpu.CompilerParams(
            dimension_semantics=("arbitrary", "arbitrary"), vmem_limit_bytes=VMEM_LIMIT),
        name="route",
    )(logits)


def _group_metadata(counts, n_rows, tile):
    n_tiles = n_rows // tile
    n_steps = n_tiles + N_EXPERTS - 1
    ends = jnp.cumsum(counts)
    offs = jnp.concatenate([jnp.zeros((1,), I32), ends]).astype(I32)
    first_tile = offs[:-1] // tile
    last_tile = (ends - 1) // tile
    tiles_e = jnp.where(counts > 0, last_tile - first_tile + 1, 0)
    step_end = jnp.cumsum(tiles_e)
    step_start = step_end - tiles_e
    n_active = step_end[-1]
    s = jnp.minimum(jnp.arange(n_steps, dtype=I32), n_active - 1)
    gid = jnp.minimum(jnp.sum((step_end[None, :] <= s[:, None]).astype(I32), axis=1), N_EXPERTS - 1)
    tid = (first_tile[gid] + s - step_start[gid]).astype(I32)
    return gid, tid, offs, n_active.reshape(1).astype(I32), n_steps


def _dispatch_kernel(dest_ref, hp_ref, hs_ref, rows_ref, sem, *, n_prompt_tiles):
    i = pl.program_id(0)
    tm = hp_ref.shape[0]

    def scatter(src_ref):
        def issue(t, carry):
            for k in range(TOP_K):
                d = dest_ref[t * TOP_K + k]
                pltpu.make_async_copy(
                    src_ref.at[pl.ds(t, 1)], rows_ref.at[pl.ds(d, 1)], sem).start(priority=k % 2)
            return carry
        lax.fori_loop(0, tm, issue, 0)
        for _ in range(TOP_K):
            pltpu.make_async_copy(src_ref, rows_ref.at[pl.ds(0, tm)], sem).wait()

    @pl.when(i < n_prompt_tiles)
    def _():
        scatter(hp_ref)

    @pl.when(i >= n_prompt_tiles)
    def _():
        scatter(hs_ref)


def _dispatch(dest_flat, h2_p, h2_s):
    n_p, d = h2_p.shape
    n_s = h2_s.shape[0]
    tm = MOVE_TILE
    npt, nst = n_p // tm, n_s // tm
    return pl.pallas_call(
        functools.partial(_dispatch_kernel, n_prompt_tiles=npt),
        out_shape=jax.ShapeDtypeStruct(((n_p + n_s) * TOP_K, d), h2_p.dtype),
        grid=(npt + nst,),
        in_specs=[
            pl.BlockSpec((tm * TOP_K,), lambda i: (i,), memory_space=pltpu.SMEM),
            pl.BlockSpec((tm, d), lambda i: (jnp.minimum(i, npt - 1), 0)),
            pl.BlockSpec((tm, d), lambda i: (jnp.maximum(i - npt, 0), 0)),
        ],
        out_specs=pl.BlockSpec(memory_space=pl.ANY),
        scratch_shapes=[pltpu.SemaphoreType.DMA],
        compiler_params=pltpu.CompilerParams(
            dimension_semantics=("arbitrary",), vmem_limit_bytes=VMEM_LIMIT),
        name="dispatch",
    )(dest_flat, h2_p, h2_s)


def _experts_kernel(gid_ref, tid_ref, offs_ref, nact_ref,
                    rows_ref, wg_ref, bg_ref, wu_ref, bu_ref, wd_ref, bd_ref, out_ref,
                    wg_b, wu_b, wd_b):
    s = pl.program_id(0)
    tm = rows_ref.shape[0]

    @pl.when(s < nact_ref[0])
    def _():
        e = gid_ref[s]
        m = tid_ref[s]

        @pl.when((s == 0) | (gid_ref[jnp.maximum(s - 1, 0)] != e))
        def _():
            wg_b[...] = wg_ref[...].astype(BF16)
            wu_b[...] = wu_ref[...].astype(BF16)
            wd_b[...] = wd_ref[...].astype(BF16)

        x = rows_ref[...].astype(BF16)
        g = _dot(x, wg_b[...]) + bg_ref[...]
        u = _dot(x, wu_b[...]) + bu_ref[...]
        g = jnp.minimum(g, SWIGLU_LIMIT)
        u = jnp.clip(u, -SWIGLU_LIMIT, SWIGLU_LIMIT)
        glu = g * jax.nn.sigmoid(SWIGLU_ALPHA * g)
        y = _dot(((u + 1.0) * glu).astype(BF16), wd_b[...]) + bd_ref[...]

        row = m * tm + lax.broadcasted_iota(I32, (tm, 1), 0)
        mine = (row >= offs_ref[e]) & (row < offs_ref[e + 1])
        first_visit = (s == 0) | (tid_ref[jnp.maximum(s - 1, 0)] != m)

        @pl.when(first_visit)
        def _():
            out_ref[...] = y

        @pl.when(jnp.logical_not(first_visit))
        def _():
            out_ref[...] = jnp.where(mine, y, out_ref[...])


def _experts(gid, tid, offs, nact, n_steps, rows, wg, bg, wu, bu, wd, bd):
    r, d = rows.shape
    ne, _, f = wg.shape
    tm = EXPERT_TILE
    grid_spec = pltpu.PrefetchScalarGridSpec(
        num_scalar_prefetch=4,
        grid=(n_steps,),
        in_specs=[
            pl.BlockSpec((tm, d), lambda s, gid, tid, offs, nact: (tid[s], 0)),
            pl.BlockSpec((None, d, f), lambda s, gid, tid, offs, nact: (gid[s], 0, 0)),
            pl.BlockSpec((None, 1, f), lambda s, gid, tid, offs, nact: (gid[s], 0, 0)),
            pl.BlockSpec((None, d, f), lambda s, gid, tid, offs, nact: (gid[s], 0, 0)),
            pl.BlockSpec((None, 1, f), lambda s, gid, tid, offs, nact: (gid[s], 0, 0)),
            pl.BlockSpec((None, f, d), lambda s, gid, tid, offs, nact: (gid[s], 0, 0)),
            pl.BlockSpec((None, 1, d), lambda s, gid, tid, offs, nact: (gid[s], 0, 0)),
        ],
        out_specs=pl.BlockSpec((tm, d), lambda s, gid, tid, offs, nact: (tid[s], 0)),
        scratch_shapes=[pltpu.VMEM((d, f), BF16), pltpu.VMEM((d, f), BF16), pltpu.VMEM((f, d), BF16)],
    )
    return pl.pallas_call(
        _experts_kernel,
        out_shape=jax.ShapeDtypeStruct((r, d), F32),
        grid_spec=grid_spec,
        compiler_params=pltpu.CompilerParams(
            dimension_semantics=("arbitrary",), vmem_limit_bytes=VMEM_LIMIT),
        name="experts",
    )(gid, tid, offs, nact, rows, wg, bg.reshape(ne, 1, f), wu, bu.reshape(ne, 1, f), wd,
      bd.reshape(ne, 1, d))


def _combine_kernel(dest_ref, gate_ref, x1p_ref, x1s_ref, modp_ref, g2s_ref, fn_ref, y_hbm,
                    outp_ref, outs_ref, gbuf, sem, *, n_prompt_tiles):
    i = pl.program_id(0)
    tm = x1p_ref.shape[0]

    def issue(t, carry):
        for k in range(TOP_K):
            d = dest_ref[t * TOP_K + k]
            pltpu.make_async_copy(
                y_hbm.at[pl.ds(d, 1)], gbuf.at[k, pl.ds(t, 1)], sem).start(priority=k % 2)
        return carry
    lax.fori_loop(0, tm, issue, 0)
    for k in range(TOP_K):
        pltpu.make_async_copy(y_hbm.at[pl.ds(0, tm)], gbuf.at[k], sem).wait()

    gates = gate_ref[...]
    ffn = gates[:, 0:1] * gbuf[0]
    for k in range(1, TOP_K):
        ffn = ffn + gates[:, k:k + 1] * gbuf[k]

    @pl.when(i < n_prompt_tiles)
    def _():
        g2 = modp_ref[...][5:6, :]
        outp_ref[...] = _rmsnorm(x1p_ref[...] + g2 * ffn, fn_ref[...])

    @pl.when(i >= n_prompt_tiles)
    def _():
        outs_ref[...] = _rmsnorm(x1s_ref[...] + g2s_ref[...] * ffn, fn_ref[...])


def _combine(dest_flat, gates, x1_p, x1_s, mod_p, g2_s, final_norm, y_rows, tokens_per_seq):
    n_p, d = x1_p.shape
    n_s = x1_s.shape[0]
    tm = MOVE_TILE
    npt, nst = n_p // tm, n_s // tm
    tiles_per_seq = tokens_per_seq // tm
    pmap = lambda i: (jnp.minimum(i, npt - 1), 0)
    smap = lambda i: (jnp.maximum(i - npt, 0), 0)
    return pl.pallas_call(
        functools.partial(_combine_kernel, n_prompt_tiles=npt),
        out_shape=(jax.ShapeDtypeStruct((n_p, d), F32), jax.ShapeDtypeStruct((n_s, d), F32)),
        grid=(npt + nst,),
        in_specs=[
            pl.BlockSpec((tm * TOP_K,), lambda i: (i,), memory_space=pltpu.SMEM),
            pl.BlockSpec((tm, TOP_K), lambda i: (i, 0)),
            pl.BlockSpec((tm, d), pmap),
            pl.BlockSpec((tm, d), smap),
            pl.BlockSpec((None, 6, d), lambda i: (jnp.minimum(i, npt - 1) // tiles_per_seq, 0, 0)),
            pl.BlockSpec((tm, d), smap),
            pl.BlockSpec((1, d), lambda i: (0, 0)),
            pl.BlockSpec(memory_space=pl.ANY),
        ],
        out_specs=(pl.BlockSpec((tm, d), pmap), pl.BlockSpec((tm, d), smap)),
        scratch_shapes=[pltpu.VMEM((TOP_K, tm, d), F32), pltpu.SemaphoreType.DMA],
        compiler_params=pltpu.CompilerParams(
            dimension_semantics=("arbitrary",), vmem_limit_bytes=VMEM_LIMIT),
        name="combine",
    )(dest_flat, gates, x1_p, x1_s, mod_p, g2_s, final_norm, y_rows)


def kernel(x_prompt, x_sample, state_pool, state_conv, c_prompt, c_sample, norm1, norm2, w_ada, b_ada,
           w_in, w_pool, pool_scale, w_conv, w_out, w_router, b_router, w_gate, b_gate, w_up, b_up,
           w_down, b_down, final_norm):
    depth = norm1.shape[0]
    assert depth == 1, "single-layer step"
    bp, tp, d = x_prompt.shape
    bs, ts, _ = x_sample.shape
    dp = state_pool.shape[-1]
    n_hist = state_pool.shape[2]
    n_chist = state_conv.shape[2]
    n_p, n_s = bp * tp, bs * ts
    assert tp % TOKEN_TILE == 0 and n_p % MOVE_TILE == 0 and n_s % MOVE_TILE == 0
    assert (n_p + n_s) % ROUTE_TILE == 0 and ((n_p + n_s) * TOP_K) % EXPERT_TILE == 0
    assert tp % MOVE_TILE == 0

    l = 0
    n1 = norm1[l].reshape(1, d)
    n2 = norm2[l].reshape(1, d)
    w_in_b = w_in[l].astype(BF16)
    w_pool_b = w_pool[l].astype(BF16)
    w_out_b = w_out[l].astype(BF16)
    pscale = pool_scale[l].reshape(1, dp)
    w_r = jnp.pad(w_router[l], ((0, 0), (0, LANES - N_EXPERTS))).astype(BF16)
    b_r = jnp.pad(b_router[l], (0, LANES - N_EXPERTS), constant_values=NEG_BIG).reshape(1, LANES)

    mod = _adaln(jnp.concatenate([c_prompt, c_sample], axis=0), w_ada[l], b_ada[l])
    mod_p = mod[:bp].reshape(bp, 6, d)
    mod_s = mod[bp:]

    x1_p, h2_p, lg_p, u_tail, v_tail = _mixer_prompt(
        x_prompt, mod_p, n1, n2, w_in_b, w_pool_b, pscale, w_conv[l], w_out_b, w_r, b_r)

    xs_tm = jnp.transpose(x_sample, (1, 0, 2)).reshape(n_s, d)
    ps_tm = jnp.transpose(state_pool[l], (1, 0, 2))
    cs_tm = jnp.transpose(state_conv[l], (1, 0, 2))
    x1_s, h2_s, lg_s, newp_tm, newc_tm = _mixer_sample(
        xs_tm, mod_s, ps_tm, cs_tm, n1, n2, w_in_b, w_pool_b, pscale, w_conv[l], w_out_b, w_r, b_r, ts)

    dest, gates, counts_f = _route(jnp.concatenate([lg_p, lg_s], axis=0))
    counts = counts_f[0, :N_EXPERTS].astype(I32)
    dest_flat = dest.reshape(-1)
    n_rows = (n_p + n_s) * TOP_K
    gid, tid, offs, nact, n_steps = _group_metadata(counts, n_rows, EXPERT_TILE)

    rows = _dispatch(dest_flat, h2_p, h2_s)
    y_rows = _experts(gid, tid, offs, nact, n_steps, rows,
                      w_gate[l], b_gate[l], w_up[l], b_up[l], w_down[l], b_down[l])

    g2_s = jnp.tile(mod_s[:, 5 * d:], (ts, 1))
    y_p, y_s = _combine(dest_flat, gates, x1_p, x1_s, mod_p, g2_s, final_norm.reshape(1, d), y_rows, tp)

    y_prompt = y_p.reshape(bp, tp, d)
    y_sample = jnp.transpose(y_s.reshape(ts, bs, d), (1, 0, 2))
    new_pool_prompt = u_tail[:, POOL_HALO - n_hist:, :][None]
    new_conv_prompt = v_tail[:, CONV_HALO - n_chist:, :][None]
    new_pool_sample = jnp.transpose(newp_tm, (1, 0, 2))[None]
    new_conv_sample = jnp.transpose(newc_tm, (1, 0, 2))[None]
    return (y_prompt, y_sample, new_pool_prompt, new_conv_prompt, new_pool_sample, new_conv_sample)
```

```python
import functools

import jax
import jax.numpy as jnp
from jax import lax
from jax.experimental import pallas as pl
from jax.experimental.pallas import tpu as pltpu

F32 = jnp.float32
BF16 = jnp.bfloat16
I32 = jnp.int32

POOL_WINDOWS = (2, 4, 8, 16)
POOL_HALO = 16
CONV_TAPS = 3
CONV_HALO = 8
N_EXPERTS = 32
TOP_K = 4
SWIGLU_LIMIT = 7.0
SWIGLU_ALPHA = 1.702
EPS = 1e-5
PAST_LEN = 16384

LANES = 128
SUBLANES = 8
NEG_BIG = -1e30

TOKEN_TILE = 512
ROUTE_TILE = 512
DISPATCH_TILE = 512
COMBINE_TILE = 256
EXPERT_TILE = 512
VMEM_LIMIT = 56 * 1024 * 1024


def _rmsnorm(x, g):
    ms = jnp.mean(x * x, axis=-1, keepdims=True)
    return x * lax.rsqrt(ms + EPS) * g


def _dot(a, b):
    return jnp.dot(a, b, preferred_element_type=F32)


def _store_token_tiles(ref, val):
    rows = val.shape[0]
    for c in range(SUBLANES):
        ref[pl.ds(c, rows, stride=SUBLANES), :] = val[:, c * LANES:(c + 1) * LANES]


def _load_token_tiles(ref):
    rows = ref.shape[0] // SUBLANES
    return jnp.concatenate([ref[pl.ds(c, rows, stride=SUBLANES), :] for c in range(SUBLANES)], axis=-1)


def _adaln_kernel(c_ref, w_ref, b_ref, o_ref):
    c = c_ref[...]
    s = c * jax.nn.sigmoid(c)
    o_ref[...] = _dot(s.astype(BF16), w_ref[...].astype(BF16)) + b_ref[...]


def _adaln(c, w_ada, b_ada):
    rows, d = c.shape
    n = w_ada.shape[1]
    tn = 1024
    return pl.pallas_call(
        _adaln_kernel,
        out_shape=jax.ShapeDtypeStruct((rows, n), F32),
        grid=(n // tn,),
        in_specs=[
            pl.BlockSpec((rows, d), lambda j: (0, 0)),
            pl.BlockSpec((d, tn), lambda j: (0, j)),
            pl.BlockSpec((1, tn), lambda j: (0, j)),
        ],
        out_specs=pl.BlockSpec((rows, tn), lambda j: (0, j)),
        compiler_params=pltpu.CompilerParams(
            dimension_semantics=("arbitrary",), vmem_limit_bytes=VMEM_LIMIT),
        name="adaln",
    )(c, w_ada, b_ada.reshape(1, n))


def _mix_tail(x, pool_in, conv_out, g1, sc2, sh2, n2, wpool_ref, pscale, wout, wr, br):
    gw = pool_in.shape[1] // len(POOL_WINDOWS)
    mixed = [_dot(pool_in[:, g * gw:(g + 1) * gw].astype(BF16), wpool_ref[g])
             for g in range(len(POOL_WINDOWS))]
    pool_out = jnp.concatenate(mixed, axis=-1) * pscale
    mix_in = jnp.concatenate([pool_out, conv_out], axis=-1).astype(BF16)
    x1 = x + g1 * _dot(mix_in, wout)
    h2 = _rmsnorm(x1, n2) * (1.0 + sc2) + sh2
    logits = _dot(h2.astype(BF16), wr) + br
    return x1, h2, logits


def _mixer_prompt_kernel(x_ref, mod_ref, n1_ref, n2_ref, win_ref, wpool_ref, pscale_ref, wconv_ref,
                         wout_ref, wr_ref, br_ref,
                         x1_ref, h2_ref, lg_ref, upool_ref, vconv_ref, ubuf, vbuf):
    tt = x_ref.shape[0]
    dp = ubuf.shape[1]
    gw = dp // len(POOL_WINDOWS)
    t = pl.program_id(1)

    @pl.when(t == 0)
    def _():
        ubuf[0:POOL_HALO, :] = jnp.zeros((POOL_HALO, dp), F32)
        vbuf[0:CONV_HALO, :] = jnp.zeros((CONV_HALO, dp), F32)

    x = x_ref[...]
    mod = mod_ref[...]
    sh1, sc1, g1, sh2, sc2, _ = [mod[i:i + 1, :] for i in range(6)]
    h = _rmsnorm(x, n1_ref[...]) * (1.0 + sc1) + sh1
    z = _dot(h.astype(BF16), win_ref[...])
    u, gate_b, gate_c, val = [z[:, i * dp:(i + 1) * dp] for i in range(4)]

    ubuf[POOL_HALO:POOL_HALO + tt, :] = u
    pos = lax.broadcasted_iota(I32, (tt, gw), 0) + t * tt
    pooled = []
    for g, w in enumerate(POOL_WINDOWS):
        cols = slice(g * gw, (g + 1) * gw)
        acc = u[:, cols]
        for j in range(1, w):
            acc = acc + ubuf[POOL_HALO - j:POOL_HALO - j + tt, cols]
        cnt = jnp.minimum(pos + 1, w).astype(F32)
        pooled.append(acc / cnt - u[:, cols])
    pool_in = jnp.concatenate(pooled, axis=-1)

    v = gate_c * val
    vbuf[CONV_HALO:CONV_HALO + tt, :] = v
    wc = wconv_ref[...]
    y = (wc[0:1, :] * vbuf[CONV_HALO - 2:CONV_HALO - 2 + tt, :]
         + wc[1:2, :] * vbuf[CONV_HALO - 1:CONV_HALO - 1 + tt, :]
         + wc[2:3, :] * v)
    conv_out = gate_b * y

    x1, h2, logits = _mix_tail(x, pool_in, conv_out, g1, sc2, sh2, n2_ref[...], wpool_ref,
                               pscale_ref[...], wout_ref[...], wr_ref[...], br_ref[...])
    x1_ref[...] = x1
    _store_token_tiles(h2_ref, h2)
    lg_ref[...] = logits

    ubuf[0:POOL_HALO, :] = ubuf[tt:tt + POOL_HALO, :]
    vbuf[0:CONV_HALO, :] = vbuf[tt:tt + CONV_HALO, :]

    @pl.when(t == pl.num_programs(1) - 1)
    def _():
        upool_ref[...] = ubuf[0:POOL_HALO, :]
        vconv_ref[...] = vbuf[0:CONV_HALO, :]


def _mixer_prompt(x, mod_p, n1, n2, w_in, w_pool, pscale, w_conv, w_out, w_r, b_r):
    b, t, d = x.shape
    dp = w_pool.shape[0] * w_pool.shape[1]
    tt = min(TOKEN_TILE, t)
    const2 = lambda i, j: (0, 0)
    const3 = lambda i, j: (0, 0, 0)
    return pl.pallas_call(
        _mixer_prompt_kernel,
        out_shape=(
            jax.ShapeDtypeStruct((b * t, d), F32),
            jax.ShapeDtypeStruct((b * t * SUBLANES, LANES), F32),
            jax.ShapeDtypeStruct((b * t, LANES), F32),
            jax.ShapeDtypeStruct((b, POOL_HALO, dp), F32),
            jax.ShapeDtypeStruct((b, CONV_HALO, dp), F32),
        ),
        grid=(b, t // tt),
        in_specs=[
            pl.BlockSpec((None, tt, d), lambda i, j: (i, j, 0)),
            pl.BlockSpec((None, 6, d), lambda i, j: (i, 0, 0)),
            pl.BlockSpec((1, d), const2),
            pl.BlockSpec((1, d), const2),
            pl.BlockSpec(w_in.shape, const2),
            pl.BlockSpec(w_pool.shape, const3),
            pl.BlockSpec((1, dp), const2),
            pl.BlockSpec(w_conv.shape, const2),
            pl.BlockSpec(w_out.shape, const2),
            pl.BlockSpec(w_r.shape, const2),
            pl.BlockSpec((1, LANES), const2),
        ],
        out_specs=(
            pl.BlockSpec((tt, d), lambda i, j: (i * (t // tt) + j, 0)),
            pl.BlockSpec((tt * SUBLANES, LANES), lambda i, j: (i * (t // tt) + j, 0)),
            pl.BlockSpec((tt, LANES), lambda i, j: (i * (t // tt) + j, 0)),
            pl.BlockSpec((None, POOL_HALO, dp), lambda i, j: (i, 0, 0)),
            pl.BlockSpec((None, CONV_HALO, dp), lambda i, j: (i, 0, 0)),
        ),
        scratch_shapes=[
            pltpu.VMEM((POOL_HALO + tt, dp), F32),
            pltpu.VMEM((CONV_HALO + tt, dp), F32),
        ],
        compiler_params=pltpu.CompilerParams(
            dimension_semantics=("arbitrary", "arbitrary"), vmem_limit_bytes=VMEM_LIMIT),
        name="mixer_prompt",
    )(x, mod_p, n1, n2, w_in, w_pool, pscale, w_conv, w_out, w_r, b_r)


def _mixer_sample_kernel(x_ref, mod_ref, pstate_ref, cstate_ref, n1_ref, n2_ref, win_ref, wpool_ref,
                         pscale_ref, wconv_ref, wout_ref, wr_ref, br_ref,
                         x1_ref, h2_ref, lg_ref, newp_ref, newc_ref, *, steps):
    nb = mod_ref.shape[0]
    d = x_ref.shape[1]
    dp = pstate_ref.shape[2]
    gw = dp // len(POOL_WINDOWS)
    n_hist = pstate_ref.shape[0]
    n_chist = cstate_ref.shape[0]

    x = x_ref[...]
    mod = mod_ref[...]
    rep = lambda a: jnp.concatenate([a] * steps, axis=0)
    sh1, sc1, g1, sh2, sc2, _ = [rep(mod[:, i * d:(i + 1) * d]) for i in range(6)]
    h = _rmsnorm(x, n1_ref[...]) * (1.0 + sc1) + sh1
    z = _dot(h.astype(BF16), win_ref[...])
    u, gate_b, gate_c, val = [z[:, i * dp:(i + 1) * dp] for i in range(4)]

    ext = [pstate_ref[i] for i in range(n_hist)] + [u[s * nb:(s + 1) * nb, :] for s in range(steps)]
    pooled_steps = []
    for s in range(steps):
        groups = []
        for g, w in enumerate(POOL_WINDOWS):
            cols = slice(g * gw, (g + 1) * gw)
            acc = ext[n_hist + s][:, cols]
            for j in range(1, w):
                acc = acc + ext[n_hist + s - j][:, cols]
            cnt = float(min(PAST_LEN + s + 1, w))
            groups.append(acc / cnt - ext[n_hist + s][:, cols])
        pooled_steps.append(jnp.concatenate(groups, axis=-1))
    pool_in = jnp.concatenate(pooled_steps, axis=0)

    v = gate_c * val
    vext = [cstate_ref[i] for i in range(n_chist)] + [v[s * nb:(s + 1) * nb, :] for s in range(steps)]
    wc = wconv_ref[...]
    y = jnp.concatenate(
        [wc[0:1, :] * vext[s] + wc[1:2, :] * vext[s + 1] + wc[2:3, :] * vext[s + 2] for s in range(steps)],
        axis=0)
    conv_out = gate_b * y

    x1, h2, logits = _mix_tail(x, pool_in, conv_out, g1, sc2, sh2, n2_ref[...], wpool_ref,
                               pscale_ref[...], wout_ref[...], wr_ref[...], br_ref[...])
    x1_ref[...] = x1
    _store_token_tiles(h2_ref, h2)
    lg_ref[...] = logits
    for i in range(n_hist):
        newp_ref[i] = ext[steps + i]
    for i in range(n_chist):
        newc_ref[i] = vext[steps + i]


def _mixer_sample(x_tm, mod_s, pstate_tm, cstate_tm, n1, n2, w_in, w_pool, pscale, w_conv, w_out, w_r, b_r,
                  steps):
    rows, d = x_tm.shape
    return pl.pallas_call(
        functools.partial(_mixer_sample_kernel, steps=steps),
        out_shape=(
            jax.ShapeDtypeStruct((rows, d), F32),
            jax.ShapeDtypeStruct((rows * SUBLANES, LANES), F32),
            jax.ShapeDtypeStruct((rows, LANES), F32),
            jax.ShapeDtypeStruct(pstate_tm.shape, F32),
            jax.ShapeDtypeStruct(cstate_tm.shape, F32),
        ),
        compiler_params=pltpu.CompilerParams(vmem_limit_bytes=VMEM_LIMIT),
        name="mixer_sample",
    )(x_tm, mod_s, pstate_tm, cstate_tm, n1, n2, w_in, w_pool, pscale, w_conv, w_out, w_r, b_r)


def _route_kernel(lg_ref, dest_ref, gate_ref, cnt_ref, counts, start):
    phase = pl.program_id(0)
    i = pl.program_id(1)
    tr = lg_ref.shape[0]

    @pl.when((phase == 0) & (i == 0))
    def _():
        counts[...] = jnp.zeros_like(counts)

    @pl.when((phase == 1) & (i == 0))
    def _():
        c = counts[...]
        hi = jnp.floor(c * (1.0 / 256.0))
        lo = c - hi * 256.0
        r = lax.broadcasted_iota(I32, (LANES, LANES), 0)
        col = lax.broadcasted_iota(I32, (LANES, LANES), 1)
        upper = jnp.where(r < col, 1.0, 0.0).astype(BF16)
        start[...] = 256.0 * _dot(hi.astype(BF16), upper) + _dot(lo.astype(BF16), upper)
        cnt_ref[...] = c
        counts[...] = jnp.zeros_like(counts)

    work = lg_ref[...]
    lane = lax.broadcasted_iota(I32, (tr, LANES), 1)
    top_v, onehots = [], []
    for _ in range(TOP_K):
        m = jnp.max(work, axis=1, keepdims=True)
        idx = jnp.min(jnp.where(work == m, lane, LANES), axis=1, keepdims=True)
        sel = lane == idx
        top_v.append(m)
        onehots.append(sel)
        work = jnp.where(sel, -jnp.inf, work)
    mask = jnp.where(onehots[0] | onehots[1] | onehots[2] | onehots[3], 1.0, 0.0)

    @pl.when(phase == 0)
    def _():
        counts[0:1, :] = counts[0:1, :] + jnp.sum(mask, axis=0, keepdims=True)

    @pl.when(phase == 1)
    def _():
        r = lax.broadcasted_iota(I32, (tr, tr), 0)
        col = lax.broadcasted_iota(I32, (tr, tr), 1)
        lower = jnp.where(col < r, 1.0, 0.0).astype(BF16)
        rank = _dot(lower, mask.astype(BF16)) + counts[0:1, :] + start[0:1, :]
        counts[0:1, :] = counts[0:1, :] + jnp.sum(mask, axis=0, keepdims=True)
        es = [jnp.exp(v - top_v[0]) for v in top_v]
        denom = es[0] + es[1] + es[2] + es[3]
        for k in range(TOP_K):
            d = jnp.sum(jnp.where(onehots[k], rank, 0.0), axis=1, keepdims=True)
            dest_ref[:, k:k + 1] = d.astype(I32)
            gate_ref[:, k:k + 1] = es[k] / denom


def _route(logits):
    n = logits.shape[0]
    tr = ROUTE_TILE
    return pl.pallas_call(
        _route_kernel,
        out_shape=(
            jax.ShapeDtypeStruct((n, TOP_K), I32),
            jax.ShapeDtypeStruct((n, TOP_K), F32),
            jax.ShapeDtypeStruct((8, LANES), F32),
        ),
        grid=(2, n // tr),
        in_specs=[pl.BlockSpec((tr, LANES), lambda p, i: (i, 0))],
        out_specs=(
            pl.BlockSpec((tr, TOP_K), lambda p, i: (i * p, 0)),
            pl.BlockSpec((tr, TOP_K), lambda p, i: (i * p, 0)),
            pl.BlockSpec((8, LANES), lambda p, i: (0, 0)),
        ),
        scratch_shapes=[pltpu.VMEM((8, LANES), F32), pltpu.VMEM((8, LANES), F32)],
        compiler_params=pltpu.CompilerParams(
            dimension_semantics=("arbitrary", "arbitrary"), vmem_limit_bytes=VMEM_LIMIT),
        name="route",
    )(logits)


def _group_metadata(counts, n_rows, tile):
    n_tiles = n_rows // tile
    n_steps = n_tiles + N_EXPERTS - 1
    ends = jnp.cumsum(counts)
    offs = jnp.concatenate([jnp.zeros((1,), I32), ends]).astype(I32)
    first_tile = offs[:-1] // tile
    last_tile = (ends - 1) // tile
    tiles_e = jnp.where(counts > 0, last_tile - first_tile + 1, 0)
    step_end = jnp.cumsum(tiles_e)
    step_start = step_end - tiles_e
    n_active = step_end[-1]
    s = jnp.minimum(jnp.arange(n_steps, dtype=I32), n_active - 1)
    owner = ((s[:, None] >= step_start[None, :]) & (s[:, None] < step_end[None, :])).astype(I32)
    gid = jnp.sum(owner * jnp.arange(N_EXPERTS, dtype=I32)[None, :], axis=1)
    tid = jnp.sum(owner * (first_tile - step_start)[None, :], axis=1) + s
    return gid, tid, offs, n_active.reshape(1).astype(I32), n_steps


def _dispatch_kernel(dest_ref, hp_ref, hs_ref, rows_ref, sem, *, n_prompt_tiles):
    i = pl.program_id(0)
    tm = hp_ref.shape[0] // SUBLANES

    def scatter(src_ref):
        def issue(t, carry):
            src = src_ref.at[pl.ds(pl.multiple_of(t * SUBLANES, SUBLANES), SUBLANES)]
            for k in range(TOP_K):
                d = pl.multiple_of(dest_ref[t * TOP_K + k] * SUBLANES, SUBLANES)
                pltpu.make_async_copy(src, rows_ref.at[pl.ds(d, SUBLANES)], sem).start(priority=k % 2)
            return carry
        lax.fori_loop(0, tm, issue, 0)
        for _ in range(TOP_K):
            pltpu.make_async_copy(src_ref, rows_ref.at[pl.ds(0, tm * SUBLANES)], sem).wait()

    @pl.when(i < n_prompt_tiles)
    def _():
        scatter(hp_ref)

    @pl.when(i >= n_prompt_tiles)
    def _():
        scatter(hs_ref)


def _dispatch(dest_flat, h2_p, h2_s):
    n_p = h2_p.shape[0] // SUBLANES
    n_s = h2_s.shape[0] // SUBLANES
    tm = DISPATCH_TILE
    npt, nst = n_p // tm, n_s // tm
    return pl.pallas_call(
        functools.partial(_dispatch_kernel, n_prompt_tiles=npt),
        out_shape=jax.ShapeDtypeStruct(((n_p + n_s) * TOP_K * SUBLANES, LANES), h2_p.dtype),
        grid=(npt + nst,),
        in_specs=[
            pl.BlockSpec((tm * TOP_K,), lambda i: (i,), memory_space=pltpu.SMEM),
            pl.BlockSpec((tm * SUBLANES, LANES), lambda i: (jnp.minimum(i, npt - 1), 0)),
            pl.BlockSpec((tm * SUBLANES, LANES), lambda i: (jnp.maximum(i - npt, 0), 0)),
        ],
        out_specs=pl.BlockSpec(memory_space=pl.ANY),
        scratch_shapes=[pltpu.SemaphoreType.DMA],
        compiler_params=pltpu.CompilerParams(
            dimension_semantics=("arbitrary",), vmem_limit_bytes=VMEM_LIMIT),
        name="dispatch",
    )(dest_flat, h2_p, h2_s)


def _experts_kernel(gid_ref, tid_ref, offs_ref, nact_ref,
                    rows_ref, wg_ref, bg_ref, wu_ref, bu_ref, wd_ref, bd_ref, out_ref,
                    wg_b, wu_b, wd_b):
    s = pl.program_id(0)
    tm = out_ref.shape[0]

    @pl.when(s < nact_ref[0])
    def _():
        e = gid_ref[s]
        m = tid_ref[s]

        @pl.when((s == 0) | (gid_ref[jnp.maximum(s - 1, 0)] != e))
        def _():
            wg_b[...] = wg_ref[...].astype(BF16)
            wu_b[...] = wu_ref[...].astype(BF16)
            wd_b[...] = wd_ref[...].astype(BF16)

        x = _load_token_tiles(rows_ref).astype(BF16)
        g = _dot(x, wg_b[...]) + bg_ref[...]
        u = _dot(x, wu_b[...]) + bu_ref[...]
        g = jnp.minimum(g, SWIGLU_LIMIT)
        u = jnp.clip(u, -SWIGLU_LIMIT, SWIGLU_LIMIT)
        glu = g * jax.nn.sigmoid(SWIGLU_ALPHA * g)
        y = _dot(((u + 1.0) * glu).astype(BF16), wd_b[...]) + bd_ref[...]

        row = m * tm + lax.broadcasted_iota(I32, y.shape, 0)
        pltpu.store(out_ref, y, mask=(row >= offs_ref[e]) & (row < offs_ref[e + 1]))


def _experts(gid, tid, offs, nact, n_steps, rows, wg, bg, wu, bu, wd, bd):
    ne, d, f = wg.shape
    r = rows.shape[0] // SUBLANES
    tm = EXPERT_TILE
    grid_spec = pltpu.PrefetchScalarGridSpec(
        num_scalar_prefetch=4,
        grid=(n_steps,),
        in_specs=[
            pl.BlockSpec((tm * SUBLANES, LANES), lambda s, gid, tid, offs, nact: (tid[s], 0)),
            pl.BlockSpec((None, d, f), lambda s, gid, tid, offs, nact: (gid[s], 0, 0)),
            pl.BlockSpec((None, 1, f), lambda s, gid, tid, offs, nact: (gid[s], 0, 0)),
            pl.BlockSpec((None, d, f), lambda s, gid, tid, offs, nact: (gid[s], 0, 0)),
            pl.BlockSpec((None, 1, f), lambda s, gid, tid, offs, nact: (gid[s], 0, 0)),
            pl.BlockSpec((None, f, d), lambda s, gid, tid, offs, nact: (gid[s], 0, 0)),
            pl.BlockSpec((None, 1, d), lambda s, gid, tid, offs, nact: (gid[s], 0, 0)),
        ],
        out_specs=pl.BlockSpec((tm, d), lambda s, gid, tid, offs, nact: (tid[s], 0)),
        scratch_shapes=[pltpu.VMEM((d, f), BF16), pltpu.VMEM((d, f), BF16), pltpu.VMEM((f, d), BF16)],
    )
    return pl.pallas_call(
        _experts_kernel,
        out_shape=jax.ShapeDtypeStruct((r, d), F32),
        grid_spec=grid_spec,
        compiler_params=pltpu.CompilerParams(
            dimension_semantics=("arbitrary",), vmem_limit_bytes=VMEM_LIMIT),
        name="experts",
    )(gid, tid, offs, nact, rows, wg, bg.reshape(ne, 1, f), wu, bu.reshape(ne, 1, f), wd,
      bd.reshape(ne, 1, d))


def _combine_kernel(dest_ref, dnext_ref, gate_ref, x1p_ref, x1s_ref, modp_ref, g2s_ref, fn_ref, y_hbm,
                    outp_ref, outs_ref, gbuf, sems, *, n_prompt_tiles):
    i = pl.program_id(0)
    n = pl.num_programs(0)
    tm = x1p_ref.shape[0]
    slot = lax.rem(i, 2)

    def start_gather(idx_ref, into):
        def issue(t, carry):
            for k in range(TOP_K):
                d = idx_ref[t * TOP_K + k]
                pltpu.make_async_copy(y_hbm.at[pl.ds(d, 1)], gbuf.at[into, k, pl.ds(t, 1)],
                                      sems.at[into]).start(priority=k % 2)
            return carry
        lax.fori_loop(0, tm, issue, 0)

    @pl.when(i == 0)
    def _():
        start_gather(dest_ref, 0)

    @pl.when(i + 1 < n)
    def _():
        start_gather(dnext_ref, 1 - slot)

    for k in range(TOP_K):
        pltpu.make_async_copy(y_hbm.at[pl.ds(0, tm)], gbuf.at[slot, k], sems.at[slot]).wait()

    gates = gate_ref[...]
    ffn = gates[:, 0:1] * gbuf[slot, 0]
    for k in range(1, TOP_K):
        ffn = ffn + gates[:, k:k + 1] * gbuf[slot, k]

    @pl.when(i < n_prompt_tiles)
    def _():
        g2 = modp_ref[...][5:6, :]
        outp_ref[...] = _rmsnorm(x1p_ref[...] + g2 * ffn, fn_ref[...])

    @pl.when(i >= n_prompt_tiles)
    def _():
        outs_ref[...] = _rmsnorm(x1s_ref[...] + g2s_ref[...] * ffn, fn_ref[...])


def _combine(dest_flat, gates, x1_p, x1_s, mod_p, g2_s, final_norm, y_rows, tokens_per_seq):
    n_p, d = x1_p.shape
    n_s = x1_s.shape[0]
    tm = COMBINE_TILE
    npt, nst = n_p // tm, n_s // tm
    tiles_per_seq = tokens_per_seq // tm
    pmap = lambda i: (jnp.minimum(i, npt - 1), 0)
    smap = lambda i: (jnp.maximum(i - npt, 0), 0)
    return pl.pallas_call(
        functools.partial(_combine_kernel, n_prompt_tiles=npt),
        out_shape=(jax.ShapeDtypeStruct((n_p, d), F32), jax.ShapeDtypeStruct((n_s, d), F32)),
        grid=(npt + nst,),
        in_specs=[
            pl.BlockSpec((tm * TOP_K,), lambda i: (i,), memory_space=pltpu.SMEM),
            pl.BlockSpec((tm * TOP_K,), lambda i: (jnp.minimum(i + 1, npt + nst - 1),),
                         memory_space=pltpu.SMEM),
            pl.BlockSpec((tm, TOP_K), lambda i: (i, 0)),
            pl.BlockSpec((tm, d), pmap),
            pl.BlockSpec((tm, d), smap),
            pl.BlockSpec((None, 6, d), lambda i: (jnp.minimum(i, npt - 1) // tiles_per_seq, 0, 0)),
            pl.BlockSpec((tm, d), smap),
            pl.BlockSpec((1, d), lambda i: (0, 0)),
            pl.BlockSpec(memory_space=pl.ANY),
        ],
        out_specs=(pl.BlockSpec((tm, d), pmap), pl.BlockSpec((tm, d), smap)),
        scratch_shapes=[pltpu.VMEM((2, TOP_K, tm, d), F32), pltpu.SemaphoreType.DMA((2,))],
        compiler_params=pltpu.CompilerParams(
            dimension_semantics=("arbitrary",), vmem_limit_bytes=VMEM_LIMIT),
        name="combine",
    )(dest_flat, dest_flat, gates, x1_p, x1_s, mod_p, g2_s, final_norm, y_rows)


def kernel(x_prompt, x_sample, state_pool, state_conv, c_prompt, c_sample, norm1, norm2, w_ada, b_ada,
           w_in, w_pool, pool_scale, w_conv, w_out, w_router, b_router, w_gate, b_gate, w_up, b_up,
           w_down, b_down, final_norm):
    depth = norm1.shape[0]
    assert depth == 1, "single-layer step"
    bp, tp, d = x_prompt.shape
    bs, ts, _ = x_sample.shape
    dp = state_pool.shape[-1]
    n_hist = state_pool.shape[2]
    n_chist = state_conv.shape[2]
    n_p, n_s = bp * tp, bs * ts
    assert d == SUBLANES * LANES, "token-tile layout assumes one vreg tile per token row"
    assert tp % TOKEN_TILE == 0 and tp % COMBINE_TILE == 0 and n_s % COMBINE_TILE == 0
    assert n_p % DISPATCH_TILE == 0 and n_s % DISPATCH_TILE == 0
    assert (n_p + n_s) % ROUTE_TILE == 0 and ((n_p + n_s) * TOP_K) % EXPERT_TILE == 0

    l = 0
    n1 = norm1[l].reshape(1, d)
    n2 = norm2[l].reshape(1, d)
    w_in_b = w_in[l].astype(BF16)
    w_pool_b = w_pool[l].astype(BF16)
    w_out_b = w_out[l].astype(BF16)
    pscale = pool_scale[l].reshape(1, dp)
    w_r = jnp.pad(w_router[l], ((0, 0), (0, LANES - N_EXPERTS))).astype(BF16)
    b_r = jnp.pad(b_router[l], (0, LANES - N_EXPERTS), constant_values=NEG_BIG).reshape(1, LANES)

    mod = _adaln(jnp.concatenate([c_prompt, c_sample], axis=0), w_ada[l], b_ada[l])
    mod_p = mod[:bp].reshape(bp, 6, d)
    mod_s = mod[bp:]

    x1_p, h2_p, lg_p, u_tail, v_tail = _mixer_prompt(
        x_prompt, mod_p, n1, n2, w_in_b, w_pool_b, pscale, w_conv[l], w_out_b, w_r, b_r)

    xs_tm = jnp.transpose(x_sample, (1, 0, 2)).reshape(n_s, d)
    ps_tm = jnp.transpose(state_pool[l], (1, 0, 2))
    cs_tm = jnp.transpose(state_conv[l], (1, 0, 2))
    x1_s, h2_s, lg_s, newp_tm, newc_tm = _mixer_sample(
        xs_tm, mod_s, ps_tm, cs_tm, n1, n2, w_in_b, w_pool_b, pscale, w_conv[l], w_out_b, w_r, b_r, ts)

    dest, gates, counts_f = _route(jnp.concatenate([lg_p, lg_s], axis=0))
    counts = counts_f[0, :N_EXPERTS].astype(I32)
    dest_flat = dest.reshape(-1)
    n_rows = (n_p + n_s) * TOP_K
    gid, tid, offs, nact, n_steps = _group_metadata(counts, n_rows, EXPERT_TILE)

    rows = _dispatch(dest_flat, h2_p, h2_s)
    y_rows = _experts(gid, tid, offs, nact, n_steps, rows,
                      w_gate[l], b_gate[l], w_up[l], b_up[l], w_down[l], b_down[l])

    g2_s = jnp.tile(mod_s[:, 5 * d:], (ts, 1))
    y_p, y_s = _combine(dest_flat, gates, x1_p, x1_s, mod_p, g2_s, final_norm.reshape(1, d), y_rows, tp)

    y_prompt = y_p.reshape(bp, tp, d)
    y_sample = jnp.transpose(y_s.reshape(ts, bs, d), (1, 0, 2))
    new_pool_prompt = u_tail[:, POOL_HALO - n_hist:, :][None]
    new_conv_prompt = v_tail[:, CONV_HALO - n_chist:, :][None]
    new_pool_sample = jnp.transpose(newp_tm, (1, 0, 2))[None]
    new_conv_sample = jnp.transpose(newc_tm, (1, 0, 2))[None]
    return (y_prompt, y_sample, new_pool_prompt, new_conv_prompt, new_pool_sample, new_conv_sample)
```

```python
import functools

import jax
import jax.numpy as jnp
from jax import lax
from jax.experimental import pallas as pl
from jax.experimental.pallas import tpu as pltpu

F32 = jnp.float32
BF16 = jnp.bfloat16
I32 = jnp.int32

POOL_WINDOWS = (2, 4, 8, 16)
POOL_HALO = 16
CONV_TAPS = 3
CONV_HALO = 8
N_EXPERTS = 32
TOP_K = 4
SWIGLU_LIMIT = 7.0
SWIGLU_ALPHA = 1.702
EPS = 1e-5
PAST_LEN = 16384

LANES = 128
SUBLANES = 8
NEG_BIG = -1e30

TOKEN_TILE = 512
ROUTE_TILE = 512
DISPATCH_TILE = 512
COMBINE_TILE = 256
EXPERT_TILE = 512
VMEM_LIMIT = 56 * 1024 * 1024


def _rmsnorm(x, g):
    ms = jnp.mean(x * x, axis=-1, keepdims=True)
    return x * lax.rsqrt(ms + EPS) * g


def _dot(a, b):
    return jnp.dot(a, b, preferred_element_type=F32)


def _store_token_tiles(ref, val):
    rows = val.shape[0]
    for c in range(SUBLANES):
        ref[pl.ds(c, rows, stride=SUBLANES), :] = val[:, c * LANES:(c + 1) * LANES]


def _load_token_tiles(ref):
    rows = ref.shape[0] // SUBLANES
    return jnp.concatenate([ref[pl.ds(c, rows, stride=SUBLANES), :] for c in range(SUBLANES)], axis=-1)


def _adaln_kernel(c_ref, w_ref, b_ref, o_ref):
    c = c_ref[...]
    s = c * jax.nn.sigmoid(c)
    o_ref[...] = _dot(s.astype(BF16), w_ref[...].astype(BF16)) + b_ref[...]


def _adaln(c, w_ada, b_ada):
    rows, d = c.shape
    n = w_ada.shape[1]
    tn = 1024
    return pl.pallas_call(
        _adaln_kernel,
        out_shape=jax.ShapeDtypeStruct((rows, n), F32),
        grid=(n // tn,),
        in_specs=[
            pl.BlockSpec((rows, d), lambda j: (0, 0)),
            pl.BlockSpec((d, tn), lambda j: (0, j)),
            pl.BlockSpec((1, tn), lambda j: (0, j)),
        ],
        out_specs=pl.BlockSpec((rows, tn), lambda j: (0, j)),
        compiler_params=pltpu.CompilerParams(
            dimension_semantics=("arbitrary",), vmem_limit_bytes=VMEM_LIMIT),
        name="adaln",
    )(c, w_ada, b_ada.reshape(1, n))


def _mix_tail(x, pool_in, conv_out, g1, sc2, sh2, n2, wpool_ref, pscale, wout, wr, br):
    gw = pool_in.shape[1] // len(POOL_WINDOWS)
    mixed = [_dot(pool_in[:, g * gw:(g + 1) * gw].astype(BF16), wpool_ref[g])
             for g in range(len(POOL_WINDOWS))]
    pool_out = jnp.concatenate(mixed, axis=-1) * pscale
    mix_in = jnp.concatenate([pool_out, conv_out], axis=-1).astype(BF16)
    x1 = x + g1 * _dot(mix_in, wout)
    h2 = _rmsnorm(x1, n2) * (1.0 + sc2) + sh2
    logits = _dot(h2.astype(BF16), wr) + br
    return x1, h2, logits


def _mixer_prompt_kernel(x_ref, mod_ref, n1_ref, n2_ref, win_ref, wpool_ref, pscale_ref, wconv_ref,
                         wout_ref, wr_ref, br_ref,
                         x1_ref, h2_ref, lg_ref, upool_ref, vconv_ref, ubuf, vbuf):
    tt = x_ref.shape[0]
    dp = ubuf.shape[1]
    gw = dp // len(POOL_WINDOWS)
    t = pl.program_id(1)

    @pl.when(t == 0)
    def _():
        ubuf[0:POOL_HALO, :] = jnp.zeros((POOL_HALO, dp), F32)
        vbuf[0:CONV_HALO, :] = jnp.zeros((CONV_HALO, dp), F32)

    x = x_ref[...]
    mod = mod_ref[...]
    sh1, sc1, g1, sh2, sc2, _ = [mod[i:i + 1, :] for i in range(6)]
    h = _rmsnorm(x, n1_ref[...]) * (1.0 + sc1) + sh1
    z = _dot(h.astype(BF16), win_ref[...])
    u, gate_b, gate_c, val = [z[:, i * dp:(i + 1) * dp] for i in range(4)]

    ubuf[POOL_HALO:POOL_HALO + tt, :] = u
    pos = lax.broadcasted_iota(I32, (tt, gw), 0) + t * tt
    pooled = []
    for g, w in enumerate(POOL_WINDOWS):
        cols = slice(g * gw, (g + 1) * gw)
        acc = u[:, cols]
        for j in range(1, w):
            acc = acc + ubuf[POOL_HALO - j:POOL_HALO - j + tt, cols]
        cnt = jnp.minimum(pos + 1, w).astype(F32)
        pooled.append(acc / cnt - u[:, cols])
    pool_in = jnp.concatenate(pooled, axis=-1)

    v = gate_c * val
    vbuf[CONV_HALO:CONV_HALO + tt, :] = v
    wc = wconv_ref[...]
    y = (wc[0:1, :] * vbuf[CONV_HALO - 2:CONV_HALO - 2 + tt, :]
         + wc[1:2, :] * vbuf[CONV_HALO - 1:CONV_HALO - 1 + tt, :]
         + wc[2:3, :] * v)
    conv_out = gate_b * y

    x1, h2, logits = _mix_tail(x, pool_in, conv_out, g1, sc2, sh2, n2_ref[...], wpool_ref,
                               pscale_ref[...], wout_ref[...], wr_ref[...], br_ref[...])
    x1_ref[...] = x1
    _store_token_tiles(h2_ref, h2)
    lg_ref[...] = logits

    ubuf[0:POOL_HALO, :] = ubuf[tt:tt + POOL_HALO, :]
    vbuf[0:CONV_HALO, :] = vbuf[tt:tt + CONV_HALO, :]

    @pl.when(t == pl.num_programs(1) - 1)
    def _():
        upool_ref[...] = ubuf[0:POOL_HALO, :]
        vconv_ref[...] = vbuf[0:CONV_HALO, :]


def _mixer_prompt(x, mod_p, n1, n2, w_in, w_pool, pscale, w_conv, w_out, w_r, b_r):
    b, t, d = x.shape
    dp = w_pool.shape[0] * w_pool.shape[1]
    tt = min(TOKEN_TILE, t)
    const2 = lambda i, j: (0, 0)
    const3 = lambda i, j: (0, 0, 0)
    return pl.pallas_call(
        _mixer_prompt_kernel,
        out_shape=(
            jax.ShapeDtypeStruct((b * t, d), F32),
            jax.ShapeDtypeStruct((b * t * SUBLANES, LANES), F32),
            jax.ShapeDtypeStruct((b * t, LANES), F32),
            jax.ShapeDtypeStruct((b, POOL_HALO, dp), F32),
            jax.ShapeDtypeStruct((b, CONV_HALO, dp), F32),
        ),
        grid=(b, t // tt),
        in_specs=[
            pl.BlockSpec((None, tt, d), lambda i, j: (i, j, 0)),
            pl.BlockSpec((None, 6, d), lambda i, j: (i, 0, 0)),
            pl.BlockSpec((1, d), const2),
            pl.BlockSpec((1, d), const2),
            pl.BlockSpec(w_in.shape, const2),
            pl.BlockSpec(w_pool.shape, const3),
            pl.BlockSpec((1, dp), const2),
            pl.BlockSpec(w_conv.shape, const2),
            pl.BlockSpec(w_out.shape, const2),
            pl.BlockSpec(w_r.shape, const2),
            pl.BlockSpec((1, LANES), const2),
        ],
        out_specs=(
            pl.BlockSpec((tt, d), lambda i, j: (i * (t // tt) + j, 0)),
            pl.BlockSpec((tt * SUBLANES, LANES), lambda i, j: (i * (t // tt) + j, 0)),
            pl.BlockSpec((tt, LANES), lambda i, j: (i * (t // tt) + j, 0)),
            pl.BlockSpec((None, POOL_HALO, dp), lambda i, j: (i, 0, 0)),
            pl.BlockSpec((None, CONV_HALO, dp), lambda i, j: (i, 0, 0)),
        ),
        scratch_shapes=[
            pltpu.VMEM((POOL_HALO + tt, dp), F32),
            pltpu.VMEM((CONV_HALO + tt, dp), F32),
        ],
        compiler_params=pltpu.CompilerParams(
            dimension_semantics=("arbitrary", "arbitrary"), vmem_limit_bytes=VMEM_LIMIT),
        name="mixer_prompt",
    )(x, mod_p, n1, n2, w_in, w_pool, pscale, w_conv, w_out, w_r, b_r)


def _mixer_sample_kernel(x_ref, mod_ref, pstate_ref, cstate_ref, n1_ref, n2_ref, win_ref, wpool_ref,
                         pscale_ref, wconv_ref, wout_ref, wr_ref, br_ref,
                         x1_ref, h2_ref, lg_ref, newp_ref, newc_ref, *, steps):
    nb = mod_ref.shape[0]
    d = x_ref.shape[1]
    dp = pstate_ref.shape[2]
    gw = dp // len(POOL_WINDOWS)
    n_hist = pstate_ref.shape[0]
    n_chist = cstate_ref.shape[0]

    x = x_ref[...]
    mod = mod_ref[...]
    rep = lambda a: jnp.concatenate([a] * steps, axis=0)
    sh1, sc1, g1, sh2, sc2, _ = [rep(mod[:, i * d:(i + 1) * d]) for i in range(6)]
    h = _rmsnorm(x, n1_ref[...]) * (1.0 + sc1) + sh1
    z = _dot(h.astype(BF16), win_ref[...])
    u, gate_b, gate_c, val = [z[:, i * dp:(i + 1) * dp] for i in range(4)]

    ext = [pstate_ref[i] for i in range(n_hist)] + [u[s * nb:(s + 1) * nb, :] for s in range(steps)]
    pooled_steps = []
    for s in range(steps):
        groups = []
        for g, w in enumerate(POOL_WINDOWS):
            cols = slice(g * gw, (g + 1) * gw)
            acc = ext[n_hist + s][:, cols]
            for j in range(1, w):
                acc = acc + ext[n_hist + s - j][:, cols]
            cnt = float(min(PAST_LEN + s + 1, w))
            groups.append(acc / cnt - ext[n_hist + s][:, cols])
        pooled_steps.append(jnp.concatenate(groups, axis=-1))
    pool_in = jnp.concatenate(pooled_steps, axis=0)

    v = gate_c * val
    vext = [cstate_ref[i] for i in range(n_chist)] + [v[s * nb:(s + 1) * nb, :] for s in range(steps)]
    wc = wconv_ref[...]
    y = jnp.concatenate(
        [wc[0:1, :] * vext[s] + wc[1:2, :] * vext[s + 1] + wc[2:3, :] * vext[s + 2] for s in range(steps)],
        axis=0)
    conv_out = gate_b * y

    x1, h2, logits = _mix_tail(x, pool_in, conv_out, g1, sc2, sh2, n2_ref[...], wpool_ref,
                               pscale_ref[...], wout_ref[...], wr_ref[...], br_ref[...])
    x1_ref[...] = x1
    _store_token_tiles(h2_ref, h2)
    lg_ref[...] = logits
    for i in range(n_hist):
        newp_ref[i] = ext[steps + i]
    for i in range(n_chist):
        newc_ref[i] = vext[steps + i]


def _mixer_sample(x_tm, mod_s, pstate_tm, cstate_tm, n1, n2, w_in, w_pool, pscale, w_conv, w_out, w_r, b_r,
                  steps):
    rows, d = x_tm.shape
    return pl.pallas_call(
        functools.partial(_mixer_sample_kernel, steps=steps),
        out_shape=(
            jax.ShapeDtypeStruct((rows, d), F32),
            jax.ShapeDtypeStruct((rows * SUBLANES, LANES), F32),
            jax.ShapeDtypeStruct((rows, LANES), F32),
            jax.ShapeDtypeStruct(pstate_tm.shape, F32),
            jax.ShapeDtypeStruct(cstate_tm.shape, F32),
        ),
        compiler_params=pltpu.CompilerParams(vmem_limit_bytes=VMEM_LIMIT),
        name="mixer_sample",
    )(x_tm, mod_s, pstate_tm, cstate_tm, n1, n2, w_in, w_pool, pscale, w_conv, w_out, w_r, b_r)


def _route_kernel(lg_ref, dest_ref, gate_ref, cnt_ref, counts, start):
    phase = pl.program_id(0)
    i = pl.program_id(1)
    tr = lg_ref.shape[0]

    @pl.when((phase == 0) & (i == 0))
    def _():
        counts[...] = jnp.zeros_like(counts)

    @pl.when((phase == 1) & (i == 0))
    def _():
        c = counts[...]
        hi = jnp.floor(c * (1.0 / 256.0))
        lo = c - hi * 256.0
        r = lax.broadcasted_iota(I32, (LANES, LANES), 0)
        col = lax.broadcasted_iota(I32, (LANES, LANES), 1)
        upper = jnp.where(r < col, 1.0, 0.0).astype(BF16)
        start[...] = 256.0 * _dot(hi.astype(BF16), upper) + _dot(lo.astype(BF16), upper)
        cnt_ref[...] = c
        counts[...] = jnp.zeros_like(counts)

    work = lg_ref[...]
    lane = lax.broadcasted_iota(I32, (tr, LANES), 1)
    top_v, onehots = [], []
    for _ in range(TOP_K):
        m = jnp.max(work, axis=1, keepdims=True)
        idx = jnp.min(jnp.where(work == m, lane, LANES), axis=1, keepdims=True)
        sel = lane == idx
        top_v.append(m)
        onehots.append(sel)
        work = jnp.where(sel, -jnp.inf, work)
    mask = jnp.where(onehots[0] | onehots[1] | onehots[2] | onehots[3], 1.0, 0.0)

    @pl.when(phase == 0)
    def _():
        counts[0:1, :] = counts[0:1, :] + jnp.sum(mask, axis=0, keepdims=True)

    @pl.when(phase == 1)
    def _():
        r = lax.broadcasted_iota(I32, (tr, tr), 0)
        col = lax.broadcasted_iota(I32, (tr, tr), 1)
        lower = jnp.where(col < r, 1.0, 0.0).astype(BF16)
        rank = _dot(lower, mask.astype(BF16)) + counts[0:1, :] + start[0:1, :]
        counts[0:1, :] = counts[0:1, :] + jnp.sum(mask, axis=0, keepdims=True)
        es = [jnp.exp(v - top_v[0]) for v in top_v]
        denom = es[0] + es[1] + es[2] + es[3]
        for k in range(TOP_K):
            d = jnp.sum(jnp.where(onehots[k], rank, 0.0), axis=1, keepdims=True)
            dest_ref[:, k:k + 1] = d.astype(I32)
            gate_ref[:, k:k + 1] = es[k] / denom


def _route(logits):
    n = logits.shape[0]
    tr = ROUTE_TILE
    return pl.pallas_call(
        _route_kernel,
        out_shape=(
            jax.ShapeDtypeStruct((n, TOP_K), I32),
            jax.ShapeDtypeStruct((n, TOP_K), F32),
            jax.ShapeDtypeStruct((8, LANES), F32),
        ),
        grid=(2, n // tr),
        in_specs=[pl.BlockSpec((tr, LANES), lambda p, i: (i, 0))],
        out_specs=(
            pl.BlockSpec((tr, TOP_K), lambda p, i: (i * p, 0)),
            pl.BlockSpec((tr, TOP_K), lambda p, i: (i * p, 0)),
            pl.BlockSpec((8, LANES), lambda p, i: (0, 0)),
        ),
        scratch_shapes=[pltpu.VMEM((8, LANES), F32), pltpu.VMEM((8, LANES), F32)],
        compiler_params=pltpu.CompilerParams(
            dimension_semantics=("arbitrary", "arbitrary"), vmem_limit_bytes=VMEM_LIMIT),
        name="route",
    )(logits)


def _group_metadata(counts, n_rows, tile):
    n_tiles = n_rows // tile
    n_steps = n_tiles + N_EXPERTS - 1
    ends = jnp.cumsum(counts)
    offs = jnp.concatenate([jnp.zeros((1,), I32), ends]).astype(I32)
    first_tile = offs[:-1] // tile
    last_tile = (ends - 1) // tile
    tiles_e = jnp.where(counts > 0, last_tile - first_tile + 1, 0)
    step_end = jnp.cumsum(tiles_e)
    step_start = step_end - tiles_e
    n_active = step_end[-1]
    s = jnp.minimum(jnp.arange(n_steps, dtype=I32), n_active - 1)
    owner = ((s[:, None] >= step_start[None, :]) & (s[:, None] < step_end[None, :])).astype(I32)
    gid = jnp.sum(owner * jnp.arange(N_EXPERTS, dtype=I32)[None, :], axis=1)
    tid = jnp.sum(owner * (first_tile - step_start)[None, :], axis=1) + s
    ids = jnp.arange(N_EXPERTS, dtype=I32)
    later = (ids[None, :] > ids[:, None]) & (counts[None, :] > 0)
    next_e = jnp.min(jnp.where(later, ids[None, :], N_EXPERTS), axis=1)
    next_e = jnp.where(next_e == N_EXPERTS, -1, next_e)
    nxt = jnp.sum(owner * next_e[None, :], axis=1)
    return gid, tid, nxt, offs, n_active.reshape(1).astype(I32), n_steps


def _dispatch_kernel(dest_ref, hp_ref, hs_ref, rows_ref, sem, *, n_prompt_tiles):
    i = pl.program_id(0)
    tm = hp_ref.shape[0] // SUBLANES

    def scatter(src_ref):
        def issue(t, carry):
            src = src_ref.at[pl.ds(pl.multiple_of(t * SUBLANES, SUBLANES), SUBLANES)]
            for k in range(TOP_K):
                d = pl.multiple_of(dest_ref[t * TOP_K + k] * SUBLANES, SUBLANES)
                pltpu.make_async_copy(src, rows_ref.at[pl.ds(d, SUBLANES)], sem).start(priority=k % 2)
            return carry
        lax.fori_loop(0, tm, issue, 0)
        for _ in range(TOP_K):
            pltpu.make_async_copy(src_ref, rows_ref.at[pl.ds(0, tm * SUBLANES)], sem).wait()

    @pl.when(i < n_prompt_tiles)
    def _():
        scatter(hp_ref)

    @pl.when(i >= n_prompt_tiles)
    def _():
        scatter(hs_ref)


def _dispatch(dest_flat, h2_p, h2_s):
    n_p = h2_p.shape[0] // SUBLANES
    n_s = h2_s.shape[0] // SUBLANES
    tm = DISPATCH_TILE
    npt, nst = n_p // tm, n_s // tm
    return pl.pallas_call(
        functools.partial(_dispatch_kernel, n_prompt_tiles=npt),
        out_shape=jax.ShapeDtypeStruct(((n_p + n_s) * TOP_K * SUBLANES, LANES), h2_p.dtype),
        grid=(npt + nst,),
        in_specs=[
            pl.BlockSpec((tm * TOP_K,), lambda i: (i,), memory_space=pltpu.SMEM),
            pl.BlockSpec((tm * SUBLANES, LANES), lambda i: (jnp.minimum(i, npt - 1), 0)),
            pl.BlockSpec((tm * SUBLANES, LANES), lambda i: (jnp.maximum(i - npt, 0), 0)),
        ],
        out_specs=pl.BlockSpec(memory_space=pl.ANY),
        scratch_shapes=[pltpu.SemaphoreType.DMA],
        compiler_params=pltpu.CompilerParams(
            dimension_semantics=("arbitrary",), vmem_limit_bytes=VMEM_LIMIT),
        name="dispatch",
    )(dest_flat, h2_p, h2_s)


def _experts_kernel(gid_ref, tid_ref, nxt_ref, offs_ref, nact_ref,
                    rows_ref, bg_ref, bu_ref, bd_ref, wg_hbm, wu_hbm, wd_hbm, out_ref,
                    wg_b, wu_b, wd_b, wg_f, wu_f, wd_f, relay, sems):
    s = pl.program_id(0)
    tm = rows_ref.shape[0] // SUBLANES
    landing = ((wg_hbm, wg_f, wg_b), (wu_hbm, wu_f, wu_b), (wd_hbm, wd_f, wd_b))

    def fetch(e):
        for j, (hbm, land, _) in enumerate(landing):
            pltpu.make_async_copy(hbm.at[e], land, sems.at[j]).start()

    @pl.when(s < nact_ref[0])
    def _():
        e = gid_ref[s]
        m = tid_ref[s]

        @pl.when(s == 0)
        def _():
            fetch(e)

        @pl.when((s == 0) | (gid_ref[jnp.maximum(s - 1, 0)] != e))
        def _():
            for j, (hbm, land, half) in enumerate(landing):
                pltpu.make_async_copy(hbm.at[e], land, sems.at[j]).wait()
                half[...] = land[...].astype(BF16)

            @pl.when(nxt_ref[s] >= 0)
            def _():
                fetch(nxt_ref[s])

        x = _load_token_tiles(rows_ref).astype(BF16)
        g = _dot(x, wg_b[...]) + bg_ref[...]
        u = _dot(x, wu_b[...]) + bu_ref[...]
        g = jnp.minimum(g, SWIGLU_LIMIT)
        u = jnp.clip(u, -SWIGLU_LIMIT, SWIGLU_LIMIT)
        glu = g * jax.nn.sigmoid(SWIGLU_ALPHA * g)
        y = _dot(((u + 1.0) * glu).astype(BF16), wd_b[...]) + bd_ref[...]

        _store_token_tiles(relay, y)
        row = m * tm + lax.shift_right_logical(lax.broadcasted_iota(I32, relay.shape, 0), 3)
        pltpu.store(out_ref, relay[...], mask=(row >= offs_ref[e]) & (row < offs_ref[e + 1]))


def _experts(gid, tid, nxt, offs, nact, n_steps, rows, wg, bg, wu, bu, wd, bd):
    ne, d, f = wg.shape
    tm = EXPERT_TILE
    tile_map = lambda s, gid, tid, nxt, offs, nact: (tid[s], 0)
    bias_map = lambda s, gid, tid, nxt, offs, nact: (gid[s], 0, 0)
    grid_spec = pltpu.PrefetchScalarGridSpec(
        num_scalar_prefetch=5,
        grid=(n_steps,),
        in_specs=[
            pl.BlockSpec((tm * SUBLANES, LANES), tile_map),
            pl.BlockSpec((None, 1, f), bias_map),
            pl.BlockSpec((None, 1, f), bias_map),
            pl.BlockSpec((None, 1, d), bias_map),
            pl.BlockSpec(memory_space=pl.ANY),
            pl.BlockSpec(memory_space=pl.ANY),
            pl.BlockSpec(memory_space=pl.ANY),
        ],
        out_specs=pl.BlockSpec((tm * SUBLANES, LANES), tile_map),
        scratch_shapes=[
            pltpu.VMEM((d, f), BF16), pltpu.VMEM((d, f), BF16), pltpu.VMEM((f, d), BF16),
            pltpu.VMEM((d, f), F32), pltpu.VMEM((d, f), F32), pltpu.VMEM((f, d), F32),
            pltpu.VMEM((tm * SUBLANES, LANES), F32),
            pltpu.SemaphoreType.DMA((3,)),
        ],
    )
    return pl.pallas_call(
        _experts_kernel,
        out_shape=jax.ShapeDtypeStruct(rows.shape, F32),
        grid_spec=grid_spec,
        compiler_params=pltpu.CompilerParams(
            dimension_semantics=("arbitrary",), vmem_limit_bytes=VMEM_LIMIT),
        name="experts",
    )(gid, tid, nxt, offs, nact, rows, bg.reshape(ne, 1, f), bu.reshape(ne, 1, f), bd.reshape(ne, 1, d),
      wg, wu, wd)


def _combine_kernel(dest_ref, dnext_ref, gate_ref, x1p_ref, x1s_ref, modp_ref, g2s_ref, fn_ref, y_hbm,
                    outp_ref, outs_ref, gbuf, sems, *, n_prompt_tiles):
    i = pl.program_id(0)
    n = pl.num_programs(0)
    tm = x1p_ref.shape[0]
    slot = lax.rem(i, 2)

    def start_gather(idx_ref, into):
        def issue(t, carry):
            dst = pl.ds(pl.multiple_of(t * SUBLANES, SUBLANES), SUBLANES)
            for k in range(TOP_K):
                d = pl.multiple_of(idx_ref[t * TOP_K + k] * SUBLANES, SUBLANES)
                pltpu.make_async_copy(y_hbm.at[pl.ds(d, SUBLANES)], gbuf.at[into, k, dst],
                                      sems.at[into]).start(priority=k % 2)
            return carry
        lax.fori_loop(0, tm, issue, 0)

    @pl.when(i == 0)
    def _():
        start_gather(dest_ref, 0)

    @pl.when(i + 1 < n)
    def _():
        start_gather(dnext_ref, 1 - slot)

    for k in range(TOP_K):
        pltpu.make_async_copy(y_hbm.at[pl.ds(0, tm * SUBLANES)], gbuf.at[slot, k], sems.at[slot]).wait()

    gates = gate_ref[...]
    cols = []
    for c in range(SUBLANES):
        acc = gates[:, 0:1] * gbuf[slot, 0, pl.ds(c, tm, stride=SUBLANES), :]
        for k in range(1, TOP_K):
            acc = acc + gates[:, k:k + 1] * gbuf[slot, k, pl.ds(c, tm, stride=SUBLANES), :]
        cols.append(acc)
    ffn = jnp.concatenate(cols, axis=-1)

    @pl.when(i < n_prompt_tiles)
    def _():
        g2 = modp_ref[...][5:6, :]
        outp_ref[...] = _rmsnorm(x1p_ref[...] + g2 * ffn, fn_ref[...])

    @pl.when(i >= n_prompt_tiles)
    def _():
        outs_ref[...] = _rmsnorm(x1s_ref[...] + g2s_ref[...] * ffn, fn_ref[...])


def _combine(dest_flat, gates, x1_p, x1_s, mod_p, g2_s, final_norm, y_rows, tokens_per_seq):
    n_p, d = x1_p.shape
    n_s = x1_s.shape[0]
    tm = COMBINE_TILE
    npt, nst = n_p // tm, n_s // tm
    tiles_per_seq = tokens_per_seq // tm
    pmap = lambda i: (jnp.minimum(i, npt - 1), 0)
    smap = lambda i: (jnp.maximum(i - npt, 0), 0)
    return pl.pallas_call(
        functools.partial(_combine_kernel, n_prompt_tiles=npt),
        out_shape=(jax.ShapeDtypeStruct((n_p, d), F32), jax.ShapeDtypeStruct((n_s, d), F32)),
        grid=(npt + nst,),
        in_specs=[
            pl.BlockSpec((tm * TOP_K,), lambda i: (i,), memory_space=pltpu.SMEM),
            pl.BlockSpec((tm * TOP_K,), lambda i: (jnp.minimum(i + 1, npt + nst - 1),),
                         memory_space=pltpu.SMEM),
            pl.BlockSpec((tm, TOP_K), lambda i: (i, 0)),
            pl.BlockSpec((tm, d), pmap),
            pl.BlockSpec((tm, d), smap),
            pl.BlockSpec((None, 6, d), lambda i: (jnp.minimum(i, npt - 1) // tiles_per_seq, 0, 0)),
            pl.BlockSpec((tm, d), smap),
            pl.BlockSpec((1, d), lambda i: (0, 0)),
            pl.BlockSpec(memory_space=pl.ANY),
        ],
        out_specs=(pl.BlockSpec((tm, d), pmap), pl.BlockSpec((tm, d), smap)),
        scratch_shapes=[pltpu.VMEM((2, TOP_K, tm * SUBLANES, LANES), F32), pltpu.SemaphoreType.DMA((2,))],
        compiler_params=pltpu.CompilerParams(
            dimension_semantics=("arbitrary",), vmem_limit_bytes=VMEM_LIMIT),
        name="combine",
    )(dest_flat, dest_flat, gates, x1_p, x1_s, mod_p, g2_s, final_norm, y_rows)


def kernel(x_prompt, x_sample, state_pool, state_conv, c_prompt, c_sample, norm1, norm2, w_ada, b_ada,
           w_in, w_pool, pool_scale, w_conv, w_out, w_router, b_router, w_gate, b_gate, w_up, b_up,
           w_down, b_down, final_norm):
    depth = norm1.shape[0]
    assert depth == 1, "single-layer step"
    bp, tp, d = x_prompt.shape
    bs, ts, _ = x_sample.shape
    dp = state_pool.shape[-1]
    n_hist = state_pool.shape[2]
    n_chist = state_conv.shape[2]
    n_p, n_s = bp * tp, bs * ts
    assert d == SUBLANES * LANES, "token-tile layout assumes one vreg tile per token row"
    assert tp % TOKEN_TILE == 0 and tp % COMBINE_TILE == 0 and n_s % COMBINE_TILE == 0
    assert n_p % DISPATCH_TILE == 0 and n_s % DISPATCH_TILE == 0
    assert (n_p + n_s) % ROUTE_TILE == 0 and ((n_p + n_s) * TOP_K) % EXPERT_TILE == 0

    l = 0
    n1 = norm1[l].reshape(1, d)
    n2 = norm2[l].reshape(1, d)
    w_in_b = w_in[l].astype(BF16)
    w_pool_b = w_pool[l].astype(BF16)
    w_out_b = w_out[l].astype(BF16)
    pscale = pool_scale[l].reshape(1, dp)
    w_r = jnp.pad(w_router[l], ((0, 0), (0, LANES - N_EXPERTS))).astype(BF16)
    b_r = jnp.pad(b_router[l], (0, LANES - N_EXPERTS), constant_values=NEG_BIG).reshape(1, LANES)

    mod = _adaln(jnp.concatenate([c_prompt, c_sample], axis=0), w_ada[l], b_ada[l])
    mod_p = mod[:bp].reshape(bp, 6, d)
    mod_s = mod[bp:]

    x1_p, h2_p, lg_p, u_tail, v_tail = _mixer_prompt(
        x_prompt, mod_p, n1, n2, w_in_b, w_pool_b, pscale, w_conv[l], w_out_b, w_r, b_r)

    xs_tm = jnp.transpose(x_sample, (1, 0, 2)).reshape(n_s, d)
    ps_tm = jnp.transpose(state_pool[l], (1, 0, 2))
    cs_tm = jnp.transpose(state_conv[l], (1, 0, 2))
    x1_s, h2_s, lg_s, newp_tm, newc_tm = _mixer_sample(
        xs_tm, mod_s, ps_tm, cs_tm, n1, n2, w_in_b, w_pool_b, pscale, w_conv[l], w_out_b, w_r, b_r, ts)

    dest, gates, counts_f = _route(jnp.concatenate([lg_p, lg_s], axis=0))
    counts = counts_f[0, :N_EXPERTS].astype(I32)
    dest_flat = dest.reshape(-1)
    n_rows = (n_p + n_s) * TOP_K
    gid, tid, nxt, offs, nact, n_steps = _group_metadata(counts, n_rows, EXPERT_TILE)

    rows = _dispatch(dest_flat, h2_p, h2_s)
    y_rows = _experts(gid, tid, nxt, offs, nact, n_steps, rows,
                      w_gate[l], b_gate[l], w_up[l], b_up[l], w_down[l], b_down[l])

    g2_s = jnp.tile(mod_s[:, 5 * d:], (ts, 1))
    y_p, y_s = _combine(dest_flat, gates, x1_p, x1_s, mod_p, g2_s, final_norm.reshape(1, d), y_rows, tp)

    y_prompt = y_p.reshape(bp, tp, d)
    y_sample = jnp.transpose(y_s.reshape(ts, bs, d), (1, 0, 2))
    new_pool_prompt = u_tail[:, POOL_HALO - n_hist:, :][None]
    new_conv_prompt = v_tail[:, CONV_HALO - n_chist:, :][None]
    new_pool_sample = jnp.transpose(newp_tm, (1, 0, 2))[None]
    new_conv_sample = jnp.transpose(newc_tm, (1, 0, 2))[None]
    return (y_prompt, y_sample, new_pool_prompt, new_conv_prompt, new_pool_sample, new_conv_sample)
```

```python
import functools

import jax
import jax.numpy as jnp
from jax import lax
from jax.experimental import pallas as pl
from jax.experimental.pallas import tpu as pltpu

F32 = jnp.float32
BF16 = jnp.bfloat16
I32 = jnp.int32

POOL_WINDOWS = (2, 4, 8, 16)
POOL_HALO = 16
CONV_TAPS = 3
CONV_HALO = 8
N_EXPERTS = 32
TOP_K = 4
SWIGLU_LIMIT = 7.0
SWIGLU_ALPHA = 1.702
EPS = 1e-5
PAST_LEN = 16384

LANES = 128
SUBLANES = 8

TOKEN_TILE = 512
ROUTE_TILE = 512
DISPATCH_TILE = 512
COMBINE_TILE = 256
EXPERT_TILE = 512
EXPERT_SUBTILE = 128
VMEM_LIMIT = 56 * 1024 * 1024


def _rmsnorm(x, g):
    ms = jnp.mean(x * x, axis=-1, keepdims=True)
    return x * lax.rsqrt(ms + EPS) * g


def _dot(a, b):
    return jnp.dot(a, b, preferred_element_type=F32)


def _store_token_tiles(ref, val):
    rows = val.shape[0]
    for c in range(SUBLANES):
        ref[pl.ds(c, rows, stride=SUBLANES), :] = val[:, c * LANES:(c + 1) * LANES]


def _load_token_tiles(ref):
    rows = ref.shape[0] // SUBLANES
    return jnp.concatenate([ref[pl.ds(c, rows, stride=SUBLANES), :] for c in range(SUBLANES)], axis=-1)


ISSUE_GROUP = 4


def _for_each_token_slot(n_tokens, idx_ref, start_copy):
    def group(g, carry):
        t0 = g * ISSUE_GROUP
        idx = [[idx_ref[k, t0 + j] for k in range(TOP_K)] for j in range(ISSUE_GROUP)]
        for j in range(ISSUE_GROUP):
            for k in range(TOP_K):
                start_copy(t0 + j, k, idx[j][k])
        return carry
    lax.fori_loop(0, n_tokens // ISSUE_GROUP, group, 0)


def _adaln_kernel(c_ref, w_ref, b_ref, o_ref):
    c = c_ref[...]
    s = c * jax.nn.sigmoid(c)
    o_ref[...] = _dot(s.astype(BF16), w_ref[...].astype(BF16)) + b_ref[...]


def _adaln(c, w_ada, b_ada):
    rows, d = c.shape
    n = w_ada.shape[1]
    tn = 1024
    return pl.pallas_call(
        _adaln_kernel,
        out_shape=jax.ShapeDtypeStruct((rows, n), F32),
        grid=(n // tn,),
        in_specs=[
            pl.BlockSpec((rows, d), lambda j: (0, 0)),
            pl.BlockSpec((d, tn), lambda j: (0, j)),
            pl.BlockSpec((1, tn), lambda j: (0, j)),
        ],
        out_specs=pl.BlockSpec((rows, tn), lambda j: (0, j)),
        compiler_params=pltpu.CompilerParams(
            dimension_semantics=("arbitrary",), vmem_limit_bytes=VMEM_LIMIT),
        name="adaln",
    )(c, w_ada, b_ada.reshape(1, n))


def _mix_tail(x, pool_in, conv_out, g1, sc2, sh2, n2, wpool_ref, pscale, wout, wr, br):
    gw = pool_in.shape[1] // len(POOL_WINDOWS)
    mixed = [_dot(pool_in[:, g * gw:(g + 1) * gw].astype(BF16), wpool_ref[g])
             for g in range(len(POOL_WINDOWS))]
    pool_out = jnp.concatenate(mixed, axis=-1) * pscale
    mix_in = jnp.concatenate([pool_out, conv_out], axis=-1).astype(BF16)
    x1 = x + g1 * _dot(mix_in, wout)
    h2 = _rmsnorm(x1, n2) * (1.0 + sc2) + sh2
    logits_t = lax.dot_general(wr, h2.astype(BF16), (((1,), (1,)), ((), ())),
                               preferred_element_type=F32) + br
    return x1, h2, logits_t


def _mixer_prompt_kernel(x_ref, mod_ref, n1_ref, n2_ref, win_ref, wpool_ref, pscale_ref, wconv_ref,
                         wout_ref, wr_ref, br_ref,
                         x1_ref, h2_ref, lg_ref, upool_ref, vconv_ref, ubuf, vbuf):
    tt = x_ref.shape[0]
    dp = ubuf.shape[1]
    gw = dp // len(POOL_WINDOWS)
    t = pl.program_id(1)

    @pl.when(t == 0)
    def _():
        ubuf[0:POOL_HALO, :] = jnp.zeros((POOL_HALO, dp), F32)
        vbuf[0:CONV_HALO, :] = jnp.zeros((CONV_HALO, dp), F32)

    x = x_ref[...]
    mod = mod_ref[...]
    sh1, sc1, g1, sh2, sc2, _ = [mod[i:i + 1, :] for i in range(6)]
    h = _rmsnorm(x, n1_ref[...]) * (1.0 + sc1) + sh1
    z = _dot(h.astype(BF16), win_ref[...])
    u, gate_b, gate_c, val = [z[:, i * dp:(i + 1) * dp] for i in range(4)]

    ubuf[POOL_HALO:POOL_HALO + tt, :] = u
    pos = lax.broadcasted_iota(I32, (tt, gw), 0) + t * tt
    pooled = []
    for g, w in enumerate(POOL_WINDOWS):
        cols = slice(g * gw, (g + 1) * gw)
        acc = u[:, cols]
        for j in range(1, w):
            acc = acc + ubuf[POOL_HALO - j:POOL_HALO - j + tt, cols]
        cnt = jnp.minimum(pos + 1, w).astype(F32)
        pooled.append(acc / cnt - u[:, cols])
    pool_in = jnp.concatenate(pooled, axis=-1)

    v = gate_c * val
    vbuf[CONV_HALO:CONV_HALO + tt, :] = v
    wc = wconv_ref[...]
    y = (wc[0:1, :] * vbuf[CONV_HALO - 2:CONV_HALO - 2 + tt, :]
         + wc[1:2, :] * vbuf[CONV_HALO - 1:CONV_HALO - 1 + tt, :]
         + wc[2:3, :] * v)
    conv_out = gate_b * y

    x1, h2, logits = _mix_tail(x, pool_in, conv_out, g1, sc2, sh2, n2_ref[...], wpool_ref,
                               pscale_ref[...], wout_ref[...], wr_ref[...], br_ref[...])
    x1_ref[...] = x1
    _store_token_tiles(h2_ref, h2)
    lg_ref[...] = logits

    ubuf[0:POOL_HALO, :] = ubuf[tt:tt + POOL_HALO, :]
    vbuf[0:CONV_HALO, :] = vbuf[tt:tt + CONV_HALO, :]

    @pl.when(t == pl.num_programs(1) - 1)
    def _():
        upool_ref[...] = ubuf[0:POOL_HALO, :]
        vconv_ref[...] = vbuf[0:CONV_HALO, :]


def _mixer_prompt(x, mod_p, n1, n2, w_in, w_pool, pscale, w_conv, w_out, w_r, b_r):
    b, t, d = x.shape
    dp = w_pool.shape[0] * w_pool.shape[1]
    tt = min(TOKEN_TILE, t)
    const2 = lambda i, j: (0, 0)
    const3 = lambda i, j: (0, 0, 0)
    return pl.pallas_call(
        _mixer_prompt_kernel,
        out_shape=(
            jax.ShapeDtypeStruct((b * t, d), F32),
            jax.ShapeDtypeStruct((b * t * SUBLANES, LANES), F32),
            jax.ShapeDtypeStruct((N_EXPERTS, b * t), F32),
            jax.ShapeDtypeStruct((b, POOL_HALO, dp), F32),
            jax.ShapeDtypeStruct((b, CONV_HALO, dp), F32),
        ),
        grid=(b, t // tt),
        in_specs=[
            pl.BlockSpec((None, tt, d), lambda i, j: (i, j, 0)),
            pl.BlockSpec((None, 6, d), lambda i, j: (i, 0, 0)),
            pl.BlockSpec((1, d), const2),
            pl.BlockSpec((1, d), const2),
            pl.BlockSpec(w_in.shape, const2),
            pl.BlockSpec(w_pool.shape, const3),
            pl.BlockSpec((1, dp), const2),
            pl.BlockSpec(w_conv.shape, const2),
            pl.BlockSpec(w_out.shape, const2),
            pl.BlockSpec(w_r.shape, const2),
            pl.BlockSpec((N_EXPERTS, 1), const2),
        ],
        out_specs=(
            pl.BlockSpec((tt, d), lambda i, j: (i * (t // tt) + j, 0)),
            pl.BlockSpec((tt * SUBLANES, LANES), lambda i, j: (i * (t // tt) + j, 0)),
            pl.BlockSpec((N_EXPERTS, tt), lambda i, j: (0, i * (t // tt) + j)),
            pl.BlockSpec((None, POOL_HALO, dp), lambda i, j: (i, 0, 0)),
            pl.BlockSpec((None, CONV_HALO, dp), lambda i, j: (i, 0, 0)),
        ),
        scratch_shapes=[
            pltpu.VMEM((POOL_HALO + tt, dp), F32),
            pltpu.VMEM((CONV_HALO + tt, dp), F32),
        ],
        compiler_params=pltpu.CompilerParams(
            dimension_semantics=("arbitrary", "arbitrary"), vmem_limit_bytes=VMEM_LIMIT),
        name="mixer_prompt",
    )(x, mod_p, n1, n2, w_in, w_pool, pscale, w_conv, w_out, w_r, b_r)


def _mixer_sample_kernel(x_ref, mod_ref, pstate_ref, cstate_ref, n1_ref, n2_ref, win_ref, wpool_ref,
                         pscale_ref, wconv_ref, wout_ref, wr_ref, br_ref,
                         x1_ref, h2_ref, lg_ref, newp_ref, newc_ref, *, steps):
    nb = mod_ref.shape[0]
    d = x_ref.shape[1]
    dp = pstate_ref.shape[2]
    gw = dp // len(POOL_WINDOWS)
    n_hist = pstate_ref.shape[0]
    n_chist = cstate_ref.shape[0]

    x = x_ref[...]
    mod = mod_ref[...]
    rep = lambda a: jnp.concatenate([a] * steps, axis=0)
    sh1, sc1, g1, sh2, sc2, _ = [rep(mod[:, i * d:(i + 1) * d]) for i in range(6)]
    h = _rmsnorm(x, n1_ref[...]) * (1.0 + sc1) + sh1
    z = _dot(h.astype(BF16), win_ref[...])
    u, gate_b, gate_c, val = [z[:, i * dp:(i + 1) * dp] for i in range(4)]

    ext = [pstate_ref[i] for i in range(n_hist)] + [u[s * nb:(s + 1) * nb, :] for s in range(steps)]
    pooled_steps = []
    for s in range(steps):
        groups = []
        for g, w in enumerate(POOL_WINDOWS):
            cols = slice(g * gw, (g + 1) * gw)
            acc = ext[n_hist + s][:, cols]
            for j in range(1, w):
                acc = acc + ext[n_hist + s - j][:, cols]
            cnt = float(min(PAST_LEN + s + 1, w))
            groups.append(acc / cnt - ext[n_hist + s][:, cols])
        pooled_steps.append(jnp.concatenate(groups, axis=-1))
    pool_in = jnp.concatenate(pooled_steps, axis=0)

    v = gate_c * val
    vext = [cstate_ref[i] for i in range(n_chist)] + [v[s * nb:(s + 1) * nb, :] for s in range(steps)]
    wc = wconv_ref[...]
    y = jnp.concatenate(
        [wc[0:1, :] * vext[s] + wc[1:2, :] * vext[s + 1] + wc[2:3, :] * vext[s + 2] for s in range(steps)],
        axis=0)
    conv_out = gate_b * y

    x1, h2, logits = _mix_tail(x, pool_in, conv_out, g1, sc2, sh2, n2_ref[...], wpool_ref,
                               pscale_ref[...], wout_ref[...], wr_ref[...], br_ref[...])
    x1_ref[...] = x1
    _store_token_tiles(h2_ref, h2)
    lg_ref[...] = logits
    for i in range(n_hist):
        newp_ref[i] = ext[steps + i]
    for i in range(n_chist):
        newc_ref[i] = vext[steps + i]


def _mixer_sample(x_tm, mod_s, pstate_tm, cstate_tm, n1, n2, w_in, w_pool, pscale, w_conv, w_out, w_r, b_r,
                  steps):
    rows, d = x_tm.shape
    return pl.pallas_call(
        functools.partial(_mixer_sample_kernel, steps=steps),
        out_shape=(
            jax.ShapeDtypeStruct((rows, d), F32),
            jax.ShapeDtypeStruct((rows * SUBLANES, LANES), F32),
            jax.ShapeDtypeStruct((N_EXPERTS, rows), F32),
            jax.ShapeDtypeStruct(pstate_tm.shape, F32),
            jax.ShapeDtypeStruct(cstate_tm.shape, F32),
        ),
        compiler_params=pltpu.CompilerParams(vmem_limit_bytes=VMEM_LIMIT),
        name="mixer_sample",
    )(x_tm, mod_s, pstate_tm, cstate_tm, n1, n2, w_in, w_pool, pscale, w_conv, w_out, w_r, b_r)


def _route_kernel(lgp_ref, lgs_ref, dest_ref, gate_ref, cnt_ref, counts, start, *, n_prompt_tiles):
    phase = pl.program_id(0)
    i = pl.program_id(1)
    ne, tr = lgp_ref.shape
    reps = tr // LANES

    @pl.when((phase == 0) & (i == 0))
    def _():
        counts[...] = jnp.zeros_like(counts)

    @pl.when((phase == 1) & (i == 0))
    def _():
        c = counts[...]
        hi = jnp.floor(c * (1.0 / 256.0))
        lo = c - hi * 256.0
        r = lax.broadcasted_iota(I32, (ne, ne), 0)
        col = lax.broadcasted_iota(I32, (ne, ne), 1)
        lower = jnp.where(col < r, 1.0, 0.0).astype(BF16)
        start[...] = 256.0 * _dot(lower, hi.astype(BF16)) + _dot(lower, lo.astype(BF16))
        cnt_ref[...] = c
        counts[...] = jnp.zeros_like(counts)

    work = jnp.where(i < n_prompt_tiles, lgp_ref[...], lgs_ref[...])
    eidx = lax.broadcasted_iota(I32, (ne, tr), 0)
    top_v, onehots = [], []
    for _ in range(TOP_K):
        m = jnp.max(work, axis=0, keepdims=True)
        idx = jnp.min(jnp.where(work == m, eidx, ne), axis=0, keepdims=True)
        sel = eidx == idx
        top_v.append(m)
        onehots.append(sel)
        work = jnp.where(sel, -jnp.inf, work)
    mask = jnp.where(onehots[0] | onehots[1] | onehots[2] | onehots[3], 1.0, 0.0)
    tile_counts = jnp.broadcast_to(jnp.sum(mask, axis=1, keepdims=True), (ne, LANES))

    @pl.when(phase == 0)
    def _():
        counts[...] = counts[...] + tile_counts

    @pl.when(phase == 1)
    def _():
        r = lax.broadcasted_iota(I32, (tr, tr), 0)
        col = lax.broadcasted_iota(I32, (tr, tr), 1)
        before = jnp.where(r < col, 1.0, 0.0).astype(BF16)
        base = jnp.concatenate([counts[...] + start[...]] * reps, axis=1)
        rank = _dot(mask.astype(BF16), before) + base
        counts[...] = counts[...] + tile_counts
        es = [jnp.exp(v - top_v[0]) for v in top_v]
        denom = es[0] + es[1] + es[2] + es[3]
        gate_ref[...] = jnp.zeros_like(gate_ref)
        for k in range(TOP_K):
            d = jnp.sum(jnp.where(onehots[k], rank, 0.0), axis=0, keepdims=True)
            dest_ref[k:k + 1, :] = d.astype(I32)
            gate_ref[k:k + 1, :] = es[k] / denom


def _route(lgt_p, lgt_s):
    ne, n_p = lgt_p.shape
    n_s = lgt_s.shape[1]
    tr = ROUTE_TILE
    npt, nst = n_p // tr, n_s // tr
    return pl.pallas_call(
        functools.partial(_route_kernel, n_prompt_tiles=npt),
        out_shape=(
            jax.ShapeDtypeStruct((TOP_K, n_p + n_s), I32),
            jax.ShapeDtypeStruct((SUBLANES, n_p + n_s), F32),
            jax.ShapeDtypeStruct((ne, LANES), F32),
        ),
        grid=(2, npt + nst),
        in_specs=[
            pl.BlockSpec((ne, tr), lambda p, i: (0, jnp.minimum(i, npt - 1))),
            pl.BlockSpec((ne, tr), lambda p, i: (0, jnp.maximum(i - npt, 0))),
        ],
        out_specs=(
            pl.BlockSpec((TOP_K, tr), lambda p, i: (0, i * p)),
            pl.BlockSpec((SUBLANES, tr), lambda p, i: (0, i * p)),
            pl.BlockSpec((ne, LANES), lambda p, i: (0, 0)),
        ),
        scratch_shapes=[pltpu.VMEM((ne, LANES), F32), pltpu.VMEM((ne, LANES), F32)],
        compiler_params=pltpu.CompilerParams(
            dimension_semantics=("arbitrary", "arbitrary"), vmem_limit_bytes=VMEM_LIMIT),
        name="route",
    )(lgt_p, lgt_s)


def _group_metadata(counts, n_rows, tile):
    n_tiles = n_rows // tile
    n_steps = n_tiles + N_EXPERTS - 1
    ends = jnp.cumsum(counts)
    offs = jnp.concatenate([jnp.zeros((1,), I32), ends]).astype(I32)
    first_tile = offs[:-1] // tile
    last_tile = (ends - 1) // tile
    tiles_e = jnp.where(counts > 0, last_tile - first_tile + 1, 0)
    step_end = jnp.cumsum(tiles_e)
    step_start = step_end - tiles_e
    n_active = step_end[-1]
    s = jnp.minimum(jnp.arange(n_steps, dtype=I32), n_active - 1)
    owner = ((s[:, None] >= step_start[None, :]) & (s[:, None] < step_end[None, :])).astype(I32)
    gid = jnp.sum(owner * jnp.arange(N_EXPERTS, dtype=I32)[None, :], axis=1)
    tid = jnp.sum(owner * (first_tile - step_start)[None, :], axis=1) + s
    ids = jnp.arange(N_EXPERTS, dtype=I32)
    later = (ids[None, :] > ids[:, None]) & (counts[None, :] > 0)
    next_e = jnp.min(jnp.where(later, ids[None, :], N_EXPERTS), axis=1)
    next_e = jnp.where(next_e == N_EXPERTS, -1, next_e)
    nxt = jnp.sum(owner * next_e[None, :], axis=1)
    return gid, tid, nxt, offs, n_active.reshape(1).astype(I32), n_steps


def _dispatch_kernel(dest_ref, hp_hbm, hs_hbm, rows_ref, sem, *, n_prompt_tiles):
    i = pl.program_id(0)
    tm = dest_ref.shape[1]
    tile_rows = tm * SUBLANES

    def scatter(src_hbm, first_token):
        def start_copy(t, k, d):
            row = pl.multiple_of((first_token + t) * SUBLANES, SUBLANES)
            pltpu.make_async_copy(src_hbm.at[pl.ds(row, SUBLANES)],
                                  rows_ref.at[pl.ds(pl.multiple_of(d * SUBLANES, SUBLANES), SUBLANES)],
                                  sem).start(priority=k % 2)
        _for_each_token_slot(tm, dest_ref, start_copy)

    def wait_one_step():
        for _ in range(TOP_K):
            pltpu.make_async_copy(
                hp_hbm.at[pl.ds(0, tile_rows)], rows_ref.at[pl.ds(0, tile_rows)], sem).wait()

    @pl.when(i < n_prompt_tiles)
    def _():
        scatter(hp_hbm, i * tm)

    @pl.when(i >= n_prompt_tiles)
    def _():
        scatter(hs_hbm, (i - n_prompt_tiles) * tm)

    @pl.when(i > 0)
    def _():
        wait_one_step()

    @pl.when(i == pl.num_programs(0) - 1)
    def _():
        wait_one_step()


def _dispatch(dest, h2_p, h2_s):
    n_p = h2_p.shape[0] // SUBLANES
    n_s = h2_s.shape[0] // SUBLANES
    tm = DISPATCH_TILE
    npt, nst = n_p // tm, n_s // tm
    return pl.pallas_call(
        functools.partial(_dispatch_kernel, n_prompt_tiles=npt),
        out_shape=jax.ShapeDtypeStruct(((n_p + n_s) * TOP_K * SUBLANES, LANES), h2_p.dtype),
        grid=(npt + nst,),
        in_specs=[
            pl.BlockSpec((TOP_K, tm), lambda i: (0, i), memory_space=pltpu.SMEM),
            pl.BlockSpec(memory_space=pl.ANY),
            pl.BlockSpec(memory_space=pl.ANY),
        ],
        out_specs=pl.BlockSpec(memory_space=pl.ANY),
        scratch_shapes=[pltpu.SemaphoreType.DMA],
        compiler_params=pltpu.CompilerParams(
            dimension_semantics=("arbitrary",), vmem_limit_bytes=VMEM_LIMIT),
        name="dispatch",
    )(dest, h2_p, h2_s)


def _experts_kernel(gid_ref, tid_ref, nxt_ref, offs_ref, nact_ref,
                    rows_ref, bg_ref, bu_ref, bd_ref, wg_hbm, wu_hbm, wd_hbm, out_ref,
                    wg_b, wu_b, wd_b, wg_f, wu_f, wd_f, relay, sems):
    s = pl.program_id(0)
    tm = rows_ref.shape[0] // SUBLANES
    landing = ((wg_hbm, wg_f, wg_b), (wu_hbm, wu_f, wu_b), (wd_hbm, wd_f, wd_b))

    def fetch(e):
        for j, (hbm, land, _) in enumerate(landing):
            pltpu.make_async_copy(hbm.at[e], land, sems.at[j]).start()

    @pl.when(s < nact_ref[0])
    def _():
        e = gid_ref[s]
        m = tid_ref[s]

        @pl.when(s == 0)
        def _():
            fetch(e)

        @pl.when((s == 0) | (gid_ref[jnp.maximum(s - 1, 0)] != e))
        def _():
            for j, (hbm, land, half) in enumerate(landing):
                pltpu.make_async_copy(hbm.at[e], land, sems.at[j]).wait()
                half[...] = land[...].astype(BF16)

            @pl.when(nxt_ref[s] >= 0)
            def _():
                fetch(nxt_ref[s])

        def ffn(x):
            g = _dot(x, wg_b[...]) + bg_ref[...]
            u = _dot(x, wu_b[...]) + bu_ref[...]
            g = jnp.minimum(g, SWIGLU_LIMIT)
            u = jnp.clip(u, -SWIGLU_LIMIT, SWIGLU_LIMIT)
            glu = g * jax.nn.sigmoid(SWIGLU_ALPHA * g)
            return _dot(((u + 1.0) * glu).astype(BF16), wd_b[...]) + bd_ref[...]

        lo = offs_ref[e]
        hi = offs_ref[e + 1]
        whole_tile = (lo <= m * tm) & (hi >= (m + 1) * tm)

        @pl.when(whole_tile)
        def _():
            _store_token_tiles(out_ref, ffn(_load_token_tiles(rows_ref).astype(BF16)))

        @pl.when(jnp.logical_not(whole_tile))
        def _():
            sub = relay.shape[0] // SUBLANES
            for j in range(tm // sub):
                first = m * tm + j * sub

                @pl.when((lo < first + sub) & (hi > first))
                def _():
                    span = pl.ds(j * sub * SUBLANES, sub * SUBLANES)
                    _store_token_tiles(relay, ffn(_load_token_tiles(rows_ref.at[span]).astype(BF16)))
                    row = first + lax.shift_right_logical(lax.broadcasted_iota(I32, relay.shape, 0), 3)
                    pltpu.store(out_ref.at[span], relay[...], mask=(row >= lo) & (row < hi))


def _experts(gid, tid, nxt, offs, nact, n_steps, rows, wg, bg, wu, bu, wd, bd):
    ne, d, f = wg.shape
    tm = EXPERT_TILE
    tile_map = lambda s, gid, tid, nxt, offs, nact: (tid[s], 0)
    bias_map = lambda s, gid, tid, nxt, offs, nact: (gid[s], 0, 0)
    grid_spec = pltpu.PrefetchScalarGridSpec(
        num_scalar_prefetch=5,
        grid=(n_steps,),
        in_specs=[
            pl.BlockSpec((tm * SUBLANES, LANES), tile_map),
            pl.BlockSpec((None, 1, f), bias_map),
            pl.BlockSpec((None, 1, f), bias_map),
            pl.BlockSpec((None, 1, d), bias_map),
            pl.BlockSpec(memory_space=pl.ANY),
            pl.BlockSpec(memory_space=pl.ANY),
            pl.BlockSpec(memory_space=pl.ANY),
        ],
        out_specs=pl.BlockSpec((tm * SUBLANES, LANES), tile_map),
        scratch_shapes=[
            pltpu.VMEM((d, f), BF16), pltpu.VMEM((d, f), BF16), pltpu.VMEM((f, d), BF16),
            pltpu.VMEM((d, f), F32), pltpu.VMEM((d, f), F32), pltpu.VMEM((f, d), F32),
            pltpu.VMEM((EXPERT_SUBTILE * SUBLANES, LANES), F32),
            pltpu.SemaphoreType.DMA((3,)),
        ],
    )
    return pl.pallas_call(
        _experts_kernel,
        out_shape=jax.ShapeDtypeStruct(rows.shape, F32),
        grid_spec=grid_spec,
        compiler_params=pltpu.CompilerParams(
            dimension_semantics=("arbitrary",), vmem_limit_bytes=VMEM_LIMIT),
        name="experts",
    )(gid, tid, nxt, offs, nact, rows, bg.reshape(ne, 1, f), bu.reshape(ne, 1, f), bd.reshape(ne, 1, d),
      wg, wu, wd)


def _combine_kernel(dest_ref, dnext_ref, gate_ref, x1p_ref, x1s_ref, modp_ref, g2s_ref, fn_ref, y_hbm,
                    outp_ref, outs_ref, gbuf, sems, *, n_prompt_tiles):
    i = pl.program_id(0)
    n = pl.num_programs(0)
    tm = x1p_ref.shape[0]
    slot = lax.rem(i, 2)

    def start_gather(idx_ref, into):
        def start_copy(t, k, d):
            src = y_hbm.at[pl.ds(pl.multiple_of(d * SUBLANES, SUBLANES), SUBLANES)]
            dst = gbuf.at[into, k, pl.ds(pl.multiple_of(t * SUBLANES, SUBLANES), SUBLANES)]
            pltpu.make_async_copy(src, dst, sems.at[into]).start(priority=k % 2)
        _for_each_token_slot(tm, idx_ref, start_copy)

    @pl.when(i == 0)
    def _():
        start_gather(dest_ref, 0)

    @pl.when(i + 1 < n)
    def _():
        start_gather(dnext_ref, 1 - slot)

    for k in range(TOP_K):
        pltpu.make_async_copy(y_hbm.at[pl.ds(0, tm * SUBLANES)], gbuf.at[slot, k], sems.at[slot]).wait()

    gates = gate_ref[...].T
    cols = []
    for c in range(SUBLANES):
        acc = gates[:, 0:1] * gbuf[slot, 0, pl.ds(c, tm, stride=SUBLANES), :]
        for k in range(1, TOP_K):
            acc = acc + gates[:, k:k + 1] * gbuf[slot, k, pl.ds(c, tm, stride=SUBLANES), :]
        cols.append(acc)
    ffn = jnp.concatenate(cols, axis=-1)

    @pl.when(i < n_prompt_tiles)
    def _():
        g2 = modp_ref[...][5:6, :]
        outp_ref[...] = _rmsnorm(x1p_ref[...] + g2 * ffn, fn_ref[...])

    @pl.when(i >= n_prompt_tiles)
    def _():
        outs_ref[...] = _rmsnorm(x1s_ref[...] + g2s_ref[...] * ffn, fn_ref[...])


def _combine(dest, gates, x1_p, x1_s, mod_p, g2_s, final_norm, y_rows, tokens_per_seq):
    n_p, d = x1_p.shape
    n_s = x1_s.shape[0]
    tm = COMBINE_TILE
    npt, nst = n_p // tm, n_s // tm
    tiles_per_seq = tokens_per_seq // tm
    pmap = lambda i: (jnp.minimum(i, npt - 1), 0)
    smap = lambda i: (jnp.maximum(i - npt, 0), 0)
    return pl.pallas_call(
        functools.partial(_combine_kernel, n_prompt_tiles=npt),
        out_shape=(jax.ShapeDtypeStruct((n_p, d), F32), jax.ShapeDtypeStruct((n_s, d), F32)),
        grid=(npt + nst,),
        in_specs=[
            pl.BlockSpec((TOP_K, tm), lambda i: (0, i), memory_space=pltpu.SMEM),
            pl.BlockSpec((TOP_K, tm), lambda i: (0, jnp.minimum(i + 1, npt + nst - 1)),
                         memory_space=pltpu.SMEM),
            pl.BlockSpec((SUBLANES, tm), lambda i: (0, i)),
            pl.BlockSpec((tm, d), pmap),
            pl.BlockSpec((tm, d), smap),
            pl.BlockSpec((None, 6, d), lambda i: (jnp.minimum(i, npt - 1) // tiles_per_seq, 0, 0)),
            pl.BlockSpec((tm, d), smap),
            pl.BlockSpec((1, d), lambda i: (0, 0)),
            pl.BlockSpec(memory_space=pl.ANY),
        ],
        out_specs=(pl.BlockSpec((tm, d), pmap), pl.BlockSpec((tm, d), smap)),
        scratch_shapes=[pltpu.VMEM((2, TOP_K, tm * SUBLANES, LANES), F32), pltpu.SemaphoreType.DMA((2,))],
        compiler_params=pltpu.CompilerParams(
            dimension_semantics=("arbitrary",), vmem_limit_bytes=VMEM_LIMIT),
        name="combine",
    )(dest, dest, gates, x1_p, x1_s, mod_p, g2_s, final_norm, y_rows)


def kernel(x_prompt, x_sample, state_pool, state_conv, c_prompt, c_sample, norm1, norm2, w_ada, b_ada,
           w_in, w_pool, pool_scale, w_conv, w_out, w_router, b_router, w_gate, b_gate, w_up, b_up,
           w_down, b_down, final_norm):
    depth = norm1.shape[0]
    assert depth == 1, "single-layer step"
    bp, tp, d = x_prompt.shape
    bs, ts, _ = x_sample.shape
    dp = state_pool.shape[-1]
    n_hist = state_pool.shape[2]
    n_chist = state_conv.shape[2]
    n_p, n_s = bp * tp, bs * ts
    assert d == SUBLANES * LANES, "token-tile layout assumes one vreg tile per token row"
    assert tp % TOKEN_TILE == 0 and tp % COMBINE_TILE == 0 and n_s % COMBINE_TILE == 0
    assert n_p % DISPATCH_TILE == 0 and n_s % DISPATCH_TILE == 0
    assert (n_p + n_s) % ROUTE_TILE == 0 and ((n_p + n_s) * TOP_K) % EXPERT_TILE == 0

    l = 0
    n1 = norm1[l].reshape(1, d)
    n2 = norm2[l].reshape(1, d)
    w_in_b = w_in[l].astype(BF16)
    w_pool_b = w_pool[l].astype(BF16)
    w_out_b = w_out[l].astype(BF16)
    pscale = pool_scale[l].reshape(1, dp)
    w_r = w_router[l].T.astype(BF16)
    b_r = b_router[l].reshape(N_EXPERTS, 1)

    mod = _adaln(jnp.concatenate([c_sample, c_prompt], axis=0), w_ada[l], b_ada[l])
    mod_s = mod[:bs]
    mod_p = mod[bs:].reshape(bp, 6, d)

    x1_p, h2_p, lg_p, u_tail, v_tail = _mixer_prompt(
        x_prompt, mod_p, n1, n2, w_in_b, w_pool_b, pscale, w_conv[l], w_out_b, w_r, b_r)

    xs_tm = jnp.transpose(x_sample, (1, 0, 2)).reshape(n_s, d)
    ps_tm = jnp.transpose(state_pool[l], (1, 0, 2))
    cs_tm = jnp.transpose(state_conv[l], (1, 0, 2))
    x1_s, h2_s, lg_s, newp_tm, newc_tm = _mixer_sample(
        xs_tm, mod_s, ps_tm, cs_tm, n1, n2, w_in_b, w_pool_b, pscale, w_conv[l], w_out_b, w_r, b_r, ts)

    dest, gates, counts_f = _route(lg_p, lg_s)
    counts = counts_f[:, 0].astype(I32)
    n_rows = (n_p + n_s) * TOP_K
    gid, tid, nxt, offs, nact, n_steps = _group_metadata(counts, n_rows, EXPERT_TILE)

    rows = _dispatch(dest, h2_p, h2_s)
    y_rows = _experts(gid, tid, nxt, offs, nact, n_steps, rows,
                      w_gate[l], b_gate[l], w_up[l], b_up[l], w_down[l], b_down[l])

    g2_s = jnp.tile(mod_s[:, 5 * d:], (ts, 1))
    y_p, y_s = _combine(dest, gates, x1_p, x1_s, mod_p, g2_s, final_norm.reshape(1, d), y_rows, tp)

    y_prompt = y_p.reshape(bp, tp, d)
    y_sample = jnp.transpose(y_s.reshape(ts, bs, d), (1, 0, 2))
    new_pool_prompt = u_tail[:, POOL_HALO - n_hist:, :][None]
    new_conv_prompt = v_tail[:, CONV_HALO - n_chist:, :][None]
    new_pool_sample = jnp.transpose(newp_tm, (1, 0, 2))[None]
    new_conv_sample = jnp.transpose(newc_tm, (1, 0, 2))[None]
    return (y_prompt, y_sample, new_pool_prompt, new_conv_prompt, new_pool_sample, new_conv_sample)
```

```python
import functools

import jax
import jax.numpy as jnp
from jax import lax
from jax.experimental import pallas as pl
from jax.experimental.pallas import tpu as pltpu

F32 = jnp.float32
BF16 = jnp.bfloat16
I32 = jnp.int32

POOL_WINDOWS = (2, 4, 8, 16)
POOL_HALO = 16
CONV_TAPS = 3
CONV_HALO = 8
N_EXPERTS = 32
TOP_K = 4
SWIGLU_LIMIT = 7.0
SWIGLU_ALPHA = 1.702
EPS = 1e-5
PAST_LEN = 16384

LANES = 128
SUBLANES = 8

TOKEN_TILE = 512
ROUTE_TILE = 512
DISPATCH_TILE = 512
COMBINE_TILE = 256
EXPERT_TILE = 512
EXPERT_SUBTILE = 128
VMEM_LIMIT = 56 * 1024 * 1024


def _rmsnorm(x, g):
    ms = jnp.mean(x * x, axis=-1, keepdims=True)
    return x * lax.rsqrt(ms + EPS) * g


def _dot(a, b):
    return jnp.dot(a, b, preferred_element_type=F32)


def _store_token_tiles(ref, val):
    rows = val.shape[0]
    for c in range(SUBLANES):
        ref[pl.ds(c, rows, stride=SUBLANES), :] = val[:, c * LANES:(c + 1) * LANES]


def _load_token_tiles(ref):
    rows = ref.shape[0] // SUBLANES
    return jnp.concatenate([ref[pl.ds(c, rows, stride=SUBLANES), :] for c in range(SUBLANES)], axis=-1)


ISSUE_GROUP = 4


def _for_each_token_slot(n_tokens, idx_ref, start_copy):
    def group(g, carry):
        t0 = g * ISSUE_GROUP
        idx = [[idx_ref[k, t0 + j] for k in range(TOP_K)] for j in range(ISSUE_GROUP)]
        for j in range(ISSUE_GROUP):
            for k in range(TOP_K):
                start_copy(t0 + j, k, idx[j][k])
        return carry
    lax.fori_loop(0, n_tokens // ISSUE_GROUP, group, 0)


def _adaln_kernel(c_ref, w_ref, b_ref, o_ref):
    c = c_ref[...]
    s = c * jax.nn.sigmoid(c)
    o_ref[...] = _dot(s.astype(BF16), w_ref[...].astype(BF16)) + b_ref[...]


def _adaln(c, w_ada, b_ada):
    rows, d = c.shape
    n = w_ada.shape[1]
    tn = 1024
    return pl.pallas_call(
        _adaln_kernel,
        out_shape=jax.ShapeDtypeStruct((rows, n), F32),
        grid=(n // tn,),
        in_specs=[
            pl.BlockSpec((rows, d), lambda j: (0, 0)),
            pl.BlockSpec((d, tn), lambda j: (0, j)),
            pl.BlockSpec((1, tn), lambda j: (0, j)),
        ],
        out_specs=pl.BlockSpec((rows, tn), lambda j: (0, j)),
        compiler_params=pltpu.CompilerParams(
            dimension_semantics=("arbitrary",), vmem_limit_bytes=VMEM_LIMIT),
        name="adaln",
    )(c, w_ada, b_ada.reshape(1, n))


def _mix_tail(x, pool_in, conv_out, g1, sc2, sh2, n2, wpool_ref, pscale, wout, wr, br):
    gw = pool_in.shape[1] // len(POOL_WINDOWS)
    mixed = [_dot(pool_in[:, g * gw:(g + 1) * gw].astype(BF16), wpool_ref[g])
             for g in range(len(POOL_WINDOWS))]
    pool_out = jnp.concatenate(mixed, axis=-1) * pscale
    mix_in = jnp.concatenate([pool_out, conv_out], axis=-1).astype(BF16)
    x1 = x + g1 * _dot(mix_in, wout)
    h2 = _rmsnorm(x1, n2) * (1.0 + sc2) + sh2
    logits_t = lax.dot_general(wr, h2.astype(BF16), (((1,), (1,)), ((), ())),
                               preferred_element_type=F32) + br
    return x1, h2, logits_t


def _mixer_prompt_kernel(x_ref, mod_ref, n1_ref, n2_ref, win_ref, wpool_ref, pscale_ref, wconv_ref,
                         wout_ref, wr_ref, br_ref,
                         x1_ref, h2_ref, lg_ref, upool_ref, vconv_ref, ubuf, vbuf):
    tt = x_ref.shape[0]
    dp = ubuf.shape[1]
    gw = dp // len(POOL_WINDOWS)
    t = pl.program_id(1)

    @pl.when(t == 0)
    def _():
        ubuf[0:POOL_HALO, :] = jnp.zeros((POOL_HALO, dp), F32)
        vbuf[0:CONV_HALO, :] = jnp.zeros((CONV_HALO, dp), F32)

    x = x_ref[...]
    mod = mod_ref[...]
    sh1, sc1, g1, sh2, sc2, _ = [mod[i:i + 1, :] for i in range(6)]
    h = _rmsnorm(x, n1_ref[...]) * (1.0 + sc1) + sh1
    z = _dot(h.astype(BF16), win_ref[...])
    u, gate_b, gate_c, val = [z[:, i * dp:(i + 1) * dp] for i in range(4)]

    ubuf[POOL_HALO:POOL_HALO + tt, :] = u
    pos = lax.broadcasted_iota(I32, (tt, gw), 0) + t * tt
    pooled = []
    for g, w in enumerate(POOL_WINDOWS):
        cols = slice(g * gw, (g + 1) * gw)
        acc = u[:, cols]
        for j in range(1, w):
            acc = acc + ubuf[POOL_HALO - j:POOL_HALO - j + tt, cols]
        cnt = jnp.minimum(pos + 1, w).astype(F32)
        pooled.append(acc / cnt - u[:, cols])
    pool_in = jnp.concatenate(pooled, axis=-1)

    v = gate_c * val
    vbuf[CONV_HALO:CONV_HALO + tt, :] = v
    wc = wconv_ref[...]
    y = (wc[0:1, :] * vbuf[CONV_HALO - 2:CONV_HALO - 2 + tt, :]
         + wc[1:2, :] * vbuf[CONV_HALO - 1:CONV_HALO - 1 + tt, :]
         + wc[2:3, :] * v)
    conv_out = gate_b * y

    x1, h2, logits = _mix_tail(x, pool_in, conv_out, g1, sc2, sh2, n2_ref[...], wpool_ref,
                               pscale_ref[...], wout_ref[...], wr_ref[...], br_ref[...])
    x1_ref[...] = x1
    _store_token_tiles(h2_ref, h2)
    lg_ref[...] = logits

    ubuf[0:POOL_HALO, :] = ubuf[tt:tt + POOL_HALO, :]
    vbuf[0:CONV_HALO, :] = vbuf[tt:tt + CONV_HALO, :]

    @pl.when(t == pl.num_programs(1) - 1)
    def _():
        upool_ref[...] = ubuf[0:POOL_HALO, :]
        vconv_ref[...] = vbuf[0:CONV_HALO, :]


def _mixer_prompt(x, mod_p, n1, n2, w_in, w_pool, pscale, w_conv, w_out, w_r, b_r):
    b, t, d = x.shape
    dp = w_pool.shape[0] * w_pool.shape[1]
    tt = min(TOKEN_TILE, t)
    const2 = lambda i, j: (0, 0)
    const3 = lambda i, j: (0, 0, 0)
    return pl.pallas_call(
        _mixer_prompt_kernel,
        out_shape=(
            jax.ShapeDtypeStruct((b * t, d), F32),
            jax.ShapeDtypeStruct((b * t * SUBLANES, LANES), F32),
            jax.ShapeDtypeStruct((N_EXPERTS, b * t), F32),
            jax.ShapeDtypeStruct((b, POOL_HALO, dp), F32),
            jax.ShapeDtypeStruct((b, CONV_HALO, dp), F32),
        ),
        grid=(b, t // tt),
        in_specs=[
            pl.BlockSpec((None, tt, d), lambda i, j: (i, j, 0)),
            pl.BlockSpec((None, 6, d), lambda i, j: (i, 0, 0)),
            pl.BlockSpec((1, d), const2),
            pl.BlockSpec((1, d), const2),
            pl.BlockSpec(w_in.shape, const2),
            pl.BlockSpec(w_pool.shape, const3),
            pl.BlockSpec((1, dp), const2),
            pl.BlockSpec(w_conv.shape, const2),
            pl.BlockSpec(w_out.shape, const2),
            pl.BlockSpec(w_r.shape, const2),
            pl.BlockSpec((N_EXPERTS, 1), const2),
        ],
        out_specs=(
            pl.BlockSpec((tt, d), lambda i, j: (i * (t // tt) + j, 0)),
            pl.BlockSpec((tt * SUBLANES, LANES), lambda i, j: (i * (t // tt) + j, 0)),
            pl.BlockSpec((N_EXPERTS, tt), lambda i, j: (0, i * (t // tt) + j)),
            pl.BlockSpec((None, POOL_HALO, dp), lambda i, j: (i, 0, 0)),
            pl.BlockSpec((None, CONV_HALO, dp), lambda i, j: (i, 0, 0)),
        ),
        scratch_shapes=[
            pltpu.VMEM((POOL_HALO + tt, dp), F32),
            pltpu.VMEM((CONV_HALO + tt, dp), F32),
        ],
        compiler_params=pltpu.CompilerParams(
            dimension_semantics=("arbitrary", "arbitrary"), vmem_limit_bytes=VMEM_LIMIT),
        name="mixer_prompt",
    )(x, mod_p, n1, n2, w_in, w_pool, pscale, w_conv, w_out, w_r, b_r)


def _mixer_sample_kernel(x_ref, mod_ref, pstate_ref, cstate_ref, n1_ref, n2_ref, win_ref, wpool_ref,
                         pscale_ref, wconv_ref, wout_ref, wr_ref, br_ref,
                         x1_ref, h2_ref, lg_ref, newp_ref, newc_ref, *, steps):
    nb = mod_ref.shape[0]
    d = x_ref.shape[1]
    dp = pstate_ref.shape[2]
    gw = dp // len(POOL_WINDOWS)
    n_hist = pstate_ref.shape[0]
    n_chist = cstate_ref.shape[0]

    x = x_ref[...]
    mod = mod_ref[...]
    rep = lambda a: jnp.concatenate([a] * steps, axis=0)
    sh1, sc1, g1, sh2, sc2, _ = [rep(mod[:, i * d:(i + 1) * d]) for i in range(6)]
    h = _rmsnorm(x, n1_ref[...]) * (1.0 + sc1) + sh1
    z = _dot(h.astype(BF16), win_ref[...])
    u, gate_b, gate_c, val = [z[:, i * dp:(i + 1) * dp] for i in range(4)]

    ext = [pstate_ref[i] for i in range(n_hist)] + [u[s * nb:(s + 1) * nb, :] for s in range(steps)]
    pooled_steps = []
    for s in range(steps):
        groups = []
        for g, w in enumerate(POOL_WINDOWS):
            cols = slice(g * gw, (g + 1) * gw)
            acc = ext[n_hist + s][:, cols]
            for j in range(1, w):
                acc = acc + ext[n_hist + s - j][:, cols]
            cnt = float(min(PAST_LEN + s + 1, w))
            groups.append(acc / cnt - ext[n_hist + s][:, cols])
        pooled_steps.append(jnp.concatenate(groups, axis=-1))
    pool_in = jnp.concatenate(pooled_steps, axis=0)

    v = gate_c * val
    vext = [cstate_ref[i] for i in range(n_chist)] + [v[s * nb:(s + 1) * nb, :] for s in range(steps)]
    wc = wconv_ref[...]
    y = jnp.concatenate(
        [wc[0:1, :] * vext[s] + wc[1:2, :] * vext[s + 1] + wc[2:3, :] * vext[s + 2] for s in range(steps)],
        axis=0)
    conv_out = gate_b * y

    x1, h2, logits = _mix_tail(x, pool_in, conv_out, g1, sc2, sh2, n2_ref[...], wpool_ref,
                               pscale_ref[...], wout_ref[...], wr_ref[...], br_ref[...])
    x1_ref[...] = x1
    _store_token_tiles(h2_ref, h2)
    lg_ref[...] = logits
    for i in range(n_hist):
        newp_ref[i] = ext[steps + i]
    for i in range(n_chist):
        newc_ref[i] = vext[steps + i]


def _mixer_sample(x_tm, mod_s, pstate_tm, cstate_tm, n1, n2, w_in, w_pool, pscale, w_conv, w_out, w_r, b_r,
                  steps):
    rows, d = x_tm.shape
    return pl.pallas_call(
        functools.partial(_mixer_sample_kernel, steps=steps),
        out_shape=(
            jax.ShapeDtypeStruct((rows, d), F32),
            jax.ShapeDtypeStruct((rows * SUBLANES, LANES), F32),
            jax.ShapeDtypeStruct((N_EXPERTS, rows), F32),
            jax.ShapeDtypeStruct(pstate_tm.shape, F32),
            jax.ShapeDtypeStruct(cstate_tm.shape, F32),
        ),
        compiler_params=pltpu.CompilerParams(vmem_limit_bytes=VMEM_LIMIT),
        name="mixer_sample",
    )(x_tm, mod_s, pstate_tm, cstate_tm, n1, n2, w_in, w_pool, pscale, w_conv, w_out, w_r, b_r)


def _route_kernel(lgp_ref, lgs_ref, dest_ref, gate_ref, cnt_ref, counts, start, *, n_prompt_tiles):
    phase = pl.program_id(0)
    i = pl.program_id(1)
    ne, tr = lgp_ref.shape
    reps = tr // LANES

    @pl.when((phase == 0) & (i == 0))
    def _():
        counts[...] = jnp.zeros_like(counts)

    @pl.when((phase == 1) & (i == 0))
    def _():
        c = counts[...]
        hi = jnp.floor(c * (1.0 / 256.0))
        lo = c - hi * 256.0
        r = lax.broadcasted_iota(I32, (ne, ne), 0)
        col = lax.broadcasted_iota(I32, (ne, ne), 1)
        lower = jnp.where(col < r, 1.0, 0.0).astype(BF16)
        start[...] = 256.0 * _dot(lower, hi.astype(BF16)) + _dot(lower, lo.astype(BF16))
        cnt_ref[...] = c
        counts[...] = jnp.zeros_like(counts)

    work = jnp.where(i < n_prompt_tiles, lgp_ref[...], lgs_ref[...])
    eidx = lax.broadcasted_iota(I32, (ne, tr), 0)
    top_v, onehots = [], []
    for _ in range(TOP_K):
        m = jnp.max(work, axis=0, keepdims=True)
        idx = jnp.min(jnp.where(work == m, eidx, ne), axis=0, keepdims=True)
        sel = eidx == idx
        top_v.append(m)
        onehots.append(sel)
        work = jnp.where(sel, -jnp.inf, work)
    mask = jnp.where(onehots[0] | onehots[1] | onehots[2] | onehots[3], 1.0, 0.0)
    tile_counts = jnp.broadcast_to(jnp.sum(mask, axis=1, keepdims=True), (ne, LANES))

    @pl.when(phase == 0)
    def _():
        counts[...] = counts[...] + tile_counts

    @pl.when(phase == 1)
    def _():
        r = lax.broadcasted_iota(I32, (tr, tr), 0)
        col = lax.broadcasted_iota(I32, (tr, tr), 1)
        before = jnp.where(r < col, 1.0, 0.0).astype(BF16)
        base = jnp.concatenate([counts[...] + start[...]] * reps, axis=1)
        rank = _dot(mask.astype(BF16), before) + base
        counts[...] = counts[...] + tile_counts
        es = [jnp.exp(v - top_v[0]) for v in top_v]
        denom = es[0] + es[1] + es[2] + es[3]
        gate_ref[...] = jnp.zeros_like(gate_ref)
        for k in range(TOP_K):
            d = jnp.sum(jnp.where(onehots[k], rank, 0.0), axis=0, keepdims=True)
            dest_ref[k:k + 1, :] = d.astype(I32)
            gate_ref[k:k + 1, :] = es[k] / denom


def _route(lgt_p, lgt_s):
    ne, n_p = lgt_p.shape
    n_s = lgt_s.shape[1]
    tr = ROUTE_TILE
    npt, nst = n_p // tr, n_s // tr
    return pl.pallas_call(
        functools.partial(_route_kernel, n_prompt_tiles=npt),
        out_shape=(
            jax.ShapeDtypeStruct((TOP_K, n_p + n_s), I32),
            jax.ShapeDtypeStruct((SUBLANES, n_p + n_s), F32),
            jax.ShapeDtypeStruct((ne, LANES), F32),
        ),
        grid=(2, npt + nst),
        in_specs=[
            pl.BlockSpec((ne, tr), lambda p, i: (0, jnp.minimum(i, npt - 1))),
            pl.BlockSpec((ne, tr), lambda p, i: (0, jnp.maximum(i - npt, 0))),
        ],
        out_specs=(
            pl.BlockSpec((TOP_K, tr), lambda p, i: (0, i * p)),
            pl.BlockSpec((SUBLANES, tr), lambda p, i: (0, i * p)),
            pl.BlockSpec((ne, LANES), lambda p, i: (0, 0)),
        ),
        scratch_shapes=[pltpu.VMEM((ne, LANES), F32), pltpu.VMEM((ne, LANES), F32)],
        compiler_params=pltpu.CompilerParams(
            dimension_semantics=("arbitrary", "arbitrary"), vmem_limit_bytes=VMEM_LIMIT),
        name="route",
    )(lgt_p, lgt_s)


def _group_metadata(counts, n_rows, tile):
    n_tiles = n_rows // tile
    n_steps = n_tiles + N_EXPERTS - 1
    ends = jnp.cumsum(counts)
    offs = jnp.concatenate([jnp.zeros((1,), I32), ends]).astype(I32)
    first_tile = offs[:-1] // tile
    last_tile = (ends - 1) // tile
    tiles_e = jnp.where(counts > 0, last_tile - first_tile + 1, 0)
    step_end = jnp.cumsum(tiles_e)
    step_start = step_end - tiles_e
    n_active = step_end[-1]
    s = jnp.minimum(jnp.arange(n_steps, dtype=I32), n_active - 1)
    owner = ((s[:, None] >= step_start[None, :]) & (s[:, None] < step_end[None, :])).astype(I32)
    gid = jnp.sum(owner * jnp.arange(N_EXPERTS, dtype=I32)[None, :], axis=1)
    tid = jnp.sum(owner * (first_tile - step_start)[None, :], axis=1) + s
    ids = jnp.arange(N_EXPERTS, dtype=I32)
    later = (ids[None, :] > ids[:, None]) & (counts[None, :] > 0)
    next_e = jnp.min(jnp.where(later, ids[None, :], N_EXPERTS), axis=1)
    next_e = jnp.where(next_e == N_EXPERTS, -1, next_e)
    nxt = jnp.sum(owner * next_e[None, :], axis=1)
    return gid, tid, nxt, offs, n_active.reshape(1).astype(I32), n_steps


def _dispatch_kernel(dest_ref, hp_ref, hs_ref, rows_ref, sem, *, n_prompt_tiles):
    i = pl.program_id(0)
    tm = dest_ref.shape[1]

    def scatter(src_ref):
        def start_copy(t, k, d):
            src = src_ref.at[pl.ds(pl.multiple_of(t * SUBLANES, SUBLANES), SUBLANES)]
            dst = rows_ref.at[pl.ds(pl.multiple_of(d * SUBLANES, SUBLANES), SUBLANES)]
            pltpu.make_async_copy(src, dst, sem).start(priority=k % 2)
        _for_each_token_slot(tm, dest_ref, start_copy)
        for _ in range(TOP_K):
            pltpu.make_async_copy(src_ref, rows_ref.at[pl.ds(0, tm * SUBLANES)], sem).wait()

    @pl.when(i < n_prompt_tiles)
    def _():
        scatter(hp_ref)

    @pl.when(i >= n_prompt_tiles)
    def _():
        scatter(hs_ref)


def _dispatch(dest, h2_p, h2_s):
    n_p = h2_p.shape[0] // SUBLANES
    n_s = h2_s.shape[0] // SUBLANES
    tm = DISPATCH_TILE
    npt, nst = n_p // tm, n_s // tm
    return pl.pallas_call(
        functools.partial(_dispatch_kernel, n_prompt_tiles=npt),
        out_shape=jax.ShapeDtypeStruct(((n_p + n_s) * TOP_K * SUBLANES, LANES), h2_p.dtype),
        grid=(npt + nst,),
        in_specs=[
            pl.BlockSpec((TOP_K, tm), lambda i: (0, i), memory_space=pltpu.SMEM),
            pl.BlockSpec((tm * SUBLANES, LANES), lambda i: (jnp.minimum(i, npt - 1), 0)),
            pl.BlockSpec((tm * SUBLANES, LANES), lambda i: (jnp.maximum(i - npt, 0), 0)),
        ],
        out_specs=pl.BlockSpec(memory_space=pl.ANY),
        scratch_shapes=[pltpu.SemaphoreType.DMA],
        compiler_params=pltpu.CompilerParams(
            dimension_semantics=("arbitrary",), vmem_limit_bytes=VMEM_LIMIT),
        name="dispatch",
    )(dest, h2_p, h2_s)


def _experts_kernel(gid_ref, tid_ref, nxt_ref, offs_ref, nact_ref,
                    rows_ref, bg_ref, bu_ref, bd_ref, wg_hbm, wu_hbm, wd_hbm, out_ref,
                    wg_b, wu_b, wd_b, wg_f, wu_f, wd_f, relay, sems):
    s = pl.program_id(0)
    tm = rows_ref.shape[0] // SUBLANES
    landing = ((wg_hbm, wg_f, wg_b), (wu_hbm, wu_f, wu_b), (wd_hbm, wd_f, wd_b))

    def fetch(e):
        for j, (hbm, land, _) in enumerate(landing):
            pltpu.make_async_copy(hbm.at[e], land, sems.at[j]).start()

    @pl.when(s < nact_ref[0])
    def _():
        e = gid_ref[s]
        m = tid_ref[s]

        @pl.when(s == 0)
        def _():
            fetch(e)

        @pl.when((s == 0) | (gid_ref[jnp.maximum(s - 1, 0)] != e))
        def _():
            for j, (hbm, land, half) in enumerate(landing):
                pltpu.make_async_copy(hbm.at[e], land, sems.at[j]).wait()
                half[...] = land[...].astype(BF16)

            @pl.when(nxt_ref[s] >= 0)
            def _():
                fetch(nxt_ref[s])

        def ffn(x):
            g = _dot(x, wg_b[...]) + bg_ref[...]
            u = _dot(x, wu_b[...]) + bu_ref[...]
            g = jnp.minimum(g, SWIGLU_LIMIT)
            u = jnp.clip(u, -SWIGLU_LIMIT, SWIGLU_LIMIT)
            glu = g * jax.nn.sigmoid(SWIGLU_ALPHA * g)
            return _dot(((u + 1.0) * glu).astype(BF16), wd_b[...]) + bd_ref[...]

        lo = offs_ref[e]
        hi = offs_ref[e + 1]
        whole_tile = (lo <= m * tm) & (hi >= (m + 1) * tm)

        @pl.when(whole_tile)
        def _():
            _store_token_tiles(out_ref, ffn(_load_token_tiles(rows_ref).astype(BF16)))

        @pl.when(jnp.logical_not(whole_tile))
        def _():
            sub = relay.shape[0] // SUBLANES
            for j in range(tm // sub):
                first = m * tm + j * sub

                @pl.when((lo < first + sub) & (hi > first))
                def _():
                    span = pl.ds(j * sub * SUBLANES, sub * SUBLANES)
                    _store_token_tiles(relay, ffn(_load_token_tiles(rows_ref.at[span]).astype(BF16)))
                    row = first + lax.shift_right_logical(lax.broadcasted_iota(I32, relay.shape, 0), 3)
                    pltpu.store(out_ref.at[span], relay[...], mask=(row >= lo) & (row < hi))


def _experts(gid, tid, nxt, offs, nact, n_steps, rows, wg, bg, wu, bu, wd, bd):
    ne, d, f = wg.shape
    tm = EXPERT_TILE
    tile_map = lambda s, gid, tid, nxt, offs, nact: (tid[s], 0)
    bias_map = lambda s, gid, tid, nxt, offs, nact: (gid[s], 0, 0)
    grid_spec = pltpu.PrefetchScalarGridSpec(
        num_scalar_prefetch=5,
        grid=(n_steps,),
        in_specs=[
            pl.BlockSpec((tm * SUBLANES, LANES), tile_map),
            pl.BlockSpec((None, 1, f), bias_map),
            pl.BlockSpec((None, 1, f), bias_map),
            pl.BlockSpec((None, 1, d), bias_map),
            pl.BlockSpec(memory_space=pl.ANY),
            pl.BlockSpec(memory_space=pl.ANY),
            pl.BlockSpec(memory_space=pl.ANY),
        ],
        out_specs=pl.BlockSpec((tm * SUBLANES, LANES), tile_map),
        scratch_shapes=[
            pltpu.VMEM((d, f), BF16), pltpu.VMEM((d, f), BF16), pltpu.VMEM((f, d), BF16),
            pltpu.VMEM((d, f), F32), pltpu.VMEM((d, f), F32), pltpu.VMEM((f, d), F32),
            pltpu.VMEM((EXPERT_SUBTILE * SUBLANES, LANES), F32),
            pltpu.SemaphoreType.DMA((3,)),
        ],
    )
    return pl.pallas_call(
        _experts_kernel,
        out_shape=jax.ShapeDtypeStruct(rows.shape, F32),
        grid_spec=grid_spec,
        compiler_params=pltpu.CompilerParams(
            dimension_semantics=("arbitrary",), vmem_limit_bytes=VMEM_LIMIT),
        name="experts",
    )(gid, tid, nxt, offs, nact, rows, bg.reshape(ne, 1, f), bu.reshape(ne, 1, f), bd.reshape(ne, 1, d),
      wg, wu, wd)


def _combine_kernel(dest_ref, dnext_ref, gate_ref, x1p_ref, x1s_ref, modp_ref, g2s_ref, fn_ref, y_hbm,
                    outp_ref, outs_ref, gbuf, sems, *, n_prompt_tiles):
    i = pl.program_id(0)
    n = pl.num_programs(0)
    tm = x1p_ref.shape[0]
    slot = lax.rem(i, 2)

    def start_gather(idx_ref, into):
        def start_copy(t, k, d):
            src = y_hbm.at[pl.ds(pl.multiple_of(d * SUBLANES, SUBLANES), SUBLANES)]
            dst = gbuf.at[into, k, pl.ds(pl.multiple_of(t * SUBLANES, SUBLANES), SUBLANES)]
            pltpu.make_async_copy(src, dst, sems.at[into]).start(priority=k % 2)
        _for_each_token_slot(tm, idx_ref, start_copy)

    @pl.when(i == 0)
    def _():
        start_gather(dest_ref, 0)

    @pl.when(i + 1 < n)
    def _():
        start_gather(dnext_ref, 1 - slot)

    for k in range(TOP_K):
        pltpu.make_async_copy(y_hbm.at[pl.ds(0, tm * SUBLANES)], gbuf.at[slot, k], sems.at[slot]).wait()

    gates = gate_ref[...].T
    cols = []
    for c in range(SUBLANES):
        acc = gates[:, 0:1] * gbuf[slot, 0, pl.ds(c, tm, stride=SUBLANES), :]
        for k in range(1, TOP_K):
            acc = acc + gates[:, k:k + 1] * gbuf[slot, k, pl.ds(c, tm, stride=SUBLANES), :]
        cols.append(acc)
    ffn = jnp.concatenate(cols, axis=-1)

    @pl.when(i < n_prompt_tiles)
    def _():
        g2 = modp_ref[...][5:6, :]
        outp_ref[...] = _rmsnorm(x1p_ref[...] + g2 * ffn, fn_ref[...])

    @pl.when(i >= n_prompt_tiles)
    def _():
        outs_ref[...] = _rmsnorm(x1s_ref[...] + g2s_ref[...] * ffn, fn_ref[...])


def _combine(dest, gates, x1_p, x1_s, mod_p, g2_s, final_norm, y_rows, tokens_per_seq):
    n_p, d = x1_p.shape
    n_s = x1_s.shape[0]
    tm = COMBINE_TILE
    npt, nst = n_p // tm, n_s // tm
    tiles_per_seq = tokens_per_seq // tm
    pmap = lambda i: (jnp.minimum(i, npt - 1), 0)
    smap = lambda i: (jnp.maximum(i - npt, 0), 0)
    return pl.pallas_call(
        functools.partial(_combine_kernel, n_prompt_tiles=npt),
        out_shape=(jax.ShapeDtypeStruct((n_p, d), F32), jax.ShapeDtypeStruct((n_s, d), F32)),
        grid=(npt + nst,),
        in_specs=[
            pl.BlockSpec((TOP_K, tm), lambda i: (0, i), memory_space=pltpu.SMEM),
            pl.BlockSpec((TOP_K, tm), lambda i: (0, jnp.minimum(i + 1, npt + nst - 1)),
                         memory_space=pltpu.SMEM),
            pl.BlockSpec((SUBLANES, tm), lambda i: (0, i)),
            pl.BlockSpec((tm, d), pmap),
            pl.BlockSpec((tm, d), smap),
            pl.BlockSpec((None, 6, d), lambda i: (jnp.minimum(i, npt - 1) // tiles_per_seq, 0, 0)),
            pl.BlockSpec((tm, d), smap),
            pl.BlockSpec((1, d), lambda i: (0, 0)),
            pl.BlockSpec(memory_space=pl.ANY),
        ],
        out_specs=(pl.BlockSpec((tm, d), pmap), pl.BlockSpec((tm, d), smap)),
        scratch_shapes=[pltpu.VMEM((2, TOP_K, tm * SUBLANES, LANES), F32), pltpu.SemaphoreType.DMA((2,))],
        compiler_params=pltpu.CompilerParams(
            dimension_semantics=("arbitrary",), vmem_limit_bytes=VMEM_LIMIT),
        name="combine",
    )(dest, dest, gates, x1_p, x1_s, mod_p, g2_s, final_norm, y_rows)


def kernel(x_prompt, x_sample, state_pool, state_conv, c_prompt, c_sample, norm1, norm2, w_ada, b_ada,
           w_in, w_pool, pool_scale, w_conv, w_out, w_router, b_router, w_gate, b_gate, w_up, b_up,
           w_down, b_down, final_norm):
    depth = norm1.shape[0]
    assert depth == 1, "single-layer step"
    bp, tp, d = x_prompt.shape
    bs, ts, _ = x_sample.shape
    dp = state_pool.shape[-1]
    n_hist = state_pool.shape[2]
    n_chist = state_conv.shape[2]
    n_p, n_s = bp * tp, bs * ts
    assert d == SUBLANES * LANES, "token-tile layout assumes one vreg tile per token row"
    assert tp % TOKEN_TILE == 0 and tp % COMBINE_TILE == 0 and n_s % COMBINE_TILE == 0
    assert n_p % DISPATCH_TILE == 0 and n_s % DISPATCH_TILE == 0
    assert (n_p + n_s) % ROUTE_TILE == 0 and ((n_p + n_s) * TOP_K) % EXPERT_TILE == 0

    l = 0
    n1 = norm1[l].reshape(1, d)
    n2 = norm2[l].reshape(1, d)
    w_in_b = w_in[l].astype(BF16)
    w_pool_b = w_pool[l].astype(BF16)
    w_out_b = w_out[l].astype(BF16)
    pscale = pool_scale[l].reshape(1, dp)
    w_r = w_router[l].T.astype(BF16)
    b_r = b_router[l].reshape(N_EXPERTS, 1)

    mod = _adaln(jnp.concatenate([c_sample, c_prompt], axis=0), w_ada[l], b_ada[l])
    mod_s = mod[:bs]
    mod_p = mod[bs:].reshape(bp, 6, d)

    x1_p, h2_p, lg_p, u_tail, v_tail = _mixer_prompt(
        x_prompt, mod_p, n1, n2, w_in_b, w_pool_b, pscale, w_conv[l], w_out_b, w_r, b_r)

    xs_tm = jnp.transpose(x_sample, (1, 0, 2)).reshape(n_s, d)
    ps_tm = jnp.transpose(state_pool[l], (1, 0, 2))
    cs_tm = jnp.transpose(state_conv[l], (1, 0, 2))
    x1_s, h2_s, lg_s, newp_tm, newc_tm = _mixer_sample(
        xs_tm, mod_s, ps_tm, cs_tm, n1, n2, w_in_b, w_pool_b, pscale, w_conv[l], w_out_b, w_r, b_r, ts)

    dest, gates, counts_f = _route(lg_p, lg_s)
    counts = counts_f[:, 0].astype(I32)
    n_rows = (n_p + n_s) * TOP_K
    gid, tid, nxt, offs, nact, n_steps = _group_metadata(counts, n_rows, EXPERT_TILE)

    rows = _dispatch(dest, h2_p, h2_s)
    y_rows = _experts(gid, tid, nxt, offs, nact, n_steps, rows,
                      w_gate[l], b_gate[l], w_up[l], b_up[l], w_down[l], b_down[l])

    g2_s = jnp.tile(mod_s[:, 5 * d:], (ts, 1))
    y_p, y_s = _combine(dest, gates, x1_p, x1_s, mod_p, g2_s, final_norm.reshape(1, d), y_rows, tp)

    y_prompt = y_p.reshape(bp, tp, d)
    y_sample = jnp.transpose(y_s.reshape(ts, bs, d), (1, 0, 2))
    new_pool_prompt = u_tail[:, POOL_HALO - n_hist:, :][None]
    new_conv_prompt = v_tail[:, CONV_HALO - n_chist:, :][None]
    new_pool_sample = jnp.transpose(newp_tm, (1, 0, 2))[None]
    new_conv_sample = jnp.transpose(newc_tm, (1, 0, 2))[None]
    return (y_prompt, y_sample, new_pool_prompt, new_conv_prompt, new_pool_sample, new_conv_sample)
```

```python
import functools

import jax
import jax.numpy as jnp
from jax import lax
from jax.experimental import pallas as pl
from jax.experimental.pallas import tpu as pltpu

F32 = jnp.float32
BF16 = jnp.bfloat16
I32 = jnp.int32

POOL_WINDOWS = (2, 4, 8, 16)
POOL_HALO = 16
CONV_TAPS = 3
CONV_HALO = 8
N_EXPERTS = 32
TOP_K = 4
SWIGLU_LIMIT = 7.0
SWIGLU_ALPHA = 1.702
EPS = 1e-5
PAST_LEN = 16384

LANES = 128
SUBLANES = 8

TOKEN_TILE = 512
ROUTE_TILE = 512
DISPATCH_TILE = 512
COMBINE_TILE = 256
COMBINE_GROUP = 32
EXPERT_TILE = 512
EXPERT_SUBTILE = 128
VMEM_LIMIT = 56 * 1024 * 1024


def _rmsnorm(x, g):
    ms = jnp.mean(x * x, axis=-1, keepdims=True)
    return x * lax.rsqrt(ms + EPS) * g


def _dot(a, b):
    return jnp.dot(a, b, preferred_element_type=F32)


def _store_token_tiles(ref, val):
    rows = val.shape[0]
    for c in range(SUBLANES):
        ref[pl.ds(c, rows, stride=SUBLANES), :] = val[:, c * LANES:(c + 1) * LANES]


def _load_token_tiles(ref):
    rows = ref.shape[0] // SUBLANES
    return jnp.concatenate([ref[pl.ds(c, rows, stride=SUBLANES), :] for c in range(SUBLANES)], axis=-1)


ISSUE_GROUP = 4


def _slot_index(idx_ref, c, k, lane):
    return idx_ref[(c * TOP_K + k) * LANES + lane]


def _for_each_token_slot(idx_ref, start_copy):
    for c in range(idx_ref.shape[0] // (TOP_K * LANES)):
        def group(g, carry, c=c):
            l0 = g * ISSUE_GROUP
            idx = [[_slot_index(idx_ref, c, k, l0 + j) for k in range(TOP_K)] for j in range(ISSUE_GROUP)]
            for j in range(ISSUE_GROUP):
                for k in range(TOP_K):
                    start_copy(c * LANES + l0 + j, k, idx[j][k])
            return carry
        lax.fori_loop(0, LANES // ISSUE_GROUP, group, 0)


def _adaln_kernel(c_ref, w_ref, b_ref, o_ref):
    c = c_ref[...]
    s = c * jax.nn.sigmoid(c)
    o_ref[...] = _dot(s.astype(BF16), w_ref[...].astype(BF16)) + b_ref[...]


def _adaln(c, w_ada, b_ada):
    rows, d = c.shape
    n = w_ada.shape[1]
    tn = 1024
    return pl.pallas_call(
        _adaln_kernel,
        out_shape=jax.ShapeDtypeStruct((rows, n), F32),
        grid=(n // tn,),
        in_specs=[
            pl.BlockSpec((rows, d), lambda j: (0, 0)),
            pl.BlockSpec((d, tn), lambda j: (0, j)),
            pl.BlockSpec((1, tn), lambda j: (0, j)),
        ],
        out_specs=pl.BlockSpec((rows, tn), lambda j: (0, j)),
        compiler_params=pltpu.CompilerParams(
            dimension_semantics=("arbitrary",), vmem_limit_bytes=VMEM_LIMIT),
        name="adaln",
    )(c, w_ada, b_ada.reshape(1, n))


def _mix_tail(x, pool_in, conv_out, g1, sc2, sh2, n2, wpool_ref, pscale, wout, wr, br):
    gw = pool_in.shape[1] // len(POOL_WINDOWS)
    mixed = [_dot(pool_in[:, g * gw:(g + 1) * gw].astype(BF16), wpool_ref[g])
             for g in range(len(POOL_WINDOWS))]
    pool_out = jnp.concatenate(mixed, axis=-1) * pscale
    mix_in = jnp.concatenate([pool_out, conv_out], axis=-1).astype(BF16)
    x1 = x + g1 * _dot(mix_in, wout)
    h2 = _rmsnorm(x1, n2) * (1.0 + sc2) + sh2
    logits_t = lax.dot_general(wr, h2.astype(BF16), (((1,), (1,)), ((), ())),
                               preferred_element_type=F32) + br
    return x1, h2, logits_t


def _mixer_prompt_kernel(x_ref, mod_ref, n1_ref, n2_ref, win_ref, wpool_ref, pscale_ref, wconv_ref,
                         wout_ref, wr_ref, br_ref,
                         x1_ref, h2_ref, lg_ref, upool_ref, vconv_ref, ubuf, vbuf):
    tt = x_ref.shape[0]
    dp = ubuf.shape[1]
    gw = dp // len(POOL_WINDOWS)
    t = pl.program_id(1)

    @pl.when(t == 0)
    def _():
        ubuf[0:POOL_HALO, :] = jnp.zeros((POOL_HALO, dp), F32)
        vbuf[0:CONV_HALO, :] = jnp.zeros((CONV_HALO, dp), F32)

    x = x_ref[...]
    mod = mod_ref[...]
    sh1, sc1, g1, sh2, sc2, _ = [mod[i:i + 1, :] for i in range(6)]
    h = _rmsnorm(x, n1_ref[...]) * (1.0 + sc1) + sh1
    z = _dot(h.astype(BF16), win_ref[...])
    u, gate_b, gate_c, val = [z[:, i * dp:(i + 1) * dp] for i in range(4)]

    ubuf[POOL_HALO:POOL_HALO + tt, :] = u
    pos = lax.broadcasted_iota(I32, (tt, gw), 0) + t * tt
    pooled = []
    for g, w in enumerate(POOL_WINDOWS):
        cols = slice(g * gw, (g + 1) * gw)
        acc = u[:, cols]
        for j in range(1, w):
            acc = acc + ubuf[POOL_HALO - j:POOL_HALO - j + tt, cols]
        cnt = jnp.minimum(pos + 1, w).astype(F32)
        pooled.append(acc / cnt - u[:, cols])
    pool_in = jnp.concatenate(pooled, axis=-1)

    v = gate_c * val
    vbuf[CONV_HALO:CONV_HALO + tt, :] = v
    wc = wconv_ref[...]
    y = (wc[0:1, :] * vbuf[CONV_HALO - 2:CONV_HALO - 2 + tt, :]
         + wc[1:2, :] * vbuf[CONV_HALO - 1:CONV_HALO - 1 + tt, :]
         + wc[2:3, :] * v)
    conv_out = gate_b * y

    x1, h2, logits = _mix_tail(x, pool_in, conv_out, g1, sc2, sh2, n2_ref[...], wpool_ref,
                               pscale_ref[...], wout_ref[...], wr_ref[...], br_ref[...])
    x1_ref[...] = x1
    _store_token_tiles(h2_ref, h2)
    lg_ref[...] = logits

    ubuf[0:POOL_HALO, :] = ubuf[tt:tt + POOL_HALO, :]
    vbuf[0:CONV_HALO, :] = vbuf[tt:tt + CONV_HALO, :]

    @pl.when(t == pl.num_programs(1) - 1)
    def _():
        upool_ref[...] = ubuf[0:POOL_HALO, :]
        vconv_ref[...] = vbuf[0:CONV_HALO, :]


def _mixer_prompt(x, mod_p, n1, n2, w_in, w_pool, pscale, w_conv, w_out, w_r, b_r):
    b, t, d = x.shape
    dp = w_pool.shape[0] * w_pool.shape[1]
    tt = min(TOKEN_TILE, t)
    const2 = lambda i, j: (0, 0)
    const3 = lambda i, j: (0, 0, 0)
    return pl.pallas_call(
        _mixer_prompt_kernel,
        out_shape=(
            jax.ShapeDtypeStruct((b * t, d), F32),
            jax.ShapeDtypeStruct((b * t * SUBLANES, LANES), F32),
            jax.ShapeDtypeStruct((N_EXPERTS, b * t), F32),
            jax.ShapeDtypeStruct((b, POOL_HALO, dp), F32),
            jax.ShapeDtypeStruct((b, CONV_HALO, dp), F32),
        ),
        grid=(b, t // tt),
        in_specs=[
            pl.BlockSpec((None, tt, d), lambda i, j: (i, j, 0)),
            pl.BlockSpec((None, 6, d), lambda i, j: (i, 0, 0)),
            pl.BlockSpec((1, d), const2),
            pl.BlockSpec((1, d), const2),
            pl.BlockSpec(w_in.shape, const2),
            pl.BlockSpec(w_pool.shape, const3),
            pl.BlockSpec((1, dp), const2),
            pl.BlockSpec(w_conv.shape, const2),
            pl.BlockSpec(w_out.shape, const2),
            pl.BlockSpec(w_r.shape, const2),
            pl.BlockSpec((N_EXPERTS, 1), const2),
        ],
        out_specs=(
            pl.BlockSpec((tt, d), lambda i, j: (i * (t // tt) + j, 0)),
            pl.BlockSpec((tt * SUBLANES, LANES), lambda i, j: (i * (t // tt) + j, 0)),
            pl.BlockSpec((N_EXPERTS, tt), lambda i, j: (0, i * (t // tt) + j)),
            pl.BlockSpec((None, POOL_HALO, dp), lambda i, j: (i, 0, 0)),
            pl.BlockSpec((None, CONV_HALO, dp), lambda i, j: (i, 0, 0)),
        ),
        scratch_shapes=[
            pltpu.VMEM((POOL_HALO + tt, dp), F32),
            pltpu.VMEM((CONV_HALO + tt, dp), F32),
        ],
        compiler_params=pltpu.CompilerParams(
            dimension_semantics=("arbitrary", "arbitrary"), vmem_limit_bytes=VMEM_LIMIT),
        name="mixer_prompt",
    )(x, mod_p, n1, n2, w_in, w_pool, pscale, w_conv, w_out, w_r, b_r)


def _mixer_sample_kernel(x_ref, mod_ref, pstate_ref, cstate_ref, n1_ref, n2_ref, win_ref, wpool_ref,
                         pscale_ref, wconv_ref, wout_ref, wr_ref, br_ref,
                         x1_ref, h2_ref, lg_ref, newp_ref, newc_ref, *, steps):
    nb = mod_ref.shape[0]
    d = x_ref.shape[1]
    dp = pstate_ref.shape[2]
    gw = dp // len(POOL_WINDOWS)
    n_hist = pstate_ref.shape[0]
    n_chist = cstate_ref.shape[0]

    x = x_ref[...]
    mod = mod_ref[...]
    rep = lambda a: jnp.concatenate([a] * steps, axis=0)
    sh1, sc1, g1, sh2, sc2, _ = [rep(mod[:, i * d:(i + 1) * d]) for i in range(6)]
    h = _rmsnorm(x, n1_ref[...]) * (1.0 + sc1) + sh1
    z = _dot(h.astype(BF16), win_ref[...])
    u, gate_b, gate_c, val = [z[:, i * dp:(i + 1) * dp] for i in range(4)]

    ext = [pstate_ref[i] for i in range(n_hist)] + [u[s * nb:(s + 1) * nb, :] for s in range(steps)]
    pooled_steps = []
    for s in range(steps):
        groups = []
        for g, w in enumerate(POOL_WINDOWS):
            cols = slice(g * gw, (g + 1) * gw)
            acc = ext[n_hist + s][:, cols]
            for j in range(1, w):
                acc = acc + ext[n_hist + s - j][:, cols]
            cnt = float(min(PAST_LEN + s + 1, w))
            groups.append(acc / cnt - ext[n_hist + s][:, cols])
        pooled_steps.append(jnp.concatenate(groups, axis=-1))
    pool_in = jnp.concatenate(pooled_steps, axis=0)

    v = gate_c * val
    vext = [cstate_ref[i] for i in range(n_chist)] + [v[s * nb:(s + 1) * nb, :] for s in range(steps)]
    wc = wconv_ref[...]
    y = jnp.concatenate(
        [wc[0:1, :] * vext[s] + wc[1:2, :] * vext[s + 1] + wc[2:3, :] * vext[s + 2] for s in range(steps)],
        axis=0)
    conv_out = gate_b * y

    x1, h2, logits = _mix_tail(x, pool_in, conv_out, g1, sc2, sh2, n2_ref[...], wpool_ref,
                               pscale_ref[...], wout_ref[...], wr_ref[...], br_ref[...])
    x1_ref[...] = x1
    _store_token_tiles(h2_ref, h2)
    lg_ref[...] = logits
    for i in range(n_hist):
        newp_ref[i] = ext[steps + i]
    for i in range(n_chist):
        newc_ref[i] = vext[steps + i]


def _mixer_sample(x_tm, mod_s, pstate_tm, cstate_tm, n1, n2, w_in, w_pool, pscale, w_conv, w_out, w_r, b_r,
                  steps):
    rows, d = x_tm.shape
    return pl.pallas_call(
        functools.partial(_mixer_sample_kernel, steps=steps),
        out_shape=(
            jax.ShapeDtypeStruct((rows, d), F32),
            jax.ShapeDtypeStruct((rows * SUBLANES, LANES), F32),
            jax.ShapeDtypeStruct((N_EXPERTS, rows), F32),
            jax.ShapeDtypeStruct(pstate_tm.shape, F32),
            jax.ShapeDtypeStruct(cstate_tm.shape, F32),
        ),
        compiler_params=pltpu.CompilerParams(vmem_limit_bytes=VMEM_LIMIT),
        name="mixer_sample",
    )(x_tm, mod_s, pstate_tm, cstate_tm, n1, n2, w_in, w_pool, pscale, w_conv, w_out, w_r, b_r)


def _route_kernel(lgp_ref, lgs_ref, dest_ref, gate_ref, cnt_ref, counts, start, *, n_prompt_tiles):
    phase = pl.program_id(0)
    i = pl.program_id(1)
    ne, tr = lgp_ref.shape
    reps = tr // LANES

    @pl.when((phase == 0) & (i == 0))
    def _():
        counts[...] = jnp.zeros_like(counts)

    @pl.when((phase == 1) & (i == 0))
    def _():
        c = counts[...]
        hi = jnp.floor(c * (1.0 / 256.0))
        lo = c - hi * 256.0
        r = lax.broadcasted_iota(I32, (ne, ne), 0)
        col = lax.broadcasted_iota(I32, (ne, ne), 1)
        lower = jnp.where(col < r, 1.0, 0.0).astype(BF16)
        start[...] = 256.0 * _dot(lower, hi.astype(BF16)) + _dot(lower, lo.astype(BF16))
        cnt_ref[...] = c
        counts[...] = jnp.zeros_like(counts)

    work = jnp.where(i < n_prompt_tiles, lgp_ref[...], lgs_ref[...])
    eidx = lax.broadcasted_iota(I32, (ne, tr), 0)
    top_v, onehots = [], []
    for _ in range(TOP_K):
        m = jnp.max(work, axis=0, keepdims=True)
        idx = jnp.min(jnp.where(work == m, eidx, ne), axis=0, keepdims=True)
        sel = eidx == idx
        top_v.append(m)
        onehots.append(sel)
        work = jnp.where(sel, -jnp.inf, work)
    mask = jnp.where(onehots[0] | onehots[1] | onehots[2] | onehots[3], 1.0, 0.0)
    tile_counts = jnp.broadcast_to(jnp.sum(mask, axis=1, keepdims=True), (ne, LANES))

    @pl.when(phase == 0)
    def _():
        counts[...] = counts[...] + tile_counts

    @pl.when(phase == 1)
    def _():
        r = lax.broadcasted_iota(I32, (tr, tr), 0)
        col = lax.broadcasted_iota(I32, (tr, tr), 1)
        before = jnp.where(r < col, 1.0, 0.0).astype(BF16)
        base = jnp.concatenate([counts[...] + start[...]] * reps, axis=1)
        rank = _dot(mask.astype(BF16), before) + base
        counts[...] = counts[...] + tile_counts
        es = [jnp.exp(v - top_v[0]) for v in top_v]
        denom = es[0] + es[1] + es[2] + es[3]
        gate_ref[...] = jnp.zeros_like(gate_ref)
        for k in range(TOP_K):
            d = jnp.sum(jnp.where(onehots[k], rank, 0.0), axis=0, keepdims=True)
            for c in range(reps):
                dest_ref[c, k:k + 1, :] = d[:, c * LANES:(c + 1) * LANES].astype(I32)
            gate_ref[k:k + 1, :] = es[k] / denom


def _route(lgt_p, lgt_s):
    ne, n_p = lgt_p.shape
    n_s = lgt_s.shape[1]
    tr = ROUTE_TILE
    npt, nst = n_p // tr, n_s // tr
    return pl.pallas_call(
        functools.partial(_route_kernel, n_prompt_tiles=npt),
        out_shape=(
            jax.ShapeDtypeStruct(((n_p + n_s) // LANES, TOP_K, LANES), I32),
            jax.ShapeDtypeStruct((SUBLANES, n_p + n_s), F32),
            jax.ShapeDtypeStruct((ne, LANES), F32),
        ),
        grid=(2, npt + nst),
        in_specs=[
            pl.BlockSpec((ne, tr), lambda p, i: (0, jnp.minimum(i, npt - 1))),
            pl.BlockSpec((ne, tr), lambda p, i: (0, jnp.maximum(i - npt, 0))),
        ],
        out_specs=(
            pl.BlockSpec((tr // LANES, TOP_K, LANES), lambda p, i: (i * p, 0, 0)),
            pl.BlockSpec((SUBLANES, tr), lambda p, i: (0, i * p)),
            pl.BlockSpec((ne, LANES), lambda p, i: (0, 0)),
        ),
        scratch_shapes=[pltpu.VMEM((ne, LANES), F32), pltpu.VMEM((ne, LANES), F32)],
        compiler_params=pltpu.CompilerParams(
            dimension_semantics=("arbitrary", "arbitrary"), vmem_limit_bytes=VMEM_LIMIT),
        name="route",
    )(lgt_p, lgt_s)


def _group_metadata(counts, n_rows, tile):
    n_tiles = n_rows // tile
    n_steps = n_tiles + N_EXPERTS - 1
    ends = jnp.cumsum(counts)
    offs = jnp.concatenate([jnp.zeros((1,), I32), ends]).astype(I32)
    first_tile = offs[:-1] // tile
    last_tile = (ends - 1) // tile
    tiles_e = jnp.where(counts > 0, last_tile - first_tile + 1, 0)
    step_end = jnp.cumsum(tiles_e)
    step_start = step_end - tiles_e
    n_active = step_end[-1]
    s = jnp.minimum(jnp.arange(n_steps, dtype=I32), n_active - 1)
    owner = ((s[:, None] >= step_start[None, :]) & (s[:, None] < step_end[None, :])).astype(I32)
    gid = jnp.sum(owner * jnp.arange(N_EXPERTS, dtype=I32)[None, :], axis=1)
    tid = jnp.sum(owner * (first_tile - step_start)[None, :], axis=1) + s
    ids = jnp.arange(N_EXPERTS, dtype=I32)
    later = (ids[None, :] > ids[:, None]) & (counts[None, :] > 0)
    next_e = jnp.min(jnp.where(later, ids[None, :], N_EXPERTS), axis=1)
    next_e = jnp.where(next_e == N_EXPERTS, -1, next_e)
    nxt = jnp.sum(owner * next_e[None, :], axis=1)
    return gid, tid, nxt, offs, n_active.reshape(1).astype(I32), n_steps


def _dispatch_kernel(dest_ref, hp_ref, hs_ref, rows_ref, sem, *, n_prompt_tiles):
    i = pl.program_id(0)
    tm = dest_ref.shape[0] // TOP_K

    def scatter(src_ref):
        def start_copy(t, k, d):
            src = src_ref.at[pl.ds(pl.multiple_of(t * SUBLANES, SUBLANES), SUBLANES)]
            dst = rows_ref.at[pl.ds(pl.multiple_of(d * SUBLANES, SUBLANES), SUBLANES)]
            pltpu.make_async_copy(src, dst, sem).start(priority=k % 2)
        _for_each_token_slot(dest_ref, start_copy)
        for _ in range(TOP_K):
            pltpu.make_async_copy(src_ref, rows_ref.at[pl.ds(0, tm * SUBLANES)], sem).wait()

    @pl.when(i < n_prompt_tiles)
    def _():
        scatter(hp_ref)

    @pl.when(i >= n_prompt_tiles)
    def _():
        scatter(hs_ref)


def _dispatch(dest, h2_p, h2_s):
    n_p = h2_p.shape[0] // SUBLANES
    n_s = h2_s.shape[0] // SUBLANES
    tm = DISPATCH_TILE
    npt, nst = n_p // tm, n_s // tm
    return pl.pallas_call(
        functools.partial(_dispatch_kernel, n_prompt_tiles=npt),
        out_shape=jax.ShapeDtypeStruct(((n_p + n_s) * TOP_K * SUBLANES, LANES), h2_p.dtype),
        grid=(npt + nst,),
        in_specs=[
            pl.BlockSpec((tm * TOP_K,), lambda i: (i,), memory_space=pltpu.SMEM),
            pl.BlockSpec((tm * SUBLANES, LANES), lambda i: (jnp.minimum(i, npt - 1), 0)),
            pl.BlockSpec((tm * SUBLANES, LANES), lambda i: (jnp.maximum(i - npt, 0), 0)),
        ],
        out_specs=pl.BlockSpec(memory_space=pl.ANY),
        scratch_shapes=[pltpu.SemaphoreType.DMA],
        compiler_params=pltpu.CompilerParams(
            dimension_semantics=("arbitrary",), vmem_limit_bytes=VMEM_LIMIT),
        name="dispatch",
    )(dest, h2_p, h2_s)


def _experts_kernel(gid_ref, tid_ref, nxt_ref, offs_ref, nact_ref,
                    rows_ref, bg_ref, bu_ref, bd_ref, wg_hbm, wu_hbm, wd_hbm, out_ref,
                    wg_b, wu_b, wd_b, wg_f, wu_f, wd_f, relay, sems):
    s = pl.program_id(0)
    tm = rows_ref.shape[0] // SUBLANES
    landing = ((wg_hbm, wg_f, wg_b), (wu_hbm, wu_f, wu_b), (wd_hbm, wd_f, wd_b))

    def fetch(e):
        for j, (hbm, land, _) in enumerate(landing):
            pltpu.make_async_copy(hbm.at[e], land, sems.at[j]).start()

    @pl.when(s < nact_ref[0])
    def _():
        e = gid_ref[s]
        m = tid_ref[s]

        @pl.when(s == 0)
        def _():
            fetch(e)

        @pl.when((s == 0) | (gid_ref[jnp.maximum(s - 1, 0)] != e))
        def _():
            for j, (hbm, land, half) in enumerate(landing):
                pltpu.make_async_copy(hbm.at[e], land, sems.at[j]).wait()
                half[...] = land[...].astype(BF16)

            @pl.when(nxt_ref[s] >= 0)
            def _():
                fetch(nxt_ref[s])

        def ffn(x):
            g = _dot(x, wg_b[...]) + bg_ref[...]
            u = _dot(x, wu_b[...]) + bu_ref[...]
            g = jnp.minimum(g, SWIGLU_LIMIT)
            u = jnp.clip(u, -SWIGLU_LIMIT, SWIGLU_LIMIT)
            glu = g * jax.nn.sigmoid(SWIGLU_ALPHA * g)
            return _dot(((u + 1.0) * glu).astype(BF16), wd_b[...]) + bd_ref[...]

        lo = offs_ref[e]
        hi = offs_ref[e + 1]
        whole_tile = (lo <= m * tm) & (hi >= (m + 1) * tm)

        @pl.when(whole_tile)
        def _():
            _store_token_tiles(out_ref, ffn(_load_token_tiles(rows_ref).astype(BF16)))

        @pl.when(jnp.logical_not(whole_tile))
        def _():
            sub = relay.shape[0] // SUBLANES
            for j in range(tm // sub):
                first = m * tm + j * sub

                @pl.when((lo < first + sub) & (hi > first))
                def _():
                    span = pl.ds(j * sub * SUBLANES, sub * SUBLANES)
                    _store_token_tiles(relay, ffn(_load_token_tiles(rows_ref.at[span]).astype(BF16)))
                    row = first + lax.shift_right_logical(lax.broadcasted_iota(I32, relay.shape, 0), 3)
                    pltpu.store(out_ref.at[span], relay[...], mask=(row >= lo) & (row < hi))


def _experts(gid, tid, nxt, offs, nact, n_steps, rows, wg, bg, wu, bu, wd, bd):
    ne, d, f = wg.shape
    tm = EXPERT_TILE
    tile_map = lambda s, gid, tid, nxt, offs, nact: (tid[s], 0)
    bias_map = lambda s, gid, tid, nxt, offs, nact: (gid[s], 0, 0)
    grid_spec = pltpu.PrefetchScalarGridSpec(
        num_scalar_prefetch=5,
        grid=(n_steps,),
        in_specs=[
            pl.BlockSpec((tm * SUBLANES, LANES), tile_map),
            pl.BlockSpec((None, 1, f), bias_map),
            pl.BlockSpec((None, 1, f), bias_map),
            pl.BlockSpec((None, 1, d), bias_map),
            pl.BlockSpec(memory_space=pl.ANY),
            pl.BlockSpec(memory_space=pl.ANY),
            pl.BlockSpec(memory_space=pl.ANY),
        ],
        out_specs=pl.BlockSpec((tm * SUBLANES, LANES), tile_map),
        scratch_shapes=[
            pltpu.VMEM((d, f), BF16), pltpu.VMEM((d, f), BF16), pltpu.VMEM((f, d), BF16),
            pltpu.VMEM((d, f), F32), pltpu.VMEM((d, f), F32), pltpu.VMEM((f, d), F32),
            pltpu.VMEM((EXPERT_SUBTILE * SUBLANES, LANES), F32),
            pltpu.SemaphoreType.DMA((3,)),
        ],
    )
    return pl.pallas_call(
        _experts_kernel,
        out_shape=jax.ShapeDtypeStruct(rows.shape, F32),
        grid_spec=grid_spec,
        compiler_params=pltpu.CompilerParams(
            dimension_semantics=("arbitrary",), vmem_limit_bytes=VMEM_LIMIT),
        name="experts",
    )(gid, tid, nxt, offs, nact, rows, bg.reshape(ne, 1, f), bu.reshape(ne, 1, f), bd.reshape(ne, 1, d),
      wg, wu, wd)


def _combine_kernel(dest_ref, dnext_ref, gate_ref, x1p_ref, x1s_ref, modp_ref, g2s_ref, fn_ref, y_hbm,
                    outp_ref, outs_ref, gbuf_a, gbuf_b, gates_t, sems, *, n_prompt_tiles):
    i = pl.program_id(0)
    n = pl.num_programs(0)
    tm = x1p_ref.shape[0]
    group = COMBINE_GROUP

    def start_copy(buf, sem, t, k, d):
        src = y_hbm.at[pl.ds(pl.multiple_of(d * SUBLANES, SUBLANES), SUBLANES)]
        dst = buf.at[k, pl.ds(pl.multiple_of(t * SUBLANES, SUBLANES), SUBLANES)]
        pltpu.make_async_copy(src, dst, sem).start(priority=k % 2)

    def wait_tile(buf, sem):
        for k in range(TOP_K):
            pltpu.make_async_copy(y_hbm.at[pl.ds(0, tm * SUBLANES)], buf.at[k], sem).wait()

    def step(cur, cur_sem, nxt, nxt_sem):
        @pl.when(i == 0)
        def _():
            _for_each_token_slot(dest_ref, functools.partial(start_copy, cur, cur_sem))

        wait_tile(cur, cur_sem)
        gates_t[...] = gate_ref[...].T
        fn = fn_ref[...]

        def run(x1_ref, out_ref, g2_rows):
            def body(g, carry, c):
                l0 = g * group
                base = pl.multiple_of(c * LANES + l0, group)
                idx = [[_slot_index(dnext_ref, c, k, l0 + j) for k in range(TOP_K)] for j in range(group)]
                for j in range(group):
                    for k in range(TOP_K):
                        start_copy(nxt, nxt_sem, base + j, k, idx[j][k])
                gt = gates_t[pl.ds(base, group), :]
                tile0 = pl.multiple_of(base * SUBLANES, group * SUBLANES)
                cols = []
                for c in range(SUBLANES):
                    acc = gt[:, 0:1] * cur[0, pl.ds(tile0 + c, group, stride=SUBLANES), :]
                    for k in range(1, TOP_K):
                        acc = acc + gt[:, k:k + 1] * cur[k, pl.ds(tile0 + c, group, stride=SUBLANES), :]
                    cols.append(acc)
                ffn = jnp.concatenate(cols, axis=-1)
                rows = pl.ds(base, group)
                out_ref[rows, :] = _rmsnorm(x1_ref[rows, :] + g2_rows(rows) * ffn, fn)
                return carry
            for c in range(tm // LANES):
                lax.fori_loop(0, LANES // group, functools.partial(body, c=c), 0)

        @pl.when(i < n_prompt_tiles)
        def _():
            g2 = modp_ref[...][5:6, :]
            run(x1p_ref, outp_ref, lambda rows: g2)

        @pl.when(i >= n_prompt_tiles)
        def _():
            run(x1s_ref, outs_ref, lambda rows: g2s_ref[rows, :])

        @pl.when(i == n - 1)
        def _():
            wait_tile(nxt, nxt_sem)

    @pl.when(lax.rem(i, 2) == 0)
    def _():
        step(gbuf_a, sems.at[0], gbuf_b, sems.at[1])

    @pl.when(lax.rem(i, 2) == 1)
    def _():
        step(gbuf_b, sems.at[1], gbuf_a, sems.at[0])


def _combine(dest, gates, x1_p, x1_s, mod_p, g2_s, final_norm, y_rows, tokens_per_seq):
    n_p, d = x1_p.shape
    n_s = x1_s.shape[0]
    tm = COMBINE_TILE
    npt, nst = n_p // tm, n_s // tm
    tiles_per_seq = tokens_per_seq // tm
    pmap = lambda i: (jnp.minimum(i, npt - 1), 0)
    smap = lambda i: (jnp.maximum(i - npt, 0), 0)
    return pl.pallas_call(
        functools.partial(_combine_kernel, n_prompt_tiles=npt),
        out_shape=(jax.ShapeDtypeStruct((n_p, d), F32), jax.ShapeDtypeStruct((n_s, d), F32)),
        grid=(npt + nst,),
        in_specs=[
            pl.BlockSpec((tm * TOP_K,), lambda i: (i,), memory_space=pltpu.SMEM),
            pl.BlockSpec((tm * TOP_K,), lambda i: (jnp.minimum(i + 1, npt + nst - 1),),
                         memory_space=pltpu.SMEM),
            pl.BlockSpec((SUBLANES, tm), lambda i: (0, i)),
            pl.BlockSpec((tm, d), pmap),
            pl.BlockSpec((tm, d), smap),
            pl.BlockSpec((None, 6, d), lambda i: (jnp.minimum(i, npt - 1) // tiles_per_seq, 0, 0)),
            pl.BlockSpec((tm, d), smap),
            pl.BlockSpec((1, d), lambda i: (0, 0)),
            pl.BlockSpec(memory_space=pl.ANY),
        ],
        out_specs=(pl.BlockSpec((tm, d), pmap), pl.BlockSpec((tm, d), smap)),
        scratch_shapes=[
            pltpu.VMEM((TOP_K, tm * SUBLANES, LANES), F32),
            pltpu.VMEM((TOP_K, tm * SUBLANES, LANES), F32),
            pltpu.VMEM((tm, SUBLANES), F32),
            pltpu.SemaphoreType.DMA((2,)),
        ],
        compiler_params=pltpu.CompilerParams(
            dimension_semantics=("arbitrary",), vmem_limit_bytes=VMEM_LIMIT),
        name="combine",
    )(dest, dest, gates, x1_p, x1_s, mod_p, g2_s, final_norm, y_rows)


def kernel(x_prompt, x_sample, state_pool, state_conv, c_prompt, c_sample, norm1, norm2, w_ada, b_ada,
           w_in, w_pool, pool_scale, w_conv, w_out, w_router, b_router, w_gate, b_gate, w_up, b_up,
           w_down, b_down, final_norm):
    depth = norm1.shape[0]
    assert depth == 1, "single-layer step"
    bp, tp, d = x_prompt.shape
    bs, ts, _ = x_sample.shape
    dp = state_pool.shape[-1]
    n_hist = state_pool.shape[2]
    n_chist = state_conv.shape[2]
    n_p, n_s = bp * tp, bs * ts
    assert d == SUBLANES * LANES, "token-tile layout assumes one vreg tile per token row"
    assert tp % TOKEN_TILE == 0 and tp % COMBINE_TILE == 0 and n_s % COMBINE_TILE == 0
    assert n_p % DISPATCH_TILE == 0 and n_s % DISPATCH_TILE == 0
    assert (n_p + n_s) % ROUTE_TILE == 0 and ((n_p + n_s) * TOP_K) % EXPERT_TILE == 0

    l = 0
    n1 = norm1[l].reshape(1, d)
    n2 = norm2[l].reshape(1, d)
    w_in_b = w_in[l].astype(BF16)
    w_pool_b = w_pool[l].astype(BF16)
    w_out_b = w_out[l].astype(BF16)
    pscale = pool_scale[l].reshape(1, dp)
    w_r = w_router[l].T.astype(BF16)
    b_r = b_router[l].reshape(N_EXPERTS, 1)

    mod = _adaln(jnp.concatenate([c_sample, c_prompt], axis=0), w_ada[l], b_ada[l])
    mod_s = mod[:bs]
    mod_p = mod[bs:].reshape(bp, 6, d)

    x1_p, h2_p, lg_p, u_tail, v_tail = _mixer_prompt(
        x_prompt, mod_p, n1, n2, w_in_b, w_pool_b, pscale, w_conv[l], w_out_b, w_r, b_r)

    xs_tm = jnp.transpose(x_sample, (1, 0, 2)).reshape(n_s, d)
    ps_tm = jnp.transpose(state_pool[l], (1, 0, 2))
    cs_tm = jnp.transpose(state_conv[l], (1, 0, 2))
    x1_s, h2_s, lg_s, newp_tm, newc_tm = _mixer_sample(
        xs_tm, mod_s, ps_tm, cs_tm, n1, n2, w_in_b, w_pool_b, pscale, w_conv[l], w_out_b, w_r, b_r, ts)

    dest, gates, counts_f = _route(lg_p, lg_s)
    counts = counts_f[:, 0].astype(I32)
    n_rows = (n_p + n_s) * TOP_K
    gid, tid, nxt, offs, nact, n_steps = _group_metadata(counts, n_rows, EXPERT_TILE)

    dest = dest.reshape(-1)
    rows = _dispatch(dest, h2_p, h2_s)
    y_rows = _experts(gid, tid, nxt, offs, nact, n_steps, rows,
                      w_gate[l], b_gate[l], w_up[l], b_up[l], w_down[l], b_down[l])

    g2_s = jnp.tile(mod_s[:, 5 * d:], (ts, 1))
    y_p, y_s = _combine(dest, gates, x1_p, x1_s, mod_p, g2_s, final_norm.reshape(1, d), y_rows, tp)

    y_prompt = y_p.reshape(bp, tp, d)
    y_sample = jnp.transpose(y_s.reshape(ts, bs, d), (1, 0, 2))
    new_pool_prompt = u_tail[:, POOL_HALO - n_hist:, :][None]
    new_conv_prompt = v_tail[:, CONV_HALO - n_chist:, :][None]
    new_pool_sample = jnp.transpose(newp_tm, (1, 0, 2))[None]
    new_conv_sample = jnp.transpose(newc_tm, (1, 0, 2))[None]
    return (y_prompt, y_sample, new_pool_prompt, new_conv_prompt, new_pool_sample, new_conv_sample)
```

```python
import functools

import jax
import jax.numpy as jnp
from jax import lax
from jax.experimental import pallas as pl
from jax.experimental.pallas import tpu as pltpu

F32 = jnp.float32
BF16 = jnp.bfloat16
I32 = jnp.int32

POOL_WINDOWS = (2, 4, 8, 16)
POOL_HALO = 16
CONV_TAPS = 3
CONV_HALO = 8
N_EXPERTS = 32
TOP_K = 4
SWIGLU_LIMIT = 7.0
SWIGLU_ALPHA = 1.702
EPS = 1e-5
PAST_LEN = 16384

LANES = 128
SUBLANES = 8

TOKEN_TILE = 512
ROUTE_TILE = 512
DISPATCH_TILE = 512
COMBINE_TILE = 512
EXPERT_TILE = 512
EXPERT_SUBTILE = 128
VMEM_LIMIT = 56 * 1024 * 1024


def _rmsnorm(x, g):
    ms = jnp.mean(x * x, axis=-1, keepdims=True)
    return x * lax.rsqrt(ms + EPS) * g


def _dot(a, b):
    return jnp.dot(a, b, preferred_element_type=F32)


def _store_token_tiles(ref, val):
    rows = val.shape[0]
    for c in range(SUBLANES):
        ref[pl.ds(c, rows, stride=SUBLANES), :] = val[:, c * LANES:(c + 1) * LANES]


def _load_token_tiles(ref):
    rows = ref.shape[0] // SUBLANES
    return jnp.concatenate([ref[pl.ds(c, rows, stride=SUBLANES), :] for c in range(SUBLANES)], axis=-1)


ISSUE_GROUP = 4


def _slot_index(idx_ref, c, k, lane):
    return idx_ref[(c * TOP_K + k) * LANES + lane]


def _for_each_token_slot(idx_ref, start_copy):
    for c in range(idx_ref.shape[0] // (TOP_K * LANES)):
        def group(g, carry, c=c):
            l0 = g * ISSUE_GROUP
            idx = [[_slot_index(idx_ref, c, k, l0 + j) for k in range(TOP_K)] for j in range(ISSUE_GROUP)]
            for j in range(ISSUE_GROUP):
                for k in range(TOP_K):
                    start_copy(c * LANES + l0 + j, k, idx[j][k])
            return carry
        lax.fori_loop(0, LANES // ISSUE_GROUP, group, 0)


def _adaln_kernel(c_ref, w_ref, b_ref, o_ref):
    c = c_ref[...]
    s = c * jax.nn.sigmoid(c)
    o_ref[...] = _dot(s.astype(BF16), w_ref[...].astype(BF16)) + b_ref[...]


def _adaln(c, w_ada, b_ada):
    rows, d = c.shape
    n = w_ada.shape[1]
    tn = 1024
    return pl.pallas_call(
        _adaln_kernel,
        out_shape=jax.ShapeDtypeStruct((rows, n), F32),
        grid=(n // tn,),
        in_specs=[
            pl.BlockSpec((rows, d), lambda j: (0, 0)),
            pl.BlockSpec((d, tn), lambda j: (0, j)),
            pl.BlockSpec((1, tn), lambda j: (0, j)),
        ],
        out_specs=pl.BlockSpec((rows, tn), lambda j: (0, j)),
        compiler_params=pltpu.CompilerParams(
            dimension_semantics=("arbitrary",), vmem_limit_bytes=VMEM_LIMIT),
        name="adaln",
    )(c, w_ada, b_ada.reshape(1, n))


def _mix_tail(x, pool_in, conv_out, g1, sc2, sh2, n2, wpool_ref, pscale, wout, wr, br):
    gw = pool_in.shape[1] // len(POOL_WINDOWS)
    mixed = [_dot(pool_in[:, g * gw:(g + 1) * gw].astype(BF16), wpool_ref[g])
             for g in range(len(POOL_WINDOWS))]
    pool_out = jnp.concatenate(mixed, axis=-1) * pscale
    mix_in = jnp.concatenate([pool_out, conv_out], axis=-1).astype(BF16)
    x1 = x + g1 * _dot(mix_in, wout)
    h2 = _rmsnorm(x1, n2) * (1.0 + sc2) + sh2
    logits_t = lax.dot_general(wr, h2.astype(BF16), (((1,), (1,)), ((), ())),
                               preferred_element_type=F32) + br
    return x1, h2, logits_t


def _mixer_prompt_kernel(x_ref, mod_ref, n1_ref, n2_ref, win_ref, wpool_ref, pscale_ref, wconv_ref,
                         wout_ref, wr_ref, br_ref,
                         x1_ref, h2_ref, lg_ref, upool_ref, vconv_ref, ubuf, vbuf):
    tt = x_ref.shape[0]
    dp = ubuf.shape[1]
    gw = dp // len(POOL_WINDOWS)
    t = pl.program_id(1)

    @pl.when(t == 0)
    def _():
        ubuf[0:POOL_HALO, :] = jnp.zeros((POOL_HALO, dp), F32)
        vbuf[0:CONV_HALO, :] = jnp.zeros((CONV_HALO, dp), F32)

    x = x_ref[...]
    mod = mod_ref[...]
    sh1, sc1, g1, sh2, sc2, _ = [mod[i:i + 1, :] for i in range(6)]
    h = _rmsnorm(x, n1_ref[...]) * (1.0 + sc1) + sh1
    z = _dot(h.astype(BF16), win_ref[...])
    u, gate_b, gate_c, val = [z[:, i * dp:(i + 1) * dp] for i in range(4)]

    ubuf[POOL_HALO:POOL_HALO + tt, :] = u
    pos = lax.broadcasted_iota(I32, (tt, gw), 0) + t * tt
    pooled = []
    for g, w in enumerate(POOL_WINDOWS):
        cols = slice(g * gw, (g + 1) * gw)
        acc = u[:, cols]
        for j in range(1, w):
            acc = acc + ubuf[POOL_HALO - j:POOL_HALO - j + tt, cols]
        cnt = jnp.minimum(pos + 1, w).astype(F32)
        pooled.append(acc / cnt - u[:, cols])
    pool_in = jnp.concatenate(pooled, axis=-1)

    v = gate_c * val
    vbuf[CONV_HALO:CONV_HALO + tt, :] = v
    wc = wconv_ref[...]
    y = (wc[0:1, :] * vbuf[CONV_HALO - 2:CONV_HALO - 2 + tt, :]
         + wc[1:2, :] * vbuf[CONV_HALO - 1:CONV_HALO - 1 + tt, :]
         + wc[2:3, :] * v)
    conv_out = gate_b * y

    x1, h2, logits = _mix_tail(x, pool_in, conv_out, g1, sc2, sh2, n2_ref[...], wpool_ref,
                               pscale_ref[...], wout_ref[...], wr_ref[...], br_ref[...])
    x1_ref[...] = x1
    _store_token_tiles(h2_ref, h2)
    lg_ref[...] = logits

    ubuf[0:POOL_HALO, :] = ubuf[tt:tt + POOL_HALO, :]
    vbuf[0:CONV_HALO, :] = vbuf[tt:tt + CONV_HALO, :]

    @pl.when(t == pl.num_programs(1) - 1)
    def _():
        upool_ref[...] = ubuf[0:POOL_HALO, :]
        vconv_ref[...] = vbuf[0:CONV_HALO, :]


def _mixer_prompt(x, mod_p, n1, n2, w_in, w_pool, pscale, w_conv, w_out, w_r, b_r):
    b, t, d = x.shape
    dp = w_pool.shape[0] * w_pool.shape[1]
    tt = min(TOKEN_TILE, t)
    const2 = lambda i, j: (0, 0)
    const3 = lambda i, j: (0, 0, 0)
    return pl.pallas_call(
        _mixer_prompt_kernel,
        out_shape=(
            jax.ShapeDtypeStruct((b * t, d), F32),
            jax.ShapeDtypeStruct((b * t * SUBLANES, LANES), F32),
            jax.ShapeDtypeStruct((N_EXPERTS, b * t), F32),
            jax.ShapeDtypeStruct((b, POOL_HALO, dp), F32),
            jax.ShapeDtypeStruct((b, CONV_HALO, dp), F32),
        ),
        grid=(b, t // tt),
        in_specs=[
            pl.BlockSpec((None, tt, d), lambda i, j: (i, j, 0)),
            pl.BlockSpec((None, 6, d), lambda i, j: (i, 0, 0)),
            pl.BlockSpec((1, d), const2),
            pl.BlockSpec((1, d), const2),
            pl.BlockSpec(w_in.shape, const2),
            pl.BlockSpec(w_pool.shape, const3),
            pl.BlockSpec((1, dp), const2),
            pl.BlockSpec(w_conv.shape, const2),
            pl.BlockSpec(w_out.shape, const2),
            pl.BlockSpec(w_r.shape, const2),
            pl.BlockSpec((N_EXPERTS, 1), const2),
        ],
        out_specs=(
            pl.BlockSpec((tt, d), lambda i, j: (i * (t // tt) + j, 0)),
            pl.BlockSpec((tt * SUBLANES, LANES), lambda i, j: (i * (t // tt) + j, 0)),
            pl.BlockSpec((N_EXPERTS, tt), lambda i, j: (0, i * (t // tt) + j)),
            pl.BlockSpec((None, POOL_HALO, dp), lambda i, j: (i, 0, 0)),
            pl.BlockSpec((None, CONV_HALO, dp), lambda i, j: (i, 0, 0)),
        ),
        scratch_shapes=[
            pltpu.VMEM((POOL_HALO + tt, dp), F32),
            pltpu.VMEM((CONV_HALO + tt, dp), F32),
        ],
        compiler_params=pltpu.CompilerParams(
            dimension_semantics=("arbitrary", "arbitrary"), vmem_limit_bytes=VMEM_LIMIT),
        name="mixer_prompt",
    )(x, mod_p, n1, n2, w_in, w_pool, pscale, w_conv, w_out, w_r, b_r)


def _mixer_sample_kernel(x_ref, mod_ref, pstate_ref, cstate_ref, n1_ref, n2_ref, win_ref, wpool_ref,
                         pscale_ref, wconv_ref, wout_ref, wr_ref, br_ref,
                         x1_ref, h2_ref, lg_ref, newp_ref, newc_ref, *, steps):
    nb = pstate_ref.shape[1]
    d = x_ref.shape[1]
    dp = pstate_ref.shape[2]
    gw = dp // len(POOL_WINDOWS)
    n_hist = pstate_ref.shape[0]
    n_chist = cstate_ref.shape[0]

    x = x_ref[...]
    mod = mod_ref[0:nb, :]
    rep = lambda a: jnp.concatenate([a] * steps, axis=0)
    sh1, sc1, g1, sh2, sc2, _ = [rep(mod[:, i * d:(i + 1) * d]) for i in range(6)]
    h = _rmsnorm(x, n1_ref[...]) * (1.0 + sc1) + sh1
    z = _dot(h.astype(BF16), win_ref[...])
    u, gate_b, gate_c, val = [z[:, i * dp:(i + 1) * dp] for i in range(4)]

    ext = [pstate_ref[i] for i in range(n_hist)] + [u[s * nb:(s + 1) * nb, :] for s in range(steps)]
    pooled_steps = []
    for s in range(steps):
        groups = []
        for g, w in enumerate(POOL_WINDOWS):
            cols = slice(g * gw, (g + 1) * gw)
            acc = ext[n_hist + s][:, cols]
            for j in range(1, w):
                acc = acc + ext[n_hist + s - j][:, cols]
            cnt = float(min(PAST_LEN + s + 1, w))
            groups.append(acc / cnt - ext[n_hist + s][:, cols])
        pooled_steps.append(jnp.concatenate(groups, axis=-1))
    pool_in = jnp.concatenate(pooled_steps, axis=0)

    v = gate_c * val
    vext = [cstate_ref[i] for i in range(n_chist)] + [v[s * nb:(s + 1) * nb, :] for s in range(steps)]
    wc = wconv_ref[...]
    y = jnp.concatenate(
        [wc[0:1, :] * vext[s] + wc[1:2, :] * vext[s + 1] + wc[2:3, :] * vext[s + 2] for s in range(steps)],
        axis=0)
    conv_out = gate_b * y

    x1, h2, logits = _mix_tail(x, pool_in, conv_out, g1, sc2, sh2, n2_ref[...], wpool_ref,
                               pscale_ref[...], wout_ref[...], wr_ref[...], br_ref[...])
    x1_ref[...] = x1
    _store_token_tiles(h2_ref, h2)
    lg_ref[...] = logits
    for i in range(n_hist):
        newp_ref[i] = ext[steps + i]
    for i in range(n_chist):
        newc_ref[i] = vext[steps + i]


def _mixer_sample(x_tm, mod_s, pstate_tm, cstate_tm, n1, n2, w_in, w_pool, pscale, w_conv, w_out, w_r, b_r,
                  steps):
    rows, d = x_tm.shape
    return pl.pallas_call(
        functools.partial(_mixer_sample_kernel, steps=steps),
        out_shape=(
            jax.ShapeDtypeStruct((rows, d), F32),
            jax.ShapeDtypeStruct((rows * SUBLANES, LANES), F32),
            jax.ShapeDtypeStruct((N_EXPERTS, rows), F32),
            jax.ShapeDtypeStruct(pstate_tm.shape, F32),
            jax.ShapeDtypeStruct(cstate_tm.shape, F32),
        ),
        compiler_params=pltpu.CompilerParams(vmem_limit_bytes=VMEM_LIMIT),
        name="mixer_sample",
    )(x_tm, mod_s, pstate_tm, cstate_tm, n1, n2, w_in, w_pool, pscale, w_conv, w_out, w_r, b_r)


def _route_kernel(lgp_ref, lgs_ref, dest_ref, gate_ref, cnt_ref, counts, start, before):
    ne = lgp_ref.shape[0]
    tr = before.shape[0]
    reps = tr // LANES
    n_prompt_chunks = lgp_ref.shape[1] // tr
    n_sample_chunks = lgs_ref.shape[1] // tr
    eidx = lax.broadcasted_iota(I32, (ne, tr), 0)

    def top_k(ref, c):
        work = ref[:, pl.ds(pl.multiple_of(c * tr, tr), tr)]
        top_v, onehots = [], []
        for _ in range(TOP_K):
            m = jnp.max(work, axis=0, keepdims=True)
            idx = jnp.min(jnp.where(work == m, eidx, ne), axis=0, keepdims=True)
            sel = eidx == idx
            top_v.append(m)
            onehots.append(sel)
            work = jnp.where(sel, -jnp.inf, work)
        mask = jnp.where(onehots[0] | onehots[1] | onehots[2] | onehots[3], 1.0, 0.0)
        chunk_counts = jnp.broadcast_to(jnp.sum(mask, axis=1, keepdims=True), (ne, LANES))
        return top_v, onehots, mask, chunk_counts

    def count_chunk(ref):
        def body(c, carry):
            counts[...] = counts[...] + top_k(ref, c)[3]
            return carry
        return body

    counts[...] = jnp.zeros_like(counts)
    lax.fori_loop(0, n_prompt_chunks, count_chunk(lgp_ref), 0)
    lax.fori_loop(0, n_sample_chunks, count_chunk(lgs_ref), 0)

    total = counts[...]
    hi = jnp.floor(total * (1.0 / 256.0))
    lo = total - hi * 256.0
    r = lax.broadcasted_iota(I32, (ne, ne), 0)
    col = lax.broadcasted_iota(I32, (ne, ne), 1)
    lower = jnp.where(col < r, 1.0, 0.0).astype(BF16)
    start[...] = 256.0 * _dot(lower, hi.astype(BF16)) + _dot(lower, lo.astype(BF16))
    cnt_ref[...] = total
    counts[...] = jnp.zeros_like(counts)

    r = lax.broadcasted_iota(I32, (tr, tr), 0)
    col = lax.broadcasted_iota(I32, (tr, tr), 1)
    before[...] = jnp.where(r < col, 1.0, 0.0).astype(BF16)
    gate_ref[...] = jnp.zeros_like(gate_ref)

    def place_chunk(ref, first_chunk):
        def body(c, carry):
            top_v, onehots, mask, chunk_counts = top_k(ref, c)
            base = jnp.concatenate([counts[...] + start[...]] * reps, axis=1)
            rank = _dot(mask.astype(BF16), before[...]) + base
            counts[...] = counts[...] + chunk_counts
            es = [jnp.exp(v - top_v[0]) for v in top_v]
            denom = es[0] + es[1] + es[2] + es[3]
            chunk = first_chunk + c
            cols = pl.ds(pl.multiple_of(chunk * tr, tr), tr)
            for k in range(TOP_K):
                d = jnp.sum(jnp.where(onehots[k], rank, 0.0), axis=0, keepdims=True).astype(I32)
                for j in range(reps):
                    dest_ref[chunk * reps + j, k:k + 1, :] = d[:, j * LANES:(j + 1) * LANES]
                gate_ref[k:k + 1, cols] = es[k] / denom
            return carry
        return body

    lax.fori_loop(0, n_prompt_chunks, place_chunk(lgp_ref, 0), 0)
    lax.fori_loop(0, n_sample_chunks, place_chunk(lgs_ref, n_prompt_chunks), 0)


def _route(lgt_p, lgt_s):
    ne, n_p = lgt_p.shape
    n_s = lgt_s.shape[1]
    tr = ROUTE_TILE
    return pl.pallas_call(
        _route_kernel,
        out_shape=(
            jax.ShapeDtypeStruct(((n_p + n_s) // LANES, TOP_K, LANES), I32),
            jax.ShapeDtypeStruct((SUBLANES, n_p + n_s), F32),
            jax.ShapeDtypeStruct((ne, LANES), F32),
        ),
        scratch_shapes=[pltpu.VMEM((ne, LANES), F32), pltpu.VMEM((ne, LANES), F32),
                        pltpu.VMEM((tr, tr), BF16)],
        compiler_params=pltpu.CompilerParams(vmem_limit_bytes=VMEM_LIMIT),
        name="route",
    )(lgt_p, lgt_s)


def _group_metadata(counts, n_rows, tile):
    n_tiles = n_rows // tile
    n_steps = n_tiles + N_EXPERTS - 1
    ends = jnp.cumsum(counts)
    offs = jnp.concatenate([jnp.zeros((1,), I32), ends]).astype(I32)
    first_tile = offs[:-1] // tile
    last_tile = (ends - 1) // tile
    tiles_e = jnp.where(counts > 0, last_tile - first_tile + 1, 0)
    step_end = jnp.cumsum(tiles_e)
    step_start = step_end - tiles_e
    n_active = step_end[-1]
    s = jnp.minimum(jnp.arange(n_steps, dtype=I32), n_active - 1)
    owner = ((s[:, None] >= step_start[None, :]) & (s[:, None] < step_end[None, :])).astype(I32)
    gid = jnp.sum(owner * jnp.arange(N_EXPERTS, dtype=I32)[None, :], axis=1)
    tid = jnp.sum(owner * (first_tile - step_start)[None, :], axis=1) + s
    ids = jnp.arange(N_EXPERTS, dtype=I32)
    later = (ids[None, :] > ids[:, None]) & (counts[None, :] > 0)
    next_e = jnp.min(jnp.where(later, ids[None, :], N_EXPERTS), axis=1)
    next_e = jnp.where(next_e == N_EXPERTS, -1, next_e)
    nxt = jnp.sum(owner * next_e[None, :], axis=1)
    return gid, tid, nxt, offs, n_active.reshape(1).astype(I32), n_steps


def _dispatch_kernel(dest_ref, hp_ref, hs_ref, rows_ref, sem, *, n_prompt_tiles):
    i = pl.program_id(0)
    tm = dest_ref.shape[0] // TOP_K

    def scatter(src_ref):
        def start_copy(t, k, d):
            src = src_ref.at[pl.ds(pl.multiple_of(t * SUBLANES, SUBLANES), SUBLANES)]
            dst = rows_ref.at[pl.ds(pl.multiple_of(d * SUBLANES, SUBLANES), SUBLANES)]
            pltpu.make_async_copy(src, dst, sem).start(priority=k % 2)
        _for_each_token_slot(dest_ref, start_copy)
        for _ in range(TOP_K):
            pltpu.make_async_copy(src_ref, rows_ref.at[pl.ds(0, tm * SUBLANES)], sem).wait()

    @pl.when(i < n_prompt_tiles)
    def _():
        scatter(hp_ref)

    @pl.when(i >= n_prompt_tiles)
    def _():
        scatter(hs_ref)


def _dispatch(dest, h2_p, h2_s):
    n_p = h2_p.shape[0] // SUBLANES
    n_s = h2_s.shape[0] // SUBLANES
    tm = DISPATCH_TILE
    npt, nst = n_p // tm, n_s // tm
    return pl.pallas_call(
        functools.partial(_dispatch_kernel, n_prompt_tiles=npt),
        out_shape=jax.ShapeDtypeStruct(((n_p + n_s) * TOP_K * SUBLANES, LANES), h2_p.dtype),
        grid=(npt + nst,),
        in_specs=[
            pl.BlockSpec((tm * TOP_K,), lambda i: (i,), memory_space=pltpu.SMEM),
            pl.BlockSpec((tm * SUBLANES, LANES), lambda i: (jnp.minimum(i, npt - 1), 0)),
            pl.BlockSpec((tm * SUBLANES, LANES), lambda i: (jnp.maximum(i - npt, 0), 0)),
        ],
        out_specs=pl.BlockSpec(memory_space=pl.ANY),
        scratch_shapes=[pltpu.SemaphoreType.DMA],
        compiler_params=pltpu.CompilerParams(
            dimension_semantics=("arbitrary",), vmem_limit_bytes=VMEM_LIMIT),
        name="dispatch",
    )(dest, h2_p, h2_s)


def _experts_kernel(gid_ref, tid_ref, nxt_ref, offs_ref, nact_ref,
                    rows_ref, bg_ref, bu_ref, bd_ref, wg_hbm, wu_hbm, wd_hbm, out_ref,
                    wg_b, wu_b, wd_b, wg_f, wu_f, wd_f, relay, sems):
    s = pl.program_id(0)
    tm = rows_ref.shape[0] // SUBLANES
    landing = ((wg_hbm, wg_f, wg_b), (wu_hbm, wu_f, wu_b), (wd_hbm, wd_f, wd_b))

    def fetch(e):
        for j, (hbm, land, _) in enumerate(landing):
            pltpu.make_async_copy(hbm.at[e], land, sems.at[j]).start()

    @pl.when(s < nact_ref[0])
    def _():
        e = gid_ref[s]
        m = tid_ref[s]

        @pl.when(s == 0)
        def _():
            fetch(e)

        @pl.when((s == 0) | (gid_ref[jnp.maximum(s - 1, 0)] != e))
        def _():
            for j, (hbm, land, half) in enumerate(landing):
                pltpu.make_async_copy(hbm.at[e], land, sems.at[j]).wait()
                half[...] = land[...].astype(BF16)

            @pl.when(nxt_ref[s] >= 0)
            def _():
                fetch(nxt_ref[s])

        def ffn(x):
            g = _dot(x, wg_b[...]) + bg_ref[pl.ds(e, 1), :]
            u = _dot(x, wu_b[...]) + bu_ref[pl.ds(e, 1), :]
            g = jnp.minimum(g, SWIGLU_LIMIT)
            u = jnp.clip(u, -SWIGLU_LIMIT, SWIGLU_LIMIT)
            glu = g * jax.nn.sigmoid(SWIGLU_ALPHA * g)
            return _dot(((u + 1.0) * glu).astype(BF16), wd_b[...]) + bd_ref[pl.ds(e, 1), :]

        lo = offs_ref[e]
        hi = offs_ref[e + 1]
        whole_tile = (lo <= m * tm) & (hi >= (m + 1) * tm)

        @pl.when(whole_tile)
        def _():
            _store_token_tiles(out_ref, ffn(_load_token_tiles(rows_ref).astype(BF16)))

        @pl.when(jnp.logical_not(whole_tile))
        def _():
            sub = relay.shape[0] // SUBLANES
            for j in range(tm // sub):
                first = m * tm + j * sub

                @pl.when((lo < first + sub) & (hi > first))
                def _():
                    span = pl.ds(j * sub * SUBLANES, sub * SUBLANES)
                    _store_token_tiles(relay, ffn(_load_token_tiles(rows_ref.at[span]).astype(BF16)))
                    row = first + lax.shift_right_logical(lax.broadcasted_iota(I32, relay.shape, 0), 3)
                    pltpu.store(out_ref.at[span], relay[...], mask=(row >= lo) & (row < hi))


def _experts(gid, tid, nxt, offs, nact, n_steps, rows, wg, bg, wu, bu, wd, bd):
    ne, d, f = wg.shape
    tm = EXPERT_TILE
    tile_map = lambda s, gid, tid, nxt, offs, nact: (tid[s], 0)
    whole = lambda s, gid, tid, nxt, offs, nact: (0, 0)
    grid_spec = pltpu.PrefetchScalarGridSpec(
        num_scalar_prefetch=5,
        grid=(n_steps,),
        in_specs=[
            pl.BlockSpec((tm * SUBLANES, LANES), tile_map),
            pl.BlockSpec((ne, f), whole),
            pl.BlockSpec((ne, f), whole),
            pl.BlockSpec((ne, d), whole),
            pl.BlockSpec(memory_space=pl.ANY),
            pl.BlockSpec(memory_space=pl.ANY),
            pl.BlockSpec(memory_space=pl.ANY),
        ],
        out_specs=pl.BlockSpec((tm * SUBLANES, LANES), tile_map),
        scratch_shapes=[
            pltpu.VMEM((d, f), BF16), pltpu.VMEM((d, f), BF16), pltpu.VMEM((f, d), BF16),
            pltpu.VMEM((d, f), F32), pltpu.VMEM((d, f), F32), pltpu.VMEM((f, d), F32),
            pltpu.VMEM((EXPERT_SUBTILE * SUBLANES, LANES), F32),
            pltpu.SemaphoreType.DMA((3,)),
        ],
    )
    return pl.pallas_call(
        _experts_kernel,
        out_shape=jax.ShapeDtypeStruct(rows.shape, F32),
        grid_spec=grid_spec,
        compiler_params=pltpu.CompilerParams(
            dimension_semantics=("arbitrary",), vmem_limit_bytes=VMEM_LIMIT),
        name="experts",
    )(gid, tid, nxt, offs, nact, rows, bg, bu, bd, wg, wu, wd)


def _combine_kernel(dest_ref, dnext_ref, gate_ref, x1p_ref, x1s_ref, modp_ref, g2s_ref, fn_ref, y_hbm,
                    outp_ref, outs_ref, gbuf, sems, *, n_prompt_tiles):
    i = pl.program_id(0)
    n = pl.num_programs(0)
    tm = x1p_ref.shape[0]
    slot = lax.rem(i, 2)

    def start_gather(idx_ref, into):
        def start_copy(t, k, d):
            src = y_hbm.at[pl.ds(pl.multiple_of(d * SUBLANES, SUBLANES), SUBLANES)]
            dst = gbuf.at[into, k, pl.ds(pl.multiple_of(t * SUBLANES, SUBLANES), SUBLANES)]
            pltpu.make_async_copy(src, dst, sems.at[into]).start(priority=k % 2)
        _for_each_token_slot(idx_ref, start_copy)

    @pl.when(i == 0)
    def _():
        start_gather(dest_ref, 0)

    @pl.when(i + 1 < n)
    def _():
        start_gather(dnext_ref, 1 - slot)

    for k in range(TOP_K):
        pltpu.make_async_copy(y_hbm.at[pl.ds(0, tm * SUBLANES)], gbuf.at[slot, k], sems.at[slot]).wait()

    gates = gate_ref[...].T
    cols = []
    for c in range(SUBLANES):
        acc = gates[:, 0:1] * gbuf[slot, 0, pl.ds(c, tm, stride=SUBLANES), :]
        for k in range(1, TOP_K):
            acc = acc + gates[:, k:k + 1] * gbuf[slot, k, pl.ds(c, tm, stride=SUBLANES), :]
        cols.append(acc)
    ffn = jnp.concatenate(cols, axis=-1)

    @pl.when(i < n_prompt_tiles)
    def _():
        g2 = modp_ref[...][5:6, :]
        outp_ref[...] = _rmsnorm(x1p_ref[...] + g2 * ffn, fn_ref[...])

    @pl.when(i >= n_prompt_tiles)
    def _():
        outs_ref[...] = _rmsnorm(x1s_ref[...] + g2s_ref[...] * ffn, fn_ref[...])


def _combine(dest, gates, x1_p, x1_s, mod_p, g2_s, final_norm, y_rows, tokens_per_seq):
    n_p, d = x1_p.shape
    n_s = x1_s.shape[0]
    tm = COMBINE_TILE
    npt, nst = n_p // tm, n_s // tm
    tiles_per_seq = tokens_per_seq // tm
    pmap = lambda i: (jnp.minimum(i, npt - 1), 0)
    smap = lambda i: (jnp.maximum(i - npt, 0), 0)
    return pl.pallas_call(
        functools.partial(_combine_kernel, n_prompt_tiles=npt),
        out_shape=(jax.ShapeDtypeStruct((n_p, d), F32), jax.ShapeDtypeStruct((n_s, d), F32)),
        grid=(npt + nst,),
        in_specs=[
            pl.BlockSpec((tm * TOP_K,), lambda i: (i,), memory_space=pltpu.SMEM),
            pl.BlockSpec((tm * TOP_K,), lambda i: (jnp.minimum(i + 1, npt + nst - 1),),
                         memory_space=pltpu.SMEM),
            pl.BlockSpec((SUBLANES, tm), lambda i: (0, i)),
            pl.BlockSpec((tm, d), pmap),
            pl.BlockSpec((tm, d), smap),
            pl.BlockSpec((None, 6, d), lambda i: (jnp.minimum(i, npt - 1) // tiles_per_seq, 0, 0)),
            pl.BlockSpec((tm, d), smap),
            pl.BlockSpec((1, d), lambda i: (0, 0)),
            pl.BlockSpec(memory_space=pl.ANY),
        ],
        out_specs=(pl.BlockSpec((tm, d), pmap), pl.BlockSpec((tm, d), smap)),
        scratch_shapes=[pltpu.VMEM((2, TOP_K, tm * SUBLANES, LANES), F32), pltpu.SemaphoreType.DMA((2,))],
        compiler_params=pltpu.CompilerParams(
            dimension_semantics=("arbitrary",), vmem_limit_bytes=VMEM_LIMIT),
        name="combine",
    )(dest, dest, gates, x1_p, x1_s, mod_p, g2_s, final_norm, y_rows)


def kernel(x_prompt, x_sample, state_pool, state_conv, c_prompt, c_sample, norm1, norm2, w_ada, b_ada,
           w_in, w_pool, pool_scale, w_conv, w_out, w_router, b_router, w_gate, b_gate, w_up, b_up,
           w_down, b_down, final_norm):
    depth = norm1.shape[0]
    assert depth == 1, "single-layer step"
    bp, tp, d = x_prompt.shape
    bs, ts, _ = x_sample.shape
    dp = state_pool.shape[-1]
    n_hist = state_pool.shape[2]
    n_chist = state_conv.shape[2]
    n_p, n_s = bp * tp, bs * ts
    assert d == SUBLANES * LANES, "token-tile layout assumes one vreg tile per token row"
    assert tp % TOKEN_TILE == 0 and tp % COMBINE_TILE == 0 and n_s % COMBINE_TILE == 0
    assert n_p % DISPATCH_TILE == 0 and n_s % DISPATCH_TILE == 0
    assert (n_p + n_s) % ROUTE_TILE == 0 and ((n_p + n_s) * TOP_K) % EXPERT_TILE == 0

    l = 0
    n1 = norm1[l].reshape(1, d)
    n2 = norm2[l].reshape(1, d)
    w_in_b = w_in[l].astype(BF16)
    w_pool_b = w_pool[l].astype(BF16)
    w_out_b = w_out[l].astype(BF16)
    pscale = pool_scale[l].reshape(1, dp)
    w_r = w_router[l].T.astype(BF16)
    b_r = b_router[l].reshape(N_EXPERTS, 1)

    mod = _adaln(jnp.concatenate([c_sample, c_prompt], axis=0), w_ada[l], b_ada[l])
    mod_s = mod[:bs]
    mod_p = mod[bs:].reshape(bp, 6, d)

    x1_p, h2_p, lg_p, u_tail, v_tail = _mixer_prompt(
        x_prompt, mod_p, n1, n2, w_in_b, w_pool_b, pscale, w_conv[l], w_out_b, w_r, b_r)

    xs_tm = jnp.transpose(x_sample, (1, 0, 2)).reshape(n_s, d)
    ps_tm = jnp.transpose(state_pool[l], (1, 0, 2))
    cs_tm = jnp.transpose(state_conv[l], (1, 0, 2))
    x1_s, h2_s, lg_s, newp_tm, newc_tm = _mixer_sample(
        xs_tm, mod, ps_tm, cs_tm, n1, n2, w_in_b, w_pool_b, pscale, w_conv[l], w_out_b, w_r, b_r, ts)

    dest, gates, counts_f = _route(lg_p, lg_s)
    counts = counts_f[:, 0].astype(I32)
    n_rows = (n_p + n_s) * TOP_K
    gid, tid, nxt, offs, nact, n_steps = _group_metadata(counts, n_rows, EXPERT_TILE)

    dest = dest.reshape(-1)
    rows = _dispatch(dest, h2_p, h2_s)
    y_rows = _experts(gid, tid, nxt, offs, nact, n_steps, rows,
                      w_gate[l], b_gate[l], w_up[l], b_up[l], w_down[l], b_down[l])

    g2_s = jnp.tile(mod_s[:, 5 * d:], (ts, 1))
    y_p, y_s = _combine(dest, gates, x1_p, x1_s, mod_p, g2_s, final_norm.reshape(1, d), y_rows, tp)

    y_prompt = y_p.reshape(bp, tp, d)
    y_sample = jnp.transpose(y_s.reshape(ts, bs, d), (1, 0, 2))
    new_pool_prompt = u_tail[:, POOL_HALO - n_hist:, :][None]
    new_conv_prompt = v_tail[:, CONV_HALO - n_chist:, :][None]
    new_pool_sample = jnp.transpose(newp_tm, (1, 0, 2))[None]
    new_conv_sample = jnp.transpose(newc_tm, (1, 0, 2))[None]
    return (y_prompt, y_sample, new_pool_prompt, new_conv_prompt, new_pool_sample, new_conv_sample)
```

```python
import functools

import jax
import jax.numpy as jnp
from jax import lax
from jax.experimental import pallas as pl
from jax.experimental.pallas import tpu as pltpu

F32 = jnp.float32
BF16 = jnp.bfloat16
I32 = jnp.int32

POOL_WINDOWS = (2, 4, 8, 16)
POOL_HALO = 16
CONV_TAPS = 3
CONV_HALO = 8
N_EXPERTS = 32
TOP_K = 4
SWIGLU_LIMIT = 7.0
SWIGLU_ALPHA = 1.702
EPS = 1e-5
PAST_LEN = 16384

LANES = 128
SUBLANES = 8

TOKEN_TILE = 512
ROUTE_TILE = 512
DISPATCH_TILE = 512
COMBINE_TILE = 512
COMBINE_GROUP = 32
EXPERT_TILE = 512
EXPERT_SUBTILE = 128
VMEM_LIMIT = 56 * 1024 * 1024


def _rmsnorm(x, g):
    ms = jnp.mean(x * x, axis=-1, keepdims=True)
    return x * lax.rsqrt(ms + EPS) * g


def _dot(a, b):
    return jnp.dot(a, b, preferred_element_type=F32)


def _store_token_tiles(ref, val):
    rows = val.shape[0]
    for c in range(SUBLANES):
        ref[pl.ds(c, rows, stride=SUBLANES), :] = val[:, c * LANES:(c + 1) * LANES]


def _load_token_tiles(ref):
    rows = ref.shape[0] // SUBLANES
    return jnp.concatenate([ref[pl.ds(c, rows, stride=SUBLANES), :] for c in range(SUBLANES)], axis=-1)


ISSUE_GROUP = 4


def _slot_index(idx_ref, c, k, lane):
    return idx_ref[(c * TOP_K + k) * LANES + lane]


def _for_each_token_slot(idx_ref, start_copy):
    for c in range(idx_ref.shape[0] // (TOP_K * LANES)):
        def group(g, carry, c=c):
            l0 = g * ISSUE_GROUP
            idx = [[_slot_index(idx_ref, c, k, l0 + j) for k in range(TOP_K)] for j in range(ISSUE_GROUP)]
            for j in range(ISSUE_GROUP):
                for k in range(TOP_K):
                    start_copy(c * LANES + l0 + j, k, idx[j][k])
            return carry
        lax.fori_loop(0, LANES // ISSUE_GROUP, group, 0)


def _adaln_kernel(c_ref, w_ref, b_ref, o_ref):
    c = c_ref[...]
    s = c * jax.nn.sigmoid(c)
    o_ref[...] = _dot(s.astype(BF16), w_ref[...].astype(BF16)) + b_ref[...]


def _adaln(c, w_ada, b_ada):
    rows, d = c.shape
    n = w_ada.shape[1]
    tn = 1024
    return pl.pallas_call(
        _adaln_kernel,
        out_shape=jax.ShapeDtypeStruct((rows, n), F32),
        grid=(n // tn,),
        in_specs=[
            pl.BlockSpec((rows, d), lambda j: (0, 0)),
            pl.BlockSpec((d, tn), lambda j: (0, j)),
            pl.BlockSpec((1, tn), lambda j: (0, j)),
        ],
        out_specs=pl.BlockSpec((rows, tn), lambda j: (0, j)),
        compiler_params=pltpu.CompilerParams(
            dimension_semantics=("arbitrary",), vmem_limit_bytes=VMEM_LIMIT),
        name="adaln",
    )(c, w_ada, b_ada.reshape(1, n))


def _mix_tail(x, pool_in, conv_out, g1, sc2, sh2, n2, wpool_ref, pscale, wout, wr, br):
    gw = pool_in.shape[1] // len(POOL_WINDOWS)
    mixed = [_dot(pool_in[:, g * gw:(g + 1) * gw].astype(BF16), wpool_ref[g])
             for g in range(len(POOL_WINDOWS))]
    pool_out = jnp.concatenate(mixed, axis=-1) * pscale
    mix_in = jnp.concatenate([pool_out, conv_out], axis=-1).astype(BF16)
    x1 = x + g1 * _dot(mix_in, wout)
    h2 = _rmsnorm(x1, n2) * (1.0 + sc2) + sh2
    logits_t = lax.dot_general(wr, h2.astype(BF16), (((1,), (1,)), ((), ())),
                               preferred_element_type=F32) + br
    return x1, h2, logits_t


def _mixer_prompt_kernel(x_ref, mod_ref, n1_ref, n2_ref, win_ref, wpool_ref, pscale_ref, wconv_ref,
                         wout_ref, wr_ref, br_ref,
                         x1_ref, h2_ref, lg_ref, upool_ref, vconv_ref, ubuf, vbuf):
    tt = x_ref.shape[0]
    dp = ubuf.shape[1]
    gw = dp // len(POOL_WINDOWS)
    t = pl.program_id(1)

    @pl.when(t == 0)
    def _():
        ubuf[0:POOL_HALO, :] = jnp.zeros((POOL_HALO, dp), F32)
        vbuf[0:CONV_HALO, :] = jnp.zeros((CONV_HALO, dp), F32)

    x = x_ref[...]
    mod = mod_ref[...]
    sh1, sc1, g1, sh2, sc2, _ = [mod[i:i + 1, :] for i in range(6)]
    h = _rmsnorm(x, n1_ref[...]) * (1.0 + sc1) + sh1
    z = _dot(h.astype(BF16), win_ref[...])
    u, gate_b, gate_c, val = [z[:, i * dp:(i + 1) * dp] for i in range(4)]

    ubuf[POOL_HALO:POOL_HALO + tt, :] = u
    pos = lax.broadcasted_iota(I32, (tt, gw), 0) + t * tt
    pooled = []
    for g, w in enumerate(POOL_WINDOWS):
        cols = slice(g * gw, (g + 1) * gw)
        acc = u[:, cols]
        for j in range(1, w):
            acc = acc + ubuf[POOL_HALO - j:POOL_HALO - j + tt, cols]
        cnt = jnp.minimum(pos + 1, w).astype(F32)
        pooled.append(acc / cnt - u[:, cols])
    pool_in = jnp.concatenate(pooled, axis=-1)

    v = gate_c * val
    vbuf[CONV_HALO:CONV_HALO + tt, :] = v
    wc = wconv_ref[...]
    y = (wc[0:1, :] * vbuf[CONV_HALO - 2:CONV_HALO - 2 + tt, :]
         + wc[1:2, :] * vbuf[CONV_HALO - 1:CONV_HALO - 1 + tt, :]
         + wc[2:3, :] * v)
    conv_out = gate_b * y

    x1, h2, logits = _mix_tail(x, pool_in, conv_out, g1, sc2, sh2, n2_ref[...], wpool_ref,
                               pscale_ref[...], wout_ref[...], wr_ref[...], br_ref[...])
    x1_ref[...] = x1
    _store_token_tiles(h2_ref, h2)
    lg_ref[...] = logits

    ubuf[0:POOL_HALO, :] = ubuf[tt:tt + POOL_HALO, :]
    vbuf[0:CONV_HALO, :] = vbuf[tt:tt + CONV_HALO, :]

    @pl.when(t == pl.num_programs(1) - 1)
    def _():
        upool_ref[...] = ubuf[0:POOL_HALO, :]
        vconv_ref[...] = vbuf[0:CONV_HALO, :]


def _mixer_prompt(x, mod_p, n1, n2, w_in, w_pool, pscale, w_conv, w_out, w_r, b_r):
    b, t, d = x.shape
    dp = w_pool.shape[0] * w_pool.shape[1]
    tt = min(TOKEN_TILE, t)
    const2 = lambda i, j: (0, 0)
    const3 = lambda i, j: (0, 0, 0)
    return pl.pallas_call(
        _mixer_prompt_kernel,
        out_shape=(
            jax.ShapeDtypeStruct((b * t, d), F32),
            jax.ShapeDtypeStruct((b * t * SUBLANES, LANES), F32),
            jax.ShapeDtypeStruct((N_EXPERTS, b * t), F32),
            jax.ShapeDtypeStruct((b, POOL_HALO, dp), F32),
            jax.ShapeDtypeStruct((b, CONV_HALO, dp), F32),
        ),
        grid=(b, t // tt),
        in_specs=[
            pl.BlockSpec((None, tt, d), lambda i, j: (i, j, 0)),
            pl.BlockSpec((None, 6, d), lambda i, j: (i, 0, 0)),
            pl.BlockSpec((1, d), const2),
            pl.BlockSpec((1, d), const2),
            pl.BlockSpec(w_in.shape, const2),
            pl.BlockSpec(w_pool.shape, const3),
            pl.BlockSpec((1, dp), const2),
            pl.BlockSpec(w_conv.shape, const2),
            pl.BlockSpec(w_out.shape, const2),
            pl.BlockSpec(w_r.shape, const2),
            pl.BlockSpec((N_EXPERTS, 1), const2),
        ],
        out_specs=(
            pl.BlockSpec((tt, d), lambda i, j: (i * (t // tt) + j, 0)),
            pl.BlockSpec((tt * SUBLANES, LANES), lambda i, j: (i * (t // tt) + j, 0)),
            pl.BlockSpec((N_EXPERTS, tt), lambda i, j: (0, i * (t // tt) + j)),
            pl.BlockSpec((None, POOL_HALO, dp), lambda i, j: (i, 0, 0)),
            pl.BlockSpec((None, CONV_HALO, dp), lambda i, j: (i, 0, 0)),
        ),
        scratch_shapes=[
            pltpu.VMEM((POOL_HALO + tt, dp), F32),
            pltpu.VMEM((CONV_HALO + tt, dp), F32),
        ],
        compiler_params=pltpu.CompilerParams(
            dimension_semantics=("arbitrary", "arbitrary"), vmem_limit_bytes=VMEM_LIMIT),
        name="mixer_prompt",
    )(x, mod_p, n1, n2, w_in, w_pool, pscale, w_conv, w_out, w_r, b_r)


def _mixer_sample_kernel(x_ref, mod_ref, pstate_ref, cstate_ref, n1_ref, n2_ref, win_ref, wpool_ref,
                         pscale_ref, wconv_ref, wout_ref, wr_ref, br_ref,
                         x1_ref, h2_ref, lg_ref, newp_ref, newc_ref, *, steps):
    nb = pstate_ref.shape[1]
    d = x_ref.shape[1]
    dp = pstate_ref.shape[2]
    gw = dp // len(POOL_WINDOWS)
    n_hist = pstate_ref.shape[0]
    n_chist = cstate_ref.shape[0]

    x = x_ref[...]
    mod = mod_ref[0:nb, :]
    rep = lambda a: jnp.concatenate([a] * steps, axis=0)
    sh1, sc1, g1, sh2, sc2, _ = [rep(mod[:, i * d:(i + 1) * d]) for i in range(6)]
    h = _rmsnorm(x, n1_ref[...]) * (1.0 + sc1) + sh1
    z = _dot(h.astype(BF16), win_ref[...])
    u, gate_b, gate_c, val = [z[:, i * dp:(i + 1) * dp] for i in range(4)]

    ext = [pstate_ref[i] for i in range(n_hist)] + [u[s * nb:(s + 1) * nb, :] for s in range(steps)]
    pooled_steps = []
    for s in range(steps):
        groups = []
        for g, w in enumerate(POOL_WINDOWS):
            cols = slice(g * gw, (g + 1) * gw)
            acc = ext[n_hist + s][:, cols]
            for j in range(1, w):
                acc = acc + ext[n_hist + s - j][:, cols]
            cnt = float(min(PAST_LEN + s + 1, w))
            groups.append(acc / cnt - ext[n_hist + s][:, cols])
        pooled_steps.append(jnp.concatenate(groups, axis=-1))
    pool_in = jnp.concatenate(pooled_steps, axis=0)

    v = gate_c * val
    vext = [cstate_ref[i] for i in range(n_chist)] + [v[s * nb:(s + 1) * nb, :] for s in range(steps)]
    wc = wconv_ref[...]
    y = jnp.concatenate(
        [wc[0:1, :] * vext[s] + wc[1:2, :] * vext[s + 1] + wc[2:3, :] * vext[s + 2] for s in range(steps)],
        axis=0)
    conv_out = gate_b * y

    x1, h2, logits = _mix_tail(x, pool_in, conv_out, g1, sc2, sh2, n2_ref[...], wpool_ref,
                               pscale_ref[...], wout_ref[...], wr_ref[...], br_ref[...])
    x1_ref[...] = x1
    _store_token_tiles(h2_ref, h2)
    lg_ref[...] = logits
    for i in range(n_hist):
        newp_ref[i] = ext[steps + i]
    for i in range(n_chist):
        newc_ref[i] = vext[steps + i]


def _mixer_sample(x_tm, mod_s, pstate_tm, cstate_tm, n1, n2, w_in, w_pool, pscale, w_conv, w_out, w_r, b_r,
                  steps):
    rows, d = x_tm.shape
    return pl.pallas_call(
        functools.partial(_mixer_sample_kernel, steps=steps),
        out_shape=(
            jax.ShapeDtypeStruct((rows, d), F32),
            jax.ShapeDtypeStruct((rows * SUBLANES, LANES), F32),
            jax.ShapeDtypeStruct((N_EXPERTS, rows), F32),
            jax.ShapeDtypeStruct(pstate_tm.shape, F32),
            jax.ShapeDtypeStruct(cstate_tm.shape, F32),
        ),
        compiler_params=pltpu.CompilerParams(vmem_limit_bytes=VMEM_LIMIT),
        name="mixer_sample",
    )(x_tm, mod_s, pstate_tm, cstate_tm, n1, n2, w_in, w_pool, pscale, w_conv, w_out, w_r, b_r)


def _route_kernel(lgp_ref, lgs_ref, dest_ref, gate_ref, cnt_ref, counts, start, before):
    ne = lgp_ref.shape[0]
    tr = before.shape[0]
    reps = tr // LANES
    n_prompt_chunks = lgp_ref.shape[1] // tr
    n_sample_chunks = lgs_ref.shape[1] // tr
    eidx = lax.broadcasted_iota(I32, (ne, tr), 0)

    def top_k(ref, c):
        work = ref[:, pl.ds(pl.multiple_of(c * tr, tr), tr)]
        top_v, onehots = [], []
        for _ in range(TOP_K):
            m = jnp.max(work, axis=0, keepdims=True)
            idx = jnp.min(jnp.where(work == m, eidx, ne), axis=0, keepdims=True)
            sel = eidx == idx
            top_v.append(m)
            onehots.append(sel)
            work = jnp.where(sel, -jnp.inf, work)
        mask = jnp.where(onehots[0] | onehots[1] | onehots[2] | onehots[3], 1.0, 0.0)
        chunk_counts = jnp.broadcast_to(jnp.sum(mask, axis=1, keepdims=True), (ne, LANES))
        return top_v, onehots, mask, chunk_counts

    def count_chunk(ref):
        def body(c, carry):
            counts[...] = counts[...] + top_k(ref, c)[3]
            return carry
        return body

    counts[...] = jnp.zeros_like(counts)
    lax.fori_loop(0, n_prompt_chunks, count_chunk(lgp_ref), 0)
    lax.fori_loop(0, n_sample_chunks, count_chunk(lgs_ref), 0)

    total = counts[...]
    hi = jnp.floor(total * (1.0 / 256.0))
    lo = total - hi * 256.0
    r = lax.broadcasted_iota(I32, (ne, ne), 0)
    col = lax.broadcasted_iota(I32, (ne, ne), 1)
    lower = jnp.where(col < r, 1.0, 0.0).astype(BF16)
    start[...] = 256.0 * _dot(lower, hi.astype(BF16)) + _dot(lower, lo.astype(BF16))
    cnt_ref[...] = total
    counts[...] = jnp.zeros_like(counts)

    r = lax.broadcasted_iota(I32, (tr, tr), 0)
    col = lax.broadcasted_iota(I32, (tr, tr), 1)
    before[...] = jnp.where(r < col, 1.0, 0.0).astype(BF16)
    gate_ref[...] = jnp.zeros_like(gate_ref)

    def place_chunk(ref, first_chunk):
        def body(c, carry):
            top_v, onehots, mask, chunk_counts = top_k(ref, c)
            base = jnp.concatenate([counts[...] + start[...]] * reps, axis=1)
            rank = _dot(mask.astype(BF16), before[...]) + base
            counts[...] = counts[...] + chunk_counts
            es = [jnp.exp(v - top_v[0]) for v in top_v]
            denom = es[0] + es[1] + es[2] + es[3]
            chunk = first_chunk + c
            cols = pl.ds(pl.multiple_of(chunk * tr, tr), tr)
            for k in range(TOP_K):
                d = jnp.sum(jnp.where(onehots[k], rank, 0.0), axis=0, keepdims=True).astype(I32)
                for j in range(reps):
                    dest_ref[chunk * reps + j, k:k + 1, :] = d[:, j * LANES:(j + 1) * LANES]
                gate_ref[k:k + 1, cols] = es[k] / denom
            return carry
        return body

    lax.fori_loop(0, n_prompt_chunks, place_chunk(lgp_ref, 0), 0)
    lax.fori_loop(0, n_sample_chunks, place_chunk(lgs_ref, n_prompt_chunks), 0)


def _route(lgt_p, lgt_s):
    ne, n_p = lgt_p.shape
    n_s = lgt_s.shape[1]
    tr = ROUTE_TILE
    return pl.pallas_call(
        _route_kernel,
        out_shape=(
            jax.ShapeDtypeStruct(((n_p + n_s) // LANES, TOP_K, LANES), I32),
            jax.ShapeDtypeStruct((SUBLANES, n_p + n_s), F32),
            jax.ShapeDtypeStruct((ne, LANES), F32),
        ),
        scratch_shapes=[pltpu.VMEM((ne, LANES), F32), pltpu.VMEM((ne, LANES), F32),
                        pltpu.VMEM((tr, tr), BF16)],
        compiler_params=pltpu.CompilerParams(vmem_limit_bytes=VMEM_LIMIT),
        name="route",
    )(lgt_p, lgt_s)


def _group_metadata(counts, n_rows, tile):
    n_tiles = n_rows // tile
    n_steps = n_tiles + N_EXPERTS - 1
    ends = jnp.cumsum(counts)
    offs = jnp.concatenate([jnp.zeros((1,), I32), ends]).astype(I32)
    first_tile = offs[:-1] // tile
    last_tile = (ends - 1) // tile
    tiles_e = jnp.where(counts > 0, last_tile - first_tile + 1, 0)
    step_end = jnp.cumsum(tiles_e)
    step_start = step_end - tiles_e
    n_active = step_end[-1]
    s = jnp.minimum(jnp.arange(n_steps, dtype=I32), n_active - 1)
    owner = ((s[:, None] >= step_start[None, :]) & (s[:, None] < step_end[None, :])).astype(I32)
    gid = jnp.sum(owner * jnp.arange(N_EXPERTS, dtype=I32)[None, :], axis=1)
    tid = jnp.sum(owner * (first_tile - step_start)[None, :], axis=1) + s
    ids = jnp.arange(N_EXPERTS, dtype=I32)
    later = (ids[None, :] > ids[:, None]) & (counts[None, :] > 0)
    next_e = jnp.min(jnp.where(later, ids[None, :], N_EXPERTS), axis=1)
    next_e = jnp.where(next_e == N_EXPERTS, -1, next_e)
    nxt = jnp.sum(owner * next_e[None, :], axis=1)
    return gid, tid, nxt, offs, n_active.reshape(1).astype(I32), n_steps


def _dispatch_kernel(dest_ref, hp_ref, hs_ref, rows_ref, sem, *, n_prompt_tiles):
    i = pl.program_id(0)
    tm = dest_ref.shape[0] // TOP_K

    def scatter(src_ref):
        def start_copy(t, k, d):
            src = src_ref.at[pl.ds(pl.multiple_of(t * SUBLANES, SUBLANES), SUBLANES)]
            dst = rows_ref.at[pl.ds(pl.multiple_of(d * SUBLANES, SUBLANES), SUBLANES)]
            pltpu.make_async_copy(src, dst, sem).start(priority=k % 2)
        _for_each_token_slot(dest_ref, start_copy)
        for _ in range(TOP_K):
            pltpu.make_async_copy(src_ref, rows_ref.at[pl.ds(0, tm * SUBLANES)], sem).wait()

    @pl.when(i < n_prompt_tiles)
    def _():
        scatter(hp_ref)

    @pl.when(i >= n_prompt_tiles)
    def _():
        scatter(hs_ref)


def _dispatch(dest, h2_p, h2_s):
    n_p = h2_p.shape[0] // SUBLANES
    n_s = h2_s.shape[0] // SUBLANES
    tm = DISPATCH_TILE
    npt, nst = n_p // tm, n_s // tm
    return pl.pallas_call(
        functools.partial(_dispatch_kernel, n_prompt_tiles=npt),
        out_shape=jax.ShapeDtypeStruct(((n_p + n_s) * TOP_K * SUBLANES, LANES), h2_p.dtype),
        grid=(npt + nst,),
        in_specs=[
            pl.BlockSpec((tm * TOP_K,), lambda i: (i,), memory_space=pltpu.SMEM),
            pl.BlockSpec((tm * SUBLANES, LANES), lambda i: (jnp.minimum(i, npt - 1), 0)),
            pl.BlockSpec((tm * SUBLANES, LANES), lambda i: (jnp.maximum(i - npt, 0), 0)),
        ],
        out_specs=pl.BlockSpec(memory_space=pl.ANY),
        scratch_shapes=[pltpu.SemaphoreType.DMA],
        compiler_params=pltpu.CompilerParams(
            dimension_semantics=("arbitrary",), vmem_limit_bytes=VMEM_LIMIT),
        name="dispatch",
    )(dest, h2_p, h2_s)


def _experts_kernel(gid_ref, tid_ref, nxt_ref, offs_ref, nact_ref,
                    rows_ref, bg_ref, bu_ref, bd_ref, wg_hbm, wu_hbm, wd_hbm, out_ref,
                    wg_b, wu_b, wd_b, wg_f, wu_f, wd_f, relay, sems):
    s = pl.program_id(0)
    tm = rows_ref.shape[0] // SUBLANES
    landing = ((wg_hbm, wg_f, wg_b), (wu_hbm, wu_f, wu_b), (wd_hbm, wd_f, wd_b))

    def fetch(e):
        for j, (hbm, land, _) in enumerate(landing):
            pltpu.make_async_copy(hbm.at[e], land, sems.at[j]).start()

    @pl.when(s < nact_ref[0])
    def _():
        e = gid_ref[s]
        m = tid_ref[s]

        @pl.when(s == 0)
        def _():
            fetch(e)

        @pl.when((s == 0) | (gid_ref[jnp.maximum(s - 1, 0)] != e))
        def _():
            for j, (hbm, land, half) in enumerate(landing):
                pltpu.make_async_copy(hbm.at[e], land, sems.at[j]).wait()
                half[...] = land[...].astype(BF16)

            @pl.when(nxt_ref[s] >= 0)
            def _():
                fetch(nxt_ref[s])

        def ffn(x):
            g = _dot(x, wg_b[...]) + bg_ref[pl.ds(e, 1), :]
            u = _dot(x, wu_b[...]) + bu_ref[pl.ds(e, 1), :]
            g = jnp.minimum(g, SWIGLU_LIMIT)
            u = jnp.clip(u, -SWIGLU_LIMIT, SWIGLU_LIMIT)
            glu = g * jax.nn.sigmoid(SWIGLU_ALPHA * g)
            return _dot(((u + 1.0) * glu).astype(BF16), wd_b[...]) + bd_ref[pl.ds(e, 1), :]

        lo = offs_ref[e]
        hi = offs_ref[e + 1]
        whole_tile = (lo <= m * tm) & (hi >= (m + 1) * tm)

        @pl.when(whole_tile)
        def _():
            _store_token_tiles(out_ref, ffn(_load_token_tiles(rows_ref).astype(BF16)))

        @pl.when(jnp.logical_not(whole_tile))
        def _():
            sub = relay.shape[0] // SUBLANES
            for j in range(tm // sub):
                first = m * tm + j * sub

                @pl.when((lo < first + sub) & (hi > first))
                def _():
                    span = pl.ds(j * sub * SUBLANES, sub * SUBLANES)
                    _store_token_tiles(relay, ffn(_load_token_tiles(rows_ref.at[span]).astype(BF16)))
                    row = first + lax.shift_right_logical(lax.broadcasted_iota(I32, relay.shape, 0), 3)
                    pltpu.store(out_ref.at[span], relay[...], mask=(row >= lo) & (row < hi))


def _experts(gid, tid, nxt, offs, nact, n_steps, rows, wg, bg, wu, bu, wd, bd):
    ne, d, f = wg.shape
    tm = EXPERT_TILE
    tile_map = lambda s, gid, tid, nxt, offs, nact: (tid[s], 0)
    whole = lambda s, gid, tid, nxt, offs, nact: (0, 0)
    grid_spec = pltpu.PrefetchScalarGridSpec(
        num_scalar_prefetch=5,
        grid=(n_steps,),
        in_specs=[
            pl.BlockSpec((tm * SUBLANES, LANES), tile_map),
            pl.BlockSpec((ne, f), whole),
            pl.BlockSpec((ne, f), whole),
            pl.BlockSpec((ne, d), whole),
            pl.BlockSpec(memory_space=pl.ANY),
            pl.BlockSpec(memory_space=pl.ANY),
            pl.BlockSpec(memory_space=pl.ANY),
        ],
        out_specs=pl.BlockSpec((tm * SUBLANES, LANES), tile_map),
        scratch_shapes=[
            pltpu.VMEM((d, f), BF16), pltpu.VMEM((d, f), BF16), pltpu.VMEM((f, d), BF16),
            pltpu.VMEM((d, f), F32), pltpu.VMEM((d, f), F32), pltpu.VMEM((f, d), F32),
            pltpu.VMEM((EXPERT_SUBTILE * SUBLANES, LANES), F32),
            pltpu.SemaphoreType.DMA((3,)),
        ],
    )
    return pl.pallas_call(
        _experts_kernel,
        out_shape=jax.ShapeDtypeStruct(rows.shape, F32),
        grid_spec=grid_spec,
        compiler_params=pltpu.CompilerParams(
            dimension_semantics=("arbitrary",), vmem_limit_bytes=VMEM_LIMIT),
        name="experts",
    )(gid, tid, nxt, offs, nact, rows, bg, bu, bd, wg, wu, wd)


def _combine_kernel(dest_ref, dnext_ref, gate_ref, x1p_ref, x1s_ref, modp_ref, g2s_ref, fn_ref, y_hbm,
                    outp_ref, outs_ref, gbuf_a, gbuf_b, gates_t, sems, *, n_prompt_tiles):
    i = pl.program_id(0)
    n = pl.num_programs(0)
    tm = x1p_ref.shape[0]
    group = COMBINE_GROUP

    def start_copy(buf, sem, t, k, d):
        src = y_hbm.at[pl.ds(pl.multiple_of(d * SUBLANES, SUBLANES), SUBLANES)]
        dst = buf.at[k, pl.ds(pl.multiple_of(t * SUBLANES, SUBLANES), SUBLANES)]
        pltpu.make_async_copy(src, dst, sem).start(priority=k % 2)

    def wait_tile(buf, sem):
        for k in range(TOP_K):
            pltpu.make_async_copy(y_hbm.at[pl.ds(0, tm * SUBLANES)], buf.at[k], sem).wait()

    def step(cur, cur_sem, nxt, nxt_sem):
        @pl.when(i == 0)
        def _():
            _for_each_token_slot(dest_ref, functools.partial(start_copy, cur, cur_sem))

        wait_tile(cur, cur_sem)
        gates_t[...] = gate_ref[...].T
        fn = fn_ref[...]

        def run(x1_ref, out_ref, g2_rows):
            def body(g, carry, c):
                l0 = g * group
                base = pl.multiple_of(c * LANES + l0, group)
                idx = [[_slot_index(dnext_ref, c, k, l0 + j) for k in range(TOP_K)] for j in range(group)]
                for j in range(group):
                    for k in range(TOP_K):
                        start_copy(nxt, nxt_sem, base + j, k, idx[j][k])
                gt = gates_t[pl.ds(base, group), :]
                tile0 = pl.multiple_of(base * SUBLANES, group * SUBLANES)
                cols = []
                for c in range(SUBLANES):
                    acc = gt[:, 0:1] * cur[0, pl.ds(tile0 + c, group, stride=SUBLANES), :]
                    for k in range(1, TOP_K):
                        acc = acc + gt[:, k:k + 1] * cur[k, pl.ds(tile0 + c, group, stride=SUBLANES), :]
                    cols.append(acc)
                ffn = jnp.concatenate(cols, axis=-1)
                rows = pl.ds(base, group)
                out_ref[rows, :] = _rmsnorm(x1_ref[rows, :] + g2_rows(rows) * ffn, fn)
                return carry
            for c in range(tm // LANES):
                lax.fori_loop(0, LANES // group, functools.partial(body, c=c), 0)

        @pl.when(i < n_prompt_tiles)
        def _():
            g2 = modp_ref[...][5:6, :]
            run(x1p_ref, outp_ref, lambda rows: g2)

        @pl.when(i >= n_prompt_tiles)
        def _():
            run(x1s_ref, outs_ref, lambda rows: g2s_ref[rows, :])

        @pl.when(i == n - 1)
        def _():
            wait_tile(nxt, nxt_sem)

    @pl.when(lax.rem(i, 2) == 0)
    def _():
        step(gbuf_a, sems.at[0], gbuf_b, sems.at[1])

    @pl.when(lax.rem(i, 2) == 1)
    def _():
        step(gbuf_b, sems.at[1], gbuf_a, sems.at[0])


def _combine(dest, gates, x1_p, x1_s, mod_p, g2_s, final_norm, y_rows, tokens_per_seq):
    n_p, d = x1_p.shape
    n_s = x1_s.shape[0]
    tm = COMBINE_TILE
    npt, nst = n_p // tm, n_s // tm
    tiles_per_seq = tokens_per_seq // tm
    pmap = lambda i: (jnp.minimum(i, npt - 1), 0)
    smap = lambda i: (jnp.maximum(i - npt, 0), 0)
    return pl.pallas_call(
        functools.partial(_combine_kernel, n_prompt_tiles=npt),
        out_shape=(jax.ShapeDtypeStruct((n_p, d), F32), jax.ShapeDtypeStruct((n_s, d), F32)),
        grid=(npt + nst,),
        in_specs=[
            pl.BlockSpec((tm * TOP_K,), lambda i: (i,), memory_space=pltpu.SMEM),
            pl.BlockSpec((tm * TOP_K,), lambda i: (jnp.minimum(i + 1, npt + nst - 1),),
                         memory_space=pltpu.SMEM),
            pl.BlockSpec((SUBLANES, tm), lambda i: (0, i)),
            pl.BlockSpec((tm, d), pmap),
            pl.BlockSpec((tm, d), smap),
            pl.BlockSpec((None, 6, d), lambda i: (jnp.minimum(i, npt - 1) // tiles_per_seq, 0, 0)),
            pl.BlockSpec((tm, d), smap),
            pl.BlockSpec((1, d), lambda i: (0, 0)),
            pl.BlockSpec(memory_space=pl.ANY),
        ],
        out_specs=(pl.BlockSpec((tm, d), pmap), pl.BlockSpec((tm, d), smap)),
        scratch_shapes=[
            pltpu.VMEM((TOP_K, tm * SUBLANES, LANES), F32),
            pltpu.VMEM((TOP_K, tm * SUBLANES, LANES), F32),
            pltpu.VMEM((tm, SUBLANES), F32),
            pltpu.SemaphoreType.DMA((2,)),
        ],
        compiler_params=pltpu.CompilerParams(
            dimension_semantics=("arbitrary",), vmem_limit_bytes=VMEM_LIMIT),
        name="combine",
    )(dest, dest, gates, x1_p, x1_s, mod_p, g2_s, final_norm, y_rows)


def kernel(x_prompt, x_sample, state_pool, state_conv, c_prompt, c_sample, norm1, norm2, w_ada, b_ada,
           w_in, w_pool, pool_scale, w_conv, w_out, w_router, b_router, w_gate, b_gate, w_up, b_up,
           w_down, b_down, final_norm):
    depth = norm1.shape[0]
    assert depth == 1, "single-layer step"
    bp, tp, d = x_prompt.shape
    bs, ts, _ = x_sample.shape
    dp = state_pool.shape[-1]
    n_hist = state_pool.shape[2]
    n_chist = state_conv.shape[2]
    n_p, n_s = bp * tp, bs * ts
    assert d == SUBLANES * LANES, "token-tile layout assumes one vreg tile per token row"
    assert tp % TOKEN_TILE == 0 and tp % COMBINE_TILE == 0 and n_s % COMBINE_TILE == 0
    assert n_p % DISPATCH_TILE == 0 and n_s % DISPATCH_TILE == 0
    assert (n_p + n_s) % ROUTE_TILE == 0 and ((n_p + n_s) * TOP_K) % EXPERT_TILE == 0

    l = 0
    n1 = norm1[l].reshape(1, d)
    n2 = norm2[l].reshape(1, d)
    w_in_b = w_in[l].astype(BF16)
    w_pool_b = w_pool[l].astype(BF16)
    w_out_b = w_out[l].astype(BF16)
    pscale = pool_scale[l].reshape(1, dp)
    w_r = w_router[l].T.astype(BF16)
    b_r = b_router[l].reshape(N_EXPERTS, 1)

    mod = _adaln(jnp.concatenate([c_sample, c_prompt], axis=0), w_ada[l], b_ada[l])
    mod_s = mod[:bs]
    mod_p = mod[bs:].reshape(bp, 6, d)

    x1_p, h2_p, lg_p, u_tail, v_tail = _mixer_prompt(
        x_prompt, mod_p, n1, n2, w_in_b, w_pool_b, pscale, w_conv[l], w_out_b, w_r, b_r)

    xs_tm = jnp.transpose(x_sample, (1, 0, 2)).reshape(n_s, d)
    ps_tm = jnp.transpose(state_pool[l], (1, 0, 2))
    cs_tm = jnp.transpose(state_conv[l], (1, 0, 2))
    x1_s, h2_s, lg_s, newp_tm, newc_tm = _mixer_sample(
        xs_tm, mod, ps_tm, cs_tm, n1, n2, w_in_b, w_pool_b, pscale, w_conv[l], w_out_b, w_r, b_r, ts)

    dest, gates, counts_f = _route(lg_p, lg_s)
    counts = counts_f[:, 0].astype(I32)
    n_rows = (n_p + n_s) * TOP_K
    gid, tid, nxt, offs, nact, n_steps = _group_metadata(counts, n_rows, EXPERT_TILE)

    dest = dest.reshape(-1)
    rows = _dispatch(dest, h2_p, h2_s)
    y_rows = _experts(gid, tid, nxt, offs, nact, n_steps, rows,
                      w_gate[l], b_gate[l], w_up[l], b_up[l], w_down[l], b_down[l])

    g2_s = jnp.tile(mod_s[:, 5 * d:], (ts, 1))
    y_p, y_s = _combine(dest, gates, x1_p, x1_s, mod_p, g2_s, final_norm.reshape(1, d), y_rows, tp)

    y_prompt = y_p.reshape(bp, tp, d)
    y_sample = jnp.transpose(y_s.reshape(ts, bs, d), (1, 0, 2))
    new_pool_prompt = u_tail[:, POOL_HALO - n_hist:, :][None]
    new_conv_prompt = v_tail[:, CONV_HALO - n_chist:, :][None]
    new_pool_sample = jnp.transpose(newp_tm, (1, 0, 2))[None]
    new_conv_sample = jnp.transpose(newc_tm, (1, 0, 2))[None]
    return (y_prompt, y_sample, new_pool_prompt, new_conv_prompt, new_pool_sample, new_conv_sample)
```

```python
import functools

import jax
import jax.numpy as jnp
from jax import lax
from jax.experimental import pallas as pl
from jax.experimental.pallas import tpu as pltpu
from jax.experimental.pallas import tpu_sc as plsc

F32 = jnp.float32
BF16 = jnp.bfloat16
I32 = jnp.int32

POOL_WINDOWS = (2, 4, 8, 16)
POOL_HALO = 16
CONV_TAPS = 3
CONV_HALO = 8
N_EXPERTS = 32
TOP_K = 4
SWIGLU_LIMIT = 7.0
SWIGLU_ALPHA = 1.702
EPS = 1e-5
PAST_LEN = 16384

LANES = 128
SUBLANES = 8

TOKEN_TILE = 512
ROUTE_TILE = 512
COMBINE_TILE = 512
COMBINE_GROUP = 32
EXPERT_TILE = 512
EXPERT_SUBTILE = 128
VMEM_LIMIT = 56 * 1024 * 1024


def _rmsnorm(x, g):
    ms = jnp.mean(x * x, axis=-1, keepdims=True)
    return x * lax.rsqrt(ms + EPS) * g


def _dot(a, b):
    return jnp.dot(a, b, preferred_element_type=F32)


def _store_token_tiles(ref, val):
    rows = val.shape[0]
    for c in range(SUBLANES):
        ref[pl.ds(c, rows, stride=SUBLANES), :] = val[:, c * LANES:(c + 1) * LANES]


def _load_token_tiles(ref):
    rows = ref.shape[0] // SUBLANES
    return jnp.concatenate([ref[pl.ds(c, rows, stride=SUBLANES), :] for c in range(SUBLANES)], axis=-1)


ISSUE_GROUP = 4


def _slot_index(idx_ref, c, k, lane):
    return idx_ref[(c * TOP_K + k) * LANES + lane]


def _for_each_token_slot(idx_ref, start_copy):
    for c in range(idx_ref.shape[0] // (TOP_K * LANES)):
        def group(g, carry, c=c):
            l0 = g * ISSUE_GROUP
            idx = [[_slot_index(idx_ref, c, k, l0 + j) for k in range(TOP_K)] for j in range(ISSUE_GROUP)]
            for j in range(ISSUE_GROUP):
                for k in range(TOP_K):
                    start_copy(c * LANES + l0 + j, k, idx[j][k])
            return carry
        lax.fori_loop(0, LANES // ISSUE_GROUP, group, 0)


def _adaln_kernel(c_ref, w_ref, b_ref, o_ref):
    c = c_ref[...]
    s = c * jax.nn.sigmoid(c)
    o_ref[...] = _dot(s.astype(BF16), w_ref[...].astype(BF16)) + b_ref[...]


def _adaln(c, w_ada, b_ada):
    rows, d = c.shape
    n = w_ada.shape[1]
    tn = 1024
    return pl.pallas_call(
        _adaln_kernel,
        out_shape=jax.ShapeDtypeStruct((rows, n), F32),
        grid=(n // tn,),
        in_specs=[
            pl.BlockSpec((rows, d), lambda j: (0, 0)),
            pl.BlockSpec((d, tn), lambda j: (0, j)),
            pl.BlockSpec((1, tn), lambda j: (0, j)),
        ],
        out_specs=pl.BlockSpec((rows, tn), lambda j: (0, j)),
        compiler_params=pltpu.CompilerParams(
            dimension_semantics=("arbitrary",), vmem_limit_bytes=VMEM_LIMIT),
        name="adaln",
    )(c, w_ada, b_ada.reshape(1, n))


def _mix_tail(x, pool_in, conv_out, g1, sc2, sh2, n2, wpool_ref, pscale, wout, wr, br):
    gw = pool_in.shape[1] // len(POOL_WINDOWS)
    mixed = [_dot(pool_in[:, g * gw:(g + 1) * gw].astype(BF16), wpool_ref[g])
             for g in range(len(POOL_WINDOWS))]
    pool_out = jnp.concatenate(mixed, axis=-1) * pscale
    mix_in = jnp.concatenate([pool_out, conv_out], axis=-1).astype(BF16)
    x1 = x + g1 * _dot(mix_in, wout)
    h2 = _rmsnorm(x1, n2) * (1.0 + sc2) + sh2
    logits_t = lax.dot_general(wr, h2.astype(BF16), (((1,), (1,)), ((), ())),
                               preferred_element_type=F32) + br
    return x1, h2, logits_t


def _mixer_prompt_kernel(x_ref, mod_ref, n1_ref, n2_ref, win_ref, wpool_ref, pscale_ref, wconv_ref,
                         wout_ref, wr_ref, br_ref,
                         x1_ref, h2_ref, lg_ref, upool_ref, vconv_ref, ubuf, vbuf):
    tt = x_ref.shape[0]
    dp = ubuf.shape[1]
    gw = dp // len(POOL_WINDOWS)
    t = pl.program_id(1)

    @pl.when(t == 0)
    def _():
        ubuf[0:POOL_HALO, :] = jnp.zeros((POOL_HALO, dp), F32)
        vbuf[0:CONV_HALO, :] = jnp.zeros((CONV_HALO, dp), F32)

    x = x_ref[...]
    mod = mod_ref[...]
    sh1, sc1, g1, sh2, sc2, _ = [mod[i:i + 1, :] for i in range(6)]
    h = _rmsnorm(x, n1_ref[...]) * (1.0 + sc1) + sh1
    z = _dot(h.astype(BF16), win_ref[...])
    u, gate_b, gate_c, val = [z[:, i * dp:(i + 1) * dp] for i in range(4)]

    ubuf[POOL_HALO:POOL_HALO + tt, :] = u
    pos = lax.broadcasted_iota(I32, (tt, gw), 0) + t * tt
    pooled = []
    for g, w in enumerate(POOL_WINDOWS):
        cols = slice(g * gw, (g + 1) * gw)
        acc = u[:, cols]
        for j in range(1, w):
            acc = acc + ubuf[POOL_HALO - j:POOL_HALO - j + tt, cols]
        cnt = jnp.minimum(pos + 1, w).astype(F32)
        pooled.append(acc / cnt - u[:, cols])
    pool_in = jnp.concatenate(pooled, axis=-1)

    v = gate_c * val
    vbuf[CONV_HALO:CONV_HALO + tt, :] = v
    wc = wconv_ref[...]
    y = (wc[0:1, :] * vbuf[CONV_HALO - 2:CONV_HALO - 2 + tt, :]
         + wc[1:2, :] * vbuf[CONV_HALO - 1:CONV_HALO - 1 + tt, :]
         + wc[2:3, :] * v)
    conv_out = gate_b * y

    x1, h2, logits = _mix_tail(x, pool_in, conv_out, g1, sc2, sh2, n2_ref[...], wpool_ref,
                               pscale_ref[...], wout_ref[...], wr_ref[...], br_ref[...])
    x1_ref[...] = x1
    _store_token_tiles(h2_ref, h2)
    lg_ref[...] = logits

    ubuf[0:POOL_HALO, :] = ubuf[tt:tt + POOL_HALO, :]
    vbuf[0:CONV_HALO, :] = vbuf[tt:tt + CONV_HALO, :]

    @pl.when(t == pl.num_programs(1) - 1)
    def _():
        upool_ref[...] = ubuf[0:POOL_HALO, :]
        vconv_ref[...] = vbuf[0:CONV_HALO, :]


def _mixer_prompt(x, mod_p, n1, n2, w_in, w_pool, pscale, w_conv, w_out, w_r, b_r, extra_rows):
    b, t, d = x.shape
    dp = w_pool.shape[0] * w_pool.shape[1]
    tt = min(TOKEN_TILE, t)
    const2 = lambda i, j: (0, 0)
    const3 = lambda i, j: (0, 0, 0)
    return pl.pallas_call(
        _mixer_prompt_kernel,
        out_shape=(
            jax.ShapeDtypeStruct((b * t, d), F32),
            jax.ShapeDtypeStruct(((b * t + extra_rows) * SUBLANES, LANES), F32),
            jax.ShapeDtypeStruct((N_EXPERTS, b * t), F32),
            jax.ShapeDtypeStruct((b, POOL_HALO, dp), F32),
            jax.ShapeDtypeStruct((b, CONV_HALO, dp), F32),
        ),
        grid=(b, t // tt),
        in_specs=[
            pl.BlockSpec((None, tt, d), lambda i, j: (i, j, 0)),
            pl.BlockSpec((None, 6, d), lambda i, j: (i, 0, 0)),
            pl.BlockSpec((1, d), const2),
            pl.BlockSpec((1, d), const2),
            pl.BlockSpec(w_in.shape, const2),
            pl.BlockSpec(w_pool.shape, const3),
            pl.BlockSpec((1, dp), const2),
            pl.BlockSpec(w_conv.shape, const2),
            pl.BlockSpec(w_out.shape, const2),
            pl.BlockSpec(w_r.shape, const2),
            pl.BlockSpec((N_EXPERTS, 1), const2),
        ],
        out_specs=(
            pl.BlockSpec((tt, d), lambda i, j: (i * (t // tt) + j, 0)),
            pl.BlockSpec((tt * SUBLANES, LANES), lambda i, j: (i * (t // tt) + j, 0)),
            pl.BlockSpec((N_EXPERTS, tt), lambda i, j: (0, i * (t // tt) + j)),
            pl.BlockSpec((None, POOL_HALO, dp), lambda i, j: (i, 0, 0)),
            pl.BlockSpec((None, CONV_HALO, dp), lambda i, j: (i, 0, 0)),
        ),
        scratch_shapes=[
            pltpu.VMEM((POOL_HALO + tt, dp), F32),
            pltpu.VMEM((CONV_HALO + tt, dp), F32),
        ],
        compiler_params=pltpu.CompilerParams(
            dimension_semantics=("arbitrary", "arbitrary"), vmem_limit_bytes=VMEM_LIMIT),
        name="mixer_prompt",
    )(x, mod_p, n1, n2, w_in, w_pool, pscale, w_conv, w_out, w_r, b_r)


def _mixer_sample_kernel(x_ref, mod_ref, pstate_ref, cstate_ref, n1_ref, n2_ref, win_ref, wpool_ref,
                         pscale_ref, wconv_ref, wout_ref, wr_ref, br_ref,
                         h2_all_in, x1_ref, h2_all_ref, lg_ref, newp_ref, newc_ref, h2_tiles, *, steps):
    nb = pstate_ref.shape[1]
    d = x_ref.shape[1]
    dp = pstate_ref.shape[2]
    gw = dp // len(POOL_WINDOWS)
    n_hist = pstate_ref.shape[0]
    n_chist = cstate_ref.shape[0]

    x = x_ref[...]
    mod = mod_ref[0:nb, :]
    rep = lambda a: jnp.concatenate([a] * steps, axis=0)
    sh1, sc1, g1, sh2, sc2, _ = [rep(mod[:, i * d:(i + 1) * d]) for i in range(6)]
    h = _rmsnorm(x, n1_ref[...]) * (1.0 + sc1) + sh1
    z = _dot(h.astype(BF16), win_ref[...])
    u, gate_b, gate_c, val = [z[:, i * dp:(i + 1) * dp] for i in range(4)]

    ext = [pstate_ref[i] for i in range(n_hist)] + [u[s * nb:(s + 1) * nb, :] for s in range(steps)]
    pooled_steps = []
    for s in range(steps):
        groups = []
        for g, w in enumerate(POOL_WINDOWS):
            cols = slice(g * gw, (g + 1) * gw)
            acc = ext[n_hist + s][:, cols]
            for j in range(1, w):
                acc = acc + ext[n_hist + s - j][:, cols]
            cnt = float(min(PAST_LEN + s + 1, w))
            groups.append(acc / cnt - ext[n_hist + s][:, cols])
        pooled_steps.append(jnp.concatenate(groups, axis=-1))
    pool_in = jnp.concatenate(pooled_steps, axis=0)

    v = gate_c * val
    vext = [cstate_ref[i] for i in range(n_chist)] + [v[s * nb:(s + 1) * nb, :] for s in range(steps)]
    wc = wconv_ref[...]
    y = jnp.concatenate(
        [wc[0:1, :] * vext[s] + wc[1:2, :] * vext[s + 1] + wc[2:3, :] * vext[s + 2] for s in range(steps)],
        axis=0)
    conv_out = gate_b * y

    x1, h2, logits = _mix_tail(x, pool_in, conv_out, g1, sc2, sh2, n2_ref[...], wpool_ref,
                               pscale_ref[...], wout_ref[...], wr_ref[...], br_ref[...])
    x1_ref[...] = x1
    _store_token_tiles(h2_tiles, h2)
    tail = h2_all_ref.shape[0] - h2_tiles.shape[0]
    pltpu.sync_copy(h2_tiles, h2_all_ref.at[pl.ds(tail, h2_tiles.shape[0])])
    lg_ref[...] = logits
    for i in range(n_hist):
        newp_ref[i] = ext[steps + i]
    for i in range(n_chist):
        newc_ref[i] = vext[steps + i]


def _mixer_sample(x_tm, mod_s, pstate_tm, cstate_tm, n1, n2, w_in, w_pool, pscale, w_conv, w_out, w_r, b_r,
                  h2_all, steps):
    rows, d = x_tm.shape
    vmem = pl.BlockSpec(memory_space=pltpu.VMEM)
    hbm = pl.BlockSpec(memory_space=pl.ANY)
    return pl.pallas_call(
        functools.partial(_mixer_sample_kernel, steps=steps),
        out_shape=(
            jax.ShapeDtypeStruct((rows, d), F32),
            jax.ShapeDtypeStruct(h2_all.shape, F32),
            jax.ShapeDtypeStruct((N_EXPERTS, rows), F32),
            jax.ShapeDtypeStruct(pstate_tm.shape, F32),
            jax.ShapeDtypeStruct(cstate_tm.shape, F32),
        ),
        in_specs=[vmem] * 13 + [hbm],
        out_specs=(vmem, hbm, vmem, vmem, vmem),
        scratch_shapes=[pltpu.VMEM((rows * SUBLANES, LANES), F32)],
        input_output_aliases={13: 1},
        compiler_params=pltpu.CompilerParams(vmem_limit_bytes=VMEM_LIMIT),
        name="mixer_sample",
    )(x_tm, mod_s, pstate_tm, cstate_tm, n1, n2, w_in, w_pool, pscale, w_conv, w_out, w_r, b_r, h2_all)


def _route_kernel(lgp_ref, lgs_ref, dest_ref, gate_ref, cnt_ref, counts, start, before):
    ne = lgp_ref.shape[0]
    tr = before.shape[0]
    reps = tr // LANES
    n_prompt_chunks = lgp_ref.shape[1] // tr
    n_sample_chunks = lgs_ref.shape[1] // tr
    eidx = lax.broadcasted_iota(I32, (ne, tr), 0)

    def top_k(ref, c):
        work = ref[:, pl.ds(pl.multiple_of(c * tr, tr), tr)]
        top_v, onehots = [], []
        for _ in range(TOP_K):
            m = jnp.max(work, axis=0, keepdims=True)
            idx = jnp.min(jnp.where(work == m, eidx, ne), axis=0, keepdims=True)
            sel = eidx == idx
            top_v.append(m)
            onehots.append(sel)
            work = jnp.where(sel, -jnp.inf, work)
        mask = jnp.where(onehots[0] | onehots[1] | onehots[2] | onehots[3], 1.0, 0.0)
        chunk_counts = jnp.broadcast_to(jnp.sum(mask, axis=1, keepdims=True), (ne, LANES))
        return top_v, onehots, mask, chunk_counts

    def count_chunk(ref):
        def body(c, carry):
            counts[...] = counts[...] + top_k(ref, c)[3]
            return carry
        return body

    counts[...] = jnp.zeros_like(counts)
    lax.fori_loop(0, n_prompt_chunks, count_chunk(lgp_ref), 0)
    lax.fori_loop(0, n_sample_chunks, count_chunk(lgs_ref), 0)

    total = counts[...]
    hi = jnp.floor(total * (1.0 / 256.0))
    lo = total - hi * 256.0
    r = lax.broadcasted_iota(I32, (ne, ne), 0)
    col = lax.broadcasted_iota(I32, (ne, ne), 1)
    lower = jnp.where(col < r, 1.0, 0.0).astype(BF16)
    start[...] = 256.0 * _dot(lower, hi.astype(BF16)) + _dot(lower, lo.astype(BF16))
    cnt_ref[...] = total
    counts[...] = jnp.zeros_like(counts)

    r = lax.broadcasted_iota(I32, (tr, tr), 0)
    col = lax.broadcasted_iota(I32, (tr, tr), 1)
    before[...] = jnp.where(r < col, 1.0, 0.0).astype(BF16)
    gate_ref[...] = jnp.zeros_like(gate_ref)

    def place_chunk(ref, first_chunk):
        def body(c, carry):
            top_v, onehots, mask, chunk_counts = top_k(ref, c)
            base = jnp.concatenate([counts[...] + start[...]] * reps, axis=1)
            rank = _dot(mask.astype(BF16), before[...]) + base
            counts[...] = counts[...] + chunk_counts
            es = [jnp.exp(v - top_v[0]) for v in top_v]
            denom = es[0] + es[1] + es[2] + es[3]
            chunk = first_chunk + c
            cols = pl.ds(pl.multiple_of(chunk * tr, tr), tr)
            for k in range(TOP_K):
                d = jnp.sum(jnp.where(onehots[k], rank, 0.0), axis=0, keepdims=True).astype(I32)
                for j in range(reps):
                    dest_ref[chunk * reps + j, k:k + 1, :] = d[:, j * LANES:(j + 1) * LANES]
                gate_ref[k:k + 1, cols] = es[k] / denom
            return carry
        return body

    lax.fori_loop(0, n_prompt_chunks, place_chunk(lgp_ref, 0), 0)
    lax.fori_loop(0, n_sample_chunks, place_chunk(lgs_ref, n_prompt_chunks), 0)


def _route(lgt_p, lgt_s):
    ne, n_p = lgt_p.shape
    n_s = lgt_s.shape[1]
    tr = ROUTE_TILE
    return pl.pallas_call(
        _route_kernel,
        out_shape=(
            jax.ShapeDtypeStruct(((n_p + n_s) // LANES, TOP_K, LANES), I32),
            jax.ShapeDtypeStruct((SUBLANES, n_p + n_s), F32),
            jax.ShapeDtypeStruct((ne, LANES), F32),
        ),
        scratch_shapes=[pltpu.VMEM((ne, LANES), F32), pltpu.VMEM((ne, LANES), F32),
                        pltpu.VMEM((tr, tr), BF16)],
        compiler_params=pltpu.CompilerParams(vmem_limit_bytes=VMEM_LIMIT),
        name="route",
    )(lgt_p, lgt_s)


def _group_metadata(counts, n_rows, tile):
    n_tiles = n_rows // tile
    n_steps = n_tiles + N_EXPERTS - 1
    ends = jnp.cumsum(counts)
    offs = jnp.concatenate([jnp.zeros((1,), I32), ends]).astype(I32)
    first_tile = offs[:-1] // tile
    last_tile = (ends - 1) // tile
    tiles_e = jnp.where(counts > 0, last_tile - first_tile + 1, 0)
    step_end = jnp.cumsum(tiles_e)
    step_start = step_end - tiles_e
    n_active = step_end[-1]
    s = jnp.minimum(jnp.arange(n_steps, dtype=I32), n_active - 1)
    owner = ((s[:, None] >= step_start[None, :]) & (s[:, None] < step_end[None, :])).astype(I32)
    gid = jnp.sum(owner * jnp.arange(N_EXPERTS, dtype=I32)[None, :], axis=1)
    tid = jnp.sum(owner * (first_tile - step_start)[None, :], axis=1) + s
    ids = jnp.arange(N_EXPERTS, dtype=I32)
    later = (ids[None, :] > ids[:, None]) & (counts[None, :] > 0)
    next_e = jnp.min(jnp.where(later, ids[None, :], N_EXPERTS), axis=1)
    next_e = jnp.where(next_e == N_EXPERTS, -1, next_e)
    nxt = jnp.sum(owner * next_e[None, :], axis=1)
    return gid, tid, nxt, offs, n_active.reshape(1).astype(I32), n_steps


SC_LANES = 16
INVERT_CHUNK = 2048


def _invert(dest):
    r = dest.shape[0]
    mesh = plsc.VectorSubcoreMesh(core_axis_name="core", subcore_axis_name="subcore")
    cores = mesh.num_cores

    @functools.partial(
        pl.kernel, mesh=mesh, out_type=jax.ShapeDtypeStruct((r,), I32),
        scratch_types=[pltpu.VMEM((r,), I32), pltpu.VMEM((INVERT_CHUNK,), I32)],
        compiler_params=pltpu.CompilerParams(needs_layout_passes=False),
        name="invert",
    )
    def invert(dest_hbm, inv_hbm, table, chunk):
        worker = lax.axis_index("subcore") * cores + lax.axis_index("core")

        @pl.when(worker == 0)
        def _():
            lane = lax.iota(I32, SC_LANES)

            @pl.loop(0, r // INVERT_CHUNK)
            def _(ci):
                pltpu.sync_copy(dest_hbm.at[pl.ds(ci * INVERT_CHUNK, INVERT_CHUNK)], chunk)

                @pl.loop(0, INVERT_CHUNK // SC_LANES)
                def _(i):
                    d = chunk[pl.ds(i * SC_LANES, SC_LANES)]
                    p = ci * INVERT_CHUNK + i * SC_LANES + lane
                    tok = lax.shift_right_logical(p, 9) * LANES + (p & (LANES - 1))
                    slot = lax.shift_right_logical(p, 7) & (TOP_K - 1)
                    plsc.store_scatter(table, [d], tok * TOP_K + slot)

            pltpu.sync_copy(table, inv_hbm)

    return invert(dest)


def _experts_kernel(gid_ref, tid_ref, nxt_ref, offs_ref, nact_ref,
                    inv_cur_ref, inv_nxt_ref, bg_ref, bu_ref, bd_ref, h2_hbm, wg_hbm, wu_hbm, wd_hbm, out_ref,
                    wg_b, wu_b, wd_b, wg_f, wu_f, wd_f, relay, rows, sems, row_sems):
    s = pl.program_id(0)
    n_steps = pl.num_programs(0)
    tm = out_ref.shape[0] // SUBLANES
    landing = ((wg_hbm, wg_f, wg_b), (wu_hbm, wu_f, wu_b), (wd_hbm, wd_f, wd_b))

    def fetch(e):
        for j, (hbm, land, _) in enumerate(landing):
            pltpu.make_async_copy(hbm.at[e], land, sems.at[j]).start()

    def start_row(idx_ref, into, r):
        tok = lax.shift_right_logical(idx_ref[r], TOP_K.bit_length() - 1)
        src = h2_hbm.at[pl.ds(pl.multiple_of(tok * SUBLANES, SUBLANES), SUBLANES)]
        dst = rows.at[into, pl.ds(pl.multiple_of(r * SUBLANES, SUBLANES), SUBLANES)]
        return pltpu.make_async_copy(src, dst, row_sems.at[into])

    def gather_in_line(idx_ref, into):
        for r in range(tm):
            start_row(idx_ref, into, r).start(priority=r % 2)

    def gather_loop(idx_ref, into):
        def group(g, carry):
            for j in range(ISSUE_GROUP):
                start_row(idx_ref, into, g * ISSUE_GROUP + j).start(priority=j % 2)
            return carry
        lax.fori_loop(0, tm // ISSUE_GROUP, group, 0)

    def wait_rows(into):
        pltpu.make_async_copy(h2_hbm.at[pl.ds(0, tm * SUBLANES)], rows.at[into], row_sems.at[into]).wait()

    @pl.when(s < nact_ref[0])
    def _():
        e = gid_ref[s]
        m = tid_ref[s]
        cur = lax.rem(m, 2)
        last = nact_ref[0] - 1
        new_tile = (s == 0) | (tid_ref[jnp.maximum(s - 1, 0)] != m)
        next_differs = (s < last) & (tid_ref[jnp.minimum(s + 1, n_steps - 1)] != m)

        @pl.when(s == 0)
        def _():
            fetch(e)
            gather_loop(inv_cur_ref, cur)

        @pl.when((s == 0) | (gid_ref[jnp.maximum(s - 1, 0)] != e))
        def _():
            for j, (hbm, land, half) in enumerate(landing):
                pltpu.make_async_copy(hbm.at[e], land, sems.at[j]).wait()
                half[...] = land[...].astype(BF16)

            @pl.when(nxt_ref[s] >= 0)
            def _():
                fetch(nxt_ref[s])

        @pl.when(new_tile)
        def _():
            wait_rows(cur)

        def ffn(x):
            g = _dot(x, wg_b[...]) + bg_ref[pl.ds(e, 1), :]
            u = _dot(x, wu_b[...]) + bu_ref[pl.ds(e, 1), :]
            g = jnp.minimum(g, SWIGLU_LIMIT)
            u = jnp.clip(u, -SWIGLU_LIMIT, SWIGLU_LIMIT)
            glu = g * jax.nn.sigmoid(SWIGLU_ALPHA * g)
            return _dot(((u + 1.0) * glu).astype(BF16), wd_b[...]) + bd_ref[pl.ds(e, 1), :]

        lo = offs_ref[e]
        hi = offs_ref[e + 1]
        whole_tile = (lo <= m * tm) & (hi >= (m + 1) * tm)

        @pl.when(whole_tile)
        def _():
            x = _load_token_tiles(rows.at[cur]).astype(BF16)
            gather_in_line(inv_nxt_ref, 1 - cur)
            _store_token_tiles(out_ref, ffn(x))

        @pl.when(jnp.logical_not(whole_tile))
        def _():
            sub = relay.shape[0] // SUBLANES
            for j in range(tm // sub):
                first = m * tm + j * sub

                @pl.when((lo < first + sub) & (hi > first))
                def _():
                    span = pl.ds(j * sub * SUBLANES, sub * SUBLANES)
                    _store_token_tiles(relay, ffn(_load_token_tiles(rows.at[cur, span]).astype(BF16)))
                    row = first + lax.shift_right_logical(lax.broadcasted_iota(I32, relay.shape, 0), 3)
                    pltpu.store(out_ref.at[span], relay[...], mask=(row >= lo) & (row < hi))

            @pl.when(next_differs | (s == last))
            def _():
                gather_loop(inv_nxt_ref, 1 - cur)

        @pl.when(s == last)
        def _():
            wait_rows(1 - cur)


def _experts(gid, tid, nxt, offs, nact, n_steps, inv, h2, wg, bg, wu, bu, wd, bd):
    ne, d, f = wg.shape
    tm = EXPERT_TILE
    whole = lambda s, gid, tid, nxt, offs, nact: (0, 0)
    grid_spec = pltpu.PrefetchScalarGridSpec(
        num_scalar_prefetch=5,
        grid=(n_steps,),
        in_specs=[
            pl.BlockSpec((tm,), lambda s, gid, tid, nxt, offs, nact: (tid[s],), memory_space=pltpu.SMEM),
            pl.BlockSpec((tm,), lambda s, gid, tid, nxt, offs, nact: (tid[jnp.minimum(s + 1, n_steps - 1)],),
                         memory_space=pltpu.SMEM),
            pl.BlockSpec((ne, f), whole),
            pl.BlockSpec((ne, f), whole),
            pl.BlockSpec((ne, d), whole),
            pl.BlockSpec(memory_space=pl.ANY),
            pl.BlockSpec(memory_space=pl.ANY),
            pl.BlockSpec(memory_space=pl.ANY),
            pl.BlockSpec(memory_space=pl.ANY),
        ],
        out_specs=pl.BlockSpec((tm * SUBLANES, LANES), lambda s, gid, tid, nxt, offs, nact: (tid[s], 0)),
        scratch_shapes=[
            pltpu.VMEM((d, f), BF16), pltpu.VMEM((d, f), BF16), pltpu.VMEM((f, d), BF16),
            pltpu.VMEM((d, f), F32), pltpu.VMEM((d, f), F32), pltpu.VMEM((f, d), F32),
            pltpu.VMEM((EXPERT_SUBTILE * SUBLANES, LANES), F32),
            pltpu.VMEM((2, tm * SUBLANES, LANES), F32),
            pltpu.SemaphoreType.DMA((3,)),
            pltpu.SemaphoreType.DMA((2,)),
        ],
    )
    return pl.pallas_call(
        _experts_kernel,
        out_shape=jax.ShapeDtypeStruct((inv.shape[0] * SUBLANES, LANES), F32),
        grid_spec=grid_spec,
        compiler_params=pltpu.CompilerParams(
            dimension_semantics=("arbitrary",), vmem_limit_bytes=VMEM_LIMIT),
        name="experts",
    )(gid, tid, nxt, offs, nact, inv, inv, bg, bu, bd, h2, wg, wu, wd)


def _combine_kernel(dest_ref, dnext_ref, gate_ref, x1p_ref, x1s_ref, modp_ref, g2s_ref, fn_ref, y_hbm,
                    outp_ref, outs_ref, gbuf_a, gbuf_b, gates_t, sems, *, n_prompt_tiles):
    i = pl.program_id(0)
    n = pl.num_programs(0)
    tm = x1p_ref.shape[0]
    group = COMBINE_GROUP

    def start_copy(buf, sem, t, k, d):
        src = y_hbm.at[pl.ds(pl.multiple_of(d * SUBLANES, SUBLANES), SUBLANES)]
        dst = buf.at[k, pl.ds(pl.multiple_of(t * SUBLANES, SUBLANES), SUBLANES)]
        pltpu.make_async_copy(src, dst, sem).start(priority=k % 2)

    def wait_tile(buf, sem):
        for k in range(TOP_K):
            pltpu.make_async_copy(y_hbm.at[pl.ds(0, tm * SUBLANES)], buf.at[k], sem).wait()

    def step(cur, cur_sem, nxt, nxt_sem):
        @pl.when(i == 0)
        def _():
            _for_each_token_slot(dest_ref, functools.partial(start_copy, cur, cur_sem))

        wait_tile(cur, cur_sem)
        gates_t[...] = gate_ref[...].T
        fn = fn_ref[...]

        def run(x1_ref, out_ref, g2_rows):
            def body(g, carry, c):
                l0 = g * group
                base = pl.multiple_of(c * LANES + l0, group)
                idx = [[_slot_index(dnext_ref, c, k, l0 + j) for k in range(TOP_K)] for j in range(group)]
                for j in range(group):
                    for k in range(TOP_K):
                        start_copy(nxt, nxt_sem, base + j, k, idx[j][k])
                gt = gates_t[pl.ds(base, group), :]
                tile0 = pl.multiple_of(base * SUBLANES, group * SUBLANES)
                cols = []
                for c in range(SUBLANES):
                    acc = gt[:, 0:1] * cur[0, pl.ds(tile0 + c, group, stride=SUBLANES), :]
                    for k in range(1, TOP_K):
                        acc = acc + gt[:, k:k + 1] * cur[k, pl.ds(tile0 + c, group, stride=SUBLANES), :]
                    cols.append(acc)
                ffn = jnp.concatenate(cols, axis=-1)
                rows = pl.ds(base, group)
                out_ref[rows, :] = _rmsnorm(x1_ref[rows, :] + g2_rows(rows) * ffn, fn)
                return carry
            for c in range(tm // LANES):
                lax.fori_loop(0, LANES // group, functools.partial(body, c=c), 0)

        @pl.when(i < n_prompt_tiles)
        def _():
            g2 = modp_ref[...][5:6, :]
            run(x1p_ref, outp_ref, lambda rows: g2)

        @pl.when(i >= n_prompt_tiles)
        def _():
            run(x1s_ref, outs_ref, lambda rows: g2s_ref[rows, :])

        @pl.when(i == n - 1)
        def _():
            wait_tile(nxt, nxt_sem)

    @pl.when(lax.rem(i, 2) == 0)
    def _():
        step(gbuf_a, sems.at[0], gbuf_b, sems.at[1])

    @pl.when(lax.rem(i, 2) == 1)
    def _():
        step(gbuf_b, sems.at[1], gbuf_a, sems.at[0])


def _combine(dest, gates, x1_p, x1_s, mod_p, g2_s, final_norm, y_rows, tokens_per_seq):
    n_p, d = x1_p.shape
    n_s = x1_s.shape[0]
    tm = COMBINE_TILE
    npt, nst = n_p // tm, n_s // tm
    tiles_per_seq = tokens_per_seq // tm
    pmap = lambda i: (jnp.minimum(i, npt - 1), 0)
    smap = lambda i: (jnp.maximum(i - npt, 0), 0)
    return pl.pallas_call(
        functools.partial(_combine_kernel, n_prompt_tiles=npt),
        out_shape=(jax.ShapeDtypeStruct((n_p, d), F32), jax.ShapeDtypeStruct((n_s, d), F32)),
        grid=(npt + nst,),
        in_specs=[
            pl.BlockSpec((tm * TOP_K,), lambda i: (i,), memory_space=pltpu.SMEM),
            pl.BlockSpec((tm * TOP_K,), lambda i: (jnp.minimum(i + 1, npt + nst - 1),),
                         memory_space=pltpu.SMEM),
            pl.BlockSpec((SUBLANES, tm), lambda i: (0, i)),
            pl.BlockSpec((tm, d), pmap),
            pl.BlockSpec((tm, d), smap),
            pl.BlockSpec((None, 6, d), lambda i: (jnp.minimum(i, npt - 1) // tiles_per_seq, 0, 0)),
            pl.BlockSpec((tm, d), smap),
            pl.BlockSpec((1, d), lambda i: (0, 0)),
            pl.BlockSpec(memory_space=pl.ANY),
        ],
        out_specs=(pl.BlockSpec((tm, d), pmap), pl.BlockSpec((tm, d), smap)),
        scratch_shapes=[
            pltpu.VMEM((TOP_K, tm * SUBLANES, LANES), F32),
            pltpu.VMEM((TOP_K, tm * SUBLANES, LANES), F32),
            pltpu.VMEM((tm, SUBLANES), F32),
            pltpu.SemaphoreType.DMA((2,)),
        ],
        compiler_params=pltpu.CompilerParams(
            dimension_semantics=("arbitrary",), vmem_limit_bytes=VMEM_LIMIT),
        name="combine",
    )(dest, dest, gates, x1_p, x1_s, mod_p, g2_s, final_norm, y_rows)


def kernel(x_prompt, x_sample, state_pool, state_conv, c_prompt, c_sample, norm1, norm2, w_ada, b_ada,
           w_in, w_pool, pool_scale, w_conv, w_out, w_router, b_router, w_gate, b_gate, w_up, b_up,
           w_down, b_down, final_norm):
    depth = norm1.shape[0]
    assert depth == 1, "single-layer step"
    bp, tp, d = x_prompt.shape
    bs, ts, _ = x_sample.shape
    dp = state_pool.shape[-1]
    n_hist = state_pool.shape[2]
    n_chist = state_conv.shape[2]
    n_p, n_s = bp * tp, bs * ts
    assert d == SUBLANES * LANES, "token-tile layout assumes one vreg tile per token row"
    assert tp % TOKEN_TILE == 0 and tp % COMBINE_TILE == 0 and n_s % COMBINE_TILE == 0
    assert ((n_p + n_s) * TOP_K) % INVERT_CHUNK == 0
    assert (n_p + n_s) % ROUTE_TILE == 0 and ((n_p + n_s) * TOP_K) % EXPERT_TILE == 0

    l = 0
    n1 = norm1[l].reshape(1, d)
    n2 = norm2[l].reshape(1, d)
    w_in_b = w_in[l].astype(BF16)
    w_pool_b = w_pool[l].astype(BF16)
    w_out_b = w_out[l].astype(BF16)
    pscale = pool_scale[l].reshape(1, dp)
    w_r = w_router[l].T.astype(BF16)
    b_r = b_router[l].reshape(N_EXPERTS, 1)

    mod = _adaln(jnp.concatenate([c_sample, c_prompt], axis=0), w_ada[l], b_ada[l])
    mod_s = mod[:bs]
    mod_p = mod[bs:].reshape(bp, 6, d)

    x1_p, h2_p, lg_p, u_tail, v_tail = _mixer_prompt(
        x_prompt, mod_p, n1, n2, w_in_b, w_pool_b, pscale, w_conv[l], w_out_b, w_r, b_r, n_s)

    xs_tm = jnp.transpose(x_sample, (1, 0, 2)).reshape(n_s, d)
    ps_tm = jnp.transpose(state_pool[l], (1, 0, 2))
    cs_tm = jnp.transpose(state_conv[l], (1, 0, 2))
    x1_s, h2, lg_s, newp_tm, newc_tm = _mixer_sample(
        xs_tm, mod, ps_tm, cs_tm, n1, n2, w_in_b, w_pool_b, pscale, w_conv[l], w_out_b, w_r, b_r, h2_p, ts)

    dest, gates, counts_f = _route(lg_p, lg_s)
    counts = counts_f[:, 0].astype(I32)
    n_rows = (n_p + n_s) * TOP_K
    gid, tid, nxt, offs, nact, n_steps = _group_metadata(counts, n_rows, EXPERT_TILE)

    dest = dest.reshape(-1)
    inv = _invert(dest)
    y_rows = _experts(gid, tid, nxt, offs, nact, n_steps, inv, h2,
                      w_gate[l], b_gate[l], w_up[l], b_up[l], w_down[l], b_down[l])

    g2_s = jnp.tile(mod_s[:, 5 * d:], (ts, 1))
    y_p, y_s = _combine(dest, gates, x1_p, x1_s, mod_p, g2_s, final_norm.reshape(1, d), y_rows, tp)

    y_prompt = y_p.reshape(bp, tp, d)
    y_sample = jnp.transpose(y_s.reshape(ts, bs, d), (1, 0, 2))
    new_pool_prompt = u_tail[:, POOL_HALO - n_hist:, :][None]
    new_conv_prompt = v_tail[:, CONV_HALO - n_chist:, :][None]
    new_pool_sample = jnp.transpose(newp_tm, (1, 0, 2))[None]
    new_conv_sample = jnp.transpose(newc_tm, (1, 0, 2))[None]
    return (y_prompt, y_sample, new_pool_prompt, new_conv_prompt, new_pool_sample, new_conv_sample)
```

```python
import functools

import jax
import jax.numpy as jnp
from jax import lax
from jax.experimental import pallas as pl
from jax.experimental.pallas import tpu as pltpu
from jax.experimental.pallas import tpu_sc as plsc

F32 = jnp.float32
BF16 = jnp.bfloat16
I32 = jnp.int32

POOL_WINDOWS = (2, 4, 8, 16)
POOL_HALO = 16
CONV_TAPS = 3
CONV_HALO = 8
N_EXPERTS = 32
TOP_K = 4
SWIGLU_LIMIT = 7.0
SWIGLU_ALPHA = 1.702
EPS = 1e-5
PAST_LEN = 16384

LANES = 128
SUBLANES = 8

TOKEN_TILE = 512
ROUTE_TILE = 512
COMBINE_TILE = 512
COMBINE_GROUP = 32
EXPERT_TILE = 512
EXPERT_SUBTILE = 128
VMEM_LIMIT = 56 * 1024 * 1024


def _rmsnorm(x, g):
    ms = jnp.mean(x * x, axis=-1, keepdims=True)
    return x * lax.rsqrt(ms + EPS) * g


def _dot(a, b):
    return jnp.dot(a, b, preferred_element_type=F32)


def _store_token_tiles(ref, val):
    rows = val.shape[0]
    for c in range(SUBLANES):
        ref[pl.ds(c, rows, stride=SUBLANES), :] = val[:, c * LANES:(c + 1) * LANES]


def _load_token_tiles(ref):
    rows = ref.shape[0] // SUBLANES
    return jnp.concatenate([ref[pl.ds(c, rows, stride=SUBLANES), :] for c in range(SUBLANES)], axis=-1)


ISSUE_GROUP = 4


def _slot_index(idx_ref, c, k, lane):
    return idx_ref[(c * TOP_K + k) * LANES + lane]


def _for_each_token_slot(idx_ref, start_copy):
    for c in range(idx_ref.shape[0] // (TOP_K * LANES)):
        def group(g, carry, c=c):
            l0 = g * ISSUE_GROUP
            idx = [[_slot_index(idx_ref, c, k, l0 + j) for k in range(TOP_K)] for j in range(ISSUE_GROUP)]
            for j in range(ISSUE_GROUP):
                for k in range(TOP_K):
                    start_copy(c * LANES + l0 + j, k, idx[j][k])
            return carry
        lax.fori_loop(0, LANES // ISSUE_GROUP, group, 0)


def _adaln_kernel(c_ref, w_ref, b_ref, o_ref):
    c = c_ref[...]
    s = c * jax.nn.sigmoid(c)
    o_ref[...] = _dot(s.astype(BF16), w_ref[...].astype(BF16)) + b_ref[...]


def _adaln(c, w_ada, b_ada):
    rows, d = c.shape
    n = w_ada.shape[1]
    tn = 1024
    return pl.pallas_call(
        _adaln_kernel,
        out_shape=jax.ShapeDtypeStruct((rows, n), F32),
        grid=(n // tn,),
        in_specs=[
            pl.BlockSpec((rows, d), lambda j: (0, 0)),
            pl.BlockSpec((d, tn), lambda j: (0, j)),
            pl.BlockSpec((1, tn), lambda j: (0, j)),
        ],
        out_specs=pl.BlockSpec((rows, tn), lambda j: (0, j)),
        compiler_params=pltpu.CompilerParams(
            dimension_semantics=("arbitrary",), vmem_limit_bytes=VMEM_LIMIT),
        name="adaln",
    )(c, w_ada, b_ada.reshape(1, n))


def _mix_tail(x, pool_in, conv_out, g1, sc2, sh2, n2, wpool_ref, pscale, wout, wr, br):
    gw = pool_in.shape[1] // len(POOL_WINDOWS)
    mixed = [_dot(pool_in[:, g * gw:(g + 1) * gw].astype(BF16), wpool_ref[g])
             for g in range(len(POOL_WINDOWS))]
    pool_out = jnp.concatenate(mixed, axis=-1) * pscale
    mix_in = jnp.concatenate([pool_out, conv_out], axis=-1).astype(BF16)
    x1 = x + g1 * _dot(mix_in, wout)
    h2 = _rmsnorm(x1, n2) * (1.0 + sc2) + sh2
    logits_t = lax.dot_general(wr, h2.astype(BF16), (((1,), (1,)), ((), ())),
                               preferred_element_type=F32) + br
    return x1, h2, logits_t


def _mixer_prompt_kernel(x_ref, mod_ref, n1_ref, n2_ref, win_ref, wpool_ref, pscale_ref, wconv_ref,
                         wout_ref, wr_ref, br_ref, h2s_ref,
                         x1_ref, h2_ref, lg_ref, upool_ref, vconv_ref, ubuf, vbuf, *, tiles_per_seq):
    i = pl.program_id(0)

    @pl.when(i < pl.num_programs(0) - 1)
    def _():
        _mixer_prompt_tile(x_ref, mod_ref, n1_ref, n2_ref, win_ref, wpool_ref, pscale_ref, wconv_ref,
                           wout_ref, wr_ref, br_ref, x1_ref, h2_ref, lg_ref, upool_ref, vconv_ref,
                           ubuf, vbuf, lax.rem(i, tiles_per_seq), tiles_per_seq)

    @pl.when(i == pl.num_programs(0) - 1)
    def _():
        h2_ref[...] = h2s_ref[...]


def _mixer_prompt_tile(x_ref, mod_ref, n1_ref, n2_ref, win_ref, wpool_ref, pscale_ref, wconv_ref,
                       wout_ref, wr_ref, br_ref, x1_ref, h2_ref, lg_ref, upool_ref, vconv_ref,
                       ubuf, vbuf, t, tiles_per_seq):
    tt = x_ref.shape[0]
    dp = ubuf.shape[1]
    gw = dp // len(POOL_WINDOWS)

    @pl.when(t == 0)
    def _():
        ubuf[0:POOL_HALO, :] = jnp.zeros((POOL_HALO, dp), F32)
        vbuf[0:CONV_HALO, :] = jnp.zeros((CONV_HALO, dp), F32)

    x = x_ref[...]
    mod = mod_ref[...]
    sh1, sc1, g1, sh2, sc2, _ = [mod[i:i + 1, :] for i in range(6)]
    h = _rmsnorm(x, n1_ref[...]) * (1.0 + sc1) + sh1
    z = _dot(h.astype(BF16), win_ref[...])
    u, gate_b, gate_c, val = [z[:, i * dp:(i + 1) * dp] for i in range(4)]

    ubuf[POOL_HALO:POOL_HALO + tt, :] = u
    pos = lax.broadcasted_iota(I32, (tt, gw), 0) + t * tt
    pooled = []
    for g, w in enumerate(POOL_WINDOWS):
        cols = slice(g * gw, (g + 1) * gw)
        acc = u[:, cols]
        for j in range(1, w):
            acc = acc + ubuf[POOL_HALO - j:POOL_HALO - j + tt, cols]
        cnt = jnp.minimum(pos + 1, w).astype(F32)
        pooled.append(acc / cnt - u[:, cols])
    pool_in = jnp.concatenate(pooled, axis=-1)

    v = gate_c * val
    vbuf[CONV_HALO:CONV_HALO + tt, :] = v
    wc = wconv_ref[...]
    y = (wc[0:1, :] * vbuf[CONV_HALO - 2:CONV_HALO - 2 + tt, :]
         + wc[1:2, :] * vbuf[CONV_HALO - 1:CONV_HALO - 1 + tt, :]
         + wc[2:3, :] * v)
    conv_out = gate_b * y

    x1, h2, logits = _mix_tail(x, pool_in, conv_out, g1, sc2, sh2, n2_ref[...], wpool_ref,
                               pscale_ref[...], wout_ref[...], wr_ref[...], br_ref[...])
    x1_ref[...] = x1
    _store_token_tiles(h2_ref, h2)
    lg_ref[...] = logits

    ubuf[0:POOL_HALO, :] = ubuf[tt:tt + POOL_HALO, :]
    vbuf[0:CONV_HALO, :] = vbuf[tt:tt + CONV_HALO, :]

    @pl.when(t == tiles_per_seq - 1)
    def _():
        upool_ref[...] = ubuf[0:POOL_HALO, :]
        vconv_ref[...] = vbuf[0:CONV_HALO, :]


def _mixer_prompt(x, mod_p, n1, n2, w_in, w_pool, pscale, w_conv, w_out, w_r, b_r, h2_s):
    b, t, d = x.shape
    dp = w_pool.shape[0] * w_pool.shape[1]
    tt = min(TOKEN_TILE, t)
    nt = t // tt
    n_real = b * nt
    assert h2_s.shape[0] == tt * SUBLANES
    seq = lambda i: jnp.minimum(i, n_real - 1) // nt
    tile = lambda i: jnp.minimum(i, n_real - 1)
    const2 = lambda i: (0, 0)
    const3 = lambda i: (0, 0, 0)
    return pl.pallas_call(
        functools.partial(_mixer_prompt_kernel, tiles_per_seq=nt),
        out_shape=(
            jax.ShapeDtypeStruct((b * t, d), F32),
            jax.ShapeDtypeStruct(((n_real + 1) * tt * SUBLANES, LANES), F32),
            jax.ShapeDtypeStruct((N_EXPERTS, b * t), F32),
            jax.ShapeDtypeStruct((b, POOL_HALO, dp), F32),
            jax.ShapeDtypeStruct((b, CONV_HALO, dp), F32),
        ),
        grid=(n_real + 1,),
        in_specs=[
            pl.BlockSpec((None, tt, d), lambda i: (seq(i), tile(i) % nt, 0)),
            pl.BlockSpec((None, 6, d), lambda i: (seq(i), 0, 0)),
            pl.BlockSpec((1, d), const2),
            pl.BlockSpec((1, d), const2),
            pl.BlockSpec(w_in.shape, const2),
            pl.BlockSpec(w_pool.shape, const3),
            pl.BlockSpec((1, dp), const2),
            pl.BlockSpec(w_conv.shape, const2),
            pl.BlockSpec(w_out.shape, const2),
            pl.BlockSpec(w_r.shape, const2),
            pl.BlockSpec((N_EXPERTS, 1), const2),
            pl.BlockSpec(h2_s.shape, const2),
        ],
        out_specs=(
            pl.BlockSpec((tt, d), lambda i: (tile(i), 0)),
            pl.BlockSpec((tt * SUBLANES, LANES), lambda i: (i, 0)),
            pl.BlockSpec((N_EXPERTS, tt), lambda i: (0, tile(i))),
            pl.BlockSpec((None, POOL_HALO, dp), lambda i: (seq(i), 0, 0)),
            pl.BlockSpec((None, CONV_HALO, dp), lambda i: (seq(i), 0, 0)),
        ),
        scratch_shapes=[
            pltpu.VMEM((POOL_HALO + tt, dp), F32),
            pltpu.VMEM((CONV_HALO + tt, dp), F32),
        ],
        compiler_params=pltpu.CompilerParams(
            dimension_semantics=("arbitrary",), vmem_limit_bytes=VMEM_LIMIT),
        name="mixer_prompt",
    )(x, mod_p, n1, n2, w_in, w_pool, pscale, w_conv, w_out, w_r, b_r, h2_s)


def _mixer_sample_kernel(x_ref, mod_ref, pstate_ref, cstate_ref, n1_ref, n2_ref, win_ref, wpool_ref,
                         pscale_ref, wconv_ref, wout_ref, wr_ref, br_ref,
                         x1_ref, h2_ref, lg_ref, newp_ref, newc_ref, *, steps):
    nb = pstate_ref.shape[1]
    d = x_ref.shape[1]
    dp = pstate_ref.shape[2]
    gw = dp // len(POOL_WINDOWS)
    n_hist = pstate_ref.shape[0]
    n_chist = cstate_ref.shape[0]

    x = x_ref[...]
    mod = mod_ref[0:nb, :]
    rep = lambda a: jnp.concatenate([a] * steps, axis=0)
    sh1, sc1, g1, sh2, sc2, _ = [rep(mod[:, i * d:(i + 1) * d]) for i in range(6)]
    h = _rmsnorm(x, n1_ref[...]) * (1.0 + sc1) + sh1
    z = _dot(h.astype(BF16), win_ref[...])
    u, gate_b, gate_c, val = [z[:, i * dp:(i + 1) * dp] for i in range(4)]

    ext = [pstate_ref[i] for i in range(n_hist)] + [u[s * nb:(s + 1) * nb, :] for s in range(steps)]
    pooled_steps = []
    for s in range(steps):
        groups = []
        for g, w in enumerate(POOL_WINDOWS):
            cols = slice(g * gw, (g + 1) * gw)
            acc = ext[n_hist + s][:, cols]
            for j in range(1, w):
                acc = acc + ext[n_hist + s - j][:, cols]
            cnt = float(min(PAST_LEN + s + 1, w))
            groups.append(acc / cnt - ext[n_hist + s][:, cols])
        pooled_steps.append(jnp.concatenate(groups, axis=-1))
    pool_in = jnp.concatenate(pooled_steps, axis=0)

    v = gate_c * val
    vext = [cstate_ref[i] for i in range(n_chist)] + [v[s * nb:(s + 1) * nb, :] for s in range(steps)]
    wc = wconv_ref[...]
    y = jnp.concatenate(
        [wc[0:1, :] * vext[s] + wc[1:2, :] * vext[s + 1] + wc[2:3, :] * vext[s + 2] for s in range(steps)],
        axis=0)
    conv_out = gate_b * y

    x1, h2, logits = _mix_tail(x, pool_in, conv_out, g1, sc2, sh2, n2_ref[...], wpool_ref,
                               pscale_ref[...], wout_ref[...], wr_ref[...], br_ref[...])
    x1_ref[...] = x1
    _store_token_tiles(h2_ref, h2)
    lg_ref[...] = logits
    for i in range(n_hist):
        newp_ref[i] = ext[steps + i]
    for i in range(n_chist):
        newc_ref[i] = vext[steps + i]


def _mixer_sample(x_tm, mod_s, pstate_tm, cstate_tm, n1, n2, w_in, w_pool, pscale, w_conv, w_out, w_r, b_r,
                  steps):
    rows, d = x_tm.shape
    return pl.pallas_call(
        functools.partial(_mixer_sample_kernel, steps=steps),
        out_shape=(
            jax.ShapeDtypeStruct((rows, d), F32),
            jax.ShapeDtypeStruct((rows * SUBLANES, LANES), F32),
            jax.ShapeDtypeStruct((N_EXPERTS, rows), F32),
            jax.ShapeDtypeStruct(pstate_tm.shape, F32),
            jax.ShapeDtypeStruct(cstate_tm.shape, F32),
        ),
        compiler_params=pltpu.CompilerParams(vmem_limit_bytes=VMEM_LIMIT),
        name="mixer_sample",
    )(x_tm, mod_s, pstate_tm, cstate_tm, n1, n2, w_in, w_pool, pscale, w_conv, w_out, w_r, b_r)


def _route_kernel(lgp_ref, lgs_ref, dest_ref, gate_ref, cnt_ref, counts, start, before):
    ne = lgp_ref.shape[0]
    tr = before.shape[0]
    reps = tr // LANES
    n_prompt_chunks = lgp_ref.shape[1] // tr
    n_sample_chunks = lgs_ref.shape[1] // tr
    eidx = lax.broadcasted_iota(I32, (ne, tr), 0)

    def top_k(ref, c):
        work = ref[:, pl.ds(pl.multiple_of(c * tr, tr), tr)]
        top_v, onehots = [], []
        for _ in range(TOP_K):
            m = jnp.max(work, axis=0, keepdims=True)
            idx = jnp.min(jnp.where(work == m, eidx, ne), axis=0, keepdims=True)
            sel = eidx == idx
            top_v.append(m)
            onehots.append(sel)
            work = jnp.where(sel, -jnp.inf, work)
        mask = jnp.where(onehots[0] | onehots[1] | onehots[2] | onehots[3], 1.0, 0.0)
        chunk_counts = jnp.broadcast_to(jnp.sum(mask, axis=1, keepdims=True), (ne, LANES))
        return top_v, onehots, mask, chunk_counts

    def count_chunk(ref):
        def body(c, carry):
            counts[...] = counts[...] + top_k(ref, c)[3]
            return carry
        return body

    counts[...] = jnp.zeros_like(counts)
    lax.fori_loop(0, n_prompt_chunks, count_chunk(lgp_ref), 0)
    lax.fori_loop(0, n_sample_chunks, count_chunk(lgs_ref), 0)

    total = counts[...]
    hi = jnp.floor(total * (1.0 / 256.0))
    lo = total - hi * 256.0
    r = lax.broadcasted_iota(I32, (ne, ne), 0)
    col = lax.broadcasted_iota(I32, (ne, ne), 1)
    lower = jnp.where(col < r, 1.0, 0.0).astype(BF16)
    start[...] = 256.0 * _dot(lower, hi.astype(BF16)) + _dot(lower, lo.astype(BF16))
    cnt_ref[...] = total
    counts[...] = jnp.zeros_like(counts)

    r = lax.broadcasted_iota(I32, (tr, tr), 0)
    col = lax.broadcasted_iota(I32, (tr, tr), 1)
    before[...] = jnp.where(r < col, 1.0, 0.0).astype(BF16)
    gate_ref[...] = jnp.zeros_like(gate_ref)

    def place_chunk(ref, first_chunk):
        def body(c, carry):
            top_v, onehots, mask, chunk_counts = top_k(ref, c)
            base = jnp.concatenate([counts[...] + start[...]] * reps, axis=1)
            rank = _dot(mask.astype(BF16), before[...]) + base
            counts[...] = counts[...] + chunk_counts
            es = [jnp.exp(v - top_v[0]) for v in top_v]
            denom = es[0] + es[1] + es[2] + es[3]
            chunk = first_chunk + c
            cols = pl.ds(pl.multiple_of(chunk * tr, tr), tr)
            for k in range(TOP_K):
                d = jnp.sum(jnp.where(onehots[k], rank, 0.0), axis=0, keepdims=True).astype(I32)
                for j in range(reps):
                    dest_ref[chunk * reps + j, k:k + 1, :] = d[:, j * LANES:(j + 1) * LANES]
                gate_ref[k:k + 1, cols] = es[k] / denom
            return carry
        return body

    lax.fori_loop(0, n_prompt_chunks, place_chunk(lgp_ref, 0), 0)
    lax.fori_loop(0, n_sample_chunks, place_chunk(lgs_ref, n_prompt_chunks), 0)


def _route(lgt_p, lgt_s):
    ne, n_p = lgt_p.shape
    n_s = lgt_s.shape[1]
    tr = ROUTE_TILE
    return pl.pallas_call(
        _route_kernel,
        out_shape=(
            jax.ShapeDtypeStruct(((n_p + n_s) // LANES, TOP_K, LANES), I32),
            jax.ShapeDtypeStruct((SUBLANES, n_p + n_s), F32),
            jax.ShapeDtypeStruct((ne, LANES), F32),
        ),
        scratch_shapes=[pltpu.VMEM((ne, LANES), F32), pltpu.VMEM((ne, LANES), F32),
                        pltpu.VMEM((tr, tr), BF16)],
        compiler_params=pltpu.CompilerParams(vmem_limit_bytes=VMEM_LIMIT),
        name="route",
    )(lgt_p, lgt_s)


def _group_metadata(counts, n_rows, tile):
    n_tiles = n_rows // tile
    n_steps = n_tiles + N_EXPERTS - 1
    ends = jnp.cumsum(counts)
    offs = jnp.concatenate([jnp.zeros((1,), I32), ends]).astype(I32)
    first_tile = offs[:-1] // tile
    last_tile = (ends - 1) // tile
    tiles_e = jnp.where(counts > 0, last_tile - first_tile + 1, 0)
    step_end = jnp.cumsum(tiles_e)
    step_start = step_end - tiles_e
    n_active = step_end[-1]
    s = jnp.minimum(jnp.arange(n_steps, dtype=I32), n_active - 1)
    owner = ((s[:, None] >= step_start[None, :]) & (s[:, None] < step_end[None, :])).astype(I32)
    gid = jnp.sum(owner * jnp.arange(N_EXPERTS, dtype=I32)[None, :], axis=1)
    tid = jnp.sum(owner * (first_tile - step_start)[None, :], axis=1) + s
    ids = jnp.arange(N_EXPERTS, dtype=I32)
    later = (ids[None, :] > ids[:, None]) & (counts[None, :] > 0)
    next_e = jnp.min(jnp.where(later, ids[None, :], N_EXPERTS), axis=1)
    next_e = jnp.where(next_e == N_EXPERTS, -1, next_e)
    nxt = jnp.sum(owner * next_e[None, :], axis=1)
    return gid, tid, nxt, offs, n_active.reshape(1).astype(I32), n_steps


SC_LANES = 16
INVERT_CHUNK = 2048


def _invert(dest):
    r = dest.shape[0]
    mesh = plsc.VectorSubcoreMesh(core_axis_name="core", subcore_axis_name="subcore")
    cores = mesh.num_cores

    @functools.partial(
        pl.kernel, mesh=mesh, out_type=jax.ShapeDtypeStruct((r,), I32),
        scratch_types=[pltpu.VMEM((r,), I32), pltpu.VMEM((INVERT_CHUNK,), I32)],
        compiler_params=pltpu.CompilerParams(needs_layout_passes=False),
        name="invert",
    )
    def invert(dest_hbm, inv_hbm, table, chunk):
        worker = lax.axis_index("subcore") * cores + lax.axis_index("core")

        @pl.when(worker == 0)
        def _():
            lane = lax.iota(I32, SC_LANES)

            @pl.loop(0, r // INVERT_CHUNK)
            def _(ci):
                pltpu.sync_copy(dest_hbm.at[pl.ds(ci * INVERT_CHUNK, INVERT_CHUNK)], chunk)

                @pl.loop(0, INVERT_CHUNK // SC_LANES)
                def _(i):
                    d = chunk[pl.ds(i * SC_LANES, SC_LANES)]
                    p = ci * INVERT_CHUNK + i * SC_LANES + lane
                    tok = lax.shift_right_logical(p, 9) * LANES + (p & (LANES - 1))
                    slot = lax.shift_right_logical(p, 7) & (TOP_K - 1)
                    plsc.store_scatter(table, [d], tok * TOP_K + slot)

            pltpu.sync_copy(table, inv_hbm)

    return invert(dest)


def _experts_kernel(gid_ref, tid_ref, nxt_ref, offs_ref, nact_ref,
                    inv0_ref, inv1_ref, inv2_ref, bg_ref, bu_ref, bd_ref, h2_hbm, wg_hbm, wu_hbm, wd_hbm, out_ref,
                    wg_b, wu_b, wd_b, wg_f, wu_f, wd_f, relay, rows, sems, row_sems):
    s = pl.program_id(0)
    n_steps = pl.num_programs(0)
    tm = out_ref.shape[0] // SUBLANES
    landing = ((wg_hbm, wg_f, wg_b), (wu_hbm, wu_f, wu_b), (wd_hbm, wd_f, wd_b))

    def fetch(e):
        for j, (hbm, land, _) in enumerate(landing):
            pltpu.make_async_copy(hbm.at[e], land, sems.at[j]).start()

    def start_row(idx_ref, into, r):
        tok = lax.shift_right_logical(idx_ref[r], TOP_K.bit_length() - 1)
        src = h2_hbm.at[pl.ds(pl.multiple_of(tok * SUBLANES, SUBLANES), SUBLANES)]
        dst = rows.at[into, pl.ds(pl.multiple_of(r * SUBLANES, SUBLANES), SUBLANES)]
        return pltpu.make_async_copy(src, dst, row_sems.at[into])

    def gather_in_line(idx_ref, into):
        for r in range(tm):
            start_row(idx_ref, into, r).start(priority=r % 2)

    def gather_loop(idx_ref, into):
        def group(g, carry):
            for j in range(ISSUE_GROUP):
                start_row(idx_ref, into, g * ISSUE_GROUP + j).start(priority=j % 2)
            return carry
        lax.fori_loop(0, tm // ISSUE_GROUP, group, 0)

    def wait_rows(into):
        pltpu.make_async_copy(h2_hbm.at[pl.ds(0, tm * SUBLANES)], rows.at[into], row_sems.at[into]).wait()

    @pl.when(s < nact_ref[0])
    def _():
        e = gid_ref[s]
        m = tid_ref[s]
        cur = lax.rem(m, 3)
        ahead = lax.rem(m + 2, 3)
        last = nact_ref[0] - 1
        new_tile = (s == 0) | (tid_ref[jnp.maximum(s - 1, 0)] != m)
        next_differs = (s < last) & (tid_ref[jnp.minimum(s + 1, n_steps - 1)] != m)

        @pl.when(s == 0)
        def _():
            fetch(e)
            gather_loop(inv0_ref, cur)
            gather_loop(inv1_ref, lax.rem(m + 1, 3))

        @pl.when((s == 0) | (gid_ref[jnp.maximum(s - 1, 0)] != e))
        def _():
            for j, (hbm, land, half) in enumerate(landing):
                pltpu.make_async_copy(hbm.at[e], land, sems.at[j]).wait()
                half[...] = land[...].astype(BF16)

            @pl.when(nxt_ref[s] >= 0)
            def _():
                fetch(nxt_ref[s])

        @pl.when(new_tile)
        def _():
            wait_rows(cur)

        def ffn(x):
            g = _dot(x, wg_b[...]) + bg_ref[pl.ds(e, 1), :]
            u = _dot(x, wu_b[...]) + bu_ref[pl.ds(e, 1), :]
            g = jnp.minimum(g, SWIGLU_LIMIT)
            u = jnp.clip(u, -SWIGLU_LIMIT, SWIGLU_LIMIT)
            glu = g * jax.nn.sigmoid(SWIGLU_ALPHA * g)
            return _dot(((u + 1.0) * glu).astype(BF16), wd_b[...]) + bd_ref[pl.ds(e, 1), :]

        lo = offs_ref[e]
        hi = offs_ref[e + 1]
        whole_tile = (lo <= m * tm) & (hi >= (m + 1) * tm)

        @pl.when(whole_tile)
        def _():
            x = _load_token_tiles(rows.at[cur]).astype(BF16)
            gather_in_line(inv2_ref, ahead)
            _store_token_tiles(out_ref, ffn(x))

        @pl.when(jnp.logical_not(whole_tile))
        def _():
            sub = relay.shape[0] // SUBLANES
            for j in range(tm // sub):
                first = m * tm + j * sub

                @pl.when((lo < first + sub) & (hi > first))
                def _():
                    span = pl.ds(j * sub * SUBLANES, sub * SUBLANES)
                    _store_token_tiles(relay, ffn(_load_token_tiles(rows.at[cur, span]).astype(BF16)))
                    row = first + lax.shift_right_logical(lax.broadcasted_iota(I32, relay.shape, 0), 3)
                    pltpu.store(out_ref.at[span], relay[...], mask=(row >= lo) & (row < hi))

            @pl.when(next_differs | (s == last))
            def _():
                gather_loop(inv2_ref, ahead)

        @pl.when(s == last)
        def _():
            wait_rows(lax.rem(m + 1, 3))
            wait_rows(ahead)


def _experts(gid, tid, nxt, offs, nact, n_steps, inv, h2, wg, bg, wu, bu, wd, bd):
    ne, d, f = wg.shape
    tm = EXPERT_TILE
    whole = lambda s, gid, tid, nxt, offs, nact: (0, 0)
    last_tile = inv.shape[0] // tm - 1

    def order_of(k):
        return pl.BlockSpec((tm,), lambda s, gid, tid, nxt, offs, nact: (jnp.minimum(tid[s] + k, last_tile),),
                            memory_space=pltpu.SMEM)

    grid_spec = pltpu.PrefetchScalarGridSpec(
        num_scalar_prefetch=5,
        grid=(n_steps,),
        in_specs=[
            order_of(0), order_of(1), order_of(2),
            pl.BlockSpec((ne, f), whole),
            pl.BlockSpec((ne, f), whole),
            pl.BlockSpec((ne, d), whole),
            pl.BlockSpec(memory_space=pl.ANY),
            pl.BlockSpec(memory_space=pl.ANY),
            pl.BlockSpec(memory_space=pl.ANY),
            pl.BlockSpec(memory_space=pl.ANY),
        ],
        out_specs=pl.BlockSpec((tm * SUBLANES, LANES), lambda s, gid, tid, nxt, offs, nact: (tid[s], 0)),
        scratch_shapes=[
            pltpu.VMEM((d, f), BF16), pltpu.VMEM((d, f), BF16), pltpu.VMEM((f, d), BF16),
            pltpu.VMEM((d, f), F32), pltpu.VMEM((d, f), F32), pltpu.VMEM((f, d), F32),
            pltpu.VMEM((EXPERT_SUBTILE * SUBLANES, LANES), F32),
            pltpu.VMEM((3, tm * SUBLANES, LANES), F32),
            pltpu.SemaphoreType.DMA((3,)),
            pltpu.SemaphoreType.DMA((3,)),
        ],
    )
    return pl.pallas_call(
        _experts_kernel,
        out_shape=jax.ShapeDtypeStruct((inv.shape[0] * SUBLANES, LANES), F32),
        grid_spec=grid_spec,
        compiler_params=pltpu.CompilerParams(
            dimension_semantics=("arbitrary",), vmem_limit_bytes=VMEM_LIMIT),
        name="experts",
    )(gid, tid, nxt, offs, nact, inv, inv, inv, bg, bu, bd, h2, wg, wu, wd)


def _combine_kernel(dest_ref, dnext_ref, gate_ref, x1p_ref, x1s_ref, modp_ref, g2s_ref, fn_ref, y_hbm,
                    outp_ref, outs_ref, gbuf_a, gbuf_b, gates_t, sems, *, n_prompt_tiles):
    i = pl.program_id(0)
    n = pl.num_programs(0)
    tm = x1p_ref.shape[0]
    group = COMBINE_GROUP

    def start_copy(buf, sem, t, k, d):
        src = y_hbm.at[pl.ds(pl.multiple_of(d * SUBLANES, SUBLANES), SUBLANES)]
        dst = buf.at[k, pl.ds(pl.multiple_of(t * SUBLANES, SUBLANES), SUBLANES)]
        pltpu.make_async_copy(src, dst, sem).start(priority=k % 2)

    def wait_tile(buf, sem):
        for k in range(TOP_K):
            pltpu.make_async_copy(y_hbm.at[pl.ds(0, tm * SUBLANES)], buf.at[k], sem).wait()

    def step(cur, cur_sem, nxt, nxt_sem):
        @pl.when(i == 0)
        def _():
            _for_each_token_slot(dest_ref, functools.partial(start_copy, cur, cur_sem))

        wait_tile(cur, cur_sem)
        gates_t[...] = gate_ref[...].T
        fn = fn_ref[...]

        def run(x1_ref, out_ref, g2_rows):
            def body(g, carry, c):
                l0 = g * group
                base = pl.multiple_of(c * LANES + l0, group)
                idx = [[_slot_index(dnext_ref, c, k, l0 + j) for k in range(TOP_K)] for j in range(group)]
                for j in range(group):
                    for k in range(TOP_K):
                        start_copy(nxt, nxt_sem, base + j, k, idx[j][k])
                gt = gates_t[pl.ds(base, group), :]
                tile0 = pl.multiple_of(base * SUBLANES, group * SUBLANES)
                cols = []
                for c in range(SUBLANES):
                    acc = gt[:, 0:1] * cur[0, pl.ds(tile0 + c, group, stride=SUBLANES), :]
                    for k in range(1, TOP_K):
                        acc = acc + gt[:, k:k + 1] * cur[k, pl.ds(tile0 + c, group, stride=SUBLANES), :]
                    cols.append(acc)
                ffn = jnp.concatenate(cols, axis=-1)
                rows = pl.ds(base, group)
                out_ref[rows, :] = _rmsnorm(x1_ref[rows, :] + g2_rows(rows) * ffn, fn)
                return carry
            for c in range(tm // LANES):
                lax.fori_loop(0, LANES // group, functools.partial(body, c=c), 0)

        @pl.when(i < n_prompt_tiles)
        def _():
            g2 = modp_ref[...][5:6, :]
            run(x1p_ref, outp_ref, lambda rows: g2)

        @pl.when(i >= n_prompt_tiles)
        def _():
            run(x1s_ref, outs_ref, lambda rows: g2s_ref[rows, :])

        @pl.when(i == n - 1)
        def _():
            wait_tile(nxt, nxt_sem)

    @pl.when(lax.rem(i, 2) == 0)
    def _():
        step(gbuf_a, sems.at[0], gbuf_b, sems.at[1])

    @pl.when(lax.rem(i, 2) == 1)
    def _():
        step(gbuf_b, sems.at[1], gbuf_a, sems.at[0])


def _combine(dest, gates, x1_p, x1_s, mod_p, g2_s, final_norm, y_rows, tokens_per_seq):
    n_p, d = x1_p.shape
    n_s = x1_s.shape[0]
    tm = COMBINE_TILE
    npt, nst = n_p // tm, n_s // tm
    tiles_per_seq = tokens_per_seq // tm
    pmap = lambda i: (jnp.minimum(i, npt - 1), 0)
    smap = lambda i: (jnp.maximum(i - npt, 0), 0)
    return pl.pallas_call(
        functools.partial(_combine_kernel, n_prompt_tiles=npt),
        out_shape=(jax.ShapeDtypeStruct((n_p, d), F32), jax.ShapeDtypeStruct((n_s, d), F32)),
        grid=(npt + nst,),
        in_specs=[
            pl.BlockSpec((tm * TOP_K,), lambda i: (i,), memory_space=pltpu.SMEM),
            pl.BlockSpec((tm * TOP_K,), lambda i: (jnp.minimum(i + 1, npt + nst - 1),),
                         memory_space=pltpu.SMEM),
            pl.BlockSpec((SUBLANES, tm), lambda i: (0, i)),
            pl.BlockSpec((tm, d), pmap),
            pl.BlockSpec((tm, d), smap),
            pl.BlockSpec((None, 6, d), lambda i: (jnp.minimum(i, npt - 1) // tiles_per_seq, 0, 0)),
            pl.BlockSpec((tm, d), smap),
            pl.BlockSpec((1, d), lambda i: (0, 0)),
            pl.BlockSpec(memory_space=pl.ANY),
        ],
        out_specs=(pl.BlockSpec((tm, d), pmap), pl.BlockSpec((tm, d), smap)),
        scratch_shapes=[
            pltpu.VMEM((TOP_K, tm * SUBLANES, LANES), F32),
            pltpu.VMEM((TOP_K, tm * SUBLANES, LANES), F32),
            pltpu.VMEM((tm, SUBLANES), F32),
            pltpu.SemaphoreType.DMA((2,)),
        ],
        compiler_params=pltpu.CompilerParams(
            dimension_semantics=("arbitrary",), vmem_limit_bytes=VMEM_LIMIT),
        name="combine",
    )(dest, dest, gates, x1_p, x1_s, mod_p, g2_s, final_norm, y_rows)


def kernel(x_prompt, x_sample, state_pool, state_conv, c_prompt, c_sample, norm1, norm2, w_ada, b_ada,
           w_in, w_pool, pool_scale, w_conv, w_out, w_router, b_router, w_gate, b_gate, w_up, b_up,
           w_down, b_down, final_norm):
    depth = norm1.shape[0]
    assert depth == 1, "single-layer step"
    bp, tp, d = x_prompt.shape
    bs, ts, _ = x_sample.shape
    dp = state_pool.shape[-1]
    n_hist = state_pool.shape[2]
    n_chist = state_conv.shape[2]
    n_p, n_s = bp * tp, bs * ts
    assert d == SUBLANES * LANES, "token-tile layout assumes one vreg tile per token row"
    assert tp % TOKEN_TILE == 0 and n_s == TOKEN_TILE, "the sample group fills exactly one token tile"
    assert tp % COMBINE_TILE == 0 and n_s % COMBINE_TILE == 0
    assert ((n_p + n_s) * TOP_K) % INVERT_CHUNK == 0
    assert (n_p + n_s) % ROUTE_TILE == 0 and ((n_p + n_s) * TOP_K) % EXPERT_TILE == 0

    l = 0
    n1 = norm1[l].reshape(1, d)
    n2 = norm2[l].reshape(1, d)
    w_in_b = w_in[l].astype(BF16)
    w_pool_b = w_pool[l].astype(BF16)
    w_out_b = w_out[l].astype(BF16)
    pscale = pool_scale[l].reshape(1, dp)
    w_r = w_router[l].T.astype(BF16)
    b_r = b_router[l].reshape(N_EXPERTS, 1)

    mod = _adaln(jnp.concatenate([c_sample, c_prompt], axis=0), w_ada[l], b_ada[l])
    mod_s = mod[:bs]
    mod_p = mod[bs:].reshape(bp, 6, d)

    xs_tm = jnp.transpose(x_sample, (1, 0, 2)).reshape(n_s, d)
    ps_tm = jnp.transpose(state_pool[l], (1, 0, 2))
    cs_tm = jnp.transpose(state_conv[l], (1, 0, 2))
    x1_s, h2_s, lg_s, newp_tm, newc_tm = _mixer_sample(
        xs_tm, mod, ps_tm, cs_tm, n1, n2, w_in_b, w_pool_b, pscale, w_conv[l], w_out_b, w_r, b_r, ts)

    x1_p, h2, lg_p, u_tail, v_tail = _mixer_prompt(
        x_prompt, mod_p, n1, n2, w_in_b, w_pool_b, pscale, w_conv[l], w_out_b, w_r, b_r, h2_s)

    dest, gates, counts_f = _route(lg_p, lg_s)
    counts = counts_f[:, 0].astype(I32)
    n_rows = (n_p + n_s) * TOP_K
    gid, tid, nxt, offs, nact, n_steps = _group_metadata(counts, n_rows, EXPERT_TILE)

    dest = dest.reshape(-1)
    inv = _invert(dest)
    y_rows = _experts(gid, tid, nxt, offs, nact, n_steps, inv, h2,
                      w_gate[l], b_gate[l], w_up[l], b_up[l], w_down[l], b_down[l])

    g2_s = jnp.tile(mod_s[:, 5 * d:], (ts, 1))
    y_p, y_s = _combine(dest, gates, x1_p, x1_s, mod_p, g2_s, final_norm.reshape(1, d), y_rows, tp)

    y_prompt = y_p.reshape(bp, tp, d)
    y_sample = jnp.transpose(y_s.reshape(ts, bs, d), (1, 0, 2))
    new_pool_prompt = u_tail[:, POOL_HALO - n_hist:, :][None]
    new_conv_prompt = v_tail[:, CONV_HALO - n_chist:, :][None]
    new_pool_sample = jnp.transpose(newp_tm, (1, 0, 2))[None]
    new_conv_sample = jnp.transpose(newc_tm, (1, 0, 2))[None]
    return (y_prompt, y_sample, new_pool_prompt, new_conv_prompt, new_pool_sample, new_conv_sample)
```

```python
import functools

import jax
import jax.numpy as jnp
from jax import lax
from jax.experimental import pallas as pl
from jax.experimental.pallas import tpu as pltpu
from jax.experimental.pallas import tpu_sc as plsc

F32 = jnp.float32
BF16 = jnp.bfloat16
I32 = jnp.int32

POOL_WINDOWS = (2, 4, 8, 16)
POOL_HALO = 16
CONV_TAPS = 3
CONV_HALO = 8
N_EXPERTS = 32
TOP_K = 4
SWIGLU_LIMIT = 7.0
SWIGLU_ALPHA = 1.702
EPS = 1e-5
PAST_LEN = 16384

LANES = 128
SUBLANES = 8

TOKEN_TILE = 512
ROUTE_TILE = 512
COMBINE_TILE = 512
COMBINE_GROUP = 32
EXPERT_TILE = 512
EXPERT_SUBTILE = 128
VMEM_LIMIT = 56 * 1024 * 1024


def _rmsnorm(x, g):
    ms = jnp.mean(x * x, axis=-1, keepdims=True)
    return x * lax.rsqrt(ms + EPS) * g


def _dot(a, b):
    return jnp.dot(a, b, preferred_element_type=F32)


def _store_token_tiles(ref, val):
    rows = val.shape[0]
    for c in range(SUBLANES):
        ref[pl.ds(c, rows, stride=SUBLANES), :] = val[:, c * LANES:(c + 1) * LANES]


def _load_token_tiles(ref):
    rows = ref.shape[0] // SUBLANES
    return jnp.concatenate([ref[pl.ds(c, rows, stride=SUBLANES), :] for c in range(SUBLANES)], axis=-1)


ISSUE_GROUP = 4


def _slot_index(idx_ref, c, k, lane):
    return idx_ref[(c * TOP_K + k) * LANES + lane]


def _for_each_token_slot(idx_ref, start_copy):
    for c in range(idx_ref.shape[0] // (TOP_K * LANES)):
        def group(g, carry, c=c):
            l0 = g * ISSUE_GROUP
            idx = [[_slot_index(idx_ref, c, k, l0 + j) for k in range(TOP_K)] for j in range(ISSUE_GROUP)]
            for j in range(ISSUE_GROUP):
                for k in range(TOP_K):
                    start_copy(c * LANES + l0 + j, k, idx[j][k])
            return carry
        lax.fori_loop(0, LANES // ISSUE_GROUP, group, 0)


def _adaln_kernel(c_ref, w_ref, b_ref, o_ref):
    c = c_ref[...]
    s = c * jax.nn.sigmoid(c)
    o_ref[...] = _dot(s.astype(BF16), w_ref[...].astype(BF16)) + b_ref[...]


def _adaln(c, w_ada, b_ada):
    rows, d = c.shape
    n = w_ada.shape[1]
    tn = 1024
    return pl.pallas_call(
        _adaln_kernel,
        out_shape=jax.ShapeDtypeStruct((rows, n), F32),
        grid=(n // tn,),
        in_specs=[
            pl.BlockSpec((rows, d), lambda j: (0, 0)),
            pl.BlockSpec((d, tn), lambda j: (0, j)),
            pl.BlockSpec((1, tn), lambda j: (0, j)),
        ],
        out_specs=pl.BlockSpec((rows, tn), lambda j: (0, j)),
        compiler_params=pltpu.CompilerParams(
            dimension_semantics=("arbitrary",), vmem_limit_bytes=VMEM_LIMIT),
        name="adaln",
    )(c, w_ada, b_ada.reshape(1, n))


def _mix_tail(x, pool_in, conv_out, g1, sc2, sh2, n2, wpool_ref, pscale, wout, wr, br):
    gw = pool_in.shape[1] // len(POOL_WINDOWS)
    mixed = [_dot(pool_in[:, g * gw:(g + 1) * gw].astype(BF16), wpool_ref[g])
             for g in range(len(POOL_WINDOWS))]
    pool_out = jnp.concatenate(mixed, axis=-1) * pscale
    mix_in = jnp.concatenate([pool_out, conv_out], axis=-1).astype(BF16)
    x1 = x + g1 * _dot(mix_in, wout)
    h2 = _rmsnorm(x1, n2) * (1.0 + sc2) + sh2
    logits_t = lax.dot_general(wr, h2.astype(BF16), (((1,), (1,)), ((), ())),
                               preferred_element_type=F32) + br
    return x1, h2, logits_t


def _mixer_prompt_kernel(x_ref, mod_ref, n1_ref, n2_ref, win_ref, wpool_ref, pscale_ref, wconv_ref,
                         wout_ref, wr_ref, br_ref, h2s_ref,
                         x1_ref, h2_ref, lg_ref, upool_ref, vconv_ref, ubuf, vbuf, *, tiles_per_seq):
    i = pl.program_id(0)

    @pl.when(i < pl.num_programs(0) - 1)
    def _():
        _mixer_prompt_tile(x_ref, mod_ref, n1_ref, n2_ref, win_ref, wpool_ref, pscale_ref, wconv_ref,
                           wout_ref, wr_ref, br_ref, x1_ref, h2_ref, lg_ref, upool_ref, vconv_ref,
                           ubuf, vbuf, lax.rem(i, tiles_per_seq), tiles_per_seq)

    @pl.when(i == pl.num_programs(0) - 1)
    def _():
        h2_ref[...] = h2s_ref[...]


def _mixer_prompt_tile(x_ref, mod_ref, n1_ref, n2_ref, win_ref, wpool_ref, pscale_ref, wconv_ref,
                       wout_ref, wr_ref, br_ref, x1_ref, h2_ref, lg_ref, upool_ref, vconv_ref,
                       ubuf, vbuf, t, tiles_per_seq):
    tt = x_ref.shape[0]
    dp = ubuf.shape[1]
    gw = dp // len(POOL_WINDOWS)

    @pl.when(t == 0)
    def _():
        ubuf[0:POOL_HALO, :] = jnp.zeros((POOL_HALO, dp), F32)
        vbuf[0:CONV_HALO, :] = jnp.zeros((CONV_HALO, dp), F32)

    x = x_ref[...]
    mod = mod_ref[...]
    sh1, sc1, g1, sh2, sc2, _ = [mod[i:i + 1, :] for i in range(6)]
    h = _rmsnorm(x, n1_ref[...]) * (1.0 + sc1) + sh1
    z = _dot(h.astype(BF16), win_ref[...])
    u, gate_b, gate_c, val = [z[:, i * dp:(i + 1) * dp] for i in range(4)]

    ubuf[POOL_HALO:POOL_HALO + tt, :] = u
    pos = lax.broadcasted_iota(I32, (tt, gw), 0) + t * tt
    pooled = []
    for g, w in enumerate(POOL_WINDOWS):
        cols = slice(g * gw, (g + 1) * gw)
        acc = u[:, cols]
        for j in range(1, w):
            acc = acc + ubuf[POOL_HALO - j:POOL_HALO - j + tt, cols]
        cnt = jnp.minimum(pos + 1, w).astype(F32)
        pooled.append(acc / cnt - u[:, cols])
    pool_in = jnp.concatenate(pooled, axis=-1)

    v = gate_c * val
    vbuf[CONV_HALO:CONV_HALO + tt, :] = v
    wc = wconv_ref[...]
    y = (wc[0:1, :] * vbuf[CONV_HALO - 2:CONV_HALO - 2 + tt, :]
         + wc[1:2, :] * vbuf[CONV_HALO - 1:CONV_HALO - 1 + tt, :]
         + wc[2:3, :] * v)
    conv_out = gate_b * y

    x1, h2, logits = _mix_tail(x, pool_in, conv_out, g1, sc2, sh2, n2_ref[...], wpool_ref,
                               pscale_ref[...], wout_ref[...], wr_ref[...], br_ref[...])
    x1_ref[...] = x1
    _store_token_tiles(h2_ref, h2)
    lg_ref[...] = logits

    ubuf[0:POOL_HALO, :] = ubuf[tt:tt + POOL_HALO, :]
    vbuf[0:CONV_HALO, :] = vbuf[tt:tt + CONV_HALO, :]

    @pl.when(t == tiles_per_seq - 1)
    def _():
        upool_ref[...] = ubuf[0:POOL_HALO, :]
        vconv_ref[...] = vbuf[0:CONV_HALO, :]


def _mixer_prompt(x, mod_p, n1, n2, w_in, w_pool, pscale, w_conv, w_out, w_r, b_r, h2_s):
    b, t, d = x.shape
    dp = w_pool.shape[0] * w_pool.shape[1]
    tt = min(TOKEN_TILE, t)
    nt = t // tt
    n_real = b * nt
    assert h2_s.shape[0] == tt * SUBLANES
    seq = lambda i: jnp.minimum(i, n_real - 1) // nt
    tile = lambda i: jnp.minimum(i, n_real - 1)
    const2 = lambda i: (0, 0)
    const3 = lambda i: (0, 0, 0)
    return pl.pallas_call(
        functools.partial(_mixer_prompt_kernel, tiles_per_seq=nt),
        out_shape=(
            jax.ShapeDtypeStruct((b * t, d), F32),
            jax.ShapeDtypeStruct(((n_real + 1) * tt * SUBLANES, LANES), F32),
            jax.ShapeDtypeStruct((N_EXPERTS, b * t), F32),
            jax.ShapeDtypeStruct((b, POOL_HALO, dp), F32),
            jax.ShapeDtypeStruct((b, CONV_HALO, dp), F32),
        ),
        grid=(n_real + 1,),
        in_specs=[
            pl.BlockSpec((None, tt, d), lambda i: (seq(i), tile(i) % nt, 0)),
            pl.BlockSpec((None, 6, d), lambda i: (seq(i), 0, 0)),
            pl.BlockSpec((1, d), const2),
            pl.BlockSpec((1, d), const2),
            pl.BlockSpec(w_in.shape, const2),
            pl.BlockSpec(w_pool.shape, const3),
            pl.BlockSpec((1, dp), const2),
            pl.BlockSpec(w_conv.shape, const2),
            pl.BlockSpec(w_out.shape, const2),
            pl.BlockSpec(w_r.shape, const2),
            pl.BlockSpec((N_EXPERTS, 1), const2),
            pl.BlockSpec(h2_s.shape, const2),
        ],
        out_specs=(
            pl.BlockSpec((tt, d), lambda i: (tile(i), 0)),
            pl.BlockSpec((tt * SUBLANES, LANES), lambda i: (i, 0)),
            pl.BlockSpec((N_EXPERTS, tt), lambda i: (0, tile(i))),
            pl.BlockSpec((None, POOL_HALO, dp), lambda i: (seq(i), 0, 0)),
            pl.BlockSpec((None, CONV_HALO, dp), lambda i: (seq(i), 0, 0)),
        ),
        scratch_shapes=[
            pltpu.VMEM((POOL_HALO + tt, dp), F32),
            pltpu.VMEM((CONV_HALO + tt, dp), F32),
        ],
        compiler_params=pltpu.CompilerParams(
            dimension_semantics=("arbitrary",), vmem_limit_bytes=VMEM_LIMIT),
        name="mixer_prompt",
    )(x, mod_p, n1, n2, w_in, w_pool, pscale, w_conv, w_out, w_r, b_r, h2_s)


def _mixer_sample_kernel(x_ref, mod_ref, pstate_ref, cstate_ref, n1_ref, n2_ref, win_ref, wpool_ref,
                         pscale_ref, wconv_ref, wout_ref, wr_ref, br_ref,
                         x1_ref, h2_ref, lg_ref, newp_ref, newc_ref, *, steps):
    nb = pstate_ref.shape[1]
    d = x_ref.shape[1]
    dp = pstate_ref.shape[2]
    gw = dp // len(POOL_WINDOWS)
    n_hist = pstate_ref.shape[0]
    n_chist = cstate_ref.shape[0]

    x = x_ref[...]
    mod = mod_ref[0:nb, :]
    rep = lambda a: jnp.concatenate([a] * steps, axis=0)
    sh1, sc1, g1, sh2, sc2, _ = [rep(mod[:, i * d:(i + 1) * d]) for i in range(6)]
    h = _rmsnorm(x, n1_ref[...]) * (1.0 + sc1) + sh1
    z = _dot(h.astype(BF16), win_ref[...])
    u, gate_b, gate_c, val = [z[:, i * dp:(i + 1) * dp] for i in range(4)]

    ext = [pstate_ref[i] for i in range(n_hist)] + [u[s * nb:(s + 1) * nb, :] for s in range(steps)]
    pooled_steps = []
    for s in range(steps):
        groups = []
        for g, w in enumerate(POOL_WINDOWS):
            cols = slice(g * gw, (g + 1) * gw)
            acc = ext[n_hist + s][:, cols]
            for j in range(1, w):
                acc = acc + ext[n_hist + s - j][:, cols]
            cnt = float(min(PAST_LEN + s + 1, w))
            groups.append(acc / cnt - ext[n_hist + s][:, cols])
        pooled_steps.append(jnp.concatenate(groups, axis=-1))
    pool_in = jnp.concatenate(pooled_steps, axis=0)

    v = gate_c * val
    vext = [cstate_ref[i] for i in range(n_chist)] + [v[s * nb:(s + 1) * nb, :] for s in range(steps)]
    wc = wconv_ref[...]
    y = jnp.concatenate(
        [wc[0:1, :] * vext[s] + wc[1:2, :] * vext[s + 1] + wc[2:3, :] * vext[s + 2] for s in range(steps)],
        axis=0)
    conv_out = gate_b * y

    x1, h2, logits = _mix_tail(x, pool_in, conv_out, g1, sc2, sh2, n2_ref[...], wpool_ref,
                               pscale_ref[...], wout_ref[...], wr_ref[...], br_ref[...])
    x1_ref[...] = x1
    _store_token_tiles(h2_ref, h2)
    lg_ref[...] = logits
    for i in range(n_hist):
        newp_ref[i] = ext[steps + i]
    for i in range(n_chist):
        newc_ref[i] = vext[steps + i]


def _mixer_sample(x_tm, mod_s, pstate_tm, cstate_tm, n1, n2, w_in, w_pool, pscale, w_conv, w_out, w_r, b_r,
                  steps):
    rows, d = x_tm.shape
    return pl.pallas_call(
        functools.partial(_mixer_sample_kernel, steps=steps),
        out_shape=(
            jax.ShapeDtypeStruct((rows, d), F32),
            jax.ShapeDtypeStruct((rows * SUBLANES, LANES), F32),
            jax.ShapeDtypeStruct((N_EXPERTS, rows), F32),
            jax.ShapeDtypeStruct(pstate_tm.shape, F32),
            jax.ShapeDtypeStruct(cstate_tm.shape, F32),
        ),
        compiler_params=pltpu.CompilerParams(vmem_limit_bytes=VMEM_LIMIT),
        name="mixer_sample",
    )(x_tm, mod_s, pstate_tm, cstate_tm, n1, n2, w_in, w_pool, pscale, w_conv, w_out, w_r, b_r)


def _route_kernel(lgp_ref, lgs_ref, dest_ref, gate_ref, cnt_ref, counts, start, before):
    ne = lgp_ref.shape[0]
    tr = before.shape[0]
    reps = tr // LANES
    n_prompt_chunks = lgp_ref.shape[1] // tr
    n_sample_chunks = lgs_ref.shape[1] // tr
    eidx = lax.broadcasted_iota(I32, (ne, tr), 0)

    def top_k(ref, c):
        work = ref[:, pl.ds(pl.multiple_of(c * tr, tr), tr)]
        top_v, onehots = [], []
        for _ in range(TOP_K):
            m = jnp.max(work, axis=0, keepdims=True)
            idx = jnp.min(jnp.where(work == m, eidx, ne), axis=0, keepdims=True)
            sel = eidx == idx
            top_v.append(m)
            onehots.append(sel)
            work = jnp.where(sel, -jnp.inf, work)
        mask = jnp.where(onehots[0] | onehots[1] | onehots[2] | onehots[3], 1.0, 0.0)
        chunk_counts = jnp.broadcast_to(jnp.sum(mask, axis=1, keepdims=True), (ne, LANES))
        return top_v, onehots, mask, chunk_counts

    def count_chunk(ref):
        def body(c, carry):
            counts[...] = counts[...] + top_k(ref, c)[3]
            return carry
        return body

    counts[...] = jnp.zeros_like(counts)
    lax.fori_loop(0, n_prompt_chunks, count_chunk(lgp_ref), 0)
    lax.fori_loop(0, n_sample_chunks, count_chunk(lgs_ref), 0)

    total = counts[...]
    hi = jnp.floor(total * (1.0 / 256.0))
    lo = total - hi * 256.0
    r = lax.broadcasted_iota(I32, (ne, ne), 0)
    col = lax.broadcasted_iota(I32, (ne, ne), 1)
    lower = jnp.where(col < r, 1.0, 0.0).astype(BF16)
    start[...] = 256.0 * _dot(lower, hi.astype(BF16)) + _dot(lower, lo.astype(BF16))
    cnt_ref[...] = total
    counts[...] = jnp.zeros_like(counts)

    r = lax.broadcasted_iota(I32, (tr, tr), 0)
    col = lax.broadcasted_iota(I32, (tr, tr), 1)
    before[...] = jnp.where(r < col, 1.0, 0.0).astype(BF16)
    gate_ref[...] = jnp.zeros_like(gate_ref)

    def place_chunk(ref, first_chunk):
        def body(c, carry):
            top_v, onehots, mask, chunk_counts = top_k(ref, c)
            base = jnp.concatenate([counts[...] + start[...]] * reps, axis=1)
            rank = _dot(mask.astype(BF16), before[...]) + base
            counts[...] = counts[...] + chunk_counts
            es = [jnp.exp(v - top_v[0]) for v in top_v]
            denom = es[0] + es[1] + es[2] + es[3]
            chunk = first_chunk + c
            cols = pl.ds(pl.multiple_of(chunk * tr, tr), tr)
            for k in range(TOP_K):
                d = jnp.sum(jnp.where(onehots[k], rank, 0.0), axis=0, keepdims=True).astype(I32)
                for j in range(reps):
                    dest_ref[chunk * reps + j, k:k + 1, :] = d[:, j * LANES:(j + 1) * LANES]
                gate_ref[k:k + 1, cols] = es[k] / denom
            return carry
        return body

    lax.fori_loop(0, n_prompt_chunks, place_chunk(lgp_ref, 0), 0)
    lax.fori_loop(0, n_sample_chunks, place_chunk(lgs_ref, n_prompt_chunks), 0)


def _route(lgt_p, lgt_s):
    ne, n_p = lgt_p.shape
    n_s = lgt_s.shape[1]
    tr = ROUTE_TILE
    return pl.pallas_call(
        _route_kernel,
        out_shape=(
            jax.ShapeDtypeStruct(((n_p + n_s) // LANES, TOP_K, LANES), I32),
            jax.ShapeDtypeStruct((SUBLANES, n_p + n_s), F32),
            jax.ShapeDtypeStruct((ne, LANES), F32),
        ),
        scratch_shapes=[pltpu.VMEM((ne, LANES), F32), pltpu.VMEM((ne, LANES), F32),
                        pltpu.VMEM((tr, tr), BF16)],
        compiler_params=pltpu.CompilerParams(vmem_limit_bytes=VMEM_LIMIT),
        name="route",
    )(lgt_p, lgt_s)


def _group_metadata(counts, n_rows, tile):
    n_tiles = n_rows // tile
    n_steps = n_tiles + N_EXPERTS - 1
    ends = jnp.cumsum(counts)
    offs = jnp.concatenate([jnp.zeros((1,), I32), ends]).astype(I32)
    first_tile = offs[:-1] // tile
    last_tile = (ends - 1) // tile
    tiles_e = jnp.where(counts > 0, last_tile - first_tile + 1, 0)
    step_end = jnp.cumsum(tiles_e)
    step_start = step_end - tiles_e
    n_active = step_end[-1]
    s = jnp.minimum(jnp.arange(n_steps, dtype=I32), n_active - 1)
    owner = ((s[:, None] >= step_start[None, :]) & (s[:, None] < step_end[None, :])).astype(I32)
    gid = jnp.sum(owner * jnp.arange(N_EXPERTS, dtype=I32)[None, :], axis=1)
    tid = jnp.sum(owner * (first_tile - step_start)[None, :], axis=1) + s
    ids = jnp.arange(N_EXPERTS, dtype=I32)
    later = (ids[None, :] > ids[:, None]) & (counts[None, :] > 0)
    next_e = jnp.min(jnp.where(later, ids[None, :], N_EXPERTS), axis=1)
    next_e = jnp.where(next_e == N_EXPERTS, -1, next_e)
    nxt = jnp.sum(owner * next_e[None, :], axis=1)
    return gid, tid, nxt, offs, n_active.reshape(1).astype(I32), n_steps


SC_LANES = 16
SC_INDEX_BATCH = 128


def _invert(dest):
    r = dest.shape[0]
    mesh = plsc.VectorSubcoreMesh(core_axis_name="core", subcore_axis_name="subcore")
    per = r // mesh.num_subcores
    assert per * mesh.num_subcores == r and per % SC_INDEX_BATCH == 0

    @functools.partial(
        pl.kernel, mesh=mesh, out_type=jax.ShapeDtypeStruct((r,), I32),
        scratch_types=[pltpu.VMEM_SHARED((r,), I32), pltpu.VMEM((per,), I32), pltpu.VMEM((per,), I32)],
        compiler_params=pltpu.CompilerParams(needs_layout_passes=False),
        name="invert",
    )
    def invert(dest_hbm, inv_hbm, table, idx, ids):
        @pl.when(lax.axis_index("core") == 0)
        def _():
            base = lax.axis_index("subcore") * per
            pltpu.sync_copy(dest_hbm.at[pl.ds(base, per)], idx)
            lane = lax.iota(I32, SC_LANES)

            @pl.loop(0, per // SC_LANES)
            def _(i):
                p = base + i * SC_LANES + lane
                tok = lax.shift_right_logical(p, 9) * LANES + (p & (LANES - 1))
                slot = lax.shift_right_logical(p, 7) & (TOP_K - 1)
                ids[pl.ds(i * SC_LANES, SC_LANES)] = tok * TOP_K + slot

            @pl.loop(0, per // SC_INDEX_BATCH)
            def _(j):
                span = pl.ds(j * SC_INDEX_BATCH, SC_INDEX_BATCH)
                pltpu.sync_copy(ids.at[span], table.at[idx.at[span]])

            plsc.subcore_barrier()
            pltpu.sync_copy(table.at[pl.ds(base, per)], inv_hbm.at[pl.ds(base, per)])

    return invert(dest)


def _experts_kernel(gid_ref, tid_ref, nxt_ref, offs_ref, nact_ref,
                    inv0_ref, inv1_ref, inv2_ref, bg_ref, bu_ref, bd_ref, h2_hbm, wg_hbm, wu_hbm, wd_hbm, out_ref,
                    wg_b, wu_b, wd_b, wg_f, wu_f, wd_f, relay, rows, sems, row_sems):
    s = pl.program_id(0)
    n_steps = pl.num_programs(0)
    tm = out_ref.shape[0] // SUBLANES
    landing = ((wg_hbm, wg_f, wg_b), (wu_hbm, wu_f, wu_b), (wd_hbm, wd_f, wd_b))

    def fetch(e):
        for j, (hbm, land, _) in enumerate(landing):
            pltpu.make_async_copy(hbm.at[e], land, sems.at[j]).start()

    def start_row(idx_ref, into, r):
        tok = lax.shift_right_logical(idx_ref[r], TOP_K.bit_length() - 1)
        src = h2_hbm.at[pl.ds(pl.multiple_of(tok * SUBLANES, SUBLANES), SUBLANES)]
        dst = rows.at[into, pl.ds(pl.multiple_of(r * SUBLANES, SUBLANES), SUBLANES)]
        return pltpu.make_async_copy(src, dst, row_sems.at[into])

    def gather_in_line(idx_ref, into):
        for r in range(tm):
            start_row(idx_ref, into, r).start(priority=r % 2)

    def gather_loop(idx_ref, into):
        def group(g, carry):
            for j in range(ISSUE_GROUP):
                start_row(idx_ref, into, g * ISSUE_GROUP + j).start(priority=j % 2)
            return carry
        lax.fori_loop(0, tm // ISSUE_GROUP, group, 0)

    def wait_rows(into):
        pltpu.make_async_copy(h2_hbm.at[pl.ds(0, tm * SUBLANES)], rows.at[into], row_sems.at[into]).wait()

    @pl.when(s < nact_ref[0])
    def _():
        e = gid_ref[s]
        m = tid_ref[s]
        cur = lax.rem(m, 3)
        ahead = lax.rem(m + 2, 3)
        last = nact_ref[0] - 1
        new_tile = (s == 0) | (tid_ref[jnp.maximum(s - 1, 0)] != m)
        next_differs = (s < last) & (tid_ref[jnp.minimum(s + 1, n_steps - 1)] != m)

        @pl.when(s == 0)
        def _():
            fetch(e)
            gather_loop(inv0_ref, cur)
            gather_loop(inv1_ref, lax.rem(m + 1, 3))

        @pl.when((s == 0) | (gid_ref[jnp.maximum(s - 1, 0)] != e))
        def _():
            for j, (hbm, land, half) in enumerate(landing):
                pltpu.make_async_copy(hbm.at[e], land, sems.at[j]).wait()
                half[...] = land[...].astype(BF16)

            @pl.when(nxt_ref[s] >= 0)
            def _():
                fetch(nxt_ref[s])

        @pl.when(new_tile)
        def _():
            wait_rows(cur)

        def ffn(x):
            g = _dot(x, wg_b[...]) + bg_ref[pl.ds(e, 1), :]
            u = _dot(x, wu_b[...]) + bu_ref[pl.ds(e, 1), :]
            g = jnp.minimum(g, SWIGLU_LIMIT)
            u = jnp.clip(u, -SWIGLU_LIMIT, SWIGLU_LIMIT)
            glu = g * jax.nn.sigmoid(SWIGLU_ALPHA * g)
            return _dot(((u + 1.0) * glu).astype(BF16), wd_b[...]) + bd_ref[pl.ds(e, 1), :]

        lo = offs_ref[e]
        hi = offs_ref[e + 1]
        whole_tile = (lo <= m * tm) & (hi >= (m + 1) * tm)

        @pl.when(whole_tile)
        def _():
            x = _load_token_tiles(rows.at[cur]).astype(BF16)
            gather_in_line(inv2_ref, ahead)
            _store_token_tiles(out_ref, ffn(x))

        @pl.when(jnp.logical_not(whole_tile))
        def _():
            sub = relay.shape[0] // SUBLANES
            for j in range(tm // sub):
                first = m * tm + j * sub

                @pl.when((lo < first + sub) & (hi > first))
                def _():
                    span = pl.ds(j * sub * SUBLANES, sub * SUBLANES)
                    _store_token_tiles(relay, ffn(_load_token_tiles(rows.at[cur, span]).astype(BF16)))
                    row = first + lax.shift_right_logical(lax.broadcasted_iota(I32, relay.shape, 0), 3)
                    pltpu.store(out_ref.at[span], relay[...], mask=(row >= lo) & (row < hi))

            @pl.when(next_differs | (s == last))
            def _():
                gather_loop(inv2_ref, ahead)

        @pl.when(s == last)
        def _():
            wait_rows(lax.rem(m + 1, 3))
            wait_rows(ahead)


def _experts(gid, tid, nxt, offs, nact, n_steps, inv, h2, wg, bg, wu, bu, wd, bd):
    ne, d, f = wg.shape
    tm = EXPERT_TILE
    whole = lambda s, gid, tid, nxt, offs, nact: (0, 0)
    last_tile = inv.shape[0] // tm - 1

    def order_of(k):
        return pl.BlockSpec((tm,), lambda s, gid, tid, nxt, offs, nact: (jnp.minimum(tid[s] + k, last_tile),),
                            memory_space=pltpu.SMEM)

    grid_spec = pltpu.PrefetchScalarGridSpec(
        num_scalar_prefetch=5,
        grid=(n_steps,),
        in_specs=[
            order_of(0), order_of(1), order_of(2),
            pl.BlockSpec((ne, f), whole),
            pl.BlockSpec((ne, f), whole),
            pl.BlockSpec((ne, d), whole),
            pl.BlockSpec(memory_space=pl.ANY),
            pl.BlockSpec(memory_space=pl.ANY),
            pl.BlockSpec(memory_space=pl.ANY),
            pl.BlockSpec(memory_space=pl.ANY),
        ],
        out_specs=pl.BlockSpec((tm * SUBLANES, LANES), lambda s, gid, tid, nxt, offs, nact: (tid[s], 0)),
        scratch_shapes=[
            pltpu.VMEM((d, f), BF16), pltpu.VMEM((d, f), BF16), pltpu.VMEM((f, d), BF16),
            pltpu.VMEM((d, f), F32), pltpu.VMEM((d, f), F32), pltpu.VMEM((f, d), F32),
            pltpu.VMEM((EXPERT_SUBTILE * SUBLANES, LANES), F32),
            pltpu.VMEM((3, tm * SUBLANES, LANES), F32),
            pltpu.SemaphoreType.DMA((3,)),
            pltpu.SemaphoreType.DMA((3,)),
        ],
    )
    return pl.pallas_call(
        _experts_kernel,
        out_shape=jax.ShapeDtypeStruct((inv.shape[0] * SUBLANES, LANES), F32),
        grid_spec=grid_spec,
        compiler_params=pltpu.CompilerParams(
            dimension_semantics=("arbitrary",), vmem_limit_bytes=VMEM_LIMIT),
        name="experts",
    )(gid, tid, nxt, offs, nact, inv, inv, inv, bg, bu, bd, h2, wg, wu, wd)


def _combine_kernel(dest_ref, dnext_ref, gate_ref, x1p_ref, x1s_ref, modp_ref, g2s_ref, fn_ref, y_hbm,
                    outp_ref, outs_ref, gbuf_a, gbuf_b, gates_t, sems, *, n_prompt_tiles):
    i = pl.program_id(0)
    n = pl.num_programs(0)
    tm = x1p_ref.shape[0]
    group = COMBINE_GROUP

    def start_copy(buf, sem, t, k, d):
        src = y_hbm.at[pl.ds(pl.multiple_of(d * SUBLANES, SUBLANES), SUBLANES)]
        dst = buf.at[k, pl.ds(pl.multiple_of(t * SUBLANES, SUBLANES), SUBLANES)]
        pltpu.make_async_copy(src, dst, sem).start(priority=k % 2)

    def wait_tile(buf, sem):
        for k in range(TOP_K):
            pltpu.make_async_copy(y_hbm.at[pl.ds(0, tm * SUBLANES)], buf.at[k], sem).wait()

    def step(cur, cur_sem, nxt, nxt_sem):
        @pl.when(i == 0)
        def _():
            _for_each_token_slot(dest_ref, functools.partial(start_copy, cur, cur_sem))

        wait_tile(cur, cur_sem)
        gates_t[...] = gate_ref[...].T
        fn = fn_ref[...]

        def run(x1_ref, out_ref, g2_rows):
            def body(g, carry, c):
                l0 = g * group
                base = pl.multiple_of(c * LANES + l0, group)
                idx = [[_slot_index(dnext_ref, c, k, l0 + j) for k in range(TOP_K)] for j in range(group)]
                for j in range(group):
                    for k in range(TOP_K):
                        start_copy(nxt, nxt_sem, base + j, k, idx[j][k])
                gt = gates_t[pl.ds(base, group), :]
                tile0 = pl.multiple_of(base * SUBLANES, group * SUBLANES)
                cols = []
                for c in range(SUBLANES):
                    acc = gt[:, 0:1] * cur[0, pl.ds(tile0 + c, group, stride=SUBLANES), :]
                    for k in range(1, TOP_K):
                        acc = acc + gt[:, k:k + 1] * cur[k, pl.ds(tile0 + c, group, stride=SUBLANES), :]
                    cols.append(acc)
                ffn = jnp.concatenate(cols, axis=-1)
                rows = pl.ds(base, group)
                out_ref[rows, :] = _rmsnorm(x1_ref[rows, :] + g2_rows(rows) * ffn, fn)
                return carry
            for c in range(tm // LANES):
                lax.fori_loop(0, LANES // group, functools.partial(body, c=c), 0)

        @pl.when(i < n_prompt_tiles)
        def _():
            g2 = modp_ref[...][5:6, :]
            run(x1p_ref, outp_ref, lambda rows: g2)

        @pl.when(i >= n_prompt_tiles)
        def _():
            run(x1s_ref, outs_ref, lambda rows: g2s_ref[rows, :])

        @pl.when(i == n - 1)
        def _():
            wait_tile(nxt, nxt_sem)

    @pl.when(lax.rem(i, 2) == 0)
    def _():
        step(gbuf_a, sems.at[0], gbuf_b, sems.at[1])

    @pl.when(lax.rem(i, 2) == 1)
    def _():
        step(gbuf_b, sems.at[1], gbuf_a, sems.at[0])


def _combine(dest, gates, x1_p, x1_s, mod_p, g2_s, final_norm, y_rows, tokens_per_seq):
    n_p, d = x1_p.shape
    n_s = x1_s.shape[0]
    tm = COMBINE_TILE
    npt, nst = n_p // tm, n_s // tm
    tiles_per_seq = tokens_per_seq // tm
    pmap = lambda i: (jnp.minimum(i, npt - 1), 0)
    smap = lambda i: (jnp.maximum(i - npt, 0), 0)
    return pl.pallas_call(
        functools.partial(_combine_kernel, n_prompt_tiles=npt),
        out_shape=(jax.ShapeDtypeStruct((n_p, d), F32), jax.ShapeDtypeStruct((n_s, d), F32)),
        grid=(npt + nst,),
        in_specs=[
            pl.BlockSpec((tm * TOP_K,), lambda i: (i,), memory_space=pltpu.SMEM),
            pl.BlockSpec((tm * TOP_K,), lambda i: (jnp.minimum(i + 1, npt + nst - 1),),
                         memory_space=pltpu.SMEM),
            pl.BlockSpec((SUBLANES, tm), lambda i: (0, i)),
            pl.BlockSpec((tm, d), pmap),
            pl.BlockSpec((tm, d), smap),
            pl.BlockSpec((None, 6, d), lambda i: (jnp.minimum(i, npt - 1) // tiles_per_seq, 0, 0)),
            pl.BlockSpec((tm, d), smap),
            pl.BlockSpec((1, d), lambda i: (0, 0)),
            pl.BlockSpec(memory_space=pl.ANY),
        ],
        out_specs=(pl.BlockSpec((tm, d), pmap), pl.BlockSpec((tm, d), smap)),
        scratch_shapes=[
            pltpu.VMEM((TOP_K, tm * SUBLANES, LANES), F32),
            pltpu.VMEM((TOP_K, tm * SUBLANES, LANES), F32),
            pltpu.VMEM((tm, SUBLANES), F32),
            pltpu.SemaphoreType.DMA((2,)),
        ],
        compiler_params=pltpu.CompilerParams(
            dimension_semantics=("arbitrary",), vmem_limit_bytes=VMEM_LIMIT),
        name="combine",
    )(dest, dest, gates, x1_p, x1_s, mod_p, g2_s, final_norm, y_rows)


def kernel(x_prompt, x_sample, state_pool, state_conv, c_prompt, c_sample, norm1, norm2, w_ada, b_ada,
           w_in, w_pool, pool_scale, w_conv, w_out, w_router, b_router, w_gate, b_gate, w_up, b_up,
           w_down, b_down, final_norm):
    depth = norm1.shape[0]
    assert depth == 1, "single-layer step"
    bp, tp, d = x_prompt.shape
    bs, ts, _ = x_sample.shape
    dp = state_pool.shape[-1]
    n_hist = state_pool.shape[2]
    n_chist = state_conv.shape[2]
    n_p, n_s = bp * tp, bs * ts
    assert d == SUBLANES * LANES, "token-tile layout assumes one vreg tile per token row"
    assert tp % TOKEN_TILE == 0 and n_s == TOKEN_TILE, "the sample group fills exactly one token tile"
    assert tp % COMBINE_TILE == 0 and n_s % COMBINE_TILE == 0
    assert (n_p + n_s) % ROUTE_TILE == 0 and ((n_p + n_s) * TOP_K) % EXPERT_TILE == 0

    l = 0
    n1 = norm1[l].reshape(1, d)
    n2 = norm2[l].reshape(1, d)
    w_in_b = w_in[l].astype(BF16)
    w_pool_b = w_pool[l].astype(BF16)
    w_out_b = w_out[l].astype(BF16)
    pscale = pool_scale[l].reshape(1, dp)
    w_r = w_router[l].T.astype(BF16)
    b_r = b_router[l].reshape(N_EXPERTS, 1)

    mod = _adaln(jnp.concatenate([c_sample, c_prompt], axis=0), w_ada[l], b_ada[l])
    mod_s = mod[:bs]
    mod_p = mod[bs:].reshape(bp, 6, d)

    xs_tm = jnp.transpose(x_sample, (1, 0, 2)).reshape(n_s, d)
    ps_tm = jnp.transpose(state_pool[l], (1, 0, 2))
    cs_tm = jnp.transpose(state_conv[l], (1, 0, 2))
    x1_s, h2_s, lg_s, newp_tm, newc_tm = _mixer_sample(
        xs_tm, mod, ps_tm, cs_tm, n1, n2, w_in_b, w_pool_b, pscale, w_conv[l], w_out_b, w_r, b_r, ts)

    x1_p, h2, lg_p, u_tail, v_tail = _mixer_prompt(
        x_prompt, mod_p, n1, n2, w_in_b, w_pool_b, pscale, w_conv[l], w_out_b, w_r, b_r, h2_s)

    dest, gates, counts_f = _route(lg_p, lg_s)
    counts = counts_f[:, 0].astype(I32)
    n_rows = (n_p + n_s) * TOP_K
    gid, tid, nxt, offs, nact, n_steps = _group_metadata(counts, n_rows, EXPERT_TILE)

    dest = dest.reshape(-1)
    inv = _invert(dest)
    y_rows = _experts(gid, tid, nxt, offs, nact, n_steps, inv, h2,
                      w_gate[l], b_gate[l], w_up[l], b_up[l], w_down[l], b_down[l])

    g2_s = jnp.tile(mod_s[:, 5 * d:], (ts, 1))
    y_p, y_s = _combine(dest, gates, x1_p, x1_s, mod_p, g2_s, final_norm.reshape(1, d), y_rows, tp)

    y_prompt = y_p.reshape(bp, tp, d)
    y_sample = jnp.transpose(y_s.reshape(ts, bs, d), (1, 0, 2))
    new_pool_prompt = u_tail[:, POOL_HALO - n_hist:, :][None]
    new_conv_prompt = v_tail[:, CONV_HALO - n_chist:, :][None]
    new_pool_sample = jnp.transpose(newp_tm, (1, 0, 2))[None]
    new_conv_sample = jnp.transpose(newc_tm, (1, 0, 2))[None]
    return (y_prompt, y_sample, new_pool_prompt, new_conv_prompt, new_pool_sample, new_conv_sample)
```

```python
import functools

import jax
import jax.numpy as jnp
from jax import lax
from jax.experimental import pallas as pl
from jax.experimental.pallas import tpu as pltpu
from jax.experimental.pallas import tpu_sc as plsc

F32 = jnp.float32
BF16 = jnp.bfloat16
I32 = jnp.int32

POOL_WINDOWS = (2, 4, 8, 16)
POOL_HALO = 16
CONV_TAPS = 3
CONV_HALO = 8
N_EXPERTS = 32
TOP_K = 4
SWIGLU_LIMIT = 7.0
SWIGLU_ALPHA = 1.702
EPS = 1e-5
PAST_LEN = 16384

LANES = 128
SUBLANES = 8

TOKEN_TILE = 512
ROUTE_TILE = 512
COMBINE_TILE = 512
EXPERT_TILE = 512
EXPERT_SUBTILE = 128
VMEM_LIMIT = 56 * 1024 * 1024


def _rmsnorm(x, g):
    ms = jnp.mean(x * x, axis=-1, keepdims=True)
    return x * lax.rsqrt(ms + EPS) * g


def _dot(a, b):
    return jnp.dot(a, b, preferred_element_type=F32)


def _store_token_tiles(ref, val):
    rows = val.shape[0]
    for c in range(SUBLANES):
        ref[pl.ds(c, rows, stride=SUBLANES), :] = val[:, c * LANES:(c + 1) * LANES]


def _load_token_tiles(ref):
    rows = ref.shape[0] // SUBLANES
    return jnp.concatenate([ref[pl.ds(c, rows, stride=SUBLANES), :] for c in range(SUBLANES)], axis=-1)


ISSUE_GROUP = 4


def _adaln_kernel(c_ref, w_ref, b_ref, o_ref):
    c = c_ref[...]
    s = c * jax.nn.sigmoid(c)
    o_ref[...] = _dot(s.astype(BF16), w_ref[...].astype(BF16)) + b_ref[...]


def _adaln(c, w_ada, b_ada):
    rows, d = c.shape
    n = w_ada.shape[1]
    tn = 1024
    return pl.pallas_call(
        _adaln_kernel,
        out_shape=jax.ShapeDtypeStruct((rows, n), F32),
        grid=(n // tn,),
        in_specs=[
            pl.BlockSpec((rows, d), lambda j: (0, 0)),
            pl.BlockSpec((d, tn), lambda j: (0, j)),
            pl.BlockSpec((1, tn), lambda j: (0, j)),
        ],
        out_specs=pl.BlockSpec((rows, tn), lambda j: (0, j)),
        compiler_params=pltpu.CompilerParams(
            dimension_semantics=("arbitrary",), vmem_limit_bytes=VMEM_LIMIT),
        name="adaln",
    )(c, w_ada, b_ada.reshape(1, n))


def _mix_tail(x, pool_in, conv_out, g1, sc2, sh2, n2, wpool_ref, pscale, wout, wr, br):
    gw = pool_in.shape[1] // len(POOL_WINDOWS)
    mixed = [_dot(pool_in[:, g * gw:(g + 1) * gw].astype(BF16), wpool_ref[g])
             for g in range(len(POOL_WINDOWS))]
    pool_out = jnp.concatenate(mixed, axis=-1) * pscale
    mix_in = jnp.concatenate([pool_out, conv_out], axis=-1).astype(BF16)
    x1 = x + g1 * _dot(mix_in, wout)
    h2 = _rmsnorm(x1, n2) * (1.0 + sc2) + sh2
    logits_t = lax.dot_general(wr, h2.astype(BF16), (((1,), (1,)), ((), ())),
                               preferred_element_type=F32) + br
    return x1, h2, logits_t


def _mixer_prompt_kernel(x_ref, mod_ref, n1_ref, n2_ref, win_ref, wpool_ref, pscale_ref, wconv_ref,
                         wout_ref, wr_ref, br_ref, h2s_ref,
                         x1_ref, h2_ref, lg_ref, upool_ref, vconv_ref, ubuf, vbuf, *, tiles_per_seq):
    i = pl.program_id(0)

    @pl.when(i < pl.num_programs(0) - 1)
    def _():
        _mixer_prompt_tile(x_ref, mod_ref, n1_ref, n2_ref, win_ref, wpool_ref, pscale_ref, wconv_ref,
                           wout_ref, wr_ref, br_ref, x1_ref, h2_ref, lg_ref, upool_ref, vconv_ref,
                           ubuf, vbuf, lax.rem(i, tiles_per_seq), tiles_per_seq)

    @pl.when(i == pl.num_programs(0) - 1)
    def _():
        h2_ref[...] = h2s_ref[...]


def _mixer_prompt_tile(x_ref, mod_ref, n1_ref, n2_ref, win_ref, wpool_ref, pscale_ref, wconv_ref,
                       wout_ref, wr_ref, br_ref, x1_ref, h2_ref, lg_ref, upool_ref, vconv_ref,
                       ubuf, vbuf, t, tiles_per_seq):
    tt = x_ref.shape[0]
    dp = ubuf.shape[1]
    gw = dp // len(POOL_WINDOWS)

    @pl.when(t == 0)
    def _():
        ubuf[0:POOL_HALO, :] = jnp.zeros((POOL_HALO, dp), F32)
        vbuf[0:CONV_HALO, :] = jnp.zeros((CONV_HALO, dp), F32)

    x = x_ref[...]
    mod = mod_ref[...]
    sh1, sc1, g1, sh2, sc2, _ = [mod[i:i + 1, :] for i in range(6)]
    h = _rmsnorm(x, n1_ref[...]) * (1.0 + sc1) + sh1
    z = _dot(h.astype(BF16), win_ref[...])
    u, gate_b, gate_c, val = [z[:, i * dp:(i + 1) * dp] for i in range(4)]

    ubuf[POOL_HALO:POOL_HALO + tt, :] = u
    pos = lax.broadcasted_iota(I32, (tt, gw), 0) + t * tt
    pooled = []
    for g, w in enumerate(POOL_WINDOWS):
        cols = slice(g * gw, (g + 1) * gw)
        acc = u[:, cols]
        for j in range(1, w):
            acc = acc + ubuf[POOL_HALO - j:POOL_HALO - j + tt, cols]
        cnt = jnp.minimum(pos + 1, w).astype(F32)
        pooled.append(acc / cnt - u[:, cols])
    pool_in = jnp.concatenate(pooled, axis=-1)

    v = gate_c * val
    vbuf[CONV_HALO:CONV_HALO + tt, :] = v
    wc = wconv_ref[...]
    y = (wc[0:1, :] * vbuf[CONV_HALO - 2:CONV_HALO - 2 + tt, :]
         + wc[1:2, :] * vbuf[CONV_HALO - 1:CONV_HALO - 1 + tt, :]
         + wc[2:3, :] * v)
    conv_out = gate_b * y

    x1, h2, logits = _mix_tail(x, pool_in, conv_out, g1, sc2, sh2, n2_ref[...], wpool_ref,
                               pscale_ref[...], wout_ref[...], wr_ref[...], br_ref[...])
    x1_ref[...] = x1
    _store_token_tiles(h2_ref, h2)
    lg_ref[...] = logits

    ubuf[0:POOL_HALO, :] = ubuf[tt:tt + POOL_HALO, :]
    vbuf[0:CONV_HALO, :] = vbuf[tt:tt + CONV_HALO, :]

    @pl.when(t == tiles_per_seq - 1)
    def _():
        upool_ref[...] = ubuf[0:POOL_HALO, :]
        vconv_ref[...] = vbuf[0:CONV_HALO, :]


def _mixer_prompt(x, mod_p, n1, n2, w_in, w_pool, pscale, w_conv, w_out, w_r, b_r, h2_s):
    b, t, d = x.shape
    dp = w_pool.shape[0] * w_pool.shape[1]
    tt = min(TOKEN_TILE, t)
    nt = t // tt
    n_real = b * nt
    assert h2_s.shape[0] == tt * SUBLANES
    seq = lambda i: jnp.minimum(i, n_real - 1) // nt
    tile = lambda i: jnp.minimum(i, n_real - 1)
    const2 = lambda i: (0, 0)
    const3 = lambda i: (0, 0, 0)
    return pl.pallas_call(
        functools.partial(_mixer_prompt_kernel, tiles_per_seq=nt),
        out_shape=(
            jax.ShapeDtypeStruct((b * t, d), F32),
            jax.ShapeDtypeStruct(((n_real + 1) * tt * SUBLANES, LANES), F32),
            jax.ShapeDtypeStruct((N_EXPERTS, b * t), F32),
            jax.ShapeDtypeStruct((b, POOL_HALO, dp), F32),
            jax.ShapeDtypeStruct((b, CONV_HALO, dp), F32),
        ),
        grid=(n_real + 1,),
        in_specs=[
            pl.BlockSpec((None, tt, d), lambda i: (seq(i), tile(i) % nt, 0)),
            pl.BlockSpec((None, 6, d), lambda i: (seq(i), 0, 0)),
            pl.BlockSpec((1, d), const2),
            pl.BlockSpec((1, d), const2),
            pl.BlockSpec(w_in.shape, const2),
            pl.BlockSpec(w_pool.shape, const3),
            pl.BlockSpec((1, dp), const2),
            pl.BlockSpec(w_conv.shape, const2),
            pl.BlockSpec(w_out.shape, const2),
            pl.BlockSpec(w_r.shape, const2),
            pl.BlockSpec((N_EXPERTS, 1), const2),
            pl.BlockSpec(h2_s.shape, const2),
        ],
        out_specs=(
            pl.BlockSpec((tt, d), lambda i: (tile(i), 0)),
            pl.BlockSpec((tt * SUBLANES, LANES), lambda i: (i, 0)),
            pl.BlockSpec((N_EXPERTS, tt), lambda i: (0, tile(i))),
            pl.BlockSpec((None, POOL_HALO, dp), lambda i: (seq(i), 0, 0)),
            pl.BlockSpec((None, CONV_HALO, dp), lambda i: (seq(i), 0, 0)),
        ),
        scratch_shapes=[
            pltpu.VMEM((POOL_HALO + tt, dp), F32),
            pltpu.VMEM((CONV_HALO + tt, dp), F32),
        ],
        compiler_params=pltpu.CompilerParams(
            dimension_semantics=("arbitrary",), vmem_limit_bytes=VMEM_LIMIT),
        name="mixer_prompt",
    )(x, mod_p, n1, n2, w_in, w_pool, pscale, w_conv, w_out, w_r, b_r, h2_s)


def _mixer_sample_kernel(x_ref, mod_ref, pstate_ref, cstate_ref, n1_ref, n2_ref, win_ref, wpool_ref,
                         pscale_ref, wconv_ref, wout_ref, wr_ref, br_ref,
                         x1_ref, h2_ref, lg_ref, newp_ref, newc_ref, *, steps):
    nb = pstate_ref.shape[1]
    d = x_ref.shape[1]
    dp = pstate_ref.shape[2]
    gw = dp // len(POOL_WINDOWS)
    n_hist = pstate_ref.shape[0]
    n_chist = cstate_ref.shape[0]

    x = x_ref[...]
    mod = mod_ref[0:nb, :]
    rep = lambda a: jnp.concatenate([a] * steps, axis=0)
    sh1, sc1, g1, sh2, sc2, _ = [rep(mod[:, i * d:(i + 1) * d]) for i in range(6)]
    h = _rmsnorm(x, n1_ref[...]) * (1.0 + sc1) + sh1
    z = _dot(h.astype(BF16), win_ref[...])
    u, gate_b, gate_c, val = [z[:, i * dp:(i + 1) * dp] for i in range(4)]

    ext = [pstate_ref[i] for i in range(n_hist)] + [u[s * nb:(s + 1) * nb, :] for s in range(steps)]
    pooled_steps = []
    for s in range(steps):
        groups = []
        for g, w in enumerate(POOL_WINDOWS):
            cols = slice(g * gw, (g + 1) * gw)
            acc = ext[n_hist + s][:, cols]
            for j in range(1, w):
                acc = acc + ext[n_hist + s - j][:, cols]
            cnt = float(min(PAST_LEN + s + 1, w))
            groups.append(acc / cnt - ext[n_hist + s][:, cols])
        pooled_steps.append(jnp.concatenate(groups, axis=-1))
    pool_in = jnp.concatenate(pooled_steps, axis=0)

    v = gate_c * val
    vext = [cstate_ref[i] for i in range(n_chist)] + [v[s * nb:(s + 1) * nb, :] for s in range(steps)]
    wc = wconv_ref[...]
    y = jnp.concatenate(
        [wc[0:1, :] * vext[s] + wc[1:2, :] * vext[s + 1] + wc[2:3, :] * vext[s + 2] for s in range(steps)],
        axis=0)
    conv_out = gate_b * y

    x1, h2, logits = _mix_tail(x, pool_in, conv_out, g1, sc2, sh2, n2_ref[...], wpool_ref,
                               pscale_ref[...], wout_ref[...], wr_ref[...], br_ref[...])
    x1_ref[...] = x1
    _store_token_tiles(h2_ref, h2)
    lg_ref[...] = logits
    for i in range(n_hist):
        newp_ref[i] = ext[steps + i]
    for i in range(n_chist):
        newc_ref[i] = vext[steps + i]


def _mixer_sample(x_tm, mod_s, pstate_tm, cstate_tm, n1, n2, w_in, w_pool, pscale, w_conv, w_out, w_r, b_r,
                  steps):
    rows, d = x_tm.shape
    return pl.pallas_call(
        functools.partial(_mixer_sample_kernel, steps=steps),
        out_shape=(
            jax.ShapeDtypeStruct((rows, d), F32),
            jax.ShapeDtypeStruct((rows * SUBLANES, LANES), F32),
            jax.ShapeDtypeStruct((N_EXPERTS, rows), F32),
            jax.ShapeDtypeStruct(pstate_tm.shape, F32),
            jax.ShapeDtypeStruct(cstate_tm.shape, F32),
        ),
        compiler_params=pltpu.CompilerParams(vmem_limit_bytes=VMEM_LIMIT),
        name="mixer_sample",
    )(x_tm, mod_s, pstate_tm, cstate_tm, n1, n2, w_in, w_pool, pscale, w_conv, w_out, w_r, b_r)


def _route_kernel(lgp_ref, lgs_ref, dest_ref, gate_ref, cnt_ref, counts, start, before):
    ne = lgp_ref.shape[0]
    tr = before.shape[0]
    reps = tr // LANES
    n_prompt_chunks = lgp_ref.shape[1] // tr
    n_sample_chunks = lgs_ref.shape[1] // tr
    eidx = lax.broadcasted_iota(I32, (ne, tr), 0)

    def top_k(ref, c):
        work = ref[:, pl.ds(pl.multiple_of(c * tr, tr), tr)]
        top_v, onehots = [], []
        for _ in range(TOP_K):
            m = jnp.max(work, axis=0, keepdims=True)
            idx = jnp.min(jnp.where(work == m, eidx, ne), axis=0, keepdims=True)
            sel = eidx == idx
            top_v.append(m)
            onehots.append(sel)
            work = jnp.where(sel, -jnp.inf, work)
        mask = jnp.where(onehots[0] | onehots[1] | onehots[2] | onehots[3], 1.0, 0.0)
        chunk_counts = jnp.broadcast_to(jnp.sum(mask, axis=1, keepdims=True), (ne, LANES))
        return top_v, onehots, mask, chunk_counts

    def count_chunk(ref):
        def body(c, carry):
            counts[...] = counts[...] + top_k(ref, c)[3]
            return carry
        return body

    counts[...] = jnp.zeros_like(counts)
    lax.fori_loop(0, n_prompt_chunks, count_chunk(lgp_ref), 0)
    lax.fori_loop(0, n_sample_chunks, count_chunk(lgs_ref), 0)

    total = counts[...]
    hi = jnp.floor(total * (1.0 / 256.0))
    lo = total - hi * 256.0
    r = lax.broadcasted_iota(I32, (ne, ne), 0)
    col = lax.broadcasted_iota(I32, (ne, ne), 1)
    lower = jnp.where(col < r, 1.0, 0.0).astype(BF16)
    start[...] = 256.0 * _dot(lower, hi.astype(BF16)) + _dot(lower, lo.astype(BF16))
    cnt_ref[...] = total
    counts[...] = jnp.zeros_like(counts)

    r = lax.broadcasted_iota(I32, (tr, tr), 0)
    col = lax.broadcasted_iota(I32, (tr, tr), 1)
    before[...] = jnp.where(r < col, 1.0, 0.0).astype(BF16)
    gate_ref[...] = jnp.zeros_like(gate_ref)

    def place_chunk(ref, first_chunk):
        def body(c, carry):
            top_v, onehots, mask, chunk_counts = top_k(ref, c)
            base = jnp.concatenate([counts[...] + start[...]] * reps, axis=1)
            rank = _dot(mask.astype(BF16), before[...]) + base
            counts[...] = counts[...] + chunk_counts
            es = [jnp.exp(v - top_v[0]) for v in top_v]
            denom = es[0] + es[1] + es[2] + es[3]
            chunk = first_chunk + c
            cols = pl.ds(pl.multiple_of(chunk * tr, tr), tr)
            for k in range(TOP_K):
                d = jnp.sum(jnp.where(onehots[k], rank, 0.0), axis=0, keepdims=True).astype(I32)
                for j in range(reps):
                    dest_ref[chunk * reps + j, k:k + 1, :] = d[:, j * LANES:(j + 1) * LANES]
                gate_ref[k:k + 1, cols] = es[k] / denom
            return carry
        return body

    lax.fori_loop(0, n_prompt_chunks, place_chunk(lgp_ref, 0), 0)
    lax.fori_loop(0, n_sample_chunks, place_chunk(lgs_ref, n_prompt_chunks), 0)


def _route(lgt_p, lgt_s):
    ne, n_p = lgt_p.shape
    n_s = lgt_s.shape[1]
    tr = ROUTE_TILE
    return pl.pallas_call(
        _route_kernel,
        out_shape=(
            jax.ShapeDtypeStruct(((n_p + n_s) // LANES, TOP_K, LANES), I32),
            jax.ShapeDtypeStruct((SUBLANES, n_p + n_s), F32),
            jax.ShapeDtypeStruct((ne, LANES), F32),
        ),
        scratch_shapes=[pltpu.VMEM((ne, LANES), F32), pltpu.VMEM((ne, LANES), F32),
                        pltpu.VMEM((tr, tr), BF16)],
        compiler_params=pltpu.CompilerParams(vmem_limit_bytes=VMEM_LIMIT),
        name="route",
    )(lgt_p, lgt_s)


def _group_metadata(counts, n_rows, tile):
    n_tiles = n_rows // tile
    n_steps = n_tiles + N_EXPERTS - 1
    ends = jnp.cumsum(counts)
    offs = jnp.concatenate([jnp.zeros((1,), I32), ends]).astype(I32)
    first_tile = offs[:-1] // tile
    last_tile = (ends - 1) // tile
    tiles_e = jnp.where(counts > 0, last_tile - first_tile + 1, 0)
    step_end = jnp.cumsum(tiles_e)
    step_start = step_end - tiles_e
    n_active = step_end[-1]
    s = jnp.minimum(jnp.arange(n_steps, dtype=I32), n_active - 1)
    owner = ((s[:, None] >= step_start[None, :]) & (s[:, None] < step_end[None, :])).astype(I32)
    gid = jnp.sum(owner * jnp.arange(N_EXPERTS, dtype=I32)[None, :], axis=1)
    tid = jnp.sum(owner * (first_tile - step_start)[None, :], axis=1) + s
    ids = jnp.arange(N_EXPERTS, dtype=I32)
    later = (ids[None, :] > ids[:, None]) & (counts[None, :] > 0)
    next_e = jnp.min(jnp.where(later, ids[None, :], N_EXPERTS), axis=1)
    next_e = jnp.where(next_e == N_EXPERTS, -1, next_e)
    nxt = jnp.sum(owner * next_e[None, :], axis=1)
    return gid, tid, nxt, offs, n_active.reshape(1).astype(I32), n_steps


SC_LANES = 16
SC_INDEX_BATCH = 128


def _invert(dest):
    r = dest.shape[0]
    mesh = plsc.VectorSubcoreMesh(core_axis_name="core", subcore_axis_name="subcore")
    per = r // mesh.num_subcores
    assert per * mesh.num_subcores == r and per % SC_INDEX_BATCH == 0

    @functools.partial(
        pl.kernel, mesh=mesh, out_type=jax.ShapeDtypeStruct((r,), I32),
        scratch_types=[pltpu.VMEM_SHARED((r,), I32), pltpu.VMEM((per,), I32), pltpu.VMEM((per,), I32)],
        compiler_params=pltpu.CompilerParams(needs_layout_passes=False),
        name="invert",
    )
    def invert(dest_hbm, inv_hbm, table, idx, ids):
        @pl.when(lax.axis_index("core") == 0)
        def _():
            base = lax.axis_index("subcore") * per
            pltpu.sync_copy(dest_hbm.at[pl.ds(base, per)], idx)
            lane = lax.iota(I32, SC_LANES)

            @pl.loop(0, per // SC_LANES)
            def _(i):
                p = base + i * SC_LANES + lane
                tok = lax.shift_right_logical(p, 9) * LANES + (p & (LANES - 1))
                slot = lax.shift_right_logical(p, 7) & (TOP_K - 1)
                ids[pl.ds(i * SC_LANES, SC_LANES)] = tok * TOP_K + slot

            @pl.loop(0, per // SC_INDEX_BATCH)
            def _(j):
                span = pl.ds(j * SC_INDEX_BATCH, SC_INDEX_BATCH)
                pltpu.sync_copy(ids.at[span], table.at[idx.at[span]])

            plsc.subcore_barrier()
            pltpu.sync_copy(table.at[pl.ds(base, per)], inv_hbm.at[pl.ds(base, per)])

    return invert(dest)


def _experts_kernel(gid_ref, tid_ref, nxt_ref, offs_ref, nact_ref,
                    invp_ref, inv0_ref, inv1_ref, inv2_ref, bg_ref, bu_ref, bd_ref,
                    h2_hbm, wg_hbm, wu_hbm, wd_hbm, y4_hbm,
                    wg_b, wu_b, wd_b, wg_f, wu_f, wd_f, relay, rows, ybuf, sems, row_sems, y_sems):
    s = pl.program_id(0)
    n_steps = pl.num_programs(0)
    tile_rows = rows.shape[1]
    tm = tile_rows // SUBLANES
    landing = ((wg_hbm, wg_f, wg_b), (wu_hbm, wu_f, wu_b), (wd_hbm, wd_f, wd_b))
    slot_bits = TOP_K.bit_length() - 1

    def fetch(e):
        for j, (hbm, land, _) in enumerate(landing):
            pltpu.make_async_copy(hbm.at[e], land, sems.at[j]).start()

    def row_in(idx_ref, into, r):
        tok = lax.shift_right_logical(idx_ref[r], slot_bits)
        src = h2_hbm.at[pl.ds(pl.multiple_of(tok * SUBLANES, SUBLANES), SUBLANES)]
        dst = rows.at[into, pl.ds(pl.multiple_of(r * SUBLANES, SUBLANES), SUBLANES)]
        return pltpu.make_async_copy(src, dst, row_sems.at[into])

    def row_out(idx_ref, frm, r):
        v = idx_ref[r]
        tok = lax.shift_right_logical(v, slot_bits)
        src = ybuf.at[frm, pl.ds(pl.multiple_of(r * SUBLANES, SUBLANES), SUBLANES)]
        dst = y4_hbm.at[v & (TOP_K - 1), pl.ds(pl.multiple_of(tok * SUBLANES, SUBLANES), SUBLANES)]
        return pltpu.make_async_copy(src, dst, y_sems.at[frm])

    def in_line(make, idx_ref, buf):
        for r in range(tm):
            make(idx_ref, buf, r).start(priority=r % 2)

    def in_loop(make, idx_ref, buf):
        def group(g, carry):
            for j in range(ISSUE_GROUP):
                make(idx_ref, buf, g * ISSUE_GROUP + j).start(priority=j % 2)
            return carry
        lax.fori_loop(0, tm // ISSUE_GROUP, group, 0)

    def wait_rows(buf):
        pltpu.make_async_copy(h2_hbm.at[pl.ds(0, tile_rows)], rows.at[buf], row_sems.at[buf]).wait()

    def wait_y(buf):
        pltpu.make_async_copy(ybuf.at[buf], y4_hbm.at[0, pl.ds(0, tile_rows)], y_sems.at[buf]).wait()

    @pl.when(s < nact_ref[0])
    def _():
        e = gid_ref[s]
        m = tid_ref[s]
        cur = lax.rem(m, 3)
        before = lax.rem(m + 2, 3)
        last = nact_ref[0] - 1
        new_tile = (s == 0) | (tid_ref[jnp.maximum(s - 1, 0)] != m)
        next_differs = (s < last) & (tid_ref[jnp.minimum(s + 1, n_steps - 1)] != m)

        @pl.when(s == 0)
        def _():
            fetch(e)
            in_loop(row_in, inv0_ref, cur)
            in_loop(row_in, inv1_ref, lax.rem(m + 1, 3))
            ybuf[before] = jnp.zeros(ybuf.shape[1:], F32)

        @pl.when((s == 0) | (gid_ref[jnp.maximum(s - 1, 0)] != e))
        def _():
            for j, (hbm, land, half) in enumerate(landing):
                pltpu.make_async_copy(hbm.at[e], land, sems.at[j]).wait()
                half[...] = land[...].astype(BF16)

            @pl.when(nxt_ref[s] >= 0)
            def _():
                fetch(nxt_ref[s])

        @pl.when(new_tile)
        def _():
            wait_rows(cur)

            @pl.when(m >= 2)
            def _():
                wait_y(cur)

        def ffn(x):
            g = _dot(x, wg_b[...]) + bg_ref[pl.ds(e, 1), :]
            u = _dot(x, wu_b[...]) + bu_ref[pl.ds(e, 1), :]
            g = jnp.minimum(g, SWIGLU_LIMIT)
            u = jnp.clip(u, -SWIGLU_LIMIT, SWIGLU_LIMIT)
            glu = g * jax.nn.sigmoid(SWIGLU_ALPHA * g)
            return _dot(((u + 1.0) * glu).astype(BF16), wd_b[...]) + bd_ref[pl.ds(e, 1), :]

        lo = offs_ref[e]
        hi = offs_ref[e + 1]
        whole_tile = (lo <= m * tm) & (hi >= (m + 1) * tm)

        @pl.when(whole_tile)
        def _():
            x = _load_token_tiles(rows.at[cur]).astype(BF16)
            in_line(row_in, inv2_ref, before)
            in_line(row_out, invp_ref, before)
            _store_token_tiles(ybuf.at[cur], ffn(x))

        @pl.when(jnp.logical_not(whole_tile))
        def _():
            sub = relay.shape[0] // SUBLANES
            for j in range(tm // sub):
                first = m * tm + j * sub

                @pl.when((lo < first + sub) & (hi > first))
                def _():
                    span = pl.ds(j * sub * SUBLANES, sub * SUBLANES)
                    _store_token_tiles(relay, ffn(_load_token_tiles(rows.at[cur, span]).astype(BF16)))
                    row = first + lax.shift_right_logical(lax.broadcasted_iota(I32, relay.shape, 0), 3)
                    pltpu.store(ybuf.at[cur, span], relay[...], mask=(row >= lo) & (row < hi))

            @pl.when(next_differs | (s == last))
            def _():
                in_loop(row_in, inv2_ref, before)
                in_loop(row_out, invp_ref, before)

        @pl.when(s == last)
        def _():
            in_loop(row_out, inv0_ref, cur)
            wait_rows(lax.rem(m + 1, 3))
            wait_rows(before)
            for buf in range(3):
                wait_y(buf)


def _experts(gid, tid, nxt, offs, nact, n_steps, inv, h2, wg, bg, wu, bu, wd, bd):
    ne, d, f = wg.shape
    tm = EXPERT_TILE
    n_tokens = h2.shape[0] // SUBLANES
    n_tiles = inv.shape[0] // tm
    assert n_tiles >= 2
    inv_ext = jnp.concatenate([n_tokens * TOP_K + jnp.arange(tm, dtype=I32), inv])
    whole = lambda s, gid, tid, nxt, offs, nact: (0, 0)

    def order_of(k):
        return pl.BlockSpec(
            (tm,), lambda s, gid, tid, nxt, offs, nact: (jnp.minimum(tid[s] + k, n_tiles - 1) + 1,),
            memory_space=pltpu.SMEM)

    grid_spec = pltpu.PrefetchScalarGridSpec(
        num_scalar_prefetch=5,
        grid=(n_steps,),
        in_specs=[
            order_of(-1), order_of(0), order_of(1), order_of(2),
            pl.BlockSpec((ne, f), whole),
            pl.BlockSpec((ne, f), whole),
            pl.BlockSpec((ne, d), whole),
            pl.BlockSpec(memory_space=pl.ANY),
            pl.BlockSpec(memory_space=pl.ANY),
            pl.BlockSpec(memory_space=pl.ANY),
            pl.BlockSpec(memory_space=pl.ANY),
        ],
        out_specs=pl.BlockSpec(memory_space=pl.ANY),
        scratch_shapes=[
            pltpu.VMEM((d, f), BF16), pltpu.VMEM((d, f), BF16), pltpu.VMEM((f, d), BF16),
            pltpu.VMEM((d, f), F32), pltpu.VMEM((d, f), F32), pltpu.VMEM((f, d), F32),
            pltpu.VMEM((EXPERT_SUBTILE * SUBLANES, LANES), F32),
            pltpu.VMEM((3, tm * SUBLANES, LANES), F32),
            pltpu.VMEM((3, tm * SUBLANES, LANES), F32),
            pltpu.SemaphoreType.DMA((3,)),
            pltpu.SemaphoreType.DMA((3,)),
            pltpu.SemaphoreType.DMA((3,)),
        ],
    )
    return pl.pallas_call(
        _experts_kernel,
        out_shape=jax.ShapeDtypeStruct((TOP_K, (n_tokens + tm // TOP_K) * SUBLANES, LANES), F32),
        grid_spec=grid_spec,
        compiler_params=pltpu.CompilerParams(
            dimension_semantics=("arbitrary",), vmem_limit_bytes=VMEM_LIMIT),
        name="experts",
    )(gid, tid, nxt, offs, nact, inv_ext, inv_ext, inv_ext, inv_ext, bg, bu, bd, h2, wg, wu, wd)


def _combine_kernel(gate_ref, y4_ref, x1p_ref, x1s_ref, modp_ref, g2s_ref, fn_ref, outp_ref, outs_ref,
                    *, n_prompt_tiles):
    i = pl.program_id(0)
    tm = x1p_ref.shape[0]
    gates = gate_ref[...].T
    cols = []
    for c in range(SUBLANES):
        acc = gates[:, 0:1] * y4_ref[0, pl.ds(c, tm, stride=SUBLANES), :]
        for k in range(1, TOP_K):
            acc = acc + gates[:, k:k + 1] * y4_ref[k, pl.ds(c, tm, stride=SUBLANES), :]
        cols.append(acc)
    ffn = jnp.concatenate(cols, axis=-1)

    @pl.when(i < n_prompt_tiles)
    def _():
        g2 = modp_ref[...][5:6, :]
        outp_ref[...] = _rmsnorm(x1p_ref[...] + g2 * ffn, fn_ref[...])

    @pl.when(i >= n_prompt_tiles)
    def _():
        outs_ref[...] = _rmsnorm(x1s_ref[...] + g2s_ref[...] * ffn, fn_ref[...])


def _combine(gates, y4, x1_p, x1_s, mod_p, g2_s, final_norm, tokens_per_seq):
    n_p, d = x1_p.shape
    n_s = x1_s.shape[0]
    tm = COMBINE_TILE
    npt, nst = n_p // tm, n_s // tm
    tiles_per_seq = tokens_per_seq // tm
    pmap = lambda i: (jnp.minimum(i, npt - 1), 0)
    smap = lambda i: (jnp.maximum(i - npt, 0), 0)
    return pl.pallas_call(
        functools.partial(_combine_kernel, n_prompt_tiles=npt),
        out_shape=(jax.ShapeDtypeStruct((n_p, d), F32), jax.ShapeDtypeStruct((n_s, d), F32)),
        grid=(npt + nst,),
        in_specs=[
            pl.BlockSpec((SUBLANES, tm), lambda i: (0, i)),
            pl.BlockSpec((TOP_K, tm * SUBLANES, LANES), lambda i: (0, i, 0)),
            pl.BlockSpec((tm, d), pmap),
            pl.BlockSpec((tm, d), smap),
            pl.BlockSpec((None, 6, d), lambda i: (jnp.minimum(i, npt - 1) // tiles_per_seq, 0, 0)),
            pl.BlockSpec((tm, d), smap),
            pl.BlockSpec((1, d), lambda i: (0, 0)),
        ],
        out_specs=(pl.BlockSpec((tm, d), pmap), pl.BlockSpec((tm, d), smap)),
        compiler_params=pltpu.CompilerParams(
            dimension_semantics=("arbitrary",), vmem_limit_bytes=VMEM_LIMIT),
        name="combine",
    )(gates, y4, x1_p, x1_s, mod_p, g2_s, final_norm)


def kernel(x_prompt, x_sample, state_pool, state_conv, c_prompt, c_sample, norm1, norm2, w_ada, b_ada,
           w_in, w_pool, pool_scale, w_conv, w_out, w_router, b_router, w_gate, b_gate, w_up, b_up,
           w_down, b_down, final_norm):
    depth = norm1.shape[0]
    assert depth == 1, "single-layer step"
    bp, tp, d = x_prompt.shape
    bs, ts, _ = x_sample.shape
    dp = state_pool.shape[-1]
    n_hist = state_pool.shape[2]
    n_chist = state_conv.shape[2]
    n_p, n_s = bp * tp, bs * ts
    assert d == SUBLANES * LANES, "token-tile layout assumes one vreg tile per token row"
    assert tp % TOKEN_TILE == 0 and n_s == TOKEN_TILE, "the sample group fills exactly one token tile"
    assert tp % COMBINE_TILE == 0 and n_s % COMBINE_TILE == 0
    assert (n_p + n_s) % ROUTE_TILE == 0 and ((n_p + n_s) * TOP_K) % EXPERT_TILE == 0

    l = 0
    n1 = norm1[l].reshape(1, d)
    n2 = norm2[l].reshape(1, d)
    w_in_b = w_in[l].astype(BF16)
    w_pool_b = w_pool[l].astype(BF16)
    w_out_b = w_out[l].astype(BF16)
    pscale = pool_scale[l].reshape(1, dp)
    w_r = w_router[l].T.astype(BF16)
    b_r = b_router[l].reshape(N_EXPERTS, 1)

    mod = _adaln(jnp.concatenate([c_sample, c_prompt], axis=0), w_ada[l], b_ada[l])
    mod_s = mod[:bs]
    mod_p = mod[bs:].reshape(bp, 6, d)

    xs_tm = jnp.transpose(x_sample, (1, 0, 2)).reshape(n_s, d)
    ps_tm = jnp.transpose(state_pool[l], (1, 0, 2))
    cs_tm = jnp.transpose(state_conv[l], (1, 0, 2))
    x1_s, h2_s, lg_s, newp_tm, newc_tm = _mixer_sample(
        xs_tm, mod, ps_tm, cs_tm, n1, n2, w_in_b, w_pool_b, pscale, w_conv[l], w_out_b, w_r, b_r, ts)

    x1_p, h2, lg_p, u_tail, v_tail = _mixer_prompt(
        x_prompt, mod_p, n1, n2, w_in_b, w_pool_b, pscale, w_conv[l], w_out_b, w_r, b_r, h2_s)

    dest, gates, counts_f = _route(lg_p, lg_s)
    counts = counts_f[:, 0].astype(I32)
    n_rows = (n_p + n_s) * TOP_K
    gid, tid, nxt, offs, nact, n_steps = _group_metadata(counts, n_rows, EXPERT_TILE)

    dest = dest.reshape(-1)
    inv = _invert(dest)
    y4 = _experts(gid, tid, nxt, offs, nact, n_steps, inv, h2,
                  w_gate[l], b_gate[l], w_up[l], b_up[l], w_down[l], b_down[l])

    g2_s = jnp.tile(mod_s[:, 5 * d:], (ts, 1))
    y_p, y_s = _combine(gates, y4, x1_p, x1_s, mod_p, g2_s, final_norm.reshape(1, d), tp)

    y_prompt = y_p.reshape(bp, tp, d)
    y_sample = jnp.transpose(y_s.reshape(ts, bs, d), (1, 0, 2))
    new_pool_prompt = u_tail[:, POOL_HALO - n_hist:, :][None]
    new_conv_prompt = v_tail[:, CONV_HALO - n_chist:, :][None]
    new_pool_sample = jnp.transpose(newp_tm, (1, 0, 2))[None]
    new_conv_sample = jnp.transpose(newc_tm, (1, 0, 2))[None]
    return (y_prompt, y_sample, new_pool_prompt, new_conv_prompt, new_pool_sample, new_conv_sample)
```

```python
import functools

import jax
import jax.numpy as jnp
from jax import lax
from jax.experimental import pallas as pl
from jax.experimental.pallas import tpu as pltpu
from jax.experimental.pallas import tpu_sc as plsc

F32 = jnp.float32
BF16 = jnp.bfloat16
I32 = jnp.int32

POOL_WINDOWS = (2, 4, 8, 16)
POOL_HALO = 16
CONV_TAPS = 3
CONV_HALO = 8
N_EXPERTS = 32
TOP_K = 4
SWIGLU_LIMIT = 7.0
SWIGLU_ALPHA = 1.702
EPS = 1e-5
PAST_LEN = 16384

LANES = 128
SUBLANES = 8

TOKEN_TILE = 512
ROUTE_TILE = 512
COMBINE_TILE = 512
EXPERT_TILE = 512
EXPERT_SUBTILE = 128
VMEM_LIMIT = 56 * 1024 * 1024


def _rmsnorm(x, g):
    ms = jnp.mean(x * x, axis=-1, keepdims=True)
    return x * lax.rsqrt(ms + EPS) * g


def _dot(a, b):
    return jnp.dot(a, b, preferred_element_type=F32)


def _store_token_tiles(ref, val):
    rows = val.shape[0]
    for c in range(SUBLANES):
        ref[pl.ds(c, rows, stride=SUBLANES), :] = val[:, c * LANES:(c + 1) * LANES]


def _load_token_tiles(ref):
    rows = ref.shape[0] // SUBLANES
    return jnp.concatenate([ref[pl.ds(c, rows, stride=SUBLANES), :] for c in range(SUBLANES)], axis=-1)


ISSUE_GROUP = 4


def _adaln_kernel(c_ref, w_ref, b_ref, o_ref):
    c = c_ref[...]
    s = c * jax.nn.sigmoid(c)
    o_ref[...] = _dot(s.astype(BF16), w_ref[...].astype(BF16)) + b_ref[...]


def _adaln(c, w_ada, b_ada):
    rows, d = c.shape
    n = w_ada.shape[1]
    tn = 1024
    return pl.pallas_call(
        _adaln_kernel,
        out_shape=jax.ShapeDtypeStruct((rows, n), F32),
        grid=(n // tn,),
        in_specs=[
            pl.BlockSpec((rows, d), lambda j: (0, 0)),
            pl.BlockSpec((d, tn), lambda j: (0, j)),
            pl.BlockSpec((1, tn), lambda j: (0, j)),
        ],
        out_specs=pl.BlockSpec((rows, tn), lambda j: (0, j)),
        compiler_params=pltpu.CompilerParams(
            dimension_semantics=("arbitrary",), vmem_limit_bytes=VMEM_LIMIT),
        name="adaln",
    )(c, w_ada, b_ada.reshape(1, n))


def _mix_tail(x, pool_in, conv_out, g1, sc2, sh2, n2, wpool_ref, pscale, wout, wr, br):
    gw = pool_in.shape[1] // len(POOL_WINDOWS)
    mixed = [_dot(pool_in[:, g * gw:(g + 1) * gw].astype(BF16), wpool_ref[g])
             for g in range(len(POOL_WINDOWS))]
    pool_out = jnp.concatenate(mixed, axis=-1) * pscale
    mix_in = jnp.concatenate([pool_out, conv_out], axis=-1).astype(BF16)
    x1 = x + g1 * _dot(mix_in, wout)
    h2 = _rmsnorm(x1, n2) * (1.0 + sc2) + sh2
    logits_t = lax.dot_general(wr, h2.astype(BF16), (((1,), (1,)), ((), ())),
                               preferred_element_type=F32) + br
    return x1, h2, logits_t


def _mixer_prompt_kernel(x_ref, mod_ref, n1_ref, n2_ref, win_ref, wpool_ref, pscale_ref, wconv_ref,
                         wout_ref, wr_ref, br_ref, h2s_ref,
                         x1_ref, h2_ref, lg_ref, upool_ref, vconv_ref, ubuf, vbuf, *, tiles_per_seq):
    i = pl.program_id(0)

    @pl.when(i < pl.num_programs(0) - 1)
    def _():
        _mixer_prompt_tile(x_ref, mod_ref, n1_ref, n2_ref, win_ref, wpool_ref, pscale_ref, wconv_ref,
                           wout_ref, wr_ref, br_ref, x1_ref, h2_ref, lg_ref, upool_ref, vconv_ref,
                           ubuf, vbuf, lax.rem(i, tiles_per_seq), tiles_per_seq)

    @pl.when(i == pl.num_programs(0) - 1)
    def _():
        h2_ref[...] = h2s_ref[...]


def _mixer_prompt_tile(x_ref, mod_ref, n1_ref, n2_ref, win_ref, wpool_ref, pscale_ref, wconv_ref,
                       wout_ref, wr_ref, br_ref, x1_ref, h2_ref, lg_ref, upool_ref, vconv_ref,
                       ubuf, vbuf, t, tiles_per_seq):
    tt = x_ref.shape[0]
    dp = ubuf.shape[1]
    gw = dp // len(POOL_WINDOWS)

    @pl.when(t == 0)
    def _():
        ubuf[0:POOL_HALO, :] = jnp.zeros((POOL_HALO, dp), F32)
        vbuf[0:CONV_HALO, :] = jnp.zeros((CONV_HALO, dp), F32)

    x = x_ref[...]
    mod = mod_ref[...]
    sh1, sc1, g1, sh2, sc2, _ = [mod[i:i + 1, :] for i in range(6)]
    h = _rmsnorm(x, n1_ref[...]) * (1.0 + sc1) + sh1
    z = _dot(h.astype(BF16), win_ref[...])
    u, gate_b, gate_c, val = [z[:, i * dp:(i + 1) * dp] for i in range(4)]

    ubuf[POOL_HALO:POOL_HALO + tt, :] = u
    pos = lax.broadcasted_iota(I32, (tt, gw), 0) + t * tt
    pooled = []
    for g, w in enumerate(POOL_WINDOWS):
        cols = slice(g * gw, (g + 1) * gw)
        acc = u[:, cols]
        for j in range(1, w):
            acc = acc + ubuf[POOL_HALO - j:POOL_HALO - j + tt, cols]
        cnt = jnp.minimum(pos + 1, w).astype(F32)
        pooled.append(acc / cnt - u[:, cols])
    pool_in = jnp.concatenate(pooled, axis=-1)

    v = gate_c * val
    vbuf[CONV_HALO:CONV_HALO + tt, :] = v
    wc = wconv_ref[...]
    y = (wc[0:1, :] * vbuf[CONV_HALO - 2:CONV_HALO - 2 + tt, :]
         + wc[1:2, :] * vbuf[CONV_HALO - 1:CONV_HALO - 1 + tt, :]
         + wc[2:3, :] * v)
    conv_out = gate_b * y

    x1, h2, logits = _mix_tail(x, pool_in, conv_out, g1, sc2, sh2, n2_ref[...], wpool_ref,
                               pscale_ref[...], wout_ref[...], wr_ref[...], br_ref[...])
    x1_ref[...] = x1
    _store_token_tiles(h2_ref, h2)
    lg_ref[...] = logits

    ubuf[0:POOL_HALO, :] = ubuf[tt:tt + POOL_HALO, :]
    vbuf[0:CONV_HALO, :] = vbuf[tt:tt + CONV_HALO, :]

    @pl.when(t == tiles_per_seq - 1)
    def _():
        upool_ref[...] = ubuf[0:POOL_HALO, :]
        vconv_ref[...] = vbuf[0:CONV_HALO, :]


def _mixer_prompt(x, mod_p, n1, n2, w_in, w_pool, pscale, w_conv, w_out, w_r, b_r, h2_s):
    b, t, d = x.shape
    dp = w_pool.shape[0] * w_pool.shape[1]
    tt = min(TOKEN_TILE, t)
    nt = t // tt
    n_real = b * nt
    assert h2_s.shape[0] == tt * SUBLANES
    seq = lambda i: jnp.minimum(i, n_real - 1) // nt
    tile = lambda i: jnp.minimum(i, n_real - 1)
    const2 = lambda i: (0, 0)
    const3 = lambda i: (0, 0, 0)
    return pl.pallas_call(
        functools.partial(_mixer_prompt_kernel, tiles_per_seq=nt),
        out_shape=(
            jax.ShapeDtypeStruct((b * t, d), F32),
            jax.ShapeDtypeStruct(((n_real + 1) * tt * SUBLANES, LANES), F32),
            jax.ShapeDtypeStruct((N_EXPERTS, b * t), F32),
            jax.ShapeDtypeStruct((b, POOL_HALO, dp), F32),
            jax.ShapeDtypeStruct((b, CONV_HALO, dp), F32),
        ),
        grid=(n_real + 1,),
        in_specs=[
            pl.BlockSpec((None, tt, d), lambda i: (seq(i), tile(i) % nt, 0)),
            pl.BlockSpec((None, 6, d), lambda i: (seq(i), 0, 0)),
            pl.BlockSpec((1, d), const2),
            pl.BlockSpec((1, d), const2),
            pl.BlockSpec(w_in.shape, const2),
            pl.BlockSpec(w_pool.shape, const3),
            pl.BlockSpec((1, dp), const2),
            pl.BlockSpec(w_conv.shape, const2),
            pl.BlockSpec(w_out.shape, const2),
            pl.BlockSpec(w_r.shape, const2),
            pl.BlockSpec((N_EXPERTS, 1), const2),
            pl.BlockSpec(h2_s.shape, const2),
        ],
        out_specs=(
            pl.BlockSpec((tt, d), lambda i: (tile(i), 0)),
            pl.BlockSpec((tt * SUBLANES, LANES), lambda i: (i, 0)),
            pl.BlockSpec((N_EXPERTS, tt), lambda i: (0, tile(i))),
            pl.BlockSpec((None, POOL_HALO, dp), lambda i: (seq(i), 0, 0)),
            pl.BlockSpec((None, CONV_HALO, dp), lambda i: (seq(i), 0, 0)),
        ),
        scratch_shapes=[
            pltpu.VMEM((POOL_HALO + tt, dp), F32),
            pltpu.VMEM((CONV_HALO + tt, dp), F32),
        ],
        compiler_params=pltpu.CompilerParams(
            dimension_semantics=("arbitrary",), vmem_limit_bytes=VMEM_LIMIT),
        name="mixer_prompt",
    )(x, mod_p, n1, n2, w_in, w_pool, pscale, w_conv, w_out, w_r, b_r, h2_s)


def _mixer_sample_kernel(x_ref, mod_ref, pstate_ref, cstate_ref, n1_ref, n2_ref, win_ref, wpool_ref,
                         pscale_ref, wconv_ref, wout_ref, wr_ref, br_ref,
                         x1_ref, h2_ref, lg_ref, newp_ref, newc_ref, *, steps):
    nb = pstate_ref.shape[1]
    d = x_ref.shape[1]
    dp = pstate_ref.shape[2]
    gw = dp // len(POOL_WINDOWS)
    n_hist = pstate_ref.shape[0]
    n_chist = cstate_ref.shape[0]

    x = x_ref[...]
    mod = mod_ref[0:nb, :]
    rep = lambda a: jnp.concatenate([a] * steps, axis=0)
    sh1, sc1, g1, sh2, sc2, _ = [rep(mod[:, i * d:(i + 1) * d]) for i in range(6)]
    h = _rmsnorm(x, n1_ref[...]) * (1.0 + sc1) + sh1
    z = _dot(h.astype(BF16), win_ref[...])
    u, gate_b, gate_c, val = [z[:, i * dp:(i + 1) * dp] for i in range(4)]

    ext = [pstate_ref[i] for i in range(n_hist)] + [u[s * nb:(s + 1) * nb, :] for s in range(steps)]
    pooled_steps = []
    for s in range(steps):
        groups = []
        for g, w in enumerate(POOL_WINDOWS):
            cols = slice(g * gw, (g + 1) * gw)
            acc = ext[n_hist + s][:, cols]
            for j in range(1, w):
                acc = acc + ext[n_hist + s - j][:, cols]
            cnt = float(min(PAST_LEN + s + 1, w))
            groups.append(acc / cnt - ext[n_hist + s][:, cols])
        pooled_steps.append(jnp.concatenate(groups, axis=-1))
    pool_in = jnp.concatenate(pooled_steps, axis=0)

    v = gate_c * val
    vext = [cstate_ref[i] for i in range(n_chist)] + [v[s * nb:(s + 1) * nb, :] for s in range(steps)]
    wc = wconv_ref[...]
    y = jnp.concatenate(
        [wc[0:1, :] * vext[s] + wc[1:2, :] * vext[s + 1] + wc[2:3, :] * vext[s + 2] for s in range(steps)],
        axis=0)
    conv_out = gate_b * y

    x1, h2, logits = _mix_tail(x, pool_in, conv_out, g1, sc2, sh2, n2_ref[...], wpool_ref,
                               pscale_ref[...], wout_ref[...], wr_ref[...], br_ref[...])
    x1_ref[...] = x1
    _store_token_tiles(h2_ref, h2)
    lg_ref[...] = logits
    for i in range(n_hist):
        newp_ref[i] = ext[steps + i]
    for i in range(n_chist):
        newc_ref[i] = vext[steps + i]


def _mixer_sample(x_tm, mod_s, pstate_tm, cstate_tm, n1, n2, w_in, w_pool, pscale, w_conv, w_out, w_r, b_r,
                  steps):
    rows, d = x_tm.shape
    return pl.pallas_call(
        functools.partial(_mixer_sample_kernel, steps=steps),
        out_shape=(
            jax.ShapeDtypeStruct((rows, d), F32),
            jax.ShapeDtypeStruct((rows * SUBLANES, LANES), F32),
            jax.ShapeDtypeStruct((N_EXPERTS, rows), F32),
            jax.ShapeDtypeStruct(pstate_tm.shape, F32),
            jax.ShapeDtypeStruct(cstate_tm.shape, F32),
        ),
        compiler_params=pltpu.CompilerParams(vmem_limit_bytes=VMEM_LIMIT),
        name="mixer_sample",
    )(x_tm, mod_s, pstate_tm, cstate_tm, n1, n2, w_in, w_pool, pscale, w_conv, w_out, w_r, b_r)


def _route_kernel(lgp_ref, lgs_ref, dest_ref, gate_ref, cnt_ref, counts, start, before):
    ne = lgp_ref.shape[0]
    tr = before.shape[0]
    reps = tr // LANES
    n_prompt_chunks = lgp_ref.shape[1] // tr
    n_sample_chunks = lgs_ref.shape[1] // tr
    eidx = lax.broadcasted_iota(I32, (ne, tr), 0)

    def top_k(ref, c):
        work = ref[:, pl.ds(pl.multiple_of(c * tr, tr), tr)]
        top_v, onehots = [], []
        for _ in range(TOP_K):
            m = jnp.max(work, axis=0, keepdims=True)
            idx = jnp.min(jnp.where(work == m, eidx, ne), axis=0, keepdims=True)
            sel = eidx == idx
            top_v.append(m)
            onehots.append(sel)
            work = jnp.where(sel, -jnp.inf, work)
        mask = jnp.where(onehots[0] | onehots[1] | onehots[2] | onehots[3], 1.0, 0.0)
        chunk_counts = jnp.broadcast_to(jnp.sum(mask, axis=1, keepdims=True), (ne, LANES))
        return top_v, onehots, mask, chunk_counts

    def count_chunk(ref):
        def body(c, carry):
            counts[...] = counts[...] + top_k(ref, c)[3]
            return carry
        return body

    counts[...] = jnp.zeros_like(counts)
    lax.fori_loop(0, n_prompt_chunks, count_chunk(lgp_ref), 0)
    lax.fori_loop(0, n_sample_chunks, count_chunk(lgs_ref), 0)

    total = counts[...]
    hi = jnp.floor(total * (1.0 / 256.0))
    lo = total - hi * 256.0
    r = lax.broadcasted_iota(I32, (ne, ne), 0)
    col = lax.broadcasted_iota(I32, (ne, ne), 1)
    lower = jnp.where(col < r, 1.0, 0.0).astype(BF16)
    start[...] = 256.0 * _dot(lower, hi.astype(BF16)) + _dot(lower, lo.astype(BF16))
    cnt_ref[...] = total
    counts[...] = jnp.zeros_like(counts)

    r = lax.broadcasted_iota(I32, (tr, tr), 0)
    col = lax.broadcasted_iota(I32, (tr, tr), 1)
    before[...] = jnp.where(r < col, 1.0, 0.0).astype(BF16)
    gate_ref[...] = jnp.zeros_like(gate_ref)

    def place_chunk(ref, first_chunk):
        def body(c, carry):
            top_v, onehots, mask, chunk_counts = top_k(ref, c)
            base = jnp.concatenate([counts[...] + start[...]] * reps, axis=1)
            rank = _dot(mask.astype(BF16), before[...]) + base
            counts[...] = counts[...] + chunk_counts
            es = [jnp.exp(v - top_v[0]) for v in top_v]
            denom = es[0] + es[1] + es[2] + es[3]
            chunk = first_chunk + c
            cols = pl.ds(pl.multiple_of(chunk * tr, tr), tr)
            for k in range(TOP_K):
                d = jnp.sum(jnp.where(onehots[k], rank, 0.0), axis=0, keepdims=True).astype(I32)
                for j in range(reps):
                    dest_ref[chunk * reps + j, k:k + 1, :] = d[:, j * LANES:(j + 1) * LANES]
                gate_ref[k:k + 1, cols] = es[k] / denom
            return carry
        return body

    lax.fori_loop(0, n_prompt_chunks, place_chunk(lgp_ref, 0), 0)
    lax.fori_loop(0, n_sample_chunks, place_chunk(lgs_ref, n_prompt_chunks), 0)


def _route(lgt_p, lgt_s):
    ne, n_p = lgt_p.shape
    n_s = lgt_s.shape[1]
    tr = ROUTE_TILE
    return pl.pallas_call(
        _route_kernel,
        out_shape=(
            jax.ShapeDtypeStruct(((n_p + n_s) // LANES, TOP_K, LANES), I32),
            jax.ShapeDtypeStruct((SUBLANES, n_p + n_s), F32),
            jax.ShapeDtypeStruct((ne, LANES), F32),
        ),
        scratch_shapes=[pltpu.VMEM((ne, LANES), F32), pltpu.VMEM((ne, LANES), F32),
                        pltpu.VMEM((tr, tr), BF16)],
        compiler_params=pltpu.CompilerParams(vmem_limit_bytes=VMEM_LIMIT),
        name="route",
    )(lgt_p, lgt_s)


def _group_metadata(counts, n_rows, tile):
    n_tiles = n_rows // tile
    n_steps = n_tiles + N_EXPERTS - 1
    ends = jnp.cumsum(counts)
    offs = jnp.concatenate([jnp.zeros((1,), I32), ends]).astype(I32)
    first_tile = offs[:-1] // tile
    last_tile = (ends - 1) // tile
    tiles_e = jnp.where(counts > 0, last_tile - first_tile + 1, 0)
    step_end = jnp.cumsum(tiles_e)
    step_start = step_end - tiles_e
    n_active = step_end[-1]
    s = jnp.minimum(jnp.arange(n_steps, dtype=I32), n_active - 1)
    owner = ((s[:, None] >= step_start[None, :]) & (s[:, None] < step_end[None, :])).astype(I32)
    gid = jnp.sum(owner * jnp.arange(N_EXPERTS, dtype=I32)[None, :], axis=1)
    tid = jnp.sum(owner * (first_tile - step_start)[None, :], axis=1) + s
    ids = jnp.arange(N_EXPERTS, dtype=I32)
    later = (ids[None, :] > ids[:, None]) & (counts[None, :] > 0)
    next_e = jnp.min(jnp.where(later, ids[None, :], N_EXPERTS), axis=1)
    next_e = jnp.where(next_e == N_EXPERTS, -1, next_e)
    nxt = jnp.sum(owner * next_e[None, :], axis=1)
    return gid, tid, nxt, offs, n_active.reshape(1).astype(I32), n_steps


SC_LANES = 16
SC_INDEX_BATCH = 128


def _invert(dest):
    r = dest.shape[0]
    mesh = plsc.VectorSubcoreMesh(core_axis_name="core", subcore_axis_name="subcore")
    per = r // mesh.num_subcores
    assert per * mesh.num_subcores == r and per % SC_INDEX_BATCH == 0

    @functools.partial(
        pl.kernel, mesh=mesh, out_type=jax.ShapeDtypeStruct((r,), I32),
        scratch_types=[pltpu.VMEM_SHARED((r,), I32), pltpu.VMEM((per,), I32), pltpu.VMEM((per,), I32)],
        compiler_params=pltpu.CompilerParams(needs_layout_passes=False),
        name="invert",
    )
    def invert(dest_hbm, inv_hbm, table, idx, ids):
        @pl.when(lax.axis_index("core") == 0)
        def _():
            base = lax.axis_index("subcore") * per
            pltpu.sync_copy(dest_hbm.at[pl.ds(base, per)], idx)
            lane = lax.iota(I32, SC_LANES)

            @pl.loop(0, per // SC_LANES)
            def _(i):
                p = base + i * SC_LANES + lane
                tok = lax.shift_right_logical(p, 9) * LANES + (p & (LANES - 1))
                slot = lax.shift_right_logical(p, 7) & (TOP_K - 1)
                ids[pl.ds(i * SC_LANES, SC_LANES)] = tok * TOP_K + slot

            @pl.loop(0, per // SC_INDEX_BATCH)
            def _(j):
                span = pl.ds(j * SC_INDEX_BATCH, SC_INDEX_BATCH)
                pltpu.sync_copy(ids.at[span], table.at[idx.at[span]])

            plsc.subcore_barrier()
            pltpu.sync_copy(table.at[pl.ds(base, per)], inv_hbm.at[pl.ds(base, per)])

    return invert(dest)


def _experts_kernel(gid_ref, tid_ref, nxt_ref, offs_ref, nact_ref,
                    dstp_ref, dst0_ref, src0_ref, src1_ref, src2_ref, bg_ref, bu_ref, bd_ref,
                    h2_hbm, wg_hbm, wu_hbm, wd_hbm, y4_hbm,
                    wg_b, wu_b, wd_b, wg_f, wu_f, wd_f, relay, rows, ybuf, sems, row_sems, y_sems):
    s = pl.program_id(0)
    n_steps = pl.num_programs(0)
    tile_rows = rows.shape[1]
    tm = tile_rows // SUBLANES
    landing = ((wg_hbm, wg_f, wg_b), (wu_hbm, wu_f, wu_b), (wd_hbm, wd_f, wd_b))

    def fetch(e):
        for j, (hbm, land, _) in enumerate(landing):
            pltpu.make_async_copy(hbm.at[e], land, sems.at[j]).start()

    def row_in(idx_ref, into, r):
        src = h2_hbm.at[pl.ds(pl.multiple_of(idx_ref[r], SUBLANES), SUBLANES)]
        dst = rows.at[into, pl.ds(pl.multiple_of(r * SUBLANES, SUBLANES), SUBLANES)]
        return pltpu.make_async_copy(src, dst, row_sems.at[into])

    def row_out(idx_ref, frm, r):
        src = ybuf.at[frm, pl.ds(pl.multiple_of(r * SUBLANES, SUBLANES), SUBLANES)]
        dst = y4_hbm.at[pl.ds(pl.multiple_of(idx_ref[r], SUBLANES), SUBLANES)]
        return pltpu.make_async_copy(src, dst, y_sems.at[frm])

    def in_line(make, idx_ref, buf):
        for r in range(tm):
            make(idx_ref, buf, r).start(priority=r % 2)

    def in_loop(make, idx_ref, buf):
        def group(g, carry):
            for j in range(ISSUE_GROUP):
                make(idx_ref, buf, g * ISSUE_GROUP + j).start(priority=j % 2)
            return carry
        lax.fori_loop(0, tm // ISSUE_GROUP, group, 0)

    def wait_rows(buf):
        pltpu.make_async_copy(h2_hbm.at[pl.ds(0, tile_rows)], rows.at[buf], row_sems.at[buf]).wait()

    def wait_y(buf):
        pltpu.make_async_copy(ybuf.at[buf], y4_hbm.at[pl.ds(0, tile_rows)], y_sems.at[buf]).wait()

    @pl.when(s < nact_ref[0])
    def _():
        e = gid_ref[s]
        m = tid_ref[s]
        cur = lax.rem(m, 3)
        before = lax.rem(m + 2, 3)
        last = nact_ref[0] - 1
        new_tile = (s == 0) | (tid_ref[jnp.maximum(s - 1, 0)] != m)
        next_differs = (s < last) & (tid_ref[jnp.minimum(s + 1, n_steps - 1)] != m)

        @pl.when(s == 0)
        def _():
            fetch(e)
            in_loop(row_in, src0_ref, 0)
            in_loop(row_in, src1_ref, 1)
            ybuf[2] = jnp.zeros(ybuf.shape[1:], F32)

        @pl.when((s == 0) | (gid_ref[jnp.maximum(s - 1, 0)] != e))
        def _():
            for j, (hbm, land, half) in enumerate(landing):
                pltpu.make_async_copy(hbm.at[e], land, sems.at[j]).wait()
                half[...] = land[...].astype(BF16)

            @pl.when(nxt_ref[s] >= 0)
            def _():
                fetch(nxt_ref[s])

        @pl.when(new_tile)
        def _():
            wait_rows(cur)

            @pl.when(m >= 2)
            def _():
                wait_y(cur)

        def ffn(x):
            g = _dot(x, wg_b[...]) + bg_ref[pl.ds(e, 1), :]
            u = _dot(x, wu_b[...]) + bu_ref[pl.ds(e, 1), :]
            g = jnp.minimum(g, SWIGLU_LIMIT)
            u = jnp.clip(u, -SWIGLU_LIMIT, SWIGLU_LIMIT)
            glu = g * jax.nn.sigmoid(SWIGLU_ALPHA * g)
            return _dot(((u + 1.0) * glu).astype(BF16), wd_b[...]) + bd_ref[pl.ds(e, 1), :]

        lo = offs_ref[e]
        hi = offs_ref[e + 1]
        whole_tile = (lo <= m * tm) & (hi >= (m + 1) * tm)

        @pl.when(whole_tile)
        def _():
            x = _load_token_tiles(rows.at[cur]).astype(BF16)
            in_line(row_in, src2_ref, before)
            in_line(row_out, dstp_ref, before)
            _store_token_tiles(ybuf.at[cur], ffn(x))

        @pl.when(jnp.logical_not(whole_tile))
        def _():
            sub = relay.shape[0] // SUBLANES
            for j in range(tm // sub):
                first = m * tm + j * sub

                @pl.when((lo < first + sub) & (hi > first))
                def _():
                    span = pl.ds(j * sub * SUBLANES, sub * SUBLANES)
                    _store_token_tiles(relay, ffn(_load_token_tiles(rows.at[cur, span]).astype(BF16)))
                    row = first + lax.shift_right_logical(lax.broadcasted_iota(I32, relay.shape, 0), 3)
                    pltpu.store(ybuf.at[cur, span], relay[...], mask=(row >= lo) & (row < hi))

            @pl.when(next_differs | (s == last))
            def _():
                in_loop(row_in, src2_ref, before)
                in_loop(row_out, dstp_ref, before)

        @pl.when(s == last)
        def _():
            in_loop(row_out, dst0_ref, cur)
            wait_rows(lax.rem(m + 1, 3))
            wait_rows(before)
            for buf in range(3):
                wait_y(buf)


def _experts(gid, tid, nxt, offs, nact, n_steps, inv, h2, wg, bg, wu, bu, wd, bd):
    ne, d, f = wg.shape
    tm = EXPERT_TILE
    n_tokens = h2.shape[0] // SUBLANES
    n_tiles = inv.shape[0] // tm
    assert n_tiles >= 2
    slot_rows = (n_tokens + tm // TOP_K) * SUBLANES
    inv_ext = jnp.concatenate([n_tokens * TOP_K + jnp.arange(tm, dtype=I32), inv])
    tok = lax.shift_right_logical(inv_ext, TOP_K.bit_length() - 1)
    src_row = tok * SUBLANES
    dst_row = (inv_ext & (TOP_K - 1)) * slot_rows + src_row
    whole = lambda s, gid, tid, nxt, offs, nact: (0, 0)

    def order_of(k):
        return pl.BlockSpec(
            (tm,), lambda s, gid, tid, nxt, offs, nact: (jnp.minimum(tid[s] + k, n_tiles - 1) + 1,),
            memory_space=pltpu.SMEM)

    grid_spec = pltpu.PrefetchScalarGridSpec(
        num_scalar_prefetch=5,
        grid=(n_steps,),
        in_specs=[
            order_of(-1), order_of(0), order_of(0), order_of(1), order_of(2),
            pl.BlockSpec((ne, f), whole),
            pl.BlockSpec((ne, f), whole),
            pl.BlockSpec((ne, d), whole),
            pl.BlockSpec(memory_space=pl.ANY),
            pl.BlockSpec(memory_space=pl.ANY),
            pl.BlockSpec(memory_space=pl.ANY),
            pl.BlockSpec(memory_space=pl.ANY),
        ],
        out_specs=pl.BlockSpec(memory_space=pl.ANY),
        scratch_shapes=[
            pltpu.VMEM((d, f), BF16), pltpu.VMEM((d, f), BF16), pltpu.VMEM((f, d), BF16),
            pltpu.VMEM((d, f), F32), pltpu.VMEM((d, f), F32), pltpu.VMEM((f, d), F32),
            pltpu.VMEM((EXPERT_SUBTILE * SUBLANES, LANES), F32),
            pltpu.VMEM((3, tm * SUBLANES, LANES), F32),
            pltpu.VMEM((3, tm * SUBLANES, LANES), F32),
            pltpu.SemaphoreType.DMA((3,)),
            pltpu.SemaphoreType.DMA((3,)),
            pltpu.SemaphoreType.DMA((3,)),
        ],
    )
    y4 = pl.pallas_call(
        _experts_kernel,
        out_shape=jax.ShapeDtypeStruct((TOP_K * slot_rows, LANES), F32),
        grid_spec=grid_spec,
        compiler_params=pltpu.CompilerParams(
            dimension_semantics=("arbitrary",), vmem_limit_bytes=VMEM_LIMIT),
        name="experts",
    )(gid, tid, nxt, offs, nact, dst_row, dst_row, src_row, src_row, src_row, bg, bu, bd, h2, wg, wu, wd)
    return y4.reshape(TOP_K, slot_rows, LANES)


def _combine_kernel(gate_ref, y4_ref, x1p_ref, x1s_ref, modp_ref, g2s_ref, fn_ref, outp_ref, outs_ref,
                    *, n_prompt_tiles):
    i = pl.program_id(0)
    tm = x1p_ref.shape[0]
    gates = gate_ref[...].T
    cols = []
    for c in range(SUBLANES):
        acc = gates[:, 0:1] * y4_ref[0, pl.ds(c, tm, stride=SUBLANES), :]
        for k in range(1, TOP_K):
            acc = acc + gates[:, k:k + 1] * y4_ref[k, pl.ds(c, tm, stride=SUBLANES), :]
        cols.append(acc)
    ffn = jnp.concatenate(cols, axis=-1)

    @pl.when(i < n_prompt_tiles)
    def _():
        g2 = modp_ref[...][5:6, :]
        outp_ref[...] = _rmsnorm(x1p_ref[...] + g2 * ffn, fn_ref[...])

    @pl.when(i >= n_prompt_tiles)
    def _():
        outs_ref[...] = _rmsnorm(x1s_ref[...] + g2s_ref[...] * ffn, fn_ref[...])


def _combine(gates, y4, x1_p, x1_s, mod_p, g2_s, final_norm, tokens_per_seq):
    n_p, d = x1_p.shape
    n_s = x1_s.shape[0]
    tm = COMBINE_TILE
    npt, nst = n_p // tm, n_s // tm
    tiles_per_seq = tokens_per_seq // tm
    pmap = lambda i: (jnp.minimum(i, npt - 1), 0)
    smap = lambda i: (jnp.maximum(i - npt, 0), 0)
    return pl.pallas_call(
        functools.partial(_combine_kernel, n_prompt_tiles=npt),
        out_shape=(jax.ShapeDtypeStruct((n_p, d), F32), jax.ShapeDtypeStruct((n_s, d), F32)),
        grid=(npt + nst,),
        in_specs=[
            pl.BlockSpec((SUBLANES, tm), lambda i: (0, i)),
            pl.BlockSpec((TOP_K, tm * SUBLANES, LANES), lambda i: (0, i, 0)),
            pl.BlockSpec((tm, d), pmap),
            pl.BlockSpec((tm, d), smap),
            pl.BlockSpec((None, 6, d), lambda i: (jnp.minimum(i, npt - 1) // tiles_per_seq, 0, 0)),
            pl.BlockSpec((tm, d), smap),
            pl.BlockSpec((1, d), lambda i: (0, 0)),
        ],
        out_specs=(pl.BlockSpec((tm, d), pmap), pl.BlockSpec((tm, d), smap)),
        compiler_params=pltpu.CompilerParams(
            dimension_semantics=("arbitrary",), vmem_limit_bytes=VMEM_LIMIT),
        name="combine",
    )(gates, y4, x1_p, x1_s, mod_p, g2_s, final_norm)


def kernel(x_prompt, x_sample, state_pool, state_conv, c_prompt, c_sample, norm1, norm2, w_ada, b_ada,
           w_in, w_pool, pool_scale, w_conv, w_out, w_router, b_router, w_gate, b_gate, w_up, b_up,
           w_down, b_down, final_norm):
    depth = norm1.shape[0]
    assert depth == 1, "single-layer step"
    bp, tp, d = x_prompt.shape
    bs, ts, _ = x_sample.shape
    dp = state_pool.shape[-1]
    n_hist = state_pool.shape[2]
    n_chist = state_conv.shape[2]
    n_p, n_s = bp * tp, bs * ts
    assert d == SUBLANES * LANES, "token-tile layout assumes one vreg tile per token row"
    assert tp % TOKEN_TILE == 0 and n_s == TOKEN_TILE, "the sample group fills exactly one token tile"
    assert tp % COMBINE_TILE == 0 and n_s % COMBINE_TILE == 0
    assert (n_p + n_s) % ROUTE_TILE == 0 and ((n_p + n_s) * TOP_K) % EXPERT_TILE == 0

    l = 0
    n1 = norm1[l].reshape(1, d)
    n2 = norm2[l].reshape(1, d)
    w_in_b = w_in[l].astype(BF16)
    w_pool_b = w_pool[l].astype(BF16)
    w_out_b = w_out[l].astype(BF16)
    pscale = pool_scale[l].reshape(1, dp)
    w_r = w_router[l].T.astype(BF16)
    b_r = b_router[l].reshape(N_EXPERTS, 1)

    mod = _adaln(jnp.concatenate([c_sample, c_prompt], axis=0), w_ada[l], b_ada[l])
    mod_s = mod[:bs]
    mod_p = mod[bs:].reshape(bp, 6, d)

    xs_tm = jnp.transpose(x_sample, (1, 0, 2)).reshape(n_s, d)
    ps_tm = jnp.transpose(state_pool[l], (1, 0, 2))
    cs_tm = jnp.transpose(state_conv[l], (1, 0, 2))
    x1_s, h2_s, lg_s, newp_tm, newc_tm = _mixer_sample(
        xs_tm, mod, ps_tm, cs_tm, n1, n2, w_in_b, w_pool_b, pscale, w_conv[l], w_out_b, w_r, b_r, ts)

    x1_p, h2, lg_p, u_tail, v_tail = _mixer_prompt(
        x_prompt, mod_p, n1, n2, w_in_b, w_pool_b, pscale, w_conv[l], w_out_b, w_r, b_r, h2_s)

    dest, gates, counts_f = _route(lg_p, lg_s)
    counts = counts_f[:, 0].astype(I32)
    n_rows = (n_p + n_s) * TOP_K
    gid, tid, nxt, offs, nact, n_steps = _group_metadata(counts, n_rows, EXPERT_TILE)

    dest = dest.reshape(-1)
    inv = _invert(dest)
    y4 = _experts(gid, tid, nxt, offs, nact, n_steps, inv, h2,
                  w_gate[l], b_gate[l], w_up[l], b_up[l], w_down[l], b_down[l])

    g2_s = jnp.tile(mod_s[:, 5 * d:], (ts, 1))
    y_p, y_s = _combine(gates, y4, x1_p, x1_s, mod_p, g2_s, final_norm.reshape(1, d), tp)

    y_prompt = y_p.reshape(bp, tp, d)
    y_sample = jnp.transpose(y_s.reshape(ts, bs, d), (1, 0, 2))
    new_pool_prompt = u_tail[:, POOL_HALO - n_hist:, :][None]
    new_conv_prompt = v_tail[:, CONV_HALO - n_chist:, :][None]
    new_pool_sample = jnp.transpose(newp_tm, (1, 0, 2))[None]
    new_conv_sample = jnp.transpose(newc_tm, (1, 0, 2))[None]
    return (y_prompt, y_sample, new_pool_prompt, new_conv_prompt, new_pool_sample, new_conv_sample)
```

```python
import functools

import jax
import jax.numpy as jnp
from jax import lax
from jax.experimental import pallas as pl
from jax.experimental.pallas import tpu as pltpu
from jax.experimental.pallas import tpu_sc as plsc

F32 = jnp.float32
BF16 = jnp.bfloat16
I32 = jnp.int32

POOL_WINDOWS = (2, 4, 8, 16)
POOL_HALO = 16
CONV_TAPS = 3
CONV_HALO = 8
N_EXPERTS = 32
TOP_K = 4
SWIGLU_LIMIT = 7.0
SWIGLU_ALPHA = 1.702
EPS = 1e-5
PAST_LEN = 16384

LANES = 128
SUBLANES = 8

TOKEN_TILE = 512
ROUTE_TILE = 512
COMBINE_TILE = 512
EXPERT_TILE = 512
EXPERT_SUBTILE = 128
VMEM_LIMIT = 56 * 1024 * 1024


def _rmsnorm(x, g):
    ms = jnp.mean(x * x, axis=-1, keepdims=True)
    return x * lax.rsqrt(ms + EPS) * g


def _dot(a, b):
    return jnp.dot(a, b, preferred_element_type=F32)


def _store_token_tiles(ref, val):
    rows = val.shape[0]
    for c in range(SUBLANES):
        ref[pl.ds(c, rows, stride=SUBLANES), :] = val[:, c * LANES:(c + 1) * LANES]


def _load_token_tiles(ref):
    rows = ref.shape[0] // SUBLANES
    return jnp.concatenate([ref[pl.ds(c, rows, stride=SUBLANES), :] for c in range(SUBLANES)], axis=-1)


ISSUE_GROUP = 4


def _adaln_kernel(c_ref, w_ref, b_ref, o_ref):
    c = c_ref[...]
    s = c * jax.nn.sigmoid(c)
    o_ref[...] = _dot(s.astype(BF16), w_ref[...].astype(BF16)) + b_ref[...]


def _adaln(c, w_ada, b_ada):
    rows, d = c.shape
    n = w_ada.shape[1]
    tn = 1024
    return pl.pallas_call(
        _adaln_kernel,
        out_shape=jax.ShapeDtypeStruct((rows, n), F32),
        grid=(n // tn,),
        in_specs=[
            pl.BlockSpec((rows, d), lambda j: (0, 0)),
            pl.BlockSpec((d, tn), lambda j: (0, j)),
            pl.BlockSpec((1, tn), lambda j: (0, j)),
        ],
        out_specs=pl.BlockSpec((rows, tn), lambda j: (0, j)),
        compiler_params=pltpu.CompilerParams(
            dimension_semantics=("arbitrary",), vmem_limit_bytes=VMEM_LIMIT),
        name="adaln",
    )(c, w_ada, b_ada.reshape(1, n))


def _mix_tail(x, pool_in, conv_out, g1, sc2, sh2, n2, wpool_ref, pscale, wout, wr, br):
    gw = pool_in.shape[1] // len(POOL_WINDOWS)
    mixed = [_dot(pool_in[:, g * gw:(g + 1) * gw].astype(BF16), wpool_ref[g])
             for g in range(len(POOL_WINDOWS))]
    pool_out = jnp.concatenate(mixed, axis=-1) * pscale
    mix_in = jnp.concatenate([pool_out, conv_out], axis=-1).astype(BF16)
    x1 = x + g1 * _dot(mix_in, wout)
    h2 = _rmsnorm(x1, n2) * (1.0 + sc2) + sh2
    logits_t = lax.dot_general(wr, h2.astype(BF16), (((1,), (1,)), ((), ())),
                               preferred_element_type=F32) + br
    return x1, h2, logits_t


def _mixer_prompt_kernel(x_ref, mod_ref, n1_ref, n2_ref, win_ref, wpool_ref, pscale_ref, wconv_ref,
                         wout_ref, wr_ref, br_ref, h2s_ref,
                         x1_ref, h2_ref, lg_ref, upool_ref, vconv_ref, ubuf, vbuf, *, tiles_per_seq):
    i = pl.program_id(0)

    @pl.when(i < pl.num_programs(0) - 1)
    def _():
        _mixer_prompt_tile(x_ref, mod_ref, n1_ref, n2_ref, win_ref, wpool_ref, pscale_ref, wconv_ref,
                           wout_ref, wr_ref, br_ref, x1_ref, h2_ref, lg_ref, upool_ref, vconv_ref,
                           ubuf, vbuf, lax.rem(i, tiles_per_seq), tiles_per_seq)

    @pl.when(i == pl.num_programs(0) - 1)
    def _():
        h2_ref[...] = h2s_ref[...]


def _mixer_prompt_tile(x_ref, mod_ref, n1_ref, n2_ref, win_ref, wpool_ref, pscale_ref, wconv_ref,
                       wout_ref, wr_ref, br_ref, x1_ref, h2_ref, lg_ref, upool_ref, vconv_ref,
                       ubuf, vbuf, t, tiles_per_seq):
    tt = x_ref.shape[0]
    dp = ubuf.shape[1]
    gw = dp // len(POOL_WINDOWS)

    @pl.when(t == 0)
    def _():
        ubuf[0:POOL_HALO, :] = jnp.zeros((POOL_HALO, dp), F32)
        vbuf[0:CONV_HALO, :] = jnp.zeros((CONV_HALO, dp), F32)

    x = x_ref[...]
    mod = mod_ref[...]
    sh1, sc1, g1, sh2, sc2, _ = [mod[i:i + 1, :] for i in range(6)]
    h = _rmsnorm(x, n1_ref[...]) * (1.0 + sc1) + sh1
    z = _dot(h.astype(BF16), win_ref[...])
    u, gate_b, gate_c, val = [z[:, i * dp:(i + 1) * dp] for i in range(4)]

    ubuf[POOL_HALO:POOL_HALO + tt, :] = u
    pos = lax.broadcasted_iota(I32, (tt, gw), 0) + t * tt
    pooled = []
    for g, w in enumerate(POOL_WINDOWS):
        cols = slice(g * gw, (g + 1) * gw)
        acc = u[:, cols]
        for j in range(1, w):
            acc = acc + ubuf[POOL_HALO - j:POOL_HALO - j + tt, cols]
        cnt = jnp.minimum(pos + 1, w).astype(F32)
        pooled.append(acc / cnt - u[:, cols])
    pool_in = jnp.concatenate(pooled, axis=-1)

    v = gate_c * val
    vbuf[CONV_HALO:CONV_HALO + tt, :] = v
    wc = wconv_ref[...]
    y = (wc[0:1, :] * vbuf[CONV_HALO - 2:CONV_HALO - 2 + tt, :]
         + wc[1:2, :] * vbuf[CONV_HALO - 1:CONV_HALO - 1 + tt, :]
         + wc[2:3, :] * v)
    conv_out = gate_b * y

    x1, h2, logits = _mix_tail(x, pool_in, conv_out, g1, sc2, sh2, n2_ref[...], wpool_ref,
                               pscale_ref[...], wout_ref[...], wr_ref[...], br_ref[...])
    x1_ref[...] = x1
    _store_token_tiles(h2_ref, h2)
    lg_ref[...] = logits

    ubuf[0:POOL_HALO, :] = ubuf[tt:tt + POOL_HALO, :]
    vbuf[0:CONV_HALO, :] = vbuf[tt:tt + CONV_HALO, :]

    @pl.when(t == tiles_per_seq - 1)
    def _():
        upool_ref[...] = ubuf[0:POOL_HALO, :]
        vconv_ref[...] = vbuf[0:CONV_HALO, :]


def _mixer_prompt(x, mod_p, n1, n2, w_in, w_pool, pscale, w_conv, w_out, w_r, b_r, h2_s):
    b, t, d = x.shape
    dp = w_pool.shape[0] * w_pool.shape[1]
    tt = min(TOKEN_TILE, t)
    nt = t // tt
    n_real = b * nt
    assert h2_s.shape[0] == tt * SUBLANES
    seq = lambda i: jnp.minimum(i, n_real - 1) // nt
    tile = lambda i: jnp.minimum(i, n_real - 1)
    const2 = lambda i: (0, 0)
    const3 = lambda i: (0, 0, 0)
    return pl.pallas_call(
        functools.partial(_mixer_prompt_kernel, tiles_per_seq=nt),
        out_shape=(
            jax.ShapeDtypeStruct((b * t, d), F32),
            jax.ShapeDtypeStruct(((n_real + 1) * tt * SUBLANES, LANES), F32),
            jax.ShapeDtypeStruct((N_EXPERTS, b * t), F32),
            jax.ShapeDtypeStruct((b, POOL_HALO, dp), F32),
            jax.ShapeDtypeStruct((b, CONV_HALO, dp), F32),
        ),
        grid=(n_real + 1,),
        in_specs=[
            pl.BlockSpec((None, tt, d), lambda i: (seq(i), tile(i) % nt, 0)),
            pl.BlockSpec((None, 6, d), lambda i: (seq(i), 0, 0)),
            pl.BlockSpec((1, d), const2),
            pl.BlockSpec((1, d), const2),
            pl.BlockSpec(w_in.shape, const2),
            pl.BlockSpec(w_pool.shape, const3),
            pl.BlockSpec((1, dp), const2),
            pl.BlockSpec(w_conv.shape, const2),
            pl.BlockSpec(w_out.shape, const2),
            pl.BlockSpec(w_r.shape, const2),
            pl.BlockSpec((N_EXPERTS, 1), const2),
            pl.BlockSpec(h2_s.shape, const2),
        ],
        out_specs=(
            pl.BlockSpec((tt, d), lambda i: (tile(i), 0)),
            pl.BlockSpec((tt * SUBLANES, LANES), lambda i: (i, 0)),
            pl.BlockSpec((N_EXPERTS, tt), lambda i: (0, tile(i))),
            pl.BlockSpec((None, POOL_HALO, dp), lambda i: (seq(i), 0, 0)),
            pl.BlockSpec((None, CONV_HALO, dp), lambda i: (seq(i), 0, 0)),
        ),
        scratch_shapes=[
            pltpu.VMEM((POOL_HALO + tt, dp), F32),
            pltpu.VMEM((CONV_HALO + tt, dp), F32),
        ],
        compiler_params=pltpu.CompilerParams(
            dimension_semantics=("arbitrary",), vmem_limit_bytes=VMEM_LIMIT),
        name="mixer_prompt",
    )(x, mod_p, n1, n2, w_in, w_pool, pscale, w_conv, w_out, w_r, b_r, h2_s)


def _mixer_sample_kernel(x_ref, mod_ref, pstate_ref, cstate_ref, n1_ref, n2_ref, win_ref, wpool_ref,
                         pscale_ref, wconv_ref, wout_ref, wr_ref, br_ref,
                         x1_ref, h2_ref, lg_ref, newp_ref, newc_ref, *, steps):
    nb = pstate_ref.shape[1]
    d = x_ref.shape[1]
    dp = pstate_ref.shape[2]
    gw = dp // len(POOL_WINDOWS)
    n_hist = pstate_ref.shape[0]
    n_chist = cstate_ref.shape[0]

    x = x_ref[...]
    mod = mod_ref[0:nb, :]
    rep = lambda a: jnp.concatenate([a] * steps, axis=0)
    sh1, sc1, g1, sh2, sc2, _ = [rep(mod[:, i * d:(i + 1) * d]) for i in range(6)]
    h = _rmsnorm(x, n1_ref[...]) * (1.0 + sc1) + sh1
    z = _dot(h.astype(BF16), win_ref[...])
    u, gate_b, gate_c, val = [z[:, i * dp:(i + 1) * dp] for i in range(4)]

    ext = [pstate_ref[i] for i in range(n_hist)] + [u[s * nb:(s + 1) * nb, :] for s in range(steps)]
    pooled_steps = []
    for s in range(steps):
        groups = []
        for g, w in enumerate(POOL_WINDOWS):
            cols = slice(g * gw, (g + 1) * gw)
            acc = ext[n_hist + s][:, cols]
            for j in range(1, w):
                acc = acc + ext[n_hist + s - j][:, cols]
            cnt = float(min(PAST_LEN + s + 1, w))
            groups.append(acc / cnt - ext[n_hist + s][:, cols])
        pooled_steps.append(jnp.concatenate(groups, axis=-1))
    pool_in = jnp.concatenate(pooled_steps, axis=0)

    v = gate_c * val
    vext = [cstate_ref[i] for i in range(n_chist)] + [v[s * nb:(s + 1) * nb, :] for s in range(steps)]
    wc = wconv_ref[...]
    y = jnp.concatenate(
        [wc[0:1, :] * vext[s] + wc[1:2, :] * vext[s + 1] + wc[2:3, :] * vext[s + 2] for s in range(steps)],
        axis=0)
    conv_out = gate_b * y

    x1, h2, logits = _mix_tail(x, pool_in, conv_out, g1, sc2, sh2, n2_ref[...], wpool_ref,
                               pscale_ref[...], wout_ref[...], wr_ref[...], br_ref[...])
    x1_ref[...] = x1
    _store_token_tiles(h2_ref, h2)
    lg_ref[...] = logits
    for i in range(n_hist):
        newp_ref[i] = ext[steps + i]
    for i in range(n_chist):
        newc_ref[i] = vext[steps + i]


def _mixer_sample(x_tm, mod_s, pstate_tm, cstate_tm, n1, n2, w_in, w_pool, pscale, w_conv, w_out, w_r, b_r,
                  steps):
    rows, d = x_tm.shape
    return pl.pallas_call(
        functools.partial(_mixer_sample_kernel, steps=steps),
        out_shape=(
            jax.ShapeDtypeStruct((rows, d), F32),
            jax.ShapeDtypeStruct((rows * SUBLANES, LANES), F32),
            jax.ShapeDtypeStruct((N_EXPERTS, rows), F32),
            jax.ShapeDtypeStruct(pstate_tm.shape, F32),
            jax.ShapeDtypeStruct(cstate_tm.shape, F32),
        ),
        compiler_params=pltpu.CompilerParams(vmem_limit_bytes=VMEM_LIMIT),
        name="mixer_sample",
    )(x_tm, mod_s, pstate_tm, cstate_tm, n1, n2, w_in, w_pool, pscale, w_conv, w_out, w_r, b_r)


def _route_kernel(lgp_ref, lgs_ref, dest_ref, gate_ref, cnt_ref, counts, start, before):
    ne = lgp_ref.shape[0]
    tr = before.shape[0]
    reps = tr // LANES
    n_prompt_chunks = lgp_ref.shape[1] // tr
    n_sample_chunks = lgs_ref.shape[1] // tr
    eidx = lax.broadcasted_iota(I32, (ne, tr), 0)

    def top_k(ref, c):
        work = ref[:, pl.ds(pl.multiple_of(c * tr, tr), tr)]
        top_v, onehots = [], []
        for _ in range(TOP_K):
            m = jnp.max(work, axis=0, keepdims=True)
            idx = jnp.min(jnp.where(work == m, eidx, ne), axis=0, keepdims=True)
            sel = eidx == idx
            top_v.append(m)
            onehots.append(sel)
            work = jnp.where(sel, -jnp.inf, work)
        mask = jnp.where(onehots[0] | onehots[1] | onehots[2] | onehots[3], 1.0, 0.0)
        chunk_counts = jnp.broadcast_to(jnp.sum(mask, axis=1, keepdims=True), (ne, LANES))
        return top_v, onehots, mask, chunk_counts

    def count_chunk(ref):
        def body(c, carry):
            counts[...] = counts[...] + top_k(ref, c)[3]
            return carry
        return body

    counts[...] = jnp.zeros_like(counts)
    lax.fori_loop(0, n_prompt_chunks, count_chunk(lgp_ref), 0)
    lax.fori_loop(0, n_sample_chunks, count_chunk(lgs_ref), 0)

    total = counts[...]
    hi = jnp.floor(total * (1.0 / 256.0))
    lo = total - hi * 256.0
    r = lax.broadcasted_iota(I32, (ne, ne), 0)
    col = lax.broadcasted_iota(I32, (ne, ne), 1)
    lower = jnp.where(col < r, 1.0, 0.0).astype(BF16)
    start[...] = 256.0 * _dot(lower, hi.astype(BF16)) + _dot(lower, lo.astype(BF16))
    cnt_ref[...] = total
    counts[...] = jnp.zeros_like(counts)

    r = lax.broadcasted_iota(I32, (tr, tr), 0)
    col = lax.broadcasted_iota(I32, (tr, tr), 1)
    before[...] = jnp.where(r < col, 1.0, 0.0).astype(BF16)
    gate_ref[...] = jnp.zeros_like(gate_ref)

    def place_chunk(ref, first_chunk):
        def body(c, carry):
            top_v, onehots, mask, chunk_counts = top_k(ref, c)
            base = jnp.concatenate([counts[...] + start[...]] * reps, axis=1)
            rank = _dot(mask.astype(BF16), before[...]) + base
            counts[...] = counts[...] + chunk_counts
            es = [jnp.exp(v - top_v[0]) for v in top_v]
            denom = es[0] + es[1] + es[2] + es[3]
            chunk = first_chunk + c
            cols = pl.ds(pl.multiple_of(chunk * tr, tr), tr)
            for k in range(TOP_K):
                d = jnp.sum(jnp.where(onehots[k], rank, 0.0), axis=0, keepdims=True).astype(I32)
                for j in range(reps):
                    dest_ref[chunk * reps + j, k:k + 1, :] = d[:, j * LANES:(j + 1) * LANES]
                gate_ref[k:k + 1, cols] = es[k] / denom
            return carry
        return body

    lax.fori_loop(0, n_prompt_chunks, place_chunk(lgp_ref, 0), 0)
    lax.fori_loop(0, n_sample_chunks, place_chunk(lgs_ref, n_prompt_chunks), 0)


def _route(lgt_p, lgt_s):
    ne, n_p = lgt_p.shape
    n_s = lgt_s.shape[1]
    tr = ROUTE_TILE
    return pl.pallas_call(
        _route_kernel,
        out_shape=(
            jax.ShapeDtypeStruct(((n_p + n_s) // LANES, TOP_K, LANES), I32),
            jax.ShapeDtypeStruct((SUBLANES, n_p + n_s), F32),
            jax.ShapeDtypeStruct((ne, LANES), F32),
        ),
        scratch_shapes=[pltpu.VMEM((ne, LANES), F32), pltpu.VMEM((ne, LANES), F32),
                        pltpu.VMEM((tr, tr), BF16)],
        compiler_params=pltpu.CompilerParams(vmem_limit_bytes=VMEM_LIMIT),
        name="route",
    )(lgt_p, lgt_s)


def _group_metadata(counts, n_rows, tile):
    n_tiles = n_rows // tile
    n_steps = n_tiles + N_EXPERTS - 1
    ends = jnp.cumsum(counts)
    offs = jnp.concatenate([jnp.zeros((1,), I32), ends]).astype(I32)
    first_tile = offs[:-1] // tile
    last_tile = (ends - 1) // tile
    tiles_e = jnp.where(counts > 0, last_tile - first_tile + 1, 0)
    step_end = jnp.cumsum(tiles_e)
    step_start = step_end - tiles_e
    n_active = step_end[-1]
    s = jnp.minimum(jnp.arange(n_steps, dtype=I32), n_active - 1)
    owner = ((s[:, None] >= step_start[None, :]) & (s[:, None] < step_end[None, :])).astype(I32)
    gid = jnp.sum(owner * jnp.arange(N_EXPERTS, dtype=I32)[None, :], axis=1)
    tid = jnp.sum(owner * (first_tile - step_start)[None, :], axis=1) + s
    ids = jnp.arange(N_EXPERTS, dtype=I32)
    later = (ids[None, :] > ids[:, None]) & (counts[None, :] > 0)
    next_e = jnp.min(jnp.where(later, ids[None, :], N_EXPERTS), axis=1)
    next_e = jnp.where(next_e == N_EXPERTS, -1, next_e)
    nxt = jnp.sum(owner * next_e[None, :], axis=1)
    return gid, tid, nxt, offs, n_active.reshape(1).astype(I32), n_steps


SC_LANES = 16
SC_INDEX_BATCH = 128


def _invert(dest):
    r = dest.shape[0]
    mesh = plsc.VectorSubcoreMesh(core_axis_name="core", subcore_axis_name="subcore")
    per = r // mesh.num_subcores
    assert per * mesh.num_subcores == r and per % SC_INDEX_BATCH == 0

    @functools.partial(
        pl.kernel, mesh=mesh, out_type=jax.ShapeDtypeStruct((r,), I32),
        scratch_types=[pltpu.VMEM_SHARED((r,), I32), pltpu.VMEM((per,), I32), pltpu.VMEM((per,), I32)],
        compiler_params=pltpu.CompilerParams(needs_layout_passes=False),
        name="invert",
    )
    def invert(dest_hbm, inv_hbm, table, idx, ids):
        @pl.when(lax.axis_index("core") == 0)
        def _():
            base = lax.axis_index("subcore") * per
            pltpu.sync_copy(dest_hbm.at[pl.ds(base, per)], idx)
            lane = lax.iota(I32, SC_LANES)

            @pl.loop(0, per // SC_LANES)
            def _(i):
                p = base + i * SC_LANES + lane
                tok = lax.shift_right_logical(p, 9) * LANES + (p & (LANES - 1))
                slot = lax.shift_right_logical(p, 7) & (TOP_K - 1)
                ids[pl.ds(i * SC_LANES, SC_LANES)] = tok * TOP_K + slot

            @pl.loop(0, per // SC_INDEX_BATCH)
            def _(j):
                span = pl.ds(j * SC_INDEX_BATCH, SC_INDEX_BATCH)
                pltpu.sync_copy(ids.at[span], table.at[idx.at[span]])

            plsc.subcore_barrier()
            pltpu.sync_copy(table.at[pl.ds(base, per)], inv_hbm.at[pl.ds(base, per)])

    return invert(dest)


def _experts_kernel(gid_ref, tid_ref, nxt_ref, offs_ref, nact_ref,
                    dstp_ref, dst0_ref, src0_ref, src1_ref, src2_ref, bg_ref, bu_ref, bd_ref,
                    h2_hbm, wg_hbm, wu_hbm, wd_hbm, y4_hbm,
                    wg_b, wu_b, wd_b, wg_f, wu_f, wd_f, relay, rows, ybuf, sems, row_sems, y_sems):
    s = pl.program_id(0)
    n_steps = pl.num_programs(0)
    tile_rows = rows.shape[1]
    tm = tile_rows // SUBLANES
    landing = ((wg_hbm, wg_f, wg_b), (wu_hbm, wu_f, wu_b), (wd_hbm, wd_f, wd_b))

    def fetch(e):
        for j, (hbm, land, _) in enumerate(landing):
            pltpu.make_async_copy(hbm.at[e], land, sems.at[j]).start()

    def row_in(idx_ref, into, r):
        src = h2_hbm.at[pl.ds(pl.multiple_of(idx_ref[r], SUBLANES), SUBLANES)]
        dst = rows.at[into, pl.ds(pl.multiple_of(r * SUBLANES, SUBLANES), SUBLANES)]
        return pltpu.make_async_copy(src, dst, row_sems.at[into])

    def row_out(idx_ref, frm, r):
        src = ybuf.at[frm, pl.ds(pl.multiple_of(r * SUBLANES, SUBLANES), SUBLANES)]
        dst = y4_hbm.at[pl.ds(pl.multiple_of(idx_ref[r], SUBLANES), SUBLANES)]
        return pltpu.make_async_copy(src, dst, y_sems.at[frm])

    def in_line(make, idx_ref, buf, first=0, count=None):
        for r in range(first, tm if count is None else first + count):
            make(idx_ref, buf, r).start(priority=r % 2)

    def in_loop(make, idx_ref, buf):
        def group(g, carry):
            for j in range(ISSUE_GROUP):
                make(idx_ref, buf, g * ISSUE_GROUP + j).start(priority=j % 2)
            return carry
        lax.fori_loop(0, tm // ISSUE_GROUP, group, 0)

    def wait_rows(buf):
        pltpu.make_async_copy(h2_hbm.at[pl.ds(0, tile_rows)], rows.at[buf], row_sems.at[buf]).wait()

    def wait_y(buf):
        pltpu.make_async_copy(ybuf.at[buf], y4_hbm.at[pl.ds(0, tile_rows)], y_sems.at[buf]).wait()

    @pl.when(s < nact_ref[0])
    def _():
        e = gid_ref[s]
        m = tid_ref[s]
        cur = lax.rem(m, 3)
        before = lax.rem(m + 2, 3)
        last = nact_ref[0] - 1
        new_tile = (s == 0) | (tid_ref[jnp.maximum(s - 1, 0)] != m)
        next_differs = (s < last) & (tid_ref[jnp.minimum(s + 1, n_steps - 1)] != m)

        @pl.when(s == 0)
        def _():
            fetch(e)
            in_loop(row_in, src0_ref, 0)
            in_loop(row_in, src1_ref, 1)
            ybuf[2] = jnp.zeros(ybuf.shape[1:], F32)

        @pl.when((s == 0) | (gid_ref[jnp.maximum(s - 1, 0)] != e))
        def _():
            for j, (hbm, land, half) in enumerate(landing):
                pltpu.make_async_copy(hbm.at[e], land, sems.at[j]).wait()
                half[...] = land[...].astype(BF16)

            @pl.when(nxt_ref[s] >= 0)
            def _():
                fetch(nxt_ref[s])

        @pl.when(new_tile)
        def _():
            wait_rows(cur)

            @pl.when(m >= 2)
            def _():
                wait_y(cur)

        def ffn(load_x, after_gate=lambda: None):
            g = _dot(load_x(), wg_b[...]) + bg_ref[pl.ds(e, 1), :]
            after_gate()
            u = _dot(load_x(), wu_b[...]) + bu_ref[pl.ds(e, 1), :]
            g = jnp.minimum(g, SWIGLU_LIMIT)
            u = jnp.clip(u, -SWIGLU_LIMIT, SWIGLU_LIMIT)
            glu = g * jax.nn.sigmoid(SWIGLU_ALPHA * g)
            return _dot(((u + 1.0) * glu).astype(BF16), wd_b[...]) + bd_ref[pl.ds(e, 1), :]

        lo = offs_ref[e]
        hi = offs_ref[e + 1]
        whole_tile = (lo <= m * tm) & (hi >= (m + 1) * tm)

        @pl.when(whole_tile)
        def _():
            pinned = tm // 2
            y = ffn(lambda: _load_token_tiles(rows.at[cur]).astype(BF16),
                    lambda: in_line(row_in, src2_ref, before, 0, pinned))
            in_line(row_in, src2_ref, before, pinned)
            in_line(row_out, dstp_ref, before)
            _store_token_tiles(ybuf.at[cur], y)

        @pl.when(jnp.logical_not(whole_tile))
        def _():
            sub = relay.shape[0] // SUBLANES
            for j in range(tm // sub):
                first = m * tm + j * sub

                @pl.when((lo < first + sub) & (hi > first))
                def _():
                    span = pl.ds(j * sub * SUBLANES, sub * SUBLANES)
                    _store_token_tiles(relay, ffn(lambda: _load_token_tiles(rows.at[cur, span]).astype(BF16)))
                    row = first + lax.shift_right_logical(lax.broadcasted_iota(I32, relay.shape, 0), 3)
                    pltpu.store(ybuf.at[cur, span], relay[...], mask=(row >= lo) & (row < hi))

            @pl.when(next_differs | (s == last))
            def _():
                in_loop(row_in, src2_ref, before)
                in_loop(row_out, dstp_ref, before)

        @pl.when(s == last)
        def _():
            in_loop(row_out, dst0_ref, cur)
            wait_rows(lax.rem(m + 1, 3))
            wait_rows(before)
            for buf in range(3):
                wait_y(buf)


def _experts(gid, tid, nxt, offs, nact, n_steps, inv, h2, wg, bg, wu, bu, wd, bd):
    ne, d, f = wg.shape
    tm = EXPERT_TILE
    n_tokens = h2.shape[0] // SUBLANES
    n_tiles = inv.shape[0] // tm
    assert n_tiles >= 2
    slot_rows = (n_tokens + tm // TOP_K) * SUBLANES
    inv_ext = jnp.concatenate([n_tokens * TOP_K + jnp.arange(tm, dtype=I32), inv])
    tok = lax.shift_right_logical(inv_ext, TOP_K.bit_length() - 1)
    src_row = tok * SUBLANES
    dst_row = (inv_ext & (TOP_K - 1)) * slot_rows + src_row
    whole = lambda s, gid, tid, nxt, offs, nact: (0, 0)

    def order_of(k):
        return pl.BlockSpec(
            (tm,), lambda s, gid, tid, nxt, offs, nact: (jnp.minimum(tid[s] + k, n_tiles - 1) + 1,),
            memory_space=pltpu.SMEM)

    grid_spec = pltpu.PrefetchScalarGridSpec(
        num_scalar_prefetch=5,
        grid=(n_steps,),
        in_specs=[
            order_of(-1), order_of(0), order_of(0), order_of(1), order_of(2),
            pl.BlockSpec((ne, f), whole),
            pl.BlockSpec((ne, f), whole),
            pl.BlockSpec((ne, d), whole),
            pl.BlockSpec(memory_space=pl.ANY),
            pl.BlockSpec(memory_space=pl.ANY),
            pl.BlockSpec(memory_space=pl.ANY),
            pl.BlockSpec(memory_space=pl.ANY),
        ],
        out_specs=pl.BlockSpec(memory_space=pl.ANY),
        scratch_shapes=[
            pltpu.VMEM((d, f), BF16), pltpu.VMEM((d, f), BF16), pltpu.VMEM((f, d), BF16),
            pltpu.VMEM((d, f), F32), pltpu.VMEM((d, f), F32), pltpu.VMEM((f, d), F32),
            pltpu.VMEM((EXPERT_SUBTILE * SUBLANES, LANES), F32),
            pltpu.VMEM((3, tm * SUBLANES, LANES), F32),
            pltpu.VMEM((3, tm * SUBLANES, LANES), F32),
            pltpu.SemaphoreType.DMA((3,)),
            pltpu.SemaphoreType.DMA((3,)),
            pltpu.SemaphoreType.DMA((3,)),
        ],
    )
    y4 = pl.pallas_call(
        _experts_kernel,
        out_shape=jax.ShapeDtypeStruct((TOP_K * slot_rows, LANES), F32),
        grid_spec=grid_spec,
        compiler_params=pltpu.CompilerParams(
            dimension_semantics=("arbitrary",), vmem_limit_bytes=VMEM_LIMIT),
        name="experts",
    )(gid, tid, nxt, offs, nact, dst_row, dst_row, src_row, src_row, src_row, bg, bu, bd, h2, wg, wu, wd)
    return y4.reshape(TOP_K, slot_rows, LANES)


def _combine_kernel(gate_ref, y4_ref, x1p_ref, x1s_ref, modp_ref, g2s_ref, fn_ref, outp_ref, outs_ref,
                    *, n_prompt_tiles):
    i = pl.program_id(0)
    tm = x1p_ref.shape[0]
    gates = gate_ref[...].T
    cols = []
    for c in range(SUBLANES):
        acc = gates[:, 0:1] * y4_ref[0, pl.ds(c, tm, stride=SUBLANES), :]
        for k in range(1, TOP_K):
            acc = acc + gates[:, k:k + 1] * y4_ref[k, pl.ds(c, tm, stride=SUBLANES), :]
        cols.append(acc)
    ffn = jnp.concatenate(cols, axis=-1)

    @pl.when(i < n_prompt_tiles)
    def _():
        g2 = modp_ref[...][5:6, :]
        outp_ref[...] = _rmsnorm(x1p_ref[...] + g2 * ffn, fn_ref[...])

    @pl.when(i >= n_prompt_tiles)
    def _():
        outs_ref[...] = _rmsnorm(x1s_ref[...] + g2s_ref[...] * ffn, fn_ref[...])


def _combine(gates, y4, x1_p, x1_s, mod_p, g2_s, final_norm, tokens_per_seq):
    n_p, d = x1_p.shape
    n_s = x1_s.shape[0]
    tm = COMBINE_TILE
    npt, nst = n_p // tm, n_s // tm
    tiles_per_seq = tokens_per_seq // tm
    pmap = lambda i: (jnp.minimum(i, npt - 1), 0)
    smap = lambda i: (jnp.maximum(i - npt, 0), 0)
    return pl.pallas_call(
        functools.partial(_combine_kernel, n_prompt_tiles=npt),
        out_shape=(jax.ShapeDtypeStruct((n_p, d), F32), jax.ShapeDtypeStruct((n_s, d), F32)),
        grid=(npt + nst,),
        in_specs=[
            pl.BlockSpec((SUBLANES, tm), lambda i: (0, i)),
            pl.BlockSpec((TOP_K, tm * SUBLANES, LANES), lambda i: (0, i, 0)),
            pl.BlockSpec((tm, d), pmap),
            pl.BlockSpec((tm, d), smap),
            pl.BlockSpec((None, 6, d), lambda i: (jnp.minimum(i, npt - 1) // tiles_per_seq, 0, 0)),
            pl.BlockSpec((tm, d), smap),
            pl.BlockSpec((1, d), lambda i: (0, 0)),
        ],
        out_specs=(pl.BlockSpec((tm, d), pmap), pl.BlockSpec((tm, d), smap)),
        compiler_params=pltpu.CompilerParams(
            dimension_semantics=("arbitrary",), vmem_limit_bytes=VMEM_LIMIT),
        name="combine",
    )(gates, y4, x1_p, x1_s, mod_p, g2_s, final_norm)


def kernel(x_prompt, x_sample, state_pool, state_conv, c_prompt, c_sample, norm1, norm2, w_ada, b_ada,
           w_in, w_pool, pool_scale, w_conv, w_out, w_router, b_router, w_gate, b_gate, w_up, b_up,
           w_down, b_down, final_norm):
    depth = norm1.shape[0]
    assert depth == 1, "single-layer step"
    bp, tp, d = x_prompt.shape
    bs, ts, _ = x_sample.shape
    dp = state_pool.shape[-1]
    n_hist = state_pool.shape[2]
    n_chist = state_conv.shape[2]
    n_p, n_s = bp * tp, bs * ts
    assert d == SUBLANES * LANES, "token-tile layout assumes one vreg tile per token row"
    assert tp % TOKEN_TILE == 0 and n_s == TOKEN_TILE, "the sample group fills exactly one token tile"
    assert tp % COMBINE_TILE == 0 and n_s % COMBINE_TILE == 0
    assert (n_p + n_s) % ROUTE_TILE == 0 and ((n_p + n_s) * TOP_K) % EXPERT_TILE == 0

    l = 0
    n1 = norm1[l].reshape(1, d)
    n2 = norm2[l].reshape(1, d)
    w_in_b = w_in[l].astype(BF16)
    w_pool_b = w_pool[l].astype(BF16)
    w_out_b = w_out[l].astype(BF16)
    pscale = pool_scale[l].reshape(1, dp)
    w_r = w_router[l].T.astype(BF16)
    b_r = b_router[l].reshape(N_EXPERTS, 1)

    mod = _adaln(jnp.concatenate([c_sample, c_prompt], axis=0), w_ada[l], b_ada[l])
    mod_s = mod[:bs]
    mod_p = mod[bs:].reshape(bp, 6, d)

    xs_tm = jnp.transpose(x_sample, (1, 0, 2)).reshape(n_s, d)
    ps_tm = jnp.transpose(state_pool[l], (1, 0, 2))
    cs_tm = jnp.transpose(state_conv[l], (1, 0, 2))
    x1_s, h2_s, lg_s, newp_tm, newc_tm = _mixer_sample(
        xs_tm, mod, ps_tm, cs_tm, n1, n2, w_in_b, w_pool_b, pscale, w_conv[l], w_out_b, w_r, b_r, ts)

    x1_p, h2, lg_p, u_tail, v_tail = _mixer_prompt(
        x_prompt, mod_p, n1, n2, w_in_b, w_pool_b, pscale, w_conv[l], w_out_b, w_r, b_r, h2_s)

    dest, gates, counts_f = _route(lg_p, lg_s)
    counts = counts_f[:, 0].astype(I32)
    n_rows = (n_p + n_s) * TOP_K
    gid, tid, nxt, offs, nact, n_steps = _group_metadata(counts, n_rows, EXPERT_TILE)

    dest = dest.reshape(-1)
    inv = _invert(dest)
    y4 = _experts(gid, tid, nxt, offs, nact, n_steps, inv, h2,
                  w_gate[l], b_gate[l], w_up[l], b_up[l], w_down[l], b_down[l])

    g2_s = jnp.tile(mod_s[:, 5 * d:], (ts, 1))
    y_p, y_s = _combine(gates, y4, x1_p, x1_s, mod_p, g2_s, final_norm.reshape(1, d), tp)

    y_prompt = y_p.reshape(bp, tp, d)
    y_sample = jnp.transpose(y_s.reshape(ts, bs, d), (1, 0, 2))
    new_pool_prompt = u_tail[:, POOL_HALO - n_hist:, :][None]
    new_conv_prompt = v_tail[:, CONV_HALO - n_chist:, :][None]
    new_pool_sample = jnp.transpose(newp_tm, (1, 0, 2))[None]
    new_conv_sample = jnp.transpose(newc_tm, (1, 0, 2))[None]
    return (y_prompt, y_sample, new_pool_prompt, new_conv_prompt, new_pool_sample, new_conv_sample)
```

```python
import functools

import jax
import jax.numpy as jnp
from jax import lax
from jax.experimental import pallas as pl
from jax.experimental.pallas import tpu as pltpu
from jax.experimental.pallas import tpu_sc as plsc

F32 = jnp.float32
BF16 = jnp.bfloat16
I32 = jnp.int32

POOL_WINDOWS = (2, 4, 8, 16)
POOL_HALO = 16
CONV_TAPS = 3
CONV_HALO = 8
N_EXPERTS = 32
TOP_K = 4
SWIGLU_LIMIT = 7.0
SWIGLU_ALPHA = 1.702
EPS = 1e-5
PAST_LEN = 16384

LANES = 128
SUBLANES = 8

TOKEN_TILE = 512
ROUTE_TILE = 512
COMBINE_TILE = 512
EXPERT_TILE = 512
EXPERT_SUBTILE = 128
VMEM_LIMIT = 56 * 1024 * 1024


def _rmsnorm(x, g):
    ms = jnp.mean(x * x, axis=-1, keepdims=True)
    return x * lax.rsqrt(ms + EPS) * g


def _dot(a, b):
    return jnp.dot(a, b, preferred_element_type=F32)


def _store_token_tiles(ref, val):
    rows = val.shape[0]
    for c in range(SUBLANES):
        ref[pl.ds(c, rows, stride=SUBLANES), :] = val[:, c * LANES:(c + 1) * LANES]


def _load_token_tiles(ref):
    rows = ref.shape[0] // SUBLANES
    return jnp.concatenate([ref[pl.ds(c, rows, stride=SUBLANES), :] for c in range(SUBLANES)], axis=-1)


ISSUE_GROUP = 4


def _adaln_kernel(c_ref, w_ref, b_ref, o_ref):
    c = c_ref[...]
    s = c * jax.nn.sigmoid(c)
    o_ref[...] = _dot(s.astype(BF16), w_ref[...].astype(BF16)) + b_ref[...]


def _adaln(c, w_ada, b_ada):
    rows, d = c.shape
    n = w_ada.shape[1]
    tn = 1024
    return pl.pallas_call(
        _adaln_kernel,
        out_shape=jax.ShapeDtypeStruct((rows, n), F32),
        grid=(n // tn,),
        in_specs=[
            pl.BlockSpec((rows, d), lambda j: (0, 0)),
            pl.BlockSpec((d, tn), lambda j: (0, j)),
            pl.BlockSpec((1, tn), lambda j: (0, j)),
        ],
        out_specs=pl.BlockSpec((rows, tn), lambda j: (0, j)),
        compiler_params=pltpu.CompilerParams(
            dimension_semantics=("arbitrary",), vmem_limit_bytes=VMEM_LIMIT),
        name="adaln",
    )(c, w_ada, b_ada.reshape(1, n))


def _mix_tail(x, pool_in, conv_out, g1, sc2, sh2, n2, wpool_ref, pscale, wout, wr, br):
    gw = pool_in.shape[1] // len(POOL_WINDOWS)
    mixed = [_dot(pool_in[:, g * gw:(g + 1) * gw].astype(BF16), wpool_ref[g])
             for g in range(len(POOL_WINDOWS))]
    pool_out = jnp.concatenate(mixed, axis=-1) * pscale
    mix_in = jnp.concatenate([pool_out, conv_out], axis=-1).astype(BF16)
    x1 = x + g1 * _dot(mix_in, wout)
    h2 = _rmsnorm(x1, n2) * (1.0 + sc2) + sh2
    logits_t = lax.dot_general(wr, h2.astype(BF16), (((1,), (1,)), ((), ())),
                               preferred_element_type=F32) + br
    return x1, h2, logits_t


def _mixer_prompt_kernel(x_ref, mod_ref, n1_ref, n2_ref, win_ref, wpool_ref, pscale_ref, wconv_ref,
                         wout_ref, wr_ref, br_ref, h2s_ref,
                         x1_ref, h2_ref, lg_ref, upool_ref, vconv_ref, ubuf, vbuf, *, tiles_per_seq):
    i = pl.program_id(0)

    @pl.when(i < pl.num_programs(0) - 1)
    def _():
        _mixer_prompt_tile(x_ref, mod_ref, n1_ref, n2_ref, win_ref, wpool_ref, pscale_ref, wconv_ref,
                           wout_ref, wr_ref, br_ref, x1_ref, h2_ref, lg_ref, upool_ref, vconv_ref,
                           ubuf, vbuf, lax.rem(i, tiles_per_seq), tiles_per_seq)

    @pl.when(i == pl.num_programs(0) - 1)
    def _():
        h2_ref[...] = h2s_ref[...]


def _mixer_prompt_tile(x_ref, mod_ref, n1_ref, n2_ref, win_ref, wpool_ref, pscale_ref, wconv_ref,
                       wout_ref, wr_ref, br_ref, x1_ref, h2_ref, lg_ref, upool_ref, vconv_ref,
                       ubuf, vbuf, t, tiles_per_seq):
    tt = x_ref.shape[0]
    dp = ubuf.shape[1]
    gw = dp // len(POOL_WINDOWS)

    @pl.when(t == 0)
    def _():
        ubuf[0:POOL_HALO, :] = jnp.zeros((POOL_HALO, dp), F32)
        vbuf[0:CONV_HALO, :] = jnp.zeros((CONV_HALO, dp), F32)

    x = x_ref[...]
    mod = mod_ref[...]
    sh1, sc1, g1, sh2, sc2, _ = [mod[i:i + 1, :] for i in range(6)]
    h = _rmsnorm(x, n1_ref[...]) * (1.0 + sc1) + sh1
    z = _dot(h.astype(BF16), win_ref[...])
    u, gate_b, gate_c, val = [z[:, i * dp:(i + 1) * dp] for i in range(4)]

    ubuf[POOL_HALO:POOL_HALO + tt, :] = u
    pos = lax.broadcasted_iota(I32, (tt, gw), 0) + t * tt
    pooled = []
    for g, w in enumerate(POOL_WINDOWS):
        cols = slice(g * gw, (g + 1) * gw)
        acc = u[:, cols]
        for j in range(1, w):
            acc = acc + ubuf[POOL_HALO - j:POOL_HALO - j + tt, cols]
        cnt = jnp.minimum(pos + 1, w).astype(F32)
        pooled.append(acc / cnt - u[:, cols])
    pool_in = jnp.concatenate(pooled, axis=-1)

    v = gate_c * val
    vbuf[CONV_HALO:CONV_HALO + tt, :] = v
    wc = wconv_ref[...]
    y = (wc[0:1, :] * vbuf[CONV_HALO - 2:CONV_HALO - 2 + tt, :]
         + wc[1:2, :] * vbuf[CONV_HALO - 1:CONV_HALO - 1 + tt, :]
         + wc[2:3, :] * v)
    conv_out = gate_b * y

    x1, h2, logits = _mix_tail(x, pool_in, conv_out, g1, sc2, sh2, n2_ref[...], wpool_ref,
                               pscale_ref[...], wout_ref[...], wr_ref[...], br_ref[...])
    x1_ref[...] = x1
    _store_token_tiles(h2_ref, h2)
    lg_ref[...] = logits

    ubuf[0:POOL_HALO, :] = ubuf[tt:tt + POOL_HALO, :]
    vbuf[0:CONV_HALO, :] = vbuf[tt:tt + CONV_HALO, :]

    @pl.when(t == tiles_per_seq - 1)
    def _():
        upool_ref[...] = ubuf[0:POOL_HALO, :]
        vconv_ref[...] = vbuf[0:CONV_HALO, :]


def _mixer_prompt(x, mod_p, n1, n2, w_in, w_pool, pscale, w_conv, w_out, w_r, b_r, h2_s):
    b, t, d = x.shape
    dp = w_pool.shape[0] * w_pool.shape[1]
    tt = min(TOKEN_TILE, t)
    nt = t // tt
    n_real = b * nt
    assert h2_s.shape[0] == tt * SUBLANES
    seq = lambda i: jnp.minimum(i, n_real - 1) // nt
    tile = lambda i: jnp.minimum(i, n_real - 1)
    const2 = lambda i: (0, 0)
    const3 = lambda i: (0, 0, 0)
    return pl.pallas_call(
        functools.partial(_mixer_prompt_kernel, tiles_per_seq=nt),
        out_shape=(
            jax.ShapeDtypeStruct((b * t, d), F32),
            jax.ShapeDtypeStruct(((n_real + 1) * tt * SUBLANES, LANES), F32),
            jax.ShapeDtypeStruct((N_EXPERTS, b * t), F32),
            jax.ShapeDtypeStruct((b, POOL_HALO, dp), F32),
            jax.ShapeDtypeStruct((b, CONV_HALO, dp), F32),
        ),
        grid=(n_real + 1,),
        in_specs=[
            pl.BlockSpec((None, tt, d), lambda i: (seq(i), tile(i) % nt, 0)),
            pl.BlockSpec((None, 6, d), lambda i: (seq(i), 0, 0)),
            pl.BlockSpec((1, d), const2),
            pl.BlockSpec((1, d), const2),
            pl.BlockSpec(w_in.shape, const2),
            pl.BlockSpec(w_pool.shape, const3),
            pl.BlockSpec((1, dp), const2),
            pl.BlockSpec(w_conv.shape, const2),
            pl.BlockSpec(w_out.shape, const2),
            pl.BlockSpec(w_r.shape, const2),
            pl.BlockSpec((N_EXPERTS, 1), const2),
            pl.BlockSpec(h2_s.shape, const2),
        ],
        out_specs=(
            pl.BlockSpec((tt, d), lambda i: (tile(i), 0)),
            pl.BlockSpec((tt * SUBLANES, LANES), lambda i: (i, 0)),
            pl.BlockSpec((N_EXPERTS, tt), lambda i: (0, tile(i))),
            pl.BlockSpec((None, POOL_HALO, dp), lambda i: (seq(i), 0, 0)),
            pl.BlockSpec((None, CONV_HALO, dp), lambda i: (seq(i), 0, 0)),
        ),
        scratch_shapes=[
            pltpu.VMEM((POOL_HALO + tt, dp), F32),
            pltpu.VMEM((CONV_HALO + tt, dp), F32),
        ],
        compiler_params=pltpu.CompilerParams(
            dimension_semantics=("arbitrary",), vmem_limit_bytes=VMEM_LIMIT),
        name="mixer_prompt",
    )(x, mod_p, n1, n2, w_in, w_pool, pscale, w_conv, w_out, w_r, b_r, h2_s)


def _mixer_sample_kernel(x_ref, mod_ref, pstate_ref, cstate_ref, n1_ref, n2_ref, win_ref, wpool_ref,
                         pscale_ref, wconv_ref, wout_ref, wr_ref, br_ref,
                         x1_ref, h2_ref, lg_ref, newp_ref, newc_ref, *, steps):
    nb = pstate_ref.shape[1]
    d = x_ref.shape[1]
    dp = pstate_ref.shape[2]
    gw = dp // len(POOL_WINDOWS)
    n_hist = pstate_ref.shape[0]
    n_chist = cstate_ref.shape[0]

    x = x_ref[...]
    mod = mod_ref[0:nb, :]
    rep = lambda a: jnp.concatenate([a] * steps, axis=0)
    sh1, sc1, g1, sh2, sc2, _ = [rep(mod[:, i * d:(i + 1) * d]) for i in range(6)]
    h = _rmsnorm(x, n1_ref[...]) * (1.0 + sc1) + sh1
    z = _dot(h.astype(BF16), win_ref[...])
    u, gate_b, gate_c, val = [z[:, i * dp:(i + 1) * dp] for i in range(4)]

    ext = [pstate_ref[i] for i in range(n_hist)] + [u[s * nb:(s + 1) * nb, :] for s in range(steps)]
    pooled_steps = []
    for s in range(steps):
        groups = []
        for g, w in enumerate(POOL_WINDOWS):
            cols = slice(g * gw, (g + 1) * gw)
            acc = ext[n_hist + s][:, cols]
            for j in range(1, w):
                acc = acc + ext[n_hist + s - j][:, cols]
            cnt = float(min(PAST_LEN + s + 1, w))
            groups.append(acc / cnt - ext[n_hist + s][:, cols])
        pooled_steps.append(jnp.concatenate(groups, axis=-1))
    pool_in = jnp.concatenate(pooled_steps, axis=0)

    v = gate_c * val
    vext = [cstate_ref[i] for i in range(n_chist)] + [v[s * nb:(s + 1) * nb, :] for s in range(steps)]
    wc = wconv_ref[...]
    y = jnp.concatenate(
        [wc[0:1, :] * vext[s] + wc[1:2, :] * vext[s + 1] + wc[2:3, :] * vext[s + 2] for s in range(steps)],
        axis=0)
    conv_out = gate_b * y

    x1, h2, logits = _mix_tail(x, pool_in, conv_out, g1, sc2, sh2, n2_ref[...], wpool_ref,
                               pscale_ref[...], wout_ref[...], wr_ref[...], br_ref[...])
    x1_ref[...] = x1
    _store_token_tiles(h2_ref, h2)
    lg_ref[...] = logits
    for i in range(n_hist):
        newp_ref[i] = ext[steps + i]
    for i in range(n_chist):
        newc_ref[i] = vext[steps + i]


def _mixer_sample(x_tm, mod_s, pstate_tm, cstate_tm, n1, n2, w_in, w_pool, pscale, w_conv, w_out, w_r, b_r,
                  steps):
    rows, d = x_tm.shape
    return pl.pallas_call(
        functools.partial(_mixer_sample_kernel, steps=steps),
        out_shape=(
            jax.ShapeDtypeStruct((rows, d), F32),
            jax.ShapeDtypeStruct((rows * SUBLANES, LANES), F32),
            jax.ShapeDtypeStruct((N_EXPERTS, rows), F32),
            jax.ShapeDtypeStruct(pstate_tm.shape, F32),
            jax.ShapeDtypeStruct(cstate_tm.shape, F32),
        ),
        compiler_params=pltpu.CompilerParams(vmem_limit_bytes=VMEM_LIMIT),
        name="mixer_sample",
    )(x_tm, mod_s, pstate_tm, cstate_tm, n1, n2, w_in, w_pool, pscale, w_conv, w_out, w_r, b_r)


def _route_kernel(lgp_ref, lgs_ref, dest_ref, gate_ref, cnt_ref, counts, start, before):
    ne = lgp_ref.shape[0]
    tr = before.shape[0]
    reps = tr // LANES
    n_prompt_chunks = lgp_ref.shape[1] // tr
    n_sample_chunks = lgs_ref.shape[1] // tr
    eidx = lax.broadcasted_iota(I32, (ne, tr), 0)

    def top_k(ref, c):
        work = ref[:, pl.ds(pl.multiple_of(c * tr, tr), tr)]
        top_v, onehots = [], []
        for _ in range(TOP_K):
            m = jnp.max(work, axis=0, keepdims=True)
            idx = jnp.min(jnp.where(work == m, eidx, ne), axis=0, keepdims=True)
            sel = eidx == idx
            top_v.append(m)
            onehots.append(sel)
            work = jnp.where(sel, -jnp.inf, work)
        mask = jnp.where(onehots[0] | onehots[1] | onehots[2] | onehots[3], 1.0, 0.0)
        chunk_counts = jnp.broadcast_to(jnp.sum(mask, axis=1, keepdims=True), (ne, LANES))
        return top_v, onehots, mask, chunk_counts

    def count_chunk(ref):
        def body(c, carry):
            counts[...] = counts[...] + top_k(ref, c)[3]
            return carry
        return body

    counts[...] = jnp.zeros_like(counts)
    lax.fori_loop(0, n_prompt_chunks, count_chunk(lgp_ref), 0)
    lax.fori_loop(0, n_sample_chunks, count_chunk(lgs_ref), 0)

    total = counts[...]
    hi = jnp.floor(total * (1.0 / 256.0))
    lo = total - hi * 256.0
    r = lax.broadcasted_iota(I32, (ne, ne), 0)
    col = lax.broadcasted_iota(I32, (ne, ne), 1)
    lower = jnp.where(col < r, 1.0, 0.0).astype(BF16)
    start[...] = 256.0 * _dot(lower, hi.astype(BF16)) + _dot(lower, lo.astype(BF16))
    cnt_ref[...] = total
    counts[...] = jnp.zeros_like(counts)

    r = lax.broadcasted_iota(I32, (tr, tr), 0)
    col = lax.broadcasted_iota(I32, (tr, tr), 1)
    before[...] = jnp.where(r < col, 1.0, 0.0).astype(BF16)
    gate_ref[...] = jnp.zeros_like(gate_ref)

    def place_chunk(ref, first_chunk):
        def body(c, carry):
            top_v, onehots, mask, chunk_counts = top_k(ref, c)
            base = jnp.concatenate([counts[...] + start[...]] * reps, axis=1)
            rank = _dot(mask.astype(BF16), before[...]) + base
            counts[...] = counts[...] + chunk_counts
            es = [jnp.exp(v - top_v[0]) for v in top_v]
            denom = es[0] + es[1] + es[2] + es[3]
            chunk = first_chunk + c
            cols = pl.ds(pl.multiple_of(chunk * tr, tr), tr)
            for k in range(TOP_K):
                d = jnp.sum(jnp.where(onehots[k], rank, 0.0), axis=0, keepdims=True).astype(I32)
                for j in range(reps):
                    dest_ref[chunk * reps + j, k:k + 1, :] = d[:, j * LANES:(j + 1) * LANES]
                gate_ref[k:k + 1, cols] = es[k] / denom
            return carry
        return body

    lax.fori_loop(0, n_prompt_chunks, place_chunk(lgp_ref, 0), 0)
    lax.fori_loop(0, n_sample_chunks, place_chunk(lgs_ref, n_prompt_chunks), 0)


def _route(lgt_p, lgt_s):
    ne, n_p = lgt_p.shape
    n_s = lgt_s.shape[1]
    tr = ROUTE_TILE
    return pl.pallas_call(
        _route_kernel,
        out_shape=(
            jax.ShapeDtypeStruct(((n_p + n_s) // LANES, TOP_K, LANES), I32),
            jax.ShapeDtypeStruct((SUBLANES, n_p + n_s), F32),
            jax.ShapeDtypeStruct((ne, LANES), F32),
        ),
        scratch_shapes=[pltpu.VMEM((ne, LANES), F32), pltpu.VMEM((ne, LANES), F32),
                        pltpu.VMEM((tr, tr), BF16)],
        compiler_params=pltpu.CompilerParams(vmem_limit_bytes=VMEM_LIMIT),
        name="route",
    )(lgt_p, lgt_s)


def _group_metadata(counts, n_rows, tile):
    n_tiles = n_rows // tile
    n_steps = n_tiles + N_EXPERTS - 1
    ends = jnp.cumsum(counts)
    offs = jnp.concatenate([jnp.zeros((1,), I32), ends]).astype(I32)
    first_tile = offs[:-1] // tile
    last_tile = (ends - 1) // tile
    tiles_e = jnp.where(counts > 0, last_tile - first_tile + 1, 0)
    step_end = jnp.cumsum(tiles_e)
    step_start = step_end - tiles_e
    n_active = step_end[-1]
    s = jnp.minimum(jnp.arange(n_steps, dtype=I32), n_active - 1)
    owner = ((s[:, None] >= step_start[None, :]) & (s[:, None] < step_end[None, :])).astype(I32)
    gid = jnp.sum(owner * jnp.arange(N_EXPERTS, dtype=I32)[None, :], axis=1)
    tid = jnp.sum(owner * (first_tile - step_start)[None, :], axis=1) + s
    ids = jnp.arange(N_EXPERTS, dtype=I32)
    later = (ids[None, :] > ids[:, None]) & (counts[None, :] > 0)
    next_e = jnp.min(jnp.where(later, ids[None, :], N_EXPERTS), axis=1)
    next_e = jnp.where(next_e == N_EXPERTS, -1, next_e)
    nxt = jnp.sum(owner * next_e[None, :], axis=1)
    return gid, tid, nxt, offs, n_active.reshape(1).astype(I32), n_steps


SC_LANES = 16
SC_INDEX_BATCH = 128


def _invert(dest):
    r = dest.shape[0]
    mesh = plsc.VectorSubcoreMesh(core_axis_name="core", subcore_axis_name="subcore")
    per = r // mesh.num_subcores
    assert per * mesh.num_subcores == r and per % SC_INDEX_BATCH == 0

    @functools.partial(
        pl.kernel, mesh=mesh, out_type=jax.ShapeDtypeStruct((r,), I32),
        scratch_types=[pltpu.VMEM_SHARED((r,), I32), pltpu.VMEM((per,), I32), pltpu.VMEM((per,), I32)],
        compiler_params=pltpu.CompilerParams(needs_layout_passes=False),
        name="invert",
    )
    def invert(dest_hbm, inv_hbm, table, idx, ids):
        @pl.when(lax.axis_index("core") == 0)
        def _():
            base = lax.axis_index("subcore") * per
            pltpu.sync_copy(dest_hbm.at[pl.ds(base, per)], idx)
            lane = lax.iota(I32, SC_LANES)

            @pl.loop(0, per // SC_LANES)
            def _(i):
                p = base + i * SC_LANES + lane
                tok = lax.shift_right_logical(p, 9) * LANES + (p & (LANES - 1))
                slot = lax.shift_right_logical(p, 7) & (TOP_K - 1)
                ids[pl.ds(i * SC_LANES, SC_LANES)] = tok * TOP_K + slot

            @pl.loop(0, per // SC_INDEX_BATCH)
            def _(j):
                span = pl.ds(j * SC_INDEX_BATCH, SC_INDEX_BATCH)
                pltpu.sync_copy(ids.at[span], table.at[idx.at[span]])

            plsc.subcore_barrier()
            pltpu.sync_copy(table.at[pl.ds(base, per)], inv_hbm.at[pl.ds(base, per)])

    return invert(dest)


def _experts_kernel(gid_ref, tid_ref, nxt_ref, offs_ref, nact_ref,
                    dstp_ref, dst0_ref, src0_ref, src1_ref, src2_ref, bg_ref, bu_ref, bd_ref,
                    h2_hbm, wg_hbm, wu_hbm, wd_hbm, y4_hbm,
                    wg_b, wu_b, wd_b, wg_f, wu_f, wd_f, relay, rows, ybuf, sems, row_sems, y_sems):
    s = pl.program_id(0)
    tile_rows = rows.shape[1]
    tm = tile_rows // SUBLANES
    landing = ((wg_hbm, wg_f, wg_b), (wu_hbm, wu_f, wu_b), (wd_hbm, wd_f, wd_b))

    def fetch(e):
        for j, (hbm, land, _) in enumerate(landing):
            pltpu.make_async_copy(hbm.at[e], land, sems.at[j]).start()

    def row_in(idx_ref, into, r):
        src = h2_hbm.at[pl.ds(pl.multiple_of(idx_ref[r], SUBLANES), SUBLANES)]
        dst = rows.at[into, pl.ds(pl.multiple_of(r * SUBLANES, SUBLANES), SUBLANES)]
        return pltpu.make_async_copy(src, dst, row_sems.at[into])

    def row_out(idx_ref, frm, r):
        src = ybuf.at[frm, pl.ds(pl.multiple_of(r * SUBLANES, SUBLANES), SUBLANES)]
        dst = y4_hbm.at[pl.ds(pl.multiple_of(idx_ref[r], SUBLANES), SUBLANES)]
        return pltpu.make_async_copy(src, dst, y_sems.at[frm])

    def in_line(make, idx_ref, buf, first=0, count=None):
        for r in range(first, tm if count is None else first + count):
            make(idx_ref, buf, r).start(priority=r % 2)

    def in_loop(make, idx_ref, buf):
        def group(g, carry):
            for j in range(ISSUE_GROUP):
                make(idx_ref, buf, g * ISSUE_GROUP + j).start(priority=j % 2)
            return carry
        lax.fori_loop(0, tm // ISSUE_GROUP, group, 0)

    def wait_rows(buf):
        pltpu.make_async_copy(h2_hbm.at[pl.ds(0, tile_rows)], rows.at[buf], row_sems.at[buf]).wait()

    def wait_y(buf):
        pltpu.make_async_copy(ybuf.at[buf], y4_hbm.at[pl.ds(0, tile_rows)], y_sems.at[buf]).wait()

    @pl.when(s < nact_ref[0])
    def _():
        e = gid_ref[s]
        m = tid_ref[s]
        cur = lax.rem(m, 3)
        before = lax.rem(m + 2, 3)
        last = nact_ref[0] - 1
        new_tile = (s == 0) | (tid_ref[jnp.maximum(s - 1, 0)] != m)

        @pl.when(s == 0)
        def _():
            fetch(e)
            in_loop(row_in, src0_ref, 0)
            in_loop(row_in, src1_ref, 1)
            ybuf[2] = jnp.zeros(ybuf.shape[1:], F32)

        @pl.when((s == 0) | (gid_ref[jnp.maximum(s - 1, 0)] != e))
        def _():
            for j, (hbm, land, half) in enumerate(landing):
                pltpu.make_async_copy(hbm.at[e], land, sems.at[j]).wait()
                half[...] = land[...].astype(BF16)

            @pl.when(nxt_ref[s] >= 0)
            def _():
                fetch(nxt_ref[s])

        @pl.when(new_tile)
        def _():
            wait_rows(cur)

            @pl.when(m >= 2)
            def _():
                wait_y(cur)

        def ffn(load_x, after_gate=lambda: None):
            g = _dot(load_x(), wg_b[...]) + bg_ref[pl.ds(e, 1), :]
            after_gate()
            u = _dot(load_x(), wu_b[...]) + bu_ref[pl.ds(e, 1), :]
            g = jnp.minimum(g, SWIGLU_LIMIT)
            u = jnp.clip(u, -SWIGLU_LIMIT, SWIGLU_LIMIT)
            glu = g * jax.nn.sigmoid(SWIGLU_ALPHA * g)
            return _dot(((u + 1.0) * glu).astype(BF16), wd_b[...]) + bd_ref[pl.ds(e, 1), :]

        lo = offs_ref[e]
        hi = offs_ref[e + 1]
        whole_tile = (lo <= m * tm) & (hi >= (m + 1) * tm)

        def ffn_and_copies(load_x, first, count):
            pinned = count // 2
            y = ffn(load_x, lambda: in_line(row_in, src2_ref, before, first, pinned))
            in_line(row_in, src2_ref, before, first + pinned, count - pinned)
            in_line(row_out, dstp_ref, before, first, count)
            return y

        @pl.when(whole_tile)
        def _():
            y = ffn_and_copies(lambda: _load_token_tiles(rows.at[cur]).astype(BF16), 0, tm)
            _store_token_tiles(ybuf.at[cur], y)

        @pl.when(jnp.logical_not(whole_tile))
        def _():
            sub = relay.shape[0] // SUBLANES
            for j in range(tm // sub):
                first = m * tm + j * sub
                span = pl.ds(j * sub * SUBLANES, sub * SUBLANES)
                load_x = lambda span=span: _load_token_tiles(rows.at[cur, span]).astype(BF16)

                def put(y, first=first, span=span):
                    _store_token_tiles(relay, y)
                    row = first + lax.shift_right_logical(lax.broadcasted_iota(I32, relay.shape, 0), 3)
                    pltpu.store(ybuf.at[cur, span], relay[...], mask=(row >= lo) & (row < hi))

                evaluate = (lo < first + sub) & (hi > first)
                owns_end = hi >= first + sub

                @pl.when(evaluate & owns_end)
                def _():
                    put(ffn_and_copies(load_x, j * sub, sub))

                @pl.when(evaluate & jnp.logical_not(owns_end))
                def _():
                    put(ffn(load_x))

        @pl.when(s == last)
        def _():
            in_loop(row_out, dst0_ref, cur)
            wait_rows(lax.rem(m + 1, 3))
            wait_rows(before)
            for buf in range(3):
                wait_y(buf)


def _experts(gid, tid, nxt, offs, nact, n_steps, inv, h2, wg, bg, wu, bu, wd, bd):
    ne, d, f = wg.shape
    tm = EXPERT_TILE
    n_tokens = h2.shape[0] // SUBLANES
    n_tiles = inv.shape[0] // tm
    assert n_tiles >= 2
    slot_rows = (n_tokens + tm // TOP_K) * SUBLANES
    inv_ext = jnp.concatenate([n_tokens * TOP_K + jnp.arange(tm, dtype=I32), inv])
    tok = lax.shift_right_logical(inv_ext, TOP_K.bit_length() - 1)
    src_row = tok * SUBLANES
    dst_row = (inv_ext & (TOP_K - 1)) * slot_rows + src_row
    whole = lambda s, gid, tid, nxt, offs, nact: (0, 0)

    def order_of(k):
        return pl.BlockSpec(
            (tm,), lambda s, gid, tid, nxt, offs, nact: (jnp.minimum(tid[s] + k, n_tiles - 1) + 1,),
            memory_space=pltpu.SMEM)

    grid_spec = pltpu.PrefetchScalarGridSpec(
        num_scalar_prefetch=5,
        grid=(n_steps,),
        in_specs=[
            order_of(-1), order_of(0), order_of(0), order_of(1), order_of(2),
            pl.BlockSpec((ne, f), whole),
            pl.BlockSpec((ne, f), whole),
            pl.BlockSpec((ne, d), whole),
            pl.BlockSpec(memory_space=pl.ANY),
            pl.BlockSpec(memory_space=pl.ANY),
            pl.BlockSpec(memory_space=pl.ANY),
            pl.BlockSpec(memory_space=pl.ANY),
        ],
        out_specs=pl.BlockSpec(memory_space=pl.ANY),
        scratch_shapes=[
            pltpu.VMEM((d, f), BF16), pltpu.VMEM((d, f), BF16), pltpu.VMEM((f, d), BF16),
            pltpu.VMEM((d, f), F32), pltpu.VMEM((d, f), F32), pltpu.VMEM((f, d), F32),
            pltpu.VMEM((EXPERT_SUBTILE * SUBLANES, LANES), F32),
            pltpu.VMEM((3, tm * SUBLANES, LANES), F32),
            pltpu.VMEM((3, tm * SUBLANES, LANES), F32),
            pltpu.SemaphoreType.DMA((3,)),
            pltpu.SemaphoreType.DMA((3,)),
            pltpu.SemaphoreType.DMA((3,)),
        ],
    )
    y4 = pl.pallas_call(
        _experts_kernel,
        out_shape=jax.ShapeDtypeStruct((TOP_K * slot_rows, LANES), F32),
        grid_spec=grid_spec,
        compiler_params=pltpu.CompilerParams(
            dimension_semantics=("arbitrary",), vmem_limit_bytes=VMEM_LIMIT),
        name="experts",
    )(gid, tid, nxt, offs, nact, dst_row, dst_row, src_row, src_row, src_row, bg, bu, bd, h2, wg, wu, wd)
    return y4.reshape(TOP_K, slot_rows, LANES)


def _combine_kernel(gate_ref, y4_ref, x1p_ref, x1s_ref, modp_ref, g2s_ref, fn_ref, outp_ref, outs_ref,
                    *, n_prompt_tiles):
    i = pl.program_id(0)
    tm = x1p_ref.shape[0]
    gates = gate_ref[...].T
    cols = []
    for c in range(SUBLANES):
        acc = gates[:, 0:1] * y4_ref[0, pl.ds(c, tm, stride=SUBLANES), :]
        for k in range(1, TOP_K):
            acc = acc + gates[:, k:k + 1] * y4_ref[k, pl.ds(c, tm, stride=SUBLANES), :]
        cols.append(acc)
    ffn = jnp.concatenate(cols, axis=-1)

    @pl.when(i < n_prompt_tiles)
    def _():
        g2 = modp_ref[...][5:6, :]
        outp_ref[...] = _rmsnorm(x1p_ref[...] + g2 * ffn, fn_ref[...])

    @pl.when(i >= n_prompt_tiles)
    def _():
        outs_ref[...] = _rmsnorm(x1s_ref[...] + g2s_ref[...] * ffn, fn_ref[...])


def _combine(gates, y4, x1_p, x1_s, mod_p, g2_s, final_norm, tokens_per_seq):
    n_p, d = x1_p.shape
    n_s = x1_s.shape[0]
    tm = COMBINE_TILE
    npt, nst = n_p // tm, n_s // tm
    tiles_per_seq = tokens_per_seq // tm
    pmap = lambda i: (jnp.minimum(i, npt - 1), 0)
    smap = lambda i: (jnp.maximum(i - npt, 0), 0)
    return pl.pallas_call(
        functools.partial(_combine_kernel, n_prompt_tiles=npt),
        out_shape=(jax.ShapeDtypeStruct((n_p, d), F32), jax.ShapeDtypeStruct((n_s, d), F32)),
        grid=(npt + nst,),
        in_specs=[
            pl.BlockSpec((SUBLANES, tm), lambda i: (0, i)),
            pl.BlockSpec((TOP_K, tm * SUBLANES, LANES), lambda i: (0, i, 0)),
            pl.BlockSpec((tm, d), pmap),
            pl.BlockSpec((tm, d), smap),
            pl.BlockSpec((None, 6, d), lambda i: (jnp.minimum(i, npt - 1) // tiles_per_seq, 0, 0)),
            pl.BlockSpec((tm, d), smap),
            pl.BlockSpec((1, d), lambda i: (0, 0)),
        ],
        out_specs=(pl.BlockSpec((tm, d), pmap), pl.BlockSpec((tm, d), smap)),
        compiler_params=pltpu.CompilerParams(
            dimension_semantics=("arbitrary",), vmem_limit_bytes=VMEM_LIMIT),
        name="combine",
    )(gates, y4, x1_p, x1_s, mod_p, g2_s, final_norm)


def kernel(x_prompt, x_sample, state_pool, state_conv, c_prompt, c_sample, norm1, norm2, w_ada, b_ada,
           w_in, w_pool, pool_scale, w_conv, w_out, w_router, b_router, w_gate, b_gate, w_up, b_up,
           w_down, b_down, final_norm):
    depth = norm1.shape[0]
    assert depth == 1, "single-layer step"
    bp, tp, d = x_prompt.shape
    bs, ts, _ = x_sample.shape
    dp = state_pool.shape[-1]
    n_hist = state_pool.shape[2]
    n_chist = state_conv.shape[2]
    n_p, n_s = bp * tp, bs * ts
    assert d == SUBLANES * LANES, "token-tile layout assumes one vreg tile per token row"
    assert tp % TOKEN_TILE == 0 and n_s == TOKEN_TILE, "the sample group fills exactly one token tile"
    assert tp % COMBINE_TILE == 0 and n_s % COMBINE_TILE == 0
    assert (n_p + n_s) % ROUTE_TILE == 0 and ((n_p + n_s) * TOP_K) % EXPERT_TILE == 0

    l = 0
    n1 = norm1[l].reshape(1, d)
    n2 = norm2[l].reshape(1, d)
    w_in_b = w_in[l].astype(BF16)
    w_pool_b = w_pool[l].astype(BF16)
    w_out_b = w_out[l].astype(BF16)
    pscale = pool_scale[l].reshape(1, dp)
    w_r = w_router[l].T.astype(BF16)
    b_r = b_router[l].reshape(N_EXPERTS, 1)

    mod = _adaln(jnp.concatenate([c_sample, c_prompt], axis=0), w_ada[l], b_ada[l])
    mod_s = mod[:bs]
    mod_p = mod[bs:].reshape(bp, 6, d)

    xs_tm = jnp.transpose(x_sample, (1, 0, 2)).reshape(n_s, d)
    ps_tm = jnp.transpose(state_pool[l], (1, 0, 2))
    cs_tm = jnp.transpose(state_conv[l], (1, 0, 2))
    x1_s, h2_s, lg_s, newp_tm, newc_tm = _mixer_sample(
        xs_tm, mod, ps_tm, cs_tm, n1, n2, w_in_b, w_pool_b, pscale, w_conv[l], w_out_b, w_r, b_r, ts)

    x1_p, h2, lg_p, u_tail, v_tail = _mixer_prompt(
        x_prompt, mod_p, n1, n2, w_in_b, w_pool_b, pscale, w_conv[l], w_out_b, w_r, b_r, h2_s)

    dest, gates, counts_f = _route(lg_p, lg_s)
    counts = counts_f[:, 0].astype(I32)
    n_rows = (n_p + n_s) * TOP_K
    gid, tid, nxt, offs, nact, n_steps = _group_metadata(counts, n_rows, EXPERT_TILE)

    dest = dest.reshape(-1)
    inv = _invert(dest)
    y4 = _experts(gid, tid, nxt, offs, nact, n_steps, inv, h2,
                  w_gate[l], b_gate[l], w_up[l], b_up[l], w_down[l], b_down[l])

    g2_s = jnp.tile(mod_s[:, 5 * d:], (ts, 1))
    y_p, y_s = _combine(gates, y4, x1_p, x1_s, mod_p, g2_s, final_norm.reshape(1, d), tp)

    y_prompt = y_p.reshape(bp, tp, d)
    y_sample = jnp.transpose(y_s.reshape(ts, bs, d), (1, 0, 2))
    new_pool_prompt = u_tail[:, POOL_HALO - n_hist:, :][None]
    new_conv_prompt = v_tail[:, CONV_HALO - n_chist:, :][None]
    new_pool_sample = jnp.transpose(newp_tm, (1, 0, 2))[None]
    new_conv_sample = jnp.transpose(newc_tm, (1, 0, 2))[None]
    return (y_prompt, y_sample, new_pool_prompt, new_conv_prompt, new_pool_sample, new_conv_sample)
```

```python
import functools

import jax
import jax.numpy as jnp
from jax import lax
from jax.experimental import pallas as pl
from jax.experimental.pallas import tpu as pltpu
from jax.experimental.pallas import tpu_sc as plsc

F32 = jnp.float32
BF16 = jnp.bfloat16
I32 = jnp.int32

POOL_WINDOWS = (2, 4, 8, 16)
POOL_HALO = 16
CONV_TAPS = 3
CONV_HALO = 8
N_EXPERTS = 32
TOP_K = 4
SWIGLU_LIMIT = 7.0
SWIGLU_ALPHA = 1.702
EPS = 1e-5
PAST_LEN = 16384

LANES = 128
SUBLANES = 8

TOKEN_TILE = 512
ROUTE_TILE = 512
COMBINE_TILE = 512
EXPERT_TILE = 512
EXPERT_SUBTILE = 128
VMEM_LIMIT = 56 * 1024 * 1024


def _rmsnorm(x, g):
    ms = jnp.mean(x * x, axis=-1, keepdims=True)
    return x * lax.rsqrt(ms + EPS) * g


def _dot(a, b):
    return jnp.dot(a, b, preferred_element_type=F32)


def _store_token_tiles(ref, val):
    rows = val.shape[0]
    for c in range(SUBLANES):
        ref[pl.ds(c, rows, stride=SUBLANES), :] = val[:, c * LANES:(c + 1) * LANES]


def _load_token_tiles(ref):
    rows = ref.shape[0] // SUBLANES
    return jnp.concatenate([ref[pl.ds(c, rows, stride=SUBLANES), :] for c in range(SUBLANES)], axis=-1)


ISSUE_GROUP = 4


def _adaln_kernel(c_ref, w_ref, b_ref, o_ref):
    c = c_ref[...]
    s = c * jax.nn.sigmoid(c)
    o_ref[...] = _dot(s.astype(BF16), w_ref[...].astype(BF16)) + b_ref[...]


def _adaln(c, w_ada, b_ada):
    rows, d = c.shape
    n = w_ada.shape[1]
    tn = 1024
    return pl.pallas_call(
        _adaln_kernel,
        out_shape=jax.ShapeDtypeStruct((rows, n), F32),
        grid=(n // tn,),
        in_specs=[
            pl.BlockSpec((rows, d), lambda j: (0, 0)),
            pl.BlockSpec((d, tn), lambda j: (0, j)),
            pl.BlockSpec((1, tn), lambda j: (0, j)),
        ],
        out_specs=pl.BlockSpec((rows, tn), lambda j: (0, j)),
        compiler_params=pltpu.CompilerParams(
            dimension_semantics=("arbitrary",), vmem_limit_bytes=VMEM_LIMIT),
        name="adaln",
    )(c, w_ada, b_ada.reshape(1, n))


def _mix_tail(x, pool_in, conv_out, g1, sc2, sh2, n2, wpool_ref, pscale, wout, wr, br):
    gw = pool_in.shape[1] // len(POOL_WINDOWS)
    mixed = [_dot(pool_in[:, g * gw:(g + 1) * gw].astype(BF16), wpool_ref[g])
             for g in range(len(POOL_WINDOWS))]
    pool_out = jnp.concatenate(mixed, axis=-1) * pscale
    mix_in = jnp.concatenate([pool_out, conv_out], axis=-1).astype(BF16)
    x1 = x + g1 * _dot(mix_in, wout)
    h2 = _rmsnorm(x1, n2) * (1.0 + sc2) + sh2
    logits_t = lax.dot_general(wr, h2.astype(BF16), (((1,), (1,)), ((), ())),
                               preferred_element_type=F32) + br
    return x1, h2, logits_t


def _mixer_prompt_kernel(x_ref, mod_ref, n1_ref, n2_ref, win_ref, wpool_ref, pscale_ref, wconv_ref,
                         wout_ref, wr_ref, br_ref, h2s_ref,
                         x1_ref, h2_ref, lg_ref, upool_ref, vconv_ref, ubuf, vbuf, *, tiles_per_seq):
    i = pl.program_id(0)

    @pl.when(i < pl.num_programs(0) - 1)
    def _():
        _mixer_prompt_tile(x_ref, mod_ref, n1_ref, n2_ref, win_ref, wpool_ref, pscale_ref, wconv_ref,
                           wout_ref, wr_ref, br_ref, x1_ref, h2_ref, lg_ref, upool_ref, vconv_ref,
                           ubuf, vbuf, lax.rem(i, tiles_per_seq), tiles_per_seq)

    @pl.when(i == pl.num_programs(0) - 1)
    def _():
        h2_ref[...] = h2s_ref[...]


def _mixer_prompt_tile(x_ref, mod_ref, n1_ref, n2_ref, win_ref, wpool_ref, pscale_ref, wconv_ref,
                       wout_ref, wr_ref, br_ref, x1_ref, h2_ref, lg_ref, upool_ref, vconv_ref,
                       ubuf, vbuf, t, tiles_per_seq):
    tt = x_ref.shape[0]
    dp = ubuf.shape[1]
    gw = dp // len(POOL_WINDOWS)

    @pl.when(t == 0)
    def _():
        ubuf[0:POOL_HALO, :] = jnp.zeros((POOL_HALO, dp), F32)
        vbuf[0:CONV_HALO, :] = jnp.zeros((CONV_HALO, dp), F32)

    x = x_ref[...]
    mod = mod_ref[...]
    sh1, sc1, g1, sh2, sc2, _ = [mod[i:i + 1, :] for i in range(6)]
    h = _rmsnorm(x, n1_ref[...]) * (1.0 + sc1) + sh1
    z = _dot(h.astype(BF16), win_ref[...])
    u, gate_b, gate_c, val = [z[:, i * dp:(i + 1) * dp] for i in range(4)]

    ubuf[POOL_HALO:POOL_HALO + tt, :] = u
    pos = lax.broadcasted_iota(I32, (tt, gw), 0) + t * tt
    pooled = []
    for g, w in enumerate(POOL_WINDOWS):
        cols = slice(g * gw, (g + 1) * gw)
        acc = u[:, cols]
        for j in range(1, w):
            acc = acc + ubuf[POOL_HALO - j:POOL_HALO - j + tt, cols]
        cnt = jnp.minimum(pos + 1, w).astype(F32)
        pooled.append(acc / cnt - u[:, cols])
    pool_in = jnp.concatenate(pooled, axis=-1)

    v = gate_c * val
    vbuf[CONV_HALO:CONV_HALO + tt, :] = v
    wc = wconv_ref[...]
    y = (wc[0:1, :] * vbuf[CONV_HALO - 2:CONV_HALO - 2 + tt, :]
         + wc[1:2, :] * vbuf[CONV_HALO - 1:CONV_HALO - 1 + tt, :]
         + wc[2:3, :] * v)
    conv_out = gate_b * y

    x1, h2, logits = _mix_tail(x, pool_in, conv_out, g1, sc2, sh2, n2_ref[...], wpool_ref,
                               pscale_ref[...], wout_ref[...], wr_ref[...], br_ref[...])
    x1_ref[...] = x1
    _store_token_tiles(h2_ref, h2)
    lg_ref[...] = logits

    ubuf[0:POOL_HALO, :] = ubuf[tt:tt + POOL_HALO, :]
    vbuf[0:CONV_HALO, :] = vbuf[tt:tt + CONV_HALO, :]

    @pl.when(t == tiles_per_seq - 1)
    def _():
        upool_ref[...] = ubuf[0:POOL_HALO, :]
        vconv_ref[...] = vbuf[0:CONV_HALO, :]


def _mixer_prompt(x, mod_p, n1, n2, w_in, w_pool, pscale, w_conv, w_out, w_r, b_r, h2_s):
    b, t, d = x.shape
    dp = w_pool.shape[0] * w_pool.shape[1]
    tt = min(TOKEN_TILE, t)
    nt = t // tt
    n_real = b * nt
    assert h2_s.shape[0] == tt * SUBLANES
    seq = lambda i: jnp.minimum(i, n_real - 1) // nt
    tile = lambda i: jnp.minimum(i, n_real - 1)
    const2 = lambda i: (0, 0)
    const3 = lambda i: (0, 0, 0)
    return pl.pallas_call(
        functools.partial(_mixer_prompt_kernel, tiles_per_seq=nt),
        out_shape=(
            jax.ShapeDtypeStruct((b * t, d), F32),
            jax.ShapeDtypeStruct(((n_real + 1) * tt * SUBLANES, LANES), F32),
            jax.ShapeDtypeStruct((N_EXPERTS, b * t), F32),
            jax.ShapeDtypeStruct((b, POOL_HALO, dp), F32),
            jax.ShapeDtypeStruct((b, CONV_HALO, dp), F32),
        ),
        grid=(n_real + 1,),
        in_specs=[
            pl.BlockSpec((None, tt, d), lambda i: (seq(i), tile(i) % nt, 0)),
            pl.BlockSpec((None, 6, d), lambda i: (seq(i), 0, 0)),
            pl.BlockSpec((1, d), const2),
            pl.BlockSpec((1, d), const2),
            pl.BlockSpec(w_in.shape, const2),
            pl.BlockSpec(w_pool.shape, const3),
            pl.BlockSpec((1, dp), const2),
            pl.BlockSpec(w_conv.shape, const2),
            pl.BlockSpec(w_out.shape, const2),
            pl.BlockSpec(w_r.shape, const2),
            pl.BlockSpec((N_EXPERTS, 1), const2),
            pl.BlockSpec(h2_s.shape, const2),
        ],
        out_specs=(
            pl.BlockSpec((tt, d), lambda i: (tile(i), 0)),
            pl.BlockSpec((tt * SUBLANES, LANES), lambda i: (i, 0)),
            pl.BlockSpec((N_EXPERTS, tt), lambda i: (0, tile(i))),
            pl.BlockSpec((None, POOL_HALO, dp), lambda i: (seq(i), 0, 0)),
            pl.BlockSpec((None, CONV_HALO, dp), lambda i: (seq(i), 0, 0)),
        ),
        scratch_shapes=[
            pltpu.VMEM((POOL_HALO + tt, dp), F32),
            pltpu.VMEM((CONV_HALO + tt, dp), F32),
        ],
        compiler_params=pltpu.CompilerParams(
            dimension_semantics=("arbitrary",), vmem_limit_bytes=VMEM_LIMIT),
        name="mixer_prompt",
    )(x, mod_p, n1, n2, w_in, w_pool, pscale, w_conv, w_out, w_r, b_r, h2_s)


def _mixer_sample_kernel(x_ref, mod_ref, pstate_ref, cstate_ref, n1_ref, n2_ref, win_f32, wpool_f32,
                         pscale_ref, wconv_ref, wout_f32, wr_ref, br_ref,
                         x1_ref, h2_ref, lg_ref, newp_ref, newc_ref, win_ref, wpool_ref, wout_ref,
                         *, steps):
    win_ref[...] = win_f32[...].astype(BF16)
    wpool_ref[...] = wpool_f32[...].astype(BF16)
    wout_ref[...] = wout_f32[...].astype(BF16)
    nb = pstate_ref.shape[1]
    d = x_ref.shape[1]
    dp = pstate_ref.shape[2]
    gw = dp // len(POOL_WINDOWS)
    n_hist = pstate_ref.shape[0]
    n_chist = cstate_ref.shape[0]

    x = x_ref[...]
    mod = mod_ref[0:nb, :]
    rep = lambda a: jnp.concatenate([a] * steps, axis=0)
    sh1, sc1, g1, sh2, sc2, _ = [rep(mod[:, i * d:(i + 1) * d]) for i in range(6)]
    h = _rmsnorm(x, n1_ref[...]) * (1.0 + sc1) + sh1
    z = _dot(h.astype(BF16), win_ref[...])
    u, gate_b, gate_c, val = [z[:, i * dp:(i + 1) * dp] for i in range(4)]

    ext = [pstate_ref[i] for i in range(n_hist)] + [u[s * nb:(s + 1) * nb, :] for s in range(steps)]
    pooled_steps = []
    for s in range(steps):
        groups = []
        for g, w in enumerate(POOL_WINDOWS):
            cols = slice(g * gw, (g + 1) * gw)
            acc = ext[n_hist + s][:, cols]
            for j in range(1, w):
                acc = acc + ext[n_hist + s - j][:, cols]
            cnt = float(min(PAST_LEN + s + 1, w))
            groups.append(acc / cnt - ext[n_hist + s][:, cols])
        pooled_steps.append(jnp.concatenate(groups, axis=-1))
    pool_in = jnp.concatenate(pooled_steps, axis=0)

    v = gate_c * val
    vext = [cstate_ref[i] for i in range(n_chist)] + [v[s * nb:(s + 1) * nb, :] for s in range(steps)]
    wc = wconv_ref[...]
    y = jnp.concatenate(
        [wc[0:1, :] * vext[s] + wc[1:2, :] * vext[s + 1] + wc[2:3, :] * vext[s + 2] for s in range(steps)],
        axis=0)
    conv_out = gate_b * y

    x1, h2, logits = _mix_tail(x, pool_in, conv_out, g1, sc2, sh2, n2_ref[...], wpool_ref,
                               pscale_ref[...], wout_ref[...], wr_ref[...], br_ref[...])
    x1_ref[...] = x1
    _store_token_tiles(h2_ref, h2)
    lg_ref[...] = logits
    for i in range(n_hist):
        newp_ref[i] = ext[steps + i]
    for i in range(n_chist):
        newc_ref[i] = vext[steps + i]


def _mixer_sample(x_tm, mod_s, pstate_tm, cstate_tm, n1, n2, w_in, w_pool, pscale, w_conv, w_out, w_r, b_r,
                  steps):
    rows, d = x_tm.shape
    return pl.pallas_call(
        functools.partial(_mixer_sample_kernel, steps=steps),
        out_shape=(
            jax.ShapeDtypeStruct((rows, d), F32),
            jax.ShapeDtypeStruct((rows * SUBLANES, LANES), F32),
            jax.ShapeDtypeStruct((N_EXPERTS, rows), F32),
            jax.ShapeDtypeStruct(pstate_tm.shape, F32),
            jax.ShapeDtypeStruct(cstate_tm.shape, F32),
            jax.ShapeDtypeStruct(w_in.shape, BF16),
            jax.ShapeDtypeStruct(w_pool.shape, BF16),
            jax.ShapeDtypeStruct(w_out.shape, BF16),
        ),
        compiler_params=pltpu.CompilerParams(vmem_limit_bytes=VMEM_LIMIT),
        name="mixer_sample",
    )(x_tm, mod_s, pstate_tm, cstate_tm, n1, n2, w_in, w_pool, pscale, w_conv, w_out, w_r, b_r)


def _route_kernel(lgp_ref, lgs_ref, dest_ref, gate_ref, cnt_ref, counts, start, before):
    ne = lgp_ref.shape[0]
    tr = before.shape[0]
    reps = tr // LANES
    n_prompt_chunks = lgp_ref.shape[1] // tr
    n_sample_chunks = lgs_ref.shape[1] // tr
    eidx = lax.broadcasted_iota(I32, (ne, tr), 0)

    def top_k(ref, c):
        work = ref[:, pl.ds(pl.multiple_of(c * tr, tr), tr)]
        top_v, onehots = [], []
        for _ in range(TOP_K):
            m = jnp.max(work, axis=0, keepdims=True)
            idx = jnp.min(jnp.where(work == m, eidx, ne), axis=0, keepdims=True)
            sel = eidx == idx
            top_v.append(m)
            onehots.append(sel)
            work = jnp.where(sel, -jnp.inf, work)
        mask = jnp.where(onehots[0] | onehots[1] | onehots[2] | onehots[3], 1.0, 0.0)
        chunk_counts = jnp.broadcast_to(jnp.sum(mask, axis=1, keepdims=True), (ne, LANES))
        return top_v, onehots, mask, chunk_counts

    def count_chunk(ref):
        def body(c, carry):
            counts[...] = counts[...] + top_k(ref, c)[3]
            return carry
        return body

    counts[...] = jnp.zeros_like(counts)
    lax.fori_loop(0, n_prompt_chunks, count_chunk(lgp_ref), 0)
    lax.fori_loop(0, n_sample_chunks, count_chunk(lgs_ref), 0)

    total = counts[...]
    hi = jnp.floor(total * (1.0 / 256.0))
    lo = total - hi * 256.0
    r = lax.broadcasted_iota(I32, (ne, ne), 0)
    col = lax.broadcasted_iota(I32, (ne, ne), 1)
    lower = jnp.where(col < r, 1.0, 0.0).astype(BF16)
    start[...] = 256.0 * _dot(lower, hi.astype(BF16)) + _dot(lower, lo.astype(BF16))
    cnt_ref[...] = total
    counts[...] = jnp.zeros_like(counts)

    r = lax.broadcasted_iota(I32, (tr, tr), 0)
    col = lax.broadcasted_iota(I32, (tr, tr), 1)
    before[...] = jnp.where(r < col, 1.0, 0.0).astype(BF16)
    gate_ref[...] = jnp.zeros_like(gate_ref)

    def place_chunk(ref, first_chunk):
        def body(c, carry):
            top_v, onehots, mask, chunk_counts = top_k(ref, c)
            base = jnp.concatenate([counts[...] + start[...]] * reps, axis=1)
            rank = _dot(mask.astype(BF16), before[...]) + base
            counts[...] = counts[...] + chunk_counts
            es = [jnp.exp(v - top_v[0]) for v in top_v]
            denom = es[0] + es[1] + es[2] + es[3]
            chunk = first_chunk + c
            cols = pl.ds(pl.multiple_of(chunk * tr, tr), tr)
            for k in range(TOP_K):
                d = jnp.sum(jnp.where(onehots[k], rank, 0.0), axis=0, keepdims=True).astype(I32)
                for j in range(reps):
                    dest_ref[chunk * reps + j, k:k + 1, :] = d[:, j * LANES:(j + 1) * LANES]
                gate_ref[k:k + 1, cols] = es[k] / denom
            return carry
        return body

    lax.fori_loop(0, n_prompt_chunks, place_chunk(lgp_ref, 0), 0)
    lax.fori_loop(0, n_sample_chunks, place_chunk(lgs_ref, n_prompt_chunks), 0)


def _route(lgt_p, lgt_s):
    ne, n_p = lgt_p.shape
    n_s = lgt_s.shape[1]
    tr = ROUTE_TILE
    return pl.pallas_call(
        _route_kernel,
        out_shape=(
            jax.ShapeDtypeStruct(((n_p + n_s) // LANES, TOP_K, LANES), I32),
            jax.ShapeDtypeStruct((SUBLANES, n_p + n_s), F32),
            jax.ShapeDtypeStruct((ne, LANES), F32),
        ),
        scratch_shapes=[pltpu.VMEM((ne, LANES), F32), pltpu.VMEM((ne, LANES), F32),
                        pltpu.VMEM((tr, tr), BF16)],
        compiler_params=pltpu.CompilerParams(vmem_limit_bytes=VMEM_LIMIT),
        name="route",
    )(lgt_p, lgt_s)


def _group_metadata(counts, n_rows, tile):
    n_tiles = n_rows // tile
    n_steps = n_tiles + N_EXPERTS - 1
    ends = jnp.cumsum(counts)
    offs = jnp.concatenate([jnp.zeros((1,), I32), ends]).astype(I32)
    first_tile = offs[:-1] // tile
    last_tile = (ends - 1) // tile
    tiles_e = jnp.where(counts > 0, last_tile - first_tile + 1, 0)
    step_end = jnp.cumsum(tiles_e)
    step_start = step_end - tiles_e
    n_active = step_end[-1]
    s = jnp.minimum(jnp.arange(n_steps, dtype=I32), n_active - 1)
    owner = ((s[:, None] >= step_start[None, :]) & (s[:, None] < step_end[None, :])).astype(I32)
    gid = jnp.sum(owner * jnp.arange(N_EXPERTS, dtype=I32)[None, :], axis=1)
    tid = jnp.sum(owner * (first_tile - step_start)[None, :], axis=1) + s
    ids = jnp.arange(N_EXPERTS, dtype=I32)
    later = (ids[None, :] > ids[:, None]) & (counts[None, :] > 0)
    next_e = jnp.min(jnp.where(later, ids[None, :], N_EXPERTS), axis=1)
    next_e = jnp.where(next_e == N_EXPERTS, -1, next_e)
    nxt = jnp.sum(owner * next_e[None, :], axis=1)
    return gid, tid, nxt, offs, n_active.reshape(1).astype(I32), n_steps


SC_LANES = 16
SC_INDEX_BATCH = 128


def _invert(dest):
    r = dest.shape[0]
    mesh = plsc.VectorSubcoreMesh(core_axis_name="core", subcore_axis_name="subcore")
    per = r // mesh.num_subcores
    assert per * mesh.num_subcores == r and per % SC_INDEX_BATCH == 0

    @functools.partial(
        pl.kernel, mesh=mesh, out_type=jax.ShapeDtypeStruct((r,), I32),
        scratch_types=[pltpu.VMEM_SHARED((r,), I32), pltpu.VMEM((per,), I32), pltpu.VMEM((per,), I32)],
        compiler_params=pltpu.CompilerParams(needs_layout_passes=False),
        name="invert",
    )
    def invert(dest_hbm, inv_hbm, table, idx, ids):
        @pl.when(lax.axis_index("core") == 0)
        def _():
            base = lax.axis_index("subcore") * per
            pltpu.sync_copy(dest_hbm.at[pl.ds(base, per)], idx)
            lane = lax.iota(I32, SC_LANES)

            @pl.loop(0, per // SC_LANES)
            def _(i):
                p = base + i * SC_LANES + lane
                tok = lax.shift_right_logical(p, 9) * LANES + (p & (LANES - 1))
                slot = lax.shift_right_logical(p, 7) & (TOP_K - 1)
                ids[pl.ds(i * SC_LANES, SC_LANES)] = tok * TOP_K + slot

            @pl.loop(0, per // SC_INDEX_BATCH)
            def _(j):
                span = pl.ds(j * SC_INDEX_BATCH, SC_INDEX_BATCH)
                pltpu.sync_copy(ids.at[span], table.at[idx.at[span]])

            plsc.subcore_barrier()
            pltpu.sync_copy(table.at[pl.ds(base, per)], inv_hbm.at[pl.ds(base, per)])

    return invert(dest)


def _experts_kernel(gid_ref, tid_ref, nxt_ref, offs_ref, nact_ref,
                    dstp_ref, dst0_ref, src0_ref, src1_ref, src2_ref, bg_ref, bu_ref, bd_ref,
                    h2_hbm, wg_hbm, wu_hbm, wd_hbm, y4_hbm,
                    wg_b, wu_b, wd_b, wg_f, wu_f, wd_f, relay, rows, ybuf, sems, row_sems, y_sems):
    s = pl.program_id(0)
    tile_rows = rows.shape[1]
    tm = tile_rows // SUBLANES
    landing = ((wg_hbm, wg_f, wg_b), (wu_hbm, wu_f, wu_b), (wd_hbm, wd_f, wd_b))

    def fetch(e):
        for j, (hbm, land, _) in enumerate(landing):
            pltpu.make_async_copy(hbm.at[e], land, sems.at[j]).start()

    def row_in(idx_ref, into, r):
        src = h2_hbm.at[pl.ds(pl.multiple_of(idx_ref[r], SUBLANES), SUBLANES)]
        dst = rows.at[into, pl.ds(pl.multiple_of(r * SUBLANES, SUBLANES), SUBLANES)]
        return pltpu.make_async_copy(src, dst, row_sems.at[into])

    def row_out(idx_ref, frm, r):
        src = ybuf.at[frm, pl.ds(pl.multiple_of(r * SUBLANES, SUBLANES), SUBLANES)]
        dst = y4_hbm.at[pl.ds(pl.multiple_of(idx_ref[r], SUBLANES), SUBLANES)]
        return pltpu.make_async_copy(src, dst, y_sems.at[frm])

    def in_line(make, idx_ref, buf, first=0, count=None):
        for r in range(first, tm if count is None else first + count):
            make(idx_ref, buf, r).start(priority=r % 2)

    def in_loop(make, idx_ref, buf):
        def group(g, carry):
            for j in range(ISSUE_GROUP):
                make(idx_ref, buf, g * ISSUE_GROUP + j).start(priority=j % 2)
            return carry
        lax.fori_loop(0, tm // ISSUE_GROUP, group, 0)

    def wait_rows(buf):
        pltpu.make_async_copy(h2_hbm.at[pl.ds(0, tile_rows)], rows.at[buf], row_sems.at[buf]).wait()

    def wait_y(buf):
        pltpu.make_async_copy(ybuf.at[buf], y4_hbm.at[pl.ds(0, tile_rows)], y_sems.at[buf]).wait()

    @pl.when(s < nact_ref[0])
    def _():
        e = gid_ref[s]
        m = tid_ref[s]
        cur = lax.rem(m, 3)
        before = lax.rem(m + 2, 3)
        last = nact_ref[0] - 1
        new_tile = (s == 0) | (tid_ref[jnp.maximum(s - 1, 0)] != m)

        @pl.when(s == 0)
        def _():
            fetch(e)
            in_loop(row_in, src0_ref, 0)
            in_loop(row_in, src1_ref, 1)
            ybuf[2] = jnp.zeros(ybuf.shape[1:], F32)

        @pl.when((s == 0) | (gid_ref[jnp.maximum(s - 1, 0)] != e))
        def _():
            for j, (hbm, land, half) in enumerate(landing):
                pltpu.make_async_copy(hbm.at[e], land, sems.at[j]).wait()
                half[...] = land[...].astype(BF16)

            @pl.when(nxt_ref[s] >= 0)
            def _():
                fetch(nxt_ref[s])

        @pl.when(new_tile)
        def _():
            wait_rows(cur)

            @pl.when(m >= 2)
            def _():
                wait_y(cur)

        def ffn(load_x, after_gate=lambda: None):
            g = _dot(load_x(), wg_b[...]) + bg_ref[pl.ds(e, 1), :]
            after_gate()
            u = _dot(load_x(), wu_b[...]) + bu_ref[pl.ds(e, 1), :]
            g = jnp.minimum(g, SWIGLU_LIMIT)
            u = jnp.clip(u, -SWIGLU_LIMIT, SWIGLU_LIMIT)
            glu = g * jax.nn.sigmoid(SWIGLU_ALPHA * g)
            return _dot(((u + 1.0) * glu).astype(BF16), wd_b[...]) + bd_ref[pl.ds(e, 1), :]

        lo = offs_ref[e]
        hi = offs_ref[e + 1]
        whole_tile = (lo <= m * tm) & (hi >= (m + 1) * tm)

        def ffn_and_copies(load_x, first, count):
            pinned = count // 2
            y = ffn(load_x, lambda: in_line(row_in, src2_ref, before, first, pinned))
            in_line(row_in, src2_ref, before, first + pinned, count - pinned)
            in_line(row_out, dstp_ref, before, first, count)
            return y

        @pl.when(whole_tile)
        def _():
            y = ffn_and_copies(lambda: _load_token_tiles(rows.at[cur]).astype(BF16), 0, tm)
            _store_token_tiles(ybuf.at[cur], y)

        @pl.when(jnp.logical_not(whole_tile))
        def _():
            sub = relay.shape[0] // SUBLANES
            for j in range(tm // sub):
                first = m * tm + j * sub
                span = pl.ds(j * sub * SUBLANES, sub * SUBLANES)
                load_x = lambda span=span: _load_token_tiles(rows.at[cur, span]).astype(BF16)

                def put(y, first=first, span=span):
                    _store_token_tiles(relay, y)
                    row = first + lax.shift_right_logical(lax.broadcasted_iota(I32, relay.shape, 0), 3)
                    pltpu.store(ybuf.at[cur, span], relay[...], mask=(row >= lo) & (row < hi))

                evaluate = (lo < first + sub) & (hi > first)
                owns_end = hi >= first + sub

                @pl.when(evaluate & owns_end)
                def _():
                    put(ffn_and_copies(load_x, j * sub, sub))

                @pl.when(evaluate & jnp.logical_not(owns_end))
                def _():
                    put(ffn(load_x))

        @pl.when(s == last)
        def _():
            in_loop(row_out, dst0_ref, cur)
            wait_rows(lax.rem(m + 1, 3))
            wait_rows(before)
            for buf in range(3):
                wait_y(buf)


def _experts(gid, tid, nxt, offs, nact, n_steps, inv, h2, wg, bg, wu, bu, wd, bd):
    ne, d, f = wg.shape
    tm = EXPERT_TILE
    n_tokens = h2.shape[0] // SUBLANES
    n_tiles = inv.shape[0] // tm
    assert n_tiles >= 2
    slot_rows = (n_tokens + tm // TOP_K) * SUBLANES
    inv_ext = jnp.concatenate([n_tokens * TOP_K + jnp.arange(tm, dtype=I32), inv])
    tok = lax.shift_right_logical(inv_ext, TOP_K.bit_length() - 1)
    src_row = tok * SUBLANES
    dst_row = (inv_ext & (TOP_K - 1)) * slot_rows + src_row
    whole = lambda s, gid, tid, nxt, offs, nact: (0, 0)

    def order_of(k):
        return pl.BlockSpec(
            (tm,), lambda s, gid, tid, nxt, offs, nact: (jnp.minimum(tid[s] + k, n_tiles - 1) + 1,),
            memory_space=pltpu.SMEM)

    grid_spec = pltpu.PrefetchScalarGridSpec(
        num_scalar_prefetch=5,
        grid=(n_steps,),
        in_specs=[
            order_of(-1), order_of(0), order_of(0), order_of(1), order_of(2),
            pl.BlockSpec((ne, f), whole),
            pl.BlockSpec((ne, f), whole),
            pl.BlockSpec((ne, d), whole),
            pl.BlockSpec(memory_space=pl.ANY),
            pl.BlockSpec(memory_space=pl.ANY),
            pl.BlockSpec(memory_space=pl.ANY),
            pl.BlockSpec(memory_space=pl.ANY),
        ],
        out_specs=pl.BlockSpec(memory_space=pl.ANY),
        scratch_shapes=[
            pltpu.VMEM((d, f), BF16), pltpu.VMEM((d, f), BF16), pltpu.VMEM((f, d), BF16),
            pltpu.VMEM((d, f), F32), pltpu.VMEM((d, f), F32), pltpu.VMEM((f, d), F32),
            pltpu.VMEM((EXPERT_SUBTILE * SUBLANES, LANES), F32),
            pltpu.VMEM((3, tm * SUBLANES, LANES), F32),
            pltpu.VMEM((3, tm * SUBLANES, LANES), F32),
            pltpu.SemaphoreType.DMA((3,)),
            pltpu.SemaphoreType.DMA((3,)),
            pltpu.SemaphoreType.DMA((3,)),
        ],
    )
    y4 = pl.pallas_call(
        _experts_kernel,
        out_shape=jax.ShapeDtypeStruct((TOP_K * slot_rows, LANES), F32),
        grid_spec=grid_spec,
        compiler_params=pltpu.CompilerParams(
            dimension_semantics=("arbitrary",), vmem_limit_bytes=VMEM_LIMIT),
        name="experts",
    )(gid, tid, nxt, offs, nact, dst_row, dst_row, src_row, src_row, src_row, bg, bu, bd, h2, wg, wu, wd)
    return y4.reshape(TOP_K, slot_rows, LANES)


def _combine_kernel(gate_ref, y4_ref, x1p_ref, x1s_ref, modp_ref, g2s_ref, fn_ref, outp_ref, outs_ref,
                    *, n_prompt_tiles):
    i = pl.program_id(0)
    tm = x1p_ref.shape[0]
    gates = gate_ref[...].T
    cols = []
    for c in range(SUBLANES):
        acc = gates[:, 0:1] * y4_ref[0, pl.ds(c, tm, stride=SUBLANES), :]
        for k in range(1, TOP_K):
            acc = acc + gates[:, k:k + 1] * y4_ref[k, pl.ds(c, tm, stride=SUBLANES), :]
        cols.append(acc)
    ffn = jnp.concatenate(cols, axis=-1)

    @pl.when(i < n_prompt_tiles)
    def _():
        g2 = modp_ref[...][5:6, :]
        outp_ref[...] = _rmsnorm(x1p_ref[...] + g2 * ffn, fn_ref[...])

    @pl.when(i >= n_prompt_tiles)
    def _():
        g2 = jnp.concatenate([g2s_ref[...]] * (tm // g2s_ref.shape[0]), axis=0)
        outs_ref[...] = _rmsnorm(x1s_ref[...] + g2 * ffn, fn_ref[...])


def _combine(gates, y4, x1_p, x1_s, mod_p, mod, n_batch_s, final_norm, tokens_per_seq):
    n_p, d = x1_p.shape
    n_s = x1_s.shape[0]
    tm = COMBINE_TILE
    assert tm % n_batch_s == 0 and n_batch_s % SUBLANES == 0
    npt, nst = n_p // tm, n_s // tm
    tiles_per_seq = tokens_per_seq // tm
    pmap = lambda i: (jnp.minimum(i, npt - 1), 0)
    smap = lambda i: (jnp.maximum(i - npt, 0), 0)
    return pl.pallas_call(
        functools.partial(_combine_kernel, n_prompt_tiles=npt),
        out_shape=(jax.ShapeDtypeStruct((n_p, d), F32), jax.ShapeDtypeStruct((n_s, d), F32)),
        grid=(npt + nst,),
        in_specs=[
            pl.BlockSpec((SUBLANES, tm), lambda i: (0, i)),
            pl.BlockSpec((TOP_K, tm * SUBLANES, LANES), lambda i: (0, i, 0)),
            pl.BlockSpec((tm, d), pmap),
            pl.BlockSpec((tm, d), smap),
            pl.BlockSpec((None, 6, d), lambda i: (jnp.minimum(i, npt - 1) // tiles_per_seq, 0, 0)),
            pl.BlockSpec((n_batch_s, d), lambda i: (0, 5)),
            pl.BlockSpec((1, d), lambda i: (0, 0)),
        ],
        out_specs=(pl.BlockSpec((tm, d), pmap), pl.BlockSpec((tm, d), smap)),
        compiler_params=pltpu.CompilerParams(
            dimension_semantics=("arbitrary",), vmem_limit_bytes=VMEM_LIMIT),
        name="combine",
    )(gates, y4, x1_p, x1_s, mod_p, mod, final_norm)


def kernel(x_prompt, x_sample, state_pool, state_conv, c_prompt, c_sample, norm1, norm2, w_ada, b_ada,
           w_in, w_pool, pool_scale, w_conv, w_out, w_router, b_router, w_gate, b_gate, w_up, b_up,
           w_down, b_down, final_norm):
    depth = norm1.shape[0]
    assert depth == 1, "single-layer step"
    bp, tp, d = x_prompt.shape
    bs, ts, _ = x_sample.shape
    dp = state_pool.shape[-1]
    n_hist = state_pool.shape[2]
    n_chist = state_conv.shape[2]
    n_p, n_s = bp * tp, bs * ts
    assert d == SUBLANES * LANES, "token-tile layout assumes one vreg tile per token row"
    assert tp % TOKEN_TILE == 0 and n_s == TOKEN_TILE, "the sample group fills exactly one token tile"
    assert tp % COMBINE_TILE == 0 and n_s % COMBINE_TILE == 0
    assert (n_p + n_s) % ROUTE_TILE == 0 and ((n_p + n_s) * TOP_K) % EXPERT_TILE == 0

    l = 0
    n1 = norm1[l].reshape(1, d)
    n2 = norm2[l].reshape(1, d)
    pscale = pool_scale[l].reshape(1, dp)
    w_r = w_router[l].T.astype(BF16)
    b_r = b_router[l].reshape(N_EXPERTS, 1)

    mod = _adaln(jnp.concatenate([c_sample, c_prompt], axis=0), w_ada[l], b_ada[l])
    mod_p = mod[bs:].reshape(bp, 6, d)

    xs_tm = jnp.transpose(x_sample, (1, 0, 2)).reshape(n_s, d)
    ps_tm = jnp.transpose(state_pool[l], (1, 0, 2))
    cs_tm = jnp.transpose(state_conv[l], (1, 0, 2))
    x1_s, h2_s, lg_s, newp_tm, newc_tm, w_in_b, w_pool_b, w_out_b = _mixer_sample(
        xs_tm, mod, ps_tm, cs_tm, n1, n2, w_in[l], w_pool[l], pscale, w_conv[l], w_out[l], w_r, b_r, ts)

    x1_p, h2, lg_p, u_tail, v_tail = _mixer_prompt(
        x_prompt, mod_p, n1, n2, w_in_b, w_pool_b, pscale, w_conv[l], w_out_b, w_r, b_r, h2_s)

    dest, gates, counts_f = _route(lg_p, lg_s)
    counts = counts_f[:, 0].astype(I32)
    n_rows = (n_p + n_s) * TOP_K
    gid, tid, nxt, offs, nact, n_steps = _group_metadata(counts, n_rows, EXPERT_TILE)

    dest = dest.reshape(-1)
    inv = _invert(dest)
    y4 = _experts(gid, tid, nxt, offs, nact, n_steps, inv, h2,
                  w_gate[l], b_gate[l], w_up[l], b_up[l], w_down[l], b_down[l])

    y_p, y_s = _combine(gates, y4, x1_p, x1_s, mod_p, mod, bs, final_norm.reshape(1, d), tp)

    y_prompt = y_p.reshape(bp, tp, d)
    y_sample = jnp.transpose(y_s.reshape(ts, bs, d), (1, 0, 2))
    new_pool_prompt = u_tail[:, POOL_HALO - n_hist:, :][None]
    new_conv_prompt = v_tail[:, CONV_HALO - n_chist:, :][None]
    new_pool_sample = jnp.transpose(newp_tm, (1, 0, 2))[None]
    new_conv_sample = jnp.transpose(newc_tm, (1, 0, 2))[None]
    return (y_prompt, y_sample, new_pool_prompt, new_conv_prompt, new_pool_sample, new_conv_sample)
```

```python
import functools

import jax
import jax.numpy as jnp
from jax import lax
from jax.experimental import pallas as pl
from jax.experimental.pallas import tpu as pltpu
from jax.experimental.pallas import tpu_sc as plsc

F32 = jnp.float32
BF16 = jnp.bfloat16
I32 = jnp.int32

POOL_WINDOWS = (2, 4, 8, 16)
POOL_HALO = 16
CONV_TAPS = 3
CONV_HALO = 8
N_EXPERTS = 32
TOP_K = 4
SWIGLU_LIMIT = 7.0
SWIGLU_ALPHA = 1.702
EPS = 1e-5
PAST_LEN = 16384

LANES = 128
SUBLANES = 8

TOKEN_TILE = 512
ROUTE_TILE = 512
COMBINE_TILE = 256
EXPERT_TILE = 512
EXPERT_SUBTILE = 128
VMEM_LIMIT = 56 * 1024 * 1024


def _rmsnorm(x, g):
    ms = jnp.mean(x * x, axis=-1, keepdims=True)
    return x * lax.rsqrt(ms + EPS) * g


def _dot(a, b):
    return jnp.dot(a, b, preferred_element_type=F32)


def _store_token_tiles(ref, val):
    rows = val.shape[0]
    for c in range(SUBLANES):
        ref[pl.ds(c, rows, stride=SUBLANES), :] = val[:, c * LANES:(c + 1) * LANES]


def _load_token_tiles(ref):
    rows = ref.shape[0] // SUBLANES
    return jnp.concatenate([ref[pl.ds(c, rows, stride=SUBLANES), :] for c in range(SUBLANES)], axis=-1)


ISSUE_GROUP = 4


def _adaln_kernel(c_ref, w_ref, b_ref, o_ref):
    c = c_ref[...]
    s = c * jax.nn.sigmoid(c)
    o_ref[...] = _dot(s.astype(BF16), w_ref[...].astype(BF16)) + b_ref[...]


def _adaln(c, w_ada, b_ada):
    rows, d = c.shape
    n = w_ada.shape[1]
    tn = 1024
    return pl.pallas_call(
        _adaln_kernel,
        out_shape=jax.ShapeDtypeStruct((rows, n), F32),
        grid=(n // tn,),
        in_specs=[
            pl.BlockSpec((rows, d), lambda j: (0, 0)),
            pl.BlockSpec((d, tn), lambda j: (0, j)),
            pl.BlockSpec((1, tn), lambda j: (0, j)),
        ],
        out_specs=pl.BlockSpec((rows, tn), lambda j: (0, j)),
        compiler_params=pltpu.CompilerParams(
            dimension_semantics=("arbitrary",), vmem_limit_bytes=VMEM_LIMIT),
        name="adaln",
    )(c, w_ada, b_ada.reshape(1, n))


def _mix_tail(x, pool_in, conv_out, g1, sc2, sh2, n2, wpool_ref, pscale, wout, wr, br):
    gw = pool_in.shape[1] // len(POOL_WINDOWS)
    mixed = [_dot(pool_in[:, g * gw:(g + 1) * gw].astype(BF16), wpool_ref[g])
             for g in range(len(POOL_WINDOWS))]
    pool_out = jnp.concatenate(mixed, axis=-1) * pscale
    mix_in = jnp.concatenate([pool_out, conv_out], axis=-1).astype(BF16)
    x1 = x + g1 * _dot(mix_in, wout)
    h2 = _rmsnorm(x1, n2) * (1.0 + sc2) + sh2
    logits_t = lax.dot_general(wr, h2.astype(BF16), (((1,), (1,)), ((), ())),
                               preferred_element_type=F32) + br
    return x1, h2, logits_t


def _mixer_prompt_kernel(x_ref, mod_ref, n1_ref, n2_ref, win_ref, wpool_ref, pscale_ref, wconv_ref,
                         wout_ref, wr_ref, br_ref, h2s_ref,
                         x1_ref, h2_ref, lg_ref, upool_ref, vconv_ref, ubuf, vbuf, *, tiles_per_seq):
    i = pl.program_id(0)

    @pl.when(i < pl.num_programs(0) - 1)
    def _():
        _mixer_prompt_tile(x_ref, mod_ref, n1_ref, n2_ref, win_ref, wpool_ref, pscale_ref, wconv_ref,
                           wout_ref, wr_ref, br_ref, x1_ref, h2_ref, lg_ref, upool_ref, vconv_ref,
                           ubuf, vbuf, lax.rem(i, tiles_per_seq), tiles_per_seq)

    @pl.when(i == pl.num_programs(0) - 1)
    def _():
        h2_ref[...] = h2s_ref[...]


def _mixer_prompt_tile(x_ref, mod_ref, n1_ref, n2_ref, win_ref, wpool_ref, pscale_ref, wconv_ref,
                       wout_ref, wr_ref, br_ref, x1_ref, h2_ref, lg_ref, upool_ref, vconv_ref,
                       ubuf, vbuf, t, tiles_per_seq):
    tt = x_ref.shape[0]
    dp = ubuf.shape[1]
    gw = dp // len(POOL_WINDOWS)

    @pl.when(t == 0)
    def _():
        ubuf[0:POOL_HALO, :] = jnp.zeros((POOL_HALO, dp), F32)
        vbuf[0:CONV_HALO, :] = jnp.zeros((CONV_HALO, dp), F32)

    x = x_ref[...]
    mod = mod_ref[...]
    sh1, sc1, g1, sh2, sc2, _ = [mod[i:i + 1, :] for i in range(6)]
    h = _rmsnorm(x, n1_ref[...]) * (1.0 + sc1) + sh1
    z = _dot(h.astype(BF16), win_ref[...])
    u, gate_b, gate_c, val = [z[:, i * dp:(i + 1) * dp] for i in range(4)]

    ubuf[POOL_HALO:POOL_HALO + tt, :] = u
    pos = lax.broadcasted_iota(I32, (tt, gw), 0) + t * tt
    pooled = []
    for g, w in enumerate(POOL_WINDOWS):
        cols = slice(g * gw, (g + 1) * gw)
        acc = u[:, cols]
        for j in range(1, w):
            acc = acc + ubuf[POOL_HALO - j:POOL_HALO - j + tt, cols]
        cnt = jnp.minimum(pos + 1, w).astype(F32)
        pooled.append(acc / cnt - u[:, cols])
    pool_in = jnp.concatenate(pooled, axis=-1)

    v = gate_c * val
    vbuf[CONV_HALO:CONV_HALO + tt, :] = v
    wc = wconv_ref[...]
    y = (wc[0:1, :] * vbuf[CONV_HALO - 2:CONV_HALO - 2 + tt, :]
         + wc[1:2, :] * vbuf[CONV_HALO - 1:CONV_HALO - 1 + tt, :]
         + wc[2:3, :] * v)
    conv_out = gate_b * y

    x1, h2, logits = _mix_tail(x, pool_in, conv_out, g1, sc2, sh2, n2_ref[...], wpool_ref,
                               pscale_ref[...], wout_ref[...], wr_ref[...], br_ref[...])
    x1_ref[...] = x1
    _store_token_tiles(h2_ref, h2)
    lg_ref[...] = logits

    ubuf[0:POOL_HALO, :] = ubuf[tt:tt + POOL_HALO, :]
    vbuf[0:CONV_HALO, :] = vbuf[tt:tt + CONV_HALO, :]

    @pl.when(t == tiles_per_seq - 1)
    def _():
        upool_ref[...] = ubuf[0:POOL_HALO, :]
        vconv_ref[...] = vbuf[0:CONV_HALO, :]


def _mixer_prompt(x, mod_p, n1, n2, w_in, w_pool, pscale, w_conv, w_out, w_r, b_r, h2_s):
    b, t, d = x.shape
    dp = w_pool.shape[0] * w_pool.shape[1]
    tt = min(TOKEN_TILE, t)
    nt = t // tt
    n_real = b * nt
    assert h2_s.shape[0] == tt * SUBLANES
    seq = lambda i: jnp.minimum(i, n_real - 1) // nt
    tile = lambda i: jnp.minimum(i, n_real - 1)
    const2 = lambda i: (0, 0)
    const3 = lambda i: (0, 0, 0)
    return pl.pallas_call(
        functools.partial(_mixer_prompt_kernel, tiles_per_seq=nt),
        out_shape=(
            jax.ShapeDtypeStruct((b * t, d), F32),
            jax.ShapeDtypeStruct(((n_real + 1) * tt * SUBLANES, LANES), F32),
            jax.ShapeDtypeStruct((N_EXPERTS, b * t), F32),
            jax.ShapeDtypeStruct((b, POOL_HALO, dp), F32),
            jax.ShapeDtypeStruct((b, CONV_HALO, dp), F32),
        ),
        grid=(n_real + 1,),
        in_specs=[
            pl.BlockSpec((None, tt, d), lambda i: (seq(i), tile(i) % nt, 0)),
            pl.BlockSpec((None, 6, d), lambda i: (seq(i), 0, 0)),
            pl.BlockSpec((1, d), const2),
            pl.BlockSpec((1, d), const2),
            pl.BlockSpec(w_in.shape, const2),
            pl.BlockSpec(w_pool.shape, const3),
            pl.BlockSpec((1, dp), const2),
            pl.BlockSpec(w_conv.shape, const2),
            pl.BlockSpec(w_out.shape, const2),
            pl.BlockSpec(w_r.shape, const2),
            pl.BlockSpec((N_EXPERTS, 1), const2),
            pl.BlockSpec(h2_s.shape, const2),
        ],
        out_specs=(
            pl.BlockSpec((tt, d), lambda i: (tile(i), 0)),
            pl.BlockSpec((tt * SUBLANES, LANES), lambda i: (i, 0)),
            pl.BlockSpec((N_EXPERTS, tt), lambda i: (0, tile(i))),
            pl.BlockSpec((None, POOL_HALO, dp), lambda i: (seq(i), 0, 0)),
            pl.BlockSpec((None, CONV_HALO, dp), lambda i: (seq(i), 0, 0)),
        ),
        scratch_shapes=[
            pltpu.VMEM((POOL_HALO + tt, dp), F32),
            pltpu.VMEM((CONV_HALO + tt, dp), F32),
        ],
        compiler_params=pltpu.CompilerParams(
            dimension_semantics=("arbitrary",), vmem_limit_bytes=VMEM_LIMIT),
        name="mixer_prompt",
    )(x, mod_p, n1, n2, w_in, w_pool, pscale, w_conv, w_out, w_r, b_r, h2_s)


def _mixer_sample_kernel(x_ref, mod_ref, pstate_ref, cstate_ref, n1_ref, n2_ref, win_f32, wpool_f32,
                         pscale_ref, wconv_ref, wout_f32, wr_ref, br_ref,
                         x1_ref, h2_ref, lg_ref, newp_ref, newc_ref, win_ref, wpool_ref, wout_ref,
                         *, steps):
    win_ref[...] = win_f32[...].astype(BF16)
    wpool_ref[...] = wpool_f32[...].astype(BF16)
    wout_ref[...] = wout_f32[...].astype(BF16)
    nb = pstate_ref.shape[1]
    d = x_ref.shape[1]
    dp = pstate_ref.shape[2]
    gw = dp // len(POOL_WINDOWS)
    n_hist = pstate_ref.shape[0]
    n_chist = cstate_ref.shape[0]

    x = x_ref[...]
    mod = mod_ref[0:nb, :]
    rep = lambda a: jnp.concatenate([a] * steps, axis=0)
    sh1, sc1, g1, sh2, sc2, _ = [rep(mod[:, i * d:(i + 1) * d]) for i in range(6)]
    h = _rmsnorm(x, n1_ref[...]) * (1.0 + sc1) + sh1
    z = _dot(h.astype(BF16), win_ref[...])
    u, gate_b, gate_c, val = [z[:, i * dp:(i + 1) * dp] for i in range(4)]

    ext = [pstate_ref[i] for i in range(n_hist)] + [u[s * nb:(s + 1) * nb, :] for s in range(steps)]
    pooled_steps = []
    for s in range(steps):
        groups = []
        for g, w in enumerate(POOL_WINDOWS):
            cols = slice(g * gw, (g + 1) * gw)
            acc = ext[n_hist + s][:, cols]
            for j in range(1, w):
                acc = acc + ext[n_hist + s - j][:, cols]
            cnt = float(min(PAST_LEN + s + 1, w))
            groups.append(acc / cnt - ext[n_hist + s][:, cols])
        pooled_steps.append(jnp.concatenate(groups, axis=-1))
    pool_in = jnp.concatenate(pooled_steps, axis=0)

    v = gate_c * val
    vext = [cstate_ref[i] for i in range(n_chist)] + [v[s * nb:(s + 1) * nb, :] for s in range(steps)]
    wc = wconv_ref[...]
    y = jnp.concatenate(
        [wc[0:1, :] * vext[s] + wc[1:2, :] * vext[s + 1] + wc[2:3, :] * vext[s + 2] for s in range(steps)],
        axis=0)
    conv_out = gate_b * y

    x1, h2, logits = _mix_tail(x, pool_in, conv_out, g1, sc2, sh2, n2_ref[...], wpool_ref,
                               pscale_ref[...], wout_ref[...], wr_ref[...], br_ref[...])
    x1_ref[...] = x1
    _store_token_tiles(h2_ref, h2)
    lg_ref[...] = logits
    for i in range(n_hist):
        newp_ref[i] = ext[steps + i]
    for i in range(n_chist):
        newc_ref[i] = vext[steps + i]


def _mixer_sample(x_tm, mod_s, pstate_tm, cstate_tm, n1, n2, w_in, w_pool, pscale, w_conv, w_out, w_r, b_r,
                  steps):
    rows, d = x_tm.shape
    return pl.pallas_call(
        functools.partial(_mixer_sample_kernel, steps=steps),
        out_shape=(
            jax.ShapeDtypeStruct((rows, d), F32),
            jax.ShapeDtypeStruct((rows * SUBLANES, LANES), F32),
            jax.ShapeDtypeStruct((N_EXPERTS, rows), F32),
            jax.ShapeDtypeStruct(pstate_tm.shape, F32),
            jax.ShapeDtypeStruct(cstate_tm.shape, F32),
            jax.ShapeDtypeStruct(w_in.shape, BF16),
            jax.ShapeDtypeStruct(w_pool.shape, BF16),
            jax.ShapeDtypeStruct(w_out.shape, BF16),
        ),
        compiler_params=pltpu.CompilerParams(vmem_limit_bytes=VMEM_LIMIT),
        name="mixer_sample",
    )(x_tm, mod_s, pstate_tm, cstate_tm, n1, n2, w_in, w_pool, pscale, w_conv, w_out, w_r, b_r)


def _route_kernel(lgp_ref, lgs_ref, dest_ref, gate_ref, cnt_ref, counts, start, before):
    ne = lgp_ref.shape[0]
    tr = before.shape[0]
    reps = tr // LANES
    n_prompt_chunks = lgp_ref.shape[1] // tr
    n_sample_chunks = lgs_ref.shape[1] // tr
    eidx = lax.broadcasted_iota(I32, (ne, tr), 0)

    def top_k(ref, c):
        work = ref[:, pl.ds(pl.multiple_of(c * tr, tr), tr)]
        top_v, onehots = [], []
        for _ in range(TOP_K):
            m = jnp.max(work, axis=0, keepdims=True)
            idx = jnp.min(jnp.where(work == m, eidx, ne), axis=0, keepdims=True)
            sel = eidx == idx
            top_v.append(m)
            onehots.append(sel)
            work = jnp.where(sel, -jnp.inf, work)
        mask = jnp.where(onehots[0] | onehots[1] | onehots[2] | onehots[3], 1.0, 0.0)
        chunk_counts = jnp.broadcast_to(jnp.sum(mask, axis=1, keepdims=True), (ne, LANES))
        return top_v, onehots, mask, chunk_counts

    def count_chunk(ref):
        def body(c, carry):
            counts[...] = counts[...] + top_k(ref, c)[3]
            return carry
        return body

    counts[...] = jnp.zeros_like(counts)
    lax.fori_loop(0, n_prompt_chunks, count_chunk(lgp_ref), 0)
    lax.fori_loop(0, n_sample_chunks, count_chunk(lgs_ref), 0)

    total = counts[...]
    hi = jnp.floor(total * (1.0 / 256.0))
    lo = total - hi * 256.0
    r = lax.broadcasted_iota(I32, (ne, ne), 0)
    col = lax.broadcasted_iota(I32, (ne, ne), 1)
    lower = jnp.where(col < r, 1.0, 0.0).astype(BF16)
    start[...] = 256.0 * _dot(lower, hi.astype(BF16)) + _dot(lower, lo.astype(BF16))
    cnt_ref[...] = total
    counts[...] = jnp.zeros_like(counts)

    r = lax.broadcasted_iota(I32, (tr, tr), 0)
    col = lax.broadcasted_iota(I32, (tr, tr), 1)
    before[...] = jnp.where(r < col, 1.0, 0.0).astype(BF16)
    gate_ref[...] = jnp.zeros_like(gate_ref)

    def place_chunk(ref, first_chunk):
        def body(c, carry):
            top_v, onehots, mask, chunk_counts = top_k(ref, c)
            base = jnp.concatenate([counts[...] + start[...]] * reps, axis=1)
            rank = _dot(mask.astype(BF16), before[...]) + base
            counts[...] = counts[...] + chunk_counts
            es = [jnp.exp(v - top_v[0]) for v in top_v]
            denom = es[0] + es[1] + es[2] + es[3]
            chunk = first_chunk + c
            cols = pl.ds(pl.multiple_of(chunk * tr, tr), tr)
            for k in range(TOP_K):
                d = jnp.sum(jnp.where(onehots[k], rank, 0.0), axis=0, keepdims=True).astype(I32)
                for j in range(reps):
                    dest_ref[chunk * reps + j, k:k + 1, :] = d[:, j * LANES:(j + 1) * LANES]
                gate_ref[k:k + 1, cols] = es[k] / denom
            return carry
        return body

    lax.fori_loop(0, n_prompt_chunks, place_chunk(lgp_ref, 0), 0)
    lax.fori_loop(0, n_sample_chunks, place_chunk(lgs_ref, n_prompt_chunks), 0)


def _route(lgt_p, lgt_s):
    ne, n_p = lgt_p.shape
    n_s = lgt_s.shape[1]
    tr = ROUTE_TILE
    return pl.pallas_call(
        _route_kernel,
        out_shape=(
            jax.ShapeDtypeStruct(((n_p + n_s) // LANES, TOP_K, LANES), I32),
            jax.ShapeDtypeStruct((SUBLANES, n_p + n_s), F32),
            jax.ShapeDtypeStruct((ne, LANES), F32),
        ),
        scratch_shapes=[pltpu.VMEM((ne, LANES), F32), pltpu.VMEM((ne, LANES), F32),
                        pltpu.VMEM((tr, tr), BF16)],
        compiler_params=pltpu.CompilerParams(vmem_limit_bytes=VMEM_LIMIT),
        name="route",
    )(lgt_p, lgt_s)


def _group_metadata(counts, n_rows, tile):
    n_tiles = n_rows // tile
    n_steps = n_tiles + N_EXPERTS - 1
    ends = jnp.cumsum(counts)
    offs = jnp.concatenate([jnp.zeros((1,), I32), ends]).astype(I32)
    first_tile = offs[:-1] // tile
    last_tile = (ends - 1) // tile
    tiles_e = jnp.where(counts > 0, last_tile - first_tile + 1, 0)
    step_end = jnp.cumsum(tiles_e)
    step_start = step_end - tiles_e
    n_active = step_end[-1]
    s = jnp.minimum(jnp.arange(n_steps, dtype=I32), n_active - 1)
    owner = ((s[:, None] >= step_start[None, :]) & (s[:, None] < step_end[None, :])).astype(I32)
    gid = jnp.sum(owner * jnp.arange(N_EXPERTS, dtype=I32)[None, :], axis=1)
    tid = jnp.sum(owner * (first_tile - step_start)[None, :], axis=1) + s
    ids = jnp.arange(N_EXPERTS, dtype=I32)
    later = (ids[None, :] > ids[:, None]) & (counts[None, :] > 0)
    next_e = jnp.min(jnp.where(later, ids[None, :], N_EXPERTS), axis=1)
    next_e = jnp.where(next_e == N_EXPERTS, -1, next_e)
    nxt = jnp.sum(owner * next_e[None, :], axis=1)
    return gid, tid, nxt, offs, n_active.reshape(1).astype(I32), n_steps


SC_LANES = 16
SC_INDEX_BATCH = 128


def _invert(dest):
    r = dest.shape[0]
    mesh = plsc.VectorSubcoreMesh(core_axis_name="core", subcore_axis_name="subcore")
    per = r // mesh.num_subcores
    assert per * mesh.num_subcores == r and per % SC_INDEX_BATCH == 0

    @functools.partial(
        pl.kernel, mesh=mesh, out_type=jax.ShapeDtypeStruct((r,), I32),
        scratch_types=[pltpu.VMEM_SHARED((r,), I32), pltpu.VMEM((per,), I32), pltpu.VMEM((per,), I32)],
        compiler_params=pltpu.CompilerParams(needs_layout_passes=False),
        name="invert",
    )
    def invert(dest_hbm, inv_hbm, table, idx, ids):
        @pl.when(lax.axis_index("core") == 0)
        def _():
            base = lax.axis_index("subcore") * per
            pltpu.sync_copy(dest_hbm.at[pl.ds(base, per)], idx)
            lane = lax.iota(I32, SC_LANES)

            @pl.loop(0, per // SC_LANES)
            def _(i):
                p = base + i * SC_LANES + lane
                tok = lax.shift_right_logical(p, 9) * LANES + (p & (LANES - 1))
                slot = lax.shift_right_logical(p, 7) & (TOP_K - 1)
                ids[pl.ds(i * SC_LANES, SC_LANES)] = tok * TOP_K + slot

            @pl.loop(0, per // SC_INDEX_BATCH)
            def _(j):
                span = pl.ds(j * SC_INDEX_BATCH, SC_INDEX_BATCH)
                pltpu.sync_copy(ids.at[span], table.at[idx.at[span]])

            plsc.subcore_barrier()
            pltpu.sync_copy(table.at[pl.ds(base, per)], inv_hbm.at[pl.ds(base, per)])

    return invert(dest)


def _experts_kernel(gid_ref, tid_ref, nxt_ref, offs_ref, nact_ref,
                    dstp_ref, dst0_ref, src0_ref, src1_ref, src2_ref, bg_ref, bu_ref, bd_ref,
                    h2_hbm, wg_hbm, wu_hbm, wd_hbm, y4_hbm,
                    wg_b, wu_b, wd_b, wg_f, wu_f, wd_f, relay, rows, ybuf, sems, row_sems, y_sems):
    s = pl.program_id(0)
    tile_rows = rows.shape[1]
    tm = tile_rows // SUBLANES
    landing = ((wg_hbm, wg_f, wg_b), (wu_hbm, wu_f, wu_b), (wd_hbm, wd_f, wd_b))

    def fetch(e):
        for j, (hbm, land, _) in enumerate(landing):
            pltpu.make_async_copy(hbm.at[e], land, sems.at[j]).start()

    def row_in(idx_ref, into, r):
        src = h2_hbm.at[pl.ds(pl.multiple_of(idx_ref[r], SUBLANES), SUBLANES)]
        dst = rows.at[into, pl.ds(pl.multiple_of(r * SUBLANES, SUBLANES), SUBLANES)]
        return pltpu.make_async_copy(src, dst, row_sems.at[into])

    def row_out(idx_ref, frm, r):
        src = ybuf.at[frm, pl.ds(pl.multiple_of(r * SUBLANES, SUBLANES), SUBLANES)]
        dst = y4_hbm.at[pl.ds(pl.multiple_of(idx_ref[r], SUBLANES), SUBLANES)]
        return pltpu.make_async_copy(src, dst, y_sems.at[frm])

    def in_line(make, idx_ref, buf, first=0, count=None):
        for r in range(first, tm if count is None else first + count):
            make(idx_ref, buf, r).start(priority=r % 2)

    def in_loop(make, idx_ref, buf):
        def group(g, carry):
            for j in range(ISSUE_GROUP):
                make(idx_ref, buf, g * ISSUE_GROUP + j).start(priority=j % 2)
            return carry
        lax.fori_loop(0, tm // ISSUE_GROUP, group, 0)

    def wait_rows(buf):
        pltpu.make_async_copy(h2_hbm.at[pl.ds(0, tile_rows)], rows.at[buf], row_sems.at[buf]).wait()

    def wait_y(buf):
        pltpu.make_async_copy(ybuf.at[buf], y4_hbm.at[pl.ds(0, tile_rows)], y_sems.at[buf]).wait()

    @pl.when(s < nact_ref[0])
    def _():
        e = gid_ref[s]
        m = tid_ref[s]
        cur = lax.rem(m, 3)
        before = lax.rem(m + 2, 3)
        last = nact_ref[0] - 1
        new_tile = (s == 0) | (tid_ref[jnp.maximum(s - 1, 0)] != m)

        @pl.when(s == 0)
        def _():
            fetch(e)
            in_loop(row_in, src0_ref, 0)
            in_loop(row_in, src1_ref, 1)
            ybuf[2] = jnp.zeros(ybuf.shape[1:], F32)

        @pl.when((s == 0) | (gid_ref[jnp.maximum(s - 1, 0)] != e))
        def _():
            for j, (hbm, land, half) in enumerate(landing):
                pltpu.make_async_copy(hbm.at[e], land, sems.at[j]).wait()
                half[...] = land[...].astype(BF16)

            @pl.when(nxt_ref[s] >= 0)
            def _():
                fetch(nxt_ref[s])

        @pl.when(new_tile)
        def _():
            wait_rows(cur)

            @pl.when(m >= 2)
            def _():
                wait_y(cur)

        def ffn(load_x, after_gate=lambda: None):
            g = _dot(load_x(), wg_b[...]) + bg_ref[pl.ds(e, 1), :]
            after_gate()
            u = _dot(load_x(), wu_b[...]) + bu_ref[pl.ds(e, 1), :]
            g = jnp.minimum(g, SWIGLU_LIMIT)
            u = jnp.clip(u, -SWIGLU_LIMIT, SWIGLU_LIMIT)
            glu = g * jax.nn.sigmoid(SWIGLU_ALPHA * g)
            return _dot(((u + 1.0) * glu).astype(BF16), wd_b[...]) + bd_ref[pl.ds(e, 1), :]

        lo = offs_ref[e]
        hi = offs_ref[e + 1]
        whole_tile = (lo <= m * tm) & (hi >= (m + 1) * tm)

        def ffn_and_copies(load_x, first, count):
            pinned = count // 2
            y = ffn(load_x, lambda: in_line(row_in, src2_ref, before, first, pinned))
            in_line(row_in, src2_ref, before, first + pinned, count - pinned)
            in_line(row_out, dstp_ref, before, first, count)
            return y

        @pl.when(whole_tile)
        def _():
            y = ffn_and_copies(lambda: _load_token_tiles(rows.at[cur]).astype(BF16), 0, tm)
            _store_token_tiles(ybuf.at[cur], y)

        @pl.when(jnp.logical_not(whole_tile))
        def _():
            sub = relay.shape[0] // SUBLANES
            for j in range(tm // sub):
                first = m * tm + j * sub
                span = pl.ds(j * sub * SUBLANES, sub * SUBLANES)
                load_x = lambda span=span: _load_token_tiles(rows.at[cur, span]).astype(BF16)

                def put(y, first=first, span=span):
                    _store_token_tiles(relay, y)
                    row = first + lax.shift_right_logical(lax.broadcasted_iota(I32, relay.shape, 0), 3)
                    pltpu.store(ybuf.at[cur, span], relay[...], mask=(row >= lo) & (row < hi))

                evaluate = (lo < first + sub) & (hi > first)
                owns_end = hi >= first + sub

                @pl.when(evaluate & owns_end)
                def _():
                    put(ffn_and_copies(load_x, j * sub, sub))

                @pl.when(evaluate & jnp.logical_not(owns_end))
                def _():
                    put(ffn(load_x))

        @pl.when(s == last)
        def _():
            in_loop(row_out, dst0_ref, cur)
            wait_rows(lax.rem(m + 1, 3))
            wait_rows(before)
            for buf in range(3):
                wait_y(buf)


def _experts(gid, tid, nxt, offs, nact, n_steps, inv, h2, wg, bg, wu, bu, wd, bd):
    ne, d, f = wg.shape
    tm = EXPERT_TILE
    n_tokens = h2.shape[0] // SUBLANES
    n_tiles = inv.shape[0] // tm
    assert n_tiles >= 2
    slot_rows = (n_tokens + tm // TOP_K) * SUBLANES
    inv_ext = jnp.concatenate([n_tokens * TOP_K + jnp.arange(tm, dtype=I32), inv])
    tok = lax.shift_right_logical(inv_ext, TOP_K.bit_length() - 1)
    src_row = tok * SUBLANES
    dst_row = (inv_ext & (TOP_K - 1)) * slot_rows + src_row
    whole = lambda s, gid, tid, nxt, offs, nact: (0, 0)

    def order_of(k):
        return pl.BlockSpec(
            (tm,), lambda s, gid, tid, nxt, offs, nact: (jnp.minimum(tid[s] + k, n_tiles - 1) + 1,),
            memory_space=pltpu.SMEM)

    grid_spec = pltpu.PrefetchScalarGridSpec(
        num_scalar_prefetch=5,
        grid=(n_steps,),
        in_specs=[
            order_of(-1), order_of(0), order_of(0), order_of(1), order_of(2),
            pl.BlockSpec((ne, f), whole),
            pl.BlockSpec((ne, f), whole),
            pl.BlockSpec((ne, d), whole),
            pl.BlockSpec(memory_space=pl.ANY),
            pl.BlockSpec(memory_space=pl.ANY),
            pl.BlockSpec(memory_space=pl.ANY),
            pl.BlockSpec(memory_space=pl.ANY),
        ],
        out_specs=pl.BlockSpec(memory_space=pl.ANY),
        scratch_shapes=[
            pltpu.VMEM((d, f), BF16), pltpu.VMEM((d, f), BF16), pltpu.VMEM((f, d), BF16),
            pltpu.VMEM((d, f), F32), pltpu.VMEM((d, f), F32), pltpu.VMEM((f, d), F32),
            pltpu.VMEM((EXPERT_SUBTILE * SUBLANES, LANES), F32),
            pltpu.VMEM((3, tm * SUBLANES, LANES), F32),
            pltpu.VMEM((3, tm * SUBLANES, LANES), F32),
            pltpu.SemaphoreType.DMA((3,)),
            pltpu.SemaphoreType.DMA((3,)),
            pltpu.SemaphoreType.DMA((3,)),
        ],
    )
    y4 = pl.pallas_call(
        _experts_kernel,
        out_shape=jax.ShapeDtypeStruct((TOP_K * slot_rows, LANES), F32),
        grid_spec=grid_spec,
        compiler_params=pltpu.CompilerParams(
            dimension_semantics=("arbitrary",), vmem_limit_bytes=VMEM_LIMIT),
        name="experts",
    )(gid, tid, nxt, offs, nact, dst_row, dst_row, src_row, src_row, src_row, bg, bu, bd, h2, wg, wu, wd)
    return y4.reshape(TOP_K, slot_rows, LANES)


def _combine_kernel(gate_ref, y4_ref, x1p_ref, x1s_ref, modp_ref, g2s_ref, fn_ref, outp_ref, outs_ref,
                    *, n_prompt_tiles):
    i = pl.program_id(0)
    tm = x1p_ref.shape[0]
    gates = gate_ref[...].T
    cols = []
    for c in range(SUBLANES):
        acc = gates[:, 0:1] * y4_ref[0, pl.ds(c, tm, stride=SUBLANES), :]
        for k in range(1, TOP_K):
            acc = acc + gates[:, k:k + 1] * y4_ref[k, pl.ds(c, tm, stride=SUBLANES), :]
        cols.append(acc)
    ffn = jnp.concatenate(cols, axis=-1)

    @pl.when(i < n_prompt_tiles)
    def _():
        g2 = modp_ref[...][5:6, :]
        outp_ref[...] = _rmsnorm(x1p_ref[...] + g2 * ffn, fn_ref[...])

    @pl.when(i >= n_prompt_tiles)
    def _():
        g2 = jnp.concatenate([g2s_ref[...]] * (tm // g2s_ref.shape[0]), axis=0)
        outs_ref[...] = _rmsnorm(x1s_ref[...] + g2 * ffn, fn_ref[...])


def _combine(gates, y4, x1_p, x1_s, mod_p, mod, n_batch_s, final_norm, tokens_per_seq):
    n_p, d = x1_p.shape
    n_s = x1_s.shape[0]
    tm = COMBINE_TILE
    assert tm % n_batch_s == 0 and n_batch_s % SUBLANES == 0
    npt, nst = n_p // tm, n_s // tm
    tiles_per_seq = tokens_per_seq // tm
    pmap = lambda i: (jnp.minimum(i, npt - 1), 0)
    smap = lambda i: (jnp.maximum(i - npt, 0), 0)
    return pl.pallas_call(
        functools.partial(_combine_kernel, n_prompt_tiles=npt),
        out_shape=(jax.ShapeDtypeStruct((n_p, d), F32), jax.ShapeDtypeStruct((n_s, d), F32)),
        grid=(npt + nst,),
        in_specs=[
            pl.BlockSpec((SUBLANES, tm), lambda i: (0, i)),
            pl.BlockSpec((TOP_K, tm * SUBLANES, LANES), lambda i: (0, i, 0)),
            pl.BlockSpec((tm, d), pmap),
            pl.BlockSpec((tm, d), smap),
            pl.BlockSpec((None, 6, d), lambda i: (jnp.minimum(i, npt - 1) // tiles_per_seq, 0, 0)),
            pl.BlockSpec((n_batch_s, d), lambda i: (0, 5)),
            pl.BlockSpec((1, d), lambda i: (0, 0)),
        ],
        out_specs=(pl.BlockSpec((tm, d), pmap), pl.BlockSpec((tm, d), smap)),
        compiler_params=pltpu.CompilerParams(
            dimension_semantics=("arbitrary",), vmem_limit_bytes=VMEM_LIMIT),
        name="combine",
    )(gates, y4, x1_p, x1_s, mod_p, mod, final_norm)


def kernel(x_prompt, x_sample, state_pool, state_conv, c_prompt, c_sample, norm1, norm2, w_ada, b_ada,
           w_in, w_pool, pool_scale, w_conv, w_out, w_router, b_router, w_gate, b_gate, w_up, b_up,
           w_down, b_down, final_norm):
    depth = norm1.shape[0]
    assert depth == 1, "single-layer step"
    bp, tp, d = x_prompt.shape
    bs, ts, _ = x_sample.shape
    dp = state_pool.shape[-1]
    n_hist = state_pool.shape[2]
    n_chist = state_conv.shape[2]
    n_p, n_s = bp * tp, bs * ts
    assert d == SUBLANES * LANES, "token-tile layout assumes one vreg tile per token row"
    assert tp % TOKEN_TILE == 0 and n_s == TOKEN_TILE, "the sample group fills exactly one token tile"
    assert tp % COMBINE_TILE == 0 and n_s % COMBINE_TILE == 0
    assert (n_p + n_s) % ROUTE_TILE == 0 and ((n_p + n_s) * TOP_K) % EXPERT_TILE == 0

    l = 0
    n1 = norm1[l].reshape(1, d)
    n2 = norm2[l].reshape(1, d)
    pscale = pool_scale[l].reshape(1, dp)
    w_r = w_router[l].T.astype(BF16)
    b_r = b_router[l].reshape(N_EXPERTS, 1)

    mod = _adaln(jnp.concatenate([c_sample, c_prompt], axis=0), w_ada[l], b_ada[l])
    mod_p = mod[bs:].reshape(bp, 6, d)

    xs_tm = jnp.transpose(x_sample, (1, 0, 2)).reshape(n_s, d)
    ps_tm = jnp.transpose(state_pool[l], (1, 0, 2))
    cs_tm = jnp.transpose(state_conv[l], (1, 0, 2))
    x1_s, h2_s, lg_s, newp_tm, newc_tm, w_in_b, w_pool_b, w_out_b = _mixer_sample(
        xs_tm, mod, ps_tm, cs_tm, n1, n2, w_in[l], w_pool[l], pscale, w_conv[l], w_out[l], w_r, b_r, ts)

    x1_p, h2, lg_p, u_tail, v_tail = _mixer_prompt(
        x_prompt, mod_p, n1, n2, w_in_b, w_pool_b, pscale, w_conv[l], w_out_b, w_r, b_r, h2_s)

    dest, gates, counts_f = _route(lg_p, lg_s)
    counts = counts_f[:, 0].astype(I32)
    n_rows = (n_p + n_s) * TOP_K
    gid, tid, nxt, offs, nact, n_steps = _group_metadata(counts, n_rows, EXPERT_TILE)

    dest = dest.reshape(-1)
    inv = _invert(dest)
    y4 = _experts(gid, tid, nxt, offs, nact, n_steps, inv, h2,
                  w_gate[l], b_gate[l], w_up[l], b_up[l], w_down[l], b_down[l])

    y_p, y_s = _combine(gates, y4, x1_p, x1_s, mod_p, mod, bs, final_norm.reshape(1, d), tp)

    y_prompt = y_p.reshape(bp, tp, d)
    y_sample = jnp.transpose(y_s.reshape(ts, bs, d), (1, 0, 2))
    new_pool_prompt = u_tail[:, POOL_HALO - n_hist:, :][None]
    new_conv_prompt = v_tail[:, CONV_HALO - n_chist:, :][None]
    new_pool_sample = jnp.transpose(newp_tm, (1, 0, 2))[None]
    new_conv_sample = jnp.transpose(newc_tm, (1, 0, 2))[None]
    return (y_prompt, y_sample, new_pool_prompt, new_conv_prompt, new_pool_sample, new_conv_sample)
```

```python
import functools

import jax
import jax.numpy as jnp
from jax import lax
from jax.experimental import pallas as pl
from jax.experimental.pallas import tpu as pltpu
from jax.experimental.pallas import tpu_sc as plsc

F32 = jnp.float32
BF16 = jnp.bfloat16
I32 = jnp.int32

POOL_WINDOWS = (2, 4, 8, 16)
POOL_HALO = 16
CONV_TAPS = 3
CONV_HALO = 8
N_EXPERTS = 32
TOP_K = 4
SWIGLU_LIMIT = 7.0
SWIGLU_ALPHA = 1.702
EPS = 1e-5
PAST_LEN = 16384

LANES = 128
SUBLANES = 8

TOKEN_TILE = 512
ROUTE_TILE = 512
COMBINE_TILE = 512
EXPERT_TILE = 512
EXPERT_SUBTILE = 128
VMEM_LIMIT = 56 * 1024 * 1024


def _rmsnorm(x, g):
    ms = jnp.mean(x * x, axis=-1, keepdims=True)
    return x * lax.rsqrt(ms + EPS) * g


def _dot(a, b):
    return jnp.dot(a, b, preferred_element_type=F32)


def _store_token_tiles(ref, val):
    rows = val.shape[0]
    for c in range(SUBLANES):
        ref[pl.ds(c, rows, stride=SUBLANES), :] = val[:, c * LANES:(c + 1) * LANES]


def _load_token_tiles(ref):
    rows = ref.shape[0] // SUBLANES
    return jnp.concatenate([ref[pl.ds(c, rows, stride=SUBLANES), :] for c in range(SUBLANES)], axis=-1)


ISSUE_GROUP = 4


def _adaln_kernel(c_ref, w_ref, b_ref, o_ref):
    c = c_ref[...]
    s = c * jax.nn.sigmoid(c)
    o_ref[...] = _dot(s.astype(BF16), w_ref[...].astype(BF16)) + b_ref[...]


def _adaln(c, w_ada, b_ada):
    rows, d = c.shape
    n = w_ada.shape[1]
    tn = 1024
    return pl.pallas_call(
        _adaln_kernel,
        out_shape=jax.ShapeDtypeStruct((rows, n), F32),
        grid=(n // tn,),
        in_specs=[
            pl.BlockSpec((rows, d), lambda j: (0, 0)),
            pl.BlockSpec((d, tn), lambda j: (0, j)),
            pl.BlockSpec((1, tn), lambda j: (0, j)),
        ],
        out_specs=pl.BlockSpec((rows, tn), lambda j: (0, j)),
        compiler_params=pltpu.CompilerParams(
            dimension_semantics=("arbitrary",), vmem_limit_bytes=VMEM_LIMIT),
        name="adaln",
    )(c, w_ada, b_ada.reshape(1, n))


def _mix_tail(x, pool_in, conv_out, g1, sc2, sh2, n2, wpool_ref, pscale, wout, wr, br):
    gw = pool_in.shape[1] // len(POOL_WINDOWS)
    mixed = [_dot(pool_in[:, g * gw:(g + 1) * gw].astype(BF16), wpool_ref[g])
             for g in range(len(POOL_WINDOWS))]
    pool_out = jnp.concatenate(mixed, axis=-1) * pscale
    mix_in = jnp.concatenate([pool_out, conv_out], axis=-1).astype(BF16)
    x1 = x + g1 * _dot(mix_in, wout)
    h2 = _rmsnorm(x1, n2) * (1.0 + sc2) + sh2
    logits_t = lax.dot_general(wr, h2.astype(BF16), (((1,), (1,)), ((), ())),
                               preferred_element_type=F32) + br
    return x1, h2, logits_t


def _mixer_prompt_kernel(x_ref, mod_ref, n1_ref, n2_ref, win_ref, wpool_ref, pscale_ref, wconv_ref,
                         wout_ref, wr_ref, br_ref, h2s_ref,
                         x1_ref, h2_ref, lg_ref, upool_ref, vconv_ref, ubuf, vbuf, *, tiles_per_seq):
    i = pl.program_id(0)

    @pl.when(i < pl.num_programs(0) - 1)
    def _():
        _mixer_prompt_tile(x_ref, mod_ref, n1_ref, n2_ref, win_ref, wpool_ref, pscale_ref, wconv_ref,
                           wout_ref, wr_ref, br_ref, x1_ref, h2_ref, lg_ref, upool_ref, vconv_ref,
                           ubuf, vbuf, lax.rem(i, tiles_per_seq), tiles_per_seq)

    @pl.when(i == pl.num_programs(0) - 1)
    def _():
        h2_ref[...] = h2s_ref[...]


def _mixer_prompt_tile(x_ref, mod_ref, n1_ref, n2_ref, win_ref, wpool_ref, pscale_ref, wconv_ref,
                       wout_ref, wr_ref, br_ref, x1_ref, h2_ref, lg_ref, upool_ref, vconv_ref,
                       ubuf, vbuf, t, tiles_per_seq):
    tt = x_ref.shape[0]
    dp = ubuf.shape[1]
    gw = dp // len(POOL_WINDOWS)

    @pl.when(t == 0)
    def _():
        ubuf[0:POOL_HALO, :] = jnp.zeros((POOL_HALO, dp), F32)
        vbuf[0:CONV_HALO, :] = jnp.zeros((CONV_HALO, dp), F32)

    x = x_ref[...]
    mod = mod_ref[...]
    sh1, sc1, g1, sh2, sc2, _ = [mod[i:i + 1, :] for i in range(6)]
    h = _rmsnorm(x, n1_ref[...]) * (1.0 + sc1) + sh1
    z = _dot(h.astype(BF16), win_ref[...])
    u, gate_b, gate_c, val = [z[:, i * dp:(i + 1) * dp] for i in range(4)]

    ubuf[POOL_HALO:POOL_HALO + tt, :] = u
    pos = lax.broadcasted_iota(I32, (tt, gw), 0) + t * tt
    pooled = []
    for g, w in enumerate(POOL_WINDOWS):
        cols = slice(g * gw, (g + 1) * gw)
        acc = u[:, cols]
        for j in range(1, w):
            acc = acc + ubuf[POOL_HALO - j:POOL_HALO - j + tt, cols]
        cnt = jnp.minimum(pos + 1, w).astype(F32)
        pooled.append(acc / cnt - u[:, cols])
    pool_in = jnp.concatenate(pooled, axis=-1)

    v = gate_c * val
    vbuf[CONV_HALO:CONV_HALO + tt, :] = v
    wc = wconv_ref[...]
    y = (wc[0:1, :] * vbuf[CONV_HALO - 2:CONV_HALO - 2 + tt, :]
         + wc[1:2, :] * vbuf[CONV_HALO - 1:CONV_HALO - 1 + tt, :]
         + wc[2:3, :] * v)
    conv_out = gate_b * y

    x1, h2, logits = _mix_tail(x, pool_in, conv_out, g1, sc2, sh2, n2_ref[...], wpool_ref,
                               pscale_ref[...], wout_ref[...], wr_ref[...], br_ref[...])
    x1_ref[...] = x1
    _store_token_tiles(h2_ref, h2)
    lg_ref[...] = logits

    ubuf[0:POOL_HALO, :] = ubuf[tt:tt + POOL_HALO, :]
    vbuf[0:CONV_HALO, :] = vbuf[tt:tt + CONV_HALO, :]

    @pl.when(t == tiles_per_seq - 1)
    def _():
        upool_ref[...] = ubuf[0:POOL_HALO, :]
        vconv_ref[...] = vbuf[0:CONV_HALO, :]


def _mixer_prompt(x, mod_p, n1, n2, w_in, w_pool, pscale, w_conv, w_out, w_r, b_r, h2_s):
    b, t, d = x.shape
    dp = w_pool.shape[0] * w_pool.shape[1]
    tt = min(TOKEN_TILE, t)
    nt = t // tt
    n_real = b * nt
    assert h2_s.shape[0] == tt * SUBLANES
    seq = lambda i: jnp.minimum(i, n_real - 1) // nt
    tile = lambda i: jnp.minimum(i, n_real - 1)
    const2 = lambda i: (0, 0)
    const3 = lambda i: (0, 0, 0)
    return pl.pallas_call(
        functools.partial(_mixer_prompt_kernel, tiles_per_seq=nt),
        out_shape=(
            jax.ShapeDtypeStruct((b * t, d), F32),
            jax.ShapeDtypeStruct(((n_real + 1) * tt * SUBLANES, LANES), F32),
            jax.ShapeDtypeStruct((N_EXPERTS, b * t), F32),
            jax.ShapeDtypeStruct((b, POOL_HALO, dp), F32),
            jax.ShapeDtypeStruct((b, CONV_HALO, dp), F32),
        ),
        grid=(n_real + 1,),
        in_specs=[
            pl.BlockSpec((None, tt, d), lambda i: (seq(i), tile(i) % nt, 0)),
            pl.BlockSpec((None, 6, d), lambda i: (seq(i), 0, 0)),
            pl.BlockSpec((1, d), const2),
            pl.BlockSpec((1, d), const2),
            pl.BlockSpec(w_in.shape, const2),
            pl.BlockSpec(w_pool.shape, const3),
            pl.BlockSpec((1, dp), const2),
            pl.BlockSpec(w_conv.shape, const2),
            pl.BlockSpec(w_out.shape, const2),
            pl.BlockSpec(w_r.shape, const2),
            pl.BlockSpec((N_EXPERTS, 1), const2),
            pl.BlockSpec(h2_s.shape, const2),
        ],
        out_specs=(
            pl.BlockSpec((tt, d), lambda i: (tile(i), 0)),
            pl.BlockSpec((tt * SUBLANES, LANES), lambda i: (i, 0)),
            pl.BlockSpec((N_EXPERTS, tt), lambda i: (0, tile(i))),
            pl.BlockSpec((None, POOL_HALO, dp), lambda i: (seq(i), 0, 0)),
            pl.BlockSpec((None, CONV_HALO, dp), lambda i: (seq(i), 0, 0)),
        ),
        scratch_shapes=[
            pltpu.VMEM((POOL_HALO + tt, dp), F32),
            pltpu.VMEM((CONV_HALO + tt, dp), F32),
        ],
        compiler_params=pltpu.CompilerParams(
            dimension_semantics=("arbitrary",), vmem_limit_bytes=VMEM_LIMIT),
        name="mixer_prompt",
    )(x, mod_p, n1, n2, w_in, w_pool, pscale, w_conv, w_out, w_r, b_r, h2_s)


def _mixer_sample_kernel(x_ref, mod_ref, pstate_ref, cstate_ref, n1_ref, n2_ref, win_f32, wpool_f32,
                         pscale_ref, wconv_ref, wout_f32, wr_ref, br_ref,
                         x1_ref, h2_ref, lg_ref, newp_ref, newc_ref, win_ref, wpool_ref, wout_ref,
                         *, steps):
    win_ref[...] = win_f32[...].astype(BF16)
    wpool_ref[...] = wpool_f32[...].astype(BF16)
    wout_ref[...] = wout_f32[...].astype(BF16)
    nb = pstate_ref.shape[1]
    d = x_ref.shape[1]
    dp = pstate_ref.shape[2]
    gw = dp // len(POOL_WINDOWS)
    n_hist = pstate_ref.shape[0]
    n_chist = cstate_ref.shape[0]

    x = x_ref[...]
    mod = mod_ref[0:nb, :]
    rep = lambda a: jnp.concatenate([a] * steps, axis=0)
    sh1, sc1, g1, sh2, sc2, _ = [rep(mod[:, i * d:(i + 1) * d]) for i in range(6)]
    h = _rmsnorm(x, n1_ref[...]) * (1.0 + sc1) + sh1
    z = _dot(h.astype(BF16), win_ref[...])
    u, gate_b, gate_c, val = [z[:, i * dp:(i + 1) * dp] for i in range(4)]

    ext = [pstate_ref[i] for i in range(n_hist)] + [u[s * nb:(s + 1) * nb, :] for s in range(steps)]
    pooled_steps = []
    for s in range(steps):
        groups = []
        for g, w in enumerate(POOL_WINDOWS):
            cols = slice(g * gw, (g + 1) * gw)
            acc = ext[n_hist + s][:, cols]
            for j in range(1, w):
                acc = acc + ext[n_hist + s - j][:, cols]
            cnt = float(min(PAST_LEN + s + 1, w))
            groups.append(acc / cnt - ext[n_hist + s][:, cols])
        pooled_steps.append(jnp.concatenate(groups, axis=-1))
    pool_in = jnp.concatenate(pooled_steps, axis=0)

    v = gate_c * val
    vext = [cstate_ref[i] for i in range(n_chist)] + [v[s * nb:(s + 1) * nb, :] for s in range(steps)]
    wc = wconv_ref[...]
    y = jnp.concatenate(
        [wc[0:1, :] * vext[s] + wc[1:2, :] * vext[s + 1] + wc[2:3, :] * vext[s + 2] for s in range(steps)],
        axis=0)
    conv_out = gate_b * y

    x1, h2, logits = _mix_tail(x, pool_in, conv_out, g1, sc2, sh2, n2_ref[...], wpool_ref,
                               pscale_ref[...], wout_ref[...], wr_ref[...], br_ref[...])
    x1_ref[...] = x1
    _store_token_tiles(h2_ref, h2)
    lg_ref[...] = logits
    for i in range(n_hist):
        newp_ref[i] = ext[steps + i]
    for i in range(n_chist):
        newc_ref[i] = vext[steps + i]


def _mixer_sample(x_tm, mod_s, pstate_tm, cstate_tm, n1, n2, w_in, w_pool, pscale, w_conv, w_out, w_r, b_r,
                  steps):
    rows, d = x_tm.shape
    return pl.pallas_call(
        functools.partial(_mixer_sample_kernel, steps=steps),
        out_shape=(
            jax.ShapeDtypeStruct((rows, d), F32),
            jax.ShapeDtypeStruct((rows * SUBLANES, LANES), F32),
            jax.ShapeDtypeStruct((N_EXPERTS, rows), F32),
            jax.ShapeDtypeStruct(pstate_tm.shape, F32),
            jax.ShapeDtypeStruct(cstate_tm.shape, F32),
            jax.ShapeDtypeStruct(w_in.shape, BF16),
            jax.ShapeDtypeStruct(w_pool.shape, BF16),
            jax.ShapeDtypeStruct(w_out.shape, BF16),
        ),
        compiler_params=pltpu.CompilerParams(vmem_limit_bytes=VMEM_LIMIT),
        name="mixer_sample",
    )(x_tm, mod_s, pstate_tm, cstate_tm, n1, n2, w_in, w_pool, pscale, w_conv, w_out, w_r, b_r)


def _route_kernel(lgp_ref, lgs_ref, dest_ref, gate_ref, cnt_ref, counts, start, before, chosen):
    ne = lgp_ref.shape[0]
    tr = before.shape[0]
    reps = tr // LANES
    n_prompt_chunks = lgp_ref.shape[1] // tr
    n_sample_chunks = lgs_ref.shape[1] // tr
    eidx = lax.broadcasted_iota(I32, (ne, tr), 0)

    def selected(onehots):
        return jnp.where(onehots[0] | onehots[1] | onehots[2] | onehots[3], 1.0, 0.0)

    def select_chunk(ref, first_chunk):
        def body(c, carry):
            chunk = first_chunk + c
            work = ref[:, pl.ds(pl.multiple_of(c * tr, tr), tr)]
            top_v, onehots = [], []
            for k in range(TOP_K):
                m = jnp.max(work, axis=0, keepdims=True)
                idx = jnp.min(jnp.where(work == m, eidx, ne), axis=0, keepdims=True)
                chosen[chunk, k:k + 1, :] = idx
                top_v.append(m)
                onehots.append(eidx == idx)
                work = jnp.where(onehots[k], -jnp.inf, work)
            es = [jnp.exp(v - top_v[0]) for v in top_v]
            denom = es[0] + es[1] + es[2] + es[3]
            cols = pl.ds(pl.multiple_of(chunk * tr, tr), tr)
            for k in range(TOP_K):
                gate_ref[k:k + 1, cols] = es[k] / denom
            chunk_counts = jnp.sum(selected(onehots), axis=1, keepdims=True)
            counts[...] = counts[...] + jnp.broadcast_to(chunk_counts, (ne, LANES))
            return carry
        return body

    counts[...] = jnp.zeros_like(counts)
    gate_ref[...] = jnp.zeros_like(gate_ref)
    lax.fori_loop(0, n_prompt_chunks, select_chunk(lgp_ref, 0), 0)
    lax.fori_loop(0, n_sample_chunks, select_chunk(lgs_ref, n_prompt_chunks), 0)

    total = counts[...]
    hi = jnp.floor(total * (1.0 / 256.0))
    lo = total - hi * 256.0
    r = lax.broadcasted_iota(I32, (ne, ne), 0)
    col = lax.broadcasted_iota(I32, (ne, ne), 1)
    lower = jnp.where(col < r, 1.0, 0.0).astype(BF16)
    start[...] = 256.0 * _dot(lower, hi.astype(BF16)) + _dot(lower, lo.astype(BF16))
    cnt_ref[...] = total
    counts[...] = jnp.zeros_like(counts)

    r = lax.broadcasted_iota(I32, (tr, tr), 0)
    col = lax.broadcasted_iota(I32, (tr, tr), 1)
    before[...] = jnp.where(r < col, 1.0, 0.0).astype(BF16)

    def place_chunk(chunk, carry):
        onehots = [eidx == chosen[chunk, k:k + 1, :] for k in range(TOP_K)]
        mask = selected(onehots)
        base = jnp.concatenate([counts[...] + start[...]] * reps, axis=1)
        rank = _dot(mask.astype(BF16), before[...]) + base
        counts[...] = counts[...] + jnp.broadcast_to(jnp.sum(mask, axis=1, keepdims=True), (ne, LANES))
        for k in range(TOP_K):
            d = jnp.sum(jnp.where(onehots[k], rank, 0.0), axis=0, keepdims=True).astype(I32)
            for j in range(reps):
                dest_ref[chunk * reps + j, k:k + 1, :] = d[:, j * LANES:(j + 1) * LANES]
        return carry

    lax.fori_loop(0, n_prompt_chunks + n_sample_chunks, place_chunk, 0)


def _route(lgt_p, lgt_s):
    ne, n_p = lgt_p.shape
    n_s = lgt_s.shape[1]
    tr = ROUTE_TILE
    return pl.pallas_call(
        _route_kernel,
        out_shape=(
            jax.ShapeDtypeStruct(((n_p + n_s) // LANES, TOP_K, LANES), I32),
            jax.ShapeDtypeStruct((SUBLANES, n_p + n_s), F32),
            jax.ShapeDtypeStruct((ne, LANES), F32),
        ),
        scratch_shapes=[pltpu.VMEM((ne, LANES), F32), pltpu.VMEM((ne, LANES), F32),
                        pltpu.VMEM((tr, tr), BF16), pltpu.VMEM(((n_p + n_s) // tr, TOP_K, tr), I32)],
        compiler_params=pltpu.CompilerParams(vmem_limit_bytes=VMEM_LIMIT),
        name="route",
    )(lgt_p, lgt_s)


def _group_metadata(counts, n_rows, tile):
    n_tiles = n_rows // tile
    n_steps = n_tiles + N_EXPERTS - 1
    ends = jnp.cumsum(counts)
    offs = jnp.concatenate([jnp.zeros((1,), I32), ends]).astype(I32)
    first_tile = offs[:-1] // tile
    last_tile = (ends - 1) // tile
    tiles_e = jnp.where(counts > 0, last_tile - first_tile + 1, 0)
    step_end = jnp.cumsum(tiles_e)
    step_start = step_end - tiles_e
    n_active = step_end[-1]
    s = jnp.minimum(jnp.arange(n_steps, dtype=I32), n_active - 1)
    owner = ((s[:, None] >= step_start[None, :]) & (s[:, None] < step_end[None, :])).astype(I32)
    gid = jnp.sum(owner * jnp.arange(N_EXPERTS, dtype=I32)[None, :], axis=1)
    tid = jnp.sum(owner * (first_tile - step_start)[None, :], axis=1) + s
    ids = jnp.arange(N_EXPERTS, dtype=I32)
    later = (ids[None, :] > ids[:, None]) & (counts[None, :] > 0)
    next_e = jnp.min(jnp.where(later, ids[None, :], N_EXPERTS), axis=1)
    next_e = jnp.where(next_e == N_EXPERTS, -1, next_e)
    nxt = jnp.sum(owner * next_e[None, :], axis=1)
    return gid, tid, nxt, offs, n_active.reshape(1).astype(I32), n_steps


SC_LANES = 16
SC_INDEX_BATCH = 128


def _invert(dest):
    r = dest.shape[0]
    mesh = plsc.VectorSubcoreMesh(core_axis_name="core", subcore_axis_name="subcore")
    per = r // mesh.num_subcores
    assert per * mesh.num_subcores == r and per % SC_INDEX_BATCH == 0

    @functools.partial(
        pl.kernel, mesh=mesh, out_type=jax.ShapeDtypeStruct((r,), I32),
        scratch_types=[pltpu.VMEM_SHARED((r,), I32), pltpu.VMEM((per,), I32), pltpu.VMEM((per,), I32)],
        compiler_params=pltpu.CompilerParams(needs_layout_passes=False),
        name="invert",
    )
    def invert(dest_hbm, inv_hbm, table, idx, ids):
        @pl.when(lax.axis_index("core") == 0)
        def _():
            base = lax.axis_index("subcore") * per
            pltpu.sync_copy(dest_hbm.at[pl.ds(base, per)], idx)
            lane = lax.iota(I32, SC_LANES)

            @pl.loop(0, per // SC_LANES)
            def _(i):
                p = base + i * SC_LANES + lane
                tok = lax.shift_right_logical(p, 9) * LANES + (p & (LANES - 1))
                slot = lax.shift_right_logical(p, 7) & (TOP_K - 1)
                ids[pl.ds(i * SC_LANES, SC_LANES)] = tok * TOP_K + slot

            @pl.loop(0, per // SC_INDEX_BATCH)
            def _(j):
                span = pl.ds(j * SC_INDEX_BATCH, SC_INDEX_BATCH)
                pltpu.sync_copy(ids.at[span], table.at[idx.at[span]])

            plsc.subcore_barrier()
            pltpu.sync_copy(table.at[pl.ds(base, per)], inv_hbm.at[pl.ds(base, per)])

    return invert(dest)


def _experts_kernel(gid_ref, tid_ref, nxt_ref, offs_ref, nact_ref,
                    dstp_ref, dst0_ref, src0_ref, src1_ref, src2_ref, bg_ref, bu_ref, bd_ref,
                    h2_hbm, wg_hbm, wu_hbm, wd_hbm, y4_hbm,
                    wg_b, wu_b, wd_b, wg_f, wu_f, wd_f, relay, rows, ybuf, sems, row_sems, y_sems):
    s = pl.program_id(0)
    tile_rows = rows.shape[1]
    tm = tile_rows // SUBLANES
    landing = ((wg_hbm, wg_f, wg_b), (wu_hbm, wu_f, wu_b), (wd_hbm, wd_f, wd_b))

    def fetch(e):
        for j, (hbm, land, _) in enumerate(landing):
            pltpu.make_async_copy(hbm.at[e], land, sems.at[j]).start()

    def row_in(idx_ref, into, r):
        src = h2_hbm.at[pl.ds(pl.multiple_of(idx_ref[r], SUBLANES), SUBLANES)]
        dst = rows.at[into, pl.ds(pl.multiple_of(r * SUBLANES, SUBLANES), SUBLANES)]
        return pltpu.make_async_copy(src, dst, row_sems.at[into])

    def row_out(idx_ref, frm, r):
        src = ybuf.at[frm, pl.ds(pl.multiple_of(r * SUBLANES, SUBLANES), SUBLANES)]
        dst = y4_hbm.at[pl.ds(pl.multiple_of(idx_ref[r], SUBLANES), SUBLANES)]
        return pltpu.make_async_copy(src, dst, y_sems.at[frm])

    def in_line(make, idx_ref, buf, first=0, count=None):
        for r in range(first, tm if count is None else first + count):
            make(idx_ref, buf, r).start(priority=r % 2)

    def in_loop(make, idx_ref, buf):
        def group(g, carry):
            for j in range(ISSUE_GROUP):
                make(idx_ref, buf, g * ISSUE_GROUP + j).start(priority=j % 2)
            return carry
        lax.fori_loop(0, tm // ISSUE_GROUP, group, 0)

    def wait_rows(buf):
        pltpu.make_async_copy(h2_hbm.at[pl.ds(0, tile_rows)], rows.at[buf], row_sems.at[buf]).wait()

    def wait_y(buf):
        pltpu.make_async_copy(ybuf.at[buf], y4_hbm.at[pl.ds(0, tile_rows)], y_sems.at[buf]).wait()

    @pl.when(s < nact_ref[0])
    def _():
        e = gid_ref[s]
        m = tid_ref[s]
        cur = lax.rem(m, 3)
        before = lax.rem(m + 2, 3)
        last = nact_ref[0] - 1
        new_tile = (s == 0) | (tid_ref[jnp.maximum(s - 1, 0)] != m)

        @pl.when(s == 0)
        def _():
            fetch(e)
            in_loop(row_in, src0_ref, 0)
            in_loop(row_in, src1_ref, 1)
            ybuf[2] = jnp.zeros(ybuf.shape[1:], F32)

        @pl.when((s == 0) | (gid_ref[jnp.maximum(s - 1, 0)] != e))
        def _():
            for j, (hbm, land, half) in enumerate(landing):
                pltpu.make_async_copy(hbm.at[e], land, sems.at[j]).wait()
                half[...] = land[...].astype(BF16)

            @pl.when(nxt_ref[s] >= 0)
            def _():
                fetch(nxt_ref[s])

        @pl.when(new_tile)
        def _():
            wait_rows(cur)

            @pl.when(m >= 2)
            def _():
                wait_y(cur)

        def ffn(load_x, after_gate=lambda: None):
            g = _dot(load_x(), wg_b[...]) + bg_ref[pl.ds(e, 1), :]
            after_gate()
            u = _dot(load_x(), wu_b[...]) + bu_ref[pl.ds(e, 1), :]
            g = jnp.minimum(g, SWIGLU_LIMIT)
            u = jnp.clip(u, -SWIGLU_LIMIT, SWIGLU_LIMIT)
            glu = g * jax.nn.sigmoid(SWIGLU_ALPHA * g)
            return _dot(((u + 1.0) * glu).astype(BF16), wd_b[...]) + bd_ref[pl.ds(e, 1), :]

        lo = offs_ref[e]
        hi = offs_ref[e + 1]
        whole_tile = (lo <= m * tm) & (hi >= (m + 1) * tm)

        def ffn_and_copies(load_x, first, count):
            pinned = count // 2
            y = ffn(load_x, lambda: in_line(row_in, src2_ref, before, first, pinned))
            in_line(row_in, src2_ref, before, first + pinned, count - pinned)
            in_line(row_out, dstp_ref, before, first, count)
            return y

        @pl.when(whole_tile)
        def _():
            y = ffn_and_copies(lambda: _load_token_tiles(rows.at[cur]).astype(BF16), 0, tm)
            _store_token_tiles(ybuf.at[cur], y)

        @pl.when(jnp.logical_not(whole_tile))
        def _():
            sub = relay.shape[0] // SUBLANES
            for j in range(tm // sub):
                first = m * tm + j * sub
                span = pl.ds(j * sub * SUBLANES, sub * SUBLANES)
                load_x = lambda span=span: _load_token_tiles(rows.at[cur, span]).astype(BF16)

                def put(y, first=first, span=span):
                    _store_token_tiles(relay, y)
                    row = first + lax.shift_right_logical(lax.broadcasted_iota(I32, relay.shape, 0), 3)
                    pltpu.store(ybuf.at[cur, span], relay[...], mask=(row >= lo) & (row < hi))

                evaluate = (lo < first + sub) & (hi > first)
                owns_end = hi >= first + sub

                @pl.when(evaluate & owns_end)
                def _():
                    put(ffn_and_copies(load_x, j * sub, sub))

                @pl.when(evaluate & jnp.logical_not(owns_end))
                def _():
                    put(ffn(load_x))

        @pl.when(s == last)
        def _():
            in_loop(row_out, dst0_ref, cur)
            wait_rows(lax.rem(m + 1, 3))
            wait_rows(before)
            for buf in range(3):
                wait_y(buf)


def _experts(gid, tid, nxt, offs, nact, n_steps, inv, h2, wg, bg, wu, bu, wd, bd):
    ne, d, f = wg.shape
    tm = EXPERT_TILE
    n_tokens = h2.shape[0] // SUBLANES
    n_tiles = inv.shape[0] // tm
    assert n_tiles >= 2
    slot_rows = (n_tokens + tm // TOP_K) * SUBLANES
    inv_ext = jnp.concatenate([n_tokens * TOP_K + jnp.arange(tm, dtype=I32), inv])
    tok = lax.shift_right_logical(inv_ext, TOP_K.bit_length() - 1)
    src_row = tok * SUBLANES
    dst_row = (inv_ext & (TOP_K - 1)) * slot_rows + src_row
    whole = lambda s, gid, tid, nxt, offs, nact: (0, 0)

    def order_of(k):
        return pl.BlockSpec(
            (tm,), lambda s, gid, tid, nxt, offs, nact: (jnp.minimum(tid[s] + k, n_tiles - 1) + 1,),
            memory_space=pltpu.SMEM)

    grid_spec = pltpu.PrefetchScalarGridSpec(
        num_scalar_prefetch=5,
        grid=(n_steps,),
        in_specs=[
            order_of(-1), order_of(0), order_of(0), order_of(1), order_of(2),
            pl.BlockSpec((ne, f), whole),
            pl.BlockSpec((ne, f), whole),
            pl.BlockSpec((ne, d), whole),
            pl.BlockSpec(memory_space=pl.ANY),
            pl.BlockSpec(memory_space=pl.ANY),
            pl.BlockSpec(memory_space=pl.ANY),
            pl.BlockSpec(memory_space=pl.ANY),
        ],
        out_specs=pl.BlockSpec(memory_space=pl.ANY),
        scratch_shapes=[
            pltpu.VMEM((d, f), BF16), pltpu.VMEM((d, f), BF16), pltpu.VMEM((f, d), BF16),
            pltpu.VMEM((d, f), F32), pltpu.VMEM((d, f), F32), pltpu.VMEM((f, d), F32),
            pltpu.VMEM((EXPERT_SUBTILE * SUBLANES, LANES), F32),
            pltpu.VMEM((3, tm * SUBLANES, LANES), F32),
            pltpu.VMEM((3, tm * SUBLANES, LANES), F32),
            pltpu.SemaphoreType.DMA((3,)),
            pltpu.SemaphoreType.DMA((3,)),
            pltpu.SemaphoreType.DMA((3,)),
        ],
    )
    y4 = pl.pallas_call(
        _experts_kernel,
        out_shape=jax.ShapeDtypeStruct((TOP_K * slot_rows, LANES), F32),
        grid_spec=grid_spec,
        compiler_params=pltpu.CompilerParams(
            dimension_semantics=("arbitrary",), vmem_limit_bytes=VMEM_LIMIT),
        name="experts",
    )(gid, tid, nxt, offs, nact, dst_row, dst_row, src_row, src_row, src_row, bg, bu, bd, h2, wg, wu, wd)
    return y4.reshape(TOP_K, slot_rows, LANES)


def _combine_kernel(gate_ref, y4_ref, x1p_ref, x1s_ref, modp_ref, g2s_ref, fn_ref, outp_ref, outs_ref,
                    *, n_prompt_tiles):
    i = pl.program_id(0)
    tm = x1p_ref.shape[0]
    gates = gate_ref[...].T
    cols = []
    for c in range(SUBLANES):
        acc = gates[:, 0:1] * y4_ref[0, pl.ds(c, tm, stride=SUBLANES), :]
        for k in range(1, TOP_K):
            acc = acc + gates[:, k:k + 1] * y4_ref[k, pl.ds(c, tm, stride=SUBLANES), :]
        cols.append(acc)
    ffn = jnp.concatenate(cols, axis=-1)

    @pl.when(i < n_prompt_tiles)
    def _():
        g2 = modp_ref[...][5:6, :]
        outp_ref[...] = _rmsnorm(x1p_ref[...] + g2 * ffn, fn_ref[...])

    @pl.when(i >= n_prompt_tiles)
    def _():
        g2 = jnp.concatenate([g2s_ref[...]] * (tm // g2s_ref.shape[0]), axis=0)
        outs_ref[...] = _rmsnorm(x1s_ref[...] + g2 * ffn, fn_ref[...])


def _combine(gates, y4, x1_p, x1_s, mod_p, mod, n_batch_s, final_norm, tokens_per_seq):
    n_p, d = x1_p.shape
    n_s = x1_s.shape[0]
    tm = COMBINE_TILE
    assert tm % n_batch_s == 0 and n_batch_s % SUBLANES == 0
    npt, nst = n_p // tm, n_s // tm
    tiles_per_seq = tokens_per_seq // tm
    pmap = lambda i: (jnp.minimum(i, npt - 1), 0)
    smap = lambda i: (jnp.maximum(i - npt, 0), 0)
    return pl.pallas_call(
        functools.partial(_combine_kernel, n_prompt_tiles=npt),
        out_shape=(jax.ShapeDtypeStruct((n_p, d), F32), jax.ShapeDtypeStruct((n_s, d), F32)),
        grid=(npt + nst,),
        in_specs=[
            pl.BlockSpec((SUBLANES, tm), lambda i: (0, i)),
            pl.BlockSpec((TOP_K, tm * SUBLANES, LANES), lambda i: (0, i, 0)),
            pl.BlockSpec((tm, d), pmap),
            pl.BlockSpec((tm, d), smap),
            pl.BlockSpec((None, 6, d), lambda i: (jnp.minimum(i, npt - 1) // tiles_per_seq, 0, 0)),
            pl.BlockSpec((n_batch_s, d), lambda i: (0, 5)),
            pl.BlockSpec((1, d), lambda i: (0, 0)),
        ],
        out_specs=(pl.BlockSpec((tm, d), pmap), pl.BlockSpec((tm, d), smap)),
        compiler_params=pltpu.CompilerParams(
            dimension_semantics=("arbitrary",), vmem_limit_bytes=VMEM_LIMIT),
        name="combine",
    )(gates, y4, x1_p, x1_s, mod_p, mod, final_norm)


def kernel(x_prompt, x_sample, state_pool, state_conv, c_prompt, c_sample, norm1, norm2, w_ada, b_ada,
           w_in, w_pool, pool_scale, w_conv, w_out, w_router, b_router, w_gate, b_gate, w_up, b_up,
           w_down, b_down, final_norm):
    depth = norm1.shape[0]
    assert depth == 1, "single-layer step"
    bp, tp, d = x_prompt.shape
    bs, ts, _ = x_sample.shape
    dp = state_pool.shape[-1]
    n_hist = state_pool.shape[2]
    n_chist = state_conv.shape[2]
    n_p, n_s = bp * tp, bs * ts
    assert d == SUBLANES * LANES, "token-tile layout assumes one vreg tile per token row"
    assert tp % TOKEN_TILE == 0 and n_s == TOKEN_TILE, "the sample group fills exactly one token tile"
    assert tp % COMBINE_TILE == 0 and n_s % COMBINE_TILE == 0
    assert (n_p + n_s) % ROUTE_TILE == 0 and ((n_p + n_s) * TOP_K) % EXPERT_TILE == 0

    l = 0
    n1 = norm1[l].reshape(1, d)
    n2 = norm2[l].reshape(1, d)
    pscale = pool_scale[l].reshape(1, dp)
    w_r = w_router[l].T.astype(BF16)
    b_r = b_router[l].reshape(N_EXPERTS, 1)

    mod = _adaln(jnp.concatenate([c_sample, c_prompt], axis=0), w_ada[l], b_ada[l])
    mod_p = mod[bs:].reshape(bp, 6, d)

    xs_tm = jnp.transpose(x_sample, (1, 0, 2)).reshape(n_s, d)
    ps_tm = jnp.transpose(state_pool[l], (1, 0, 2))
    cs_tm = jnp.transpose(state_conv[l], (1, 0, 2))
    x1_s, h2_s, lg_s, newp_tm, newc_tm, w_in_b, w_pool_b, w_out_b = _mixer_sample(
        xs_tm, mod, ps_tm, cs_tm, n1, n2, w_in[l], w_pool[l], pscale, w_conv[l], w_out[l], w_r, b_r, ts)

    x1_p, h2, lg_p, u_tail, v_tail = _mixer_prompt(
        x_prompt, mod_p, n1, n2, w_in_b, w_pool_b, pscale, w_conv[l], w_out_b, w_r, b_r, h2_s)

    dest, gates, counts_f = _route(lg_p, lg_s)
    counts = counts_f[:, 0].astype(I32)
    n_rows = (n_p + n_s) * TOP_K
    gid, tid, nxt, offs, nact, n_steps = _group_metadata(counts, n_rows, EXPERT_TILE)

    dest = dest.reshape(-1)
    inv = _invert(dest)
    y4 = _experts(gid, tid, nxt, offs, nact, n_steps, inv, h2,
                  w_gate[l], b_gate[l], w_up[l], b_up[l], w_down[l], b_down[l])

    y_p, y_s = _combine(gates, y4, x1_p, x1_s, mod_p, mod, bs, final_norm.reshape(1, d), tp)

    y_prompt = y_p.reshape(bp, tp, d)
    y_sample = jnp.transpose(y_s.reshape(ts, bs, d), (1, 0, 2))
    new_pool_prompt = u_tail[:, POOL_HALO - n_hist:, :][None]
    new_conv_prompt = v_tail[:, CONV_HALO - n_chist:, :][None]
    new_pool_sample = jnp.transpose(newp_tm, (1, 0, 2))[None]
    new_conv_sample = jnp.transpose(newc_tm, (1, 0, 2))[None]
    return (y_prompt, y_sample, new_pool_prompt, new_conv_prompt, new_pool_sample, new_conv_sample)
```

```python
import functools

import jax
import jax.numpy as jnp
from jax import lax
from jax.experimental import pallas as pl
from jax.experimental.pallas import tpu as pltpu
from jax.experimental.pallas import tpu_sc as plsc

F32 = jnp.float32
BF16 = jnp.bfloat16
I32 = jnp.int32

POOL_WINDOWS = (2, 4, 8, 16)
POOL_HALO = 16
CONV_TAPS = 3
CONV_HALO = 8
N_EXPERTS = 32
TOP_K = 4
SWIGLU_LIMIT = 7.0
SWIGLU_ALPHA = 1.702
EPS = 1e-5
PAST_LEN = 16384

LANES = 128
SUBLANES = 8

TOKEN_TILE = 512
ROUTE_TILE = 512
COMBINE_TILE = 512
EXPERT_TILE = 1024
EXPERT_SUBTILE = 128
VMEM_LIMIT = 56 * 1024 * 1024


def _rmsnorm(x, g):
    ms = jnp.mean(x * x, axis=-1, keepdims=True)
    return x * lax.rsqrt(ms + EPS) * g


def _dot(a, b):
    return jnp.dot(a, b, preferred_element_type=F32)


def _store_token_tiles(ref, val):
    rows = val.shape[0]
    for c in range(SUBLANES):
        ref[pl.ds(c, rows, stride=SUBLANES), :] = val[:, c * LANES:(c + 1) * LANES]


def _load_token_tiles(ref):
    rows = ref.shape[0] // SUBLANES
    return jnp.concatenate([ref[pl.ds(c, rows, stride=SUBLANES), :] for c in range(SUBLANES)], axis=-1)


ISSUE_GROUP = 4


def _adaln_kernel(c_ref, w_ref, b_ref, o_ref):
    c = c_ref[...]
    s = c * jax.nn.sigmoid(c)
    o_ref[...] = _dot(s.astype(BF16), w_ref[...].astype(BF16)) + b_ref[...]


def _adaln(c, w_ada, b_ada):
    rows, d = c.shape
    n = w_ada.shape[1]
    tn = 1024
    return pl.pallas_call(
        _adaln_kernel,
        out_shape=jax.ShapeDtypeStruct((rows, n), F32),
        grid=(n // tn,),
        in_specs=[
            pl.BlockSpec((rows, d), lambda j: (0, 0)),
            pl.BlockSpec((d, tn), lambda j: (0, j)),
            pl.BlockSpec((1, tn), lambda j: (0, j)),
        ],
        out_specs=pl.BlockSpec((rows, tn), lambda j: (0, j)),
        compiler_params=pltpu.CompilerParams(
            dimension_semantics=("arbitrary",), vmem_limit_bytes=VMEM_LIMIT),
        name="adaln",
    )(c, w_ada, b_ada.reshape(1, n))


def _mix_tail(x, pool_in, conv_out, g1, sc2, sh2, n2, wpool_ref, pscale, wout, wr, br):
    gw = pool_in.shape[1] // len(POOL_WINDOWS)
    mixed = [_dot(pool_in[:, g * gw:(g + 1) * gw].astype(BF16), wpool_ref[g])
             for g in range(len(POOL_WINDOWS))]
    pool_out = jnp.concatenate(mixed, axis=-1) * pscale
    mix_in = jnp.concatenate([pool_out, conv_out], axis=-1).astype(BF16)
    x1 = x + g1 * _dot(mix_in, wout)
    h2 = _rmsnorm(x1, n2) * (1.0 + sc2) + sh2
    logits_t = lax.dot_general(wr, h2.astype(BF16), (((1,), (1,)), ((), ())),
                               preferred_element_type=F32) + br
    return x1, h2, logits_t


def _mixer_prompt_kernel(x_ref, mod_ref, n1_ref, n2_ref, win_ref, wpool_ref, pscale_ref, wconv_ref,
                         wout_ref, wr_ref, br_ref, h2s_ref,
                         x1_ref, h2_ref, lg_ref, upool_ref, vconv_ref, ubuf, vbuf, *, tiles_per_seq):
    i = pl.program_id(0)

    @pl.when(i < pl.num_programs(0) - 1)
    def _():
        _mixer_prompt_tile(x_ref, mod_ref, n1_ref, n2_ref, win_ref, wpool_ref, pscale_ref, wconv_ref,
                           wout_ref, wr_ref, br_ref, x1_ref, h2_ref, lg_ref, upool_ref, vconv_ref,
                           ubuf, vbuf, lax.rem(i, tiles_per_seq), tiles_per_seq)

    @pl.when(i == pl.num_programs(0) - 1)
    def _():
        h2_ref[...] = h2s_ref[...]


def _mixer_prompt_tile(x_ref, mod_ref, n1_ref, n2_ref, win_ref, wpool_ref, pscale_ref, wconv_ref,
                       wout_ref, wr_ref, br_ref, x1_ref, h2_ref, lg_ref, upool_ref, vconv_ref,
                       ubuf, vbuf, t, tiles_per_seq):
    tt = x_ref.shape[0]
    dp = ubuf.shape[1]
    gw = dp // len(POOL_WINDOWS)

    @pl.when(t == 0)
    def _():
        ubuf[0:POOL_HALO, :] = jnp.zeros((POOL_HALO, dp), F32)
        vbuf[0:CONV_HALO, :] = jnp.zeros((CONV_HALO, dp), F32)

    x = x_ref[...]
    mod = mod_ref[...]
    sh1, sc1, g1, sh2, sc2, _ = [mod[i:i + 1, :] for i in range(6)]
    h = _rmsnorm(x, n1_ref[...]) * (1.0 + sc1) + sh1
    z = _dot(h.astype(BF16), win_ref[...])
    u, gate_b, gate_c, val = [z[:, i * dp:(i + 1) * dp] for i in range(4)]

    ubuf[POOL_HALO:POOL_HALO + tt, :] = u
    pos = lax.broadcasted_iota(I32, (tt, gw), 0) + t * tt
    pooled = []
    for g, w in enumerate(POOL_WINDOWS):
        cols = slice(g * gw, (g + 1) * gw)
        acc = u[:, cols]
        for j in range(1, w):
            acc = acc + ubuf[POOL_HALO - j:POOL_HALO - j + tt, cols]
        cnt = jnp.minimum(pos + 1, w).astype(F32)
        pooled.append(acc / cnt - u[:, cols])
    pool_in = jnp.concatenate(pooled, axis=-1)

    v = gate_c * val
    vbuf[CONV_HALO:CONV_HALO + tt, :] = v
    wc = wconv_ref[...]
    y = (wc[0:1, :] * vbuf[CONV_HALO - 2:CONV_HALO - 2 + tt, :]
         + wc[1:2, :] * vbuf[CONV_HALO - 1:CONV_HALO - 1 + tt, :]
         + wc[2:3, :] * v)
    conv_out = gate_b * y

    x1, h2, logits = _mix_tail(x, pool_in, conv_out, g1, sc2, sh2, n2_ref[...], wpool_ref,
                               pscale_ref[...], wout_ref[...], wr_ref[...], br_ref[...])
    x1_ref[...] = x1
    _store_token_tiles(h2_ref, h2)
    lg_ref[...] = logits

    ubuf[0:POOL_HALO, :] = ubuf[tt:tt + POOL_HALO, :]
    vbuf[0:CONV_HALO, :] = vbuf[tt:tt + CONV_HALO, :]

    @pl.when(t == tiles_per_seq - 1)
    def _():
        upool_ref[...] = ubuf[0:POOL_HALO, :]
        vconv_ref[...] = vbuf[0:CONV_HALO, :]


def _mixer_prompt(x, mod_p, n1, n2, w_in, w_pool, pscale, w_conv, w_out, w_r, b_r, h2_s):
    b, t, d = x.shape
    dp = w_pool.shape[0] * w_pool.shape[1]
    tt = min(TOKEN_TILE, t)
    nt = t // tt
    n_real = b * nt
    assert h2_s.shape[0] == tt * SUBLANES
    seq = lambda i: jnp.minimum(i, n_real - 1) // nt
    tile = lambda i: jnp.minimum(i, n_real - 1)
    const2 = lambda i: (0, 0)
    const3 = lambda i: (0, 0, 0)
    return pl.pallas_call(
        functools.partial(_mixer_prompt_kernel, tiles_per_seq=nt),
        out_shape=(
            jax.ShapeDtypeStruct((b * t, d), F32),
            jax.ShapeDtypeStruct(((n_real + 1) * tt * SUBLANES, LANES), F32),
            jax.ShapeDtypeStruct((N_EXPERTS, b * t), F32),
            jax.ShapeDtypeStruct((b, POOL_HALO, dp), F32),
            jax.ShapeDtypeStruct((b, CONV_HALO, dp), F32),
        ),
        grid=(n_real + 1,),
        in_specs=[
            pl.BlockSpec((None, tt, d), lambda i: (seq(i), tile(i) % nt, 0)),
            pl.BlockSpec((None, 6, d), lambda i: (seq(i), 0, 0)),
            pl.BlockSpec((1, d), const2),
            pl.BlockSpec((1, d), const2),
            pl.BlockSpec(w_in.shape, const2),
            pl.BlockSpec(w_pool.shape, const3),
            pl.BlockSpec((1, dp), const2),
            pl.BlockSpec(w_conv.shape, const2),
            pl.BlockSpec(w_out.shape, const2),
            pl.BlockSpec(w_r.shape, const2),
            pl.BlockSpec((N_EXPERTS, 1), const2),
            pl.BlockSpec(h2_s.shape, const2),
        ],
        out_specs=(
            pl.BlockSpec((tt, d), lambda i: (tile(i), 0)),
            pl.BlockSpec((tt * SUBLANES, LANES), lambda i: (i, 0)),
            pl.BlockSpec((N_EXPERTS, tt), lambda i: (0, tile(i))),
            pl.BlockSpec((None, POOL_HALO, dp), lambda i: (seq(i), 0, 0)),
            pl.BlockSpec((None, CONV_HALO, dp), lambda i: (seq(i), 0, 0)),
        ),
        scratch_shapes=[
            pltpu.VMEM((POOL_HALO + tt, dp), F32),
            pltpu.VMEM((CONV_HALO + tt, dp), F32),
        ],
        compiler_params=pltpu.CompilerParams(
            dimension_semantics=("arbitrary",), vmem_limit_bytes=VMEM_LIMIT),
        name="mixer_prompt",
    )(x, mod_p, n1, n2, w_in, w_pool, pscale, w_conv, w_out, w_r, b_r, h2_s)


def _mixer_sample_kernel(x_ref, mod_ref, pstate_ref, cstate_ref, n1_ref, n2_ref, win_f32, wpool_f32,
                         pscale_ref, wconv_ref, wout_f32, wr_ref, br_ref,
                         x1_ref, h2_ref, lg_ref, newp_ref, newc_ref, win_ref, wpool_ref, wout_ref,
                         *, steps):
    win_ref[...] = win_f32[...].astype(BF16)
    wpool_ref[...] = wpool_f32[...].astype(BF16)
    wout_ref[...] = wout_f32[...].astype(BF16)
    nb = pstate_ref.shape[1]
    d = x_ref.shape[1]
    dp = pstate_ref.shape[2]
    gw = dp // len(POOL_WINDOWS)
    n_hist = pstate_ref.shape[0]
    n_chist = cstate_ref.shape[0]

    x = x_ref[...]
    mod = mod_ref[0:nb, :]
    rep = lambda a: jnp.concatenate([a] * steps, axis=0)
    sh1, sc1, g1, sh2, sc2, _ = [rep(mod[:, i * d:(i + 1) * d]) for i in range(6)]
    h = _rmsnorm(x, n1_ref[...]) * (1.0 + sc1) + sh1
    z = _dot(h.astype(BF16), win_ref[...])
    u, gate_b, gate_c, val = [z[:, i * dp:(i + 1) * dp] for i in range(4)]

    ext = [pstate_ref[i] for i in range(n_hist)] + [u[s * nb:(s + 1) * nb, :] for s in range(steps)]
    pooled_steps = []
    for s in range(steps):
        groups = []
        for g, w in enumerate(POOL_WINDOWS):
            cols = slice(g * gw, (g + 1) * gw)
            acc = ext[n_hist + s][:, cols]
            for j in range(1, w):
                acc = acc + ext[n_hist + s - j][:, cols]
            cnt = float(min(PAST_LEN + s + 1, w))
            groups.append(acc / cnt - ext[n_hist + s][:, cols])
        pooled_steps.append(jnp.concatenate(groups, axis=-1))
    pool_in = jnp.concatenate(pooled_steps, axis=0)

    v = gate_c * val
    vext = [cstate_ref[i] for i in range(n_chist)] + [v[s * nb:(s + 1) * nb, :] for s in range(steps)]
    wc = wconv_ref[...]
    y = jnp.concatenate(
        [wc[0:1, :] * vext[s] + wc[1:2, :] * vext[s + 1] + wc[2:3, :] * vext[s + 2] for s in range(steps)],
        axis=0)
    conv_out = gate_b * y

    x1, h2, logits = _mix_tail(x, pool_in, conv_out, g1, sc2, sh2, n2_ref[...], wpool_ref,
                               pscale_ref[...], wout_ref[...], wr_ref[...], br_ref[...])
    x1_ref[...] = x1
    _store_token_tiles(h2_ref, h2)
    lg_ref[...] = logits
    for i in range(n_hist):
        newp_ref[i] = ext[steps + i]
    for i in range(n_chist):
        newc_ref[i] = vext[steps + i]


def _mixer_sample(x_tm, mod_s, pstate_tm, cstate_tm, n1, n2, w_in, w_pool, pscale, w_conv, w_out, w_r, b_r,
                  steps):
    rows, d = x_tm.shape
    return pl.pallas_call(
        functools.partial(_mixer_sample_kernel, steps=steps),
        out_shape=(
            jax.ShapeDtypeStruct((rows, d), F32),
            jax.ShapeDtypeStruct((rows * SUBLANES, LANES), F32),
            jax.ShapeDtypeStruct((N_EXPERTS, rows), F32),
            jax.ShapeDtypeStruct(pstate_tm.shape, F32),
            jax.ShapeDtypeStruct(cstate_tm.shape, F32),
            jax.ShapeDtypeStruct(w_in.shape, BF16),
            jax.ShapeDtypeStruct(w_pool.shape, BF16),
            jax.ShapeDtypeStruct(w_out.shape, BF16),
        ),
        compiler_params=pltpu.CompilerParams(vmem_limit_bytes=VMEM_LIMIT),
        name="mixer_sample",
    )(x_tm, mod_s, pstate_tm, cstate_tm, n1, n2, w_in, w_pool, pscale, w_conv, w_out, w_r, b_r)


def _route_kernel(lgp_ref, lgs_ref, dest_ref, gate_ref, cnt_ref, counts, start, before, chosen):
    ne = lgp_ref.shape[0]
    tr = before.shape[0]
    reps = tr // LANES
    n_prompt_chunks = lgp_ref.shape[1] // tr
    n_sample_chunks = lgs_ref.shape[1] // tr
    eidx = lax.broadcasted_iota(I32, (ne, tr), 0)

    def selected(onehots):
        return jnp.where(onehots[0] | onehots[1] | onehots[2] | onehots[3], 1.0, 0.0)

    def select_chunk(ref, first_chunk):
        def body(c, carry):
            chunk = first_chunk + c
            work = ref[:, pl.ds(pl.multiple_of(c * tr, tr), tr)]
            top_v, onehots = [], []
            for k in range(TOP_K):
                m = jnp.max(work, axis=0, keepdims=True)
                idx = jnp.min(jnp.where(work == m, eidx, ne), axis=0, keepdims=True)
                chosen[chunk, k:k + 1, :] = idx
                top_v.append(m)
                onehots.append(eidx == idx)
                work = jnp.where(onehots[k], -jnp.inf, work)
            es = [jnp.exp(v - top_v[0]) for v in top_v]
            denom = es[0] + es[1] + es[2] + es[3]
            cols = pl.ds(pl.multiple_of(chunk * tr, tr), tr)
            for k in range(TOP_K):
                gate_ref[k:k + 1, cols] = es[k] / denom
            chunk_counts = jnp.sum(selected(onehots), axis=1, keepdims=True)
            counts[...] = counts[...] + jnp.broadcast_to(chunk_counts, (ne, LANES))
            return carry
        return body

    counts[...] = jnp.zeros_like(counts)
    gate_ref[...] = jnp.zeros_like(gate_ref)
    lax.fori_loop(0, n_prompt_chunks, select_chunk(lgp_ref, 0), 0)
    lax.fori_loop(0, n_sample_chunks, select_chunk(lgs_ref, n_prompt_chunks), 0)

    total = counts[...]
    hi = jnp.floor(total * (1.0 / 256.0))
    lo = total - hi * 256.0
    r = lax.broadcasted_iota(I32, (ne, ne), 0)
    col = lax.broadcasted_iota(I32, (ne, ne), 1)
    lower = jnp.where(col < r, 1.0, 0.0).astype(BF16)
    start[...] = 256.0 * _dot(lower, hi.astype(BF16)) + _dot(lower, lo.astype(BF16))
    cnt_ref[...] = total
    counts[...] = jnp.zeros_like(counts)

    r = lax.broadcasted_iota(I32, (tr, tr), 0)
    col = lax.broadcasted_iota(I32, (tr, tr), 1)
    before[...] = jnp.where(r < col, 1.0, 0.0).astype(BF16)

    def place_chunk(chunk, carry):
        onehots = [eidx == chosen[chunk, k:k + 1, :] for k in range(TOP_K)]
        mask = selected(onehots)
        base = jnp.concatenate([counts[...] + start[...]] * reps, axis=1)
        rank = _dot(mask.astype(BF16), before[...]) + base
        counts[...] = counts[...] + jnp.broadcast_to(jnp.sum(mask, axis=1, keepdims=True), (ne, LANES))
        for k in range(TOP_K):
            d = jnp.sum(jnp.where(onehots[k], rank, 0.0), axis=0, keepdims=True).astype(I32)
            for j in range(reps):
                dest_ref[chunk * reps + j, k:k + 1, :] = d[:, j * LANES:(j + 1) * LANES]
        return carry

    lax.fori_loop(0, n_prompt_chunks + n_sample_chunks, place_chunk, 0)


def _route(lgt_p, lgt_s):
    ne, n_p = lgt_p.shape
    n_s = lgt_s.shape[1]
    tr = ROUTE_TILE
    return pl.pallas_call(
        _route_kernel,
        out_shape=(
            jax.ShapeDtypeStruct(((n_p + n_s) // LANES, TOP_K, LANES), I32),
            jax.ShapeDtypeStruct((SUBLANES, n_p + n_s), F32),
            jax.ShapeDtypeStruct((ne, LANES), F32),
        ),
        scratch_shapes=[pltpu.VMEM((ne, LANES), F32), pltpu.VMEM((ne, LANES), F32),
                        pltpu.VMEM((tr, tr), BF16), pltpu.VMEM(((n_p + n_s) // tr, TOP_K, tr), I32)],
        compiler_params=pltpu.CompilerParams(vmem_limit_bytes=VMEM_LIMIT),
        name="route",
    )(lgt_p, lgt_s)


def _group_metadata(counts, n_rows, tile):
    n_tiles = n_rows // tile
    n_steps = n_tiles + N_EXPERTS - 1
    ends = jnp.cumsum(counts)
    offs = jnp.concatenate([jnp.zeros((1,), I32), ends]).astype(I32)
    first_tile = offs[:-1] // tile
    last_tile = (ends - 1) // tile
    tiles_e = jnp.where(counts > 0, last_tile - first_tile + 1, 0)
    step_end = jnp.cumsum(tiles_e)
    step_start = step_end - tiles_e
    n_active = step_end[-1]
    s = jnp.minimum(jnp.arange(n_steps, dtype=I32), n_active - 1)
    owner = ((s[:, None] >= step_start[None, :]) & (s[:, None] < step_end[None, :])).astype(I32)
    gid = jnp.sum(owner * jnp.arange(N_EXPERTS, dtype=I32)[None, :], axis=1)
    tid = jnp.sum(owner * (first_tile - step_start)[None, :], axis=1) + s
    ids = jnp.arange(N_EXPERTS, dtype=I32)
    later = (ids[None, :] > ids[:, None]) & (counts[None, :] > 0)
    next_e = jnp.min(jnp.where(later, ids[None, :], N_EXPERTS), axis=1)
    next_e = jnp.where(next_e == N_EXPERTS, -1, next_e)
    nxt = jnp.sum(owner * next_e[None, :], axis=1)
    return gid, tid, nxt, offs, n_active.reshape(1).astype(I32), n_steps


SC_LANES = 16
SC_INDEX_BATCH = 128


def _invert(dest):
    r = dest.shape[0]
    mesh = plsc.VectorSubcoreMesh(core_axis_name="core", subcore_axis_name="subcore")
    per = r // mesh.num_subcores
    assert per * mesh.num_subcores == r and per % SC_INDEX_BATCH == 0

    @functools.partial(
        pl.kernel, mesh=mesh, out_type=jax.ShapeDtypeStruct((r,), I32),
        scratch_types=[pltpu.VMEM_SHARED((r,), I32), pltpu.VMEM((per,), I32), pltpu.VMEM((per,), I32)],
        compiler_params=pltpu.CompilerParams(needs_layout_passes=False),
        name="invert",
    )
    def invert(dest_hbm, inv_hbm, table, idx, ids):
        @pl.when(lax.axis_index("core") == 0)
        def _():
            base = lax.axis_index("subcore") * per
            pltpu.sync_copy(dest_hbm.at[pl.ds(base, per)], idx)
            lane = lax.iota(I32, SC_LANES)

            @pl.loop(0, per // SC_LANES)
            def _(i):
                p = base + i * SC_LANES + lane
                tok = lax.shift_right_logical(p, 9) * LANES + (p & (LANES - 1))
                slot = lax.shift_right_logical(p, 7) & (TOP_K - 1)
                ids[pl.ds(i * SC_LANES, SC_LANES)] = tok * TOP_K + slot

            @pl.loop(0, per // SC_INDEX_BATCH)
            def _(j):
                span = pl.ds(j * SC_INDEX_BATCH, SC_INDEX_BATCH)
                pltpu.sync_copy(ids.at[span], table.at[idx.at[span]])

            plsc.subcore_barrier()
            pltpu.sync_copy(table.at[pl.ds(base, per)], inv_hbm.at[pl.ds(base, per)])

    return invert(dest)


def _experts_kernel(gid_ref, tid_ref, nxt_ref, offs_ref, nact_ref,
                    dstp_ref, dst0_ref, src0_ref, src1_ref, src2_ref, bg_ref, bu_ref, bd_ref,
                    h2_hbm, wg_hbm, wu_hbm, wd_hbm, y4_hbm,
                    wg_b, wu_b, wd_b, wg_f, wu_f, wd_f, relay, rows, ybuf, sems, row_sems, y_sems):
    s = pl.program_id(0)
    tile_rows = rows.shape[1]
    tm = tile_rows // SUBLANES
    landing = ((wg_hbm, wg_f, wg_b), (wu_hbm, wu_f, wu_b), (wd_hbm, wd_f, wd_b))

    def fetch(e):
        for j, (hbm, land, _) in enumerate(landing):
            pltpu.make_async_copy(hbm.at[e], land, sems.at[j]).start()

    def row_in(idx_ref, into, r):
        src = h2_hbm.at[pl.ds(pl.multiple_of(idx_ref[r], SUBLANES), SUBLANES)]
        dst = rows.at[into, pl.ds(pl.multiple_of(r * SUBLANES, SUBLANES), SUBLANES)]
        return pltpu.make_async_copy(src, dst, row_sems.at[into])

    def row_out(idx_ref, frm, r):
        src = ybuf.at[frm, pl.ds(pl.multiple_of(r * SUBLANES, SUBLANES), SUBLANES)]
        dst = y4_hbm.at[pl.ds(pl.multiple_of(idx_ref[r], SUBLANES), SUBLANES)]
        return pltpu.make_async_copy(src, dst, y_sems.at[frm])

    def in_line(make, idx_ref, buf, first=0, count=None):
        for r in range(first, tm if count is None else first + count):
            make(idx_ref, buf, r).start(priority=r % 2)

    def in_loop(make, idx_ref, buf):
        def group(g, carry):
            for j in range(ISSUE_GROUP):
                make(idx_ref, buf, g * ISSUE_GROUP + j).start(priority=j % 2)
            return carry
        lax.fori_loop(0, tm // ISSUE_GROUP, group, 0)

    def wait_rows(buf):
        pltpu.make_async_copy(h2_hbm.at[pl.ds(0, tile_rows)], rows.at[buf], row_sems.at[buf]).wait()

    def wait_y(buf):
        pltpu.make_async_copy(ybuf.at[buf], y4_hbm.at[pl.ds(0, tile_rows)], y_sems.at[buf]).wait()

    @pl.when(s < nact_ref[0])
    def _():
        e = gid_ref[s]
        m = tid_ref[s]
        cur = lax.rem(m, 3)
        before = lax.rem(m + 2, 3)
        last = nact_ref[0] - 1
        new_tile = (s == 0) | (tid_ref[jnp.maximum(s - 1, 0)] != m)

        @pl.when(s == 0)
        def _():
            fetch(e)
            in_loop(row_in, src0_ref, 0)
            in_loop(row_in, src1_ref, 1)
            ybuf[2] = jnp.zeros(ybuf.shape[1:], F32)

        @pl.when((s == 0) | (gid_ref[jnp.maximum(s - 1, 0)] != e))
        def _():
            for j, (hbm, land, half) in enumerate(landing):
                pltpu.make_async_copy(hbm.at[e], land, sems.at[j]).wait()
                half[...] = land[...].astype(BF16)

            @pl.when(nxt_ref[s] >= 0)
            def _():
                fetch(nxt_ref[s])

        @pl.when(new_tile)
        def _():
            wait_rows(cur)

            @pl.when(m >= 2)
            def _():
                wait_y(cur)

        def ffn(load_x, after_gate=lambda: None):
            g = _dot(load_x(), wg_b[...]) + bg_ref[pl.ds(e, 1), :]
            after_gate()
            u = _dot(load_x(), wu_b[...]) + bu_ref[pl.ds(e, 1), :]
            g = jnp.minimum(g, SWIGLU_LIMIT)
            u = jnp.clip(u, -SWIGLU_LIMIT, SWIGLU_LIMIT)
            glu = g * jax.nn.sigmoid(SWIGLU_ALPHA * g)
            return _dot(((u + 1.0) * glu).astype(BF16), wd_b[...]) + bd_ref[pl.ds(e, 1), :]

        lo = offs_ref[e]
        hi = offs_ref[e + 1]
        whole_tile = (lo <= m * tm) & (hi >= (m + 1) * tm)

        def ffn_and_copies(load_x, first, count):
            pinned = count // 2
            y = ffn(load_x, lambda: in_line(row_in, src2_ref, before, first, pinned))
            in_line(row_in, src2_ref, before, first + pinned, count - pinned)
            in_line(row_out, dstp_ref, before, first, count)
            return y

        @pl.when(whole_tile)
        def _():
            y = ffn_and_copies(lambda: _load_token_tiles(rows.at[cur]).astype(BF16), 0, tm)
            _store_token_tiles(ybuf.at[cur], y)

        @pl.when(jnp.logical_not(whole_tile))
        def _():
            sub = relay.shape[0] // SUBLANES
            for j in range(tm // sub):
                first = m * tm + j * sub
                span = pl.ds(j * sub * SUBLANES, sub * SUBLANES)
                load_x = lambda span=span: _load_token_tiles(rows.at[cur, span]).astype(BF16)

                def put(y, first=first, span=span):
                    _store_token_tiles(relay, y)
                    row = first + lax.shift_right_logical(lax.broadcasted_iota(I32, relay.shape, 0), 3)
                    pltpu.store(ybuf.at[cur, span], relay[...], mask=(row >= lo) & (row < hi))

                evaluate = (lo < first + sub) & (hi > first)
                owns_end = hi >= first + sub

                @pl.when(evaluate & owns_end)
                def _():
                    put(ffn_and_copies(load_x, j * sub, sub))

                @pl.when(evaluate & jnp.logical_not(owns_end))
                def _():
                    put(ffn(load_x))

        @pl.when(s == last)
        def _():
            in_loop(row_out, dst0_ref, cur)
            wait_rows(lax.rem(m + 1, 3))
            wait_rows(before)
            for buf in range(3):
                wait_y(buf)


def _experts(gid, tid, nxt, offs, nact, n_steps, inv, h2, wg, bg, wu, bu, wd, bd):
    ne, d, f = wg.shape
    tm = EXPERT_TILE
    n_tokens = h2.shape[0] // SUBLANES
    n_tiles = inv.shape[0] // tm
    assert n_tiles >= 2
    slot_rows = (n_tokens + tm // TOP_K) * SUBLANES
    inv_ext = jnp.concatenate([n_tokens * TOP_K + jnp.arange(tm, dtype=I32), inv])
    tok = lax.shift_right_logical(inv_ext, TOP_K.bit_length() - 1)
    src_row = tok * SUBLANES
    dst_row = (inv_ext & (TOP_K - 1)) * slot_rows + src_row
    whole = lambda s, gid, tid, nxt, offs, nact: (0, 0)

    def order_of(k):
        return pl.BlockSpec(
            (tm,), lambda s, gid, tid, nxt, offs, nact: (jnp.minimum(tid[s] + k, n_tiles - 1) + 1,),
            memory_space=pltpu.SMEM)

    grid_spec = pltpu.PrefetchScalarGridSpec(
        num_scalar_prefetch=5,
        grid=(n_steps,),
        in_specs=[
            order_of(-1), order_of(0), order_of(0), order_of(1), order_of(2),
            pl.BlockSpec((ne, f), whole),
            pl.BlockSpec((ne, f), whole),
            pl.BlockSpec((ne, d), whole),
            pl.BlockSpec(memory_space=pl.ANY),
            pl.BlockSpec(memory_space=pl.ANY),
            pl.BlockSpec(memory_space=pl.ANY),
            pl.BlockSpec(memory_space=pl.ANY),
        ],
        out_specs=pl.BlockSpec(memory_space=pl.ANY),
        scratch_shapes=[
            pltpu.VMEM((d, f), BF16), pltpu.VMEM((d, f), BF16), pltpu.VMEM((f, d), BF16),
            pltpu.VMEM((d, f), F32), pltpu.VMEM((d, f), F32), pltpu.VMEM((f, d), F32),
            pltpu.VMEM((EXPERT_SUBTILE * SUBLANES, LANES), F32),
            pltpu.VMEM((3, tm * SUBLANES, LANES), F32),
            pltpu.VMEM((3, tm * SUBLANES, LANES), F32),
            pltpu.SemaphoreType.DMA((3,)),
            pltpu.SemaphoreType.DMA((3,)),
            pltpu.SemaphoreType.DMA((3,)),
        ],
    )
    y4 = pl.pallas_call(
        _experts_kernel,
        out_shape=jax.ShapeDtypeStruct((TOP_K * slot_rows, LANES), F32),
        grid_spec=grid_spec,
        compiler_params=pltpu.CompilerParams(
            dimension_semantics=("arbitrary",), vmem_limit_bytes=VMEM_LIMIT),
        name="experts",
    )(gid, tid, nxt, offs, nact, dst_row, dst_row, src_row, src_row, src_row, bg, bu, bd, h2, wg, wu, wd)
    return y4.reshape(TOP_K, slot_rows, LANES)


def _combine_kernel(gate_ref, y4_ref, x1p_ref, x1s_ref, modp_ref, g2s_ref, fn_ref, outp_ref, outs_ref,
                    *, n_prompt_tiles):
    i = pl.program_id(0)
    tm = x1p_ref.shape[0]
    gates = gate_ref[...].T
    cols = []
    for c in range(SUBLANES):
        acc = gates[:, 0:1] * y4_ref[0, pl.ds(c, tm, stride=SUBLANES), :]
        for k in range(1, TOP_K):
            acc = acc + gates[:, k:k + 1] * y4_ref[k, pl.ds(c, tm, stride=SUBLANES), :]
        cols.append(acc)
    ffn = jnp.concatenate(cols, axis=-1)

    @pl.when(i < n_prompt_tiles)
    def _():
        g2 = modp_ref[...][5:6, :]
        outp_ref[...] = _rmsnorm(x1p_ref[...] + g2 * ffn, fn_ref[...])

    @pl.when(i >= n_prompt_tiles)
    def _():
        g2 = jnp.concatenate([g2s_ref[...]] * (tm // g2s_ref.shape[0]), axis=0)
        outs_ref[...] = _rmsnorm(x1s_ref[...] + g2 * ffn, fn_ref[...])


def _combine(gates, y4, x1_p, x1_s, mod_p, mod, n_batch_s, final_norm, tokens_per_seq):
    n_p, d = x1_p.shape
    n_s = x1_s.shape[0]
    tm = COMBINE_TILE
    assert tm % n_batch_s == 0 and n_batch_s % SUBLANES == 0
    npt, nst = n_p // tm, n_s // tm
    tiles_per_seq = tokens_per_seq // tm
    pmap = lambda i: (jnp.minimum(i, npt - 1), 0)
    smap = lambda i: (jnp.maximum(i - npt, 0), 0)
    return pl.pallas_call(
        functools.partial(_combine_kernel, n_prompt_tiles=npt),
        out_shape=(jax.ShapeDtypeStruct((n_p, d), F32), jax.ShapeDtypeStruct((n_s, d), F32)),
        grid=(npt + nst,),
        in_specs=[
            pl.BlockSpec((SUBLANES, tm), lambda i: (0, i)),
            pl.BlockSpec((TOP_K, tm * SUBLANES, LANES), lambda i: (0, i, 0)),
            pl.BlockSpec((tm, d), pmap),
            pl.BlockSpec((tm, d), smap),
            pl.BlockSpec((None, 6, d), lambda i: (jnp.minimum(i, npt - 1) // tiles_per_seq, 0, 0)),
            pl.BlockSpec((n_batch_s, d), lambda i: (0, 5)),
            pl.BlockSpec((1, d), lambda i: (0, 0)),
        ],
        out_specs=(pl.BlockSpec((tm, d), pmap), pl.BlockSpec((tm, d), smap)),
        compiler_params=pltpu.CompilerParams(
            dimension_semantics=("arbitrary",), vmem_limit_bytes=VMEM_LIMIT),
        name="combine",
    )(gates, y4, x1_p, x1_s, mod_p, mod, final_norm)


def kernel(x_prompt, x_sample, state_pool, state_conv, c_prompt, c_sample, norm1, norm2, w_ada, b_ada,
           w_in, w_pool, pool_scale, w_conv, w_out, w_router, b_router, w_gate, b_gate, w_up, b_up,
           w_down, b_down, final_norm):
    depth = norm1.shape[0]
    assert depth == 1, "single-layer step"
    bp, tp, d = x_prompt.shape
    bs, ts, _ = x_sample.shape
    dp = state_pool.shape[-1]
    n_hist = state_pool.shape[2]
    n_chist = state_conv.shape[2]
    n_p, n_s = bp * tp, bs * ts
    assert d == SUBLANES * LANES, "token-tile layout assumes one vreg tile per token row"
    assert tp % TOKEN_TILE == 0 and n_s == TOKEN_TILE, "the sample group fills exactly one token tile"
    assert tp % COMBINE_TILE == 0 and n_s % COMBINE_TILE == 0
    assert (n_p + n_s) % ROUTE_TILE == 0 and ((n_p + n_s) * TOP_K) % EXPERT_TILE == 0

    l = 0
    n1 = norm1[l].reshape(1, d)
    n2 = norm2[l].reshape(1, d)
    pscale = pool_scale[l].reshape(1, dp)
    w_r = w_router[l].T.astype(BF16)
    b_r = b_router[l].reshape(N_EXPERTS, 1)

    mod = _adaln(jnp.concatenate([c_sample, c_prompt], axis=0), w_ada[l], b_ada[l])
    mod_p = mod[bs:].reshape(bp, 6, d)

    xs_tm = jnp.transpose(x_sample, (1, 0, 2)).reshape(n_s, d)
    ps_tm = jnp.transpose(state_pool[l], (1, 0, 2))
    cs_tm = jnp.transpose(state_conv[l], (1, 0, 2))
    x1_s, h2_s, lg_s, newp_tm, newc_tm, w_in_b, w_pool_b, w_out_b = _mixer_sample(
        xs_tm, mod, ps_tm, cs_tm, n1, n2, w_in[l], w_pool[l], pscale, w_conv[l], w_out[l], w_r, b_r, ts)

    x1_p, h2, lg_p, u_tail, v_tail = _mixer_prompt(
        x_prompt, mod_p, n1, n2, w_in_b, w_pool_b, pscale, w_conv[l], w_out_b, w_r, b_r, h2_s)

    dest, gates, counts_f = _route(lg_p, lg_s)
    counts = counts_f[:, 0].astype(I32)
    n_rows = (n_p + n_s) * TOP_K
    gid, tid, nxt, offs, nact, n_steps = _group_metadata(counts, n_rows, EXPERT_TILE)

    dest = dest.reshape(-1)
    inv = _invert(dest)
    y4 = _experts(gid, tid, nxt, offs, nact, n_steps, inv, h2,
                  w_gate[l], b_gate[l], w_up[l], b_up[l], w_down[l], b_down[l])

    y_p, y_s = _combine(gates, y4, x1_p, x1_s, mod_p, mod, bs, final_norm.reshape(1, d), tp)

    y_prompt = y_p.reshape(bp, tp, d)
    y_sample = jnp.transpose(y_s.reshape(ts, bs, d), (1, 0, 2))
    new_pool_prompt = u_tail[:, POOL_HALO - n_hist:, :][None]
    new_conv_prompt = v_tail[:, CONV_HALO - n_chist:, :][None]
    new_pool_sample = jnp.transpose(newp_tm, (1, 0, 2))[None]
    new_conv_sample = jnp.transpose(newc_tm, (1, 0, 2))[None]
    return (y_prompt, y_sample, new_pool_prompt, new_conv_prompt, new_pool_sample, new_conv_sample)
```

```python
import functools

import jax
import jax.numpy as jnp
from jax import lax
from jax.experimental import pallas as pl
from jax.experimental.pallas import tpu as pltpu
from jax.experimental.pallas import tpu_sc as plsc

F32 = jnp.float32
BF16 = jnp.bfloat16
I32 = jnp.int32

POOL_WINDOWS = (2, 4, 8, 16)
POOL_HALO = 16
CONV_TAPS = 3
CONV_HALO = 8
N_EXPERTS = 32
TOP_K = 4
SWIGLU_LIMIT = 7.0
SWIGLU_ALPHA = 1.702
EPS = 1e-5
PAST_LEN = 16384

LANES = 128
SUBLANES = 8

TOKEN_TILE = 512
ROUTE_TILE = 512
COMBINE_TILE = 512
EXPERT_TILE = 256
EXPERT_SUBTILE = 128
VMEM_LIMIT = 56 * 1024 * 1024


def _rmsnorm(x, g):
    ms = jnp.mean(x * x, axis=-1, keepdims=True)
    return x * lax.rsqrt(ms + EPS) * g


def _dot(a, b):
    return jnp.dot(a, b, preferred_element_type=F32)


def _store_token_tiles(ref, val):
    rows = val.shape[0]
    for c in range(SUBLANES):
        ref[pl.ds(c, rows, stride=SUBLANES), :] = val[:, c * LANES:(c + 1) * LANES]


def _load_token_tiles(ref):
    rows = ref.shape[0] // SUBLANES
    return jnp.concatenate([ref[pl.ds(c, rows, stride=SUBLANES), :] for c in range(SUBLANES)], axis=-1)


ISSUE_GROUP = 4


def _adaln_kernel(c_ref, w_ref, b_ref, o_ref):
    c = c_ref[...]
    s = c * jax.nn.sigmoid(c)
    o_ref[...] = _dot(s.astype(BF16), w_ref[...].astype(BF16)) + b_ref[...]


def _adaln(c, w_ada, b_ada):
    rows, d = c.shape
    n = w_ada.shape[1]
    tn = 1024
    return pl.pallas_call(
        _adaln_kernel,
        out_shape=jax.ShapeDtypeStruct((rows, n), F32),
        grid=(n // tn,),
        in_specs=[
            pl.BlockSpec((rows, d), lambda j: (0, 0)),
            pl.BlockSpec((d, tn), lambda j: (0, j)),
            pl.BlockSpec((1, tn), lambda j: (0, j)),
        ],
        out_specs=pl.BlockSpec((rows, tn), lambda j: (0, j)),
        compiler_params=pltpu.CompilerParams(
            dimension_semantics=("arbitrary",), vmem_limit_bytes=VMEM_LIMIT),
        name="adaln",
    )(c, w_ada, b_ada.reshape(1, n))


def _mix_tail(x, pool_in, conv_out, g1, sc2, sh2, n2, wpool_ref, pscale, wout, wr, br):
    gw = pool_in.shape[1] // len(POOL_WINDOWS)
    mixed = [_dot(pool_in[:, g * gw:(g + 1) * gw].astype(BF16), wpool_ref[g])
             for g in range(len(POOL_WINDOWS))]
    pool_out = jnp.concatenate(mixed, axis=-1) * pscale
    mix_in = jnp.concatenate([pool_out, conv_out], axis=-1).astype(BF16)
    x1 = x + g1 * _dot(mix_in, wout)
    h2 = _rmsnorm(x1, n2) * (1.0 + sc2) + sh2
    logits_t = lax.dot_general(wr, h2.astype(BF16), (((1,), (1,)), ((), ())),
                               preferred_element_type=F32) + br
    return x1, h2, logits_t


def _mixer_prompt_kernel(x_ref, mod_ref, n1_ref, n2_ref, win_ref, wpool_ref, pscale_ref, wconv_ref,
                         wout_ref, wr_ref, br_ref, h2s_ref,
                         x1_ref, h2_ref, lg_ref, upool_ref, vconv_ref, ubuf, vbuf, *, tiles_per_seq):
    i = pl.program_id(0)

    @pl.when(i < pl.num_programs(0) - 1)
    def _():
        _mixer_prompt_tile(x_ref, mod_ref, n1_ref, n2_ref, win_ref, wpool_ref, pscale_ref, wconv_ref,
                           wout_ref, wr_ref, br_ref, x1_ref, h2_ref, lg_ref, upool_ref, vconv_ref,
                           ubuf, vbuf, lax.rem(i, tiles_per_seq), tiles_per_seq)

    @pl.when(i == pl.num_programs(0) - 1)
    def _():
        h2_ref[...] = h2s_ref[...]


def _mixer_prompt_tile(x_ref, mod_ref, n1_ref, n2_ref, win_ref, wpool_ref, pscale_ref, wconv_ref,
                       wout_ref, wr_ref, br_ref, x1_ref, h2_ref, lg_ref, upool_ref, vconv_ref,
                       ubuf, vbuf, t, tiles_per_seq):
    tt = x_ref.shape[0]
    dp = ubuf.shape[1]
    gw = dp // len(POOL_WINDOWS)

    @pl.when(t == 0)
    def _():
        ubuf[0:POOL_HALO, :] = jnp.zeros((POOL_HALO, dp), F32)
        vbuf[0:CONV_HALO, :] = jnp.zeros((CONV_HALO, dp), F32)

    x = x_ref[...]
    mod = mod_ref[...]
    sh1, sc1, g1, sh2, sc2, _ = [mod[i:i + 1, :] for i in range(6)]
    h = _rmsnorm(x, n1_ref[...]) * (1.0 + sc1) + sh1
    z = _dot(h.astype(BF16), win_ref[...])
    u, gate_b, gate_c, val = [z[:, i * dp:(i + 1) * dp] for i in range(4)]

    ubuf[POOL_HALO:POOL_HALO + tt, :] = u
    pos = lax.broadcasted_iota(I32, (tt, gw), 0) + t * tt
    pooled = []
    for g, w in enumerate(POOL_WINDOWS):
        cols = slice(g * gw, (g + 1) * gw)
        acc = u[:, cols]
        for j in range(1, w):
            acc = acc + ubuf[POOL_HALO - j:POOL_HALO - j + tt, cols]
        cnt = jnp.minimum(pos + 1, w).astype(F32)
        pooled.append(acc / cnt - u[:, cols])
    pool_in = jnp.concatenate(pooled, axis=-1)

    v = gate_c * val
    vbuf[CONV_HALO:CONV_HALO + tt, :] = v
    wc = wconv_ref[...]
    y = (wc[0:1, :] * vbuf[CONV_HALO - 2:CONV_HALO - 2 + tt, :]
         + wc[1:2, :] * vbuf[CONV_HALO - 1:CONV_HALO - 1 + tt, :]
         + wc[2:3, :] * v)
    conv_out = gate_b * y

    x1, h2, logits = _mix_tail(x, pool_in, conv_out, g1, sc2, sh2, n2_ref[...], wpool_ref,
                               pscale_ref[...], wout_ref[...], wr_ref[...], br_ref[...])
    x1_ref[...] = x1
    _store_token_tiles(h2_ref, h2)
    lg_ref[...] = logits

    ubuf[0:POOL_HALO, :] = ubuf[tt:tt + POOL_HALO, :]
    vbuf[0:CONV_HALO, :] = vbuf[tt:tt + CONV_HALO, :]

    @pl.when(t == tiles_per_seq - 1)
    def _():
        upool_ref[...] = ubuf[0:POOL_HALO, :]
        vconv_ref[...] = vbuf[0:CONV_HALO, :]


def _mixer_prompt(x, mod_p, n1, n2, w_in, w_pool, pscale, w_conv, w_out, w_r, b_r, h2_s):
    b, t, d = x.shape
    dp = w_pool.shape[0] * w_pool.shape[1]
    tt = min(TOKEN_TILE, t)
    nt = t // tt
    n_real = b * nt
    assert h2_s.shape[0] == tt * SUBLANES
    seq = lambda i: jnp.minimum(i, n_real - 1) // nt
    tile = lambda i: jnp.minimum(i, n_real - 1)
    const2 = lambda i: (0, 0)
    const3 = lambda i: (0, 0, 0)
    return pl.pallas_call(
        functools.partial(_mixer_prompt_kernel, tiles_per_seq=nt),
        out_shape=(
            jax.ShapeDtypeStruct((b * t, d), F32),
            jax.ShapeDtypeStruct(((n_real + 1) * tt * SUBLANES, LANES), F32),
            jax.ShapeDtypeStruct((N_EXPERTS, b * t), F32),
            jax.ShapeDtypeStruct((b, POOL_HALO, dp), F32),
            jax.ShapeDtypeStruct((b, CONV_HALO, dp), F32),
        ),
        grid=(n_real + 1,),
        in_specs=[
            pl.BlockSpec((None, tt, d), lambda i: (seq(i), tile(i) % nt, 0)),
            pl.BlockSpec((None, 6, d), lambda i: (seq(i), 0, 0)),
            pl.BlockSpec((1, d), const2),
            pl.BlockSpec((1, d), const2),
            pl.BlockSpec(w_in.shape, const2),
            pl.BlockSpec(w_pool.shape, const3),
            pl.BlockSpec((1, dp), const2),
            pl.BlockSpec(w_conv.shape, const2),
            pl.BlockSpec(w_out.shape, const2),
            pl.BlockSpec(w_r.shape, const2),
            pl.BlockSpec((N_EXPERTS, 1), const2),
            pl.BlockSpec(h2_s.shape, const2),
        ],
        out_specs=(
            pl.BlockSpec((tt, d), lambda i: (tile(i), 0)),
            pl.BlockSpec((tt * SUBLANES, LANES), lambda i: (i, 0)),
            pl.BlockSpec((N_EXPERTS, tt), lambda i: (0, tile(i))),
            pl.BlockSpec((None, POOL_HALO, dp), lambda i: (seq(i), 0, 0)),
            pl.BlockSpec((None, CONV_HALO, dp), lambda i: (seq(i), 0, 0)),
        ),
        scratch_shapes=[
            pltpu.VMEM((POOL_HALO + tt, dp), F32),
            pltpu.VMEM((CONV_HALO + tt, dp), F32),
        ],
        compiler_params=pltpu.CompilerParams(
            dimension_semantics=("arbitrary",), vmem_limit_bytes=VMEM_LIMIT),
        name="mixer_prompt",
    )(x, mod_p, n1, n2, w_in, w_pool, pscale, w_conv, w_out, w_r, b_r, h2_s)


def _mixer_sample_kernel(x_ref, mod_ref, pstate_ref, cstate_ref, n1_ref, n2_ref, win_f32, wpool_f32,
                         pscale_ref, wconv_ref, wout_f32, wr_ref, br_ref,
                         x1_ref, h2_ref, lg_ref, newp_ref, newc_ref, win_ref, wpool_ref, wout_ref,
                         *, steps):
    win_ref[...] = win_f32[...].astype(BF16)
    wpool_ref[...] = wpool_f32[...].astype(BF16)
    wout_ref[...] = wout_f32[...].astype(BF16)
    nb = pstate_ref.shape[1]
    d = x_ref.shape[1]
    dp = pstate_ref.shape[2]
    gw = dp // len(POOL_WINDOWS)
    n_hist = pstate_ref.shape[0]
    n_chist = cstate_ref.shape[0]

    x = x_ref[...]
    mod = mod_ref[0:nb, :]
    rep = lambda a: jnp.concatenate([a] * steps, axis=0)
    sh1, sc1, g1, sh2, sc2, _ = [rep(mod[:, i * d:(i + 1) * d]) for i in range(6)]
    h = _rmsnorm(x, n1_ref[...]) * (1.0 + sc1) + sh1
    z = _dot(h.astype(BF16), win_ref[...])
    u, gate_b, gate_c, val = [z[:, i * dp:(i + 1) * dp] for i in range(4)]

    ext = [pstate_ref[i] for i in range(n_hist)] + [u[s * nb:(s + 1) * nb, :] for s in range(steps)]
    pooled_steps = []
    for s in range(steps):
        groups = []
        for g, w in enumerate(POOL_WINDOWS):
            cols = slice(g * gw, (g + 1) * gw)
            acc = ext[n_hist + s][:, cols]
            for j in range(1, w):
                acc = acc + ext[n_hist + s - j][:, cols]
            cnt = float(min(PAST_LEN + s + 1, w))
            groups.append(acc / cnt - ext[n_hist + s][:, cols])
        pooled_steps.append(jnp.concatenate(groups, axis=-1))
    pool_in = jnp.concatenate(pooled_steps, axis=0)

    v = gate_c * val
    vext = [cstate_ref[i] for i in range(n_chist)] + [v[s * nb:(s + 1) * nb, :] for s in range(steps)]
    wc = wconv_ref[...]
    y = jnp.concatenate(
        [wc[0:1, :] * vext[s] + wc[1:2, :] * vext[s + 1] + wc[2:3, :] * vext[s + 2] for s in range(steps)],
        axis=0)
    conv_out = gate_b * y

    x1, h2, logits = _mix_tail(x, pool_in, conv_out, g1, sc2, sh2, n2_ref[...], wpool_ref,
                               pscale_ref[...], wout_ref[...], wr_ref[...], br_ref[...])
    x1_ref[...] = x1
    _store_token_tiles(h2_ref, h2)
    lg_ref[...] = logits
    for i in range(n_hist):
        newp_ref[i] = ext[steps + i]
    for i in range(n_chist):
        newc_ref[i] = vext[steps + i]


def _mixer_sample(x_tm, mod_s, pstate_tm, cstate_tm, n1, n2, w_in, w_pool, pscale, w_conv, w_out, w_r, b_r,
                  steps):
    rows, d = x_tm.shape
    return pl.pallas_call(
        functools.partial(_mixer_sample_kernel, steps=steps),
        out_shape=(
            jax.ShapeDtypeStruct((rows, d), F32),
            jax.ShapeDtypeStruct((rows * SUBLANES, LANES), F32),
            jax.ShapeDtypeStruct((N_EXPERTS, rows), F32),
            jax.ShapeDtypeStruct(pstate_tm.shape, F32),
            jax.ShapeDtypeStruct(cstate_tm.shape, F32),
            jax.ShapeDtypeStruct(w_in.shape, BF16),
            jax.ShapeDtypeStruct(w_pool.shape, BF16),
            jax.ShapeDtypeStruct(w_out.shape, BF16),
        ),
        compiler_params=pltpu.CompilerParams(vmem_limit_bytes=VMEM_LIMIT),
        name="mixer_sample",
    )(x_tm, mod_s, pstate_tm, cstate_tm, n1, n2, w_in, w_pool, pscale, w_conv, w_out, w_r, b_r)


def _route_kernel(lgp_ref, lgs_ref, dest_ref, gate_ref, cnt_ref, counts, start, before, chosen):
    ne = lgp_ref.shape[0]
    tr = before.shape[0]
    reps = tr // LANES
    n_prompt_chunks = lgp_ref.shape[1] // tr
    n_sample_chunks = lgs_ref.shape[1] // tr
    eidx = lax.broadcasted_iota(I32, (ne, tr), 0)

    def selected(onehots):
        return jnp.where(onehots[0] | onehots[1] | onehots[2] | onehots[3], 1.0, 0.0)

    def select_chunk(ref, first_chunk):
        def body(c, carry):
            chunk = first_chunk + c
            work = ref[:, pl.ds(pl.multiple_of(c * tr, tr), tr)]
            top_v, onehots = [], []
            for k in range(TOP_K):
                m = jnp.max(work, axis=0, keepdims=True)
                idx = jnp.min(jnp.where(work == m, eidx, ne), axis=0, keepdims=True)
                chosen[chunk, k:k + 1, :] = idx
                top_v.append(m)
                onehots.append(eidx == idx)
                work = jnp.where(onehots[k], -jnp.inf, work)
            es = [jnp.exp(v - top_v[0]) for v in top_v]
            denom = es[0] + es[1] + es[2] + es[3]
            cols = pl.ds(pl.multiple_of(chunk * tr, tr), tr)
            for k in range(TOP_K):
                gate_ref[k:k + 1, cols] = es[k] / denom
            chunk_counts = jnp.sum(selected(onehots), axis=1, keepdims=True)
            counts[...] = counts[...] + jnp.broadcast_to(chunk_counts, (ne, LANES))
            return carry
        return body

    counts[...] = jnp.zeros_like(counts)
    gate_ref[...] = jnp.zeros_like(gate_ref)
    lax.fori_loop(0, n_prompt_chunks, select_chunk(lgp_ref, 0), 0)
    lax.fori_loop(0, n_sample_chunks, select_chunk(lgs_ref, n_prompt_chunks), 0)

    total = counts[...]
    hi = jnp.floor(total * (1.0 / 256.0))
    lo = total - hi * 256.0
    r = lax.broadcasted_iota(I32, (ne, ne), 0)
    col = lax.broadcasted_iota(I32, (ne, ne), 1)
    lower = jnp.where(col < r, 1.0, 0.0).astype(BF16)
    start[...] = 256.0 * _dot(lower, hi.astype(BF16)) + _dot(lower, lo.astype(BF16))
    cnt_ref[...] = total
    counts[...] = jnp.zeros_like(counts)

    r = lax.broadcasted_iota(I32, (tr, tr), 0)
    col = lax.broadcasted_iota(I32, (tr, tr), 1)
    before[...] = jnp.where(r < col, 1.0, 0.0).astype(BF16)

    def place_chunk(chunk, carry):
        onehots = [eidx == chosen[chunk, k:k + 1, :] for k in range(TOP_K)]
        mask = selected(onehots)
        base = jnp.concatenate([counts[...] + start[...]] * reps, axis=1)
        rank = _dot(mask.astype(BF16), before[...]) + base
        counts[...] = counts[...] + jnp.broadcast_to(jnp.sum(mask, axis=1, keepdims=True), (ne, LANES))
        for k in range(TOP_K):
            d = jnp.sum(jnp.where(onehots[k], rank, 0.0), axis=0, keepdims=True).astype(I32)
            for j in range(reps):
                dest_ref[chunk * reps + j, k:k + 1, :] = d[:, j * LANES:(j + 1) * LANES]
        return carry

    lax.fori_loop(0, n_prompt_chunks + n_sample_chunks, place_chunk, 0)


def _route(lgt_p, lgt_s):
    ne, n_p = lgt_p.shape
    n_s = lgt_s.shape[1]
    tr = ROUTE_TILE
    return pl.pallas_call(
        _route_kernel,
        out_shape=(
            jax.ShapeDtypeStruct(((n_p + n_s) // LANES, TOP_K, LANES), I32),
            jax.ShapeDtypeStruct((SUBLANES, n_p + n_s), F32),
            jax.ShapeDtypeStruct((ne, LANES), F32),
        ),
        scratch_shapes=[pltpu.VMEM((ne, LANES), F32), pltpu.VMEM((ne, LANES), F32),
                        pltpu.VMEM((tr, tr), BF16), pltpu.VMEM(((n_p + n_s) // tr, TOP_K, tr), I32)],
        compiler_params=pltpu.CompilerParams(vmem_limit_bytes=VMEM_LIMIT),
        name="route",
    )(lgt_p, lgt_s)


def _group_metadata(counts, n_rows, tile):
    n_tiles = n_rows // tile
    n_steps = n_tiles + N_EXPERTS - 1
    ends = jnp.cumsum(counts)
    offs = jnp.concatenate([jnp.zeros((1,), I32), ends]).astype(I32)
    first_tile = offs[:-1] // tile
    last_tile = (ends - 1) // tile
    tiles_e = jnp.where(counts > 0, last_tile - first_tile + 1, 0)
    step_end = jnp.cumsum(tiles_e)
    step_start = step_end - tiles_e
    n_active = step_end[-1]
    s = jnp.minimum(jnp.arange(n_steps, dtype=I32), n_active - 1)
    owner = ((s[:, None] >= step_start[None, :]) & (s[:, None] < step_end[None, :])).astype(I32)
    gid = jnp.sum(owner * jnp.arange(N_EXPERTS, dtype=I32)[None, :], axis=1)
    tid = jnp.sum(owner * (first_tile - step_start)[None, :], axis=1) + s
    ids = jnp.arange(N_EXPERTS, dtype=I32)
    later = (ids[None, :] > ids[:, None]) & (counts[None, :] > 0)
    next_e = jnp.min(jnp.where(later, ids[None, :], N_EXPERTS), axis=1)
    next_e = jnp.where(next_e == N_EXPERTS, -1, next_e)
    nxt = jnp.sum(owner * next_e[None, :], axis=1)
    return gid, tid, nxt, offs, n_active.reshape(1).astype(I32), n_steps


SC_LANES = 16
SC_INDEX_BATCH = 128


def _invert(dest):
    r = dest.shape[0]
    mesh = plsc.VectorSubcoreMesh(core_axis_name="core", subcore_axis_name="subcore")
    per = r // mesh.num_subcores
    assert per * mesh.num_subcores == r and per % SC_INDEX_BATCH == 0

    @functools.partial(
        pl.kernel, mesh=mesh, out_type=jax.ShapeDtypeStruct((r,), I32),
        scratch_types=[pltpu.VMEM_SHARED((r,), I32), pltpu.VMEM((per,), I32), pltpu.VMEM((per,), I32)],
        compiler_params=pltpu.CompilerParams(needs_layout_passes=False),
        name="invert",
    )
    def invert(dest_hbm, inv_hbm, table, idx, ids):
        @pl.when(lax.axis_index("core") == 0)
        def _():
            base = lax.axis_index("subcore") * per
            pltpu.sync_copy(dest_hbm.at[pl.ds(base, per)], idx)
            lane = lax.iota(I32, SC_LANES)

            @pl.loop(0, per // SC_LANES)
            def _(i):
                p = base + i * SC_LANES + lane
                tok = lax.shift_right_logical(p, 9) * LANES + (p & (LANES - 1))
                slot = lax.shift_right_logical(p, 7) & (TOP_K - 1)
                ids[pl.ds(i * SC_LANES, SC_LANES)] = tok * TOP_K + slot

            @pl.loop(0, per // SC_INDEX_BATCH)
            def _(j):
                span = pl.ds(j * SC_INDEX_BATCH, SC_INDEX_BATCH)
                pltpu.sync_copy(ids.at[span], table.at[idx.at[span]])

            plsc.subcore_barrier()
            pltpu.sync_copy(table.at[pl.ds(base, per)], inv_hbm.at[pl.ds(base, per)])

    return invert(dest)


def _experts_kernel(gid_ref, tid_ref, nxt_ref, offs_ref, nact_ref,
                    dstp_ref, dst0_ref, src0_ref, src1_ref, src2_ref, bg_ref, bu_ref, bd_ref,
                    h2_hbm, wg_hbm, wu_hbm, wd_hbm, y4_hbm,
                    wg_b, wu_b, wd_b, wg_f, wu_f, wd_f, relay, rows, ybuf, sems, row_sems, y_sems):
    s = pl.program_id(0)
    tile_rows = rows.shape[1]
    tm = tile_rows // SUBLANES
    landing = ((wg_hbm, wg_f, wg_b), (wu_hbm, wu_f, wu_b), (wd_hbm, wd_f, wd_b))

    def fetch(e):
        for j, (hbm, land, _) in enumerate(landing):
            pltpu.make_async_copy(hbm.at[e], land, sems.at[j]).start()

    def row_in(idx_ref, into, r):
        src = h2_hbm.at[pl.ds(pl.multiple_of(idx_ref[r], SUBLANES), SUBLANES)]
        dst = rows.at[into, pl.ds(pl.multiple_of(r * SUBLANES, SUBLANES), SUBLANES)]
        return pltpu.make_async_copy(src, dst, row_sems.at[into])

    def row_out(idx_ref, frm, r):
        src = ybuf.at[frm, pl.ds(pl.multiple_of(r * SUBLANES, SUBLANES), SUBLANES)]
        dst = y4_hbm.at[pl.ds(pl.multiple_of(idx_ref[r], SUBLANES), SUBLANES)]
        return pltpu.make_async_copy(src, dst, y_sems.at[frm])

    def in_line(make, idx_ref, buf, first=0, count=None):
        for r in range(first, tm if count is None else first + count):
            make(idx_ref, buf, r).start(priority=r % 2)

    def in_loop(make, idx_ref, buf):
        def group(g, carry):
            for j in range(ISSUE_GROUP):
                make(idx_ref, buf, g * ISSUE_GROUP + j).start(priority=j % 2)
            return carry
        lax.fori_loop(0, tm // ISSUE_GROUP, group, 0)

    def wait_rows(buf):
        pltpu.make_async_copy(h2_hbm.at[pl.ds(0, tile_rows)], rows.at[buf], row_sems.at[buf]).wait()

    def wait_y(buf):
        pltpu.make_async_copy(ybuf.at[buf], y4_hbm.at[pl.ds(0, tile_rows)], y_sems.at[buf]).wait()

    @pl.when(s < nact_ref[0])
    def _():
        e = gid_ref[s]
        m = tid_ref[s]
        cur = lax.rem(m, 3)
        before = lax.rem(m + 2, 3)
        last = nact_ref[0] - 1
        new_tile = (s == 0) | (tid_ref[jnp.maximum(s - 1, 0)] != m)

        @pl.when(s == 0)
        def _():
            fetch(e)
            in_loop(row_in, src0_ref, 0)
            in_loop(row_in, src1_ref, 1)
            ybuf[2] = jnp.zeros(ybuf.shape[1:], F32)

        @pl.when((s == 0) | (gid_ref[jnp.maximum(s - 1, 0)] != e))
        def _():
            for j, (hbm, land, half) in enumerate(landing):
                pltpu.make_async_copy(hbm.at[e], land, sems.at[j]).wait()
                half[...] = land[...].astype(BF16)

            @pl.when(nxt_ref[s] >= 0)
            def _():
                fetch(nxt_ref[s])

        @pl.when(new_tile)
        def _():
            wait_rows(cur)

            @pl.when(m >= 2)
            def _():
                wait_y(cur)

        def ffn(load_x, after_gate=lambda: None):
            g = _dot(load_x(), wg_b[...]) + bg_ref[pl.ds(e, 1), :]
            after_gate()
            u = _dot(load_x(), wu_b[...]) + bu_ref[pl.ds(e, 1), :]
            g = jnp.minimum(g, SWIGLU_LIMIT)
            u = jnp.clip(u, -SWIGLU_LIMIT, SWIGLU_LIMIT)
            glu = g * jax.nn.sigmoid(SWIGLU_ALPHA * g)
            return _dot(((u + 1.0) * glu).astype(BF16), wd_b[...]) + bd_ref[pl.ds(e, 1), :]

        lo = offs_ref[e]
        hi = offs_ref[e + 1]
        whole_tile = (lo <= m * tm) & (hi >= (m + 1) * tm)

        def ffn_and_copies(load_x, first, count):
            pinned = count // 2
            y = ffn(load_x, lambda: in_line(row_in, src2_ref, before, first, pinned))
            in_line(row_in, src2_ref, before, first + pinned, count - pinned)
            in_line(row_out, dstp_ref, before, first, count)
            return y

        @pl.when(whole_tile)
        def _():
            y = ffn_and_copies(lambda: _load_token_tiles(rows.at[cur]).astype(BF16), 0, tm)
            _store_token_tiles(ybuf.at[cur], y)

        @pl.when(jnp.logical_not(whole_tile))
        def _():
            sub = relay.shape[0] // SUBLANES
            for j in range(tm // sub):
                first = m * tm + j * sub
                span = pl.ds(j * sub * SUBLANES, sub * SUBLANES)
                load_x = lambda span=span: _load_token_tiles(rows.at[cur, span]).astype(BF16)

                def put(y, first=first, span=span):
                    _store_token_tiles(relay, y)
                    row = first + lax.shift_right_logical(lax.broadcasted_iota(I32, relay.shape, 0), 3)
                    pltpu.store(ybuf.at[cur, span], relay[...], mask=(row >= lo) & (row < hi))

                evaluate = (lo < first + sub) & (hi > first)
                owns_end = hi >= first + sub

                @pl.when(evaluate & owns_end)
                def _():
                    put(ffn_and_copies(load_x, j * sub, sub))

                @pl.when(evaluate & jnp.logical_not(owns_end))
                def _():
                    put(ffn(load_x))

        @pl.when(s == last)
        def _():
            in_loop(row_out, dst0_ref, cur)
            wait_rows(lax.rem(m + 1, 3))
            wait_rows(before)
            for buf in range(3):
                wait_y(buf)


def _experts(gid, tid, nxt, offs, nact, n_steps, inv, h2, wg, bg, wu, bu, wd, bd):
    ne, d, f = wg.shape
    tm = EXPERT_TILE
    n_tokens = h2.shape[0] // SUBLANES
    n_tiles = inv.shape[0] // tm
    assert n_tiles >= 2
    slot_rows = (n_tokens + tm // TOP_K) * SUBLANES
    inv_ext = jnp.concatenate([n_tokens * TOP_K + jnp.arange(tm, dtype=I32), inv])
    tok = lax.shift_right_logical(inv_ext, TOP_K.bit_length() - 1)
    src_row = tok * SUBLANES
    dst_row = (inv_ext & (TOP_K - 1)) * slot_rows + src_row
    whole = lambda s, gid, tid, nxt, offs, nact: (0, 0)

    def order_of(k):
        return pl.BlockSpec(
            (tm,), lambda s, gid, tid, nxt, offs, nact: (jnp.minimum(tid[s] + k, n_tiles - 1) + 1,),
            memory_space=pltpu.SMEM)

    grid_spec = pltpu.PrefetchScalarGridSpec(
        num_scalar_prefetch=5,
        grid=(n_steps,),
        in_specs=[
            order_of(-1), order_of(0), order_of(0), order_of(1), order_of(2),
            pl.BlockSpec((ne, f), whole),
            pl.BlockSpec((ne, f), whole),
            pl.BlockSpec((ne, d), whole),
            pl.BlockSpec(memory_space=pl.ANY),
            pl.BlockSpec(memory_space=pl.ANY),
            pl.BlockSpec(memory_space=pl.ANY),
            pl.BlockSpec(memory_space=pl.ANY),
        ],
        out_specs=pl.BlockSpec(memory_space=pl.ANY),
        scratch_shapes=[
            pltpu.VMEM((d, f), BF16), pltpu.VMEM((d, f), BF16), pltpu.VMEM((f, d), BF16),
            pltpu.VMEM((d, f), F32), pltpu.VMEM((d, f), F32), pltpu.VMEM((f, d), F32),
            pltpu.VMEM((EXPERT_SUBTILE * SUBLANES, LANES), F32),
            pltpu.VMEM((3, tm * SUBLANES, LANES), F32),
            pltpu.VMEM((3, tm * SUBLANES, LANES), F32),
            pltpu.SemaphoreType.DMA((3,)),
            pltpu.SemaphoreType.DMA((3,)),
            pltpu.SemaphoreType.DMA((3,)),
        ],
    )
    y4 = pl.pallas_call(
        _experts_kernel,
        out_shape=jax.ShapeDtypeStruct((TOP_K * slot_rows, LANES), F32),
        grid_spec=grid_spec,
        compiler_params=pltpu.CompilerParams(
            dimension_semantics=("arbitrary",), vmem_limit_bytes=VMEM_LIMIT),
        name="experts",
    )(gid, tid, nxt, offs, nact, dst_row, dst_row, src_row, src_row, src_row, bg, bu, bd, h2, wg, wu, wd)
    return y4.reshape(TOP_K, slot_rows, LANES)


def _combine_kernel(gate_ref, y4_ref, x1p_ref, x1s_ref, modp_ref, g2s_ref, fn_ref, outp_ref, outs_ref,
                    *, n_prompt_tiles):
    i = pl.program_id(0)
    tm = x1p_ref.shape[0]
    gates = gate_ref[...].T
    cols = []
    for c in range(SUBLANES):
        acc = gates[:, 0:1] * y4_ref[0, pl.ds(c, tm, stride=SUBLANES), :]
        for k in range(1, TOP_K):
            acc = acc + gates[:, k:k + 1] * y4_ref[k, pl.ds(c, tm, stride=SUBLANES), :]
        cols.append(acc)
    ffn = jnp.concatenate(cols, axis=-1)

    @pl.when(i < n_prompt_tiles)
    def _():
        g2 = modp_ref[...][5:6, :]
        outp_ref[...] = _rmsnorm(x1p_ref[...] + g2 * ffn, fn_ref[...])

    @pl.when(i >= n_prompt_tiles)
    def _():
        g2 = jnp.concatenate([g2s_ref[...]] * (tm // g2s_ref.shape[0]), axis=0)
        outs_ref[...] = _rmsnorm(x1s_ref[...] + g2 * ffn, fn_ref[...])


def _combine(gates, y4, x1_p, x1_s, mod_p, mod, n_batch_s, final_norm, tokens_per_seq):
    n_p, d = x1_p.shape
    n_s = x1_s.shape[0]
    tm = COMBINE_TILE
    assert tm % n_batch_s == 0 and n_batch_s % SUBLANES == 0
    npt, nst = n_p // tm, n_s // tm
    tiles_per_seq = tokens_per_seq // tm
    pmap = lambda i: (jnp.minimum(i, npt - 1), 0)
    smap = lambda i: (jnp.maximum(i - npt, 0), 0)
    return pl.pallas_call(
        functools.partial(_combine_kernel, n_prompt_tiles=npt),
        out_shape=(jax.ShapeDtypeStruct((n_p, d), F32), jax.ShapeDtypeStruct((n_s, d), F32)),
        grid=(npt + nst,),
        in_specs=[
            pl.BlockSpec((SUBLANES, tm), lambda i: (0, i)),
            pl.BlockSpec((TOP_K, tm * SUBLANES, LANES), lambda i: (0, i, 0)),
            pl.BlockSpec((tm, d), pmap),
            pl.BlockSpec((tm, d), smap),
            pl.BlockSpec((None, 6, d), lambda i: (jnp.minimum(i, npt - 1) // tiles_per_seq, 0, 0)),
            pl.BlockSpec((n_batch_s, d), lambda i: (0, 5)),
            pl.BlockSpec((1, d), lambda i: (0, 0)),
        ],
        out_specs=(pl.BlockSpec((tm, d), pmap), pl.BlockSpec((tm, d), smap)),
        compiler_params=pltpu.CompilerParams(
            dimension_semantics=("arbitrary",), vmem_limit_bytes=VMEM_LIMIT),
        name="combine",
    )(gates, y4, x1_p, x1_s, mod_p, mod, final_norm)


def kernel(x_prompt, x_sample, state_pool, state_conv, c_prompt, c_sample, norm1, norm2, w_ada, b_ada,
           w_in, w_pool, pool_scale, w_conv, w_out, w_router, b_router, w_gate, b_gate, w_up, b_up,
           w_down, b_down, final_norm):
    depth = norm1.shape[0]
    assert depth == 1, "single-layer step"
    bp, tp, d = x_prompt.shape
    bs, ts, _ = x_sample.shape
    dp = state_pool.shape[-1]
    n_hist = state_pool.shape[2]
    n_chist = state_conv.shape[2]
    n_p, n_s = bp * tp, bs * ts
    assert d == SUBLANES * LANES, "token-tile layout assumes one vreg tile per token row"
    assert tp % TOKEN_TILE == 0 and n_s == TOKEN_TILE, "the sample group fills exactly one token tile"
    assert tp % COMBINE_TILE == 0 and n_s % COMBINE_TILE == 0
    assert (n_p + n_s) % ROUTE_TILE == 0 and ((n_p + n_s) * TOP_K) % EXPERT_TILE == 0

    l = 0
    n1 = norm1[l].reshape(1, d)
    n2 = norm2[l].reshape(1, d)
    pscale = pool_scale[l].reshape(1, dp)
    w_r = w_router[l].T.astype(BF16)
    b_r = b_router[l].reshape(N_EXPERTS, 1)

    mod = _adaln(jnp.concatenate([c_sample, c_prompt], axis=0), w_ada[l], b_ada[l])
    mod_p = mod[bs:].reshape(bp, 6, d)

    xs_tm = jnp.transpose(x_sample, (1, 0, 2)).reshape(n_s, d)
    ps_tm = jnp.transpose(state_pool[l], (1, 0, 2))
    cs_tm = jnp.transpose(state_conv[l], (1, 0, 2))
    x1_s, h2_s, lg_s, newp_tm, newc_tm, w_in_b, w_pool_b, w_out_b = _mixer_sample(
        xs_tm, mod, ps_tm, cs_tm, n1, n2, w_in[l], w_pool[l], pscale, w_conv[l], w_out[l], w_r, b_r, ts)

    x1_p, h2, lg_p, u_tail, v_tail = _mixer_prompt(
        x_prompt, mod_p, n1, n2, w_in_b, w_pool_b, pscale, w_conv[l], w_out_b, w_r, b_r, h2_s)

    dest, gates, counts_f = _route(lg_p, lg_s)
    counts = counts_f[:, 0].astype(I32)
    n_rows = (n_p + n_s) * TOP_K
    gid, tid, nxt, offs, nact, n_steps = _group_metadata(counts, n_rows, EXPERT_TILE)

    dest = dest.reshape(-1)
    inv = _invert(dest)
    y4 = _experts(gid, tid, nxt, offs, nact, n_steps, inv, h2,
                  w_gate[l], b_gate[l], w_up[l], b_up[l], w_down[l], b_down[l])

    y_p, y_s = _combine(gates, y4, x1_p, x1_s, mod_p, mod, bs, final_norm.reshape(1, d), tp)

    y_prompt = y_p.reshape(bp, tp, d)
    y_sample = jnp.transpose(y_s.reshape(ts, bs, d), (1, 0, 2))
    new_pool_prompt = u_tail[:, POOL_HALO - n_hist:, :][None]
    new_conv_prompt = v_tail[:, CONV_HALO - n_chist:, :][None]
    new_pool_sample = jnp.transpose(newp_tm, (1, 0, 2))[None]
    new_conv_sample = jnp.transpose(newc_tm, (1, 0, 2))[None]
    return (y_prompt, y_sample, new_pool_prompt, new_conv_prompt, new_pool_sample, new_conv_sample)
```

```python
import functools

import jax
import jax.numpy as jnp
from jax import lax
from jax.experimental import pallas as pl
from jax.experimental.pallas import tpu as pltpu
from jax.experimental.pallas import tpu_sc as plsc

F32 = jnp.float32
BF16 = jnp.bfloat16
I32 = jnp.int32

POOL_WINDOWS = (2, 4, 8, 16)
POOL_HALO = 16
CONV_TAPS = 3
CONV_HALO = 8
N_EXPERTS = 32
TOP_K = 4
SWIGLU_LIMIT = 7.0
SWIGLU_ALPHA = 1.702
EPS = 1e-5
PAST_LEN = 16384

LANES = 128
SUBLANES = 8

TOKEN_TILE = 512
ROUTE_TILE = 512
COMBINE_TILE = 512
EXPERT_TILE = 512
EXPERT_SUBTILE = 256
VMEM_LIMIT = 56 * 1024 * 1024


def _rmsnorm(x, g):
    ms = jnp.mean(x * x, axis=-1, keepdims=True)
    return x * lax.rsqrt(ms + EPS) * g


def _dot(a, b):
    return jnp.dot(a, b, preferred_element_type=F32)


def _store_token_tiles(ref, val):
    rows = val.shape[0]
    for c in range(SUBLANES):
        ref[pl.ds(c, rows, stride=SUBLANES), :] = val[:, c * LANES:(c + 1) * LANES]


def _load_token_tiles(ref):
    rows = ref.shape[0] // SUBLANES
    return jnp.concatenate([ref[pl.ds(c, rows, stride=SUBLANES), :] for c in range(SUBLANES)], axis=-1)


ISSUE_GROUP = 4


def _adaln_kernel(c_ref, w_ref, b_ref, o_ref):
    c = c_ref[...]
    s = c * jax.nn.sigmoid(c)
    o_ref[...] = _dot(s.astype(BF16), w_ref[...].astype(BF16)) + b_ref[...]


def _adaln(c, w_ada, b_ada):
    rows, d = c.shape
    n = w_ada.shape[1]
    tn = 1024
    return pl.pallas_call(
        _adaln_kernel,
        out_shape=jax.ShapeDtypeStruct((rows, n), F32),
        grid=(n // tn,),
        in_specs=[
            pl.BlockSpec((rows, d), lambda j: (0, 0)),
            pl.BlockSpec((d, tn), lambda j: (0, j)),
            pl.BlockSpec((1, tn), lambda j: (0, j)),
        ],
        out_specs=pl.BlockSpec((rows, tn), lambda j: (0, j)),
        compiler_params=pltpu.CompilerParams(
            dimension_semantics=("arbitrary",), vmem_limit_bytes=VMEM_LIMIT),
        name="adaln",
    )(c, w_ada, b_ada.reshape(1, n))


def _mix_tail(x, pool_in, conv_out, g1, sc2, sh2, n2, wpool_ref, pscale, wout, wr, br):
    gw = pool_in.shape[1] // len(POOL_WINDOWS)
    mixed = [_dot(pool_in[:, g * gw:(g + 1) * gw].astype(BF16), wpool_ref[g])
             for g in range(len(POOL_WINDOWS))]
    pool_out = jnp.concatenate(mixed, axis=-1) * pscale
    mix_in = jnp.concatenate([pool_out, conv_out], axis=-1).astype(BF16)
    x1 = x + g1 * _dot(mix_in, wout)
    h2 = _rmsnorm(x1, n2) * (1.0 + sc2) + sh2
    logits_t = lax.dot_general(wr, h2.astype(BF16), (((1,), (1,)), ((), ())),
                               preferred_element_type=F32) + br
    return x1, h2, logits_t


def _mixer_prompt_kernel(x_ref, mod_ref, n1_ref, n2_ref, win_ref, wpool_ref, pscale_ref, wconv_ref,
                         wout_ref, wr_ref, br_ref, h2s_ref,
                         x1_ref, h2_ref, lg_ref, upool_ref, vconv_ref, ubuf, vbuf, *, tiles_per_seq):
    i = pl.program_id(0)

    @pl.when(i < pl.num_programs(0) - 1)
    def _():
        _mixer_prompt_tile(x_ref, mod_ref, n1_ref, n2_ref, win_ref, wpool_ref, pscale_ref, wconv_ref,
                           wout_ref, wr_ref, br_ref, x1_ref, h2_ref, lg_ref, upool_ref, vconv_ref,
                           ubuf, vbuf, lax.rem(i, tiles_per_seq), tiles_per_seq)

    @pl.when(i == pl.num_programs(0) - 1)
    def _():
        h2_ref[...] = h2s_ref[...]


def _mixer_prompt_tile(x_ref, mod_ref, n1_ref, n2_ref, win_ref, wpool_ref, pscale_ref, wconv_ref,
                       wout_ref, wr_ref, br_ref, x1_ref, h2_ref, lg_ref, upool_ref, vconv_ref,
                       ubuf, vbuf, t, tiles_per_seq):
    tt = x_ref.shape[0]
    dp = ubuf.shape[1]
    gw = dp // len(POOL_WINDOWS)

    @pl.when(t == 0)
    def _():
        ubuf[0:POOL_HALO, :] = jnp.zeros((POOL_HALO, dp), F32)
        vbuf[0:CONV_HALO, :] = jnp.zeros((CONV_HALO, dp), F32)

    x = x_ref[...]
    mod = mod_ref[...]
    sh1, sc1, g1, sh2, sc2, _ = [mod[i:i + 1, :] for i in range(6)]
    h = _rmsnorm(x, n1_ref[...]) * (1.0 + sc1) + sh1
    z = _dot(h.astype(BF16), win_ref[...])
    u, gate_b, gate_c, val = [z[:, i * dp:(i + 1) * dp] for i in range(4)]

    ubuf[POOL_HALO:POOL_HALO + tt, :] = u
    pos = lax.broadcasted_iota(I32, (tt, gw), 0) + t * tt
    pooled = []
    for g, w in enumerate(POOL_WINDOWS):
        cols = slice(g * gw, (g + 1) * gw)
        acc = u[:, cols]
        for j in range(1, w):
            acc = acc + ubuf[POOL_HALO - j:POOL_HALO - j + tt, cols]
        cnt = jnp.minimum(pos + 1, w).astype(F32)
        pooled.append(acc / cnt - u[:, cols])
    pool_in = jnp.concatenate(pooled, axis=-1)

    v = gate_c * val
    vbuf[CONV_HALO:CONV_HALO + tt, :] = v
    wc = wconv_ref[...]
    y = (wc[0:1, :] * vbuf[CONV_HALO - 2:CONV_HALO - 2 + tt, :]
         + wc[1:2, :] * vbuf[CONV_HALO - 1:CONV_HALO - 1 + tt, :]
         + wc[2:3, :] * v)
    conv_out = gate_b * y

    x1, h2, logits = _mix_tail(x, pool_in, conv_out, g1, sc2, sh2, n2_ref[...], wpool_ref,
                               pscale_ref[...], wout_ref[...], wr_ref[...], br_ref[...])
    x1_ref[...] = x1
    _store_token_tiles(h2_ref, h2)
    lg_ref[...] = logits

    ubuf[0:POOL_HALO, :] = ubuf[tt:tt + POOL_HALO, :]
    vbuf[0:CONV_HALO, :] = vbuf[tt:tt + CONV_HALO, :]

    @pl.when(t == tiles_per_seq - 1)
    def _():
        upool_ref[...] = ubuf[0:POOL_HALO, :]
        vconv_ref[...] = vbuf[0:CONV_HALO, :]


def _mixer_prompt(x, mod_p, n1, n2, w_in, w_pool, pscale, w_conv, w_out, w_r, b_r, h2_s):
    b, t, d = x.shape
    dp = w_pool.shape[0] * w_pool.shape[1]
    tt = min(TOKEN_TILE, t)
    nt = t // tt
    n_real = b * nt
    assert h2_s.shape[0] == tt * SUBLANES
    seq = lambda i: jnp.minimum(i, n_real - 1) // nt
    tile = lambda i: jnp.minimum(i, n_real - 1)
    const2 = lambda i: (0, 0)
    const3 = lambda i: (0, 0, 0)
    return pl.pallas_call(
        functools.partial(_mixer_prompt_kernel, tiles_per_seq=nt),
        out_shape=(
            jax.ShapeDtypeStruct((b * t, d), F32),
            jax.ShapeDtypeStruct(((n_real + 1) * tt * SUBLANES, LANES), F32),
            jax.ShapeDtypeStruct((N_EXPERTS, b * t), F32),
            jax.ShapeDtypeStruct((b, POOL_HALO, dp), F32),
            jax.ShapeDtypeStruct((b, CONV_HALO, dp), F32),
        ),
        grid=(n_real + 1,),
        in_specs=[
            pl.BlockSpec((None, tt, d), lambda i: (seq(i), tile(i) % nt, 0)),
            pl.BlockSpec((None, 6, d), lambda i: (seq(i), 0, 0)),
            pl.BlockSpec((1, d), const2),
            pl.BlockSpec((1, d), const2),
            pl.BlockSpec(w_in.shape, const2),
            pl.BlockSpec(w_pool.shape, const3),
            pl.BlockSpec((1, dp), const2),
            pl.BlockSpec(w_conv.shape, const2),
            pl.BlockSpec(w_out.shape, const2),
            pl.BlockSpec(w_r.shape, const2),
            pl.BlockSpec((N_EXPERTS, 1), const2),
            pl.BlockSpec(h2_s.shape, const2),
        ],
        out_specs=(
            pl.BlockSpec((tt, d), lambda i: (tile(i), 0)),
            pl.BlockSpec((tt * SUBLANES, LANES), lambda i: (i, 0)),
            pl.BlockSpec((N_EXPERTS, tt), lambda i: (0, tile(i))),
            pl.BlockSpec((None, POOL_HALO, dp), lambda i: (seq(i), 0, 0)),
            pl.BlockSpec((None, CONV_HALO, dp), lambda i: (seq(i), 0, 0)),
        ),
        scratch_shapes=[
            pltpu.VMEM((POOL_HALO + tt, dp), F32),
            pltpu.VMEM((CONV_HALO + tt, dp), F32),
        ],
        compiler_params=pltpu.CompilerParams(
            dimension_semantics=("arbitrary",), vmem_limit_bytes=VMEM_LIMIT),
        name="mixer_prompt",
    )(x, mod_p, n1, n2, w_in, w_pool, pscale, w_conv, w_out, w_r, b_r, h2_s)


def _mixer_sample_kernel(x_ref, mod_ref, pstate_ref, cstate_ref, n1_ref, n2_ref, win_f32, wpool_f32,
                         pscale_ref, wconv_ref, wout_f32, wr_ref, br_ref,
                         x1_ref, h2_ref, lg_ref, newp_ref, newc_ref, win_ref, wpool_ref, wout_ref,
                         *, steps):
    win_ref[...] = win_f32[...].astype(BF16)
    wpool_ref[...] = wpool_f32[...].astype(BF16)
    wout_ref[...] = wout_f32[...].astype(BF16)
    nb = pstate_ref.shape[1]
    d = x_ref.shape[1]
    dp = pstate_ref.shape[2]
    gw = dp // len(POOL_WINDOWS)
    n_hist = pstate_ref.shape[0]
    n_chist = cstate_ref.shape[0]

    x = x_ref[...]
    mod = mod_ref[0:nb, :]
    rep = lambda a: jnp.concatenate([a] * steps, axis=0)
    sh1, sc1, g1, sh2, sc2, _ = [rep(mod[:, i * d:(i + 1) * d]) for i in range(6)]
    h = _rmsnorm(x, n1_ref[...]) * (1.0 + sc1) + sh1
    z = _dot(h.astype(BF16), win_ref[...])
    u, gate_b, gate_c, val = [z[:, i * dp:(i + 1) * dp] for i in range(4)]

    ext = [pstate_ref[i] for i in range(n_hist)] + [u[s * nb:(s + 1) * nb, :] for s in range(steps)]
    pooled_steps = []
    for s in range(steps):
        groups = []
        for g, w in enumerate(POOL_WINDOWS):
            cols = slice(g * gw, (g + 1) * gw)
            acc = ext[n_hist + s][:, cols]
            for j in range(1, w):
                acc = acc + ext[n_hist + s - j][:, cols]
            cnt = float(min(PAST_LEN + s + 1, w))
            groups.append(acc / cnt - ext[n_hist + s][:, cols])
        pooled_steps.append(jnp.concatenate(groups, axis=-1))
    pool_in = jnp.concatenate(pooled_steps, axis=0)

    v = gate_c * val
    vext = [cstate_ref[i] for i in range(n_chist)] + [v[s * nb:(s + 1) * nb, :] for s in range(steps)]
    wc = wconv_ref[...]
    y = jnp.concatenate(
        [wc[0:1, :] * vext[s] + wc[1:2, :] * vext[s + 1] + wc[2:3, :] * vext[s + 2] for s in range(steps)],
        axis=0)
    conv_out = gate_b * y

    x1, h2, logits = _mix_tail(x, pool_in, conv_out, g1, sc2, sh2, n2_ref[...], wpool_ref,
                               pscale_ref[...], wout_ref[...], wr_ref[...], br_ref[...])
    x1_ref[...] = x1
    _store_token_tiles(h2_ref, h2)
    lg_ref[...] = logits
    for i in range(n_hist):
        newp_ref[i] = ext[steps + i]
    for i in range(n_chist):
        newc_ref[i] = vext[steps + i]


def _mixer_sample(x_tm, mod_s, pstate_tm, cstate_tm, n1, n2, w_in, w_pool, pscale, w_conv, w_out, w_r, b_r,
                  steps):
    rows, d = x_tm.shape
    return pl.pallas_call(
        functools.partial(_mixer_sample_kernel, steps=steps),
        out_shape=(
            jax.ShapeDtypeStruct((rows, d), F32),
            jax.ShapeDtypeStruct((rows * SUBLANES, LANES), F32),
            jax.ShapeDtypeStruct((N_EXPERTS, rows), F32),
            jax.ShapeDtypeStruct(pstate_tm.shape, F32),
            jax.ShapeDtypeStruct(cstate_tm.shape, F32),
            jax.ShapeDtypeStruct(w_in.shape, BF16),
            jax.ShapeDtypeStruct(w_pool.shape, BF16),
            jax.ShapeDtypeStruct(w_out.shape, BF16),
        ),
        compiler_params=pltpu.CompilerParams(vmem_limit_bytes=VMEM_LIMIT),
        name="mixer_sample",
    )(x_tm, mod_s, pstate_tm, cstate_tm, n1, n2, w_in, w_pool, pscale, w_conv, w_out, w_r, b_r)


def _route_kernel(lgp_ref, lgs_ref, dest_ref, gate_ref, cnt_ref, counts, start, before, chosen):
    ne = lgp_ref.shape[0]
    tr = before.shape[0]
    reps = tr // LANES
    n_prompt_chunks = lgp_ref.shape[1] // tr
    n_sample_chunks = lgs_ref.shape[1] // tr
    eidx = lax.broadcasted_iota(I32, (ne, tr), 0)

    def selected(onehots):
        return jnp.where(onehots[0] | onehots[1] | onehots[2] | onehots[3], 1.0, 0.0)

    def select_chunk(ref, first_chunk):
        def body(c, carry):
            chunk = first_chunk + c
            work = ref[:, pl.ds(pl.multiple_of(c * tr, tr), tr)]
            top_v, onehots = [], []
            for k in range(TOP_K):
                m = jnp.max(work, axis=0, keepdims=True)
                idx = jnp.min(jnp.where(work == m, eidx, ne), axis=0, keepdims=True)
                chosen[chunk, k:k + 1, :] = idx
                top_v.append(m)
                onehots.append(eidx == idx)
                work = jnp.where(onehots[k], -jnp.inf, work)
            es = [jnp.exp(v - top_v[0]) for v in top_v]
            denom = es[0] + es[1] + es[2] + es[3]
            cols = pl.ds(pl.multiple_of(chunk * tr, tr), tr)
            for k in range(TOP_K):
                gate_ref[k:k + 1, cols] = es[k] / denom
            chunk_counts = jnp.sum(selected(onehots), axis=1, keepdims=True)
            counts[...] = counts[...] + jnp.broadcast_to(chunk_counts, (ne, LANES))
            return carry
        return body

    counts[...] = jnp.zeros_like(counts)
    gate_ref[...] = jnp.zeros_like(gate_ref)
    lax.fori_loop(0, n_prompt_chunks, select_chunk(lgp_ref, 0), 0)
    lax.fori_loop(0, n_sample_chunks, select_chunk(lgs_ref, n_prompt_chunks), 0)

    total = counts[...]
    hi = jnp.floor(total * (1.0 / 256.0))
    lo = total - hi * 256.0
    r = lax.broadcasted_iota(I32, (ne, ne), 0)
    col = lax.broadcasted_iota(I32, (ne, ne), 1)
    lower = jnp.where(col < r, 1.0, 0.0).astype(BF16)
    start[...] = 256.0 * _dot(lower, hi.astype(BF16)) + _dot(lower, lo.astype(BF16))
    cnt_ref[...] = total
    counts[...] = jnp.zeros_like(counts)

    r = lax.broadcasted_iota(I32, (tr, tr), 0)
    col = lax.broadcasted_iota(I32, (tr, tr), 1)
    before[...] = jnp.where(r < col, 1.0, 0.0).astype(BF16)

    def place_chunk(chunk, carry):
        onehots = [eidx == chosen[chunk, k:k + 1, :] for k in range(TOP_K)]
        mask = selected(onehots)
        base = jnp.concatenate([counts[...] + start[...]] * reps, axis=1)
        rank = _dot(mask.astype(BF16), before[...]) + base
        counts[...] = counts[...] + jnp.broadcast_to(jnp.sum(mask, axis=1, keepdims=True), (ne, LANES))
        for k in range(TOP_K):
            d = jnp.sum(jnp.where(onehots[k], rank, 0.0), axis=0, keepdims=True).astype(I32)
            for j in range(reps):
                dest_ref[chunk * reps + j, k:k + 1, :] = d[:, j * LANES:(j + 1) * LANES]
        return carry

    lax.fori_loop(0, n_prompt_chunks + n_sample_chunks, place_chunk, 0)


def _route(lgt_p, lgt_s):
    ne, n_p = lgt_p.shape
    n_s = lgt_s.shape[1]
    tr = ROUTE_TILE
    return pl.pallas_call(
        _route_kernel,
        out_shape=(
            jax.ShapeDtypeStruct(((n_p + n_s) // LANES, TOP_K, LANES), I32),
            jax.ShapeDtypeStruct((SUBLANES, n_p + n_s), F32),
            jax.ShapeDtypeStruct((ne, LANES), F32),
        ),
        scratch_shapes=[pltpu.VMEM((ne, LANES), F32), pltpu.VMEM((ne, LANES), F32),
                        pltpu.VMEM((tr, tr), BF16), pltpu.VMEM(((n_p + n_s) // tr, TOP_K, tr), I32)],
        compiler_params=pltpu.CompilerParams(vmem_limit_bytes=VMEM_LIMIT),
        name="route",
    )(lgt_p, lgt_s)


def _group_metadata(counts, n_rows, tile):
    n_tiles = n_rows // tile
    n_steps = n_tiles + N_EXPERTS - 1
    ends = jnp.cumsum(counts)
    offs = jnp.concatenate([jnp.zeros((1,), I32), ends]).astype(I32)
    first_tile = offs[:-1] // tile
    last_tile = (ends - 1) // tile
    tiles_e = jnp.where(counts > 0, last_tile - first_tile + 1, 0)
    step_end = jnp.cumsum(tiles_e)
    step_start = step_end - tiles_e
    n_active = step_end[-1]
    s = jnp.minimum(jnp.arange(n_steps, dtype=I32), n_active - 1)
    owner = ((s[:, None] >= step_start[None, :]) & (s[:, None] < step_end[None, :])).astype(I32)
    gid = jnp.sum(owner * jnp.arange(N_EXPERTS, dtype=I32)[None, :], axis=1)
    tid = jnp.sum(owner * (first_tile - step_start)[None, :], axis=1) + s
    ids = jnp.arange(N_EXPERTS, dtype=I32)
    later = (ids[None, :] > ids[:, None]) & (counts[None, :] > 0)
    next_e = jnp.min(jnp.where(later, ids[None, :], N_EXPERTS), axis=1)
    next_e = jnp.where(next_e == N_EXPERTS, -1, next_e)
    nxt = jnp.sum(owner * next_e[None, :], axis=1)
    return gid, tid, nxt, offs, n_active.reshape(1).astype(I32), n_steps


SC_LANES = 16
SC_INDEX_BATCH = 128


def _invert(dest):
    r = dest.shape[0]
    mesh = plsc.VectorSubcoreMesh(core_axis_name="core", subcore_axis_name="subcore")
    per = r // mesh.num_subcores
    assert per * mesh.num_subcores == r and per % SC_INDEX_BATCH == 0

    @functools.partial(
        pl.kernel, mesh=mesh, out_type=jax.ShapeDtypeStruct((r,), I32),
        scratch_types=[pltpu.VMEM_SHARED((r,), I32), pltpu.VMEM((per,), I32), pltpu.VMEM((per,), I32)],
        compiler_params=pltpu.CompilerParams(needs_layout_passes=False),
        name="invert",
    )
    def invert(dest_hbm, inv_hbm, table, idx, ids):
        @pl.when(lax.axis_index("core") == 0)
        def _():
            base = lax.axis_index("subcore") * per
            pltpu.sync_copy(dest_hbm.at[pl.ds(base, per)], idx)
            lane = lax.iota(I32, SC_LANES)

            @pl.loop(0, per // SC_LANES)
            def _(i):
                p = base + i * SC_LANES + lane
                tok = lax.shift_right_logical(p, 9) * LANES + (p & (LANES - 1))
                slot = lax.shift_right_logical(p, 7) & (TOP_K - 1)
                ids[pl.ds(i * SC_LANES, SC_LANES)] = tok * TOP_K + slot

            @pl.loop(0, per // SC_INDEX_BATCH)
            def _(j):
                span = pl.ds(j * SC_INDEX_BATCH, SC_INDEX_BATCH)
                pltpu.sync_copy(ids.at[span], table.at[idx.at[span]])

            plsc.subcore_barrier()
            pltpu.sync_copy(table.at[pl.ds(base, per)], inv_hbm.at[pl.ds(base, per)])

    return invert(dest)


def _experts_kernel(gid_ref, tid_ref, nxt_ref, offs_ref, nact_ref,
                    dstp_ref, dst0_ref, src0_ref, src1_ref, src2_ref, bg_ref, bu_ref, bd_ref,
                    h2_hbm, wg_hbm, wu_hbm, wd_hbm, y4_hbm,
                    wg_b, wu_b, wd_b, wg_f, wu_f, wd_f, relay, rows, ybuf, sems, row_sems, y_sems):
    s = pl.program_id(0)
    tile_rows = rows.shape[1]
    tm = tile_rows // SUBLANES
    landing = ((wg_hbm, wg_f, wg_b), (wu_hbm, wu_f, wu_b), (wd_hbm, wd_f, wd_b))

    def fetch(e):
        for j, (hbm, land, _) in enumerate(landing):
            pltpu.make_async_copy(hbm.at[e], land, sems.at[j]).start()

    def row_in(idx_ref, into, r):
        src = h2_hbm.at[pl.ds(pl.multiple_of(idx_ref[r], SUBLANES), SUBLANES)]
        dst = rows.at[into, pl.ds(pl.multiple_of(r * SUBLANES, SUBLANES), SUBLANES)]
        return pltpu.make_async_copy(src, dst, row_sems.at[into])

    def row_out(idx_ref, frm, r):
        src = ybuf.at[frm, pl.ds(pl.multiple_of(r * SUBLANES, SUBLANES), SUBLANES)]
        dst = y4_hbm.at[pl.ds(pl.multiple_of(idx_ref[r], SUBLANES), SUBLANES)]
        return pltpu.make_async_copy(src, dst, y_sems.at[frm])

    def in_line(make, idx_ref, buf, first=0, count=None):
        for r in range(first, tm if count is None else first + count):
            make(idx_ref, buf, r).start(priority=r % 2)

    def in_loop(make, idx_ref, buf):
        def group(g, carry):
            for j in range(ISSUE_GROUP):
                make(idx_ref, buf, g * ISSUE_GROUP + j).start(priority=j % 2)
            return carry
        lax.fori_loop(0, tm // ISSUE_GROUP, group, 0)

    def wait_rows(buf):
        pltpu.make_async_copy(h2_hbm.at[pl.ds(0, tile_rows)], rows.at[buf], row_sems.at[buf]).wait()

    def wait_y(buf):
        pltpu.make_async_copy(ybuf.at[buf], y4_hbm.at[pl.ds(0, tile_rows)], y_sems.at[buf]).wait()

    @pl.when(s < nact_ref[0])
    def _():
        e = gid_ref[s]
        m = tid_ref[s]
        cur = lax.rem(m, 3)
        before = lax.rem(m + 2, 3)
        last = nact_ref[0] - 1
        new_tile = (s == 0) | (tid_ref[jnp.maximum(s - 1, 0)] != m)

        @pl.when(s == 0)
        def _():
            fetch(e)
            in_loop(row_in, src0_ref, 0)
            in_loop(row_in, src1_ref, 1)
            ybuf[2] = jnp.zeros(ybuf.shape[1:], F32)

        @pl.when((s == 0) | (gid_ref[jnp.maximum(s - 1, 0)] != e))
        def _():
            for j, (hbm, land, half) in enumerate(landing):
                pltpu.make_async_copy(hbm.at[e], land, sems.at[j]).wait()
                half[...] = land[...].astype(BF16)

            @pl.when(nxt_ref[s] >= 0)
            def _():
                fetch(nxt_ref[s])

        @pl.when(new_tile)
        def _():
            wait_rows(cur)

            @pl.when(m >= 2)
            def _():
                wait_y(cur)

        def ffn(load_x, after_gate=lambda: None):
            g = _dot(load_x(), wg_b[...]) + bg_ref[pl.ds(e, 1), :]
            after_gate()
            u = _dot(load_x(), wu_b[...]) + bu_ref[pl.ds(e, 1), :]
            g = jnp.minimum(g, SWIGLU_LIMIT)
            u = jnp.clip(u, -SWIGLU_LIMIT, SWIGLU_LIMIT)
            glu = g * jax.nn.sigmoid(SWIGLU_ALPHA * g)
            return _dot(((u + 1.0) * glu).astype(BF16), wd_b[...]) + bd_ref[pl.ds(e, 1), :]

        lo = offs_ref[e]
        hi = offs_ref[e + 1]
        whole_tile = (lo <= m * tm) & (hi >= (m + 1) * tm)

        def ffn_and_copies(load_x, first, count):
            pinned = count // 2
            y = ffn(load_x, lambda: in_line(row_in, src2_ref, before, first, pinned))
            in_line(row_in, src2_ref, before, first + pinned, count - pinned)
            in_line(row_out, dstp_ref, before, first, count)
            return y

        @pl.when(whole_tile)
        def _():
            y = ffn_and_copies(lambda: _load_token_tiles(rows.at[cur]).astype(BF16), 0, tm)
            _store_token_tiles(ybuf.at[cur], y)

        @pl.when(jnp.logical_not(whole_tile))
        def _():
            sub = relay.shape[0] // SUBLANES
            for j in range(tm // sub):
                first = m * tm + j * sub
                span = pl.ds(j * sub * SUBLANES, sub * SUBLANES)
                load_x = lambda span=span: _load_token_tiles(rows.at[cur, span]).astype(BF16)

                def put(y, first=first, span=span):
                    _store_token_tiles(relay, y)
                    row = first + lax.shift_right_logical(lax.broadcasted_iota(I32, relay.shape, 0), 3)
                    pltpu.store(ybuf.at[cur, span], relay[...], mask=(row >= lo) & (row < hi))

                evaluate = (lo < first + sub) & (hi > first)
                owns_end = hi >= first + sub

                @pl.when(evaluate & owns_end)
                def _():
                    put(ffn_and_copies(load_x, j * sub, sub))

                @pl.when(evaluate & jnp.logical_not(owns_end))
                def _():
                    put(ffn(load_x))

        @pl.when(s == last)
        def _():
            in_loop(row_out, dst0_ref, cur)
            wait_rows(lax.rem(m + 1, 3))
            wait_rows(before)
            for buf in range(3):
                wait_y(buf)


def _experts(gid, tid, nxt, offs, nact, n_steps, inv, h2, wg, bg, wu, bu, wd, bd):
    ne, d, f = wg.shape
    tm = EXPERT_TILE
    n_tokens = h2.shape[0] // SUBLANES
    n_tiles = inv.shape[0] // tm
    assert n_tiles >= 2
    slot_rows = (n_tokens + tm // TOP_K) * SUBLANES
    inv_ext = jnp.concatenate([n_tokens * TOP_K + jnp.arange(tm, dtype=I32), inv])
    tok = lax.shift_right_logical(inv_ext, TOP_K.bit_length() - 1)
    src_row = tok * SUBLANES
    dst_row = (inv_ext & (TOP_K - 1)) * slot_rows + src_row
    whole = lambda s, gid, tid, nxt, offs, nact: (0, 0)

    def order_of(k):
        return pl.BlockSpec(
            (tm,), lambda s, gid, tid, nxt, offs, nact: (jnp.minimum(tid[s] + k, n_tiles - 1) + 1,),
            memory_space=pltpu.SMEM)

    grid_spec = pltpu.PrefetchScalarGridSpec(
        num_scalar_prefetch=5,
        grid=(n_steps,),
        in_specs=[
            order_of(-1), order_of(0), order_of(0), order_of(1), order_of(2),
            pl.BlockSpec((ne, f), whole),
            pl.BlockSpec((ne, f), whole),
            pl.BlockSpec((ne, d), whole),
            pl.BlockSpec(memory_space=pl.ANY),
            pl.BlockSpec(memory_space=pl.ANY),
            pl.BlockSpec(memory_space=pl.ANY),
            pl.BlockSpec(memory_space=pl.ANY),
        ],
        out_specs=pl.BlockSpec(memory_space=pl.ANY),
        scratch_shapes=[
            pltpu.VMEM((d, f), BF16), pltpu.VMEM((d, f), BF16), pltpu.VMEM((f, d), BF16),
            pltpu.VMEM((d, f), F32), pltpu.VMEM((d, f), F32), pltpu.VMEM((f, d), F32),
            pltpu.VMEM((EXPERT_SUBTILE * SUBLANES, LANES), F32),
            pltpu.VMEM((3, tm * SUBLANES, LANES), F32),
            pltpu.VMEM((3, tm * SUBLANES, LANES), F32),
            pltpu.SemaphoreType.DMA((3,)),
            pltpu.SemaphoreType.DMA((3,)),
            pltpu.SemaphoreType.DMA((3,)),
        ],
    )
    y4 = pl.pallas_call(
        _experts_kernel,
        out_shape=jax.ShapeDtypeStruct((TOP_K * slot_rows, LANES), F32),
        grid_spec=grid_spec,
        compiler_params=pltpu.CompilerParams(
            dimension_semantics=("arbitrary",), vmem_limit_bytes=VMEM_LIMIT),
        name="experts",
    )(gid, tid, nxt, offs, nact, dst_row, dst_row, src_row, src_row, src_row, bg, bu, bd, h2, wg, wu, wd)
    return y4.reshape(TOP_K, slot_rows, LANES)


def _combine_kernel(gate_ref, y4_ref, x1p_ref, x1s_ref, modp_ref, g2s_ref, fn_ref, outp_ref, outs_ref,
                    *, n_prompt_tiles):
    i = pl.program_id(0)
    tm = x1p_ref.shape[0]
    gates = gate_ref[...].T
    cols = []
    for c in range(SUBLANES):
        acc = gates[:, 0:1] * y4_ref[0, pl.ds(c, tm, stride=SUBLANES), :]
        for k in range(1, TOP_K):
            acc = acc + gates[:, k:k + 1] * y4_ref[k, pl.ds(c, tm, stride=SUBLANES), :]
        cols.append(acc)
    ffn = jnp.concatenate(cols, axis=-1)

    @pl.when(i < n_prompt_tiles)
    def _():
        g2 = modp_ref[...][5:6, :]
        outp_ref[...] = _rmsnorm(x1p_ref[...] + g2 * ffn, fn_ref[...])

    @pl.when(i >= n_prompt_tiles)
    def _():
        g2 = jnp.concatenate([g2s_ref[...]] * (tm // g2s_ref.shape[0]), axis=0)
        outs_ref[...] = _rmsnorm(x1s_ref[...] + g2 * ffn, fn_ref[...])


def _combine(gates, y4, x1_p, x1_s, mod_p, mod, n_batch_s, final_norm, tokens_per_seq):
    n_p, d = x1_p.shape
    n_s = x1_s.shape[0]
    tm = COMBINE_TILE
    assert tm % n_batch_s == 0 and n_batch_s % SUBLANES == 0
    npt, nst = n_p // tm, n_s // tm
    tiles_per_seq = tokens_per_seq // tm
    pmap = lambda i: (jnp.minimum(i, npt - 1), 0)
    smap = lambda i: (jnp.maximum(i - npt, 0), 0)
    return pl.pallas_call(
        functools.partial(_combine_kernel, n_prompt_tiles=npt),
        out_shape=(jax.ShapeDtypeStruct((n_p, d), F32), jax.ShapeDtypeStruct((n_s, d), F32)),
        grid=(npt + nst,),
        in_specs=[
            pl.BlockSpec((SUBLANES, tm), lambda i: (0, i)),
            pl.BlockSpec((TOP_K, tm * SUBLANES, LANES), lambda i: (0, i, 0)),
            pl.BlockSpec((tm, d), pmap),
            pl.BlockSpec((tm, d), smap),
            pl.BlockSpec((None, 6, d), lambda i: (jnp.minimum(i, npt - 1) // tiles_per_seq, 0, 0)),
            pl.BlockSpec((n_batch_s, d), lambda i: (0, 5)),
            pl.BlockSpec((1, d), lambda i: (0, 0)),
        ],
        out_specs=(pl.BlockSpec((tm, d), pmap), pl.BlockSpec((tm, d), smap)),
        compiler_params=pltpu.CompilerParams(
            dimension_semantics=("arbitrary",), vmem_limit_bytes=VMEM_LIMIT),
        name="combine",
    )(gates, y4, x1_p, x1_s, mod_p, mod, final_norm)


def kernel(x_prompt, x_sample, state_pool, state_conv, c_prompt, c_sample, norm1, norm2, w_ada, b_ada,
           w_in, w_pool, pool_scale, w_conv, w_out, w_router, b_router, w_gate, b_gate, w_up, b_up,
           w_down, b_down, final_norm):
    depth = norm1.shape[0]
    assert depth == 1, "single-layer step"
    bp, tp, d = x_prompt.shape
    bs, ts, _ = x_sample.shape
    dp = state_pool.shape[-1]
    n_hist = state_pool.shape[2]
    n_chist = state_conv.shape[2]
    n_p, n_s = bp * tp, bs * ts
    assert d == SUBLANES * LANES, "token-tile layout assumes one vreg tile per token row"
    assert tp % TOKEN_TILE == 0 and n_s == TOKEN_TILE, "the sample group fills exactly one token tile"
    assert tp % COMBINE_TILE == 0 and n_s % COMBINE_TILE == 0
    assert (n_p + n_s) % ROUTE_TILE == 0 and ((n_p + n_s) * TOP_K) % EXPERT_TILE == 0

    l = 0
    n1 = norm1[l].reshape(1, d)
    n2 = norm2[l].reshape(1, d)
    pscale = pool_scale[l].reshape(1, dp)
    w_r = w_router[l].T.astype(BF16)
    b_r = b_router[l].reshape(N_EXPERTS, 1)

    mod = _adaln(jnp.concatenate([c_sample, c_prompt], axis=0), w_ada[l], b_ada[l])
    mod_p = mod[bs:].reshape(bp, 6, d)

    xs_tm = jnp.transpose(x_sample, (1, 0, 2)).reshape(n_s, d)
    ps_tm = jnp.transpose(state_pool[l], (1, 0, 2))
    cs_tm = jnp.transpose(state_conv[l], (1, 0, 2))
    x1_s, h2_s, lg_s, newp_tm, newc_tm, w_in_b, w_pool_b, w_out_b = _mixer_sample(
        xs_tm, mod, ps_tm, cs_tm, n1, n2, w_in[l], w_pool[l], pscale, w_conv[l], w_out[l], w_r, b_r, ts)

    x1_p, h2, lg_p, u_tail, v_tail = _mixer_prompt(
        x_prompt, mod_p, n1, n2, w_in_b, w_pool_b, pscale, w_conv[l], w_out_b, w_r, b_r, h2_s)

    dest, gates, counts_f = _route(lg_p, lg_s)
    counts = counts_f[:, 0].astype(I32)
    n_rows = (n_p + n_s) * TOP_K
    gid, tid, nxt, offs, nact, n_steps = _group_metadata(counts, n_rows, EXPERT_TILE)

    dest = dest.reshape(-1)
    inv = _invert(dest)
    y4 = _experts(gid, tid, nxt, offs, nact, n_steps, inv, h2,
                  w_gate[l], b_gate[l], w_up[l], b_up[l], w_down[l], b_down[l])

    y_p, y_s = _combine(gates, y4, x1_p, x1_s, mod_p, mod, bs, final_norm.reshape(1, d), tp)

    y_prompt = y_p.reshape(bp, tp, d)
    y_sample = jnp.transpose(y_s.reshape(ts, bs, d), (1, 0, 2))
    new_pool_prompt = u_tail[:, POOL_HALO - n_hist:, :][None]
    new_conv_prompt = v_tail[:, CONV_HALO - n_chist:, :][None]
    new_pool_sample = jnp.transpose(newp_tm, (1, 0, 2))[None]
    new_conv_sample = jnp.transpose(newc_tm, (1, 0, 2))[None]
    return (y_prompt, y_sample, new_pool_prompt, new_conv_prompt, new_pool_sample, new_conv_sample)
```

```python
import functools

import jax
import jax.numpy as jnp
from jax import lax
from jax.experimental import pallas as pl
from jax.experimental.pallas import tpu as pltpu
from jax.experimental.pallas import tpu_sc as plsc

F32 = jnp.float32
BF16 = jnp.bfloat16
I32 = jnp.int32

POOL_WINDOWS = (2, 4, 8, 16)
POOL_HALO = 16
CONV_TAPS = 3
CONV_HALO = 8
N_EXPERTS = 32
TOP_K = 4
SWIGLU_LIMIT = 7.0
SWIGLU_ALPHA = 1.702
EPS = 1e-5
PAST_LEN = 16384

LANES = 128
SUBLANES = 8

TOKEN_TILE = 512
ROUTE_TILE = 512
COMBINE_TILE = 512
EXPERT_TILE = 1024
EXPERT_PART = 512
EXPERT_SUBTILE = 128
VMEM_LIMIT = 56 * 1024 * 1024


def _rmsnorm(x, g):
    ms = jnp.mean(x * x, axis=-1, keepdims=True)
    return x * lax.rsqrt(ms + EPS) * g


def _dot(a, b):
    return jnp.dot(a, b, preferred_element_type=F32)


def _store_token_tiles(ref, val):
    rows = val.shape[0]
    for c in range(SUBLANES):
        ref[pl.ds(c, rows, stride=SUBLANES), :] = val[:, c * LANES:(c + 1) * LANES]


def _load_token_tiles(ref):
    rows = ref.shape[0] // SUBLANES
    return jnp.concatenate([ref[pl.ds(c, rows, stride=SUBLANES), :] for c in range(SUBLANES)], axis=-1)


ISSUE_GROUP = 4


def _adaln_kernel(c_ref, w_ref, b_ref, o_ref):
    c = c_ref[...]
    s = c * jax.nn.sigmoid(c)
    o_ref[...] = _dot(s.astype(BF16), w_ref[...].astype(BF16)) + b_ref[...]


def _adaln(c, w_ada, b_ada):
    rows, d = c.shape
    n = w_ada.shape[1]
    tn = 1024
    return pl.pallas_call(
        _adaln_kernel,
        out_shape=jax.ShapeDtypeStruct((rows, n), F32),
        grid=(n // tn,),
        in_specs=[
            pl.BlockSpec((rows, d), lambda j: (0, 0)),
            pl.BlockSpec((d, tn), lambda j: (0, j)),
            pl.BlockSpec((1, tn), lambda j: (0, j)),
        ],
        out_specs=pl.BlockSpec((rows, tn), lambda j: (0, j)),
        compiler_params=pltpu.CompilerParams(
            dimension_semantics=("arbitrary",), vmem_limit_bytes=VMEM_LIMIT),
        name="adaln",
    )(c, w_ada, b_ada.reshape(1, n))


def _mix_tail(x, pool_in, conv_out, g1, sc2, sh2, n2, wpool_ref, pscale, wout, wr, br):
    gw = pool_in.shape[1] // len(POOL_WINDOWS)
    mixed = [_dot(pool_in[:, g * gw:(g + 1) * gw].astype(BF16), wpool_ref[g])
             for g in range(len(POOL_WINDOWS))]
    pool_out = jnp.concatenate(mixed, axis=-1) * pscale
    mix_in = jnp.concatenate([pool_out, conv_out], axis=-1).astype(BF16)
    x1 = x + g1 * _dot(mix_in, wout)
    h2 = _rmsnorm(x1, n2) * (1.0 + sc2) + sh2
    logits_t = lax.dot_general(wr, h2.astype(BF16), (((1,), (1,)), ((), ())),
                               preferred_element_type=F32) + br
    return x1, h2, logits_t


def _mixer_prompt_kernel(x_ref, mod_ref, n1_ref, n2_ref, win_ref, wpool_ref, pscale_ref, wconv_ref,
                         wout_ref, wr_ref, br_ref, h2s_ref,
                         x1_ref, h2_ref, lg_ref, upool_ref, vconv_ref, ubuf, vbuf, *, tiles_per_seq):
    i = pl.program_id(0)

    @pl.when(i < pl.num_programs(0) - 1)
    def _():
        _mixer_prompt_tile(x_ref, mod_ref, n1_ref, n2_ref, win_ref, wpool_ref, pscale_ref, wconv_ref,
                           wout_ref, wr_ref, br_ref, x1_ref, h2_ref, lg_ref, upool_ref, vconv_ref,
                           ubuf, vbuf, lax.rem(i, tiles_per_seq), tiles_per_seq)

    @pl.when(i == pl.num_programs(0) - 1)
    def _():
        h2_ref[...] = h2s_ref[...]


def _mixer_prompt_tile(x_ref, mod_ref, n1_ref, n2_ref, win_ref, wpool_ref, pscale_ref, wconv_ref,
                       wout_ref, wr_ref, br_ref, x1_ref, h2_ref, lg_ref, upool_ref, vconv_ref,
                       ubuf, vbuf, t, tiles_per_seq):
    tt = x_ref.shape[0]
    dp = ubuf.shape[1]
    gw = dp // len(POOL_WINDOWS)

    @pl.when(t == 0)
    def _():
        ubuf[0:POOL_HALO, :] = jnp.zeros((POOL_HALO, dp), F32)
        vbuf[0:CONV_HALO, :] = jnp.zeros((CONV_HALO, dp), F32)

    x = x_ref[...]
    mod = mod_ref[...]
    sh1, sc1, g1, sh2, sc2, _ = [mod[i:i + 1, :] for i in range(6)]
    h = _rmsnorm(x, n1_ref[...]) * (1.0 + sc1) + sh1
    z = _dot(h.astype(BF16), win_ref[...])
    u, gate_b, gate_c, val = [z[:, i * dp:(i + 1) * dp] for i in range(4)]

    ubuf[POOL_HALO:POOL_HALO + tt, :] = u
    pos = lax.broadcasted_iota(I32, (tt, gw), 0) + t * tt
    pooled = []
    for g, w in enumerate(POOL_WINDOWS):
        cols = slice(g * gw, (g + 1) * gw)
        acc = u[:, cols]
        for j in range(1, w):
            acc = acc + ubuf[POOL_HALO - j:POOL_HALO - j + tt, cols]
        cnt = jnp.minimum(pos + 1, w).astype(F32)
        pooled.append(acc / cnt - u[:, cols])
    pool_in = jnp.concatenate(pooled, axis=-1)

    v = gate_c * val
    vbuf[CONV_HALO:CONV_HALO + tt, :] = v
    wc = wconv_ref[...]
    y = (wc[0:1, :] * vbuf[CONV_HALO - 2:CONV_HALO - 2 + tt, :]
         + wc[1:2, :] * vbuf[CONV_HALO - 1:CONV_HALO - 1 + tt, :]
         + wc[2:3, :] * v)
    conv_out = gate_b * y

    x1, h2, logits = _mix_tail(x, pool_in, conv_out, g1, sc2, sh2, n2_ref[...], wpool_ref,
                               pscale_ref[...], wout_ref[...], wr_ref[...], br_ref[...])
    x1_ref[...] = x1
    _store_token_tiles(h2_ref, h2)
    lg_ref[...] = logits

    ubuf[0:POOL_HALO, :] = ubuf[tt:tt + POOL_HALO, :]
    vbuf[0:CONV_HALO, :] = vbuf[tt:tt + CONV_HALO, :]

    @pl.when(t == tiles_per_seq - 1)
    def _():
        upool_ref[...] = ubuf[0:POOL_HALO, :]
        vconv_ref[...] = vbuf[0:CONV_HALO, :]


def _mixer_prompt(x, mod_p, n1, n2, w_in, w_pool, pscale, w_conv, w_out, w_r, b_r, h2_s):
    b, t, d = x.shape
    dp = w_pool.shape[0] * w_pool.shape[1]
    tt = min(TOKEN_TILE, t)
    nt = t // tt
    n_real = b * nt
    assert h2_s.shape[0] == tt * SUBLANES
    seq = lambda i: jnp.minimum(i, n_real - 1) // nt
    tile = lambda i: jnp.minimum(i, n_real - 1)
    const2 = lambda i: (0, 0)
    const3 = lambda i: (0, 0, 0)
    return pl.pallas_call(
        functools.partial(_mixer_prompt_kernel, tiles_per_seq=nt),
        out_shape=(
            jax.ShapeDtypeStruct((b * t, d), F32),
            jax.ShapeDtypeStruct(((n_real + 1) * tt * SUBLANES, LANES), F32),
            jax.ShapeDtypeStruct((N_EXPERTS, b * t), F32),
            jax.ShapeDtypeStruct((b, POOL_HALO, dp), F32),
            jax.ShapeDtypeStruct((b, CONV_HALO, dp), F32),
        ),
        grid=(n_real + 1,),
        in_specs=[
            pl.BlockSpec((None, tt, d), lambda i: (seq(i), tile(i) % nt, 0)),
            pl.BlockSpec((None, 6, d), lambda i: (seq(i), 0, 0)),
            pl.BlockSpec((1, d), const2),
            pl.BlockSpec((1, d), const2),
            pl.BlockSpec(w_in.shape, const2),
            pl.BlockSpec(w_pool.shape, const3),
            pl.BlockSpec((1, dp), const2),
            pl.BlockSpec(w_conv.shape, const2),
            pl.BlockSpec(w_out.shape, const2),
            pl.BlockSpec(w_r.shape, const2),
            pl.BlockSpec((N_EXPERTS, 1), const2),
            pl.BlockSpec(h2_s.shape, const2),
        ],
        out_specs=(
            pl.BlockSpec((tt, d), lambda i: (tile(i), 0)),
            pl.BlockSpec((tt * SUBLANES, LANES), lambda i: (i, 0)),
            pl.BlockSpec((N_EXPERTS, tt), lambda i: (0, tile(i))),
            pl.BlockSpec((None, POOL_HALO, dp), lambda i: (seq(i), 0, 0)),
            pl.BlockSpec((None, CONV_HALO, dp), lambda i: (seq(i), 0, 0)),
        ),
        scratch_shapes=[
            pltpu.VMEM((POOL_HALO + tt, dp), F32),
            pltpu.VMEM((CONV_HALO + tt, dp), F32),
        ],
        compiler_params=pltpu.CompilerParams(
            dimension_semantics=("arbitrary",), vmem_limit_bytes=VMEM_LIMIT),
        name="mixer_prompt",
    )(x, mod_p, n1, n2, w_in, w_pool, pscale, w_conv, w_out, w_r, b_r, h2_s)


def _mixer_sample_kernel(x_ref, mod_ref, pstate_ref, cstate_ref, n1_ref, n2_ref, win_f32, wpool_f32,
                         pscale_ref, wconv_ref, wout_f32, wr_ref, br_ref,
                         x1_ref, h2_ref, lg_ref, newp_ref, newc_ref, win_ref, wpool_ref, wout_ref,
                         *, steps):
    win_ref[...] = win_f32[...].astype(BF16)
    wpool_ref[...] = wpool_f32[...].astype(BF16)
    wout_ref[...] = wout_f32[...].astype(BF16)
    nb = pstate_ref.shape[1]
    d = x_ref.shape[1]
    dp = pstate_ref.shape[2]
    gw = dp // len(POOL_WINDOWS)
    n_hist = pstate_ref.shape[0]
    n_chist = cstate_ref.shape[0]

    x = x_ref[...]
    mod = mod_ref[0:nb, :]
    rep = lambda a: jnp.concatenate([a] * steps, axis=0)
    sh1, sc1, g1, sh2, sc2, _ = [rep(mod[:, i * d:(i + 1) * d]) for i in range(6)]
    h = _rmsnorm(x, n1_ref[...]) * (1.0 + sc1) + sh1
    z = _dot(h.astype(BF16), win_ref[...])
    u, gate_b, gate_c, val = [z[:, i * dp:(i + 1) * dp] for i in range(4)]

    ext = [pstate_ref[i] for i in range(n_hist)] + [u[s * nb:(s + 1) * nb, :] for s in range(steps)]
    pooled_steps = []
    for s in range(steps):
        groups = []
        for g, w in enumerate(POOL_WINDOWS):
            cols = slice(g * gw, (g + 1) * gw)
            acc = ext[n_hist + s][:, cols]
            for j in range(1, w):
                acc = acc + ext[n_hist + s - j][:, cols]
            cnt = float(min(PAST_LEN + s + 1, w))
            groups.append(acc / cnt - ext[n_hist + s][:, cols])
        pooled_steps.append(jnp.concatenate(groups, axis=-1))
    pool_in = jnp.concatenate(pooled_steps, axis=0)

    v = gate_c * val
    vext = [cstate_ref[i] for i in range(n_chist)] + [v[s * nb:(s + 1) * nb, :] for s in range(steps)]
    wc = wconv_ref[...]
    y = jnp.concatenate(
        [wc[0:1, :] * vext[s] + wc[1:2, :] * vext[s + 1] + wc[2:3, :] * vext[s + 2] for s in range(steps)],
        axis=0)
    conv_out = gate_b * y

    x1, h2, logits = _mix_tail(x, pool_in, conv_out, g1, sc2, sh2, n2_ref[...], wpool_ref,
                               pscale_ref[...], wout_ref[...], wr_ref[...], br_ref[...])
    x1_ref[...] = x1
    _store_token_tiles(h2_ref, h2)
    lg_ref[...] = logits
    for i in range(n_hist):
        newp_ref[i] = ext[steps + i]
    for i in range(n_chist):
        newc_ref[i] = vext[steps + i]


def _mixer_sample(x_tm, mod_s, pstate_tm, cstate_tm, n1, n2, w_in, w_pool, pscale, w_conv, w_out, w_r, b_r,
                  steps):
    rows, d = x_tm.shape
    return pl.pallas_call(
        functools.partial(_mixer_sample_kernel, steps=steps),
        out_shape=(
            jax.ShapeDtypeStruct((rows, d), F32),
            jax.ShapeDtypeStruct((rows * SUBLANES, LANES), F32),
            jax.ShapeDtypeStruct((N_EXPERTS, rows), F32),
            jax.ShapeDtypeStruct(pstate_tm.shape, F32),
            jax.ShapeDtypeStruct(cstate_tm.shape, F32),
            jax.ShapeDtypeStruct(w_in.shape, BF16),
            jax.ShapeDtypeStruct(w_pool.shape, BF16),
            jax.ShapeDtypeStruct(w_out.shape, BF16),
        ),
        compiler_params=pltpu.CompilerParams(vmem_limit_bytes=VMEM_LIMIT),
        name="mixer_sample",
    )(x_tm, mod_s, pstate_tm, cstate_tm, n1, n2, w_in, w_pool, pscale, w_conv, w_out, w_r, b_r)


def _route_kernel(lgp_ref, lgs_ref, dest_ref, gate_ref, cnt_ref, counts, start, before, chosen):
    ne = lgp_ref.shape[0]
    tr = before.shape[0]
    reps = tr // LANES
    n_prompt_chunks = lgp_ref.shape[1] // tr
    n_sample_chunks = lgs_ref.shape[1] // tr
    eidx = lax.broadcasted_iota(I32, (ne, tr), 0)

    def selected(onehots):
        return jnp.where(onehots[0] | onehots[1] | onehots[2] | onehots[3], 1.0, 0.0)

    def select_chunk(ref, first_chunk):
        def body(c, carry):
            chunk = first_chunk + c
            work = ref[:, pl.ds(pl.multiple_of(c * tr, tr), tr)]
            top_v, onehots = [], []
            for k in range(TOP_K):
                m = jnp.max(work, axis=0, keepdims=True)
                idx = jnp.min(jnp.where(work == m, eidx, ne), axis=0, keepdims=True)
                chosen[chunk, k:k + 1, :] = idx
                top_v.append(m)
                onehots.append(eidx == idx)
                work = jnp.where(onehots[k], -jnp.inf, work)
            es = [jnp.exp(v - top_v[0]) for v in top_v]
            denom = es[0] + es[1] + es[2] + es[3]
            cols = pl.ds(pl.multiple_of(chunk * tr, tr), tr)
            for k in range(TOP_K):
                gate_ref[k:k + 1, cols] = es[k] / denom
            chunk_counts = jnp.sum(selected(onehots), axis=1, keepdims=True)
            counts[...] = counts[...] + jnp.broadcast_to(chunk_counts, (ne, LANES))
            return carry
        return body

    counts[...] = jnp.zeros_like(counts)
    gate_ref[...] = jnp.zeros_like(gate_ref)
    lax.fori_loop(0, n_prompt_chunks, select_chunk(lgp_ref, 0), 0)
    lax.fori_loop(0, n_sample_chunks, select_chunk(lgs_ref, n_prompt_chunks), 0)

    total = counts[...]
    hi = jnp.floor(total * (1.0 / 256.0))
    lo = total - hi * 256.0
    r = lax.broadcasted_iota(I32, (ne, ne), 0)
    col = lax.broadcasted_iota(I32, (ne, ne), 1)
    lower = jnp.where(col < r, 1.0, 0.0).astype(BF16)
    start[...] = 256.0 * _dot(lower, hi.astype(BF16)) + _dot(lower, lo.astype(BF16))
    cnt_ref[...] = total
    counts[...] = jnp.zeros_like(counts)

    r = lax.broadcasted_iota(I32, (tr, tr), 0)
    col = lax.broadcasted_iota(I32, (tr, tr), 1)
    before[...] = jnp.where(r < col, 1.0, 0.0).astype(BF16)

    def place_chunk(chunk, carry):
        onehots = [eidx == chosen[chunk, k:k + 1, :] for k in range(TOP_K)]
        mask = selected(onehots)
        base = jnp.concatenate([counts[...] + start[...]] * reps, axis=1)
        rank = _dot(mask.astype(BF16), before[...]) + base
        counts[...] = counts[...] + jnp.broadcast_to(jnp.sum(mask, axis=1, keepdims=True), (ne, LANES))
        for k in range(TOP_K):
            d = jnp.sum(jnp.where(onehots[k], rank, 0.0), axis=0, keepdims=True).astype(I32)
            for j in range(reps):
                dest_ref[chunk * reps + j, k:k + 1, :] = d[:, j * LANES:(j + 1) * LANES]
        return carry

    lax.fori_loop(0, n_prompt_chunks + n_sample_chunks, place_chunk, 0)


def _route(lgt_p, lgt_s):
    ne, n_p = lgt_p.shape
    n_s = lgt_s.shape[1]
    tr = ROUTE_TILE
    return pl.pallas_call(
        _route_kernel,
        out_shape=(
            jax.ShapeDtypeStruct(((n_p + n_s) // LANES, TOP_K, LANES), I32),
            jax.ShapeDtypeStruct((SUBLANES, n_p + n_s), F32),
            jax.ShapeDtypeStruct((ne, LANES), F32),
        ),
        scratch_shapes=[pltpu.VMEM((ne, LANES), F32), pltpu.VMEM((ne, LANES), F32),
                        pltpu.VMEM((tr, tr), BF16), pltpu.VMEM(((n_p + n_s) // tr, TOP_K, tr), I32)],
        compiler_params=pltpu.CompilerParams(vmem_limit_bytes=VMEM_LIMIT),
        name="route",
    )(lgt_p, lgt_s)


def _group_metadata(counts, n_rows, tile):
    n_tiles = n_rows // tile
    n_steps = n_tiles + N_EXPERTS - 1
    ends = jnp.cumsum(counts)
    offs = jnp.concatenate([jnp.zeros((1,), I32), ends]).astype(I32)
    first_tile = offs[:-1] // tile
    last_tile = (ends - 1) // tile
    tiles_e = jnp.where(counts > 0, last_tile - first_tile + 1, 0)
    step_end = jnp.cumsum(tiles_e)
    step_start = step_end - tiles_e
    n_active = step_end[-1]
    s = jnp.minimum(jnp.arange(n_steps, dtype=I32), n_active - 1)
    owner = ((s[:, None] >= step_start[None, :]) & (s[:, None] < step_end[None, :])).astype(I32)
    gid = jnp.sum(owner * jnp.arange(N_EXPERTS, dtype=I32)[None, :], axis=1)
    tid = jnp.sum(owner * (first_tile - step_start)[None, :], axis=1) + s
    ids = jnp.arange(N_EXPERTS, dtype=I32)
    later = (ids[None, :] > ids[:, None]) & (counts[None, :] > 0)
    next_e = jnp.min(jnp.where(later, ids[None, :], N_EXPERTS), axis=1)
    next_e = jnp.where(next_e == N_EXPERTS, -1, next_e)
    nxt = jnp.sum(owner * next_e[None, :], axis=1)
    return gid, tid, nxt, offs, n_active.reshape(1).astype(I32), n_steps


SC_LANES = 16
SC_INDEX_BATCH = 128


def _invert(dest):
    r = dest.shape[0]
    mesh = plsc.VectorSubcoreMesh(core_axis_name="core", subcore_axis_name="subcore")
    per = r // mesh.num_subcores
    assert per * mesh.num_subcores == r and per % SC_INDEX_BATCH == 0

    @functools.partial(
        pl.kernel, mesh=mesh, out_type=jax.ShapeDtypeStruct((r,), I32),
        scratch_types=[pltpu.VMEM_SHARED((r,), I32), pltpu.VMEM((per,), I32), pltpu.VMEM((per,), I32)],
        compiler_params=pltpu.CompilerParams(needs_layout_passes=False),
        name="invert",
    )
    def invert(dest_hbm, inv_hbm, table, idx, ids):
        @pl.when(lax.axis_index("core") == 0)
        def _():
            base = lax.axis_index("subcore") * per
            pltpu.sync_copy(dest_hbm.at[pl.ds(base, per)], idx)
            lane = lax.iota(I32, SC_LANES)

            @pl.loop(0, per // SC_LANES)
            def _(i):
                p = base + i * SC_LANES + lane
                tok = lax.shift_right_logical(p, 9) * LANES + (p & (LANES - 1))
                slot = lax.shift_right_logical(p, 7) & (TOP_K - 1)
                ids[pl.ds(i * SC_LANES, SC_LANES)] = tok * TOP_K + slot

            @pl.loop(0, per // SC_INDEX_BATCH)
            def _(j):
                span = pl.ds(j * SC_INDEX_BATCH, SC_INDEX_BATCH)
                pltpu.sync_copy(ids.at[span], table.at[idx.at[span]])

            plsc.subcore_barrier()
            pltpu.sync_copy(table.at[pl.ds(base, per)], inv_hbm.at[pl.ds(base, per)])

    return invert(dest)


def _experts_kernel(gid_ref, tid_ref, nxt_ref, offs_ref, nact_ref,
                    dstp_ref, dst0_ref, src0_ref, src1_ref, src2_ref, bg_ref, bu_ref, bd_ref,
                    h2_hbm, wg_hbm, wu_hbm, wd_hbm, y4_hbm,
                    wg_b, wu_b, wd_b, wg_f, wu_f, wd_f, relay, rows, ybuf, sems, row_sems, y_sems):
    s = pl.program_id(0)
    tile_rows = rows.shape[1]
    tm = tile_rows // SUBLANES
    landing = ((wg_hbm, wg_f, wg_b), (wu_hbm, wu_f, wu_b), (wd_hbm, wd_f, wd_b))

    def fetch(e):
        for j, (hbm, land, _) in enumerate(landing):
            pltpu.make_async_copy(hbm.at[e], land, sems.at[j]).start()

    def row_in(idx_ref, into, r):
        src = h2_hbm.at[pl.ds(pl.multiple_of(idx_ref[r], SUBLANES), SUBLANES)]
        dst = rows.at[into, pl.ds(pl.multiple_of(r * SUBLANES, SUBLANES), SUBLANES)]
        return pltpu.make_async_copy(src, dst, row_sems.at[into])

    def row_out(idx_ref, frm, r):
        src = ybuf.at[frm, pl.ds(pl.multiple_of(r * SUBLANES, SUBLANES), SUBLANES)]
        dst = y4_hbm.at[pl.ds(pl.multiple_of(idx_ref[r], SUBLANES), SUBLANES)]
        return pltpu.make_async_copy(src, dst, y_sems.at[frm])

    def in_line(make, idx_ref, buf, first=0, count=None):
        for r in range(first, tm if count is None else first + count):
            make(idx_ref, buf, r).start(priority=r % 2)

    def in_loop(make, idx_ref, buf):
        def group(g, carry):
            for j in range(ISSUE_GROUP):
                make(idx_ref, buf, g * ISSUE_GROUP + j).start(priority=j % 2)
            return carry
        lax.fori_loop(0, tm // ISSUE_GROUP, group, 0)

    def wait_rows(buf):
        pltpu.make_async_copy(h2_hbm.at[pl.ds(0, tile_rows)], rows.at[buf], row_sems.at[buf]).wait()

    def wait_y(buf):
        pltpu.make_async_copy(ybuf.at[buf], y4_hbm.at[pl.ds(0, tile_rows)], y_sems.at[buf]).wait()

    @pl.when(s < nact_ref[0])
    def _():
        e = gid_ref[s]
        m = tid_ref[s]
        cur = lax.rem(m, 3)
        before = lax.rem(m + 2, 3)
        last = nact_ref[0] - 1
        new_tile = (s == 0) | (tid_ref[jnp.maximum(s - 1, 0)] != m)

        @pl.when(s == 0)
        def _():
            fetch(e)
            in_loop(row_in, src0_ref, 0)
            in_loop(row_in, src1_ref, 1)
            ybuf[2] = jnp.zeros(ybuf.shape[1:], F32)

        @pl.when((s == 0) | (gid_ref[jnp.maximum(s - 1, 0)] != e))
        def _():
            for j, (hbm, land, half) in enumerate(landing):
                pltpu.make_async_copy(hbm.at[e], land, sems.at[j]).wait()
                half[...] = land[...].astype(BF16)

            @pl.when(nxt_ref[s] >= 0)
            def _():
                fetch(nxt_ref[s])

        @pl.when(new_tile)
        def _():
            wait_rows(cur)

            @pl.when(m >= 2)
            def _():
                wait_y(cur)

        def ffn(load_x, after_gate=lambda: None):
            g = _dot(load_x(), wg_b[...]) + bg_ref[pl.ds(e, 1), :]
            after_gate()
            u = _dot(load_x(), wu_b[...]) + bu_ref[pl.ds(e, 1), :]
            g = jnp.minimum(g, SWIGLU_LIMIT)
            u = jnp.clip(u, -SWIGLU_LIMIT, SWIGLU_LIMIT)
            glu = g * jax.nn.sigmoid(SWIGLU_ALPHA * g)
            return _dot(((u + 1.0) * glu).astype(BF16), wd_b[...]) + bd_ref[pl.ds(e, 1), :]

        lo = offs_ref[e]
        hi = offs_ref[e + 1]
        whole_tile = (lo <= m * tm) & (hi >= (m + 1) * tm)

        def ffn_and_copies(load_x, first, count):
            pinned = count // 2
            y = ffn(load_x, lambda: in_line(row_in, src2_ref, before, first, pinned))
            in_line(row_in, src2_ref, before, first + pinned, count - pinned)
            in_line(row_out, dstp_ref, before, first, count)
            return y

        @pl.when(whole_tile)
        def _():
            y = ffn_and_copies(lambda: _load_token_tiles(rows.at[cur]).astype(BF16), 0, tm)
            _store_token_tiles(ybuf.at[cur], y)

        @pl.when(jnp.logical_not(whole_tile))
        def _():
            sub = relay.shape[0] // SUBLANES
            part = min(EXPERT_PART, tm)
            parts_whole = []
            for p in range(tm // part if part < tm else 0):
                p_first = m * tm + p * part
                parts_whole.append((lo <= p_first) & (hi >= p_first + part))
                p_span = pl.ds(p * part * SUBLANES, part * SUBLANES)

                @pl.when(parts_whole[p])
                def _(p=p, p_span=p_span):
                    y = ffn_and_copies(lambda: _load_token_tiles(rows.at[cur, p_span]).astype(BF16),
                                       p * part, part)
                    _store_token_tiles(ybuf.at[cur, p_span], y)

            for j in range(tm // sub):
                first = m * tm + j * sub
                span = pl.ds(j * sub * SUBLANES, sub * SUBLANES)
                load_x = lambda span=span: _load_token_tiles(rows.at[cur, span]).astype(BF16)

                def put(y, first=first, span=span):
                    _store_token_tiles(relay, y)
                    row = first + lax.shift_right_logical(lax.broadcasted_iota(I32, relay.shape, 0), 3)
                    pltpu.store(ybuf.at[cur, span], relay[...], mask=(row >= lo) & (row < hi))

                evaluate = (lo < first + sub) & (hi > first)
                if parts_whole:
                    evaluate = evaluate & jnp.logical_not(parts_whole[j * sub // part])
                owns_end = hi >= first + sub

                @pl.when(evaluate & owns_end)
                def _():
                    put(ffn_and_copies(load_x, j * sub, sub))

                @pl.when(evaluate & jnp.logical_not(owns_end))
                def _():
                    put(ffn(load_x))

        @pl.when(s == last)
        def _():
            in_loop(row_out, dst0_ref, cur)
            wait_rows(lax.rem(m + 1, 3))
            wait_rows(before)
            for buf in range(3):
                wait_y(buf)


def _experts(gid, tid, nxt, offs, nact, n_steps, inv, h2, wg, bg, wu, bu, wd, bd):
    ne, d, f = wg.shape
    tm = EXPERT_TILE
    n_tokens = h2.shape[0] // SUBLANES
    n_tiles = inv.shape[0] // tm
    assert n_tiles >= 2
    slot_rows = (n_tokens + tm // TOP_K) * SUBLANES
    inv_ext = jnp.concatenate([n_tokens * TOP_K + jnp.arange(tm, dtype=I32), inv])
    tok = lax.shift_right_logical(inv_ext, TOP_K.bit_length() - 1)
    src_row = tok * SUBLANES
    dst_row = (inv_ext & (TOP_K - 1)) * slot_rows + src_row
    whole = lambda s, gid, tid, nxt, offs, nact: (0, 0)

    def order_of(k):
        return pl.BlockSpec(
            (tm,), lambda s, gid, tid, nxt, offs, nact: (jnp.minimum(tid[s] + k, n_tiles - 1) + 1,),
            memory_space=pltpu.SMEM)

    grid_spec = pltpu.PrefetchScalarGridSpec(
        num_scalar_prefetch=5,
        grid=(n_steps,),
        in_specs=[
            order_of(-1), order_of(0), order_of(0), order_of(1), order_of(2),
            pl.BlockSpec((ne, f), whole),
            pl.BlockSpec((ne, f), whole),
            pl.BlockSpec((ne, d), whole),
            pl.BlockSpec(memory_space=pl.ANY),
            pl.BlockSpec(memory_space=pl.ANY),
            pl.BlockSpec(memory_space=pl.ANY),
            pl.BlockSpec(memory_space=pl.ANY),
        ],
        out_specs=pl.BlockSpec(memory_space=pl.ANY),
        scratch_shapes=[
            pltpu.VMEM((d, f), BF16), pltpu.VMEM((d, f), BF16), pltpu.VMEM((f, d), BF16),
            pltpu.VMEM((d, f), F32), pltpu.VMEM((d, f), F32), pltpu.VMEM((f, d), F32),
            pltpu.VMEM((EXPERT_SUBTILE * SUBLANES, LANES), F32),
            pltpu.VMEM((3, tm * SUBLANES, LANES), F32),
            pltpu.VMEM((3, tm * SUBLANES, LANES), F32),
            pltpu.SemaphoreType.DMA((3,)),
            pltpu.SemaphoreType.DMA((3,)),
            pltpu.SemaphoreType.DMA((3,)),
        ],
    )
    y4 = pl.pallas_call(
        _experts_kernel,
        out_shape=jax.ShapeDtypeStruct((TOP_K * slot_rows, LANES), F32),
        grid_spec=grid_spec,
        compiler_params=pltpu.CompilerParams(
            dimension_semantics=("arbitrary",), vmem_limit_bytes=VMEM_LIMIT),
        name="experts",
    )(gid, tid, nxt, offs, nact, dst_row, dst_row, src_row, src_row, src_row, bg, bu, bd, h2, wg, wu, wd)
    return y4.reshape(TOP_K, slot_rows, LANES)


def _combine_kernel(gate_ref, y4_ref, x1p_ref, x1s_ref, modp_ref, g2s_ref, fn_ref, outp_ref, outs_ref,
                    *, n_prompt_tiles):
    i = pl.program_id(0)
    tm = x1p_ref.shape[0]
    gates = gate_ref[...].T
    cols = []
    for c in range(SUBLANES):
        acc = gates[:, 0:1] * y4_ref[0, pl.ds(c, tm, stride=SUBLANES), :]
        for k in range(1, TOP_K):
            acc = acc + gates[:, k:k + 1] * y4_ref[k, pl.ds(c, tm, stride=SUBLANES), :]
        cols.append(acc)
    ffn = jnp.concatenate(cols, axis=-1)

    @pl.when(i < n_prompt_tiles)
    def _():
        g2 = modp_ref[...][5:6, :]
        outp_ref[...] = _rmsnorm(x1p_ref[...] + g2 * ffn, fn_ref[...])

    @pl.when(i >= n_prompt_tiles)
    def _():
        g2 = jnp.concatenate([g2s_ref[...]] * (tm // g2s_ref.shape[0]), axis=0)
        outs_ref[...] = _rmsnorm(x1s_ref[...] + g2 * ffn, fn_ref[...])


def _combine(gates, y4, x1_p, x1_s, mod_p, mod, n_batch_s, final_norm, tokens_per_seq):
    n_p, d = x1_p.shape
    n_s = x1_s.shape[0]
    tm = COMBINE_TILE
    assert tm % n_batch_s == 0 and n_batch_s % SUBLANES == 0
    npt, nst = n_p // tm, n_s // tm
    tiles_per_seq = tokens_per_seq // tm
    pmap = lambda i: (jnp.minimum(i, npt - 1), 0)
    smap = lambda i: (jnp.maximum(i - npt, 0), 0)
    return pl.pallas_call(
        functools.partial(_combine_kernel, n_prompt_tiles=npt),
        out_shape=(jax.ShapeDtypeStruct((n_p, d), F32), jax.ShapeDtypeStruct((n_s, d), F32)),
        grid=(npt + nst,),
        in_specs=[
            pl.BlockSpec((SUBLANES, tm), lambda i: (0, i)),
            pl.BlockSpec((TOP_K, tm * SUBLANES, LANES), lambda i: (0, i, 0)),
            pl.BlockSpec((tm, d), pmap),
            pl.BlockSpec((tm, d), smap),
            pl.BlockSpec((None, 6, d), lambda i: (jnp.minimum(i, npt - 1) // tiles_per_seq, 0, 0)),
            pl.BlockSpec((n_batch_s, d), lambda i: (0, 5)),
            pl.BlockSpec((1, d), lambda i: (0, 0)),
        ],
        out_specs=(pl.BlockSpec((tm, d), pmap), pl.BlockSpec((tm, d), smap)),
        compiler_params=pltpu.CompilerParams(
            dimension_semantics=("arbitrary",), vmem_limit_bytes=VMEM_LIMIT),
        name="combine",
    )(gates, y4, x1_p, x1_s, mod_p, mod, final_norm)


def kernel(x_prompt, x_sample, state_pool, state_conv, c_prompt, c_sample, norm1, norm2, w_ada, b_ada,
           w_in, w_pool, pool_scale, w_conv, w_out, w_router, b_router, w_gate, b_gate, w_up, b_up,
           w_down, b_down, final_norm):
    depth = norm1.shape[0]
    assert depth == 1, "single-layer step"
    bp, tp, d = x_prompt.shape
    bs, ts, _ = x_sample.shape
    dp = state_pool.shape[-1]
    n_hist = state_pool.shape[2]
    n_chist = state_conv.shape[2]
    n_p, n_s = bp * tp, bs * ts
    assert d == SUBLANES * LANES, "token-tile layout assumes one vreg tile per token row"
    assert tp % TOKEN_TILE == 0 and n_s == TOKEN_TILE, "the sample group fills exactly one token tile"
    assert tp % COMBINE_TILE == 0 and n_s % COMBINE_TILE == 0
    assert (n_p + n_s) % ROUTE_TILE == 0 and ((n_p + n_s) * TOP_K) % EXPERT_TILE == 0

    l = 0
    n1 = norm1[l].reshape(1, d)
    n2 = norm2[l].reshape(1, d)
    pscale = pool_scale[l].reshape(1, dp)
    w_r = w_router[l].T.astype(BF16)
    b_r = b_router[l].reshape(N_EXPERTS, 1)

    mod = _adaln(jnp.concatenate([c_sample, c_prompt], axis=0), w_ada[l], b_ada[l])
    mod_p = mod[bs:].reshape(bp, 6, d)

    xs_tm = jnp.transpose(x_sample, (1, 0, 2)).reshape(n_s, d)
    ps_tm = jnp.transpose(state_pool[l], (1, 0, 2))
    cs_tm = jnp.transpose(state_conv[l], (1, 0, 2))
    x1_s, h2_s, lg_s, newp_tm, newc_tm, w_in_b, w_pool_b, w_out_b = _mixer_sample(
        xs_tm, mod, ps_tm, cs_tm, n1, n2, w_in[l], w_pool[l], pscale, w_conv[l], w_out[l], w_r, b_r, ts)

    x1_p, h2, lg_p, u_tail, v_tail = _mixer_prompt(
        x_prompt, mod_p, n1, n2, w_in_b, w_pool_b, pscale, w_conv[l], w_out_b, w_r, b_r, h2_s)

    dest, gates, counts_f = _route(lg_p, lg_s)
    counts = counts_f[:, 0].astype(I32)
    n_rows = (n_p + n_s) * TOP_K
    gid, tid, nxt, offs, nact, n_steps = _group_metadata(counts, n_rows, EXPERT_TILE)

    dest = dest.reshape(-1)
    inv = _invert(dest)
    y4 = _experts(gid, tid, nxt, offs, nact, n_steps, inv, h2,
                  w_gate[l], b_gate[l], w_up[l], b_up[l], w_down[l], b_down[l])

    y_p, y_s = _combine(gates, y4, x1_p, x1_s, mod_p, mod, bs, final_norm.reshape(1, d), tp)

    y_prompt = y_p.reshape(bp, tp, d)
    y_sample = jnp.transpose(y_s.reshape(ts, bs, d), (1, 0, 2))
    new_pool_prompt = u_tail[:, POOL_HALO - n_hist:, :][None]
    new_conv_prompt = v_tail[:, CONV_HALO - n_chist:, :][None]
    new_pool_sample = jnp.transpose(newp_tm, (1, 0, 2))[None]
    new_conv_sample = jnp.transpose(newc_tm, (1, 0, 2))[None]
    return (y_prompt, y_sample, new_pool_prompt, new_conv_prompt, new_pool_sample, new_conv_sample)
```

```python
import functools

import jax
import jax.numpy as jnp
from jax import lax
from jax.experimental import pallas as pl
from jax.experimental.pallas import tpu as pltpu
from jax.experimental.pallas import tpu_sc as plsc

F32 = jnp.float32
BF16 = jnp.bfloat16
I32 = jnp.int32

POOL_WINDOWS = (2, 4, 8, 16)
POOL_HALO = 16
CONV_TAPS = 3
CONV_HALO = 8
N_EXPERTS = 32
TOP_K = 4
SWIGLU_LIMIT = 7.0
SWIGLU_ALPHA = 1.702
EPS = 1e-5
PAST_LEN = 16384

LANES = 128
SUBLANES = 8

TOKEN_TILE = 512
ROUTE_TILE = 512
COMBINE_TILE = 512
EXPERT_TILE = 512
EXPERT_SUBTILE = 128
VMEM_LIMIT = 56 * 1024 * 1024


def _rmsnorm(x, g):
    ms = jnp.mean(x * x, axis=-1, keepdims=True)
    return x * lax.rsqrt(ms + EPS) * g


def _dot(a, b):
    return jnp.dot(a, b, preferred_element_type=F32)


def _store_token_tiles(ref, val):
    rows = val.shape[0]
    for c in range(SUBLANES):
        ref[pl.ds(c, rows, stride=SUBLANES), :] = val[:, c * LANES:(c + 1) * LANES]


def _load_token_tiles(ref):
    rows = ref.shape[0] // SUBLANES
    return jnp.concatenate([ref[pl.ds(c, rows, stride=SUBLANES), :] for c in range(SUBLANES)], axis=-1)


ISSUE_GROUP = 4


def _adaln_kernel(c_ref, w_ref, b_ref, o_ref):
    c = c_ref[...]
    s = c * jax.nn.sigmoid(c)
    o_ref[...] = _dot(s.astype(BF16), w_ref[...].astype(BF16)) + b_ref[...]


def _adaln(c, w_ada, b_ada):
    rows, d = c.shape
    n = w_ada.shape[1]
    tn = 1024
    return pl.pallas_call(
        _adaln_kernel,
        out_shape=jax.ShapeDtypeStruct((rows, n), F32),
        grid=(n // tn,),
        in_specs=[
            pl.BlockSpec((rows, d), lambda j: (0, 0)),
            pl.BlockSpec((d, tn), lambda j: (0, j)),
            pl.BlockSpec((1, tn), lambda j: (0, j)),
        ],
        out_specs=pl.BlockSpec((rows, tn), lambda j: (0, j)),
        compiler_params=pltpu.CompilerParams(
            dimension_semantics=("arbitrary",), vmem_limit_bytes=VMEM_LIMIT),
        name="adaln",
    )(c, w_ada, b_ada.reshape(1, n))


def _mix_tail(x, pool_in, conv_out, g1, sc2, sh2, n2, wpool_ref, pscale, wout, wr, br):
    gw = pool_in.shape[1] // len(POOL_WINDOWS)
    mixed = [_dot(pool_in[:, g * gw:(g + 1) * gw].astype(BF16), wpool_ref[g])
             for g in range(len(POOL_WINDOWS))]
    pool_out = jnp.concatenate(mixed, axis=-1) * pscale
    mix_in = jnp.concatenate([pool_out, conv_out], axis=-1).astype(BF16)
    x1 = x + g1 * _dot(mix_in, wout)
    h2 = _rmsnorm(x1, n2) * (1.0 + sc2) + sh2
    logits_t = lax.dot_general(wr, h2.astype(BF16), (((1,), (1,)), ((), ())),
                               preferred_element_type=F32) + br
    return x1, h2, logits_t


def _mixer_prompt_kernel(x_ref, mod_ref, n1_ref, n2_ref, win_ref, wpool_ref, pscale_ref, wconv_ref,
                         wout_ref, wr_ref, br_ref, h2s_ref,
                         x1_ref, h2_ref, lg_ref, upool_ref, vconv_ref, ubuf, vbuf, *, tiles_per_seq):
    i = pl.program_id(0)

    @pl.when(i < pl.num_programs(0) - 1)
    def _():
        _mixer_prompt_tile(x_ref, mod_ref, n1_ref, n2_ref, win_ref, wpool_ref, pscale_ref, wconv_ref,
                           wout_ref, wr_ref, br_ref, x1_ref, h2_ref, lg_ref, upool_ref, vconv_ref,
                           ubuf, vbuf, lax.rem(i, tiles_per_seq), tiles_per_seq)

    @pl.when(i == pl.num_programs(0) - 1)
    def _():
        h2_ref[...] = h2s_ref[...]


def _mixer_prompt_tile(x_ref, mod_ref, n1_ref, n2_ref, win_ref, wpool_ref, pscale_ref, wconv_ref,
                       wout_ref, wr_ref, br_ref, x1_ref, h2_ref, lg_ref, upool_ref, vconv_ref,
                       ubuf, vbuf, t, tiles_per_seq):
    tt = x_ref.shape[0]
    dp = ubuf.shape[1]
    gw = dp // len(POOL_WINDOWS)

    @pl.when(t == 0)
    def _():
        ubuf[0:POOL_HALO, :] = jnp.zeros((POOL_HALO, dp), F32)
        vbuf[0:CONV_HALO, :] = jnp.zeros((CONV_HALO, dp), F32)

    x = x_ref[...]
    mod = mod_ref[...]
    sh1, sc1, g1, sh2, sc2, _ = [mod[i:i + 1, :] for i in range(6)]
    h = _rmsnorm(x, n1_ref[...]) * (1.0 + sc1) + sh1
    z = _dot(h.astype(BF16), win_ref[...])
    u, gate_b, gate_c, val = [z[:, i * dp:(i + 1) * dp] for i in range(4)]

    ubuf[POOL_HALO:POOL_HALO + tt, :] = u
    pos = lax.broadcasted_iota(I32, (tt, gw), 0) + t * tt
    pooled = []
    for g, w in enumerate(POOL_WINDOWS):
        cols = slice(g * gw, (g + 1) * gw)
        acc = u[:, cols]
        for j in range(1, w):
            acc = acc + ubuf[POOL_HALO - j:POOL_HALO - j + tt, cols]
        cnt = jnp.minimum(pos + 1, w).astype(F32)
        pooled.append(acc / cnt - u[:, cols])
    pool_in = jnp.concatenate(pooled, axis=-1)

    v = gate_c * val
    vbuf[CONV_HALO:CONV_HALO + tt, :] = v
    wc = wconv_ref[...]
    y = (wc[0:1, :] * vbuf[CONV_HALO - 2:CONV_HALO - 2 + tt, :]
         + wc[1:2, :] * vbuf[CONV_HALO - 1:CONV_HALO - 1 + tt, :]
         + wc[2:3, :] * v)
    conv_out = gate_b * y

    x1, h2, logits = _mix_tail(x, pool_in, conv_out, g1, sc2, sh2, n2_ref[...], wpool_ref,
                               pscale_ref[...], wout_ref[...], wr_ref[...], br_ref[...])
    x1_ref[...] = x1
    _store_token_tiles(h2_ref, h2)
    lg_ref[...] = logits

    ubuf[0:POOL_HALO, :] = ubuf[tt:tt + POOL_HALO, :]
    vbuf[0:CONV_HALO, :] = vbuf[tt:tt + CONV_HALO, :]

    @pl.when(t == tiles_per_seq - 1)
    def _():
        upool_ref[...] = ubuf[0:POOL_HALO, :]
        vconv_ref[...] = vbuf[0:CONV_HALO, :]


def _mixer_prompt(x, mod_p, n1, n2, w_in, w_pool, pscale, w_conv, w_out, w_r, b_r, h2_s):
    b, t, d = x.shape
    dp = w_pool.shape[0] * w_pool.shape[1]
    tt = min(TOKEN_TILE, t)
    nt = t // tt
    n_real = b * nt
    assert h2_s.shape[0] == tt * SUBLANES
    seq = lambda i: jnp.minimum(i, n_real - 1) // nt
    tile = lambda i: jnp.minimum(i, n_real - 1)
    const2 = lambda i: (0, 0)
    const3 = lambda i: (0, 0, 0)
    return pl.pallas_call(
        functools.partial(_mixer_prompt_kernel, tiles_per_seq=nt),
        out_shape=(
            jax.ShapeDtypeStruct((b * t, d), F32),
            jax.ShapeDtypeStruct(((n_real + 1) * tt * SUBLANES, LANES), F32),
            jax.ShapeDtypeStruct((N_EXPERTS, b * t), F32),
            jax.ShapeDtypeStruct((b, POOL_HALO, dp), F32),
            jax.ShapeDtypeStruct((b, CONV_HALO, dp), F32),
        ),
        grid=(n_real + 1,),
        in_specs=[
            pl.BlockSpec((None, tt, d), lambda i: (seq(i), tile(i) % nt, 0)),
            pl.BlockSpec((None, 6, d), lambda i: (seq(i), 0, 0)),
            pl.BlockSpec((1, d), const2),
            pl.BlockSpec((1, d), const2),
            pl.BlockSpec(w_in.shape, const2),
            pl.BlockSpec(w_pool.shape, const3),
            pl.BlockSpec((1, dp), const2),
            pl.BlockSpec(w_conv.shape, const2),
            pl.BlockSpec(w_out.shape, const2),
            pl.BlockSpec(w_r.shape, const2),
            pl.BlockSpec((N_EXPERTS, 1), const2),
            pl.BlockSpec(h2_s.shape, const2),
        ],
        out_specs=(
            pl.BlockSpec((tt, d), lambda i: (tile(i), 0)),
            pl.BlockSpec((tt * SUBLANES, LANES), lambda i: (i, 0)),
            pl.BlockSpec((N_EXPERTS, tt), lambda i: (0, tile(i))),
            pl.BlockSpec((None, POOL_HALO, dp), lambda i: (seq(i), 0, 0)),
            pl.BlockSpec((None, CONV_HALO, dp), lambda i: (seq(i), 0, 0)),
        ),
        scratch_shapes=[
            pltpu.VMEM((POOL_HALO + tt, dp), F32),
            pltpu.VMEM((CONV_HALO + tt, dp), F32),
        ],
        compiler_params=pltpu.CompilerParams(
            dimension_semantics=("arbitrary",), vmem_limit_bytes=VMEM_LIMIT),
        name="mixer_prompt",
    )(x, mod_p, n1, n2, w_in, w_pool, pscale, w_conv, w_out, w_r, b_r, h2_s)


def _mixer_sample_kernel(x_ref, mod_ref, pstate_ref, cstate_ref, n1_ref, n2_ref, win_f32, wpool_f32,
                         pscale_ref, wconv_ref, wout_f32, wr_ref, br_ref,
                         x1_ref, h2_ref, lg_ref, newp_ref, newc_ref, win_ref, wpool_ref, wout_ref,
                         *, steps):
    win_ref[...] = win_f32[...].astype(BF16)
    wpool_ref[...] = wpool_f32[...].astype(BF16)
    wout_ref[...] = wout_f32[...].astype(BF16)
    nb = pstate_ref.shape[1]
    d = x_ref.shape[1]
    dp = pstate_ref.shape[2]
    gw = dp // len(POOL_WINDOWS)
    n_hist = pstate_ref.shape[0]
    n_chist = cstate_ref.shape[0]

    x = x_ref[...]
    mod = mod_ref[0:nb, :]
    rep = lambda a: jnp.concatenate([a] * steps, axis=0)
    sh1, sc1, g1, sh2, sc2, _ = [rep(mod[:, i * d:(i + 1) * d]) for i in range(6)]
    h = _rmsnorm(x, n1_ref[...]) * (1.0 + sc1) + sh1
    z = _dot(h.astype(BF16), win_ref[...])
    u, gate_b, gate_c, val = [z[:, i * dp:(i + 1) * dp] for i in range(4)]

    ext = [pstate_ref[i] for i in range(n_hist)] + [u[s * nb:(s + 1) * nb, :] for s in range(steps)]
    pooled_steps = []
    for s in range(steps):
        groups = []
        for g, w in enumerate(POOL_WINDOWS):
            cols = slice(g * gw, (g + 1) * gw)
            acc = ext[n_hist + s][:, cols]
            for j in range(1, w):
                acc = acc + ext[n_hist + s - j][:, cols]
            cnt = float(min(PAST_LEN + s + 1, w))
            groups.append(acc / cnt - ext[n_hist + s][:, cols])
        pooled_steps.append(jnp.concatenate(groups, axis=-1))
    pool_in = jnp.concatenate(pooled_steps, axis=0)

    v = gate_c * val
    vext = [cstate_ref[i] for i in range(n_chist)] + [v[s * nb:(s + 1) * nb, :] for s in range(steps)]
    wc = wconv_ref[...]
    y = jnp.concatenate(
        [wc[0:1, :] * vext[s] + wc[1:2, :] * vext[s + 1] + wc[2:3, :] * vext[s + 2] for s in range(steps)],
        axis=0)
    conv_out = gate_b * y

    x1, h2, logits = _mix_tail(x, pool_in, conv_out, g1, sc2, sh2, n2_ref[...], wpool_ref,
                               pscale_ref[...], wout_ref[...], wr_ref[...], br_ref[...])
    x1_ref[...] = x1
    _store_token_tiles(h2_ref, h2)
    lg_ref[...] = logits
    for i in range(n_hist):
        newp_ref[i] = ext[steps + i]
    for i in range(n_chist):
        newc_ref[i] = vext[steps + i]


def _mixer_sample(x_tm, mod_s, pstate_tm, cstate_tm, n1, n2, w_in, w_pool, pscale, w_conv, w_out, w_r, b_r,
                  steps):
    rows, d = x_tm.shape
    return pl.pallas_call(
        functools.partial(_mixer_sample_kernel, steps=steps),
        out_shape=(
            jax.ShapeDtypeStruct((rows, d), F32),
            jax.ShapeDtypeStruct((rows * SUBLANES, LANES), F32),
            jax.ShapeDtypeStruct((N_EXPERTS, rows), F32),
            jax.ShapeDtypeStruct(pstate_tm.shape, F32),
            jax.ShapeDtypeStruct(cstate_tm.shape, F32),
            jax.ShapeDtypeStruct(w_in.shape, BF16),
            jax.ShapeDtypeStruct(w_pool.shape, BF16),
            jax.ShapeDtypeStruct(w_out.shape, BF16),
        ),
        compiler_params=pltpu.CompilerParams(vmem_limit_bytes=VMEM_LIMIT),
        name="mixer_sample",
    )(x_tm, mod_s, pstate_tm, cstate_tm, n1, n2, w_in, w_pool, pscale, w_conv, w_out, w_r, b_r)


def _route_kernel(lgp_ref, lgs_ref, dest_ref, gate_ref, cnt_ref, counts, start, before, chosen):
    ne = lgp_ref.shape[0]
    tr = before.shape[0]
    reps = tr // LANES
    n_prompt_chunks = lgp_ref.shape[1] // tr
    n_sample_chunks = lgs_ref.shape[1] // tr
    eidx = lax.broadcasted_iota(I32, (ne, tr), 0)

    def selected(onehots):
        return jnp.where(onehots[0] | onehots[1] | onehots[2] | onehots[3], 1.0, 0.0)

    def select_chunk(ref, first_chunk):
        def body(c, carry):
            chunk = first_chunk + c
            work = ref[:, pl.ds(pl.multiple_of(c * tr, tr), tr)]
            top_v, onehots = [], []
            for k in range(TOP_K):
                m = jnp.max(work, axis=0, keepdims=True)
                idx = jnp.min(jnp.where(work == m, eidx, ne), axis=0, keepdims=True)
                chosen[chunk, k:k + 1, :] = idx
                top_v.append(m)
                onehots.append(eidx == idx)
                work = jnp.where(onehots[k], -jnp.inf, work)
            es = [jnp.exp(v - top_v[0]) for v in top_v]
            denom = es[0] + es[1] + es[2] + es[3]
            cols = pl.ds(pl.multiple_of(chunk * tr, tr), tr)
            for k in range(TOP_K):
                gate_ref[k:k + 1, cols] = es[k] / denom
            chunk_counts = jnp.sum(selected(onehots), axis=1, keepdims=True)
            counts[...] = counts[...] + jnp.broadcast_to(chunk_counts, (ne, LANES))
            return carry
        return body

    counts[...] = jnp.zeros_like(counts)
    gate_ref[...] = jnp.zeros_like(gate_ref)
    lax.fori_loop(0, n_prompt_chunks, select_chunk(lgp_ref, 0), 0)
    lax.fori_loop(0, n_sample_chunks, select_chunk(lgs_ref, n_prompt_chunks), 0)

    total = counts[...]
    hi = jnp.floor(total * (1.0 / 256.0))
    lo = total - hi * 256.0
    r = lax.broadcasted_iota(I32, (ne, ne), 0)
    col = lax.broadcasted_iota(I32, (ne, ne), 1)
    lower = jnp.where(col < r, 1.0, 0.0).astype(BF16)
    start[...] = 256.0 * _dot(lower, hi.astype(BF16)) + _dot(lower, lo.astype(BF16))
    cnt_ref[...] = total
    counts[...] = jnp.zeros_like(counts)

    r = lax.broadcasted_iota(I32, (tr, tr), 0)
    col = lax.broadcasted_iota(I32, (tr, tr), 1)
    before[...] = jnp.where(r < col, 1.0, 0.0).astype(BF16)

    def place_chunk(chunk, carry):
        onehots = [eidx == chosen[chunk, k:k + 1, :] for k in range(TOP_K)]
        mask = selected(onehots)
        base = jnp.concatenate([counts[...] + start[...]] * reps, axis=1)
        rank = _dot(mask.astype(BF16), before[...]) + base
        counts[...] = counts[...] + jnp.broadcast_to(jnp.sum(mask, axis=1, keepdims=True), (ne, LANES))
        for k in range(TOP_K):
            d = jnp.sum(jnp.where(onehots[k], rank, 0.0), axis=0, keepdims=True).astype(I32)
            for j in range(reps):
                dest_ref[chunk * reps + j, k:k + 1, :] = d[:, j * LANES:(j + 1) * LANES]
        return carry

    lax.fori_loop(0, n_prompt_chunks + n_sample_chunks, place_chunk, 0)


def _route(lgt_p, lgt_s):
    ne, n_p = lgt_p.shape
    n_s = lgt_s.shape[1]
    tr = ROUTE_TILE
    return pl.pallas_call(
        _route_kernel,
        out_shape=(
            jax.ShapeDtypeStruct(((n_p + n_s) // LANES, TOP_K, LANES), I32),
            jax.ShapeDtypeStruct((SUBLANES, n_p + n_s), F32),
            jax.ShapeDtypeStruct((ne, LANES), F32),
        ),
        scratch_shapes=[pltpu.VMEM((ne, LANES), F32), pltpu.VMEM((ne, LANES), F32),
                        pltpu.VMEM((tr, tr), BF16), pltpu.VMEM(((n_p + n_s) // tr, TOP_K, tr), I32)],
        compiler_params=pltpu.CompilerParams(vmem_limit_bytes=VMEM_LIMIT),
        name="route",
    )(lgt_p, lgt_s)


def _group_metadata(counts, n_rows, tile):
    n_tiles = n_rows // tile
    n_steps = n_tiles + N_EXPERTS - 1
    ends = jnp.cumsum(counts)
    offs = jnp.concatenate([jnp.zeros((1,), I32), ends]).astype(I32)
    first_tile = offs[:-1] // tile
    last_tile = (ends - 1) // tile
    tiles_e = jnp.where(counts > 0, last_tile - first_tile + 1, 0)
    step_end = jnp.cumsum(tiles_e)
    step_start = step_end - tiles_e
    n_active = step_end[-1]
    s = jnp.minimum(jnp.arange(n_steps, dtype=I32), n_active - 1)
    owner = ((s[:, None] >= step_start[None, :]) & (s[:, None] < step_end[None, :])).astype(I32)
    gid = jnp.sum(owner * jnp.arange(N_EXPERTS, dtype=I32)[None, :], axis=1)
    tid = jnp.sum(owner * (first_tile - step_start)[None, :], axis=1) + s
    ids = jnp.arange(N_EXPERTS, dtype=I32)
    later = (ids[None, :] > ids[:, None]) & (counts[None, :] > 0)
    next_e = jnp.min(jnp.where(later, ids[None, :], N_EXPERTS), axis=1)
    next_e = jnp.where(next_e == N_EXPERTS, -1, next_e)
    nxt = jnp.sum(owner * next_e[None, :], axis=1)
    return gid, tid, nxt, offs, n_active.reshape(1).astype(I32), n_steps


SC_LANES = 16
SC_INDEX_BATCH = 128


def _invert(dest):
    r = dest.shape[0]
    mesh = plsc.VectorSubcoreMesh(core_axis_name="core", subcore_axis_name="subcore")
    per = r // mesh.num_subcores
    assert per * mesh.num_subcores == r and per % SC_INDEX_BATCH == 0

    @functools.partial(
        pl.kernel, mesh=mesh, out_type=jax.ShapeDtypeStruct((r,), I32),
        scratch_types=[pltpu.VMEM_SHARED((r,), I32), pltpu.VMEM((per,), I32), pltpu.VMEM((per,), I32)],
        compiler_params=pltpu.CompilerParams(needs_layout_passes=False),
        name="invert",
    )
    def invert(dest_hbm, inv_hbm, table, idx, ids):
        @pl.when(lax.axis_index("core") == 0)
        def _():
            base = lax.axis_index("subcore") * per
            pltpu.sync_copy(dest_hbm.at[pl.ds(base, per)], idx)
            lane = lax.iota(I32, SC_LANES)

            @pl.loop(0, per // SC_LANES)
            def _(i):
                p = base + i * SC_LANES + lane
                tok = lax.shift_right_logical(p, 9) * LANES + (p & (LANES - 1))
                slot = lax.shift_right_logical(p, 7) & (TOP_K - 1)
                ids[pl.ds(i * SC_LANES, SC_LANES)] = tok * TOP_K + slot

            @pl.loop(0, per // SC_INDEX_BATCH)
            def _(j):
                span = pl.ds(j * SC_INDEX_BATCH, SC_INDEX_BATCH)
                pltpu.sync_copy(ids.at[span], table.at[idx.at[span]])

            plsc.subcore_barrier()
            pltpu.sync_copy(table.at[pl.ds(base, per)], inv_hbm.at[pl.ds(base, per)])

    return invert(dest)


def _experts_kernel(gid_ref, tid_ref, nxt_ref, offs_ref, nact_ref,
                    dstp_ref, dst0_ref, src0_ref, src1_ref, src2_ref, bg_ref, bu_ref, bd_ref,
                    h2_hbm, wg_hbm, wu_hbm, wd_hbm, y4_hbm,
                    wg_b, wu_b, wd_b, wg_f, wu_f, wd_f, relay, rows, ybuf, sems, row_sems, y_sems):
    s = pl.program_id(0)
    tile_rows = rows.shape[1]
    tm = tile_rows // SUBLANES
    landing = ((wg_hbm, wg_f, wg_b), (wu_hbm, wu_f, wu_b), (wd_hbm, wd_f, wd_b))

    def fetch(e):
        for j, (hbm, land, _) in enumerate(landing):
            pltpu.make_async_copy(hbm.at[e], land, sems.at[j]).start()

    def row_in(idx_ref, into, r):
        src = h2_hbm.at[pl.ds(pl.multiple_of(idx_ref[r], SUBLANES), SUBLANES)]
        dst = rows.at[into, pl.ds(pl.multiple_of(r * SUBLANES, SUBLANES), SUBLANES)]
        return pltpu.make_async_copy(src, dst, row_sems.at[into])

    def row_out(idx_ref, frm, r):
        src = ybuf.at[frm, pl.ds(pl.multiple_of(r * SUBLANES, SUBLANES), SUBLANES)]
        dst = y4_hbm.at[pl.ds(pl.multiple_of(idx_ref[r], SUBLANES), SUBLANES)]
        return pltpu.make_async_copy(src, dst, y_sems.at[frm])

    def in_line(make, idx_ref, buf, first, count):
        for r in range(count):
            make(idx_ref, buf, first + r).start(priority=r % 2)

    def in_loop(make, idx_ref, buf):
        def group(g, carry):
            for j in range(ISSUE_GROUP):
                make(idx_ref, buf, g * ISSUE_GROUP + j).start(priority=j % 2)
            return carry
        lax.fori_loop(0, tm // ISSUE_GROUP, group, 0)

    def wait_rows(buf):
        pltpu.make_async_copy(h2_hbm.at[pl.ds(0, tile_rows)], rows.at[buf], row_sems.at[buf]).wait()

    def wait_y(buf):
        pltpu.make_async_copy(ybuf.at[buf], y4_hbm.at[pl.ds(0, tile_rows)], y_sems.at[buf]).wait()

    @pl.when(s < nact_ref[0])
    def _():
        e = gid_ref[s]
        m = tid_ref[s]
        cur = lax.rem(m, 3)
        before = lax.rem(m + 2, 3)
        last = nact_ref[0] - 1
        new_tile = (s == 0) | (tid_ref[jnp.maximum(s - 1, 0)] != m)

        @pl.when(s == 0)
        def _():
            fetch(e)
            in_loop(row_in, src0_ref, 0)
            in_loop(row_in, src1_ref, 1)
            ybuf[2] = jnp.zeros(ybuf.shape[1:], F32)

        @pl.when((s == 0) | (gid_ref[jnp.maximum(s - 1, 0)] != e))
        def _():
            for j, (hbm, land, half) in enumerate(landing):
                pltpu.make_async_copy(hbm.at[e], land, sems.at[j]).wait()
                half[...] = land[...].astype(BF16)

            @pl.when(nxt_ref[s] >= 0)
            def _():
                fetch(nxt_ref[s])

        @pl.when(new_tile)
        def _():
            wait_rows(cur)

            @pl.when(m >= 2)
            def _():
                wait_y(cur)

        def ffn(load_x, after_gate=lambda: None):
            g = _dot(load_x(), wg_b[...]) + bg_ref[pl.ds(e, 1), :]
            after_gate()
            u = _dot(load_x(), wu_b[...]) + bu_ref[pl.ds(e, 1), :]
            g = jnp.minimum(g, SWIGLU_LIMIT)
            u = jnp.clip(u, -SWIGLU_LIMIT, SWIGLU_LIMIT)
            glu = g * jax.nn.sigmoid(SWIGLU_ALPHA * g)
            return _dot(((u + 1.0) * glu).astype(BF16), wd_b[...]) + bd_ref[pl.ds(e, 1), :]

        lo = offs_ref[e]
        hi = offs_ref[e + 1]
        whole_tile = (lo <= m * tm) & (hi >= (m + 1) * tm)

        def ffn_and_copies(load_x, first, count):
            pinned = count // 2
            y = ffn(load_x, lambda: in_line(row_in, src2_ref, before, first, pinned))
            in_line(row_in, src2_ref, before, first + pinned, count - pinned)
            in_line(row_out, dstp_ref, before, first, count)
            return y

        @pl.when(whole_tile)
        def _():
            y = ffn_and_copies(lambda: _load_token_tiles(rows.at[cur]).astype(BF16), 0, tm)
            _store_token_tiles(ybuf.at[cur], y)

        @pl.when(jnp.logical_not(whole_tile))
        def _():
            sub = relay.shape[0] // SUBLANES

            def sub_block(j, carry):
                first = m * tm + j * sub
                span = pl.ds(pl.multiple_of(j * (sub * SUBLANES), sub * SUBLANES), sub * SUBLANES)
                load_x = lambda: _load_token_tiles(rows.at[cur, span]).astype(BF16)

                def put(y):
                    _store_token_tiles(relay, y)
                    row = first + lax.shift_right_logical(lax.broadcasted_iota(I32, relay.shape, 0), 3)
                    pltpu.store(ybuf.at[cur, span], relay[...], mask=(row >= lo) & (row < hi))

                evaluate = (lo < first + sub) & (hi > first)
                owns_end = hi >= first + sub

                @pl.when(evaluate & owns_end)
                def _():
                    put(ffn_and_copies(load_x, j * sub, sub))

                @pl.when(evaluate & jnp.logical_not(owns_end))
                def _():
                    put(ffn(load_x))

                return carry

            lax.fori_loop(0, tm // sub, sub_block, 0)

        @pl.when(s == last)
        def _():
            in_loop(row_out, dst0_ref, cur)
            wait_rows(lax.rem(m + 1, 3))
            wait_rows(before)
            for buf in range(3):
                wait_y(buf)


def _experts(gid, tid, nxt, offs, nact, n_steps, inv, h2, wg, bg, wu, bu, wd, bd):
    ne, d, f = wg.shape
    tm = EXPERT_TILE
    n_tokens = h2.shape[0] // SUBLANES
    n_tiles = inv.shape[0] // tm
    assert n_tiles >= 2
    slot_rows = (n_tokens + tm // TOP_K) * SUBLANES
    inv_ext = jnp.concatenate([n_tokens * TOP_K + jnp.arange(tm, dtype=I32), inv])
    tok = lax.shift_right_logical(inv_ext, TOP_K.bit_length() - 1)
    src_row = tok * SUBLANES
    dst_row = (inv_ext & (TOP_K - 1)) * slot_rows + src_row
    whole = lambda s, gid, tid, nxt, offs, nact: (0, 0)

    def order_of(k):
        return pl.BlockSpec(
            (tm,), lambda s, gid, tid, nxt, offs, nact: (jnp.minimum(tid[s] + k, n_tiles - 1) + 1,),
            memory_space=pltpu.SMEM)

    grid_spec = pltpu.PrefetchScalarGridSpec(
        num_scalar_prefetch=5,
        grid=(n_steps,),
        in_specs=[
            order_of(-1), order_of(0), order_of(0), order_of(1), order_of(2),
            pl.BlockSpec((ne, f), whole),
            pl.BlockSpec((ne, f), whole),
            pl.BlockSpec((ne, d), whole),
            pl.BlockSpec(memory_space=pl.ANY),
            pl.BlockSpec(memory_space=pl.ANY),
            pl.BlockSpec(memory_space=pl.ANY),
            pl.BlockSpec(memory_space=pl.ANY),
        ],
        out_specs=pl.BlockSpec(memory_space=pl.ANY),
        scratch_shapes=[
            pltpu.VMEM((d, f), BF16), pltpu.VMEM((d, f), BF16), pltpu.VMEM((f, d), BF16),
            pltpu.VMEM((d, f), F32), pltpu.VMEM((d, f), F32), pltpu.VMEM((f, d), F32),
            pltpu.VMEM((EXPERT_SUBTILE * SUBLANES, LANES), F32),
            pltpu.VMEM((3, tm * SUBLANES, LANES), F32),
            pltpu.VMEM((3, tm * SUBLANES, LANES), F32),
            pltpu.SemaphoreType.DMA((3,)),
            pltpu.SemaphoreType.DMA((3,)),
            pltpu.SemaphoreType.DMA((3,)),
        ],
    )
    y4 = pl.pallas_call(
        _experts_kernel,
        out_shape=jax.ShapeDtypeStruct((TOP_K * slot_rows, LANES), F32),
        grid_spec=grid_spec,
        compiler_params=pltpu.CompilerParams(
            dimension_semantics=("arbitrary",), vmem_limit_bytes=VMEM_LIMIT),
        name="experts",
    )(gid, tid, nxt, offs, nact, dst_row, dst_row, src_row, src_row, src_row, bg, bu, bd, h2, wg, wu, wd)
    return y4.reshape(TOP_K, slot_rows, LANES)


def _combine_kernel(gate_ref, y4_ref, x1p_ref, x1s_ref, modp_ref, g2s_ref, fn_ref, outp_ref, outs_ref,
                    *, n_prompt_tiles):
    i = pl.program_id(0)
    tm = x1p_ref.shape[0]
    gates = gate_ref[...].T
    cols = []
    for c in range(SUBLANES):
        acc = gates[:, 0:1] * y4_ref[0, pl.ds(c, tm, stride=SUBLANES), :]
        for k in range(1, TOP_K):
            acc = acc + gates[:, k:k + 1] * y4_ref[k, pl.ds(c, tm, stride=SUBLANES), :]
        cols.append(acc)
    ffn = jnp.concatenate(cols, axis=-1)

    @pl.when(i < n_prompt_tiles)
    def _():
        g2 = modp_ref[...][5:6, :]
        outp_ref[...] = _rmsnorm(x1p_ref[...] + g2 * ffn, fn_ref[...])

    @pl.when(i >= n_prompt_tiles)
    def _():
        g2 = jnp.concatenate([g2s_ref[...]] * (tm // g2s_ref.shape[0]), axis=0)
        outs_ref[...] = _rmsnorm(x1s_ref[...] + g2 * ffn, fn_ref[...])


def _combine(gates, y4, x1_p, x1_s, mod_p, mod, n_batch_s, final_norm, tokens_per_seq):
    n_p, d = x1_p.shape
    n_s = x1_s.shape[0]
    tm = COMBINE_TILE
    assert tm % n_batch_s == 0 and n_batch_s % SUBLANES == 0
    npt, nst = n_p // tm, n_s // tm
    tiles_per_seq = tokens_per_seq // tm
    pmap = lambda i: (jnp.minimum(i, npt - 1), 0)
    smap = lambda i: (jnp.maximum(i - npt, 0), 0)
    return pl.pallas_call(
        functools.partial(_combine_kernel, n_prompt_tiles=npt),
        out_shape=(jax.ShapeDtypeStruct((n_p, d), F32), jax.ShapeDtypeStruct((n_s, d), F32)),
        grid=(npt + nst,),
        in_specs=[
            pl.BlockSpec((SUBLANES, tm), lambda i: (0, i)),
            pl.BlockSpec((TOP_K, tm * SUBLANES, LANES), lambda i: (0, i, 0)),
            pl.BlockSpec((tm, d), pmap),
            pl.BlockSpec((tm, d), smap),
            pl.BlockSpec((None, 6, d), lambda i: (jnp.minimum(i, npt - 1) // tiles_per_seq, 0, 0)),
            pl.BlockSpec((n_batch_s, d), lambda i: (0, 5)),
            pl.BlockSpec((1, d), lambda i: (0, 0)),
        ],
        out_specs=(pl.BlockSpec((tm, d), pmap), pl.BlockSpec((tm, d), smap)),
        compiler_params=pltpu.CompilerParams(
            dimension_semantics=("arbitrary",), vmem_limit_bytes=VMEM_LIMIT),
        name="combine",
    )(gates, y4, x1_p, x1_s, mod_p, mod, final_norm)


def kernel(x_prompt, x_sample, state_pool, state_conv, c_prompt, c_sample, norm1, norm2, w_ada, b_ada,
           w_in, w_pool, pool_scale, w_conv, w_out, w_router, b_router, w_gate, b_gate, w_up, b_up,
           w_down, b_down, final_norm):
    depth = norm1.shape[0]
    assert depth == 1, "single-layer step"
    bp, tp, d = x_prompt.shape
    bs, ts, _ = x_sample.shape
    dp = state_pool.shape[-1]
    n_hist = state_pool.shape[2]
    n_chist = state_conv.shape[2]
    n_p, n_s = bp * tp, bs * ts
    assert d == SUBLANES * LANES, "token-tile layout assumes one vreg tile per token row"
    assert tp % TOKEN_TILE == 0 and n_s == TOKEN_TILE, "the sample group fills exactly one token tile"
    assert tp % COMBINE_TILE == 0 and n_s % COMBINE_TILE == 0
    assert (n_p + n_s) % ROUTE_TILE == 0 and ((n_p + n_s) * TOP_K) % EXPERT_TILE == 0

    l = 0
    n1 = norm1[l].reshape(1, d)
    n2 = norm2[l].reshape(1, d)
    pscale = pool_scale[l].reshape(1, dp)
    w_r = w_router[l].T.astype(BF16)
    b_r = b_router[l].reshape(N_EXPERTS, 1)

    mod = _adaln(jnp.concatenate([c_sample, c_prompt], axis=0), w_ada[l], b_ada[l])
    mod_p = mod[bs:].reshape(bp, 6, d)

    xs_tm = jnp.transpose(x_sample, (1, 0, 2)).reshape(n_s, d)
    ps_tm = jnp.transpose(state_pool[l], (1, 0, 2))
    cs_tm = jnp.transpose(state_conv[l], (1, 0, 2))
    x1_s, h2_s, lg_s, newp_tm, newc_tm, w_in_b, w_pool_b, w_out_b = _mixer_sample(
        xs_tm, mod, ps_tm, cs_tm, n1, n2, w_in[l], w_pool[l], pscale, w_conv[l], w_out[l], w_r, b_r, ts)

    x1_p, h2, lg_p, u_tail, v_tail = _mixer_prompt(
        x_prompt, mod_p, n1, n2, w_in_b, w_pool_b, pscale, w_conv[l], w_out_b, w_r, b_r, h2_s)

    dest, gates, counts_f = _route(lg_p, lg_s)
    counts = counts_f[:, 0].astype(I32)
    n_rows = (n_p + n_s) * TOP_K
    gid, tid, nxt, offs, nact, n_steps = _group_metadata(counts, n_rows, EXPERT_TILE)

    dest = dest.reshape(-1)
    inv = _invert(dest)
    y4 = _experts(gid, tid, nxt, offs, nact, n_steps, inv, h2,
                  w_gate[l], b_gate[l], w_up[l], b_up[l], w_down[l], b_down[l])

    y_p, y_s = _combine(gates, y4, x1_p, x1_s, mod_p, mod, bs, final_norm.reshape(1, d), tp)

    y_prompt = y_p.reshape(bp, tp, d)
    y_sample = jnp.transpose(y_s.reshape(ts, bs, d), (1, 0, 2))
    new_pool_prompt = u_tail[:, POOL_HALO - n_hist:, :][None]
    new_conv_prompt = v_tail[:, CONV_HALO - n_chist:, :][None]
    new_pool_sample = jnp.transpose(newp_tm, (1, 0, 2))[None]
    new_conv_sample = jnp.transpose(newc_tm, (1, 0, 2))[None]
    return (y_prompt, y_sample, new_pool_prompt, new_conv_prompt, new_pool_sample, new_conv_sample)
```

```python
import functools

import jax
import jax.numpy as jnp
from jax import lax
from jax.experimental import pallas as pl
from jax.experimental.pallas import tpu as pltpu
from jax.experimental.pallas import tpu_sc as plsc

F32 = jnp.float32
BF16 = jnp.bfloat16
I32 = jnp.int32

POOL_WINDOWS = (2, 4, 8, 16)
POOL_HALO = 16
CONV_TAPS = 3
CONV_HALO = 8
N_EXPERTS = 32
TOP_K = 4
SWIGLU_LIMIT = 7.0
SWIGLU_ALPHA = 1.702
EPS = 1e-5
PAST_LEN = 16384

LANES = 128
SUBLANES = 8

TOKEN_TILE = 512
ROUTE_TILE = 256
COMBINE_TILE = 512
EXPERT_TILE = 512
EXPERT_SUBTILE = 128
VMEM_LIMIT = 56 * 1024 * 1024


def _rmsnorm(x, g):
    ms = jnp.mean(x * x, axis=-1, keepdims=True)
    return x * lax.rsqrt(ms + EPS) * g


def _dot(a, b):
    return jnp.dot(a, b, preferred_element_type=F32)


def _store_token_tiles(ref, val):
    rows = val.shape[0]
    for c in range(SUBLANES):
        ref[pl.ds(c, rows, stride=SUBLANES), :] = val[:, c * LANES:(c + 1) * LANES]


def _load_token_tiles(ref):
    rows = ref.shape[0] // SUBLANES
    return jnp.concatenate([ref[pl.ds(c, rows, stride=SUBLANES), :] for c in range(SUBLANES)], axis=-1)


ISSUE_GROUP = 4


def _adaln_kernel(c_ref, w_ref, b_ref, o_ref):
    c = c_ref[...]
    s = c * jax.nn.sigmoid(c)
    o_ref[...] = _dot(s.astype(BF16), w_ref[...].astype(BF16)) + b_ref[...]


def _adaln(c, w_ada, b_ada):
    rows, d = c.shape
    n = w_ada.shape[1]
    tn = 1024
    return pl.pallas_call(
        _adaln_kernel,
        out_shape=jax.ShapeDtypeStruct((rows, n), F32),
        grid=(n // tn,),
        in_specs=[
            pl.BlockSpec((rows, d), lambda j: (0, 0)),
            pl.BlockSpec((d, tn), lambda j: (0, j)),
            pl.BlockSpec((1, tn), lambda j: (0, j)),
        ],
        out_specs=pl.BlockSpec((rows, tn), lambda j: (0, j)),
        compiler_params=pltpu.CompilerParams(
            dimension_semantics=("arbitrary",), vmem_limit_bytes=VMEM_LIMIT),
        name="adaln",
    )(c, w_ada, b_ada.reshape(1, n))


def _mix_tail(x, pool_in, conv_out, g1, sc2, sh2, n2, wpool_ref, pscale, wout, wr, br):
    gw = pool_in.shape[1] // len(POOL_WINDOWS)
    mixed = [_dot(pool_in[:, g * gw:(g + 1) * gw].astype(BF16), wpool_ref[g])
             for g in range(len(POOL_WINDOWS))]
    pool_out = jnp.concatenate(mixed, axis=-1) * pscale
    mix_in = jnp.concatenate([pool_out, conv_out], axis=-1).astype(BF16)
    x1 = x + g1 * _dot(mix_in, wout)
    h2 = _rmsnorm(x1, n2) * (1.0 + sc2) + sh2
    logits_t = lax.dot_general(wr, h2.astype(BF16), (((1,), (1,)), ((), ())),
                               preferred_element_type=F32) + br
    return x1, h2, logits_t


def _mixer_prompt_kernel(x_ref, mod_ref, n1_ref, n2_ref, win_ref, wpool_ref, pscale_ref, wconv_ref,
                         wout_ref, wr_ref, br_ref, h2s_ref,
                         x1_ref, h2_ref, lg_ref, upool_ref, vconv_ref, ubuf, vbuf, *, tiles_per_seq):
    i = pl.program_id(0)

    @pl.when(i < pl.num_programs(0) - 1)
    def _():
        _mixer_prompt_tile(x_ref, mod_ref, n1_ref, n2_ref, win_ref, wpool_ref, pscale_ref, wconv_ref,
                           wout_ref, wr_ref, br_ref, x1_ref, h2_ref, lg_ref, upool_ref, vconv_ref,
                           ubuf, vbuf, lax.rem(i, tiles_per_seq), tiles_per_seq)

    @pl.when(i == pl.num_programs(0) - 1)
    def _():
        h2_ref[...] = h2s_ref[...]


def _mixer_prompt_tile(x_ref, mod_ref, n1_ref, n2_ref, win_ref, wpool_ref, pscale_ref, wconv_ref,
                       wout_ref, wr_ref, br_ref, x1_ref, h2_ref, lg_ref, upool_ref, vconv_ref,
                       ubuf, vbuf, t, tiles_per_seq):
    tt = x_ref.shape[0]
    dp = ubuf.shape[1]
    gw = dp // len(POOL_WINDOWS)

    @pl.when(t == 0)
    def _():
        ubuf[0:POOL_HALO, :] = jnp.zeros((POOL_HALO, dp), F32)
        vbuf[0:CONV_HALO, :] = jnp.zeros((CONV_HALO, dp), F32)

    x = x_ref[...]
    mod = mod_ref[...]
    sh1, sc1, g1, sh2, sc2, _ = [mod[i:i + 1, :] for i in range(6)]
    h = _rmsnorm(x, n1_ref[...]) * (1.0 + sc1) + sh1
    z = _dot(h.astype(BF16), win_ref[...])
    u, gate_b, gate_c, val = [z[:, i * dp:(i + 1) * dp] for i in range(4)]

    ubuf[POOL_HALO:POOL_HALO + tt, :] = u
    pos = lax.broadcasted_iota(I32, (tt, gw), 0) + t * tt
    pooled = []
    for g, w in enumerate(POOL_WINDOWS):
        cols = slice(g * gw, (g + 1) * gw)
        acc = u[:, cols]
        for j in range(1, w):
            acc = acc + ubuf[POOL_HALO - j:POOL_HALO - j + tt, cols]
        cnt = jnp.minimum(pos + 1, w).astype(F32)
        pooled.append(acc / cnt - u[:, cols])
    pool_in = jnp.concatenate(pooled, axis=-1)

    v = gate_c * val
    vbuf[CONV_HALO:CONV_HALO + tt, :] = v
    wc = wconv_ref[...]
    y = (wc[0:1, :] * vbuf[CONV_HALO - 2:CONV_HALO - 2 + tt, :]
         + wc[1:2, :] * vbuf[CONV_HALO - 1:CONV_HALO - 1 + tt, :]
         + wc[2:3, :] * v)
    conv_out = gate_b * y

    x1, h2, logits = _mix_tail(x, pool_in, conv_out, g1, sc2, sh2, n2_ref[...], wpool_ref,
                               pscale_ref[...], wout_ref[...], wr_ref[...], br_ref[...])
    x1_ref[...] = x1
    _store_token_tiles(h2_ref, h2)
    lg_ref[...] = logits

    ubuf[0:POOL_HALO, :] = ubuf[tt:tt + POOL_HALO, :]
    vbuf[0:CONV_HALO, :] = vbuf[tt:tt + CONV_HALO, :]

    @pl.when(t == tiles_per_seq - 1)
    def _():
        upool_ref[...] = ubuf[0:POOL_HALO, :]
        vconv_ref[...] = vbuf[0:CONV_HALO, :]


def _mixer_prompt(x, mod_p, n1, n2, w_in, w_pool, pscale, w_conv, w_out, w_r, b_r, h2_s):
    b, t, d = x.shape
    dp = w_pool.shape[0] * w_pool.shape[1]
    tt = min(TOKEN_TILE, t)
    nt = t // tt
    n_real = b * nt
    assert h2_s.shape[0] == tt * SUBLANES
    seq = lambda i: jnp.minimum(i, n_real - 1) // nt
    tile = lambda i: jnp.minimum(i, n_real - 1)
    const2 = lambda i: (0, 0)
    const3 = lambda i: (0, 0, 0)
    return pl.pallas_call(
        functools.partial(_mixer_prompt_kernel, tiles_per_seq=nt),
        out_shape=(
            jax.ShapeDtypeStruct((b * t, d), F32),
            jax.ShapeDtypeStruct(((n_real + 1) * tt * SUBLANES, LANES), F32),
            jax.ShapeDtypeStruct((N_EXPERTS, b * t), F32),
            jax.ShapeDtypeStruct((b, POOL_HALO, dp), F32),
            jax.ShapeDtypeStruct((b, CONV_HALO, dp), F32),
        ),
        grid=(n_real + 1,),
        in_specs=[
            pl.BlockSpec((None, tt, d), lambda i: (seq(i), tile(i) % nt, 0)),
            pl.BlockSpec((None, 6, d), lambda i: (seq(i), 0, 0)),
            pl.BlockSpec((1, d), const2),
            pl.BlockSpec((1, d), const2),
            pl.BlockSpec(w_in.shape, const2),
            pl.BlockSpec(w_pool.shape, const3),
            pl.BlockSpec((1, dp), const2),
            pl.BlockSpec(w_conv.shape, const2),
            pl.BlockSpec(w_out.shape, const2),
            pl.BlockSpec(w_r.shape, const2),
            pl.BlockSpec((N_EXPERTS, 1), const2),
            pl.BlockSpec(h2_s.shape, const2),
        ],
        out_specs=(
            pl.BlockSpec((tt, d), lambda i: (tile(i), 0)),
            pl.BlockSpec((tt * SUBLANES, LANES), lambda i: (i, 0)),
            pl.BlockSpec((N_EXPERTS, tt), lambda i: (0, tile(i))),
            pl.BlockSpec((None, POOL_HALO, dp), lambda i: (seq(i), 0, 0)),
            pl.BlockSpec((None, CONV_HALO, dp), lambda i: (seq(i), 0, 0)),
        ),
        scratch_shapes=[
            pltpu.VMEM((POOL_HALO + tt, dp), F32),
            pltpu.VMEM((CONV_HALO + tt, dp), F32),
        ],
        compiler_params=pltpu.CompilerParams(
            dimension_semantics=("arbitrary",), vmem_limit_bytes=VMEM_LIMIT),
        name="mixer_prompt",
    )(x, mod_p, n1, n2, w_in, w_pool, pscale, w_conv, w_out, w_r, b_r, h2_s)


def _mixer_sample_kernel(x_ref, mod_ref, pstate_ref, cstate_ref, n1_ref, n2_ref, win_f32, wpool_f32,
                         pscale_ref, wconv_ref, wout_f32, wr_ref, br_ref,
                         x1_ref, h2_ref, lg_ref, newp_ref, newc_ref, win_ref, wpool_ref, wout_ref,
                         *, steps):
    win_ref[...] = win_f32[...].astype(BF16)
    wpool_ref[...] = wpool_f32[...].astype(BF16)
    wout_ref[...] = wout_f32[...].astype(BF16)
    nb = pstate_ref.shape[1]
    d = x_ref.shape[1]
    dp = pstate_ref.shape[2]
    gw = dp // len(POOL_WINDOWS)
    n_hist = pstate_ref.shape[0]
    n_chist = cstate_ref.shape[0]

    x = x_ref[...]
    mod = mod_ref[0:nb, :]
    rep = lambda a: jnp.concatenate([a] * steps, axis=0)
    sh1, sc1, g1, sh2, sc2, _ = [rep(mod[:, i * d:(i + 1) * d]) for i in range(6)]
    h = _rmsnorm(x, n1_ref[...]) * (1.0 + sc1) + sh1
    z = _dot(h.astype(BF16), win_ref[...])
    u, gate_b, gate_c, val = [z[:, i * dp:(i + 1) * dp] for i in range(4)]

    ext = [pstate_ref[i] for i in range(n_hist)] + [u[s * nb:(s + 1) * nb, :] for s in range(steps)]
    pooled_steps = []
    for s in range(steps):
        groups = []
        for g, w in enumerate(POOL_WINDOWS):
            cols = slice(g * gw, (g + 1) * gw)
            acc = ext[n_hist + s][:, cols]
            for j in range(1, w):
                acc = acc + ext[n_hist + s - j][:, cols]
            cnt = float(min(PAST_LEN + s + 1, w))
            groups.append(acc / cnt - ext[n_hist + s][:, cols])
        pooled_steps.append(jnp.concatenate(groups, axis=-1))
    pool_in = jnp.concatenate(pooled_steps, axis=0)

    v = gate_c * val
    vext = [cstate_ref[i] for i in range(n_chist)] + [v[s * nb:(s + 1) * nb, :] for s in range(steps)]
    wc = wconv_ref[...]
    y = jnp.concatenate(
        [wc[0:1, :] * vext[s] + wc[1:2, :] * vext[s + 1] + wc[2:3, :] * vext[s + 2] for s in range(steps)],
        axis=0)
    conv_out = gate_b * y

    x1, h2, logits = _mix_tail(x, pool_in, conv_out, g1, sc2, sh2, n2_ref[...], wpool_ref,
                               pscale_ref[...], wout_ref[...], wr_ref[...], br_ref[...])
    x1_ref[...] = x1
    _store_token_tiles(h2_ref, h2)
    lg_ref[...] = logits
    for i in range(n_hist):
        newp_ref[i] = ext[steps + i]
    for i in range(n_chist):
        newc_ref[i] = vext[steps + i]


def _mixer_sample(x_tm, mod_s, pstate_tm, cstate_tm, n1, n2, w_in, w_pool, pscale, w_conv, w_out, w_r, b_r,
                  steps):
    rows, d = x_tm.shape
    return pl.pallas_call(
        functools.partial(_mixer_sample_kernel, steps=steps),
        out_shape=(
            jax.ShapeDtypeStruct((rows, d), F32),
            jax.ShapeDtypeStruct((rows * SUBLANES, LANES), F32),
            jax.ShapeDtypeStruct((N_EXPERTS, rows), F32),
            jax.ShapeDtypeStruct(pstate_tm.shape, F32),
            jax.ShapeDtypeStruct(cstate_tm.shape, F32),
            jax.ShapeDtypeStruct(w_in.shape, BF16),
            jax.ShapeDtypeStruct(w_pool.shape, BF16),
            jax.ShapeDtypeStruct(w_out.shape, BF16),
        ),
        compiler_params=pltpu.CompilerParams(vmem_limit_bytes=VMEM_LIMIT),
        name="mixer_sample",
    )(x_tm, mod_s, pstate_tm, cstate_tm, n1, n2, w_in, w_pool, pscale, w_conv, w_out, w_r, b_r)


def _route_kernel(lgp_ref, lgs_ref, dest_ref, gate_ref, cnt_ref, counts, start, before, chosen):
    ne = lgp_ref.shape[0]
    tr = before.shape[0]
    reps = tr // LANES
    n_prompt_chunks = lgp_ref.shape[1] // tr
    n_sample_chunks = lgs_ref.shape[1] // tr
    eidx = lax.broadcasted_iota(I32, (ne, tr), 0)

    def selected(onehots):
        return jnp.where(onehots[0] | onehots[1] | onehots[2] | onehots[3], 1.0, 0.0)

    def select_chunk(ref, first_chunk):
        def body(c, carry):
            chunk = first_chunk + c
            work = ref[:, pl.ds(pl.multiple_of(c * tr, tr), tr)]
            top_v, onehots = [], []
            for k in range(TOP_K):
                m = jnp.max(work, axis=0, keepdims=True)
                idx = jnp.min(jnp.where(work == m, eidx, ne), axis=0, keepdims=True)
                chosen[chunk, k:k + 1, :] = idx
                top_v.append(m)
                onehots.append(eidx == idx)
                work = jnp.where(onehots[k], -jnp.inf, work)
            es = [jnp.exp(v - top_v[0]) for v in top_v]
            denom = es[0] + es[1] + es[2] + es[3]
            cols = pl.ds(pl.multiple_of(chunk * tr, tr), tr)
            for k in range(TOP_K):
                gate_ref[k:k + 1, cols] = es[k] / denom
            chunk_counts = jnp.sum(selected(onehots), axis=1, keepdims=True)
            counts[...] = counts[...] + jnp.broadcast_to(chunk_counts, (ne, LANES))
            return carry
        return body

    counts[...] = jnp.zeros_like(counts)
    gate_ref[...] = jnp.zeros_like(gate_ref)
    lax.fori_loop(0, n_prompt_chunks, select_chunk(lgp_ref, 0), 0)
    lax.fori_loop(0, n_sample_chunks, select_chunk(lgs_ref, n_prompt_chunks), 0)

    total = counts[...]
    hi = jnp.floor(total * (1.0 / 256.0))
    lo = total - hi * 256.0
    r = lax.broadcasted_iota(I32, (ne, ne), 0)
    col = lax.broadcasted_iota(I32, (ne, ne), 1)
    lower = jnp.where(col < r, 1.0, 0.0).astype(BF16)
    start[...] = 256.0 * _dot(lower, hi.astype(BF16)) + _dot(lower, lo.astype(BF16))
    cnt_ref[...] = total
    counts[...] = jnp.zeros_like(counts)

    r = lax.broadcasted_iota(I32, (tr, tr), 0)
    col = lax.broadcasted_iota(I32, (tr, tr), 1)
    before[...] = jnp.where(r < col, 1.0, 0.0).astype(BF16)

    def place_chunk(chunk, carry):
        onehots = [eidx == chosen[chunk, k:k + 1, :] for k in range(TOP_K)]
        mask = selected(onehots)
        base = jnp.concatenate([counts[...] + start[...]] * reps, axis=1)
        rank = _dot(mask.astype(BF16), before[...]) + base
        counts[...] = counts[...] + jnp.broadcast_to(jnp.sum(mask, axis=1, keepdims=True), (ne, LANES))
        for k in range(TOP_K):
            d = jnp.sum(jnp.where(onehots[k], rank, 0.0), axis=0, keepdims=True).astype(I32)
            for j in range(reps):
                dest_ref[chunk * reps + j, k:k + 1, :] = d[:, j * LANES:(j + 1) * LANES]
        return carry

    lax.fori_loop(0, n_prompt_chunks + n_sample_chunks, place_chunk, 0)


def _route(lgt_p, lgt_s):
    ne, n_p = lgt_p.shape
    n_s = lgt_s.shape[1]
    tr = ROUTE_TILE
    return pl.pallas_call(
        _route_kernel,
        out_shape=(
            jax.ShapeDtypeStruct(((n_p + n_s) // LANES, TOP_K, LANES), I32),
            jax.ShapeDtypeStruct((SUBLANES, n_p + n_s), F32),
            jax.ShapeDtypeStruct((ne, LANES), F32),
        ),
        scratch_shapes=[pltpu.VMEM((ne, LANES), F32), pltpu.VMEM((ne, LANES), F32),
                        pltpu.VMEM((tr, tr), BF16), pltpu.VMEM(((n_p + n_s) // tr, TOP_K, tr), I32)],
        compiler_params=pltpu.CompilerParams(vmem_limit_bytes=VMEM_LIMIT),
        name="route",
    )(lgt_p, lgt_s)


def _group_metadata(counts, n_rows, tile):
    n_tiles = n_rows // tile
    n_steps = n_tiles + N_EXPERTS - 1
    ends = jnp.cumsum(counts)
    offs = jnp.concatenate([jnp.zeros((1,), I32), ends]).astype(I32)
    first_tile = offs[:-1] // tile
    last_tile = (ends - 1) // tile
    tiles_e = jnp.where(counts > 0, last_tile - first_tile + 1, 0)
    step_end = jnp.cumsum(tiles_e)
    step_start = step_end - tiles_e
    n_active = step_end[-1]
    s = jnp.minimum(jnp.arange(n_steps, dtype=I32), n_active - 1)
    owner = ((s[:, None] >= step_start[None, :]) & (s[:, None] < step_end[None, :])).astype(I32)
    gid = jnp.sum(owner * jnp.arange(N_EXPERTS, dtype=I32)[None, :], axis=1)
    tid = jnp.sum(owner * (first_tile - step_start)[None, :], axis=1) + s
    ids = jnp.arange(N_EXPERTS, dtype=I32)
    later = (ids[None, :] > ids[:, None]) & (counts[None, :] > 0)
    next_e = jnp.min(jnp.where(later, ids[None, :], N_EXPERTS), axis=1)
    next_e = jnp.where(next_e == N_EXPERTS, -1, next_e)
    nxt = jnp.sum(owner * next_e[None, :], axis=1)
    return gid, tid, nxt, offs, n_active.reshape(1).astype(I32), n_steps


SC_LANES = 16
SC_INDEX_BATCH = 128


def _invert(dest):
    r = dest.shape[0]
    mesh = plsc.VectorSubcoreMesh(core_axis_name="core", subcore_axis_name="subcore")
    per = r // mesh.num_subcores
    assert per * mesh.num_subcores == r and per % SC_INDEX_BATCH == 0

    @functools.partial(
        pl.kernel, mesh=mesh, out_type=jax.ShapeDtypeStruct((r,), I32),
        scratch_types=[pltpu.VMEM_SHARED((r,), I32), pltpu.VMEM((per,), I32), pltpu.VMEM((per,), I32)],
        compiler_params=pltpu.CompilerParams(needs_layout_passes=False),
        name="invert",
    )
    def invert(dest_hbm, inv_hbm, table, idx, ids):
        @pl.when(lax.axis_index("core") == 0)
        def _():
            base = lax.axis_index("subcore") * per
            pltpu.sync_copy(dest_hbm.at[pl.ds(base, per)], idx)
            lane = lax.iota(I32, SC_LANES)

            @pl.loop(0, per // SC_LANES)
            def _(i):
                p = base + i * SC_LANES + lane
                tok = lax.shift_right_logical(p, 9) * LANES + (p & (LANES - 1))
                slot = lax.shift_right_logical(p, 7) & (TOP_K - 1)
                ids[pl.ds(i * SC_LANES, SC_LANES)] = tok * TOP_K + slot

            @pl.loop(0, per // SC_INDEX_BATCH)
            def _(j):
                span = pl.ds(j * SC_INDEX_BATCH, SC_INDEX_BATCH)
                pltpu.sync_copy(ids.at[span], table.at[idx.at[span]])

            plsc.subcore_barrier()
            pltpu.sync_copy(table.at[pl.ds(base, per)], inv_hbm.at[pl.ds(base, per)])

    return invert(dest)


def _experts_kernel(gid_ref, tid_ref, nxt_ref, offs_ref, nact_ref,
                    dstp_ref, dst0_ref, src0_ref, src1_ref, src2_ref, bg_ref, bu_ref, bd_ref,
                    h2_hbm, wg_hbm, wu_hbm, wd_hbm, y4_hbm,
                    wg_b, wu_b, wd_b, wg_f, wu_f, wd_f, relay, rows, ybuf, sems, row_sems, y_sems):
    s = pl.program_id(0)
    tile_rows = rows.shape[1]
    tm = tile_rows // SUBLANES
    landing = ((wg_hbm, wg_f, wg_b), (wu_hbm, wu_f, wu_b), (wd_hbm, wd_f, wd_b))

    def fetch(e):
        for j, (hbm, land, _) in enumerate(landing):
            pltpu.make_async_copy(hbm.at[e], land, sems.at[j]).start()

    def row_in(idx_ref, into, r):
        src = h2_hbm.at[pl.ds(pl.multiple_of(idx_ref[r], SUBLANES), SUBLANES)]
        dst = rows.at[into, pl.ds(pl.multiple_of(r * SUBLANES, SUBLANES), SUBLANES)]
        return pltpu.make_async_copy(src, dst, row_sems.at[into])

    def row_out(idx_ref, frm, r):
        src = ybuf.at[frm, pl.ds(pl.multiple_of(r * SUBLANES, SUBLANES), SUBLANES)]
        dst = y4_hbm.at[pl.ds(pl.multiple_of(idx_ref[r], SUBLANES), SUBLANES)]
        return pltpu.make_async_copy(src, dst, y_sems.at[frm])

    def in_line(make, idx_ref, buf, first=0, count=None):
        for r in range(first, tm if count is None else first + count):
            make(idx_ref, buf, r).start(priority=r % 2)

    def in_loop(make, idx_ref, buf):
        def group(g, carry):
            for j in range(ISSUE_GROUP):
                make(idx_ref, buf, g * ISSUE_GROUP + j).start(priority=j % 2)
            return carry
        lax.fori_loop(0, tm // ISSUE_GROUP, group, 0)

    def wait_rows(buf):
        pltpu.make_async_copy(h2_hbm.at[pl.ds(0, tile_rows)], rows.at[buf], row_sems.at[buf]).wait()

    def wait_y(buf):
        pltpu.make_async_copy(ybuf.at[buf], y4_hbm.at[pl.ds(0, tile_rows)], y_sems.at[buf]).wait()

    @pl.when(s < nact_ref[0])
    def _():
        e = gid_ref[s]
        m = tid_ref[s]
        cur = lax.rem(m, 3)
        before = lax.rem(m + 2, 3)
        last = nact_ref[0] - 1
        new_tile = (s == 0) | (tid_ref[jnp.maximum(s - 1, 0)] != m)

        @pl.when(s == 0)
        def _():
            fetch(e)
            in_loop(row_in, src0_ref, 0)
            in_loop(row_in, src1_ref, 1)
            ybuf[2] = jnp.zeros(ybuf.shape[1:], F32)

        @pl.when((s == 0) | (gid_ref[jnp.maximum(s - 1, 0)] != e))
        def _():
            for j, (hbm, land, half) in enumerate(landing):
                pltpu.make_async_copy(hbm.at[e], land, sems.at[j]).wait()
                half[...] = land[...].astype(BF16)

            @pl.when(nxt_ref[s] >= 0)
            def _():
                fetch(nxt_ref[s])

        @pl.when(new_tile)
        def _():
            wait_rows(cur)

            @pl.when(m >= 2)
            def _():
                wait_y(cur)

        def ffn(load_x, after_gate=lambda: None):
            g = _dot(load_x(), wg_b[...]) + bg_ref[pl.ds(e, 1), :]
            after_gate()
            u = _dot(load_x(), wu_b[...]) + bu_ref[pl.ds(e, 1), :]
            g = jnp.minimum(g, SWIGLU_LIMIT)
            u = jnp.clip(u, -SWIGLU_LIMIT, SWIGLU_LIMIT)
            glu = g * jax.nn.sigmoid(SWIGLU_ALPHA * g)
            return _dot(((u + 1.0) * glu).astype(BF16), wd_b[...]) + bd_ref[pl.ds(e, 1), :]

        lo = offs_ref[e]
        hi = offs_ref[e + 1]
        whole_tile = (lo <= m * tm) & (hi >= (m + 1) * tm)

        def ffn_and_copies(load_x, first, count):
            pinned = count // 2
            y = ffn(load_x, lambda: in_line(row_in, src2_ref, before, first, pinned))
            in_line(row_in, src2_ref, before, first + pinned, count - pinned)
            in_line(row_out, dstp_ref, before, first, count)
            return y

        @pl.when(whole_tile)
        def _():
            y = ffn_and_copies(lambda: _load_token_tiles(rows.at[cur]).astype(BF16), 0, tm)
            _store_token_tiles(ybuf.at[cur], y)

        @pl.when(jnp.logical_not(whole_tile))
        def _():
            sub = relay.shape[0] // SUBLANES
            for j in range(tm // sub):
                first = m * tm + j * sub
                span = pl.ds(j * sub * SUBLANES, sub * SUBLANES)
                load_x = lambda span=span: _load_token_tiles(rows.at[cur, span]).astype(BF16)

                def put(y, first=first, span=span):
                    _store_token_tiles(relay, y)
                    row = first + lax.shift_right_logical(lax.broadcasted_iota(I32, relay.shape, 0), 3)
                    pltpu.store(ybuf.at[cur, span], relay[...], mask=(row >= lo) & (row < hi))

                evaluate = (lo < first + sub) & (hi > first)
                owns_end = hi >= first + sub

                @pl.when(evaluate & owns_end)
                def _():
                    put(ffn_and_copies(load_x, j * sub, sub))

                @pl.when(evaluate & jnp.logical_not(owns_end))
                def _():
                    put(ffn(load_x))

        @pl.when(s == last)
        def _():
            in_loop(row_out, dst0_ref, cur)
            wait_rows(lax.rem(m + 1, 3))
            wait_rows(before)
            for buf in range(3):
                wait_y(buf)


def _experts(gid, tid, nxt, offs, nact, n_steps, inv, h2, wg, bg, wu, bu, wd, bd):
    ne, d, f = wg.shape
    tm = EXPERT_TILE
    n_tokens = h2.shape[0] // SUBLANES
    n_tiles = inv.shape[0] // tm
    assert n_tiles >= 2
    slot_rows = (n_tokens + tm // TOP_K) * SUBLANES
    inv_ext = jnp.concatenate([n_tokens * TOP_K + jnp.arange(tm, dtype=I32), inv])
    tok = lax.shift_right_logical(inv_ext, TOP_K.bit_length() - 1)
    src_row = tok * SUBLANES
    dst_row = (inv_ext & (TOP_K - 1)) * slot_rows + src_row
    whole = lambda s, gid, tid, nxt, offs, nact: (0, 0)

    def order_of(k):
        return pl.BlockSpec(
            (tm,), lambda s, gid, tid, nxt, offs, nact: (jnp.minimum(tid[s] + k, n_tiles - 1) + 1,),
            memory_space=pltpu.SMEM)

    grid_spec = pltpu.PrefetchScalarGridSpec(
        num_scalar_prefetch=5,
        grid=(n_steps,),
        in_specs=[
            order_of(-1), order_of(0), order_of(0), order_of(1), order_of(2),
            pl.BlockSpec((ne, f), whole),
            pl.BlockSpec((ne, f), whole),
            pl.BlockSpec((ne, d), whole),
            pl.BlockSpec(memory_space=pl.ANY),
            pl.BlockSpec(memory_space=pl.ANY),
            pl.BlockSpec(memory_space=pl.ANY),
            pl.BlockSpec(memory_space=pl.ANY),
        ],
        out_specs=pl.BlockSpec(memory_space=pl.ANY),
        scratch_shapes=[
            pltpu.VMEM((d, f), BF16), pltpu.VMEM((d, f), BF16), pltpu.VMEM((f, d), BF16),
            pltpu.VMEM((d, f), F32), pltpu.VMEM((d, f), F32), pltpu.VMEM((f, d), F32),
            pltpu.VMEM((EXPERT_SUBTILE * SUBLANES, LANES), F32),
            pltpu.VMEM((3, tm * SUBLANES, LANES), F32),
            pltpu.VMEM((3, tm * SUBLANES, LANES), F32),
            pltpu.SemaphoreType.DMA((3,)),
            pltpu.SemaphoreType.DMA((3,)),
            pltpu.SemaphoreType.DMA((3,)),
        ],
    )
    y4 = pl.pallas_call(
        _experts_kernel,
        out_shape=jax.ShapeDtypeStruct((TOP_K * slot_rows, LANES), F32),
        grid_spec=grid_spec,
        compiler_params=pltpu.CompilerParams(
            dimension_semantics=("arbitrary",), vmem_limit_bytes=VMEM_LIMIT),
        name="experts",
    )(gid, tid, nxt, offs, nact, dst_row, dst_row, src_row, src_row, src_row, bg, bu, bd, h2, wg, wu, wd)
    return y4.reshape(TOP_K, slot_rows, LANES)


def _combine_kernel(gate_ref, y4_ref, x1p_ref, x1s_ref, modp_ref, g2s_ref, fn_ref, outp_ref, outs_ref,
                    *, n_prompt_tiles):
    i = pl.program_id(0)
    tm = x1p_ref.shape[0]
    gates = gate_ref[...].T
    cols = []
    for c in range(SUBLANES):
        acc = gates[:, 0:1] * y4_ref[0, pl.ds(c, tm, stride=SUBLANES), :]
        for k in range(1, TOP_K):
            acc = acc + gates[:, k:k + 1] * y4_ref[k, pl.ds(c, tm, stride=SUBLANES), :]
        cols.append(acc)
    ffn = jnp.concatenate(cols, axis=-1)

    @pl.when(i < n_prompt_tiles)
    def _():
        g2 = modp_ref[...][5:6, :]
        outp_ref[...] = _rmsnorm(x1p_ref[...] + g2 * ffn, fn_ref[...])

    @pl.when(i >= n_prompt_tiles)
    def _():
        g2 = jnp.concatenate([g2s_ref[...]] * (tm // g2s_ref.shape[0]), axis=0)
        outs_ref[...] = _rmsnorm(x1s_ref[...] + g2 * ffn, fn_ref[...])


def _combine(gates, y4, x1_p, x1_s, mod_p, mod, n_batch_s, final_norm, tokens_per_seq):
    n_p, d = x1_p.shape
    n_s = x1_s.shape[0]
    tm = COMBINE_TILE
    assert tm % n_batch_s == 0 and n_batch_s % SUBLANES == 0
    npt, nst = n_p // tm, n_s // tm
    tiles_per_seq = tokens_per_seq // tm
    pmap = lambda i: (jnp.minimum(i, npt - 1), 0)
    smap = lambda i: (jnp.maximum(i - npt, 0), 0)
    return pl.pallas_call(
        functools.partial(_combine_kernel, n_prompt_tiles=npt),
        out_shape=(jax.ShapeDtypeStruct((n_p, d), F32), jax.ShapeDtypeStruct((n_s, d), F32)),
        grid=(npt + nst,),
        in_specs=[
            pl.BlockSpec((SUBLANES, tm), lambda i: (0, i)),
            pl.BlockSpec((TOP_K, tm * SUBLANES, LANES), lambda i: (0, i, 0)),
            pl.BlockSpec((tm, d), pmap),
            pl.BlockSpec((tm, d), smap),
            pl.BlockSpec((None, 6, d), lambda i: (jnp.minimum(i, npt - 1) // tiles_per_seq, 0, 0)),
            pl.BlockSpec((n_batch_s, d), lambda i: (0, 5)),
            pl.BlockSpec((1, d), lambda i: (0, 0)),
        ],
        out_specs=(pl.BlockSpec((tm, d), pmap), pl.BlockSpec((tm, d), smap)),
        compiler_params=pltpu.CompilerParams(
            dimension_semantics=("arbitrary",), vmem_limit_bytes=VMEM_LIMIT),
        name="combine",
    )(gates, y4, x1_p, x1_s, mod_p, mod, final_norm)


def kernel(x_prompt, x_sample, state_pool, state_conv, c_prompt, c_sample, norm1, norm2, w_ada, b_ada,
           w_in, w_pool, pool_scale, w_conv, w_out, w_router, b_router, w_gate, b_gate, w_up, b_up,
           w_down, b_down, final_norm):
    depth = norm1.shape[0]
    assert depth == 1, "single-layer step"
    bp, tp, d = x_prompt.shape
    bs, ts, _ = x_sample.shape
    dp = state_pool.shape[-1]
    n_hist = state_pool.shape[2]
    n_chist = state_conv.shape[2]
    n_p, n_s = bp * tp, bs * ts
    assert d == SUBLANES * LANES, "token-tile layout assumes one vreg tile per token row"
    assert tp % TOKEN_TILE == 0 and n_s == TOKEN_TILE, "the sample group fills exactly one token tile"
    assert tp % COMBINE_TILE == 0 and n_s % COMBINE_TILE == 0
    assert (n_p + n_s) % ROUTE_TILE == 0 and ((n_p + n_s) * TOP_K) % EXPERT_TILE == 0

    l = 0
    n1 = norm1[l].reshape(1, d)
    n2 = norm2[l].reshape(1, d)
    pscale = pool_scale[l].reshape(1, dp)
    w_r = w_router[l].T.astype(BF16)
    b_r = b_router[l].reshape(N_EXPERTS, 1)

    mod = _adaln(jnp.concatenate([c_sample, c_prompt], axis=0), w_ada[l], b_ada[l])
    mod_p = mod[bs:].reshape(bp, 6, d)

    xs_tm = jnp.transpose(x_sample, (1, 0, 2)).reshape(n_s, d)
    ps_tm = jnp.transpose(state_pool[l], (1, 0, 2))
    cs_tm = jnp.transpose(state_conv[l], (1, 0, 2))
    x1_s, h2_s, lg_s, newp_tm, newc_tm, w_in_b, w_pool_b, w_out_b = _mixer_sample(
        xs_tm, mod, ps_tm, cs_tm, n1, n2, w_in[l], w_pool[l], pscale, w_conv[l], w_out[l], w_r, b_r, ts)

    x1_p, h2, lg_p, u_tail, v_tail = _mixer_prompt(
        x_prompt, mod_p, n1, n2, w_in_b, w_pool_b, pscale, w_conv[l], w_out_b, w_r, b_r, h2_s)

    dest, gates, counts_f = _route(lg_p, lg_s)
    counts = counts_f[:, 0].astype(I32)
    n_rows = (n_p + n_s) * TOP_K
    gid, tid, nxt, offs, nact, n_steps = _group_metadata(counts, n_rows, EXPERT_TILE)

    dest = dest.reshape(-1)
    inv = _invert(dest)
    y4 = _experts(gid, tid, nxt, offs, nact, n_steps, inv, h2,
                  w_gate[l], b_gate[l], w_up[l], b_up[l], w_down[l], b_down[l])

    y_p, y_s = _combine(gates, y4, x1_p, x1_s, mod_p, mod, bs, final_norm.reshape(1, d), tp)

    y_prompt = y_p.reshape(bp, tp, d)
    y_sample = jnp.transpose(y_s.reshape(ts, bs, d), (1, 0, 2))
    new_pool_prompt = u_tail[:, POOL_HALO - n_hist:, :][None]
    new_conv_prompt = v_tail[:, CONV_HALO - n_chist:, :][None]
    new_pool_sample = jnp.transpose(newp_tm, (1, 0, 2))[None]
    new_conv_sample = jnp.transpose(newc_tm, (1, 0, 2))[None]
    return (y_prompt, y_sample, new_pool_prompt, new_conv_prompt, new_pool_sample, new_conv_sample)
```

```python
import functools

import jax
import jax.numpy as jnp
from jax import lax
from jax.experimental import pallas as pl
from jax.experimental.pallas import tpu as pltpu
from jax.experimental.pallas import tpu_sc as plsc

F32 = jnp.float32
BF16 = jnp.bfloat16
I32 = jnp.int32

POOL_WINDOWS = (2, 4, 8, 16)
POOL_HALO = 16
CONV_TAPS = 3
CONV_HALO = 8
N_EXPERTS = 32
TOP_K = 4
SWIGLU_LIMIT = 7.0
SWIGLU_ALPHA = 1.702
EPS = 1e-5
PAST_LEN = 16384

LANES = 128
SUBLANES = 8

TOKEN_TILE = 512
ROUTE_TILE = 512
COMBINE_TILE = 512
EXPERT_TILE = 512
EXPERT_SUBTILE = 128
VMEM_LIMIT = 56 * 1024 * 1024


def _rmsnorm(x, g):
    ms = jnp.mean(x * x, axis=-1, keepdims=True)
    return x * lax.rsqrt(ms + EPS) * g


def _dot(a, b):
    return jnp.dot(a, b, preferred_element_type=F32)


def _store_token_tiles(ref, val):
    rows = val.shape[0]
    for c in range(SUBLANES):
        ref[pl.ds(c, rows, stride=SUBLANES), :] = val[:, c * LANES:(c + 1) * LANES]


def _load_token_tiles(ref):
    rows = ref.shape[0] // SUBLANES
    return jnp.concatenate([ref[pl.ds(c, rows, stride=SUBLANES), :] for c in range(SUBLANES)], axis=-1)


ISSUE_GROUP = 4


def _adaln_kernel(c_ref, w_ref, b_ref, o_ref):
    c = c_ref[...]
    s = c * jax.nn.sigmoid(c)
    o_ref[...] = _dot(s.astype(BF16), w_ref[...].astype(BF16)) + b_ref[...]


def _adaln(c, w_ada, b_ada):
    rows, d = c.shape
    n = w_ada.shape[1]
    tn = 512
    return pl.pallas_call(
        _adaln_kernel,
        out_shape=jax.ShapeDtypeStruct((rows, n), F32),
        grid=(n // tn,),
        in_specs=[
            pl.BlockSpec((rows, d), lambda j: (0, 0)),
            pl.BlockSpec((d, tn), lambda j: (0, j)),
            pl.BlockSpec((1, tn), lambda j: (0, j)),
        ],
        out_specs=pl.BlockSpec((rows, tn), lambda j: (0, j)),
        compiler_params=pltpu.CompilerParams(
            dimension_semantics=("arbitrary",), vmem_limit_bytes=VMEM_LIMIT),
        name="adaln",
    )(c, w_ada, b_ada.reshape(1, n))


def _mix_tail(x, pool_in, conv_out, g1, sc2, sh2, n2, wpool_ref, pscale, wout, wr, br):
    gw = pool_in.shape[1] // len(POOL_WINDOWS)
    mixed = [_dot(pool_in[:, g * gw:(g + 1) * gw].astype(BF16), wpool_ref[g])
             for g in range(len(POOL_WINDOWS))]
    pool_out = jnp.concatenate(mixed, axis=-1) * pscale
    mix_in = jnp.concatenate([pool_out, conv_out], axis=-1).astype(BF16)
    x1 = x + g1 * _dot(mix_in, wout)
    h2 = _rmsnorm(x1, n2) * (1.0 + sc2) + sh2
    logits_t = lax.dot_general(wr, h2.astype(BF16), (((1,), (1,)), ((), ())),
                               preferred_element_type=F32) + br
    return x1, h2, logits_t


def _mixer_prompt_kernel(x_ref, mod_ref, n1_ref, n2_ref, win_ref, wpool_ref, pscale_ref, wconv_ref,
                         wout_ref, wr_ref, br_ref, h2s_ref,
                         x1_ref, h2_ref, lg_ref, upool_ref, vconv_ref, ubuf, vbuf, *, tiles_per_seq):
    i = pl.program_id(0)

    @pl.when(i < pl.num_programs(0) - 1)
    def _():
        _mixer_prompt_tile(x_ref, mod_ref, n1_ref, n2_ref, win_ref, wpool_ref, pscale_ref, wconv_ref,
                           wout_ref, wr_ref, br_ref, x1_ref, h2_ref, lg_ref, upool_ref, vconv_ref,
                           ubuf, vbuf, lax.rem(i, tiles_per_seq), tiles_per_seq)

    @pl.when(i == pl.num_programs(0) - 1)
    def _():
        h2_ref[...] = h2s_ref[...]


def _mixer_prompt_tile(x_ref, mod_ref, n1_ref, n2_ref, win_ref, wpool_ref, pscale_ref, wconv_ref,
                       wout_ref, wr_ref, br_ref, x1_ref, h2_ref, lg_ref, upool_ref, vconv_ref,
                       ubuf, vbuf, t, tiles_per_seq):
    tt = x_ref.shape[0]
    dp = ubuf.shape[1]
    gw = dp // len(POOL_WINDOWS)

    @pl.when(t == 0)
    def _():
        ubuf[0:POOL_HALO, :] = jnp.zeros((POOL_HALO, dp), F32)
        vbuf[0:CONV_HALO, :] = jnp.zeros((CONV_HALO, dp), F32)

    x = x_ref[...]
    mod = mod_ref[...]
    sh1, sc1, g1, sh2, sc2, _ = [mod[i:i + 1, :] for i in range(6)]
    h = _rmsnorm(x, n1_ref[...]) * (1.0 + sc1) + sh1
    z = _dot(h.astype(BF16), win_ref[...])
    u, gate_b, gate_c, val = [z[:, i * dp:(i + 1) * dp] for i in range(4)]

    ubuf[POOL_HALO:POOL_HALO + tt, :] = u
    pos = lax.broadcasted_iota(I32, (tt, gw), 0) + t * tt
    pooled = []
    for g, w in enumerate(POOL_WINDOWS):
        cols = slice(g * gw, (g + 1) * gw)
        acc = u[:, cols]
        for j in range(1, w):
            acc = acc + ubuf[POOL_HALO - j:POOL_HALO - j + tt, cols]
        cnt = jnp.minimum(pos + 1, w).astype(F32)
        pooled.append(acc / cnt - u[:, cols])
    pool_in = jnp.concatenate(pooled, axis=-1)

    v = gate_c * val
    vbuf[CONV_HALO:CONV_HALO + tt, :] = v
    wc = wconv_ref[...]
    y = (wc[0:1, :] * vbuf[CONV_HALO - 2:CONV_HALO - 2 + tt, :]
         + wc[1:2, :] * vbuf[CONV_HALO - 1:CONV_HALO - 1 + tt, :]
         + wc[2:3, :] * v)
    conv_out = gate_b * y

    x1, h2, logits = _mix_tail(x, pool_in, conv_out, g1, sc2, sh2, n2_ref[...], wpool_ref,
                               pscale_ref[...], wout_ref[...], wr_ref[...], br_ref[...])
    x1_ref[...] = x1
    _store_token_tiles(h2_ref, h2)
    lg_ref[...] = logits

    ubuf[0:POOL_HALO, :] = ubuf[tt:tt + POOL_HALO, :]
    vbuf[0:CONV_HALO, :] = vbuf[tt:tt + CONV_HALO, :]

    @pl.when(t == tiles_per_seq - 1)
    def _():
        upool_ref[...] = ubuf[0:POOL_HALO, :]
        vconv_ref[...] = vbuf[0:CONV_HALO, :]


def _mixer_prompt(x, mod_p, n1, n2, w_in, w_pool, pscale, w_conv, w_out, w_r, b_r, h2_s):
    b, t, d = x.shape
    dp = w_pool.shape[0] * w_pool.shape[1]
    tt = min(TOKEN_TILE, t)
    nt = t // tt
    n_real = b * nt
    assert h2_s.shape[0] == tt * SUBLANES
    seq = lambda i: jnp.minimum(i, n_real - 1) // nt
    tile = lambda i: jnp.minimum(i, n_real - 1)
    const2 = lambda i: (0, 0)
    const3 = lambda i: (0, 0, 0)
    return pl.pallas_call(
        functools.partial(_mixer_prompt_kernel, tiles_per_seq=nt),
        out_shape=(
            jax.ShapeDtypeStruct((b * t, d), F32),
            jax.ShapeDtypeStruct(((n_real + 1) * tt * SUBLANES, LANES), F32),
            jax.ShapeDtypeStruct((N_EXPERTS, b * t), F32),
            jax.ShapeDtypeStruct((b, POOL_HALO, dp), F32),
            jax.ShapeDtypeStruct((b, CONV_HALO, dp), F32),
        ),
        grid=(n_real + 1,),
        in_specs=[
            pl.BlockSpec((None, tt, d), lambda i: (seq(i), tile(i) % nt, 0)),
            pl.BlockSpec((None, 6, d), lambda i: (seq(i), 0, 0)),
            pl.BlockSpec((1, d), const2),
            pl.BlockSpec((1, d), const2),
            pl.BlockSpec(w_in.shape, const2),
            pl.BlockSpec(w_pool.shape, const3),
            pl.BlockSpec((1, dp), const2),
            pl.BlockSpec(w_conv.shape, const2),
            pl.BlockSpec(w_out.shape, const2),
            pl.BlockSpec(w_r.shape, const2),
            pl.BlockSpec((N_EXPERTS, 1), const2),
            pl.BlockSpec(h2_s.shape, const2),
        ],
        out_specs=(
            pl.BlockSpec((tt, d), lambda i: (tile(i), 0)),
            pl.BlockSpec((tt * SUBLANES, LANES), lambda i: (i, 0)),
            pl.BlockSpec((N_EXPERTS, tt), lambda i: (0, tile(i))),
            pl.BlockSpec((None, POOL_HALO, dp), lambda i: (seq(i), 0, 0)),
            pl.BlockSpec((None, CONV_HALO, dp), lambda i: (seq(i), 0, 0)),
        ),
        scratch_shapes=[
            pltpu.VMEM((POOL_HALO + tt, dp), F32),
            pltpu.VMEM((CONV_HALO + tt, dp), F32),
        ],
        compiler_params=pltpu.CompilerParams(
            dimension_semantics=("arbitrary",), vmem_limit_bytes=VMEM_LIMIT),
        name="mixer_prompt",
    )(x, mod_p, n1, n2, w_in, w_pool, pscale, w_conv, w_out, w_r, b_r, h2_s)


def _mixer_sample_kernel(x_ref, mod_ref, pstate_ref, cstate_ref, n1_ref, n2_ref, win_f32, wpool_f32,
                         pscale_ref, wconv_ref, wout_f32, wr_ref, br_ref,
                         x1_ref, h2_ref, lg_ref, newp_ref, newc_ref, win_ref, wpool_ref, wout_ref,
                         *, steps):
    win_ref[...] = win_f32[...].astype(BF16)
    wpool_ref[...] = wpool_f32[...].astype(BF16)
    wout_ref[...] = wout_f32[...].astype(BF16)
    nb = pstate_ref.shape[1]
    d = x_ref.shape[1]
    dp = pstate_ref.shape[2]
    gw = dp // len(POOL_WINDOWS)
    n_hist = pstate_ref.shape[0]
    n_chist = cstate_ref.shape[0]

    x = x_ref[...]
    mod = mod_ref[0:nb, :]
    rep = lambda a: jnp.concatenate([a] * steps, axis=0)
    sh1, sc1, g1, sh2, sc2, _ = [rep(mod[:, i * d:(i + 1) * d]) for i in range(6)]
    h = _rmsnorm(x, n1_ref[...]) * (1.0 + sc1) + sh1
    z = _dot(h.astype(BF16), win_ref[...])
    u, gate_b, gate_c, val = [z[:, i * dp:(i + 1) * dp] for i in range(4)]

    ext = [pstate_ref[i] for i in range(n_hist)] + [u[s * nb:(s + 1) * nb, :] for s in range(steps)]
    pooled_steps = []
    for s in range(steps):
        groups = []
        for g, w in enumerate(POOL_WINDOWS):
            cols = slice(g * gw, (g + 1) * gw)
            acc = ext[n_hist + s][:, cols]
            for j in range(1, w):
                acc = acc + ext[n_hist + s - j][:, cols]
            cnt = float(min(PAST_LEN + s + 1, w))
            groups.append(acc / cnt - ext[n_hist + s][:, cols])
        pooled_steps.append(jnp.concatenate(groups, axis=-1))
    pool_in = jnp.concatenate(pooled_steps, axis=0)

    v = gate_c * val
    vext = [cstate_ref[i] for i in range(n_chist)] + [v[s * nb:(s + 1) * nb, :] for s in range(steps)]
    wc = wconv_ref[...]
    y = jnp.concatenate(
        [wc[0:1, :] * vext[s] + wc[1:2, :] * vext[s + 1] + wc[2:3, :] * vext[s + 2] for s in range(steps)],
        axis=0)
    conv_out = gate_b * y

    x1, h2, logits = _mix_tail(x, pool_in, conv_out, g1, sc2, sh2, n2_ref[...], wpool_ref,
                               pscale_ref[...], wout_ref[...], wr_ref[...], br_ref[...])
    x1_ref[...] = x1
    _store_token_tiles(h2_ref, h2)
    lg_ref[...] = logits
    for i in range(n_hist):
        newp_ref[i] = ext[steps + i]
    for i in range(n_chist):
        newc_ref[i] = vext[steps + i]


def _mixer_sample(x_tm, mod_s, pstate_tm, cstate_tm, n1, n2, w_in, w_pool, pscale, w_conv, w_out, w_r, b_r,
                  steps):
    rows, d = x_tm.shape
    return pl.pallas_call(
        functools.partial(_mixer_sample_kernel, steps=steps),
        out_shape=(
            jax.ShapeDtypeStruct((rows, d), F32),
            jax.ShapeDtypeStruct((rows * SUBLANES, LANES), F32),
            jax.ShapeDtypeStruct((N_EXPERTS, rows), F32),
            jax.ShapeDtypeStruct(pstate_tm.shape, F32),
            jax.ShapeDtypeStruct(cstate_tm.shape, F32),
            jax.ShapeDtypeStruct(w_in.shape, BF16),
            jax.ShapeDtypeStruct(w_pool.shape, BF16),
            jax.ShapeDtypeStruct(w_out.shape, BF16),
        ),
        compiler_params=pltpu.CompilerParams(vmem_limit_bytes=VMEM_LIMIT),
        name="mixer_sample",
    )(x_tm, mod_s, pstate_tm, cstate_tm, n1, n2, w_in, w_pool, pscale, w_conv, w_out, w_r, b_r)


def _route_kernel(lgp_ref, lgs_ref, dest_ref, gate_ref, cnt_ref, counts, start, before, chosen):
    ne = lgp_ref.shape[0]
    tr = before.shape[0]
    reps = tr // LANES
    n_prompt_chunks = lgp_ref.shape[1] // tr
    n_sample_chunks = lgs_ref.shape[1] // tr
    eidx = lax.broadcasted_iota(I32, (ne, tr), 0)

    def selected(onehots):
        return jnp.where(onehots[0] | onehots[1] | onehots[2] | onehots[3], 1.0, 0.0)

    def select_chunk(ref, first_chunk):
        def body(c, carry):
            chunk = first_chunk + c
            work = ref[:, pl.ds(pl.multiple_of(c * tr, tr), tr)]
            top_v, onehots = [], []
            for k in range(TOP_K):
                m = jnp.max(work, axis=0, keepdims=True)
                idx = jnp.min(jnp.where(work == m, eidx, ne), axis=0, keepdims=True)
                chosen[chunk, k:k + 1, :] = idx
                top_v.append(m)
                onehots.append(eidx == idx)
                work = jnp.where(onehots[k], -jnp.inf, work)
            es = [jnp.exp(v - top_v[0]) for v in top_v]
            denom = es[0] + es[1] + es[2] + es[3]
            cols = pl.ds(pl.multiple_of(chunk * tr, tr), tr)
            for k in range(TOP_K):
                gate_ref[k:k + 1, cols] = es[k] / denom
            chunk_counts = jnp.sum(selected(onehots), axis=1, keepdims=True)
            counts[...] = counts[...] + jnp.broadcast_to(chunk_counts, (ne, LANES))
            return carry
        return body

    counts[...] = jnp.zeros_like(counts)
    gate_ref[...] = jnp.zeros_like(gate_ref)
    lax.fori_loop(0, n_prompt_chunks, select_chunk(lgp_ref, 0), 0)
    lax.fori_loop(0, n_sample_chunks, select_chunk(lgs_ref, n_prompt_chunks), 0)

    total = counts[...]
    hi = jnp.floor(total * (1.0 / 256.0))
    lo = total - hi * 256.0
    r = lax.broadcasted_iota(I32, (ne, ne), 0)
    col = lax.broadcasted_iota(I32, (ne, ne), 1)
    lower = jnp.where(col < r, 1.0, 0.0).astype(BF16)
    start[...] = 256.0 * _dot(lower, hi.astype(BF16)) + _dot(lower, lo.astype(BF16))
    cnt_ref[...] = total
    counts[...] = jnp.zeros_like(counts)

    r = lax.broadcasted_iota(I32, (tr, tr), 0)
    col = lax.broadcasted_iota(I32, (tr, tr), 1)
    before[...] = jnp.where(r < col, 1.0, 0.0).astype(BF16)

    def place_chunk(chunk, carry):
        onehots = [eidx == chosen[chunk, k:k + 1, :] for k in range(TOP_K)]
        mask = selected(onehots)
        base = jnp.concatenate([counts[...] + start[...]] * reps, axis=1)
        rank = _dot(mask.astype(BF16), before[...]) + base
        counts[...] = counts[...] + jnp.broadcast_to(jnp.sum(mask, axis=1, keepdims=True), (ne, LANES))
        for k in range(TOP_K):
            d = jnp.sum(jnp.where(onehots[k], rank, 0.0), axis=0, keepdims=True).astype(I32)
            for j in range(reps):
                dest_ref[chunk * reps + j, k:k + 1, :] = d[:, j * LANES:(j + 1) * LANES]
        return carry

    lax.fori_loop(0, n_prompt_chunks + n_sample_chunks, place_chunk, 0)


def _route(lgt_p, lgt_s):
    ne, n_p = lgt_p.shape
    n_s = lgt_s.shape[1]
    tr = ROUTE_TILE
    return pl.pallas_call(
        _route_kernel,
        out_shape=(
            jax.ShapeDtypeStruct(((n_p + n_s) // LANES, TOP_K, LANES), I32),
            jax.ShapeDtypeStruct((SUBLANES, n_p + n_s), F32),
            jax.ShapeDtypeStruct((ne, LANES), F32),
        ),
        scratch_shapes=[pltpu.VMEM((ne, LANES), F32), pltpu.VMEM((ne, LANES), F32),
                        pltpu.VMEM((tr, tr), BF16), pltpu.VMEM(((n_p + n_s) // tr, TOP_K, tr), I32)],
        compiler_params=pltpu.CompilerParams(vmem_limit_bytes=VMEM_LIMIT),
        name="route",
    )(lgt_p, lgt_s)


def _group_metadata(counts, n_rows, tile):
    n_tiles = n_rows // tile
    n_steps = n_tiles + N_EXPERTS - 1
    ends = jnp.cumsum(counts)
    offs = jnp.concatenate([jnp.zeros((1,), I32), ends]).astype(I32)
    first_tile = offs[:-1] // tile
    last_tile = (ends - 1) // tile
    tiles_e = jnp.where(counts > 0, last_tile - first_tile + 1, 0)
    step_end = jnp.cumsum(tiles_e)
    step_start = step_end - tiles_e
    n_active = step_end[-1]
    s = jnp.minimum(jnp.arange(n_steps, dtype=I32), n_active - 1)
    owner = ((s[:, None] >= step_start[None, :]) & (s[:, None] < step_end[None, :])).astype(I32)
    gid = jnp.sum(owner * jnp.arange(N_EXPERTS, dtype=I32)[None, :], axis=1)
    tid = jnp.sum(owner * (first_tile - step_start)[None, :], axis=1) + s
    ids = jnp.arange(N_EXPERTS, dtype=I32)
    later = (ids[None, :] > ids[:, None]) & (counts[None, :] > 0)
    next_e = jnp.min(jnp.where(later, ids[None, :], N_EXPERTS), axis=1)
    next_e = jnp.where(next_e == N_EXPERTS, -1, next_e)
    nxt = jnp.sum(owner * next_e[None, :], axis=1)
    return gid, tid, nxt, offs, n_active.reshape(1).astype(I32), n_steps


SC_LANES = 16
SC_INDEX_BATCH = 128


def _invert(dest):
    r = dest.shape[0]
    mesh = plsc.VectorSubcoreMesh(core_axis_name="core", subcore_axis_name="subcore")
    per = r // mesh.num_subcores
    assert per * mesh.num_subcores == r and per % SC_INDEX_BATCH == 0

    @functools.partial(
        pl.kernel, mesh=mesh, out_type=jax.ShapeDtypeStruct((r,), I32),
        scratch_types=[pltpu.VMEM_SHARED((r,), I32), pltpu.VMEM((per,), I32), pltpu.VMEM((per,), I32)],
        compiler_params=pltpu.CompilerParams(needs_layout_passes=False),
        name="invert",
    )
    def invert(dest_hbm, inv_hbm, table, idx, ids):
        @pl.when(lax.axis_index("core") == 0)
        def _():
            base = lax.axis_index("subcore") * per
            pltpu.sync_copy(dest_hbm.at[pl.ds(base, per)], idx)
            lane = lax.iota(I32, SC_LANES)

            @pl.loop(0, per // SC_LANES)
            def _(i):
                p = base + i * SC_LANES + lane
                tok = lax.shift_right_logical(p, 9) * LANES + (p & (LANES - 1))
                slot = lax.shift_right_logical(p, 7) & (TOP_K - 1)
                ids[pl.ds(i * SC_LANES, SC_LANES)] = tok * TOP_K + slot

            @pl.loop(0, per // SC_INDEX_BATCH)
            def _(j):
                span = pl.ds(j * SC_INDEX_BATCH, SC_INDEX_BATCH)
                pltpu.sync_copy(ids.at[span], table.at[idx.at[span]])

            plsc.subcore_barrier()
            pltpu.sync_copy(table.at[pl.ds(base, per)], inv_hbm.at[pl.ds(base, per)])

    return invert(dest)


def _experts_kernel(gid_ref, tid_ref, nxt_ref, offs_ref, nact_ref,
                    dstp_ref, dst0_ref, src0_ref, src1_ref, src2_ref, bg_ref, bu_ref, bd_ref,
                    h2_hbm, wg_hbm, wu_hbm, wd_hbm, y4_hbm,
                    wg_b, wu_b, wd_b, wg_f, wu_f, wd_f, relay, rows, ybuf, sems, row_sems, y_sems):
    s = pl.program_id(0)
    tile_rows = rows.shape[1]
    tm = tile_rows // SUBLANES
    landing = ((wg_hbm, wg_f, wg_b), (wu_hbm, wu_f, wu_b), (wd_hbm, wd_f, wd_b))

    def fetch(e):
        for j, (hbm, land, _) in enumerate(landing):
            pltpu.make_async_copy(hbm.at[e], land, sems.at[j]).start()

    def row_in(idx_ref, into, r):
        src = h2_hbm.at[pl.ds(pl.multiple_of(idx_ref[r], SUBLANES), SUBLANES)]
        dst = rows.at[into, pl.ds(pl.multiple_of(r * SUBLANES, SUBLANES), SUBLANES)]
        return pltpu.make_async_copy(src, dst, row_sems.at[into])

    def row_out(idx_ref, frm, r):
        src = ybuf.at[frm, pl.ds(pl.multiple_of(r * SUBLANES, SUBLANES), SUBLANES)]
        dst = y4_hbm.at[pl.ds(pl.multiple_of(idx_ref[r], SUBLANES), SUBLANES)]
        return pltpu.make_async_copy(src, dst, y_sems.at[frm])

    def in_line(make, idx_ref, buf, first=0, count=None):
        for r in range(first, tm if count is None else first + count):
            make(idx_ref, buf, r).start(priority=r % 2)

    def in_loop(make, idx_ref, buf):
        def group(g, carry):
            for j in range(ISSUE_GROUP):
                make(idx_ref, buf, g * ISSUE_GROUP + j).start(priority=j % 2)
            return carry
        lax.fori_loop(0, tm // ISSUE_GROUP, group, 0)

    def wait_rows(buf):
        pltpu.make_async_copy(h2_hbm.at[pl.ds(0, tile_rows)], rows.at[buf], row_sems.at[buf]).wait()

    def wait_y(buf):
        pltpu.make_async_copy(ybuf.at[buf], y4_hbm.at[pl.ds(0, tile_rows)], y_sems.at[buf]).wait()

    @pl.when(s < nact_ref[0])
    def _():
        e = gid_ref[s]
        m = tid_ref[s]
        cur = lax.rem(m, 3)
        before = lax.rem(m + 2, 3)
        last = nact_ref[0] - 1
        new_tile = (s == 0) | (tid_ref[jnp.maximum(s - 1, 0)] != m)

        @pl.when(s == 0)
        def _():
            fetch(e)
            in_loop(row_in, src0_ref, 0)
            in_loop(row_in, src1_ref, 1)
            ybuf[2] = jnp.zeros(ybuf.shape[1:], F32)

        @pl.when((s == 0) | (gid_ref[jnp.maximum(s - 1, 0)] != e))
        def _():
            for j, (hbm, land, half) in enumerate(landing):
                pltpu.make_async_copy(hbm.at[e], land, sems.at[j]).wait()
                half[...] = land[...].astype(BF16)

            @pl.when(nxt_ref[s] >= 0)
            def _():
                fetch(nxt_ref[s])

        @pl.when(new_tile)
        def _():
            wait_rows(cur)

            @pl.when(m >= 2)
            def _():
                wait_y(cur)

        def ffn(load_x, after_gate=lambda: None):
            g = _dot(load_x(), wg_b[...]) + bg_ref[pl.ds(e, 1), :]
            after_gate()
            u = _dot(load_x(), wu_b[...]) + bu_ref[pl.ds(e, 1), :]
            g = jnp.minimum(g, SWIGLU_LIMIT)
            u = jnp.clip(u, -SWIGLU_LIMIT, SWIGLU_LIMIT)
            glu = g * jax.nn.sigmoid(SWIGLU_ALPHA * g)
            return _dot(((u + 1.0) * glu).astype(BF16), wd_b[...]) + bd_ref[pl.ds(e, 1), :]

        lo = offs_ref[e]
        hi = offs_ref[e + 1]
        whole_tile = (lo <= m * tm) & (hi >= (m + 1) * tm)

        def ffn_and_copies(load_x, first, count):
            pinned = count // 2
            y = ffn(load_x, lambda: in_line(row_in, src2_ref, before, first, pinned))
            in_line(row_in, src2_ref, before, first + pinned, count - pinned)
            in_line(row_out, dstp_ref, before, first, count)
            return y

        @pl.when(whole_tile)
        def _():
            y = ffn_and_copies(lambda: _load_token_tiles(rows.at[cur]).astype(BF16), 0, tm)
            _store_token_tiles(ybuf.at[cur], y)

        @pl.when(jnp.logical_not(whole_tile))
        def _():
            sub = relay.shape[0] // SUBLANES
            for j in range(tm // sub):
                first = m * tm + j * sub
                span = pl.ds(j * sub * SUBLANES, sub * SUBLANES)
                load_x = lambda span=span: _load_token_tiles(rows.at[cur, span]).astype(BF16)

                def put(y, first=first, span=span):
                    _store_token_tiles(relay, y)
                    row = first + lax.shift_right_logical(lax.broadcasted_iota(I32, relay.shape, 0), 3)
                    pltpu.store(ybuf.at[cur, span], relay[...], mask=(row >= lo) & (row < hi))

                evaluate = (lo < first + sub) & (hi > first)
                owns_end = hi >= first + sub

                @pl.when(evaluate & owns_end)
                def _():
                    put(ffn_and_copies(load_x, j * sub, sub))

                @pl.when(evaluate & jnp.logical_not(owns_end))
                def _():
                    put(ffn(load_x))

        @pl.when(s == last)
        def _():
            in_loop(row_out, dst0_ref, cur)
            wait_rows(lax.rem(m + 1, 3))
            wait_rows(before)
            for buf in range(3):
                wait_y(buf)


def _experts(gid, tid, nxt, offs, nact, n_steps, inv, h2, wg, bg, wu, bu, wd, bd):
    ne, d, f = wg.shape
    tm = EXPERT_TILE
    n_tokens = h2.shape[0] // SUBLANES
    n_tiles = inv.shape[0] // tm
    assert n_tiles >= 2
    slot_rows = (n_tokens + tm // TOP_K) * SUBLANES
    inv_ext = jnp.concatenate([n_tokens * TOP_K + jnp.arange(tm, dtype=I32), inv])
    tok = lax.shift_right_logical(inv_ext, TOP_K.bit_length() - 1)
    src_row = tok * SUBLANES
    dst_row = (inv_ext & (TOP_K - 1)) * slot_rows + src_row
    whole = lambda s, gid, tid, nxt, offs, nact: (0, 0)

    def order_of(k):
        return pl.BlockSpec(
            (tm,), lambda s, gid, tid, nxt, offs, nact: (jnp.minimum(tid[s] + k, n_tiles - 1) + 1,),
            memory_space=pltpu.SMEM)

    grid_spec = pltpu.PrefetchScalarGridSpec(
        num_scalar_prefetch=5,
        grid=(n_steps,),
        in_specs=[
            order_of(-1), order_of(0), order_of(0), order_of(1), order_of(2),
            pl.BlockSpec((ne, f), whole),
            pl.BlockSpec((ne, f), whole),
            pl.BlockSpec((ne, d), whole),
            pl.BlockSpec(memory_space=pl.ANY),
            pl.BlockSpec(memory_space=pl.ANY),
            pl.BlockSpec(memory_space=pl.ANY),
            pl.BlockSpec(memory_space=pl.ANY),
        ],
        out_specs=pl.BlockSpec(memory_space=pl.ANY),
        scratch_shapes=[
            pltpu.VMEM((d, f), BF16), pltpu.VMEM((d, f), BF16), pltpu.VMEM((f, d), BF16),
            pltpu.VMEM((d, f), F32), pltpu.VMEM((d, f), F32), pltpu.VMEM((f, d), F32),
            pltpu.VMEM((EXPERT_SUBTILE * SUBLANES, LANES), F32),
            pltpu.VMEM((3, tm * SUBLANES, LANES), F32),
            pltpu.VMEM((3, tm * SUBLANES, LANES), F32),
            pltpu.SemaphoreType.DMA((3,)),
            pltpu.SemaphoreType.DMA((3,)),
            pltpu.SemaphoreType.DMA((3,)),
        ],
    )
    y4 = pl.pallas_call(
        _experts_kernel,
        out_shape=jax.ShapeDtypeStruct((TOP_K * slot_rows, LANES), F32),
        grid_spec=grid_spec,
        compiler_params=pltpu.CompilerParams(
            dimension_semantics=("arbitrary",), vmem_limit_bytes=VMEM_LIMIT),
        name="experts",
    )(gid, tid, nxt, offs, nact, dst_row, dst_row, src_row, src_row, src_row, bg, bu, bd, h2, wg, wu, wd)
    return y4.reshape(TOP_K, slot_rows, LANES)


def _combine_kernel(gate_ref, y4_ref, x1p_ref, x1s_ref, modp_ref, g2s_ref, fn_ref, outp_ref, outs_ref,
                    *, n_prompt_tiles):
    i = pl.program_id(0)
    tm = x1p_ref.shape[0]
    gates = gate_ref[...].T
    cols = []
    for c in range(SUBLANES):
        acc = gates[:, 0:1] * y4_ref[0, pl.ds(c, tm, stride=SUBLANES), :]
        for k in range(1, TOP_K):
            acc = acc + gates[:, k:k + 1] * y4_ref[k, pl.ds(c, tm, stride=SUBLANES), :]
        cols.append(acc)
    ffn = jnp.concatenate(cols, axis=-1)

    @pl.when(i < n_prompt_tiles)
    def _():
        g2 = modp_ref[...][5:6, :]
        outp_ref[...] = _rmsnorm(x1p_ref[...] + g2 * ffn, fn_ref[...])

    @pl.when(i >= n_prompt_tiles)
    def _():
        g2 = jnp.concatenate([g2s_ref[...]] * (tm // g2s_ref.shape[0]), axis=0)
        outs_ref[...] = _rmsnorm(x1s_ref[...] + g2 * ffn, fn_ref[...])


def _combine(gates, y4, x1_p, x1_s, mod_p, mod, n_batch_s, final_norm, tokens_per_seq):
    n_p, d = x1_p.shape
    n_s = x1_s.shape[0]
    tm = COMBINE_TILE
    assert tm % n_batch_s == 0 and n_batch_s % SUBLANES == 0
    npt, nst = n_p // tm, n_s // tm
    tiles_per_seq = tokens_per_seq // tm
    pmap = lambda i: (jnp.minimum(i, npt - 1), 0)
    smap = lambda i: (jnp.maximum(i - npt, 0), 0)
    return pl.pallas_call(
        functools.partial(_combine_kernel, n_prompt_tiles=npt),
        out_shape=(jax.ShapeDtypeStruct((n_p, d), F32), jax.ShapeDtypeStruct((n_s, d), F32)),
        grid=(npt + nst,),
        in_specs=[
            pl.BlockSpec((SUBLANES, tm), lambda i: (0, i)),
            pl.BlockSpec((TOP_K, tm * SUBLANES, LANES), lambda i: (0, i, 0)),
            pl.BlockSpec((tm, d), pmap),
            pl.BlockSpec((tm, d), smap),
            pl.BlockSpec((None, 6, d), lambda i: (jnp.minimum(i, npt - 1) // tiles_per_seq, 0, 0)),
            pl.BlockSpec((n_batch_s, d), lambda i: (0, 5)),
            pl.BlockSpec((1, d), lambda i: (0, 0)),
        ],
        out_specs=(pl.BlockSpec((tm, d), pmap), pl.BlockSpec((tm, d), smap)),
        compiler_params=pltpu.CompilerParams(
            dimension_semantics=("arbitrary",), vmem_limit_bytes=VMEM_LIMIT),
        name="combine",
    )(gates, y4, x1_p, x1_s, mod_p, mod, final_norm)


def kernel(x_prompt, x_sample, state_pool, state_conv, c_prompt, c_sample, norm1, norm2, w_ada, b_ada,
           w_in, w_pool, pool_scale, w_conv, w_out, w_router, b_router, w_gate, b_gate, w_up, b_up,
           w_down, b_down, final_norm):
    depth = norm1.shape[0]
    assert depth == 1, "single-layer step"
    bp, tp, d = x_prompt.shape
    bs, ts, _ = x_sample.shape
    dp = state_pool.shape[-1]
    n_hist = state_pool.shape[2]
    n_chist = state_conv.shape[2]
    n_p, n_s = bp * tp, bs * ts
    assert d == SUBLANES * LANES, "token-tile layout assumes one vreg tile per token row"
    assert tp % TOKEN_TILE == 0 and n_s == TOKEN_TILE, "the sample group fills exactly one token tile"
    assert tp % COMBINE_TILE == 0 and n_s % COMBINE_TILE == 0
    assert (n_p + n_s) % ROUTE_TILE == 0 and ((n_p + n_s) * TOP_K) % EXPERT_TILE == 0

    l = 0
    n1 = norm1[l].reshape(1, d)
    n2 = norm2[l].reshape(1, d)
    pscale = pool_scale[l].reshape(1, dp)
    w_r = w_router[l].T.astype(BF16)
    b_r = b_router[l].reshape(N_EXPERTS, 1)

    mod = _adaln(jnp.concatenate([c_sample, c_prompt], axis=0), w_ada[l], b_ada[l])
    mod_p = mod[bs:].reshape(bp, 6, d)

    xs_tm = jnp.transpose(x_sample, (1, 0, 2)).reshape(n_s, d)
    ps_tm = jnp.transpose(state_pool[l], (1, 0, 2))
    cs_tm = jnp.transpose(state_conv[l], (1, 0, 2))
    x1_s, h2_s, lg_s, newp_tm, newc_tm, w_in_b, w_pool_b, w_out_b = _mixer_sample(
        xs_tm, mod, ps_tm, cs_tm, n1, n2, w_in[l], w_pool[l], pscale, w_conv[l], w_out[l], w_r, b_r, ts)

    x1_p, h2, lg_p, u_tail, v_tail = _mixer_prompt(
        x_prompt, mod_p, n1, n2, w_in_b, w_pool_b, pscale, w_conv[l], w_out_b, w_r, b_r, h2_s)

    dest, gates, counts_f = _route(lg_p, lg_s)
    counts = counts_f[:, 0].astype(I32)
    n_rows = (n_p + n_s) * TOP_K
    gid, tid, nxt, offs, nact, n_steps = _group_metadata(counts, n_rows, EXPERT_TILE)

    dest = dest.reshape(-1)
    inv = _invert(dest)
    y4 = _experts(gid, tid, nxt, offs, nact, n_steps, inv, h2,
                  w_gate[l], b_gate[l], w_up[l], b_up[l], w_down[l], b_down[l])

    y_p, y_s = _combine(gates, y4, x1_p, x1_s, mod_p, mod, bs, final_norm.reshape(1, d), tp)

    y_prompt = y_p.reshape(bp, tp, d)
    y_sample = jnp.transpose(y_s.reshape(ts, bs, d), (1, 0, 2))
    new_pool_prompt = u_tail[:, POOL_HALO - n_hist:, :][None]
    new_conv_prompt = v_tail[:, CONV_HALO - n_chist:, :][None]
    new_pool_sample = jnp.transpose(newp_tm, (1, 0, 2))[None]
    new_conv_sample = jnp.transpose(newc_tm, (1, 0, 2))[None]
    return (y_prompt, y_sample, new_pool_prompt, new_conv_prompt, new_pool_sample, new_conv_sample)
```

```python
import functools

import jax
import jax.numpy as jnp
from jax import lax
from jax.experimental import pallas as pl
from jax.experimental.pallas import tpu as pltpu
from jax.experimental.pallas import tpu_sc as plsc

F32 = jnp.float32
BF16 = jnp.bfloat16
I32 = jnp.int32

POOL_WINDOWS = (2, 4, 8, 16)
POOL_HALO = 16
CONV_TAPS = 3
CONV_HALO = 8
N_EXPERTS = 32
TOP_K = 4
SWIGLU_LIMIT = 7.0
SWIGLU_ALPHA = 1.702
EPS = 1e-5
PAST_LEN = 16384

LANES = 128
SUBLANES = 8

TOKEN_TILE = 512
ROUTE_TILE = 512
COMBINE_TILE = 512
EXPERT_TILE = 512
EXPERT_SUBTILE = 128
VMEM_LIMIT = 56 * 1024 * 1024


def _rmsnorm(x, g):
    ms = jnp.mean(x * x, axis=-1, keepdims=True)
    return x * lax.rsqrt(ms + EPS) * g


def _dot(a, b):
    return jnp.dot(a, b, preferred_element_type=F32)


def _store_token_tiles(ref, val):
    rows = val.shape[0]
    for c in range(SUBLANES):
        ref[pl.ds(c, rows, stride=SUBLANES), :] = val[:, c * LANES:(c + 1) * LANES]


def _load_token_tiles(ref):
    rows = ref.shape[0] // SUBLANES
    return jnp.concatenate([ref[pl.ds(c, rows, stride=SUBLANES), :] for c in range(SUBLANES)], axis=-1)


ISSUE_GROUP = 4


def _adaln_kernel(c_ref, w_ref, b_ref, o_ref):
    c = c_ref[...]
    s = c * jax.nn.sigmoid(c)
    o_ref[...] = _dot(s.astype(BF16), w_ref[...].astype(BF16)) + b_ref[...]


def _adaln(c, w_ada, b_ada):
    rows, d = c.shape
    n = w_ada.shape[1]
    tn = 2048
    return pl.pallas_call(
        _adaln_kernel,
        out_shape=jax.ShapeDtypeStruct((rows, n), F32),
        grid=(n // tn,),
        in_specs=[
            pl.BlockSpec((rows, d), lambda j: (0, 0)),
            pl.BlockSpec((d, tn), lambda j: (0, j)),
            pl.BlockSpec((1, tn), lambda j: (0, j)),
        ],
        out_specs=pl.BlockSpec((rows, tn), lambda j: (0, j)),
        compiler_params=pltpu.CompilerParams(
            dimension_semantics=("arbitrary",), vmem_limit_bytes=VMEM_LIMIT),
        name="adaln",
    )(c, w_ada, b_ada.reshape(1, n))


def _mix_tail(x, pool_in, conv_out, g1, sc2, sh2, n2, wpool_ref, pscale, wout, wr, br):
    gw = pool_in.shape[1] // len(POOL_WINDOWS)
    mixed = [_dot(pool_in[:, g * gw:(g + 1) * gw].astype(BF16), wpool_ref[g])
             for g in range(len(POOL_WINDOWS))]
    pool_out = jnp.concatenate(mixed, axis=-1) * pscale
    mix_in = jnp.concatenate([pool_out, conv_out], axis=-1).astype(BF16)
    x1 = x + g1 * _dot(mix_in, wout)
    h2 = _rmsnorm(x1, n2) * (1.0 + sc2) + sh2
    logits_t = lax.dot_general(wr, h2.astype(BF16), (((1,), (1,)), ((), ())),
                               preferred_element_type=F32) + br
    return x1, h2, logits_t


def _mixer_prompt_kernel(x_ref, mod_ref, n1_ref, n2_ref, win_ref, wpool_ref, pscale_ref, wconv_ref,
                         wout_ref, wr_ref, br_ref, h2s_ref,
                         x1_ref, h2_ref, lg_ref, upool_ref, vconv_ref, ubuf, vbuf, *, tiles_per_seq):
    i = pl.program_id(0)

    @pl.when(i < pl.num_programs(0) - 1)
    def _():
        _mixer_prompt_tile(x_ref, mod_ref, n1_ref, n2_ref, win_ref, wpool_ref, pscale_ref, wconv_ref,
                           wout_ref, wr_ref, br_ref, x1_ref, h2_ref, lg_ref, upool_ref, vconv_ref,
                           ubuf, vbuf, lax.rem(i, tiles_per_seq), tiles_per_seq)

    @pl.when(i == pl.num_programs(0) - 1)
    def _():
        h2_ref[...] = h2s_ref[...]


def _mixer_prompt_tile(x_ref, mod_ref, n1_ref, n2_ref, win_ref, wpool_ref, pscale_ref, wconv_ref,
                       wout_ref, wr_ref, br_ref, x1_ref, h2_ref, lg_ref, upool_ref, vconv_ref,
                       ubuf, vbuf, t, tiles_per_seq):
    tt = x_ref.shape[0]
    dp = ubuf.shape[1]
    gw = dp // len(POOL_WINDOWS)

    @pl.when(t == 0)
    def _():
        ubuf[0:POOL_HALO, :] = jnp.zeros((POOL_HALO, dp), F32)
        vbuf[0:CONV_HALO, :] = jnp.zeros((CONV_HALO, dp), F32)

    x = x_ref[...]
    mod = mod_ref[...]
    sh1, sc1, g1, sh2, sc2, _ = [mod[i:i + 1, :] for i in range(6)]
    h = _rmsnorm(x, n1_ref[...]) * (1.0 + sc1) + sh1
    z = _dot(h.astype(BF16), win_ref[...])
    u, gate_b, gate_c, val = [z[:, i * dp:(i + 1) * dp] for i in range(4)]

    ubuf[POOL_HALO:POOL_HALO + tt, :] = u
    pos = lax.broadcasted_iota(I32, (tt, gw), 0) + t * tt
    pooled = []
    for g, w in enumerate(POOL_WINDOWS):
        cols = slice(g * gw, (g + 1) * gw)
        acc = u[:, cols]
        for j in range(1, w):
            acc = acc + ubuf[POOL_HALO - j:POOL_HALO - j + tt, cols]
        cnt = jnp.minimum(pos + 1, w).astype(F32)
        pooled.append(acc / cnt - u[:, cols])
    pool_in = jnp.concatenate(pooled, axis=-1)

    v = gate_c * val
    vbuf[CONV_HALO:CONV_HALO + tt, :] = v
    wc = wconv_ref[...]
    y = (wc[0:1, :] * vbuf[CONV_HALO - 2:CONV_HALO - 2 + tt, :]
         + wc[1:2, :] * vbuf[CONV_HALO - 1:CONV_HALO - 1 + tt, :]
         + wc[2:3, :] * v)
    conv_out = gate_b * y

    x1, h2, logits = _mix_tail(x, pool_in, conv_out, g1, sc2, sh2, n2_ref[...], wpool_ref,
                               pscale_ref[...], wout_ref[...], wr_ref[...], br_ref[...])
    x1_ref[...] = x1
    _store_token_tiles(h2_ref, h2)
    lg_ref[...] = logits

    ubuf[0:POOL_HALO, :] = ubuf[tt:tt + POOL_HALO, :]
    vbuf[0:CONV_HALO, :] = vbuf[tt:tt + CONV_HALO, :]

    @pl.when(t == tiles_per_seq - 1)
    def _():
        upool_ref[...] = ubuf[0:POOL_HALO, :]
        vconv_ref[...] = vbuf[0:CONV_HALO, :]


def _mixer_prompt(x, mod_p, n1, n2, w_in, w_pool, pscale, w_conv, w_out, w_r, b_r, h2_s):
    b, t, d = x.shape
    dp = w_pool.shape[0] * w_pool.shape[1]
    tt = min(TOKEN_TILE, t)
    nt = t // tt
    n_real = b * nt
    assert h2_s.shape[0] == tt * SUBLANES
    seq = lambda i: jnp.minimum(i, n_real - 1) // nt
    tile = lambda i: jnp.minimum(i, n_real - 1)
    const2 = lambda i: (0, 0)
    const3 = lambda i: (0, 0, 0)
    return pl.pallas_call(
        functools.partial(_mixer_prompt_kernel, tiles_per_seq=nt),
        out_shape=(
            jax.ShapeDtypeStruct((b * t, d), F32),
            jax.ShapeDtypeStruct(((n_real + 1) * tt * SUBLANES, LANES), F32),
            jax.ShapeDtypeStruct((N_EXPERTS, b * t), F32),
            jax.ShapeDtypeStruct((b, POOL_HALO, dp), F32),
            jax.ShapeDtypeStruct((b, CONV_HALO, dp), F32),
        ),
        grid=(n_real + 1,),
        in_specs=[
            pl.BlockSpec((None, tt, d), lambda i: (seq(i), tile(i) % nt, 0)),
            pl.BlockSpec((None, 6, d), lambda i: (seq(i), 0, 0)),
            pl.BlockSpec((1, d), const2),
            pl.BlockSpec((1, d), const2),
            pl.BlockSpec(w_in.shape, const2),
            pl.BlockSpec(w_pool.shape, const3),
            pl.BlockSpec((1, dp), const2),
            pl.BlockSpec(w_conv.shape, const2),
            pl.BlockSpec(w_out.shape, const2),
            pl.BlockSpec(w_r.shape, const2),
            pl.BlockSpec((N_EXPERTS, 1), const2),
            pl.BlockSpec(h2_s.shape, const2),
        ],
        out_specs=(
            pl.BlockSpec((tt, d), lambda i: (tile(i), 0)),
            pl.BlockSpec((tt * SUBLANES, LANES), lambda i: (i, 0)),
            pl.BlockSpec((N_EXPERTS, tt), lambda i: (0, tile(i))),
            pl.BlockSpec((None, POOL_HALO, dp), lambda i: (seq(i), 0, 0)),
            pl.BlockSpec((None, CONV_HALO, dp), lambda i: (seq(i), 0, 0)),
        ),
        scratch_shapes=[
            pltpu.VMEM((POOL_HALO + tt, dp), F32),
            pltpu.VMEM((CONV_HALO + tt, dp), F32),
        ],
        compiler_params=pltpu.CompilerParams(
            dimension_semantics=("arbitrary",), vmem_limit_bytes=VMEM_LIMIT),
        name="mixer_prompt",
    )(x, mod_p, n1, n2, w_in, w_pool, pscale, w_conv, w_out, w_r, b_r, h2_s)


def _mixer_sample_kernel(x_ref, mod_ref, pstate_ref, cstate_ref, n1_ref, n2_ref, win_f32, wpool_f32,
                         pscale_ref, wconv_ref, wout_f32, wr_ref, br_ref,
                         x1_ref, h2_ref, lg_ref, newp_ref, newc_ref, win_ref, wpool_ref, wout_ref,
                         *, steps):
    win_ref[...] = win_f32[...].astype(BF16)
    wpool_ref[...] = wpool_f32[...].astype(BF16)
    wout_ref[...] = wout_f32[...].astype(BF16)
    nb = pstate_ref.shape[1]
    d = x_ref.shape[1]
    dp = pstate_ref.shape[2]
    gw = dp // len(POOL_WINDOWS)
    n_hist = pstate_ref.shape[0]
    n_chist = cstate_ref.shape[0]

    x = x_ref[...]
    mod = mod_ref[0:nb, :]
    rep = lambda a: jnp.concatenate([a] * steps, axis=0)
    sh1, sc1, g1, sh2, sc2, _ = [rep(mod[:, i * d:(i + 1) * d]) for i in range(6)]
    h = _rmsnorm(x, n1_ref[...]) * (1.0 + sc1) + sh1
    z = _dot(h.astype(BF16), win_ref[...])
    u, gate_b, gate_c, val = [z[:, i * dp:(i + 1) * dp] for i in range(4)]

    ext = [pstate_ref[i] for i in range(n_hist)] + [u[s * nb:(s + 1) * nb, :] for s in range(steps)]
    pooled_steps = []
    for s in range(steps):
        groups = []
        for g, w in enumerate(POOL_WINDOWS):
            cols = slice(g * gw, (g + 1) * gw)
            acc = ext[n_hist + s][:, cols]
            for j in range(1, w):
                acc = acc + ext[n_hist + s - j][:, cols]
            cnt = float(min(PAST_LEN + s + 1, w))
            groups.append(acc / cnt - ext[n_hist + s][:, cols])
        pooled_steps.append(jnp.concatenate(groups, axis=-1))
    pool_in = jnp.concatenate(pooled_steps, axis=0)

    v = gate_c * val
    vext = [cstate_ref[i] for i in range(n_chist)] + [v[s * nb:(s + 1) * nb, :] for s in range(steps)]
    wc = wconv_ref[...]
    y = jnp.concatenate(
        [wc[0:1, :] * vext[s] + wc[1:2, :] * vext[s + 1] + wc[2:3, :] * vext[s + 2] for s in range(steps)],
        axis=0)
    conv_out = gate_b * y

    x1, h2, logits = _mix_tail(x, pool_in, conv_out, g1, sc2, sh2, n2_ref[...], wpool_ref,
                               pscale_ref[...], wout_ref[...], wr_ref[...], br_ref[...])
    x1_ref[...] = x1
    _store_token_tiles(h2_ref, h2)
    lg_ref[...] = logits
    for i in range(n_hist):
        newp_ref[i] = ext[steps + i]
    for i in range(n_chist):
        newc_ref[i] = vext[steps + i]


def _mixer_sample(x_tm, mod_s, pstate_tm, cstate_tm, n1, n2, w_in, w_pool, pscale, w_conv, w_out, w_r, b_r,
                  steps):
    rows, d = x_tm.shape
    return pl.pallas_call(
        functools.partial(_mixer_sample_kernel, steps=steps),
        out_shape=(
            jax.ShapeDtypeStruct((rows, d), F32),
            jax.ShapeDtypeStruct((rows * SUBLANES, LANES), F32),
            jax.ShapeDtypeStruct((N_EXPERTS, rows), F32),
            jax.ShapeDtypeStruct(pstate_tm.shape, F32),
            jax.ShapeDtypeStruct(cstate_tm.shape, F32),
            jax.ShapeDtypeStruct(w_in.shape, BF16),
            jax.ShapeDtypeStruct(w_pool.shape, BF16),
            jax.ShapeDtypeStruct(w_out.shape, BF16),
        ),
        compiler_params=pltpu.CompilerParams(vmem_limit_bytes=VMEM_LIMIT),
        name="mixer_sample",
    )(x_tm, mod_s, pstate_tm, cstate_tm, n1, n2, w_in, w_pool, pscale, w_conv, w_out, w_r, b_r)


def _route_kernel(lgp_ref, lgs_ref, dest_ref, gate_ref, cnt_ref, counts, start, before, chosen):
    ne = lgp_ref.shape[0]
    tr = before.shape[0]
    reps = tr // LANES
    n_prompt_chunks = lgp_ref.shape[1] // tr
    n_sample_chunks = lgs_ref.shape[1] // tr
    eidx = lax.broadcasted_iota(I32, (ne, tr), 0)

    def selected(onehots):
        return jnp.where(onehots[0] | onehots[1] | onehots[2] | onehots[3], 1.0, 0.0)

    def select_chunk(ref, first_chunk):
        def body(c, carry):
            chunk = first_chunk + c
            work = ref[:, pl.ds(pl.multiple_of(c * tr, tr), tr)]
            top_v, onehots = [], []
            for k in range(TOP_K):
                m = jnp.max(work, axis=0, keepdims=True)
                idx = jnp.min(jnp.where(work == m, eidx, ne), axis=0, keepdims=True)
                chosen[chunk, k:k + 1, :] = idx
                top_v.append(m)
                onehots.append(eidx == idx)
                work = jnp.where(onehots[k], -jnp.inf, work)
            es = [jnp.exp(v - top_v[0]) for v in top_v]
            denom = es[0] + es[1] + es[2] + es[3]
            cols = pl.ds(pl.multiple_of(chunk * tr, tr), tr)
            for k in range(TOP_K):
                gate_ref[k:k + 1, cols] = es[k] / denom
            chunk_counts = jnp.sum(selected(onehots), axis=1, keepdims=True)
            counts[...] = counts[...] + jnp.broadcast_to(chunk_counts, (ne, LANES))
            return carry
        return body

    counts[...] = jnp.zeros_like(counts)
    gate_ref[...] = jnp.zeros_like(gate_ref)
    lax.fori_loop(0, n_prompt_chunks, select_chunk(lgp_ref, 0), 0)
    lax.fori_loop(0, n_sample_chunks, select_chunk(lgs_ref, n_prompt_chunks), 0)

    total = counts[...]
    hi = jnp.floor(total * (1.0 / 256.0))
    lo = total - hi * 256.0
    r = lax.broadcasted_iota(I32, (ne, ne), 0)
    col = lax.broadcasted_iota(I32, (ne, ne), 1)
    lower = jnp.where(col < r, 1.0, 0.0).astype(BF16)
    start[...] = 256.0 * _dot(lower, hi.astype(BF16)) + _dot(lower, lo.astype(BF16))
    cnt_ref[...] = total
    counts[...] = jnp.zeros_like(counts)

    r = lax.broadcasted_iota(I32, (tr, tr), 0)
    col = lax.broadcasted_iota(I32, (tr, tr), 1)
    before[...] = jnp.where(r < col, 1.0, 0.0).astype(BF16)

    def place_chunk(chunk, carry):
        onehots = [eidx == chosen[chunk, k:k + 1, :] for k in range(TOP_K)]
        mask = selected(onehots)
        base = jnp.concatenate([counts[...] + start[...]] * reps, axis=1)
        rank = _dot(mask.astype(BF16), before[...]) + base
        counts[...] = counts[...] + jnp.broadcast_to(jnp.sum(mask, axis=1, keepdims=True), (ne, LANES))
        for k in range(TOP_K):
            d = jnp.sum(jnp.where(onehots[k], rank, 0.0), axis=0, keepdims=True).astype(I32)
            for j in range(reps):
                dest_ref[chunk * reps + j, k:k + 1, :] = d[:, j * LANES:(j + 1) * LANES]
        return carry

    lax.fori_loop(0, n_prompt_chunks + n_sample_chunks, place_chunk, 0)


def _route(lgt_p, lgt_s):
    ne, n_p = lgt_p.shape
    n_s = lgt_s.shape[1]
    tr = ROUTE_TILE
    return pl.pallas_call(
        _route_kernel,
        out_shape=(
            jax.ShapeDtypeStruct(((n_p + n_s) // LANES, TOP_K, LANES), I32),
            jax.ShapeDtypeStruct((SUBLANES, n_p + n_s), F32),
            jax.ShapeDtypeStruct((ne, LANES), F32),
        ),
        scratch_shapes=[pltpu.VMEM((ne, LANES), F32), pltpu.VMEM((ne, LANES), F32),
                        pltpu.VMEM((tr, tr), BF16), pltpu.VMEM(((n_p + n_s) // tr, TOP_K, tr), I32)],
        compiler_params=pltpu.CompilerParams(vmem_limit_bytes=VMEM_LIMIT),
        name="route",
    )(lgt_p, lgt_s)


def _group_metadata(counts, n_rows, tile):
    n_tiles = n_rows // tile
    n_steps = n_tiles + N_EXPERTS - 1
    ends = jnp.cumsum(counts)
    offs = jnp.concatenate([jnp.zeros((1,), I32), ends]).astype(I32)
    first_tile = offs[:-1] // tile
    last_tile = (ends - 1) // tile
    tiles_e = jnp.where(counts > 0, last_tile - first_tile + 1, 0)
    step_end = jnp.cumsum(tiles_e)
    step_start = step_end - tiles_e
    n_active = step_end[-1]
    s = jnp.minimum(jnp.arange(n_steps, dtype=I32), n_active - 1)
    owner = ((s[:, None] >= step_start[None, :]) & (s[:, None] < step_end[None, :])).astype(I32)
    gid = jnp.sum(owner * jnp.arange(N_EXPERTS, dtype=I32)[None, :], axis=1)
    tid = jnp.sum(owner * (first_tile - step_start)[None, :], axis=1) + s
    ids = jnp.arange(N_EXPERTS, dtype=I32)
    later = (ids[None, :] > ids[:, None]) & (counts[None, :] > 0)
    next_e = jnp.min(jnp.where(later, ids[None, :], N_EXPERTS), axis=1)
    next_e = jnp.where(next_e == N_EXPERTS, -1, next_e)
    nxt = jnp.sum(owner * next_e[None, :], axis=1)
    return gid, tid, nxt, offs, n_active.reshape(1).astype(I32), n_steps


SC_LANES = 16
SC_INDEX_BATCH = 128


def _invert(dest):
    r = dest.shape[0]
    mesh = plsc.VectorSubcoreMesh(core_axis_name="core", subcore_axis_name="subcore")
    per = r // mesh.num_subcores
    assert per * mesh.num_subcores == r and per % SC_INDEX_BATCH == 0

    @functools.partial(
        pl.kernel, mesh=mesh, out_type=jax.ShapeDtypeStruct((r,), I32),
        scratch_types=[pltpu.VMEM_SHARED((r,), I32), pltpu.VMEM((per,), I32), pltpu.VMEM((per,), I32)],
        compiler_params=pltpu.CompilerParams(needs_layout_passes=False),
        name="invert",
    )
    def invert(dest_hbm, inv_hbm, table, idx, ids):
        @pl.when(lax.axis_index("core") == 0)
        def _():
            base = lax.axis_index("subcore") * per
            pltpu.sync_copy(dest_hbm.at[pl.ds(base, per)], idx)
            lane = lax.iota(I32, SC_LANES)

            @pl.loop(0, per // SC_LANES)
            def _(i):
                p = base + i * SC_LANES + lane
                tok = lax.shift_right_logical(p, 9) * LANES + (p & (LANES - 1))
                slot = lax.shift_right_logical(p, 7) & (TOP_K - 1)
                ids[pl.ds(i * SC_LANES, SC_LANES)] = tok * TOP_K + slot

            @pl.loop(0, per // SC_INDEX_BATCH)
            def _(j):
                span = pl.ds(j * SC_INDEX_BATCH, SC_INDEX_BATCH)
                pltpu.sync_copy(ids.at[span], table.at[idx.at[span]])

            plsc.subcore_barrier()
            pltpu.sync_copy(table.at[pl.ds(base, per)], inv_hbm.at[pl.ds(base, per)])

    return invert(dest)


def _experts_kernel(gid_ref, tid_ref, nxt_ref, offs_ref, nact_ref,
                    dstp_ref, dst0_ref, src0_ref, src1_ref, src2_ref, bg_ref, bu_ref, bd_ref,
                    h2_hbm, wg_hbm, wu_hbm, wd_hbm, y4_hbm,
                    wg_b, wu_b, wd_b, wg_f, wu_f, wd_f, relay, rows, ybuf, sems, row_sems, y_sems):
    s = pl.program_id(0)
    tile_rows = rows.shape[1]
    tm = tile_rows // SUBLANES
    landing = ((wg_hbm, wg_f, wg_b), (wu_hbm, wu_f, wu_b), (wd_hbm, wd_f, wd_b))

    def fetch(e):
        for j, (hbm, land, _) in enumerate(landing):
            pltpu.make_async_copy(hbm.at[e], land, sems.at[j]).start()

    def row_in(idx_ref, into, r):
        src = h2_hbm.at[pl.ds(pl.multiple_of(idx_ref[r], SUBLANES), SUBLANES)]
        dst = rows.at[into, pl.ds(pl.multiple_of(r * SUBLANES, SUBLANES), SUBLANES)]
        return pltpu.make_async_copy(src, dst, row_sems.at[into])

    def row_out(idx_ref, frm, r):
        src = ybuf.at[frm, pl.ds(pl.multiple_of(r * SUBLANES, SUBLANES), SUBLANES)]
        dst = y4_hbm.at[pl.ds(pl.multiple_of(idx_ref[r], SUBLANES), SUBLANES)]
        return pltpu.make_async_copy(src, dst, y_sems.at[frm])

    def in_line(make, idx_ref, buf, first=0, count=None):
        for r in range(first, tm if count is None else first + count):
            make(idx_ref, buf, r).start(priority=r % 2)

    def in_loop(make, idx_ref, buf):
        def group(g, carry):
            for j in range(ISSUE_GROUP):
                make(idx_ref, buf, g * ISSUE_GROUP + j).start(priority=j % 2)
            return carry
        lax.fori_loop(0, tm // ISSUE_GROUP, group, 0)

    def wait_rows(buf):
        pltpu.make_async_copy(h2_hbm.at[pl.ds(0, tile_rows)], rows.at[buf], row_sems.at[buf]).wait()

    def wait_y(buf):
        pltpu.make_async_copy(ybuf.at[buf], y4_hbm.at[pl.ds(0, tile_rows)], y_sems.at[buf]).wait()

    @pl.when(s < nact_ref[0])
    def _():
        e = gid_ref[s]
        m = tid_ref[s]
        cur = lax.rem(m, 3)
        before = lax.rem(m + 2, 3)
        last = nact_ref[0] - 1
        new_tile = (s == 0) | (tid_ref[jnp.maximum(s - 1, 0)] != m)

        @pl.when(s == 0)
        def _():
            fetch(e)
            in_loop(row_in, src0_ref, 0)
            in_loop(row_in, src1_ref, 1)
            ybuf[2] = jnp.zeros(ybuf.shape[1:], F32)

        @pl.when((s == 0) | (gid_ref[jnp.maximum(s - 1, 0)] != e))
        def _():
            for j, (hbm, land, half) in enumerate(landing):
                pltpu.make_async_copy(hbm.at[e], land, sems.at[j]).wait()
                half[...] = land[...].astype(BF16)

            @pl.when(nxt_ref[s] >= 0)
            def _():
                fetch(nxt_ref[s])

        @pl.when(new_tile)
        def _():
            wait_rows(cur)

            @pl.when(m >= 2)
            def _():
                wait_y(cur)

        def ffn(load_x, after_gate=lambda: None):
            g = _dot(load_x(), wg_b[...]) + bg_ref[pl.ds(e, 1), :]
            after_gate()
            u = _dot(load_x(), wu_b[...]) + bu_ref[pl.ds(e, 1), :]
            g = jnp.minimum(g, SWIGLU_LIMIT)
            u = jnp.clip(u, -SWIGLU_LIMIT, SWIGLU_LIMIT)
            glu = g * jax.nn.sigmoid(SWIGLU_ALPHA * g)
            return _dot(((u + 1.0) * glu).astype(BF16), wd_b[...]) + bd_ref[pl.ds(e, 1), :]

        lo = offs_ref[e]
        hi = offs_ref[e + 1]
        whole_tile = (lo <= m * tm) & (hi >= (m + 1) * tm)

        def ffn_and_copies(load_x, first, count):
            pinned = count // 2
            y = ffn(load_x, lambda: in_line(row_in, src2_ref, before, first, pinned))
            in_line(row_in, src2_ref, before, first + pinned, count - pinned)
            in_line(row_out, dstp_ref, before, first, count)
            return y

        @pl.when(whole_tile)
        def _():
            y = ffn_and_copies(lambda: _load_token_tiles(rows.at[cur]).astype(BF16), 0, tm)
            _store_token_tiles(ybuf.at[cur], y)

        @pl.when(jnp.logical_not(whole_tile))
        def _():
            sub = relay.shape[0] // SUBLANES
            for j in range(tm // sub):
                first = m * tm + j * sub
                span = pl.ds(j * sub * SUBLANES, sub * SUBLANES)
                load_x = lambda span=span: _load_token_tiles(rows.at[cur, span]).astype(BF16)

                def put(y, first=first, span=span):
                    _store_token_tiles(relay, y)
                    row = first + lax.shift_right_logical(lax.broadcasted_iota(I32, relay.shape, 0), 3)
                    pltpu.store(ybuf.at[cur, span], relay[...], mask=(row >= lo) & (row < hi))

                evaluate = (lo < first + sub) & (hi > first)
                owns_end = hi >= first + sub

                @pl.when(evaluate & owns_end)
                def _():
                    put(ffn_and_copies(load_x, j * sub, sub))

                @pl.when(evaluate & jnp.logical_not(owns_end))
                def _():
                    put(ffn(load_x))

        @pl.when(s == last)
        def _():
            in_loop(row_out, dst0_ref, cur)
            wait_rows(lax.rem(m + 1, 3))
            wait_rows(before)
            for buf in range(3):
                wait_y(buf)


def _experts(gid, tid, nxt, offs, nact, n_steps, inv, h2, wg, bg, wu, bu, wd, bd):
    ne, d, f = wg.shape
    tm = EXPERT_TILE
    n_tokens = h2.shape[0] // SUBLANES
    n_tiles = inv.shape[0] // tm
    assert n_tiles >= 2
    slot_rows = (n_tokens + tm // TOP_K) * SUBLANES
    inv_ext = jnp.concatenate([n_tokens * TOP_K + jnp.arange(tm, dtype=I32), inv])
    tok = lax.shift_right_logical(inv_ext, TOP_K.bit_length() - 1)
    src_row = tok * SUBLANES
    dst_row = (inv_ext & (TOP_K - 1)) * slot_rows + src_row
    whole = lambda s, gid, tid, nxt, offs, nact: (0, 0)

    def order_of(k):
        return pl.BlockSpec(
            (tm,), lambda s, gid, tid, nxt, offs, nact: (jnp.minimum(tid[s] + k, n_tiles - 1) + 1,),
            memory_space=pltpu.SMEM)

    grid_spec = pltpu.PrefetchScalarGridSpec(
        num_scalar_prefetch=5,
        grid=(n_steps,),
        in_specs=[
            order_of(-1), order_of(0), order_of(0), order_of(1), order_of(2),
            pl.BlockSpec((ne, f), whole),
            pl.BlockSpec((ne, f), whole),
            pl.BlockSpec((ne, d), whole),
            pl.BlockSpec(memory_space=pl.ANY),
            pl.BlockSpec(memory_space=pl.ANY),
            pl.BlockSpec(memory_space=pl.ANY),
            pl.BlockSpec(memory_space=pl.ANY),
        ],
        out_specs=pl.BlockSpec(memory_space=pl.ANY),
        scratch_shapes=[
            pltpu.VMEM((d, f), BF16), pltpu.VMEM((d, f), BF16), pltpu.VMEM((f, d), BF16),
            pltpu.VMEM((d, f), F32), pltpu.VMEM((d, f), F32), pltpu.VMEM((f, d), F32),
            pltpu.VMEM((EXPERT_SUBTILE * SUBLANES, LANES), F32),
            pltpu.VMEM((3, tm * SUBLANES, LANES), F32),
            pltpu.VMEM((3, tm * SUBLANES, LANES), F32),
            pltpu.SemaphoreType.DMA((3,)),
            pltpu.SemaphoreType.DMA((3,)),
            pltpu.SemaphoreType.DMA((3,)),
        ],
    )
    y4 = pl.pallas_call(
        _experts_kernel,
        out_shape=jax.ShapeDtypeStruct((TOP_K * slot_rows, LANES), F32),
        grid_spec=grid_spec,
        compiler_params=pltpu.CompilerParams(
            dimension_semantics=("arbitrary",), vmem_limit_bytes=VMEM_LIMIT),
        name="experts",
    )(gid, tid, nxt, offs, nact, dst_row, dst_row, src_row, src_row, src_row, bg, bu, bd, h2, wg, wu, wd)
    return y4.reshape(TOP_K, slot_rows, LANES)


def _combine_kernel(gate_ref, y4_ref, x1p_ref, x1s_ref, modp_ref, g2s_ref, fn_ref, outp_ref, outs_ref,
                    *, n_prompt_tiles):
    i = pl.program_id(0)
    tm = x1p_ref.shape[0]
    gates = gate_ref[...].T
    cols = []
    for c in range(SUBLANES):
        acc = gates[:, 0:1] * y4_ref[0, pl.ds(c, tm, stride=SUBLANES), :]
        for k in range(1, TOP_K):
            acc = acc + gates[:, k:k + 1] * y4_ref[k, pl.ds(c, tm, stride=SUBLANES), :]
        cols.append(acc)
    ffn = jnp.concatenate(cols, axis=-1)

    @pl.when(i < n_prompt_tiles)
    def _():
        g2 = modp_ref[...][5:6, :]
        outp_ref[...] = _rmsnorm(x1p_ref[...] + g2 * ffn, fn_ref[...])

    @pl.when(i >= n_prompt_tiles)
    def _():
        g2 = jnp.concatenate([g2s_ref[...]] * (tm // g2s_ref.shape[0]), axis=0)
        outs_ref[...] = _rmsnorm(x1s_ref[...] + g2 * ffn, fn_ref[...])


def _combine(gates, y4, x1_p, x1_s, mod_p, mod, n_batch_s, final_norm, tokens_per_seq):
    n_p, d = x1_p.shape
    n_s = x1_s.shape[0]
    tm = COMBINE_TILE
    assert tm % n_batch_s == 0 and n_batch_s % SUBLANES == 0
    npt, nst = n_p // tm, n_s // tm
    tiles_per_seq = tokens_per_seq // tm
    pmap = lambda i: (jnp.minimum(i, npt - 1), 0)
    smap = lambda i: (jnp.maximum(i - npt, 0), 0)
    return pl.pallas_call(
        functools.partial(_combine_kernel, n_prompt_tiles=npt),
        out_shape=(jax.ShapeDtypeStruct((n_p, d), F32), jax.ShapeDtypeStruct((n_s, d), F32)),
        grid=(npt + nst,),
        in_specs=[
            pl.BlockSpec((SUBLANES, tm), lambda i: (0, i)),
            pl.BlockSpec((TOP_K, tm * SUBLANES, LANES), lambda i: (0, i, 0)),
            pl.BlockSpec((tm, d), pmap),
            pl.BlockSpec((tm, d), smap),
            pl.BlockSpec((None, 6, d), lambda i: (jnp.minimum(i, npt - 1) // tiles_per_seq, 0, 0)),
            pl.BlockSpec((n_batch_s, d), lambda i: (0, 5)),
            pl.BlockSpec((1, d), lambda i: (0, 0)),
        ],
        out_specs=(pl.BlockSpec((tm, d), pmap), pl.BlockSpec((tm, d), smap)),
        compiler_params=pltpu.CompilerParams(
            dimension_semantics=("arbitrary",), vmem_limit_bytes=VMEM_LIMIT),
        name="combine",
    )(gates, y4, x1_p, x1_s, mod_p, mod, final_norm)


def kernel(x_prompt, x_sample, state_pool, state_conv, c_prompt, c_sample, norm1, norm2, w_ada, b_ada,
           w_in, w_pool, pool_scale, w_conv, w_out, w_router, b_router, w_gate, b_gate, w_up, b_up,
           w_down, b_down, final_norm):
    depth = norm1.shape[0]
    assert depth == 1, "single-layer step"
    bp, tp, d = x_prompt.shape
    bs, ts, _ = x_sample.shape
    dp = state_pool.shape[-1]
    n_hist = state_pool.shape[2]
    n_chist = state_conv.shape[2]
    n_p, n_s = bp * tp, bs * ts
    assert d == SUBLANES * LANES, "token-tile layout assumes one vreg tile per token row"
    assert tp % TOKEN_TILE == 0 and n_s == TOKEN_TILE, "the sample group fills exactly one token tile"
    assert tp % COMBINE_TILE == 0 and n_s % COMBINE_TILE == 0
    assert (n_p + n_s) % ROUTE_TILE == 0 and ((n_p + n_s) * TOP_K) % EXPERT_TILE == 0

    l = 0
    n1 = norm1[l].reshape(1, d)
    n2 = norm2[l].reshape(1, d)
    pscale = pool_scale[l].reshape(1, dp)
    w_r = w_router[l].T.astype(BF16)
    b_r = b_router[l].reshape(N_EXPERTS, 1)

    mod = _adaln(jnp.concatenate([c_sample, c_prompt], axis=0), w_ada[l], b_ada[l])
    mod_p = mod[bs:].reshape(bp, 6, d)

    xs_tm = jnp.transpose(x_sample, (1, 0, 2)).reshape(n_s, d)
    ps_tm = jnp.transpose(state_pool[l], (1, 0, 2))
    cs_tm = jnp.transpose(state_conv[l], (1, 0, 2))
    x1_s, h2_s, lg_s, newp_tm, newc_tm, w_in_b, w_pool_b, w_out_b = _mixer_sample(
        xs_tm, mod, ps_tm, cs_tm, n1, n2, w_in[l], w_pool[l], pscale, w_conv[l], w_out[l], w_r, b_r, ts)

    x1_p, h2, lg_p, u_tail, v_tail = _mixer_prompt(
        x_prompt, mod_p, n1, n2, w_in_b, w_pool_b, pscale, w_conv[l], w_out_b, w_r, b_r, h2_s)

    dest, gates, counts_f = _route(lg_p, lg_s)
    counts = counts_f[:, 0].astype(I32)
    n_rows = (n_p + n_s) * TOP_K
    gid, tid, nxt, offs, nact, n_steps = _group_metadata(counts, n_rows, EXPERT_TILE)

    dest = dest.reshape(-1)
    inv = _invert(dest)
    y4 = _experts(gid, tid, nxt, offs, nact, n_steps, inv, h2,
                  w_gate[l], b_gate[l], w_up[l], b_up[l], w_down[l], b_down[l])

    y_p, y_s = _combine(gates, y4, x1_p, x1_s, mod_p, mod, bs, final_norm.reshape(1, d), tp)

    y_prompt = y_p.reshape(bp, tp, d)
    y_sample = jnp.transpose(y_s.reshape(ts, bs, d), (1, 0, 2))
    new_pool_prompt = u_tail[:, POOL_HALO - n_hist:, :][None]
    new_conv_prompt = v_tail[:, CONV_HALO - n_chist:, :][None]
    new_pool_sample = jnp.transpose(newp_tm, (1, 0, 2))[None]
    new_conv_sample = jnp.transpose(newc_tm, (1, 0, 2))[None]
    return (y_prompt, y_sample, new_pool_prompt, new_conv_prompt, new_pool_sample, new_conv_sample)
```

```python
import functools

import jax
import jax.numpy as jnp
from jax import lax
from jax.experimental import pallas as pl
from jax.experimental.pallas import tpu as pltpu
from jax.experimental.pallas import tpu_sc as plsc

F32 = jnp.float32
BF16 = jnp.bfloat16
I32 = jnp.int32

POOL_WINDOWS = (2, 4, 8, 16)
POOL_HALO = 16
CONV_TAPS = 3
CONV_HALO = 8
N_EXPERTS = 32
TOP_K = 4
SWIGLU_LIMIT = 7.0
SWIGLU_ALPHA = 1.702
EPS = 1e-5
PAST_LEN = 16384

LANES = 128
SUBLANES = 8

TOKEN_TILE = 512
ROUTE_TILE = 512
COMBINE_TILE = 512
EXPERT_TILE = 512
EXPERT_SUBTILE = 128
VMEM_LIMIT = 56 * 1024 * 1024


def _rmsnorm(x, g):
    ms = jnp.mean(x * x, axis=-1, keepdims=True)
    return x * lax.rsqrt(ms + EPS) * g


def _dot(a, b):
    return jnp.dot(a, b, preferred_element_type=F32)


def _store_token_tiles(ref, val):
    rows = val.shape[0]
    for c in range(SUBLANES):
        ref[pl.ds(c, rows, stride=SUBLANES), :] = val[:, c * LANES:(c + 1) * LANES]


def _load_token_tiles(ref):
    rows = ref.shape[0] // SUBLANES
    return jnp.concatenate([ref[pl.ds(c, rows, stride=SUBLANES), :] for c in range(SUBLANES)], axis=-1)


ISSUE_GROUP = 4


def _adaln_kernel(c_ref, w_ref, b_ref, o_ref):
    c = c_ref[...]
    s = c * jax.nn.sigmoid(c)
    o_ref[...] = _dot(s.astype(BF16), w_ref[...].astype(BF16)) + b_ref[...]


def _adaln(c, w_ada, b_ada):
    rows, d = c.shape
    n = w_ada.shape[1]
    tn = 1024
    return pl.pallas_call(
        _adaln_kernel,
        out_shape=jax.ShapeDtypeStruct((rows, n), F32),
        grid=(n // tn,),
        in_specs=[
            pl.BlockSpec((rows, d), lambda j: (0, 0)),
            pl.BlockSpec((d, tn), lambda j: (0, j)),
            pl.BlockSpec((1, tn), lambda j: (0, j)),
        ],
        out_specs=pl.BlockSpec((rows, tn), lambda j: (0, j)),
        compiler_params=pltpu.CompilerParams(
            dimension_semantics=("arbitrary",), vmem_limit_bytes=VMEM_LIMIT),
        name="adaln",
    )(c, w_ada, b_ada.reshape(1, n))


def _mix_tail(x, pool_in, conv_out, g1, sc2, sh2, n2, wpool_ref, pscale, wout, wr, br):
    gw = pool_in.shape[1] // len(POOL_WINDOWS)
    mixed = [_dot(pool_in[:, g * gw:(g + 1) * gw].astype(BF16), wpool_ref[g])
             for g in range(len(POOL_WINDOWS))]
    pool_out = jnp.concatenate(mixed, axis=-1) * pscale
    mix_in = jnp.concatenate([pool_out, conv_out], axis=-1).astype(BF16)
    x1 = x + g1 * _dot(mix_in, wout)
    h2 = _rmsnorm(x1, n2) * (1.0 + sc2) + sh2
    logits_t = lax.dot_general(wr, h2.astype(BF16), (((1,), (1,)), ((), ())),
                               preferred_element_type=F32) + br
    return x1, h2, logits_t


def _mixer_prompt_kernel(x_ref, mod_ref, n1_ref, n2_ref, win_ref, wpool_ref, pscale_ref, wconv_ref,
                         wout_ref, wr_ref, br_ref, h2s_ref,
                         x1_ref, h2_ref, lg_ref, upool_ref, vconv_ref, ubuf, vbuf, *, tiles_per_seq):
    i = pl.program_id(0)

    @pl.when(i < pl.num_programs(0) - 1)
    def _():
        _mixer_prompt_tile(x_ref, mod_ref, n1_ref, n2_ref, win_ref, wpool_ref, pscale_ref, wconv_ref,
                           wout_ref, wr_ref, br_ref, x1_ref, h2_ref, lg_ref, upool_ref, vconv_ref,
                           ubuf, vbuf, lax.rem(i, tiles_per_seq), tiles_per_seq)

    @pl.when(i == pl.num_programs(0) - 1)
    def _():
        h2_ref[...] = h2s_ref[...]


def _mixer_prompt_tile(x_ref, mod_ref, n1_ref, n2_ref, win_ref, wpool_ref, pscale_ref, wconv_ref,
                       wout_ref, wr_ref, br_ref, x1_ref, h2_ref, lg_ref, upool_ref, vconv_ref,
                       ubuf, vbuf, t, tiles_per_seq):
    tt = x_ref.shape[0]
    dp = ubuf.shape[1]
    gw = dp // len(POOL_WINDOWS)

    @pl.when(t == 0)
    def _():
        ubuf[0:POOL_HALO, :] = jnp.zeros((POOL_HALO, dp), F32)
        vbuf[0:CONV_HALO, :] = jnp.zeros((CONV_HALO, dp), F32)

    x = x_ref[...]
    mod = mod_ref[...]
    sh1, sc1, g1, sh2, sc2, _ = [mod[i:i + 1, :] for i in range(6)]
    h = _rmsnorm(x, n1_ref[...]) * (1.0 + sc1) + sh1
    z = _dot(h.astype(BF16), win_ref[...])
    u, gate_b, gate_c, val = [z[:, i * dp:(i + 1) * dp] for i in range(4)]

    ubuf[POOL_HALO:POOL_HALO + tt, :] = u
    pos = lax.broadcasted_iota(I32, (tt, gw), 0) + t * tt
    pooled = []
    for g, w in enumerate(POOL_WINDOWS):
        cols = slice(g * gw, (g + 1) * gw)
        acc = u[:, cols]
        for j in range(1, w):
            acc = acc + ubuf[POOL_HALO - j:POOL_HALO - j + tt, cols]
        cnt = jnp.minimum(pos + 1, w).astype(F32)
        pooled.append(acc / cnt - u[:, cols])
    pool_in = jnp.concatenate(pooled, axis=-1)

    v = gate_c * val
    vbuf[CONV_HALO:CONV_HALO + tt, :] = v
    wc = wconv_ref[...]
    y = (wc[0:1, :] * vbuf[CONV_HALO - 2:CONV_HALO - 2 + tt, :]
         + wc[1:2, :] * vbuf[CONV_HALO - 1:CONV_HALO - 1 + tt, :]
         + wc[2:3, :] * v)
    conv_out = gate_b * y

    x1, h2, logits = _mix_tail(x, pool_in, conv_out, g1, sc2, sh2, n2_ref[...], wpool_ref,
                               pscale_ref[...], wout_ref[...], wr_ref[...], br_ref[...])
    x1_ref[...] = x1
    _store_token_tiles(h2_ref, h2)
    lg_ref[...] = logits

    ubuf[0:POOL_HALO, :] = ubuf[tt:tt + POOL_HALO, :]
    vbuf[0:CONV_HALO, :] = vbuf[tt:tt + CONV_HALO, :]

    @pl.when(t == tiles_per_seq - 1)
    def _():
        upool_ref[...] = ubuf[0:POOL_HALO, :]
        vconv_ref[...] = vbuf[0:CONV_HALO, :]


def _mixer_prompt(x, mod_p, n1, n2, w_in, w_pool, pscale, w_conv, w_out, w_r, b_r, h2_s):
    b, t, d = x.shape
    dp = w_pool.shape[0] * w_pool.shape[1]
    tt = min(TOKEN_TILE, t)
    nt = t // tt
    n_real = b * nt
    assert h2_s.shape[0] == tt * SUBLANES
    seq = lambda i: jnp.minimum(i, n_real - 1) // nt
    tile = lambda i: jnp.minimum(i, n_real - 1)
    const2 = lambda i: (0, 0)
    const3 = lambda i: (0, 0, 0)
    return pl.pallas_call(
        functools.partial(_mixer_prompt_kernel, tiles_per_seq=nt),
        out_shape=(
            jax.ShapeDtypeStruct((b * t, d), F32),
            jax.ShapeDtypeStruct(((n_real + 1) * tt * SUBLANES, LANES), F32),
            jax.ShapeDtypeStruct((N_EXPERTS, b * t), F32),
            jax.ShapeDtypeStruct((b, POOL_HALO, dp), F32),
            jax.ShapeDtypeStruct((b, CONV_HALO, dp), F32),
        ),
        grid=(n_real + 1,),
        in_specs=[
            pl.BlockSpec((None, tt, d), lambda i: (seq(i), tile(i) % nt, 0)),
            pl.BlockSpec((None, 6, d), lambda i: (seq(i), 0, 0)),
            pl.BlockSpec((1, d), const2),
            pl.BlockSpec((1, d), const2),
            pl.BlockSpec(w_in.shape, const2),
            pl.BlockSpec(w_pool.shape, const3),
            pl.BlockSpec((1, dp), const2),
            pl.BlockSpec(w_conv.shape, const2),
            pl.BlockSpec(w_out.shape, const2),
            pl.BlockSpec(w_r.shape, const2),
            pl.BlockSpec((N_EXPERTS, 1), const2),
            pl.BlockSpec(h2_s.shape, const2),
        ],
        out_specs=(
            pl.BlockSpec((tt, d), lambda i: (tile(i), 0)),
            pl.BlockSpec((tt * SUBLANES, LANES), lambda i: (i, 0)),
            pl.BlockSpec((N_EXPERTS, tt), lambda i: (0, tile(i))),
            pl.BlockSpec((None, POOL_HALO, dp), lambda i: (seq(i), 0, 0)),
            pl.BlockSpec((None, CONV_HALO, dp), lambda i: (seq(i), 0, 0)),
        ),
        scratch_shapes=[
            pltpu.VMEM((POOL_HALO + tt, dp), F32),
            pltpu.VMEM((CONV_HALO + tt, dp), F32),
        ],
        compiler_params=pltpu.CompilerParams(
            dimension_semantics=("arbitrary",), vmem_limit_bytes=VMEM_LIMIT),
        name="mixer_prompt",
    )(x, mod_p, n1, n2, w_in, w_pool, pscale, w_conv, w_out, w_r, b_r, h2_s)


def _mixer_sample_kernel(x_ref, mod_ref, pstate_ref, cstate_ref, n1_ref, n2_ref, win_f32, wpool_f32,
                         pscale_ref, wconv_ref, wout_f32, wr_ref, br_ref,
                         x1_ref, h2_ref, lg_ref, newp_ref, newc_ref, win_ref, wpool_ref, wout_ref,
                         *, steps):
    win_ref[...] = win_f32[...].astype(BF16)
    wpool_ref[...] = wpool_f32[...].astype(BF16)
    wout_ref[...] = wout_f32[...].astype(BF16)
    nb = pstate_ref.shape[1]
    d = x_ref.shape[1]
    dp = pstate_ref.shape[2]
    gw = dp // len(POOL_WINDOWS)
    n_hist = pstate_ref.shape[0]
    n_chist = cstate_ref.shape[0]

    x = x_ref[...]
    mod = mod_ref[0:nb, :]
    rep = lambda a: jnp.concatenate([a] * steps, axis=0)
    sh1, sc1, g1, sh2, sc2, _ = [rep(mod[:, i * d:(i + 1) * d]) for i in range(6)]
    h = _rmsnorm(x, n1_ref[...]) * (1.0 + sc1) + sh1
    z = _dot(h.astype(BF16), win_ref[...])
    u, gate_b, gate_c, val = [z[:, i * dp:(i + 1) * dp] for i in range(4)]

    ext = [pstate_ref[i] for i in range(n_hist)] + [u[s * nb:(s + 1) * nb, :] for s in range(steps)]
    pooled_steps = []
    for s in range(steps):
        groups = []
        for g, w in enumerate(POOL_WINDOWS):
            cols = slice(g * gw, (g + 1) * gw)
            acc = ext[n_hist + s][:, cols]
            for j in range(1, w):
                acc = acc + ext[n_hist + s - j][:, cols]
            cnt = float(min(PAST_LEN + s + 1, w))
            groups.append(acc / cnt - ext[n_hist + s][:, cols])
        pooled_steps.append(jnp.concatenate(groups, axis=-1))
    pool_in = jnp.concatenate(pooled_steps, axis=0)

    v = gate_c * val
    vext = [cstate_ref[i] for i in range(n_chist)] + [v[s * nb:(s + 1) * nb, :] for s in range(steps)]
    wc = wconv_ref[...]
    y = jnp.concatenate(
        [wc[0:1, :] * vext[s] + wc[1:2, :] * vext[s + 1] + wc[2:3, :] * vext[s + 2] for s in range(steps)],
        axis=0)
    conv_out = gate_b * y

    x1, h2, logits = _mix_tail(x, pool_in, conv_out, g1, sc2, sh2, n2_ref[...], wpool_ref,
                               pscale_ref[...], wout_ref[...], wr_ref[...], br_ref[...])
    x1_ref[...] = x1
    _store_token_tiles(h2_ref, h2)
    lg_ref[...] = logits
    for i in range(n_hist):
        newp_ref[i] = ext[steps + i]
    for i in range(n_chist):
        newc_ref[i] = vext[steps + i]


def _mixer_sample(x_tm, mod_s, pstate_tm, cstate_tm, n1, n2, w_in, w_pool, pscale, w_conv, w_out, w_r, b_r,
                  steps):
    rows, d = x_tm.shape
    return pl.pallas_call(
        functools.partial(_mixer_sample_kernel, steps=steps),
        out_shape=(
            jax.ShapeDtypeStruct((rows, d), F32),
            jax.ShapeDtypeStruct((rows * SUBLANES, LANES), F32),
            jax.ShapeDtypeStruct((N_EXPERTS, rows), F32),
            jax.ShapeDtypeStruct(pstate_tm.shape, F32),
            jax.ShapeDtypeStruct(cstate_tm.shape, F32),
            jax.ShapeDtypeStruct(w_in.shape, BF16),
            jax.ShapeDtypeStruct(w_pool.shape, BF16),
            jax.ShapeDtypeStruct(w_out.shape, BF16),
        ),
        compiler_params=pltpu.CompilerParams(vmem_limit_bytes=VMEM_LIMIT),
        name="mixer_sample",
    )(x_tm, mod_s, pstate_tm, cstate_tm, n1, n2, w_in, w_pool, pscale, w_conv, w_out, w_r, b_r)


def _route_kernel(lgp_ref, lgs_ref, dest_ref, gate_ref, cnt_ref, counts, start, before, chosen):
    ne = lgp_ref.shape[0]
    tr = before.shape[0]
    reps = tr // LANES
    n_prompt_chunks = lgp_ref.shape[1] // tr
    n_sample_chunks = lgs_ref.shape[1] // tr
    eidx = lax.broadcasted_iota(I32, (ne, tr), 0)

    def selected(onehots):
        return jnp.where(onehots[0] | onehots[1] | onehots[2] | onehots[3], 1.0, 0.0)

    def select_chunk(ref, first_chunk):
        def body(c, carry):
            chunk = first_chunk + c
            work = ref[:, pl.ds(pl.multiple_of(c * tr, tr), tr)]
            top_v, onehots = [], []
            for k in range(TOP_K):
                m = jnp.max(work, axis=0, keepdims=True)
                idx = jnp.min(jnp.where(work == m, eidx, ne), axis=0, keepdims=True)
                chosen[chunk, k:k + 1, :] = idx
                top_v.append(m)
                onehots.append(eidx == idx)
                work = jnp.where(onehots[k], -jnp.inf, work)
            es = [jnp.exp(v - top_v[0]) for v in top_v]
            denom = es[0] + es[1] + es[2] + es[3]
            cols = pl.ds(pl.multiple_of(chunk * tr, tr), tr)
            for k in range(TOP_K):
                gate_ref[k:k + 1, cols] = es[k] / denom
            chunk_counts = jnp.sum(selected(onehots), axis=1, keepdims=True)
            counts[...] = counts[...] + jnp.broadcast_to(chunk_counts, (ne, LANES))
            return carry
        return body

    counts[...] = jnp.zeros_like(counts)
    gate_ref[...] = jnp.zeros_like(gate_ref)
    lax.fori_loop(0, n_prompt_chunks, select_chunk(lgp_ref, 0), 0)
    lax.fori_loop(0, n_sample_chunks, select_chunk(lgs_ref, n_prompt_chunks), 0)

    total = counts[...]
    hi = jnp.floor(total * (1.0 / 256.0))
    lo = total - hi * 256.0
    r = lax.broadcasted_iota(I32, (ne, ne), 0)
    col = lax.broadcasted_iota(I32, (ne, ne), 1)
    lower = jnp.where(col < r, 1.0, 0.0).astype(BF16)
    start[...] = 256.0 * _dot(lower, hi.astype(BF16)) + _dot(lower, lo.astype(BF16))
    cnt_ref[...] = total
    counts[...] = jnp.zeros_like(counts)

    r = lax.broadcasted_iota(I32, (tr, tr), 0)
    col = lax.broadcasted_iota(I32, (tr, tr), 1)
    before[...] = jnp.where(r < col, 1.0, 0.0).astype(BF16)

    def place_chunk(chunk, carry):
        onehots = [eidx == chosen[chunk, k:k + 1, :] for k in range(TOP_K)]
        mask = selected(onehots)
        base = jnp.concatenate([counts[...] + start[...]] * reps, axis=1)
        rank = _dot(mask.astype(BF16), before[...]) + base
        counts[...] = counts[...] + jnp.broadcast_to(jnp.sum(mask, axis=1, keepdims=True), (ne, LANES))
        for k in range(TOP_K):
            d = jnp.sum(jnp.where(onehots[k], rank, 0.0), axis=0, keepdims=True).astype(I32)
            for j in range(reps):
                dest_ref[chunk * reps + j, k:k + 1, :] = d[:, j * LANES:(j + 1) * LANES]
        return carry

    lax.fori_loop(0, n_prompt_chunks + n_sample_chunks, place_chunk, 0)


def _route(lgt_p, lgt_s):
    ne, n_p = lgt_p.shape
    n_s = lgt_s.shape[1]
    tr = ROUTE_TILE
    return pl.pallas_call(
        _route_kernel,
        out_shape=(
            jax.ShapeDtypeStruct(((n_p + n_s) // LANES, TOP_K, LANES), I32),
            jax.ShapeDtypeStruct((SUBLANES, n_p + n_s), F32),
            jax.ShapeDtypeStruct((ne, LANES), F32),
        ),
        scratch_shapes=[pltpu.VMEM((ne, LANES), F32), pltpu.VMEM((ne, LANES), F32),
                        pltpu.VMEM((tr, tr), BF16), pltpu.VMEM(((n_p + n_s) // tr, TOP_K, tr), I32)],
        compiler_params=pltpu.CompilerParams(vmem_limit_bytes=VMEM_LIMIT),
        name="route",
    )(lgt_p, lgt_s)


def _group_metadata(counts, n_rows, tile):
    n_tiles = n_rows // tile
    n_steps = n_tiles + N_EXPERTS - 1
    ends = jnp.cumsum(counts)
    offs = jnp.concatenate([jnp.zeros((1,), I32), ends]).astype(I32)
    first_tile = offs[:-1] // tile
    last_tile = (ends - 1) // tile
    tiles_e = jnp.where(counts > 0, last_tile - first_tile + 1, 0)
    step_end = jnp.cumsum(tiles_e)
    step_start = step_end - tiles_e
    n_active = step_end[-1]
    s = jnp.minimum(jnp.arange(n_steps, dtype=I32), n_active - 1)
    owner = ((s[:, None] >= step_start[None, :]) & (s[:, None] < step_end[None, :])).astype(I32)
    gid = jnp.sum(owner * jnp.arange(N_EXPERTS, dtype=I32)[None, :], axis=1)
    tid = jnp.sum(owner * (first_tile - step_start)[None, :], axis=1) + s
    ids = jnp.arange(N_EXPERTS, dtype=I32)
    later = (ids[None, :] > ids[:, None]) & (counts[None, :] > 0)
    next_e = jnp.min(jnp.where(later, ids[None, :], N_EXPERTS), axis=1)
    next_e = jnp.where(next_e == N_EXPERTS, -1, next_e)
    nxt = jnp.sum(owner * next_e[None, :], axis=1)
    return gid, tid, nxt, offs, n_active.reshape(1).astype(I32), n_steps


SC_LANES = 16
SC_INDEX_BATCH = 128


def _invert(dest):
    r = dest.shape[0]
    mesh = plsc.VectorSubcoreMesh(core_axis_name="core", subcore_axis_name="subcore")
    per = r // mesh.num_subcores
    assert per * mesh.num_subcores == r and per % SC_INDEX_BATCH == 0

    @functools.partial(
        pl.kernel, mesh=mesh, out_type=jax.ShapeDtypeStruct((r,), I32),
        scratch_types=[pltpu.VMEM_SHARED((r,), I32), pltpu.VMEM((per,), I32), pltpu.VMEM((per,), I32)],
        compiler_params=pltpu.CompilerParams(needs_layout_passes=False),
        name="invert",
    )
    def invert(dest_hbm, inv_hbm, table, idx, ids):
        @pl.when(lax.axis_index("core") == 0)
        def _():
            base = lax.axis_index("subcore") * per
            pltpu.sync_copy(dest_hbm.at[pl.ds(base, per)], idx)
            lane = lax.iota(I32, SC_LANES)

            @pl.loop(0, per // SC_LANES)
            def _(i):
                p = base + i * SC_LANES + lane
                tok = lax.shift_right_logical(p, 9) * LANES + (p & (LANES - 1))
                slot = lax.shift_right_logical(p, 7) & (TOP_K - 1)
                ids[pl.ds(i * SC_LANES, SC_LANES)] = tok * TOP_K + slot

            @pl.loop(0, per // SC_INDEX_BATCH)
            def _(j):
                span = pl.ds(j * SC_INDEX_BATCH, SC_INDEX_BATCH)
                pltpu.sync_copy(ids.at[span], table.at[idx.at[span]])

            plsc.subcore_barrier()
            pltpu.sync_copy(table.at[pl.ds(base, per)], inv_hbm.at[pl.ds(base, per)])

    return invert(dest)


def _experts_kernel(gid_ref, tid_ref, nxt_ref, offs_ref, nact_ref,
                    dstp_ref, dst0_ref, src0_ref, src1_ref, src2_ref, bg_ref, bu_ref, bd_ref,
                    h2_hbm, wg_hbm, wu_hbm, wd_hbm, y4_hbm,
                    wg_b, wu_b, wd_b, wg_f, wu_f, wd_f, relay, rows, ybuf, sems, row_sems, y_sems):
    s = pl.program_id(0)
    tile_rows = rows.shape[1]
    tm = tile_rows // SUBLANES
    landing = ((wg_hbm, wg_f, wg_b), (wu_hbm, wu_f, wu_b), (wd_hbm, wd_f, wd_b))

    def fetch(e):
        for j, (hbm, land, _) in enumerate(landing):
            pltpu.make_async_copy(hbm.at[e], land, sems.at[j]).start()

    def row_in(idx_ref, into, r):
        src = h2_hbm.at[pl.ds(pl.multiple_of(idx_ref[r], SUBLANES), SUBLANES)]
        dst = rows.at[into, pl.ds(pl.multiple_of(r * SUBLANES, SUBLANES), SUBLANES)]
        return pltpu.make_async_copy(src, dst, row_sems.at[into])

    def row_out(idx_ref, frm, r):
        src = ybuf.at[frm, pl.ds(pl.multiple_of(r * SUBLANES, SUBLANES), SUBLANES)]
        dst = y4_hbm.at[pl.ds(pl.multiple_of(idx_ref[r], SUBLANES), SUBLANES)]
        return pltpu.make_async_copy(src, dst, y_sems.at[frm])

    def in_line(make, idx_ref, buf, first=0, count=None):
        for r in range(first, tm if count is None else first + count):
            make(idx_ref, buf, r).start(priority=r % 2)

    def in_loop(make, idx_ref, buf):
        def group(g, carry):
            for j in range(ISSUE_GROUP):
                make(idx_ref, buf, g * ISSUE_GROUP + j).start(priority=j % 2)
            return carry
        lax.fori_loop(0, tm // ISSUE_GROUP, group, 0)

    def wait_rows(buf):
        pltpu.make_async_copy(h2_hbm.at[pl.ds(0, tile_rows)], rows.at[buf], row_sems.at[buf]).wait()

    def wait_y(buf):
        pltpu.make_async_copy(ybuf.at[buf], y4_hbm.at[pl.ds(0, tile_rows)], y_sems.at[buf]).wait()

    @pl.when(s < nact_ref[0])
    def _():
        e = gid_ref[s]
        m = tid_ref[s]
        cur = lax.rem(m, 3)
        before = lax.rem(m + 2, 3)
        last = nact_ref[0] - 1
        new_tile = (s == 0) | (tid_ref[jnp.maximum(s - 1, 0)] != m)

        @pl.when(s == 0)
        def _():
            fetch(e)
            in_loop(row_in, src0_ref, 0)
            in_loop(row_in, src1_ref, 1)
            ybuf[2] = jnp.zeros(ybuf.shape[1:], F32)

        @pl.when((s == 0) | (gid_ref[jnp.maximum(s - 1, 0)] != e))
        def _():
            for j, (hbm, land, half) in enumerate(landing):
                pltpu.make_async_copy(hbm.at[e], land, sems.at[j]).wait()
                half[...] = land[...].astype(BF16)

            @pl.when(nxt_ref[s] >= 0)
            def _():
                fetch(nxt_ref[s])

        @pl.when(new_tile)
        def _():
            wait_rows(cur)

            @pl.when(m >= 2)
            def _():
                wait_y(cur)

        def ffn(load_x, after_gate=lambda: None):
            g = _dot(load_x(), wg_b[...]) + bg_ref[pl.ds(e, 1), :]
            after_gate()
            u = _dot(load_x(), wu_b[...]) + bu_ref[pl.ds(e, 1), :]
            g = jnp.minimum(g, SWIGLU_LIMIT)
            u = jnp.clip(u, -SWIGLU_LIMIT, SWIGLU_LIMIT)
            glu = g * jax.nn.sigmoid(SWIGLU_ALPHA * g)
            return _dot(((u + 1.0) * glu).astype(BF16), wd_b[...]) + bd_ref[pl.ds(e, 1), :]

        lo = offs_ref[e]
        hi = offs_ref[e + 1]
        whole_tile = (lo <= m * tm) & (hi >= (m + 1) * tm)

        def ffn_and_copies(load_x, first, count):
            pinned = count // 2
            y = ffn(load_x, lambda: in_line(row_in, src2_ref, before, first, pinned))
            in_line(row_in, src2_ref, before, first + pinned, count - pinned)
            in_line(row_out, dstp_ref, before, first, count)
            return y

        @pl.when(whole_tile)
        def _():
            y = ffn_and_copies(lambda: _load_token_tiles(rows.at[cur]).astype(BF16), 0, tm)
            _store_token_tiles(ybuf.at[cur], y)

        @pl.when(jnp.logical_not(whole_tile))
        def _():
            sub = relay.shape[0] // SUBLANES
            for j in range(tm // sub):
                first = m * tm + j * sub
                span = pl.ds(j * sub * SUBLANES, sub * SUBLANES)
                load_x = lambda span=span: _load_token_tiles(rows.at[cur, span]).astype(BF16)

                def put(y, first=first, span=span):
                    _store_token_tiles(relay, y)
                    row = first + lax.shift_right_logical(lax.broadcasted_iota(I32, relay.shape, 0), 3)
                    pltpu.store(ybuf.at[cur, span], relay[...], mask=(row >= lo) & (row < hi))

                evaluate = (lo < first + sub) & (hi > first)
                owns_end = hi >= first + sub

                @pl.when(evaluate & owns_end)
                def _():
                    put(ffn_and_copies(load_x, j * sub, sub))

                @pl.when(evaluate & jnp.logical_not(owns_end))
                def _():
                    put(ffn(load_x))

        @pl.when(s == last)
        def _():
            in_loop(row_out, dst0_ref, cur)
            wait_rows(lax.rem(m + 1, 3))
            wait_rows(before)
            for buf in range(3):
                wait_y(buf)


def _experts(gid, tid, nxt, offs, nact, n_steps, inv, h2, wg, bg, wu, bu, wd, bd):
    ne, d, f = wg.shape
    tm = EXPERT_TILE
    n_tokens = h2.shape[0] // SUBLANES
    n_tiles = inv.shape[0] // tm
    assert n_tiles >= 2
    slot_rows = (n_tokens + tm // TOP_K) * SUBLANES
    inv_ext = jnp.concatenate([n_tokens * TOP_K + jnp.arange(tm, dtype=I32), inv])
    tok = lax.shift_right_logical(inv_ext, TOP_K.bit_length() - 1)
    src_row = tok * SUBLANES
    dst_row = (inv_ext & (TOP_K - 1)) * slot_rows + src_row
    whole = lambda s, gid, tid, nxt, offs, nact: (0, 0)

    def order_of(k):
        return pl.BlockSpec(
            (tm,), lambda s, gid, tid, nxt, offs, nact: (jnp.minimum(tid[s] + k, n_tiles - 1) + 1,),
            memory_space=pltpu.SMEM)

    grid_spec = pltpu.PrefetchScalarGridSpec(
        num_scalar_prefetch=5,
        grid=(n_steps,),
        in_specs=[
            order_of(-1), order_of(0), order_of(0), order_of(1), order_of(2),
            pl.BlockSpec((ne, f), whole),
            pl.BlockSpec((ne, f), whole),
            pl.BlockSpec((ne, d), whole),
            pl.BlockSpec(memory_space=pl.ANY),
            pl.BlockSpec(memory_space=pl.ANY),
            pl.BlockSpec(memory_space=pl.ANY),
            pl.BlockSpec(memory_space=pl.ANY),
        ],
        out_specs=pl.BlockSpec(memory_space=pl.ANY),
        scratch_shapes=[
            pltpu.VMEM((d, f), BF16), pltpu.VMEM((d, f), BF16), pltpu.VMEM((f, d), BF16),
            pltpu.VMEM((d, f), F32), pltpu.VMEM((d, f), F32), pltpu.VMEM((f, d), F32),
            pltpu.VMEM((EXPERT_SUBTILE * SUBLANES, LANES), F32),
            pltpu.VMEM((3, tm * SUBLANES, LANES), F32),
            pltpu.VMEM((3, tm * SUBLANES, LANES), F32),
            pltpu.SemaphoreType.DMA((3,)),
            pltpu.SemaphoreType.DMA((3,)),
            pltpu.SemaphoreType.DMA((3,)),
        ],
    )
    y4 = pl.pallas_call(
        _experts_kernel,
        out_shape=jax.ShapeDtypeStruct((TOP_K * slot_rows, LANES), F32),
        grid_spec=grid_spec,
        compiler_params=pltpu.CompilerParams(
            dimension_semantics=("arbitrary",), vmem_limit_bytes=VMEM_LIMIT),
        name="experts",
    )(gid, tid, nxt, offs, nact, dst_row, dst_row, src_row, src_row, src_row, bg, bu, bd, h2, wg, wu, wd)
    return y4.reshape(TOP_K, slot_rows, LANES)


def _combine_kernel(gate_ref, y4_hbm, x1p_ref, x1s_ref, modp_ref, g2s_ref, fn_ref, outp_ref, outs_ref,
                    ring, ring_sems, *, n_prompt_tiles):
    i = pl.program_id(0)
    n = pl.num_programs(0)
    tm = x1p_ref.shape[0]
    depth = ring.shape[0]

    def fetch(t):
        slot = lax.rem(t, depth)
        src = y4_hbm.at[:, pl.ds(pl.multiple_of(t * (tm * SUBLANES), tm * SUBLANES), tm * SUBLANES), :]
        return pltpu.make_async_copy(src, ring.at[slot], ring_sems.at[slot])

    @pl.when(i == 0)
    def _():
        for t in range(depth - 1):
            @pl.when(t < n)
            def _():
                fetch(t).start()

    @pl.when(i + depth - 1 < n)
    def _():
        fetch(i + depth - 1).start()

    fetch(i).wait()
    slot = lax.rem(i, depth)
    gates = gate_ref[...].T
    cols = []
    for c in range(SUBLANES):
        acc = gates[:, 0:1] * ring[slot, 0, pl.ds(c, tm, stride=SUBLANES), :]
        for k in range(1, TOP_K):
            acc = acc + gates[:, k:k + 1] * ring[slot, k, pl.ds(c, tm, stride=SUBLANES), :]
        cols.append(acc)
    ffn = jnp.concatenate(cols, axis=-1)

    @pl.when(i < n_prompt_tiles)
    def _():
        g2 = modp_ref[...][5:6, :]
        outp_ref[...] = _rmsnorm(x1p_ref[...] + g2 * ffn, fn_ref[...])

    @pl.when(i >= n_prompt_tiles)
    def _():
        g2 = jnp.concatenate([g2s_ref[...]] * (tm // g2s_ref.shape[0]), axis=0)
        outs_ref[...] = _rmsnorm(x1s_ref[...] + g2 * ffn, fn_ref[...])


def _combine(gates, y4, x1_p, x1_s, mod_p, mod, n_batch_s, final_norm, tokens_per_seq):
    n_p, d = x1_p.shape
    n_s = x1_s.shape[0]
    tm = COMBINE_TILE
    assert tm % n_batch_s == 0 and n_batch_s % SUBLANES == 0
    npt, nst = n_p // tm, n_s // tm
    tiles_per_seq = tokens_per_seq // tm
    pmap = lambda i: (jnp.minimum(i, npt - 1), 0)
    smap = lambda i: (jnp.maximum(i - npt, 0), 0)
    return pl.pallas_call(
        functools.partial(_combine_kernel, n_prompt_tiles=npt),
        out_shape=(jax.ShapeDtypeStruct((n_p, d), F32), jax.ShapeDtypeStruct((n_s, d), F32)),
        grid=(npt + nst,),
        in_specs=[
            pl.BlockSpec((SUBLANES, tm), lambda i: (0, i)),
            pl.BlockSpec(memory_space=pl.ANY),
            pl.BlockSpec((tm, d), pmap),
            pl.BlockSpec((tm, d), smap),
            pl.BlockSpec((None, 6, d), lambda i: (jnp.minimum(i, npt - 1) // tiles_per_seq, 0, 0)),
            pl.BlockSpec((n_batch_s, d), lambda i: (0, 5)),
            pl.BlockSpec((1, d), lambda i: (0, 0)),
        ],
        out_specs=(pl.BlockSpec((tm, d), pmap), pl.BlockSpec((tm, d), smap)),
        scratch_shapes=[pltpu.VMEM((3, TOP_K, tm * SUBLANES, LANES), F32), pltpu.SemaphoreType.DMA((3,))],
        compiler_params=pltpu.CompilerParams(
            dimension_semantics=("arbitrary",), vmem_limit_bytes=VMEM_LIMIT),
        name="combine",
    )(gates, y4, x1_p, x1_s, mod_p, mod, final_norm)


def kernel(x_prompt, x_sample, state_pool, state_conv, c_prompt, c_sample, norm1, norm2, w_ada, b_ada,
           w_in, w_pool, pool_scale, w_conv, w_out, w_router, b_router, w_gate, b_gate, w_up, b_up,
           w_down, b_down, final_norm):
    depth = norm1.shape[0]
    assert depth == 1, "single-layer step"
    bp, tp, d = x_prompt.shape
    bs, ts, _ = x_sample.shape
    dp = state_pool.shape[-1]
    n_hist = state_pool.shape[2]
    n_chist = state_conv.shape[2]
    n_p, n_s = bp * tp, bs * ts
    assert d == SUBLANES * LANES, "token-tile layout assumes one vreg tile per token row"
    assert tp % TOKEN_TILE == 0 and n_s == TOKEN_TILE, "the sample group fills exactly one token tile"
    assert tp % COMBINE_TILE == 0 and n_s % COMBINE_TILE == 0
    assert (n_p + n_s) % ROUTE_TILE == 0 and ((n_p + n_s) * TOP_K) % EXPERT_TILE == 0

    l = 0
    n1 = norm1[l].reshape(1, d)
    n2 = norm2[l].reshape(1, d)
    pscale = pool_scale[l].reshape(1, dp)
    w_r = w_router[l].T.astype(BF16)
    b_r = b_router[l].reshape(N_EXPERTS, 1)

    mod = _adaln(jnp.concatenate([c_sample, c_prompt], axis=0), w_ada[l], b_ada[l])
    mod_p = mod[bs:].reshape(bp, 6, d)

    xs_tm = jnp.transpose(x_sample, (1, 0, 2)).reshape(n_s, d)
    ps_tm = jnp.transpose(state_pool[l], (1, 0, 2))
    cs_tm = jnp.transpose(state_conv[l], (1, 0, 2))
    x1_s, h2_s, lg_s, newp_tm, newc_tm, w_in_b, w_pool_b, w_out_b = _mixer_sample(
        xs_tm, mod, ps_tm, cs_tm, n1, n2, w_in[l], w_pool[l], pscale, w_conv[l], w_out[l], w_r, b_r, ts)

    x1_p, h2, lg_p, u_tail, v_tail = _mixer_prompt(
        x_prompt, mod_p, n1, n2, w_in_b, w_pool_b, pscale, w_conv[l], w_out_b, w_r, b_r, h2_s)

    dest, gates, counts_f = _route(lg_p, lg_s)
    counts = counts_f[:, 0].astype(I32)
    n_rows = (n_p + n_s) * TOP_K
    gid, tid, nxt, offs, nact, n_steps = _group_metadata(counts, n_rows, EXPERT_TILE)

    dest = dest.reshape(-1)
    inv = _invert(dest)
    y4 = _experts(gid, tid, nxt, offs, nact, n_steps, inv, h2,
                  w_gate[l], b_gate[l], w_up[l], b_up[l], w_down[l], b_down[l])

    y_p, y_s = _combine(gates, y4, x1_p, x1_s, mod_p, mod, bs, final_norm.reshape(1, d), tp)

    y_prompt = y_p.reshape(bp, tp, d)
    y_sample = jnp.transpose(y_s.reshape(ts, bs, d), (1, 0, 2))
    new_pool_prompt = u_tail[:, POOL_HALO - n_hist:, :][None]
    new_conv_prompt = v_tail[:, CONV_HALO - n_chist:, :][None]
    new_pool_sample = jnp.transpose(newp_tm, (1, 0, 2))[None]
    new_conv_sample = jnp.transpose(newc_tm, (1, 0, 2))[None]
    return (y_prompt, y_sample, new_pool_prompt, new_conv_prompt, new_pool_sample, new_conv_sample)
```
